```python
import math
import jax, jax.numpy as jnp
from jax import lax
import numpy as np

D_MODEL = 2048
BATCH = 8
SEQ = 4096
DEPTH = 2

NORM_EPS = 1e-6
CONV_WIDTH = 4

D_FF = ((8 * D_MODEL + 3 * 256 - 1) // (3 * 256)) * 256

GLA_DV = D_MODEL // 2
GLA_HEADS = 4
GLA_HEAD_V = GLA_DV // GLA_HEADS
GLA_DK = GLA_DV // 2
GLA_HEAD_K = GLA_DK // GLA_HEADS
GLA_GATE_RANK = 16
GLA_GATE_NORM = 16.0
GLA_CHUNK = 64

LRU_WIDTH = D_MODEL // 2
LRU_BLOCKS = 8
LRU_BLOCK = LRU_WIDTH // LRU_BLOCKS
LRU_C = 8.0

EVEN_IN = 2 * GLA_DK + 2 * GLA_DV + GLA_GATE_RANK + 2 * LRU_WIDTH
EVEN_MIX = GLA_DV + LRU_WIDTH

SSD_D_INNER = 2 * D_MODEL
SSD_HEAD_DIM = 64
SSD_HEADS = SSD_D_INNER // SSD_HEAD_DIM
SSD_GROUPS = 8
SSD_STATE = 128
SSD_CHUNK = 64
SSD_CONV_DIM = SSD_D_INNER + 2 * SSD_GROUPS * SSD_STATE
ODD_IN = SSD_D_INNER + SSD_CONV_DIM + SSD_HEADS

kernel_name = "hybrid_gla_rglru_mamba2_swiglu"


def rmsnorm(x, w):
    xf = x.astype(jnp.float32)
    y = xf * lax.rsqrt(jnp.mean(xf * xf, axis=-1, keepdims=True) + NORM_EPS)
    return (y * w.astype(jnp.float32)).astype(x.dtype)


def causal_depthwise_conv(x, w, b):
    K, C = w.shape
    y = lax.conv_general_dilated(
        x, w[:, None, :].astype(x.dtype), window_strides=(1,),
        padding=[(K - 1, 0)], dimension_numbers=("NWC", "WIO", "NWC"),
        feature_group_count=C)
    return y + b.astype(x.dtype)


def _scan_chunk_states(dS, decay):
    def step(s_prev, inp):
        ds_c, dec_c = inp
        return dec_c * s_prev + ds_c, s_prev
    s0 = jnp.zeros(dS.shape[:1] + dS.shape[2:], dS.dtype)
    _, prev = lax.scan(step, s0, (jnp.moveaxis(dS, 1, 0), jnp.moveaxis(decay, 1, 0)))
    return jnp.moveaxis(prev, 0, 1)


def gla_mixer(q, k, v, g, g_lr, w_gate_up, b_gate, w_onorm):
    B, S, _ = q.shape
    H, dk, dv, L = GLA_HEADS, GLA_HEAD_K, GLA_HEAD_V, GLA_CHUNK
    nc = S // L
    f32 = jnp.float32
    q = q.astype(f32).reshape(B, nc, L, H, dk) * (dk ** -0.5)
    k = k.astype(f32).reshape(B, nc, L, H, dk)
    v = v.astype(f32).reshape(B, nc, L, H, dv)
    log_a = jax.nn.log_sigmoid(g_lr.astype(f32) @ w_gate_up.astype(f32)
                               + b_gate.astype(f32)) / GLA_GATE_NORM
    bcum = jnp.cumsum(log_a.reshape(B, nc, L, H, dk), axis=2)
    b_last = bcum[:, :, -1]
    b_mid = bcum[:, :, L // 2:L // 2 + 1]
    q_in = q * jnp.exp(bcum - b_mid)
    k_in = k * jnp.exp(b_mid - bcum)
    causal = jnp.tril(jnp.ones((L, L), bool))
    scores = jnp.where(causal, jnp.einsum("bclhd,bcshd->bchls", q_in, k_in), 0.0)
    o_intra = jnp.einsum("bchls,bcshv->bclhv", scores, v)
    k_st = k * jnp.exp(b_last[:, :, None] - bcum)
    dS = jnp.einsum("bclhd,bclhv->bchdv", k_st, v)
    s_prev = _scan_chunk_states(dS, jnp.exp(b_last)[..., None])
    o_inter = jnp.einsum("bclhd,bchdv->bclhv", q * jnp.exp(bcum), s_prev)
    o = (o_intra + o_inter).reshape(B, S, H, dv)
    o = rmsnorm(o, w_onorm) * jax.nn.silu(g.astype(f32).reshape(B, S, H, dv))
    return o.reshape(B, S, H * dv)


def rglru_mixer(x_br, gate_br, conv_w, conv_b, w_a, b_a, w_i, b_i, lam):
    B, S, W = x_br.shape
    f32 = jnp.float32
    xc = causal_depthwise_conv(x_br, conv_w, conv_b).astype(f32)
    xblk = xc.reshape(B, S, LRU_BLOCKS, LRU_BLOCK)
    r = jax.nn.sigmoid(jnp.einsum("bsnc,ncd->bsnd", xblk, w_a.astype(f32)).reshape(B, S, W)
                       + b_a.astype(f32))
    i = jax.nn.sigmoid(jnp.einsum("bsnc,ncd->bsnd", xblk, w_i.astype(f32)).reshape(B, S, W)
                       + b_i.astype(f32))
    log_a = LRU_C * r * jax.nn.log_sigmoid(lam.astype(f32))
    a = jnp.exp(log_a)
    u = jnp.sqrt(-jnp.expm1(2.0 * log_a)) * (i * xc)

    def combine(c1, c2):
        a1, b1 = c1
        a2, b2 = c2
        return a1 * a2, a2 * b1 + b2

    _, h = lax.associative_scan(combine, (a, u), axis=1)
    return h * jax.nn.gelu(gate_br.astype(f32), approximate=True)


def even_mixer(h, w_in, gla_w_gate, gla_b_gate, gla_w_onorm, lru_conv_w, lru_conv_b,
               lru_w_a, lru_b_a, lru_w_i, lru_b_i, lru_lam, w_out):
    proj = h @ w_in
    cuts = [GLA_DK, 2 * GLA_DK, 2 * GLA_DK + GLA_DV, 2 * GLA_DK + 2 * GLA_DV,
            2 * GLA_DK + 2 * GLA_DV + GLA_GATE_RANK,
            2 * GLA_DK + 2 * GLA_DV + GLA_GATE_RANK + LRU_WIDTH]
    q, k, v, g, g_lr, x_br, gate_br = jnp.split(proj, cuts, axis=-1)
    o_gla = gla_mixer(q, k, v, g, g_lr, gla_w_gate, gla_b_gate, gla_w_onorm)
    o_lru = rglru_mixer(x_br, gate_br, lru_conv_w, lru_conv_b, lru_w_a, lru_b_a,
                        lru_w_i, lru_b_i, lru_lam)
    mix = jnp.concatenate([o_gla, o_lru], axis=-1).astype(h.dtype)
    return mix @ w_out


def ssd_scan(x, dt, A, Bm, Cm):
    Bsz, S, H, P = x.shape
    G, N, L = SSD_GROUPS, SSD_STATE, SSD_CHUNK
    Hg, nc = H // G, S // L
    x = x.reshape(Bsz, nc, L, G, Hg, P)
    dt = dt.reshape(Bsz, nc, L, G, Hg)
    Bm = Bm.reshape(Bsz, nc, L, G, N)
    Cm = Cm.reshape(Bsz, nc, L, G, N)
    acs = jnp.cumsum(dt * A.reshape(G, Hg), axis=2)
    seg = acs[:, :, :, None] - acs[:, :, None, :]
    causal = jnp.tril(jnp.ones((L, L), bool))[:, :, None, None]
    decay = jnp.where(causal, jnp.exp(jnp.minimum(seg, 0.0)), 0.0)
    cb = jnp.einsum("bclgn,bcsgn->bclsg", Cm, Bm)
    wts = cb[..., None] * decay * dt[:, :, None]
    y_diag = jnp.einsum("bclsgh,bcsghp->bclghp", wts, x)
    xw = x * (jnp.exp(acs[:, :, -1:] - acs) * dt)[..., None]
    states = jnp.einsum("bclgn,bclghp->bcghpn", Bm, xw)
    s_prev = _scan_chunk_states(states, jnp.exp(acs[:, :, -1])[..., None, None])
    y_off = jnp.einsum("bclgn,bcghpn->bclghp", Cm, s_prev) * jnp.exp(acs)[..., None]
    return (y_diag + y_off).reshape(Bsz, S, H, P)


def odd_mixer(h, w_in, conv_w, conv_b, dt_bias, a_log, d_skip, gnorm, w_out):
    Bsz, S, _ = h.shape
    f32 = jnp.float32
    proj = h @ w_in
    z, xbc, dt = jnp.split(proj, [SSD_D_INNER, SSD_D_INNER + SSD_CONV_DIM], axis=-1)
    xbc = jax.nn.silu(causal_depthwise_conv(xbc, conv_w, conv_b).astype(f32))
    xs, Bm, Cm = jnp.split(xbc, [SSD_D_INNER, SSD_D_INNER + SSD_GROUPS * SSD_STATE], axis=-1)
    dt = jax.nn.softplus(dt.astype(f32) + dt_bias.astype(f32))
    A = -jnp.exp(a_log.astype(f32))
    xh = xs.reshape(Bsz, S, SSD_HEADS, SSD_HEAD_DIM)
    y = ssd_scan(xh, dt, A,
                 Bm.reshape(Bsz, S, SSD_GROUPS, SSD_STATE),
                 Cm.reshape(Bsz, S, SSD_GROUPS, SSD_STATE))
    y = y + d_skip.astype(f32)[:, None] * xh
    y = y.reshape(Bsz, S, SSD_D_INNER) * jax.nn.silu(z.astype(f32))
    y = rmsnorm(y.reshape(Bsz, S, SSD_GROUPS, SSD_D_INNER // SSD_GROUPS),
                gnorm.reshape(SSD_GROUPS, SSD_D_INNER // SSD_GROUPS))
    return y.reshape(Bsz, S, SSD_D_INNER).astype(h.dtype) @ w_out


def swiglu(h, w_gate, w_up, w_down):
    return (jax.nn.silu(h @ w_gate) * (h @ w_up)) @ w_down


def _fwd_setup_inputs(seed: int = 0) -> dict:
    key = jax.random.key(seed)
    ks = iter(jax.random.split(key, 48))
    ne = (DEPTH + 1) // 2
    no = DEPTH // 2
    f32 = jnp.float32

    def nrm(shape, scale):
        return jax.random.normal(next(ks), shape, f32) * scale

    def gain(shape):
        return 1.0 + nrm(shape, 0.02)

    x = nrm((BATCH, SEQ, D_MODEL), 1.0)
    ev_norm = gain((ne, D_MODEL))
    ev_w_in = nrm((ne, D_MODEL, EVEN_IN), D_MODEL ** -0.5)
    ev_gla_w_gate = nrm((ne, GLA_GATE_RANK, GLA_DK), GLA_GATE_RANK ** -0.5)
    ev_gla_b_gate = nrm((ne, GLA_DK), 0.02)
    ev_gla_w_onorm = gain((ne, GLA_HEAD_V))
    ev_lru_conv_w = nrm((ne, CONV_WIDTH, LRU_WIDTH), CONV_WIDTH ** -0.5)
    ev_lru_conv_b = nrm((ne, LRU_WIDTH), 0.02)
    ev_lru_w_a = nrm((ne, LRU_BLOCKS, LRU_BLOCK, LRU_BLOCK), LRU_BLOCK ** -0.5)
    ev_lru_b_a = nrm((ne, LRU_WIDTH), 0.02)
    ev_lru_w_i = nrm((ne, LRU_BLOCKS, LRU_BLOCK, LRU_BLOCK), LRU_BLOCK ** -0.5)
    ev_lru_b_i = nrm((ne, LRU_WIDTH), 0.02)
    u = jax.random.uniform(next(ks), (ne, LRU_WIDTH), f32, minval=0.9, maxval=0.999)
    a0 = u ** (1.0 / LRU_C)
    ev_lru_lam = jnp.log(a0) - jnp.log1p(-a0)
    ev_w_out = nrm((ne, EVEN_MIX, D_MODEL), EVEN_MIX ** -0.5)
    od_norm = gain((no, D_MODEL))
    od_w_in = nrm((no, D_MODEL, ODD_IN), D_MODEL ** -0.5)
    od_conv_w = nrm((no, CONV_WIDTH, SSD_CONV_DIM), CONV_WIDTH ** -0.5)
    od_conv_b = nrm((no, SSD_CONV_DIM), 0.02)
    dt0 = jnp.exp(jax.random.uniform(next(ks), (no, SSD_HEADS), f32,
                                     minval=math.log(1e-3), maxval=math.log(0.1)))
    od_dt_bias = dt0 + jnp.log(-jnp.expm1(-dt0))
    od_a_log = jnp.log(jax.random.uniform(next(ks), (no, SSD_HEADS), f32, minval=1.0, maxval=16.0))
    od_d_skip = gain((no, SSD_HEADS))
    od_gnorm = gain((no, SSD_D_INNER))
    od_w_out = nrm((no, SSD_D_INNER, D_MODEL), SSD_D_INNER ** -0.5)
    ffn_norm = gain((DEPTH, D_MODEL))
    ffn_w_gate = nrm((DEPTH, D_MODEL, D_FF), D_MODEL ** -0.5)
    ffn_w_up = nrm((DEPTH, D_MODEL, D_FF), D_MODEL ** -0.5)
    ffn_w_down = nrm((DEPTH, D_FF, D_MODEL), D_FF ** -0.5)
    final_norm = gain((D_MODEL,))
    return {
        "x": x,
        "ev_norm": ev_norm, "ev_w_in": ev_w_in,
        "ev_gla_w_gate": ev_gla_w_gate, "ev_gla_b_gate": ev_gla_b_gate,
        "ev_gla_w_onorm": ev_gla_w_onorm,
        "ev_lru_conv_w": ev_lru_conv_w, "ev_lru_conv_b": ev_lru_conv_b,
        "ev_lru_w_a": ev_lru_w_a, "ev_lru_b_a": ev_lru_b_a,
        "ev_lru_w_i": ev_lru_w_i, "ev_lru_b_i": ev_lru_b_i,
        "ev_lru_lam": ev_lru_lam, "ev_w_out": ev_w_out,
        "od_norm": od_norm, "od_w_in": od_w_in,
        "od_conv_w": od_conv_w, "od_conv_b": od_conv_b,
        "od_dt_bias": od_dt_bias, "od_a_log": od_a_log, "od_d_skip": od_d_skip,
        "od_gnorm": od_gnorm, "od_w_out": od_w_out,
        "ffn_norm": ffn_norm, "ffn_w_gate": ffn_w_gate, "ffn_w_up": ffn_w_up,
        "ffn_w_down": ffn_w_down, "final_norm": final_norm,
    }


def _fwd_reference(x, ev_norm, ev_w_in, ev_gla_w_gate, ev_gla_b_gate, ev_gla_w_onorm,
              ev_lru_conv_w, ev_lru_conv_b, ev_lru_w_a, ev_lru_b_a, ev_lru_w_i,
              ev_lru_b_i, ev_lru_lam, ev_w_out,
              od_norm, od_w_in, od_conv_w, od_conv_b, od_dt_bias, od_a_log,
              od_d_skip, od_gnorm, od_w_out,
              ffn_norm, ffn_w_gate, ffn_w_up, ffn_w_down, final_norm):
    for layer in range(DEPTH):
        j = layer // 2
        if layer % 2 == 0:
            h = rmsnorm(x, ev_norm[j])
            x = x + even_mixer(h, ev_w_in[j], ev_gla_w_gate[j], ev_gla_b_gate[j],
                               ev_gla_w_onorm[j], ev_lru_conv_w[j], ev_lru_conv_b[j],
                               ev_lru_w_a[j], ev_lru_b_a[j], ev_lru_w_i[j], ev_lru_b_i[j],
                               ev_lru_lam[j], ev_w_out[j]).astype(x.dtype)
        else:
            h = rmsnorm(x, od_norm[j])
            x = x + odd_mixer(h, od_w_in[j], od_conv_w[j], od_conv_b[j], od_dt_bias[j],
                              od_a_log[j], od_d_skip[j], od_gnorm[j], od_w_out[j]).astype(x.dtype)
        h = rmsnorm(x, ffn_norm[layer])
        x = x + swiglu(h, ffn_w_gate[layer], ffn_w_up[layer], ffn_w_down[layer]).astype(x.dtype)
    return rmsnorm(x, final_norm)


import jax as _jax
import jax.numpy as _jnp

TWIN_FORMAT = 'train_step'
FWD_PARAMS = ['x', 'ev_norm', 'ev_w_in', 'ev_gla_w_gate', 'ev_gla_b_gate', 'ev_gla_w_onorm', 'ev_lru_conv_w', 'ev_lru_conv_b', 'ev_lru_w_a', 'ev_lru_b_a', 'ev_lru_w_i', 'ev_lru_b_i', 'ev_lru_lam', 'ev_w_out', 'od_norm', 'od_w_in', 'od_conv_w', 'od_conv_b', 'od_dt_bias', 'od_a_log', 'od_d_skip', 'od_gnorm', 'od_w_out', 'ffn_norm', 'ffn_w_gate', 'ffn_w_up', 'ffn_w_down', 'final_norm']
TWIN_WEIGHTS = ['ev_norm', 'ev_w_in', 'ev_gla_w_gate', 'ev_gla_b_gate', 'ev_gla_w_onorm', 'ev_lru_conv_w', 'ev_lru_conv_b', 'ev_lru_w_a', 'ev_lru_b_a', 'ev_lru_w_i', 'ev_lru_b_i', 'ev_lru_lam', 'ev_w_out', 'od_norm', 'od_w_in', 'od_conv_w', 'od_conv_b', 'od_dt_bias', 'od_a_log', 'od_d_skip', 'od_gnorm', 'od_w_out', 'ffn_norm', 'ffn_w_gate', 'ffn_w_up', 'ffn_w_down', 'final_norm']
TWIN_DIFF_INPUT = 'x'
TWIN_INPUTS = ['x', 'ev_norm', 'ev_w_in', 'ev_gla_w_gate', 'ev_gla_b_gate', 'ev_gla_w_onorm', 'ev_lru_conv_w', 'ev_lru_conv_b', 'ev_lru_w_a', 'ev_lru_b_a', 'ev_lru_w_i', 'ev_lru_b_i', 'ev_lru_lam', 'ev_w_out', 'od_norm', 'od_w_in', 'od_conv_w', 'od_conv_b', 'od_dt_bias', 'od_a_log', 'od_d_skip', 'od_gnorm', 'od_w_out', 'ffn_norm', 'ffn_w_gate', 'ffn_w_up', 'ffn_w_down', 'final_norm', 'loss_target', 'm_ev_norm', 'm_ev_w_in', 'm_ev_gla_w_gate', 'm_ev_gla_b_gate', 'm_ev_gla_w_onorm', 'm_ev_lru_conv_w', 'm_ev_lru_conv_b', 'm_ev_lru_w_a', 'm_ev_lru_b_a', 'm_ev_lru_w_i', 'm_ev_lru_b_i', 'm_ev_lru_lam', 'm_ev_w_out', 'm_od_norm', 'm_od_w_in', 'm_od_conv_w', 'm_od_conv_b', 'm_od_dt_bias', 'm_od_a_log', 'm_od_d_skip', 'm_od_gnorm', 'm_od_w_out', 'm_ffn_norm', 'm_ffn_w_gate', 'm_ffn_w_up', 'm_ffn_w_down', 'm_final_norm', 'v_ev_norm', 'v_ev_w_in', 'v_ev_gla_w_gate', 'v_ev_gla_b_gate', 'v_ev_gla_w_onorm', 'v_ev_lru_conv_w', 'v_ev_lru_conv_b', 'v_ev_lru_w_a', 'v_ev_lru_b_a', 'v_ev_lru_w_i', 'v_ev_lru_b_i', 'v_ev_lru_lam', 'v_ev_w_out', 'v_od_norm', 'v_od_w_in', 'v_od_conv_w', 'v_od_conv_b', 'v_od_dt_bias', 'v_od_a_log', 'v_od_d_skip', 'v_od_gnorm', 'v_od_w_out', 'v_ffn_norm', 'v_ffn_w_gate', 'v_ffn_w_up', 'v_ffn_w_down', 'v_final_norm']
TWIN_OUTPUTS = ['loss', 'grad_x', 'grad_ev_norm', 'grad_ev_w_in', 'grad_ev_gla_w_gate', 'grad_ev_gla_b_gate', 'grad_ev_gla_w_onorm', 'grad_ev_lru_conv_w', 'grad_ev_lru_conv_b', 'grad_ev_lru_w_a', 'grad_ev_lru_b_a', 'grad_ev_lru_w_i', 'grad_ev_lru_b_i', 'grad_ev_lru_lam', 'grad_ev_w_out', 'grad_od_norm', 'grad_od_w_in', 'grad_od_conv_w', 'grad_od_conv_b', 'grad_od_dt_bias', 'grad_od_a_log', 'grad_od_d_skip', 'grad_od_gnorm', 'grad_od_w_out', 'grad_ffn_norm', 'grad_ffn_w_gate', 'grad_ffn_w_up', 'grad_ffn_w_down', 'grad_final_norm', 'delta_ev_norm', 'delta_ev_w_in', 'delta_ev_gla_w_gate', 'delta_ev_gla_b_gate', 'delta_ev_gla_w_onorm', 'delta_ev_lru_conv_w', 'delta_ev_lru_conv_b', 'delta_ev_lru_w_a', 'delta_ev_lru_b_a', 'delta_ev_lru_w_i', 'delta_ev_lru_b_i', 'delta_ev_lru_lam', 'delta_ev_w_out', 'delta_od_norm', 'delta_od_w_in', 'delta_od_conv_w', 'delta_od_conv_b', 'delta_od_dt_bias', 'delta_od_a_log', 'delta_od_d_skip', 'delta_od_gnorm', 'delta_od_w_out', 'delta_ffn_norm', 'delta_ffn_w_gate', 'delta_ffn_w_up', 'delta_ffn_w_down', 'delta_final_norm', 'new_m_ev_norm', 'new_m_ev_w_in', 'new_m_ev_gla_w_gate', 'new_m_ev_gla_b_gate', 'new_m_ev_gla_w_onorm', 'new_m_ev_lru_conv_w', 'new_m_ev_lru_conv_b', 'new_m_ev_lru_w_a', 'new_m_ev_lru_b_a', 'new_m_ev_lru_w_i', 'new_m_ev_lru_b_i', 'new_m_ev_lru_lam', 'new_m_ev_w_out', 'new_m_od_norm', 'new_m_od_w_in', 'new_m_od_conv_w', 'new_m_od_conv_b', 'new_m_od_dt_bias', 'new_m_od_a_log', 'new_m_od_d_skip', 'new_m_od_gnorm', 'new_m_od_w_out', 'new_m_ffn_norm', 'new_m_ffn_w_gate', 'new_m_ffn_w_up', 'new_m_ffn_w_down', 'new_m_final_norm', 'new_v_ev_norm', 'new_v_ev_w_in', 'new_v_ev_gla_w_gate', 'new_v_ev_gla_b_gate', 'new_v_ev_gla_w_onorm', 'new_v_ev_lru_conv_w', 'new_v_ev_lru_conv_b', 'new_v_ev_lru_w_a', 'new_v_ev_lru_b_a', 'new_v_ev_lru_w_i', 'new_v_ev_lru_b_i', 'new_v_ev_lru_lam', 'new_v_ev_w_out', 'new_v_od_norm', 'new_v_od_w_in', 'new_v_od_conv_w', 'new_v_od_conv_b', 'new_v_od_dt_bias', 'new_v_od_a_log', 'new_v_od_d_skip', 'new_v_od_gnorm', 'new_v_od_w_out', 'new_v_ffn_norm', 'new_v_ffn_w_gate', 'new_v_ffn_w_up', 'new_v_ffn_w_down', 'new_v_final_norm']
TWIN_LEAF_KINDS = {'loss': 'loss', 'grad_x': 'grad_x', 'grad_ev_norm': 'grad_w', 'grad_ev_w_in': 'grad_w', 'grad_ev_gla_w_gate': 'grad_w', 'grad_ev_gla_b_gate': 'grad_w', 'grad_ev_gla_w_onorm': 'grad_w', 'grad_ev_lru_conv_w': 'grad_w', 'grad_ev_lru_conv_b': 'grad_w', 'grad_ev_lru_w_a': 'grad_w', 'grad_ev_lru_b_a': 'grad_w', 'grad_ev_lru_w_i': 'grad_w', 'grad_ev_lru_b_i': 'grad_w', 'grad_ev_lru_lam': 'grad_w', 'grad_ev_w_out': 'grad_w', 'grad_od_norm': 'grad_w', 'grad_od_w_in': 'grad_w', 'grad_od_conv_w': 'grad_w', 'grad_od_conv_b': 'grad_w', 'grad_od_dt_bias': 'grad_w', 'grad_od_a_log': 'grad_w', 'grad_od_d_skip': 'grad_w', 'grad_od_gnorm': 'grad_w', 'grad_od_w_out': 'grad_w', 'grad_ffn_norm': 'grad_w', 'grad_ffn_w_gate': 'grad_w', 'grad_ffn_w_up': 'grad_w', 'grad_ffn_w_down': 'grad_w', 'grad_final_norm': 'grad_w', 'delta_ev_norm': 'delta_w', 'delta_ev_w_in': 'delta_w', 'delta_ev_gla_w_gate': 'delta_w', 'delta_ev_gla_b_gate': 'delta_w', 'delta_ev_gla_w_onorm': 'delta_w', 'delta_ev_lru_conv_w': 'delta_w', 'delta_ev_lru_conv_b': 'delta_w', 'delta_ev_lru_w_a': 'delta_w', 'delta_ev_lru_b_a': 'delta_w', 'delta_ev_lru_w_i': 'delta_w', 'delta_ev_lru_b_i': 'delta_w', 'delta_ev_lru_lam': 'delta_w', 'delta_ev_w_out': 'delta_w', 'delta_od_norm': 'delta_w', 'delta_od_w_in': 'delta_w', 'delta_od_conv_w': 'delta_w', 'delta_od_conv_b': 'delta_w', 'delta_od_dt_bias': 'delta_w', 'delta_od_a_log': 'delta_w', 'delta_od_d_skip': 'delta_w', 'delta_od_gnorm': 'delta_w', 'delta_od_w_out': 'delta_w', 'delta_ffn_norm': 'delta_w', 'delta_ffn_w_gate': 'delta_w', 'delta_ffn_w_up': 'delta_w', 'delta_ffn_w_down': 'delta_w', 'delta_final_norm': 'delta_w', 'new_m_ev_norm': 'new_m', 'new_m_ev_w_in': 'new_m', 'new_m_ev_gla_w_gate': 'new_m', 'new_m_ev_gla_b_gate': 'new_m', 'new_m_ev_gla_w_onorm': 'new_m', 'new_m_ev_lru_conv_w': 'new_m', 'new_m_ev_lru_conv_b': 'new_m', 'new_m_ev_lru_w_a': 'new_m', 'new_m_ev_lru_b_a': 'new_m', 'new_m_ev_lru_w_i': 'new_m', 'new_m_ev_lru_b_i': 'new_m', 'new_m_ev_lru_lam': 'new_m', 'new_m_ev_w_out': 'new_m', 'new_m_od_norm': 'new_m', 'new_m_od_w_in': 'new_m', 'new_m_od_conv_w': 'new_m', 'new_m_od_conv_b': 'new_m', 'new_m_od_dt_bias': 'new_m', 'new_m_od_a_log': 'new_m', 'new_m_od_d_skip': 'new_m', 'new_m_od_gnorm': 'new_m', 'new_m_od_w_out': 'new_m', 'new_m_ffn_norm': 'new_m', 'new_m_ffn_w_gate': 'new_m', 'new_m_ffn_w_up': 'new_m', 'new_m_ffn_w_down': 'new_m', 'new_m_final_norm': 'new_m', 'new_v_ev_norm': 'new_v', 'new_v_ev_w_in': 'new_v', 'new_v_ev_gla_w_gate': 'new_v', 'new_v_ev_gla_b_gate': 'new_v', 'new_v_ev_gla_w_onorm': 'new_v', 'new_v_ev_lru_conv_w': 'new_v', 'new_v_ev_lru_conv_b': 'new_v', 'new_v_ev_lru_w_a': 'new_v', 'new_v_ev_lru_b_a': 'new_v', 'new_v_ev_lru_w_i': 'new_v', 'new_v_ev_lru_b_i': 'new_v', 'new_v_ev_lru_lam': 'new_v', 'new_v_ev_w_out': 'new_v', 'new_v_od_norm': 'new_v', 'new_v_od_w_in': 'new_v', 'new_v_od_conv_w': 'new_v', 'new_v_od_conv_b': 'new_v', 'new_v_od_dt_bias': 'new_v', 'new_v_od_a_log': 'new_v', 'new_v_od_d_skip': 'new_v', 'new_v_od_gnorm': 'new_v', 'new_v_od_w_out': 'new_v', 'new_v_ffn_norm': 'new_v', 'new_v_ffn_w_gate': 'new_v', 'new_v_ffn_w_up': 'new_v', 'new_v_ffn_w_down': 'new_v', 'new_v_final_norm': 'new_v'}


def _forward(args):
    return _fwd_reference(*[args[k] for k in FWD_PARAMS])


def _output_shape():
    def fwd():
        inp = _fwd_setup_inputs(0)
        return _fwd_reference(*[inp[k] for k in FWD_PARAMS])
    out = _jax.eval_shape(fwd)
    return out.shape, out.dtype

N_MICROBATCH = 1
ADAM_LR = 0.001
ADAM_B1 = 0.9
ADAM_B2 = 0.999
ADAM_EPS = 1e-08
ADAM_WD = 0.01
ADAM_STEP = 10
PER_EXAMPLE_BATCH_AXIS = {'x': 0, 'loss_target': 0}
SHARED_INPUTS = []
_WEIGHT_DTYPES = {'ev_norm': _jnp.float32, 'ev_w_in': _jnp.float32, 'ev_gla_w_gate': _jnp.float32, 'ev_gla_b_gate': _jnp.float32, 'ev_gla_w_onorm': _jnp.float32, 'ev_lru_conv_w': _jnp.float32, 'ev_lru_conv_b': _jnp.float32, 'ev_lru_w_a': _jnp.float32, 'ev_lru_b_a': _jnp.float32, 'ev_lru_w_i': _jnp.float32, 'ev_lru_b_i': _jnp.float32, 'ev_lru_lam': _jnp.float32, 'ev_w_out': _jnp.float32, 'od_norm': _jnp.float32, 'od_w_in': _jnp.float32, 'od_conv_w': _jnp.float32, 'od_conv_b': _jnp.float32, 'od_dt_bias': _jnp.float32, 'od_a_log': _jnp.float32, 'od_d_skip': _jnp.float32, 'od_gnorm': _jnp.float32, 'od_w_out': _jnp.float32, 'ffn_norm': _jnp.float32, 'ffn_w_gate': _jnp.float32, 'ffn_w_up': _jnp.float32, 'ffn_w_down': _jnp.float32, 'final_norm': _jnp.float32}
MOMENT_SCALE = {'ev_norm': 1.092980e-01, 'ev_w_in': 6.929627e-02, 'ev_gla_w_gate': 1.103399e-02, 'ev_gla_b_gate': 4.348453e-02, 'ev_gla_w_onorm': 1.388562e-01, 'ev_lru_conv_w': 5.352081e-02, 'ev_lru_conv_b': 5.346711e-01, 'ev_lru_w_a': 1.478285e-02, 'ev_lru_b_a': 1.245963e-02, 'ev_lru_w_i': 2.667857e-02, 'ev_lru_b_i': 1.952999e-02, 'ev_lru_lam': 2.378820e-02, 'ev_w_out': 5.887356e-02, 'od_norm': 9.036117e-02, 'od_w_in': 3.974007e-02, 'od_conv_w': 3.655330e-02, 'od_conv_b': 4.757157e-02, 'od_dt_bias': 1.193479e-01, 'od_a_log': 1.078678e-01, 'od_d_skip': 1.939162e-01, 'od_gnorm': 4.197825e-02, 'od_w_out': 5.839100e-02, 'ffn_norm': 6.434744e-02, 'ffn_w_gate': 2.760596e-02, 'ffn_w_up': 2.674918e-02, 'ffn_w_down': 4.429884e-02, 'final_norm': 1.599072e+01}


def _to_microbatches(a, axis):
    t = _jnp.moveaxis(a, axis, 0)
    t = t.reshape((N_MICROBATCH, t.shape[0] // N_MICROBATCH) + t.shape[1:])
    return _jnp.moveaxis(t, 1, axis + 1)


def setup_inputs(seed: int = 0) -> dict:
    inp = _fwd_setup_inputs(seed)
    key = _jax.random.fold_in(_jax.random.key(seed), 7919)
    shape, _ = _output_shape()
    out = dict(inp)
    out["loss_target"] = _jax.random.normal(_jax.random.fold_in(key, 0), shape, _jnp.float32)
    for i, name in enumerate(TWIN_WEIGHTS):
        w = inp[name].astype(_jnp.float32)
        if MOMENT_SCALE is None:
            s = _jnp.sqrt(_jnp.mean(_jnp.square(w)) + 1e-30)
        else:
            s = MOMENT_SCALE[name]
        km, kv = _jax.random.split(_jax.random.fold_in(key, i + 1))
        out[name] = w
        out["m_" + name] = s * _jax.random.normal(km, w.shape, _jnp.float32)
        out["v_" + name] = (s * s) * _jax.random.uniform(kv, w.shape, _jnp.float32, 0.5, 1.5)
    if N_MICROBATCH > 1:
        for name, axis in PER_EXAMPLE_BATCH_AXIS.items():
            out[name] = _to_microbatches(out[name], axis)
    return {'x': out['x'], 'ev_norm': out['ev_norm'], 'ev_w_in': out['ev_w_in'], 'ev_gla_w_gate': out['ev_gla_w_gate'], 'ev_gla_b_gate': out['ev_gla_b_gate'], 'ev_gla_w_onorm': out['ev_gla_w_onorm'], 'ev_lru_conv_w': out['ev_lru_conv_w'], 'ev_lru_conv_b': out['ev_lru_conv_b'], 'ev_lru_w_a': out['ev_lru_w_a'], 'ev_lru_b_a': out['ev_lru_b_a'], 'ev_lru_w_i': out['ev_lru_w_i'], 'ev_lru_b_i': out['ev_lru_b_i'], 'ev_lru_lam': out['ev_lru_lam'], 'ev_w_out': out['ev_w_out'], 'od_norm': out['od_norm'], 'od_w_in': out['od_w_in'], 'od_conv_w': out['od_conv_w'], 'od_conv_b': out['od_conv_b'], 'od_dt_bias': out['od_dt_bias'], 'od_a_log': out['od_a_log'], 'od_d_skip': out['od_d_skip'], 'od_gnorm': out['od_gnorm'], 'od_w_out': out['od_w_out'], 'ffn_norm': out['ffn_norm'], 'ffn_w_gate': out['ffn_w_gate'], 'ffn_w_up': out['ffn_w_up'], 'ffn_w_down': out['ffn_w_down'], 'final_norm': out['final_norm'], 'loss_target': out['loss_target'], 'm_ev_norm': out['m_ev_norm'], 'm_ev_w_in': out['m_ev_w_in'], 'm_ev_gla_w_gate': out['m_ev_gla_w_gate'], 'm_ev_gla_b_gate': out['m_ev_gla_b_gate'], 'm_ev_gla_w_onorm': out['m_ev_gla_w_onorm'], 'm_ev_lru_conv_w': out['m_ev_lru_conv_w'], 'm_ev_lru_conv_b': out['m_ev_lru_conv_b'], 'm_ev_lru_w_a': out['m_ev_lru_w_a'], 'm_ev_lru_b_a': out['m_ev_lru_b_a'], 'm_ev_lru_w_i': out['m_ev_lru_w_i'], 'm_ev_lru_b_i': out['m_ev_lru_b_i'], 'm_ev_lru_lam': out['m_ev_lru_lam'], 'm_ev_w_out': out['m_ev_w_out'], 'm_od_norm': out['m_od_norm'], 'm_od_w_in': out['m_od_w_in'], 'm_od_conv_w': out['m_od_conv_w'], 'm_od_conv_b': out['m_od_conv_b'], 'm_od_dt_bias': out['m_od_dt_bias'], 'm_od_a_log': out['m_od_a_log'], 'm_od_d_skip': out['m_od_d_skip'], 'm_od_gnorm': out['m_od_gnorm'], 'm_od_w_out': out['m_od_w_out'], 'm_ffn_norm': out['m_ffn_norm'], 'm_ffn_w_gate': out['m_ffn_w_gate'], 'm_ffn_w_up': out['m_ffn_w_up'], 'm_ffn_w_down': out['m_ffn_w_down'], 'm_final_norm': out['m_final_norm'], 'v_ev_norm': out['v_ev_norm'], 'v_ev_w_in': out['v_ev_w_in'], 'v_ev_gla_w_gate': out['v_ev_gla_w_gate'], 'v_ev_gla_b_gate': out['v_ev_gla_b_gate'], 'v_ev_gla_w_onorm': out['v_ev_gla_w_onorm'], 'v_ev_lru_conv_w': out['v_ev_lru_conv_w'], 'v_ev_lru_conv_b': out['v_ev_lru_conv_b'], 'v_ev_lru_w_a': out['v_ev_lru_w_a'], 'v_ev_lru_b_a': out['v_ev_lru_b_a'], 'v_ev_lru_w_i': out['v_ev_lru_w_i'], 'v_ev_lru_b_i': out['v_ev_lru_b_i'], 'v_ev_lru_lam': out['v_ev_lru_lam'], 'v_ev_w_out': out['v_ev_w_out'], 'v_od_norm': out['v_od_norm'], 'v_od_w_in': out['v_od_w_in'], 'v_od_conv_w': out['v_od_conv_w'], 'v_od_conv_b': out['v_od_conv_b'], 'v_od_dt_bias': out['v_od_dt_bias'], 'v_od_a_log': out['v_od_a_log'], 'v_od_d_skip': out['v_od_d_skip'], 'v_od_gnorm': out['v_od_gnorm'], 'v_od_w_out': out['v_od_w_out'], 'v_ffn_norm': out['v_ffn_norm'], 'v_ffn_w_gate': out['v_ffn_w_gate'], 'v_ffn_w_up': out['v_ffn_w_up'], 'v_ffn_w_down': out['v_ffn_w_down'], 'v_final_norm': out['v_final_norm']}


def _loss(weights, diff, rest, loss_target):
    with _jax.named_scope("forward"):
        args = {**rest, TWIN_DIFF_INPUT: diff, **{k: w.astype(_WEIGHT_DTYPES[k]) for k, w in weights.items()}}
        y = _forward(args)
    with _jax.named_scope("loss_head"):
        err = _jnp.square(y.astype(_jnp.float32) - loss_target)
        return 0.5 * _jnp.sum(_jnp.mean(err, axis=-1)) if err.ndim else 0.5 * err


def _adamw(w, g, m, v):
    m = ADAM_B1 * m + (1.0 - ADAM_B1) * g
    v = ADAM_B2 * v + (1.0 - ADAM_B2) * _jnp.square(g)
    m_hat = m / (1.0 - ADAM_B1 ** ADAM_STEP)
    v_hat = v / (1.0 - ADAM_B2 ** ADAM_STEP)
    delta = -ADAM_LR * (m_hat / (_jnp.sqrt(v_hat) + ADAM_EPS) + ADAM_WD * w)
    return delta, m, v


def reference(x, ev_norm, ev_w_in, ev_gla_w_gate, ev_gla_b_gate, ev_gla_w_onorm, ev_lru_conv_w, ev_lru_conv_b, ev_lru_w_a, ev_lru_b_a, ev_lru_w_i, ev_lru_b_i, ev_lru_lam, ev_w_out, od_norm, od_w_in, od_conv_w, od_conv_b, od_dt_bias, od_a_log, od_d_skip, od_gnorm, od_w_out, ffn_norm, ffn_w_gate, ffn_w_up, ffn_w_down, final_norm, loss_target, m_ev_norm, m_ev_w_in, m_ev_gla_w_gate, m_ev_gla_b_gate, m_ev_gla_w_onorm, m_ev_lru_conv_w, m_ev_lru_conv_b, m_ev_lru_w_a, m_ev_lru_b_a, m_ev_lru_w_i, m_ev_lru_b_i, m_ev_lru_lam, m_ev_w_out, m_od_norm, m_od_w_in, m_od_conv_w, m_od_conv_b, m_od_dt_bias, m_od_a_log, m_od_d_skip, m_od_gnorm, m_od_w_out, m_ffn_norm, m_ffn_w_gate, m_ffn_w_up, m_ffn_w_down, m_final_norm, v_ev_norm, v_ev_w_in, v_ev_gla_w_gate, v_ev_gla_b_gate, v_ev_gla_w_onorm, v_ev_lru_conv_w, v_ev_lru_conv_b, v_ev_lru_w_a, v_ev_lru_b_a, v_ev_lru_w_i, v_ev_lru_b_i, v_ev_lru_lam, v_ev_w_out, v_od_norm, v_od_w_in, v_od_conv_w, v_od_conv_b, v_od_dt_bias, v_od_a_log, v_od_d_skip, v_od_gnorm, v_od_w_out, v_ffn_norm, v_ffn_w_gate, v_ffn_w_up, v_ffn_w_down, v_final_norm):
    given = dict(x=x, ev_norm=ev_norm, ev_w_in=ev_w_in, ev_gla_w_gate=ev_gla_w_gate, ev_gla_b_gate=ev_gla_b_gate, ev_gla_w_onorm=ev_gla_w_onorm, ev_lru_conv_w=ev_lru_conv_w, ev_lru_conv_b=ev_lru_conv_b, ev_lru_w_a=ev_lru_w_a, ev_lru_b_a=ev_lru_b_a, ev_lru_w_i=ev_lru_w_i, ev_lru_b_i=ev_lru_b_i, ev_lru_lam=ev_lru_lam, ev_w_out=ev_w_out, od_norm=od_norm, od_w_in=od_w_in, od_conv_w=od_conv_w, od_conv_b=od_conv_b, od_dt_bias=od_dt_bias, od_a_log=od_a_log, od_d_skip=od_d_skip, od_gnorm=od_gnorm, od_w_out=od_w_out, ffn_norm=ffn_norm, ffn_w_gate=ffn_w_gate, ffn_w_up=ffn_w_up, ffn_w_down=ffn_w_down, final_norm=final_norm, loss_target=loss_target, m_ev_norm=m_ev_norm, m_ev_w_in=m_ev_w_in, m_ev_gla_w_gate=m_ev_gla_w_gate, m_ev_gla_b_gate=m_ev_gla_b_gate, m_ev_gla_w_onorm=m_ev_gla_w_onorm, m_ev_lru_conv_w=m_ev_lru_conv_w, m_ev_lru_conv_b=m_ev_lru_conv_b, m_ev_lru_w_a=m_ev_lru_w_a, m_ev_lru_b_a=m_ev_lru_b_a, m_ev_lru_w_i=m_ev_lru_w_i, m_ev_lru_b_i=m_ev_lru_b_i, m_ev_lru_lam=m_ev_lru_lam, m_ev_w_out=m_ev_w_out, m_od_norm=m_od_norm, m_od_w_in=m_od_w_in, m_od_conv_w=m_od_conv_w, m_od_conv_b=m_od_conv_b, m_od_dt_bias=m_od_dt_bias, m_od_a_log=m_od_a_log, m_od_d_skip=m_od_d_skip, m_od_gnorm=m_od_gnorm, m_od_w_out=m_od_w_out, m_ffn_norm=m_ffn_norm, m_ffn_w_gate=m_ffn_w_gate, m_ffn_w_up=m_ffn_w_up, m_ffn_w_down=m_ffn_w_down, m_final_norm=m_final_norm, v_ev_norm=v_ev_norm, v_ev_w_in=v_ev_w_in, v_ev_gla_w_gate=v_ev_gla_w_gate, v_ev_gla_b_gate=v_ev_gla_b_gate, v_ev_gla_w_onorm=v_ev_gla_w_onorm, v_ev_lru_conv_w=v_ev_lru_conv_w, v_ev_lru_conv_b=v_ev_lru_conv_b, v_ev_lru_w_a=v_ev_lru_w_a, v_ev_lru_b_a=v_ev_lru_b_a, v_ev_lru_w_i=v_ev_lru_w_i, v_ev_lru_b_i=v_ev_lru_b_i, v_ev_lru_lam=v_ev_lru_lam, v_ev_w_out=v_ev_w_out, v_od_norm=v_od_norm, v_od_w_in=v_od_w_in, v_od_conv_w=v_od_conv_w, v_od_conv_b=v_od_conv_b, v_od_dt_bias=v_od_dt_bias, v_od_a_log=v_od_a_log, v_od_d_skip=v_od_d_skip, v_od_gnorm=v_od_gnorm, v_od_w_out=v_od_w_out, v_ffn_norm=v_ffn_norm, v_ffn_w_gate=v_ffn_w_gate, v_ffn_w_up=v_ffn_w_up, v_ffn_w_down=v_ffn_w_down, v_final_norm=v_final_norm)
    weights = {n: given[n] for n in TWIN_WEIGHTS}
    shared = {n: given[n] for n in SHARED_INPUTS}
    per_example = {n: given[n] for n in ['x']}
    grad_fn = _jax.value_and_grad(_loss, argnums=(0, 1))

    def one_microbatch(ex, loss_target):
        ex = dict(ex)
        diff = ex.pop(TWIN_DIFF_INPUT)
        return grad_fn(weights, diff, {**shared, **ex}, loss_target)

    if N_MICROBATCH == 1:
        loss, (grad_w, grad_x) = one_microbatch(per_example, given["loss_target"])
    else:
        def body(carry, xs):
            loss_sum, grad_sum = carry
            l_k, (gw_k, gx_k) = one_microbatch(xs[0], xs[1])
            with _jax.named_scope("update"):
                return (loss_sum + l_k, _jax.tree.map(_jnp.add, grad_sum, gw_k)), gx_k

        init = (_jnp.zeros((), _jnp.float32), _jax.tree.map(_jnp.zeros_like, weights))
        (loss, grad_w), grad_x = _jax.lax.scan(body, init, (per_example, given["loss_target"]))
    with _jax.named_scope("update"):
        delta_w, new_m, new_v = {}, {}, {}
        for n in TWIN_WEIGHTS:
            delta_w[n], new_m[n], new_v[n] = _adamw(weights[n], grad_w[n], given["m_" + n], given["v_" + n])
    return (loss, grad_x, *[grad_w[n] for n in TWIN_WEIGHTS], *[delta_w[n] for n in TWIN_WEIGHTS],
            *[new_m[n] for n in TWIN_WEIGHTS], *[new_v[n] for n in TWIN_WEIGHTS])
```

```python
import functools
import math
from typing import NamedTuple

import jax
import jax.numpy as jnp
from jax import lax
from jax.experimental import pallas as pl
from jax.experimental.pallas import tpu as pltpu

F32 = jnp.float32
BF16 = jnp.bfloat16
MESH_AXES = ("x", "y", "c")
N_DEV = 8
LANES = 128
SUBLANES = 8
VMEM_LIMIT = 56 * 1024 * 1024

NORM_EPS = 1e-6
CONV_WIDTH = 4
CHUNK = 64
HEAD_K = 128
HEAD_V = 256
GATE_RANK = 16
GATE_NORM = 16.0
LRU_BLOCK = 128
LRU_C = 8.0
SSD_P = 64
SSD_N = 128
SSD_HG = 8
SSD_GW = SSD_HG * SSD_P

ADAM_LR = 0.001
ADAM_B1 = 0.9
ADAM_B2 = 0.999
ADAM_EPS = 1e-08
ADAM_WD = 0.01
ADAM_STEP = 10


class Cfg(NamedTuple):
    S: int
    D: int
    DFF: int

    @property
    def GH(self):
        return self.D // 512

    @property
    def NB(self):
        return self.D // 256

    @property
    def NG(self):
        return self.D // 256

    @property
    def DK(self):
        return HEAD_K * self.GH

    @property
    def DV(self):
        return HEAD_V * self.GH

    @property
    def W(self):
        return LRU_BLOCK * self.NB

    @property
    def DI(self):
        return SSD_GW * self.NG

    @property
    def CD(self):
        return self.DI + 2 * self.NG * SSD_N

    @property
    def NH(self):
        return SSD_HG * self.NG

    @property
    def EVEN_IN(self):
        return 2 * self.DK + 2 * self.DV + GATE_RANK + 2 * self.W

    @property
    def ODD_IN(self):
        return self.DI + self.CD + self.NH

    @property
    def EP(self):
        return _round_up(2 * self.DK + 2 * self.DV + 2 * self.W + LANES, 768)

    @property
    def OP(self):
        return _round_up(self.DI + self.CD + LANES, 768)


def _round_up(n, m):
    return (n + m - 1) // m * m


def _tile(n, pref):
    if n <= pref:
        return n
    t = pref - pref % LANES
    while n % t:
        t -= LANES
    return t


def _cparams(n_axes):
    return pltpu.CompilerParams(dimension_semantics=("arbitrary",) * n_axes, vmem_limit_bytes=VMEM_LIMIT)


def _dg(a, b, ca, cb):
    return lax.dot_general(a.astype(BF16), b.astype(BF16), (((ca,), (cb,)), ((), ())), preferred_element_type=F32)


@functools.partial(jax.custom_vjp, nondiff_argnums=(2, 3))
def bdot(a, b, ca, cb):
    return _dg(a, b, ca, cb)


def _bdot_fwd(a, b, ca, cb):
    return _dg(a, b, ca, cb), (a, b)


def _bdot_bwd(ca, cb, res, g):
    a, b = res
    da = _dg(g, b, 1, 1 - cb) if ca == 1 else _dg(b, g, 1 - cb, 1)
    db = _dg(a, g, 1 - ca, 0) if cb == 0 else _dg(g, a, 0, 1 - ca)
    return da.astype(a.dtype), db.astype(b.dtype)


bdot.defvjp(_bdot_fwd, _bdot_bwd)


def _lower_tri(n):
    r = lax.broadcasted_iota(jnp.int32, (n, n), 0)
    c = lax.broadcasted_iota(jnp.int32, (n, n), 1)
    return c <= r


@jax.custom_vjp
def cumsum_rows(x):
    n = x.shape[0]
    return lax.dot_general(_lower_tri(n).astype(F32), x, (((1,), (0,)), ((), ())),
                           precision=lax.Precision.HIGHEST, preferred_element_type=F32)


def _cumsum_fwd(x):
    return cumsum_rows(x), None


def _cumsum_bwd(_, g):
    n = g.shape[0]
    return (lax.dot_general(_lower_tri(n).astype(F32), g, (((0,), (0,)), ((), ())),
                            precision=lax.Precision.HIGHEST, preferred_element_type=F32),)


cumsum_rows.defvjp(_cumsum_fwd, _cumsum_bwd)


def _row(x, i):
    r = lax.broadcasted_iota(jnp.int32, x.shape, 0)
    return jnp.sum(jnp.where(r == i, x, 0.0), axis=0, keepdims=True)


def _softplus_raw(x):
    return jnp.maximum(x, 0.0) + jnp.log(1.0 + jnp.exp(-jnp.abs(x)))


@jax.custom_vjp
def softplus(x):
    return _softplus_raw(x)


softplus.defvjp(lambda x: (_softplus_raw(x), x), lambda x, g: (g * jax.nn.sigmoid(x),))


@jax.custom_vjp
def log_sigmoid(x):
    return -_softplus_raw(-x)


log_sigmoid.defvjp(lambda x: (-_softplus_raw(-x), x), lambda x, g: (g * jax.nn.sigmoid(-x),))


def silu(x):
    return x * jax.nn.sigmoid(x)


def gelu_tanh(x):
    return 0.5 * x * (1.0 + jnp.tanh(math.sqrt(2.0 / math.pi) * (x + 0.044715 * (x * x * x))))


def _expm1(x):
    series = x * (1.0 + 0.5 * x * (1.0 + (1.0 / 3.0) * x))
    return jnp.where(jnp.abs(x) < 1e-2, series, jnp.exp(x) - 1.0)


def rms(x, w):
    return x * lax.rsqrt(jnp.mean(x * x, axis=-1, keepdims=True) + NORM_EPS) * w


def _rows_iota(shape):
    return lax.broadcasted_iota(jnp.int32, shape, 0)


def _scan_up(a, u):
    n = a.shape[0]
    r = _rows_iota(a.shape)
    d = 1
    while d < n:
        m = r >= d
        a_s = jnp.where(m, pltpu.roll(a, d, 0), 1.0)
        u_s = jnp.where(m, pltpu.roll(u, d, 0), 0.0)
        u = a * u_s + u
        a = a * a_s
        d *= 2
    return u


def _scan_down(a, u):
    n = a.shape[0]
    r = _rows_iota(a.shape)
    d = 1
    while d < n:
        m = r < n - d
        a_s = jnp.where(m, pltpu.roll(a, n - d, 0), 1.0)
        u_s = jnp.where(m, pltpu.roll(u, n - d, 0), 0.0)
        u = a * u_s + u
        a = a * a_s
        d *= 2
    return u


@jax.custom_vjp
def lin_scan(a, u):
    return _scan_up(a, u)


def _lin_scan_fwd(a, u):
    h = _scan_up(a, u)
    return h, (a, h)


def _lin_scan_bwd(res, g):
    a, h = res
    n = a.shape[0]
    r = _rows_iota(a.shape)
    a_next = jnp.where(r < n - 1, pltpu.roll(a, n - 1, 0), 0.0)
    gt = _scan_down(a_next, g)
    h_prev = jnp.where(r >= 1, pltpu.roll(h, 1, 0), 0.0)
    return gt * h_prev, gt


lin_scan.defvjp(_lin_scan_fwd, _lin_scan_bwd)


def _expand_heads(v):
    r = v.shape[0]
    return jnp.concatenate([jnp.broadcast_to(v[:, h:h + 1], (r, SSD_P)) for h in range(SSD_HG)], axis=1)


@jax.custom_vjp
def _split_heads(x):
    return tuple(x[:, h * SSD_P:(h + 1) * SSD_P] for h in range(SSD_HG))


_split_heads.defvjp(lambda x: (_split_heads(x), None), lambda _, gs: (jnp.concatenate(gs, axis=1),))


def matmul(a, b, *, ta=False, tb=False, res=None, out_dtype=F32, name, tm=1024, tn=1024, tk=512):
    if ta:
        kk, m = a.shape
    else:
        m, kk = a.shape
    if tb:
        n, kb = b.shape
    else:
        kb, n = b.shape
    assert kk == kb, (a.shape, b.shape, ta, tb)
    tm, tn, tk = _tile(m, tm), _tile(n, tn), _tile(kk, tk)
    nk = kk // tk
    ca, cb = (0 if ta else 1), (1 if tb else 0)

    def body(*refs):
        if res is None:
            a_ref, b_ref, o_ref, acc_ref = refs
        else:
            a_ref, b_ref, r_ref, o_ref, acc_ref = refs
        k = pl.program_id(2)

        @pl.when(k == 0)
        def _():
            acc_ref[...] = jnp.zeros_like(acc_ref)

        acc_ref[...] += lax.dot_general(a_ref[...].astype(BF16), b_ref[...].astype(BF16),
                                        (((ca,), (cb,)), ((), ())), preferred_element_type=F32)

        @pl.when(k == nk - 1)
        def _():
            acc = acc_ref[...]
            if res is not None:
                acc = acc + r_ref[...].astype(F32)
            o_ref[...] = acc.astype(o_ref.dtype)

    a_spec = pl.BlockSpec((tk, tm), lambda i, j, k: (k, i)) if ta else pl.BlockSpec((tm, tk), lambda i, j, k: (i, k))
    b_spec = pl.BlockSpec((tn, tk), lambda i, j, k: (j, k)) if tb else pl.BlockSpec((tk, tn), lambda i, j, k: (k, j))
    in_specs = [a_spec, b_spec]
    args = [a, b]
    if res is not None:
        in_specs.append(pl.BlockSpec((tm, tn), lambda i, j, k: (i, j)))
        args.append(res)
    return pl.pallas_call(
        body, name=name, grid=(m // tm, n // tn, nk), in_specs=in_specs,
        out_specs=pl.BlockSpec((tm, tn), lambda i, j, k: (i, j)),
        out_shape=jax.ShapeDtypeStruct((m, n), out_dtype),
        scratch_shapes=[pltpu.VMEM((tm, tn), F32)], compiler_params=_cparams(3),
    )(*args)


def seq_call(name, fn, grid, ins, outs, accs=(), carries=()):
    n_in, n_out, n_acc = len(ins), len(outs), len(accs)

    def body(*refs):
        in_refs = refs[:n_in]
        out_refs = refs[n_in:n_in + n_out]
        acc_refs = refs[n_in + n_out:n_in + n_out + n_acc]
        c_refs = refs[n_in + n_out + n_acc:]

        if acc_refs or c_refs:
            @pl.when(pl.program_id(1) == 0)
            def _():
                for r in tuple(acc_refs) + tuple(c_refs):
                    r[...] = jnp.zeros_like(r)

        o, a, c = fn([r[...] for r in in_refs], [r[...] for r in c_refs])
        for r, v in zip(out_refs, o, strict=True):
            r[...] = v.astype(r.dtype)
        for r, v in zip(acc_refs, a, strict=True):
            r[...] += v
        for r, v in zip(c_refs, c, strict=True):
            r[...] = v

    return pl.pallas_call(
        body, name=name, grid=grid,
        in_specs=[pl.BlockSpec(blk, im) for _, blk, im in ins],
        out_specs=[pl.BlockSpec(blk, im) for _, _, blk, im in outs] + [pl.BlockSpec(blk, im) for _, blk, im in accs],
        out_shape=[jax.ShapeDtypeStruct(s, d) for s, d, _, _ in outs] + [jax.ShapeDtypeStruct(s, F32) for s, _, _ in accs],
        scratch_shapes=[pltpu.VMEM(s, F32) for s in carries], compiler_params=_cparams(2),
    )(*[a for a, _, _ in ins])


def exchange(x, name):
    ns, r, lanes = x.shape
    assert ns in (1, N_DEV)

    def body(x_ref, o_ref, send_sems, recv_sems, local_sem):
        pos = [lax.axis_index(n) for n in MESH_AXES]
        me = 4 * pos[0] + 2 * pos[1] + pos[2]
        own = pltpu.make_async_copy(x_ref.at[0 if ns == 1 else me], o_ref.at[me], local_sem)
        own.start()
        copies = []
        for k in range(1, N_DEV):
            bits = ((k >> 2) & 1, (k >> 1) & 1, k & 1)
            peer = tuple(1 - p if b else p for p, b in zip(pos, bits))
            peer_id = 4 * peer[0] + 2 * peer[1] + peer[2]
            cp = pltpu.make_async_remote_copy(
                src_ref=x_ref.at[0 if ns == 1 else peer_id], dst_ref=o_ref.at[me],
                send_sem=send_sems.at[k - 1], recv_sem=recv_sems.at[k - 1],
                device_id=peer, device_id_type=pl.DeviceIdType.MESH)
            cp.start()
            copies.append(cp)
        for cp in copies:
            cp.wait()
        own.wait()

    hbm = pl.BlockSpec(memory_space=pltpu.HBM)
    return pl.pallas_call(
        body, name=name, in_specs=[hbm], out_specs=hbm,
        out_shape=jax.ShapeDtypeStruct((N_DEV, r, lanes), x.dtype),
        scratch_shapes=[pltpu.SemaphoreType.DMA((N_DEV - 1,)), pltpu.SemaphoreType.DMA((N_DEV - 1,)),
                        pltpu.SemaphoreType.DMA],
    )(x)


def reduce_slots(slots, name):
    _, r, lanes = slots.shape
    tr = _pick_rows(r)

    def body(s_ref, o_ref):
        acc = s_ref[0].astype(F32)
        for j in range(1, N_DEV):
            acc = acc + s_ref[j].astype(F32)
        o_ref[...] = acc

    return pl.pallas_call(
        body, name=name, grid=(r // tr,),
        in_specs=[pl.BlockSpec((N_DEV, tr, lanes), lambda i: (0, i, 0))],
        out_specs=pl.BlockSpec((tr, lanes), lambda i: (i, 0)),
        out_shape=jax.ShapeDtypeStruct((r, lanes), F32), compiler_params=_cparams(1),
    )(slots)


def _pick_rows(r, pref=2048):
    t = min(r, pref)
    t -= t % 16
    while r % t:
        t -= 16
    return t


def adamw(w, g, m, v, name):
    r, lanes = w.shape
    tr = _pick_rows(r)

    def body(w_ref, g_ref, m_ref, v_ref, d_ref, nm_ref, nv_ref):
        gg = g_ref[...]
        nm = ADAM_B1 * m_ref[...] + (1.0 - ADAM_B1) * gg
        nv = ADAM_B2 * v_ref[...] + (1.0 - ADAM_B2) * (gg * gg)
        m_hat = nm / (1.0 - ADAM_B1 ** ADAM_STEP)
        v_hat = nv / (1.0 - ADAM_B2 ** ADAM_STEP)
        d_ref[...] = -ADAM_LR * (m_hat / (jnp.sqrt(v_hat) + ADAM_EPS) + ADAM_WD * w_ref[...])
        nm_ref[...] = nm
        nv_ref[...] = nv

    spec = pl.BlockSpec((tr, lanes), lambda i: (i, 0))
    shp = jax.ShapeDtypeStruct((r, lanes), F32)
    return pl.pallas_call(body, name=name, grid=(r // tr,), in_specs=[spec] * 4, out_specs=[spec] * 3,
                          out_shape=[shp] * 3, compiler_params=_cparams(1))(w, g, m, v)


def _token_tile(cfg):
    return min(cfg.S, 256)


def norm_fwd(cfg, x, w, name):
    ts = _token_tile(cfg)
    d = x.shape[1]

    def fn(ins, _):
        xv, wv = ins
        return [rms(xv, wv)], [], []

    return seq_call(name, fn, (1, cfg.S // ts),
                    [(x, (ts, d), lambda g, t: (t, 0)), (w, (1, d), lambda g, t: (0, 0))],
                    [((cfg.S, d), BF16, (ts, d), lambda g, t: (t, 0))])[0]


def norm_bwd(cfg, x, w, dh, dres, name):
    ts = _token_tile(cfg)
    d = x.shape[1]

    def fn(ins, _):
        xv, wv, dhv, drv = ins
        _, vjp = jax.vjp(rms, xv, wv)
        dx, dw = vjp(dhv.astype(F32))
        return [dx + drv], [dw], []

    row = lambda g, t: (t, 0)
    out = seq_call(name, fn, (1, cfg.S // ts),
                   [(x, (ts, d), row), (w, (1, d), lambda g, t: (0, 0)), (dh, (ts, d), row), (dres, (ts, d), row)],
                   [((cfg.S, d), F32, (ts, d), row)],
                   accs=[((1, d), (1, d), lambda g, t: (0, 0))])
    return out[0], out[1]


def head_fwd_bwd(cfg, x, w, target, name):
    ts = _token_tile(cfg)
    d = x.shape[1]

    def fn(ins, _):
        xv, wv, tv = ins
        y, vjp = jax.vjp(rms, xv, wv)
        err = y - tv
        loss = 0.5 * jnp.sum(err * err) / d
        dx, dw = vjp(err / d)
        return [dx], [jnp.full((SUBLANES, LANES), loss, F32), dw], []

    row = lambda g, t: (t, 0)
    fixed = lambda g, t: (0, 0)
    dx, loss, dw = seq_call(name, fn, (1, cfg.S // ts),
                            [(x, (ts, d), row), (w, (1, d), fixed), (target, (ts, d), row)],
                            [((cfg.S, d), F32, (ts, d), row)],
                            accs=[((SUBLANES, LANES), (SUBLANES, LANES), fixed), ((1, d), (1, d), fixed)])
    return loss, dx, dw


def act_fwd(cfg, gate, up, name):
    ts, tf = min(cfg.S, 512), _tile(cfg.DFF, 512)

    def fn(ins, _):
        return [silu(ins[0]) * ins[1]], [], []

    blk = lambda g, t: (t, g)
    return seq_call(name, fn, (cfg.DFF // tf, cfg.S // ts),
                    [(gate, (ts, tf), blk), (up, (ts, tf), blk)],
                    [((cfg.S, cfg.DFF), BF16, (ts, tf), blk)])[0]


def act_bwd(cfg, gate, up, dact, name):
    ts, tf = min(cfg.S, 512), _tile(cfg.DFF, 512)

    def fn(ins, _):
        gv, uv, dv = ins
        _, vjp = jax.vjp(lambda a, b: silu(a) * b, gv, uv)
        dg, du = vjp(dv.astype(F32))
        return [dg, du], [], []

    blk = lambda g, t: (t, g)
    shp = ((cfg.S, cfg.DFF), BF16, (ts, tf), blk)
    return seq_call(name, fn, (cfg.DFF // tf, cfg.S // ts),
                    [(gate, (ts, tf), blk), (up, (ts, tf), blk), (dact, (ts, tf), blk)], [shp, shp])


CONV_COLS = 256
HALO = SUBLANES


def _shift_down(x, halo, j):
    if j == 0:
        return x
    r8 = _rows_iota(halo.shape)
    top = jnp.where(r8 >= j, pltpu.roll(x[:HALO], j, 0), pltpu.roll(halo, j, 0))
    return jnp.concatenate([top, pltpu.roll(x, j, 0)[HALO:]], axis=0)


def _shift_up(x, halo, j):
    if j == 0:
        return x
    n = x.shape[0]
    r8 = _rows_iota(halo.shape)
    bot = jnp.where(r8 < HALO - j, pltpu.roll(x[n - HALO:], HALO - j, 0), pltpu.roll(halo, HALO - j, 0))
    return jnp.concatenate([pltpu.roll(x, n - j, 0)[:n - HALO], bot], axis=0)


def _conv_tile(cfg):
    return min(cfg.S, 512)


def conv_fwd(cfg, src, col0, width, w, b, name):
    tt, cb = _conv_tile(cfg), CONV_COLS
    c0, hb = col0 // cb, tt // HALO
    nt = cfg.S // tt

    def body(x_ref, h_ref, w_ref, b_ref, o_ref):
        t = pl.program_id(1)
        x = x_ref[...]
        halo = jnp.where(t > 0, h_ref[...], 0.0)
        wv = w_ref[...]
        acc = b_ref[...] + wv[CONV_WIDTH - 1:CONV_WIDTH] * x
        for j in range(1, CONV_WIDTH):
            acc = acc + wv[CONV_WIDTH - 1 - j:CONV_WIDTH - j] * _shift_down(x, halo, j)
        o_ref[...] = acc

    return pl.pallas_call(
        body, name=name, grid=(width // cb, nt),
        in_specs=[pl.BlockSpec((tt, cb), lambda c, t: (t, c0 + c)),
                  pl.BlockSpec((HALO, cb), lambda c, t: (jnp.maximum(t * hb - 1, 0), c0 + c)),
                  pl.BlockSpec((CONV_WIDTH, cb), lambda c, t: (0, c)),
                  pl.BlockSpec((1, cb), lambda c, t: (0, c))],
        out_specs=pl.BlockSpec((tt, cb), lambda c, t: (t, c)),
        out_shape=jax.ShapeDtypeStruct((cfg.S, width), F32), compiler_params=_cparams(2),
    )(src, src, w, b)


def conv_bwd(cfg, src, col0, width, w, dy, name):
    tt, cb = _conv_tile(cfg), CONV_COLS
    c0, hb = col0 // cb, tt // HALO
    nt = cfg.S // tt

    def body(x_ref, h_ref, w_ref, dy_ref, dh_ref, dx_ref, dw_ref, db_ref):
        t = pl.program_id(1)

        @pl.when(t == 0)
        def _():
            dw_ref[...] = jnp.zeros_like(dw_ref)
            db_ref[...] = jnp.zeros_like(db_ref)

        x = x_ref[...]
        halo = jnp.where(t > 0, h_ref[...], 0.0)
        dy = dy_ref[...]
        dhalo = jnp.where(t < nt - 1, dh_ref[...], 0.0)
        wv = w_ref[...]
        dx = wv[CONV_WIDTH - 1:CONV_WIDTH] * dy
        rows = [jnp.sum(dy * x, axis=0, keepdims=True)]
        for j in range(1, CONV_WIDTH):
            dx = dx + wv[CONV_WIDTH - 1 - j:CONV_WIDTH - j] * _shift_up(dy, dhalo, j)
            rows.insert(0, jnp.sum(dy * _shift_down(x, halo, j), axis=0, keepdims=True))
        dx_ref[...] = dx.astype(dx_ref.dtype)
        dw_ref[...] += jnp.concatenate(rows, axis=0)
        db_ref[...] += jnp.sum(dy, axis=0, keepdims=True)

    return pl.pallas_call(
        body, name=name, grid=(width // cb, nt),
        in_specs=[pl.BlockSpec((tt, cb), lambda c, t: (t, c0 + c)),
                  pl.BlockSpec((HALO, cb), lambda c, t: (jnp.maximum(t * hb - 1, 0), c0 + c)),
                  pl.BlockSpec((CONV_WIDTH, cb), lambda c, t: (0, c)),
                  pl.BlockSpec((tt, cb), lambda c, t: (t, c)),
                  pl.BlockSpec((HALO, cb), lambda c, t: (jnp.minimum((t + 1) * hb, nt * hb - 1), c))],
        out_specs=[pl.BlockSpec((tt, cb), lambda c, t: (t, c)),
                   pl.BlockSpec((CONV_WIDTH, cb), lambda c, t: (0, c)),
                   pl.BlockSpec((1, cb), lambda c, t: (0, c))],
        out_shape=[jax.ShapeDtypeStruct((cfg.S, width), BF16), jax.ShapeDtypeStruct((CONV_WIDTH, width), F32),
                   jax.ShapeDtypeStruct((1, width), F32)],
        compiler_params=_cparams(2),
    )(src, src, w, dy, dy)


def _gla_core(gh, q, k, v, g, glr, wg, bg, wn, st):
    n = glr.shape[0]
    causal = _lower_tri(n)
    outs, new = [], []
    for h in range(gh):
        log_a = log_sigmoid(bdot(glr, wg[h], 1, 0) + bg[h]) * (1.0 / GATE_NORM)
        bcum = cumsum_rows(log_a)
        b_last, b_mid = _row(bcum, n - 1), _row(bcum, n // 2)
        qs = q[h] * (HEAD_K ** -0.5)
        scores = jnp.where(causal, bdot(qs * jnp.exp(bcum - b_mid), k[h] * jnp.exp(b_mid - bcum), 1, 1), 0.0)
        o = bdot(scores, v[h], 1, 0) + bdot(qs * jnp.exp(bcum), st[h], 1, 1)
        new.append(st[h] * jnp.exp(b_last) + bdot(v[h], k[h] * jnp.exp(b_last - bcum), 0, 0))
        outs.append(rms(o, wn) * silu(g[h]))
    return jnp.concatenate(outs, axis=1), new


def _gla_ins(cfg, p0, wg, bg, wn, tmap):
    gh = cfg.GH
    ins = []
    for h in range(gh):
        ins.append((p0, (CHUNK, HEAD_K), lambda g, t, h=h: (tmap(t), h)))
    for h in range(gh):
        ins.append((p0, (CHUNK, HEAD_K), lambda g, t, h=h: (tmap(t), gh + h)))
    for h in range(gh):
        ins.append((p0, (CHUNK, HEAD_V), lambda g, t, h=h: (tmap(t), gh + h)))
    for h in range(gh):
        ins.append((p0, (CHUNK, HEAD_V), lambda g, t, h=h: (tmap(t), 2 * gh + h)))
    ins.append((p0, (CHUNK, LANES), lambda g, t: (tmap(t), 10 * gh)))
    for h in range(gh):
        ins.append((wg, (LANES, HEAD_K), lambda g, t, h=h: (0, h)))
    for h in range(gh):
        ins.append((bg, (1, HEAD_K), lambda g, t, h=h: (0, h)))
    ins.append((wn, (1, HEAD_V), lambda g, t: (0, 0)))
    return ins


def _gla_unpack(gh, vals):
    q, k, v, g = (vals[i * gh:(i + 1) * gh] for i in range(4))
    glr = vals[4 * gh]
    wg = vals[4 * gh + 1:5 * gh + 1]
    bg = vals[5 * gh + 1:6 * gh + 1]
    wn = vals[6 * gh + 1]
    return q, k, v, g, glr, wg, bg, wn, vals[6 * gh + 2:]


def gla_fwd(cfg, p0, wg, bg, wn, name):
    gh, nc = cfg.GH, cfg.S // CHUNK

    def fn(ins, st):
        q, k, v, g, glr, wgv, bgv, wnv, _ = _gla_unpack(gh, ins)
        out, new = _gla_core(gh, q, k, v, g, glr, wgv, bgv, wnv, st)
        return [out, jnp.stack(st)], [], new

    return seq_call(name, fn, (1, nc), _gla_ins(cfg, p0, wg, bg, wn, lambda t: t),
                    [((cfg.S, cfg.DV), BF16, (CHUNK, cfg.DV), lambda g, t: (t, 0)),
                     ((nc, gh, HEAD_V, HEAD_K), F32, (None, gh, HEAD_V, HEAD_K), lambda g, t: (t, 0, 0, 0))],
                    carries=[(HEAD_V, HEAD_K)] * gh)


def gla_bwd(cfg, p0, wg, bg, wn, states, dout, name):
    gh, nc = cfg.GH, cfg.S // CHUNK
    rev = lambda t: nc - 1 - t

    def fn(ins, dst):
        q, k, v, g, glr, wgv, bgv, wnv, rest = _gla_unpack(gh, ins)
        st_all, do = rest
        st = [st_all[h] for h in range(gh)]
        _, vjp = jax.vjp(functools.partial(_gla_core, gh), q, k, v, g, glr, wgv, bgv, wnv, st)
        dq, dk, dv, dg, dglr, dwg, dbg, dwn, dstate = vjp((do.astype(F32), list(dst)))
        return ([jnp.concatenate(list(dq) + list(dk) + list(dv) + list(dg), axis=1), dglr],
                [jnp.concatenate(dwg, axis=1), jnp.concatenate(dbg, axis=1), dwn], dstate)

    ins = _gla_ins(cfg, p0, wg, bg, wn, rev)
    ins.append((states, (None, gh, HEAD_V, HEAD_K), lambda g, t: (rev(t), 0, 0, 0)))
    ins.append((dout, (CHUNK, cfg.DV), lambda g, t: (rev(t), 0)))
    wide = 2 * cfg.DK + 2 * cfg.DV
    fixed = lambda g, t: (0, 0)
    return seq_call(name, fn, (1, nc), ins,
                    [((cfg.S, wide), BF16, (CHUNK, wide), lambda g, t: (rev(t), 0)),
                     ((cfg.S, LANES), BF16, (CHUNK, LANES), lambda g, t: (rev(t), 0))],
                    accs=[((LANES, cfg.DK), (LANES, cfg.DK), fixed), ((1, cfg.DK), (1, cfg.DK), fixed),
                          ((1, HEAD_V), (1, HEAD_V), fixed)],
                    carries=[(HEAD_V, HEAD_K)] * gh)


def _lru_core(xc, gate, wa, wi, ba, bi, lam, h_in):
    r = jax.nn.sigmoid(bdot(xc, wa, 1, 0) + ba)
    i = jax.nn.sigmoid(bdot(xc, wi, 1, 0) + bi)
    log_a = LRU_C * r * log_sigmoid(lam)
    a = jnp.exp(log_a)
    u = jnp.sqrt(-_expm1(2.0 * log_a)) * (i * xc)
    first = _rows_iota(a.shape) == 0
    h = lin_scan(a, u + jnp.where(first, a * h_in, 0.0))
    return h * gelu_tanh(gate), _row(h, a.shape[0] - 1)


def _lru_tile(cfg):
    return min(cfg.S, 512)


def _lru_ins(cfg, xc, p0, wa, wi, ba, bi, lam, tmap):
    tt, gh = _lru_tile(cfg), cfg.GH
    vec = lambda g, t: (0, g)
    return [(xc, (tt, LRU_BLOCK), lambda g, t: (tmap(t), g)),
            (p0, (tt, LRU_BLOCK), lambda g, t: (tmap(t), 8 * gh + g)),
            (wa, (None, LRU_BLOCK, LRU_BLOCK), lambda g, t: (g, 0, 0)),
            (wi, (None, LRU_BLOCK, LRU_BLOCK), lambda g, t: (g, 0, 0)),
            (ba, (1, LRU_BLOCK), vec), (bi, (1, LRU_BLOCK), vec), (lam, (1, LRU_BLOCK), vec)]


def lru_fwd(cfg, xc, p0, wa, wi, ba, bi, lam, name):
    tt, nb = _lru_tile(cfg), cfg.NB
    nt = cfg.S // tt

    def fn(ins, c):
        out, h_last = _lru_core(*ins, c[0])
        return [out, c[0]], [], [h_last]

    return seq_call(name, fn, (nb, nt), _lru_ins(cfg, xc, p0, wa, wi, ba, bi, lam, lambda t: t),
                    [((cfg.S, cfg.W), BF16, (tt, LRU_BLOCK), lambda g, t: (t, g)),
                     ((nb, nt, 1, LRU_BLOCK), F32, (None, None, 1, LRU_BLOCK), lambda g, t: (g, t, 0, 0))],
                    carries=[(1, LRU_BLOCK)])


def lru_bwd(cfg, xc, p0, wa, wi, ba, bi, lam, states, dout, name):
    tt, nb = _lru_tile(cfg), cfg.NB
    nt = cfg.S // tt
    rev = lambda t: nt - 1 - t

    def fn(ins, c):
        *fwd_ins, h_in, do = ins
        _, vjp = jax.vjp(_lru_core, *fwd_ins, h_in)
        dxc, dgate, dwa, dwi, dba, dbi, dlam, dh = vjp((do.astype(F32), c[0]))
        return [dxc, dgate], [dwa, dwi, dba, dbi, dlam], [dh]

    ins = _lru_ins(cfg, xc, p0, wa, wi, ba, bi, lam, rev)
    ins.append((states, (None, None, 1, LRU_BLOCK), lambda g, t: (g, rev(t), 0, 0)))
    ins.append((dout, (tt, LRU_BLOCK), lambda g, t: (rev(t), g)))
    mat = ((nb, LRU_BLOCK, LRU_BLOCK), (None, LRU_BLOCK, LRU_BLOCK), lambda g, t: (g, 0, 0))
    vec = ((1, cfg.W), (1, LRU_BLOCK), lambda g, t: (0, g))
    return seq_call(name, fn, (nb, nt), ins,
                    [((cfg.S, cfg.W), F32, (tt, LRU_BLOCK), lambda g, t: (rev(t), g)),
                     ((cfg.S, cfg.W), BF16, (tt, LRU_BLOCK), lambda g, t: (rev(t), g))],
                    accs=[mat, mat, vec, vec, vec], carries=[(1, LRU_BLOCK)])


def _ssd_core(xc, bc, cc, z, dt_raw, dt_bias, a_log, d_skip, gn, st):
    n = xc.shape[0]
    x, bm, cm = silu(xc), silu(bc), silu(cc)
    dt = softplus(dt_raw + dt_bias)
    acs = cumsum_rows(dt * (-jnp.exp(a_log)))
    acs_t = acs.T
    acs_e, dt_e = _expand_heads(acs), _expand_heads(dt)
    last_e = _expand_heads(_row(acs, n - 1))
    causal = _lower_tri(n)
    cb = bdot(cm, bm, 1, 1)
    xdt = x * dt_e
    y_diag = []
    for h, xh in enumerate(_split_heads(xdt)):
        seg = acs[:, h:h + 1] - acs_t[h:h + 1, :]
        decay = jnp.where(causal, jnp.exp(jnp.minimum(seg, 0.0)), 0.0)
        y_diag.append(bdot(cb * decay, xh, 1, 0))
    y = jnp.concatenate(y_diag, axis=1) + bdot(cm, st, 1, 0) * jnp.exp(acs_e)
    new = st * jnp.exp(last_e) + bdot(bm, xdt * jnp.exp(last_e - acs_e), 0, 0)
    y = (y + _expand_heads(d_skip) * x) * silu(z)
    return rms(y, gn), new


def _ssd_ins(cfg, xc, p1, dt_raw, dt_bias, a_log, d_skip, gn, tmap):
    ng = cfg.NG
    vec = lambda g, t: (g, 0, 0)
    return [(xc, (CHUNK, SSD_GW), lambda g, t: (tmap(t), g)),
            (xc, (CHUNK, SSD_N), lambda g, t: (tmap(t), 4 * ng + g)),
            (xc, (CHUNK, SSD_N), lambda g, t: (tmap(t), 5 * ng + g)),
            (p1, (CHUNK, SSD_GW), lambda g, t: (tmap(t), g)),
            (dt_raw, (None, CHUNK, LANES), lambda g, t: (g, tmap(t), 0)),
            (dt_bias, (None, 1, LANES), vec), (a_log, (None, 1, LANES), vec), (d_skip, (None, 1, LANES), vec),
            (gn, (1, SSD_GW), lambda g, t: (0, g))]


def ssd_fwd(cfg, xc, p1, dt_raw, dt_bias, a_log, d_skip, gn, name):
    ng, nc = cfg.NG, cfg.S // CHUNK

    def fn(ins, c):
        out, new = _ssd_core(*ins, c[0])
        return [out, c[0]], [], [new]

    return seq_call(name, fn, (ng, nc), _ssd_ins(cfg, xc, p1, dt_raw, dt_bias, a_log, d_skip, gn, lambda t: t),
                    [((cfg.S, cfg.DI), BF16, (CHUNK, SSD_GW), lambda g, t: (t, g)),
                     ((ng, nc, SSD_N, SSD_GW), F32, (None, None, SSD_N, SSD_GW), lambda g, t: (g, t, 0, 0))],
                    carries=[(SSD_N, SSD_GW)])


def ssd_bwd(cfg, xc, p1, dt_raw, dt_bias, a_log, d_skip, gn, states, dout, name):
    ng, nc = cfg.NG, cfg.S // CHUNK
    rev = lambda t: nc - 1 - t

    def fn(ins, c):
        *fwd_ins, st, do = ins
        _, vjp = jax.vjp(_ssd_core, *fwd_ins, st)
        dxc, dbc, dcc, dz, ddt, dbias, dalog, dskip, dgn, dst = vjp((do.astype(F32), c[0]))
        return [dxc, dbc, dcc, dz, ddt], [dbias, dalog, dskip, dgn], [dst]

    ins = _ssd_ins(cfg, xc, p1, dt_raw, dt_bias, a_log, d_skip, gn, rev)
    ins.append((states, (None, None, SSD_N, SSD_GW), lambda g, t: (g, rev(t), 0, 0)))
    ins.append((dout, (CHUNK, SSD_GW), lambda g, t: (rev(t), g)))
    col = lambda g, t: (rev(t), g)
    vec = ((ng, 1, LANES), (None, 1, LANES), lambda g, t: (g, 0, 0))
    return seq_call(name, fn, (ng, nc), ins,
                    [((cfg.S, cfg.DI), F32, (CHUNK, SSD_GW), col),
                     ((cfg.S, ng * SSD_N), F32, (CHUNK, SSD_N), col),
                     ((cfg.S, ng * SSD_N), F32, (CHUNK, SSD_N), col),
                     ((cfg.S, cfg.DI), BF16, (CHUNK, SSD_GW), col),
                     ((ng, cfg.S, LANES), F32, (None, CHUNK, LANES), lambda g, t: (g, rev(t), 0))],
                    accs=[vec, vec, vec, ((1, cfg.DI), (1, SSD_GW), lambda g, t: (0, g))],
                    carries=[(SSD_N, SSD_GW)])


PACK_ALIGN = 16 * LANES


def pack(arrays, dtype):
    pieces = []
    for a in arrays:
        flat = a.reshape(-1).astype(dtype)
        pad = _round_up(flat.shape[0], PACK_ALIGN) - flat.shape[0]
        pieces.append(jnp.pad(flat, (0, pad)) if pad else flat)
    return jnp.concatenate(pieces).reshape(-1, LANES)


def pack_slots(arrays, dtype):
    pieces = []
    for a in arrays:
        flat = a.reshape(N_DEV, -1).astype(dtype)
        pad = _round_up(flat.shape[1], PACK_ALIGN) - flat.shape[1]
        pieces.append(jnp.pad(flat, ((0, 0), (0, pad))) if pad else flat)
    return jnp.concatenate(pieces, axis=1).reshape(N_DEV, -1, LANES)


def unpack(buf, shapes):
    lead = buf.shape[:-2]
    flat = buf.reshape(lead + (-1,))
    out, off = [], 0
    for s in shapes:
        n = math.prod(s)
        out.append(flat[..., off:off + n].reshape(lead + tuple(s)))
        off += _round_up(n, PACK_ALIGN)
    return out


def _cols_to_slots(full):
    r = full.shape[0]
    return full.reshape(r, N_DEV, -1).transpose(1, 0, 2)


def _slots_to_cols(slots):
    return slots.transpose(1, 0, 2).reshape(slots.shape[1], -1)


def _even_in_padded(cfg, w):
    main = 2 * cfg.DK + 2 * cfg.DV
    return jnp.concatenate([w[:, :main], w[:, main + GATE_RANK:], w[:, main:main + GATE_RANK],
                            jnp.zeros((w.shape[0], cfg.EP - cfg.EVEN_IN), w.dtype)], axis=1)


def _even_in_unpadded(cfg, wp):
    main = 2 * cfg.DK + 2 * cfg.DV
    rest = main + 2 * cfg.W
    return jnp.concatenate([wp[:, :main], wp[:, rest:rest + GATE_RANK], wp[:, main:rest]], axis=1)


def _odd_in_padded(cfg, w):
    return jnp.concatenate([w, jnp.zeros((w.shape[0], cfg.OP - cfg.ODD_IN), w.dtype)], axis=1)


def _group_lanes(cfg, v):
    lead = v.shape[:-1]
    g = jnp.moveaxis(v.reshape(lead + (cfg.NG, SSD_HG)), -2, 0)
    return jnp.pad(g, [(0, 0)] * (g.ndim - 1) + [(0, LANES - SSD_HG)])


def _ungroup_lanes(cfg, g):
    v = jnp.moveaxis(g[..., :SSD_HG], 0, -2)
    return v.reshape(v.shape[:-2] + (cfg.NH,))


def train_step(cfg, p, loss_target):
    S, D, DFF = cfg.S, cfg.D, cfg.DFF
    me = 4 * lax.axis_index("x") + 2 * lax.axis_index("y") + lax.axis_index("c")

    big = ["ev_w_in", "ev_w_out", "od_w_in", "od_w_out", "ffn_w_gate", "ffn_w_up", "ffn_w_down"]
    small_sharded = ["ev_gla_w_gate", "ev_lru_conv_w", "od_norm", "od_conv_w", "od_conv_b", "od_gnorm"]
    replicated = ["ev_norm", "ev_gla_b_gate", "ev_gla_w_onorm", "ev_lru_conv_b", "ev_lru_w_a", "ev_lru_b_a",
                  "ev_lru_w_i", "ev_lru_b_i", "ev_lru_lam", "od_dt_bias", "od_a_log", "od_d_skip", "ffn_norm",
                  "final_norm"]

    big_shapes = [p[n].shape for n in big]
    wg_all = exchange(pack([p[n] for n in big], BF16)[None], "gather_weights")
    gw = dict(zip(big, unpack(wg_all, big_shapes)))
    ss_shapes = [p[n].shape for n in small_sharded]
    ss_all = exchange(pack([p[n] for n in small_sharded], F32)[None], "gather_small")
    gs = dict(zip(small_sharded, unpack(ss_all, ss_shapes)))

    w_ev_in = _even_in_padded(cfg, _slots_to_cols(gw["ev_w_in"][:, 0]))
    w_ev_out = gw["ev_w_out"][:, 0].reshape(D, D)
    w_od_in = _odd_in_padded(cfg, _slots_to_cols(gw["od_w_in"][:, 0]))
    w_od_out = gw["od_w_out"][:, 0].reshape(cfg.DI, D)
    w_gate = [_slots_to_cols(gw["ffn_w_gate"][:, l]) for l in range(2)]
    w_up = [_slots_to_cols(gw["ffn_w_up"][:, l]) for l in range(2)]
    w_down = [gw["ffn_w_down"][:, l].reshape(DFF, D) for l in range(2)]

    gla_wg = jnp.pad(_slots_to_cols(gs["ev_gla_w_gate"][:, 0]), ((0, LANES - GATE_RANK), (0, 0)))
    lru_cw = _slots_to_cols(gs["ev_lru_conv_w"][:, 0])
    od_norm = gs["od_norm"].transpose(1, 0, 2).reshape(1, D)
    od_cw = _slots_to_cols(gs["od_conv_w"][:, 0])
    od_cb = gs["od_conv_b"].transpose(1, 0, 2).reshape(1, cfg.CD)
    od_gn = gs["od_gnorm"].transpose(1, 0, 2).reshape(1, cfg.DI)

    x0 = p["x"][0]
    target = loss_target[0]
    ev_norm = p["ev_norm"]
    bg = p["ev_gla_b_gate"]
    wn = p["ev_gla_w_onorm"]
    lru_cb = p["ev_lru_conv_b"]
    wa, wi = p["ev_lru_w_a"][0], p["ev_lru_w_i"][0]
    ba, bi, lam = p["ev_lru_b_a"], p["ev_lru_b_i"], p["ev_lru_lam"]
    dt_bias, a_log, d_skip = (_group_lanes(cfg, p[n]) for n in ("od_dt_bias", "od_a_log", "od_d_skip"))
    ffn_norm = [p["ffn_norm"][l:l + 1] for l in range(2)]
    final_norm = p["final_norm"].reshape(1, D)

    def ffn_forward(l, x):
        h = norm_fwd(cfg, x, ffn_norm[l], f"ffn{l}_norm")
        gate = matmul(h, w_gate[l], name=f"ffn{l}_gate")
        up = matmul(h, w_up[l], name=f"ffn{l}_up")
        act = act_fwd(cfg, gate, up, f"ffn{l}_act")
        return matmul(act, w_down[l], res=x, name=f"ffn{l}_down"), (h, gate, up, act)

    h0 = norm_fwd(cfg, x0, ev_norm, "ev_norm")
    p0 = matmul(h0, w_ev_in, name="ev_in", tn=768)
    gla_out, gla_states = gla_fwd(cfg, p0, gla_wg, bg, wn, "gla_fwd")
    lru_col = 2 * cfg.DK + 2 * cfg.DV
    lru_xc = conv_fwd(cfg, p0, lru_col, cfg.W, lru_cw, lru_cb, "lru_conv")
    lru_out, lru_states = lru_fwd(cfg, lru_xc, p0, wa, wi, ba, bi, lam, "lru_fwd")
    mix = jnp.concatenate([gla_out, lru_out], axis=1)
    x1 = matmul(mix, w_ev_out, res=x0, name="ev_out")
    x2, ffn0_saved = ffn_forward(0, x1)

    h2 = norm_fwd(cfg, x2, od_norm, "od_norm")
    p1 = matmul(h2, w_od_in, name="od_in", tn=768)
    od_xc = conv_fwd(cfg, p1, cfg.DI, cfg.CD, od_cw, od_cb, "od_conv")
    dt_col = cfg.DI + cfg.CD
    dt_raw = _group_lanes(cfg, p1[:, dt_col:dt_col + cfg.NH])
    ssd_out, ssd_states = ssd_fwd(cfg, od_xc, p1, dt_raw, dt_bias, a_log, d_skip, od_gn, "ssd_fwd")
    x3 = matmul(ssd_out, w_od_out, res=x2, name="od_out")
    x4, ffn1_saved = ffn_forward(1, x3)

    loss_part, dx4, d_final_norm = head_fwd_bwd(cfg, x4, final_norm, target, "head")
    loss = lax.psum(loss_part[0, 0], MESH_AXES)

    def ffn_backward(l, x, saved, dx_out):
        h, gate, up, act = saved
        dact = matmul(dx_out, w_down[l], tb=True, name=f"ffn{l}_dact")
        d_down = matmul(act, dx_out, ta=True, out_dtype=BF16, name=f"ffn{l}_dwdown")
        dgate, dup = act_bwd(cfg, gate, up, dact, f"ffn{l}_act_bwd")
        dh = matmul(dgate, w_gate[l], tb=True, name=f"ffn{l}_dh_gate")
        dh = matmul(dup, w_up[l], tb=True, res=dh, name=f"ffn{l}_dh_up")
        d_gate = matmul(h, dgate, ta=True, out_dtype=BF16, name=f"ffn{l}_dwgate")
        d_up = matmul(h, dup, ta=True, out_dtype=BF16, name=f"ffn{l}_dwup")
        dx, dnorm = norm_bwd(cfg, x, ffn_norm[l], dh, dx_out, f"ffn{l}_norm_bwd")
        return dx, dnorm, d_gate, d_up, d_down

    dx3, d_ffn_norm1, d_gate1, d_up1, d_down1 = ffn_backward(1, x3, ffn1_saved, dx4)

    d_ssd_out = matmul(dx3, w_od_out, tb=True, name="od_dmix")
    d_od_out = matmul(ssd_out, dx3, ta=True, out_dtype=BF16, name="od_dwout")
    dxs, dbm, dcm, dz, d_dt_raw, d_dt_bias, d_a_log, d_d_skip, d_od_gn = ssd_bwd(
        cfg, od_xc, p1, dt_raw, dt_bias, a_log, d_skip, od_gn, ssd_states, d_ssd_out, "ssd_bwd")
    d_od_xc = jnp.concatenate([dxs, dbm, dcm], axis=1)
    d_xbc, d_od_cw, d_od_cb = conv_bwd(cfg, p1, cfg.DI, cfg.CD, od_cw, d_od_xc, "od_conv_bwd")
    d_dt = _ungroup_lanes(cfg, d_dt_raw).astype(BF16)
    dp1 = jnp.concatenate([dz, d_xbc, d_dt, jnp.zeros((S, cfg.OP - cfg.ODD_IN), BF16)], axis=1)
    dh2 = matmul(dp1, w_od_in, tb=True, name="od_dh", tk=768)
    d_od_in = matmul(h2, dp1, ta=True, out_dtype=BF16, name="od_dwin", tn=768)
    dx2, d_od_norm = norm_bwd(cfg, x2, od_norm, dh2, dx3, "od_norm_bwd")

    dx1, d_ffn_norm0, d_gate0, d_up0, d_down0 = ffn_backward(0, x1, ffn0_saved, dx2)

    d_mix = matmul(dx1, w_ev_out, tb=True, name="ev_dmix")
    d_ev_out = matmul(mix, dx1, ta=True, out_dtype=BF16, name="ev_dwout")
    d_qkvg, d_glr, d_gla_wg, d_bg, d_wn = gla_bwd(cfg, p0, gla_wg, bg, wn, gla_states, d_mix[:, :cfg.DV], "gla_bwd")
    d_lru_xc, d_gate_br, d_wa, d_wi, d_ba, d_bi, d_lam = lru_bwd(
        cfg, lru_xc, p0, wa, wi, ba, bi, lam, lru_states, d_mix[:, cfg.DV:], "lru_bwd")
    d_xbr, d_lru_cw, d_lru_cb = conv_bwd(cfg, p0, lru_col, cfg.W, lru_cw, d_lru_xc, "lru_conv_bwd")
    dp0 = jnp.concatenate([d_qkvg, d_xbr, d_gate_br, d_glr, jnp.zeros((S, cfg.EP - lru_col - 2 * cfg.W - LANES), BF16)],
                          axis=1)
    dh0 = matmul(dp0, w_ev_in, tb=True, name="ev_dh", tk=768)
    d_ev_in = matmul(h0, dp0, ta=True, out_dtype=BF16, name="ev_dwin", tn=768)
    grad_x, d_ev_norm = norm_bwd(cfg, x0, ev_norm, dh0, dx1, "ev_norm_bwd")

    d_ev_in = _even_in_unpadded(cfg, d_ev_in)
    d_od_in = d_od_in[:, :cfg.ODD_IN]
    rows_split = lambda a: a.reshape((N_DEV, a.shape[0] // N_DEV) + a.shape[1:])
    big_grad_slots = {
        "ev_w_in": _cols_to_slots(d_ev_in)[:, None], "ev_w_out": rows_split(d_ev_out)[:, None],
        "od_w_in": _cols_to_slots(d_od_in)[:, None], "od_w_out": rows_split(d_od_out)[:, None],
        "ffn_w_gate": jnp.stack([_cols_to_slots(d_gate0), _cols_to_slots(d_gate1)], axis=1),
        "ffn_w_up": jnp.stack([_cols_to_slots(d_up0), _cols_to_slots(d_up1)], axis=1),
        "ffn_w_down": jnp.stack([rows_split(d_down0), rows_split(d_down1)], axis=1),
    }
    recv = exchange(pack_slots([big_grad_slots[n] for n in big], BF16), "exchange_grads")
    g_big = reduce_slots(recv, "sum_grads")

    small_full = {
        "ev_gla_w_gate": d_gla_wg[:GATE_RANK][None], "ev_lru_conv_w": d_lru_cw[None], "od_norm": d_od_norm,
        "od_conv_w": d_od_cw[None], "od_conv_b": d_od_cb, "od_gnorm": d_od_gn,
        "ev_norm": d_ev_norm, "ev_gla_b_gate": d_bg, "ev_gla_w_onorm": d_wn, "ev_lru_conv_b": d_lru_cb,
        "ev_lru_w_a": d_wa[None], "ev_lru_b_a": d_ba, "ev_lru_w_i": d_wi[None], "ev_lru_b_i": d_bi,
        "ev_lru_lam": d_lam, "od_dt_bias": _ungroup_lanes(cfg, d_dt_bias), "od_a_log": _ungroup_lanes(cfg, d_a_log),
        "od_d_skip": _ungroup_lanes(cfg, d_d_skip), "ffn_norm": jnp.concatenate([d_ffn_norm0, d_ffn_norm1], axis=0),
        "final_norm": d_final_norm.reshape(D),
    }
    small_names = small_sharded + replicated
    small_all = exchange(pack([small_full[n] for n in small_names], F32)[None], "gather_small_grads")
    small_sum = unpack(reduce_slots(small_all, "sum_small_grads"), [small_full[n].shape for n in small_names])
    g_small = dict(zip(small_names, small_sum))
    for n in small_sharded:
        width = p[n].shape[-1]
        g_small[n] = lax.dynamic_slice_in_dim(g_small[n], me * width, width, axis=g_small[n].ndim - 1)

    out = {"loss": loss, "grad_x": grad_x[None]}
    for names, grads in ((big, unpack(g_big, big_shapes)), (small_names, [g_small[n] for n in small_names])):
        shapes = [p[n].shape for n in names]
        g_buf = pack(grads, F32)
        delta, new_m, new_v = adamw(pack([p[n] for n in names], F32), g_buf, pack([p["m_" + n] for n in names], F32),
                                    pack([p["v_" + n] for n in names], F32), "adamw_" + names[0])
        for kind, buf in (("grad_", g_buf), ("delta_", delta), ("new_m_", new_m), ("new_v_", new_v)):
            for n, a in zip(names, unpack(buf, shapes)):
                out[kind + n] = a
    return out


WEIGHTS = ['ev_norm', 'ev_w_in', 'ev_gla_w_gate', 'ev_gla_b_gate', 'ev_gla_w_onorm', 'ev_lru_conv_w', 'ev_lru_conv_b',
           'ev_lru_w_a', 'ev_lru_b_a', 'ev_lru_w_i', 'ev_lru_b_i', 'ev_lru_lam', 'ev_w_out', 'od_norm', 'od_w_in',
           'od_conv_w', 'od_conv_b', 'od_dt_bias', 'od_a_log', 'od_d_skip', 'od_gnorm', 'od_w_out', 'ffn_norm',
           'ffn_w_gate', 'ffn_w_up', 'ffn_w_down', 'final_norm']


def kernel(x, ev_norm, ev_w_in, ev_gla_w_gate, ev_gla_b_gate, ev_gla_w_onorm, ev_lru_conv_w, ev_lru_conv_b, ev_lru_w_a, ev_lru_b_a, ev_lru_w_i, ev_lru_b_i, ev_lru_lam, ev_w_out, od_norm, od_w_in, od_conv_w, od_conv_b, od_dt_bias, od_a_log, od_d_skip, od_gnorm, od_w_out, ffn_norm, ffn_w_gate, ffn_w_up, ffn_w_down, final_norm, loss_target, m_ev_norm, m_ev_w_in, m_ev_gla_w_gate, m_ev_gla_b_gate, m_ev_gla_w_onorm, m_ev_lru_conv_w, m_ev_lru_conv_b, m_ev_lru_w_a, m_ev_lru_b_a, m_ev_lru_w_i, m_ev_lru_b_i, m_ev_lru_lam, m_ev_w_out, m_od_norm, m_od_w_in, m_od_conv_w, m_od_conv_b, m_od_dt_bias, m_od_a_log, m_od_d_skip, m_od_gnorm, m_od_w_out, m_ffn_norm, m_ffn_w_gate, m_ffn_w_up, m_ffn_w_down, m_final_norm, v_ev_norm, v_ev_w_in, v_ev_gla_w_gate, v_ev_gla_b_gate, v_ev_gla_w_onorm, v_ev_lru_conv_w, v_ev_lru_conv_b, v_ev_lru_w_a, v_ev_lru_b_a, v_ev_lru_w_i, v_ev_lru_b_i, v_ev_lru_lam, v_ev_w_out, v_od_norm, v_od_w_in, v_od_conv_w, v_od_conv_b, v_od_dt_bias, v_od_a_log, v_od_d_skip, v_od_gnorm, v_od_w_out, v_ffn_norm, v_ffn_w_gate, v_ffn_w_up, v_ffn_w_down, v_final_norm):
    args = dict(locals())
    p = {n: a for n, a in args.items() if n != "loss_target"}
    cfg = Cfg(S=x.shape[1], D=x.shape[2], DFF=ffn_w_gate.shape[2] * N_DEV)
    out = train_step(cfg, p, loss_target)
    return (out["loss"], out["grad_x"], *[out["grad_" + w] for w in WEIGHTS], *[out["delta_" + w] for w in WEIGHTS],
            *[out["new_m_" + w] for w in WEIGHTS], *[out["new_v_" + w] for w in WEIGHTS])
```

```python
import functools
import math
from typing import NamedTuple

import jax
import jax.numpy as jnp
from jax import lax
from jax.experimental import pallas as pl
from jax.experimental.pallas import tpu as pltpu

F32 = jnp.float32
BF16 = jnp.bfloat16
MESH_AXES = ("x", "y", "c")
N_DEV = 8
LANES = 128
SUBLANES = 8
VMEM_LIMIT = 56 * 1024 * 1024

NORM_EPS = 1e-6
CONV_WIDTH = 4
CHUNK = 64
HEAD_K = 128
HEAD_V = 256
GATE_RANK = 16
GATE_NORM = 16.0
LRU_BLOCK = 128
LRU_C = 8.0
SSD_P = 64
SSD_N = 128
SSD_HG = 8
SSD_GW = SSD_HG * SSD_P

ADAM_LR = 0.001
ADAM_B1 = 0.9
ADAM_B2 = 0.999
ADAM_EPS = 1e-08
ADAM_WD = 0.01
ADAM_STEP = 10


class Cfg(NamedTuple):
    S: int
    D: int
    DFF: int

    @property
    def GH(self):
        return self.D // 512

    @property
    def NB(self):
        return self.D // 256

    @property
    def NG(self):
        return self.D // 256

    @property
    def DK(self):
        return HEAD_K * self.GH

    @property
    def DV(self):
        return HEAD_V * self.GH

    @property
    def W(self):
        return LRU_BLOCK * self.NB

    @property
    def DI(self):
        return SSD_GW * self.NG

    @property
    def CD(self):
        return self.DI + 2 * self.NG * SSD_N

    @property
    def NH(self):
        return SSD_HG * self.NG

    @property
    def EVEN_IN(self):
        return 2 * self.DK + 2 * self.DV + GATE_RANK + 2 * self.W

    @property
    def ODD_IN(self):
        return self.DI + self.CD + self.NH

    @property
    def EP(self):
        return _round_up(2 * self.DK + 2 * self.DV + 2 * self.W + LANES, 768)

    @property
    def OP(self):
        return _round_up(self.DI + self.CD + LANES, 768)


def _round_up(n, m):
    return (n + m - 1) // m * m


def _tile(n, pref):
    if n <= pref:
        return n
    t = pref - pref % LANES
    while n % t:
        t -= LANES
    return t


def _cparams(n_axes):
    return pltpu.CompilerParams(dimension_semantics=("arbitrary",) * n_axes, vmem_limit_bytes=VMEM_LIMIT)


def _dg(a, b, ca, cb):
    return lax.dot_general(a.astype(BF16), b.astype(BF16), (((ca,), (cb,)), ((), ())), preferred_element_type=F32)


@functools.partial(jax.custom_vjp, nondiff_argnums=(2, 3))
def bdot(a, b, ca, cb):
    return _dg(a, b, ca, cb)


def _bdot_fwd(a, b, ca, cb):
    return _dg(a, b, ca, cb), (a, b)


def _bdot_bwd(ca, cb, res, g):
    a, b = res
    da = _dg(g, b, 1, 1 - cb) if ca == 1 else _dg(b, g, 1 - cb, 1)
    db = _dg(a, g, 1 - ca, 0) if cb == 0 else _dg(g, a, 0, 1 - ca)
    return da.astype(a.dtype), db.astype(b.dtype)


bdot.defvjp(_bdot_fwd, _bdot_bwd)


def _lower_tri(n):
    r = lax.broadcasted_iota(jnp.int32, (n, n), 0)
    c = lax.broadcasted_iota(jnp.int32, (n, n), 1)
    return c <= r


@jax.custom_vjp
def cumsum_rows(x):
    n = x.shape[0]
    return lax.dot_general(_lower_tri(n).astype(F32), x, (((1,), (0,)), ((), ())),
                           precision=lax.Precision.HIGHEST, preferred_element_type=F32)


def _cumsum_fwd(x):
    return cumsum_rows(x), None


def _cumsum_bwd(_, g):
    n = g.shape[0]
    return (lax.dot_general(_lower_tri(n).astype(F32), g, (((0,), (0,)), ((), ())),
                            precision=lax.Precision.HIGHEST, preferred_element_type=F32),)


cumsum_rows.defvjp(_cumsum_fwd, _cumsum_bwd)


def _row(x, i):
    r = lax.broadcasted_iota(jnp.int32, x.shape, 0)
    return jnp.sum(jnp.where(r == i, x, 0.0), axis=0, keepdims=True)


def _softplus_raw(x):
    return jnp.maximum(x, 0.0) + jnp.log(1.0 + jnp.exp(-jnp.abs(x)))


@jax.custom_vjp
def softplus(x):
    return _softplus_raw(x)


softplus.defvjp(lambda x: (_softplus_raw(x), x), lambda x, g: (g * jax.nn.sigmoid(x),))


@jax.custom_vjp
def log_sigmoid(x):
    return -_softplus_raw(-x)


log_sigmoid.defvjp(lambda x: (-_softplus_raw(-x), x), lambda x, g: (g * jax.nn.sigmoid(-x),))


def silu(x):
    return x * jax.nn.sigmoid(x)


def gelu_tanh(x):
    return 0.5 * x * (1.0 + jnp.tanh(math.sqrt(2.0 / math.pi) * (x + 0.044715 * (x * x * x))))


def _expm1(x):
    series = x * (1.0 + 0.5 * x * (1.0 + (1.0 / 3.0) * x))
    return jnp.where(jnp.abs(x) < 1e-2, series, jnp.exp(x) - 1.0)


def rms(x, w):
    return x * lax.rsqrt(jnp.mean(x * x, axis=-1, keepdims=True) + NORM_EPS) * w


def _rows_iota(shape):
    return lax.broadcasted_iota(jnp.int32, shape, 0)


def _scan_up(a, u):
    n = a.shape[0]
    r = _rows_iota(a.shape)
    d = 1
    while d < n:
        m = r >= d
        a_s = jnp.where(m, pltpu.roll(a, d, 0), 1.0)
        u_s = jnp.where(m, pltpu.roll(u, d, 0), 0.0)
        u = a * u_s + u
        a = a * a_s
        d *= 2
    return u


def _scan_down(a, u):
    n = a.shape[0]
    r = _rows_iota(a.shape)
    d = 1
    while d < n:
        m = r < n - d
        a_s = jnp.where(m, pltpu.roll(a, n - d, 0), 1.0)
        u_s = jnp.where(m, pltpu.roll(u, n - d, 0), 0.0)
        u = a * u_s + u
        a = a * a_s
        d *= 2
    return u


@jax.custom_vjp
def lin_scan(a, u):
    return _scan_up(a, u)


def _lin_scan_fwd(a, u):
    h = _scan_up(a, u)
    return h, (a, h)


def _lin_scan_bwd(res, g):
    a, h = res
    n = a.shape[0]
    r = _rows_iota(a.shape)
    a_next = jnp.where(r < n - 1, pltpu.roll(a, n - 1, 0), 0.0)
    gt = _scan_down(a_next, g)
    h_prev = jnp.where(r >= 1, pltpu.roll(h, 1, 0), 0.0)
    return gt * h_prev, gt


lin_scan.defvjp(_lin_scan_fwd, _lin_scan_bwd)


def _expand_heads(v):
    r = v.shape[0]
    return jnp.concatenate([jnp.broadcast_to(v[:, h:h + 1], (r, SSD_P)) for h in range(SSD_HG)], axis=1)


@jax.custom_vjp
def _split_heads(x):
    return tuple(x[:, h * SSD_P:(h + 1) * SSD_P] for h in range(SSD_HG))


_split_heads.defvjp(lambda x: (_split_heads(x), None), lambda _, gs: (jnp.concatenate(gs, axis=1),))


def matmul(a, b, *, ta=False, tb=False, a_slot=None, b_slot=None, b_lead=(), res=None, out_dtype=F32, name,
           tm=512, tn=1024, tk=2048):
    lead = tuple(b_lead)
    ra, ca_ = a.shape[-2:]
    rb, cb_ = b.shape[-2:]
    m_st, ka_st = (ca_, ra) if ta else (ra, ca_)
    kb_st, n_st = (cb_, rb) if tb else (rb, cb_)
    kslot = a_slot == "k"
    assert kslot == (b_slot == "k")
    m = m_st * (N_DEV if a_slot == "m" else 1)
    n = n_st * (N_DEV if b_slot == "n" else 1)
    assert ka_st == kb_st, (a.shape, b.shape, ta, tb)
    tm = m_st if a_slot == "m" else _tile(m, tm)
    tn = n_st if b_slot == "n" else _tile(n, tn)
    tk = ka_st if kslot else _tile(ka_st, tk)
    nk = ka_st // tk
    ca, cb = (0 if ta else 1), (1 if tb else 0)
    nl = (None,) * len(lead)

    if a_slot is None:
        a_spec = pl.BlockSpec((tk, tm), lambda i, j, k: (k, i)) if ta else pl.BlockSpec((tm, tk), lambda i, j, k: (i, k))
    elif a_slot == "m":
        a_spec = (pl.BlockSpec((None, tk, tm), lambda i, j, k: (i, k, 0)) if ta
                  else pl.BlockSpec((None, tm, tk), lambda i, j, k: (i, 0, k)))
    else:
        a_spec = (pl.BlockSpec((N_DEV, tk, tm), lambda i, j, k: (0, 0, i)) if ta
                  else pl.BlockSpec((N_DEV, tm, tk), lambda i, j, k: (0, i, 0)))
    if b_slot is None:
        b_spec = (pl.BlockSpec(nl + (tn, tk), lambda i, j, k: lead + (j, k)) if tb
                  else pl.BlockSpec(nl + (tk, tn), lambda i, j, k: lead + (k, j)))
    elif b_slot == "n":
        b_spec = (pl.BlockSpec((None,) + nl + (tn, tk), lambda i, j, k: (j,) + lead + (0, k)) if tb
                  else pl.BlockSpec((None,) + nl + (tk, tn), lambda i, j, k: (j,) + lead + (k, 0)))
    else:
        b_spec = (pl.BlockSpec((N_DEV,) + nl + (tn, tk), lambda i, j, k: (0,) + lead + (j, 0)) if tb
                  else pl.BlockSpec((N_DEV,) + nl + (tk, tn), lambda i, j, k: (0,) + lead + (0, j)))
    if a_slot == "m":
        o_spec, o_shape = pl.BlockSpec((None, tm, tn), lambda i, j, k: (i, 0, j)), (N_DEV, tm, n)
    elif b_slot == "n":
        o_spec, o_shape = pl.BlockSpec((None, tm, tn), lambda i, j, k: (j, i, 0)), (N_DEV, m, tn)
    else:
        o_spec, o_shape = pl.BlockSpec((tm, tn), lambda i, j, k: (i, j)), (m, n)
    assert res is None or (a_slot != "m" and b_slot != "n")

    def dot(x, y):
        return lax.dot_general(x.astype(BF16), y.astype(BF16), (((ca,), (cb,)), ((), ())), preferred_element_type=F32)

    def body(*refs):
        a_ref, b_ref = refs[:2]
        r_ref = refs[2] if res is not None else None
        o_ref = refs[3 if res is not None else 2]

        def finish(acc):
            if r_ref is not None:
                acc = acc + r_ref[...].astype(F32)
            o_ref[...] = acc.astype(o_ref.dtype)

        if kslot:
            acc = dot(a_ref[0], b_ref[0])
            for s in range(1, N_DEV):
                acc = acc + dot(a_ref[s], b_ref[s])
            finish(acc)
        elif nk == 1:
            finish(dot(a_ref[...], b_ref[...]))
        else:
            acc_ref = refs[-1]
            k = pl.program_id(2)

            @pl.when(k == 0)
            def _():
                acc_ref[...] = dot(a_ref[...], b_ref[...])

            @pl.when(k > 0)
            def _():
                acc_ref[...] += dot(a_ref[...], b_ref[...])

            @pl.when(k == nk - 1)
            def _():
                finish(acc_ref[...])

    in_specs = [a_spec, b_spec]
    args = [a, b]
    if res is not None:
        in_specs.append(pl.BlockSpec((tm, tn), lambda i, j, k: (i, j)))
        args.append(res)
    return pl.pallas_call(
        body, name=name, grid=(m // tm, n // tn, nk), in_specs=in_specs, out_specs=o_spec,
        out_shape=jax.ShapeDtypeStruct(o_shape, out_dtype),
        scratch_shapes=[pltpu.VMEM((tm, tn), F32)] if nk > 1 else [], compiler_params=_cparams(3),
    )(*args)


def seq_call(name, fn, grid, ins, outs, accs=(), carries=()):
    n_in, n_out, n_acc = len(ins), len(outs), len(accs)

    def body(*refs):
        in_refs = refs[:n_in]
        out_refs = refs[n_in:n_in + n_out]
        acc_refs = refs[n_in + n_out:n_in + n_out + n_acc]
        c_refs = refs[n_in + n_out + n_acc:]

        if acc_refs or c_refs:
            @pl.when(pl.program_id(1) == 0)
            def _():
                for r in tuple(acc_refs) + tuple(c_refs):
                    r[...] = jnp.zeros_like(r)

        o, a, c = fn([r[...] for r in in_refs], [r[...] for r in c_refs])
        for r, v in zip(out_refs, o, strict=True):
            r[...] = v.astype(r.dtype)
        for r, v in zip(acc_refs, a, strict=True):
            r[...] += v
        for r, v in zip(c_refs, c, strict=True):
            r[...] = v

    return pl.pallas_call(
        body, name=name, grid=grid,
        in_specs=[pl.BlockSpec(blk, im) for _, blk, im in ins],
        out_specs=[pl.BlockSpec(blk, im) for _, _, blk, im in outs] + [pl.BlockSpec(blk, im) for _, blk, im in accs],
        out_shape=[jax.ShapeDtypeStruct(s, d) for s, d, _, _ in outs] + [jax.ShapeDtypeStruct(s, F32) for s, _, _ in accs],
        scratch_shapes=[pltpu.VMEM(s, F32) for s in carries], compiler_params=_cparams(2),
    )(*[a for a, _, _ in ins])


def exchange(arrays, gather, name):
    n = len(arrays)

    def body(*refs):
        x_refs, o_refs = refs[:n], refs[n:2 * n]
        send_sems, recv_sems, local_sems = refs[2 * n:]
        pos = [lax.axis_index(ax) for ax in MESH_AXES]
        me = 4 * pos[0] + 2 * pos[1] + pos[2]
        copies = []
        for i in range(n):
            own = pltpu.make_async_copy(x_refs[i] if gather else x_refs[i].at[me], o_refs[i].at[me], local_sems.at[i])
            own.start()
            copies.append(own)
        for k in range(1, N_DEV):
            bits = ((k >> 2) & 1, (k >> 1) & 1, k & 1)
            peer = tuple(1 - p if b else p for p, b in zip(pos, bits))
            peer_id = 4 * peer[0] + 2 * peer[1] + peer[2]
            for i in range(n):
                cp = pltpu.make_async_remote_copy(
                    src_ref=x_refs[i] if gather else x_refs[i].at[peer_id], dst_ref=o_refs[i].at[me],
                    send_sem=send_sems.at[i * (N_DEV - 1) + k - 1], recv_sem=recv_sems.at[i * (N_DEV - 1) + k - 1],
                    device_id=peer, device_id_type=pl.DeviceIdType.MESH)
                cp.start()
                copies.append(cp)
        for cp in copies:
            cp.wait()

    hbm = pl.BlockSpec(memory_space=pltpu.HBM)
    return pl.pallas_call(
        body, name=name, in_specs=[hbm] * n, out_specs=[hbm] * n,
        out_shape=[jax.ShapeDtypeStruct(((N_DEV,) + a.shape) if gather else a.shape, a.dtype) for a in arrays],
        scratch_shapes=[pltpu.SemaphoreType.DMA((n * (N_DEV - 1),)), pltpu.SemaphoreType.DMA((n * (N_DEV - 1),)),
                        pltpu.SemaphoreType.DMA((n,))],
    )(*arrays)


def _adam_update(w, g, m, v):
    nm = ADAM_B1 * m + (1.0 - ADAM_B1) * g
    nv = ADAM_B2 * v + (1.0 - ADAM_B2) * (g * g)
    m_hat = nm / (1.0 - ADAM_B1 ** ADAM_STEP)
    v_hat = nv / (1.0 - ADAM_B2 ** ADAM_STEP)
    return -ADAM_LR * (m_hat / (jnp.sqrt(v_hat) + ADAM_EPS) + ADAM_WD * w), nm, nv


def _sum_slots(s_ref):
    acc = s_ref[0].astype(F32)
    for j in range(1, N_DEV):
        acc = acc + s_ref[j].astype(F32)
    return acc


ADAM_BLOCK_BYTES = 10 * 1024 * 1024


def adamw_sharded(recv, w, m, v, name):
    nl, r, c = w.shape
    per_row = _round_up(c, LANES) * (N_DEV * recv.dtype.itemsize + 7 * 4)
    tr = r
    while tr * per_row > ADAM_BLOCK_BYTES and tr % 16 == 0:
        tr //= 2

    def body(s_ref, w_ref, m_ref, v_ref, g_ref, d_ref, nm_ref, nv_ref):
        g = _sum_slots(s_ref)
        g_ref[...] = g
        d_ref[...], nm_ref[...], nv_ref[...] = _adam_update(w_ref[...], g, m_ref[...], v_ref[...])

    spec = pl.BlockSpec((None, tr, c), lambda l, i: (l, i, 0))
    shp = jax.ShapeDtypeStruct(w.shape, F32)
    return pl.pallas_call(
        body, name=name, grid=(nl, r // tr),
        in_specs=[pl.BlockSpec((N_DEV, None, tr, c), lambda l, i: (0, l, i, 0)), spec, spec, spec],
        out_specs=[spec] * 4, out_shape=[shp] * 4, compiler_params=_cparams(2))(recv, w, m, v)


def reduce_slots(slots, name):
    _, r, lanes = slots.shape
    tr = _tile(r, 2048)

    def body(s_ref, o_ref):
        o_ref[...] = _sum_slots(s_ref)

    return pl.pallas_call(
        body, name=name, grid=(r // tr,),
        in_specs=[pl.BlockSpec((N_DEV, tr, lanes), lambda i: (0, i, 0))],
        out_specs=pl.BlockSpec((tr, lanes), lambda i: (i, 0)),
        out_shape=jax.ShapeDtypeStruct((r, lanes), F32), compiler_params=_cparams(1),
    )(slots)


def adamw(w, g, m, v, name):
    r, lanes = w.shape
    tr = _tile(r, 2048)

    def body(w_ref, g_ref, m_ref, v_ref, d_ref, nm_ref, nv_ref):
        d_ref[...], nm_ref[...], nv_ref[...] = _adam_update(w_ref[...], g_ref[...], m_ref[...], v_ref[...])

    spec = pl.BlockSpec((tr, lanes), lambda i: (i, 0))
    shp = jax.ShapeDtypeStruct((r, lanes), F32)
    return pl.pallas_call(body, name=name, grid=(r // tr,), in_specs=[spec] * 4, out_specs=[spec] * 3,
                          out_shape=[shp] * 3, compiler_params=_cparams(1))(w, g, m, v)


def cols_from_slots(slots, place, width, name):
    _, _, rows, c = slots.shape
    tr = _tile(rows, 256)

    def body(s_ref, o_ref):
        o_ref[...] = place(jnp.concatenate([s_ref[j] for j in range(N_DEV)], axis=1))

    return pl.pallas_call(
        body, name=name, grid=(rows // tr,),
        in_specs=[pl.BlockSpec((N_DEV, None, tr, c), lambda i: (0, 0, i, 0))],
        out_specs=pl.BlockSpec((tr, width), lambda i: (i, 0)),
        out_shape=jax.ShapeDtypeStruct((rows, width), slots.dtype), compiler_params=_cparams(1))(slots)


def slots_from_cols(full, pick, c, name):
    rows, wide = full.shape
    tr = _tile(rows, 256)

    def body(x_ref, o_ref):
        v = pick(x_ref[...])
        for j in range(N_DEV):
            o_ref[j] = v[:, j * c:(j + 1) * c]

    return pl.pallas_call(
        body, name=name, grid=(rows // tr,),
        in_specs=[pl.BlockSpec((tr, wide), lambda i: (i, 0))],
        out_specs=pl.BlockSpec((N_DEV, None, tr, c), lambda i: (0, 0, i, 0)),
        out_shape=jax.ShapeDtypeStruct((N_DEV, 1, rows, c), full.dtype), compiler_params=_cparams(1))(full)


def _token_tile(cfg):
    return min(cfg.S, 256)


def norm_fwd(cfg, x, w, name):
    ts = _token_tile(cfg)
    d = x.shape[1]

    def fn(ins, _):
        xv, wv = ins
        return [rms(xv, wv)], [], []

    return seq_call(name, fn, (1, cfg.S // ts),
                    [(x, (ts, d), lambda g, t: (t, 0)), (w, (1, d), lambda g, t: (0, 0))],
                    [((cfg.S, d), BF16, (ts, d), lambda g, t: (t, 0))])[0]


def norm_bwd(cfg, x, w, dh, dres, name):
    ts = _token_tile(cfg)
    d = x.shape[1]

    def fn(ins, _):
        xv, wv, dhv, drv = ins
        _, vjp = jax.vjp(rms, xv, wv)
        dx, dw = vjp(dhv.astype(F32))
        return [dx + drv], [dw], []

    row = lambda g, t: (t, 0)
    out = seq_call(name, fn, (1, cfg.S // ts),
                   [(x, (ts, d), row), (w, (1, d), lambda g, t: (0, 0)), (dh, (ts, d), row), (dres, (ts, d), row)],
                   [((cfg.S, d), F32, (ts, d), row)],
                   accs=[((1, d), (1, d), lambda g, t: (0, 0))])
    return out[0], out[1]


def head_fwd_bwd(cfg, x, w, target, name):
    ts = _token_tile(cfg)
    d = x.shape[1]

    def fn(ins, _):
        xv, wv, tv = ins
        y, vjp = jax.vjp(rms, xv, wv)
        err = y - tv
        loss = 0.5 * jnp.sum(err * err) / d
        dx, dw = vjp(err / d)
        return [dx], [jnp.full((SUBLANES, LANES), loss, F32), dw], []

    row = lambda g, t: (t, 0)
    fixed = lambda g, t: (0, 0)
    dx, loss, dw = seq_call(name, fn, (1, cfg.S // ts),
                            [(x, (ts, d), row), (w, (1, d), fixed), (target, (ts, d), row)],
                            [((cfg.S, d), F32, (ts, d), row)],
                            accs=[((SUBLANES, LANES), (SUBLANES, LANES), fixed), ((1, d), (1, d), fixed)])
    return loss, dx, dw


def act_fwd(cfg, gate, up, name):
    ts, c = min(cfg.S, 512), gate.shape[2]

    def fn(ins, _):
        return [silu(ins[0]) * ins[1]], [], []

    blk = lambda g, t: (g, t, 0)
    return seq_call(name, fn, (N_DEV, cfg.S // ts),
                    [(gate, (None, ts, c), blk), (up, (None, ts, c), blk)],
                    [(gate.shape, BF16, (None, ts, c), blk)])[0]


def act_bwd(cfg, gate, up, dact, name):
    ts, c = min(cfg.S, 512), gate.shape[2]

    def fn(ins, _):
        gv, uv, dv = ins
        _, vjp = jax.vjp(lambda a, b: silu(a) * b, gv, uv)
        dg, du = vjp(dv.astype(F32))
        return [dg, du], [], []

    blk = lambda g, t: (g, t, 0)
    shp = (gate.shape, BF16, (None, ts, c), blk)
    return seq_call(name, fn, (N_DEV, cfg.S // ts),
                    [(gate, (None, ts, c), blk), (up, (None, ts, c), blk), (dact, (None, ts, c), blk)], [shp, shp])


CONV_COLS = 256
HALO = SUBLANES


def _shift_down(x, halo, j):
    if j == 0:
        return x
    r8 = _rows_iota(halo.shape)
    top = jnp.where(r8 >= j, pltpu.roll(x[:HALO], j, 0), pltpu.roll(halo, j, 0))
    return jnp.concatenate([top, pltpu.roll(x, j, 0)[HALO:]], axis=0)


def _shift_up(x, halo, j):
    if j == 0:
        return x
    n = x.shape[0]
    r8 = _rows_iota(halo.shape)
    bot = jnp.where(r8 < HALO - j, pltpu.roll(x[n - HALO:], HALO - j, 0), pltpu.roll(halo, HALO - j, 0))
    return jnp.concatenate([pltpu.roll(x, n - j, 0)[:n - HALO], bot], axis=0)


def _conv_tile(cfg):
    return min(cfg.S, 512)


def conv_fwd(cfg, src, col0, width, w, b, name):
    tt, cb = _conv_tile(cfg), CONV_COLS
    c0, hb = col0 // cb, tt // HALO
    nt = cfg.S // tt

    def body(x_ref, h_ref, w_ref, b_ref, o_ref):
        t = pl.program_id(1)
        x = x_ref[...]
        halo = jnp.where(t > 0, h_ref[...], 0.0)
        wv = w_ref[...]
        acc = b_ref[...] + wv[CONV_WIDTH - 1:CONV_WIDTH] * x
        for j in range(1, CONV_WIDTH):
            acc = acc + wv[CONV_WIDTH - 1 - j:CONV_WIDTH - j] * _shift_down(x, halo, j)
        o_ref[...] = acc

    return pl.pallas_call(
        body, name=name, grid=(width // cb, nt),
        in_specs=[pl.BlockSpec((tt, cb), lambda c, t: (t, c0 + c)),
                  pl.BlockSpec((HALO, cb), lambda c, t: (jnp.maximum(t * hb - 1, 0), c0 + c)),
                  pl.BlockSpec((CONV_WIDTH, cb), lambda c, t: (0, c)),
                  pl.BlockSpec((1, cb), lambda c, t: (0, c))],
        out_specs=pl.BlockSpec((tt, cb), lambda c, t: (t, c)),
        out_shape=jax.ShapeDtypeStruct((cfg.S, width), F32), compiler_params=_cparams(2),
    )(src, src, w, b)


def conv_bwd(cfg, src, col0, width, w, dy, name):
    tt, cb = _conv_tile(cfg), CONV_COLS
    c0, hb = col0 // cb, tt // HALO
    nt = cfg.S // tt

    def body(x_ref, h_ref, w_ref, dy_ref, dh_ref, dx_ref, dw_ref, db_ref):
        t = pl.program_id(1)

        @pl.when(t == 0)
        def _():
            dw_ref[...] = jnp.zeros_like(dw_ref)
            db_ref[...] = jnp.zeros_like(db_ref)

        x = x_ref[...]
        halo = jnp.where(t > 0, h_ref[...], 0.0)
        dy = dy_ref[...]
        dhalo = jnp.where(t < nt - 1, dh_ref[...], 0.0)
        wv = w_ref[...]
        dx = wv[CONV_WIDTH - 1:CONV_WIDTH] * dy
        rows = [jnp.sum(dy * x, axis=0, keepdims=True)]
        for j in range(1, CONV_WIDTH):
            dx = dx + wv[CONV_WIDTH - 1 - j:CONV_WIDTH - j] * _shift_up(dy, dhalo, j)
            rows.insert(0, jnp.sum(dy * _shift_down(x, halo, j), axis=0, keepdims=True))
        dx_ref[...] = dx.astype(dx_ref.dtype)
        dw_ref[...] += jnp.concatenate(rows, axis=0)
        db_ref[...] += jnp.sum(dy, axis=0, keepdims=True)

    return pl.pallas_call(
        body, name=name, grid=(width // cb, nt),
        in_specs=[pl.BlockSpec((tt, cb), lambda c, t: (t, c0 + c)),
                  pl.BlockSpec((HALO, cb), lambda c, t: (jnp.maximum(t * hb - 1, 0), c0 + c)),
                  pl.BlockSpec((CONV_WIDTH, cb), lambda c, t: (0, c)),
                  pl.BlockSpec((tt, cb), lambda c, t: (t, c)),
                  pl.BlockSpec((HALO, cb), lambda c, t: (jnp.minimum((t + 1) * hb, nt * hb - 1), c))],
        out_specs=[pl.BlockSpec((tt, cb), lambda c, t: (t, c)),
                   pl.BlockSpec((CONV_WIDTH, cb), lambda c, t: (0, c)),
                   pl.BlockSpec((1, cb), lambda c, t: (0, c))],
        out_shape=[jax.ShapeDtypeStruct((cfg.S, width), BF16), jax.ShapeDtypeStruct((CONV_WIDTH, width), F32),
                   jax.ShapeDtypeStruct((1, width), F32)],
        compiler_params=_cparams(2),
    )(src, src, w, dy, dy)


def _gla_core(gh, q, k, v, g, glr, wg, bg, wn, st):
    n = glr.shape[0]
    causal = _lower_tri(n)
    outs, new = [], []
    for h in range(gh):
        log_a = log_sigmoid(bdot(glr, wg[h], 1, 0) + bg[h]) * (1.0 / GATE_NORM)
        bcum = cumsum_rows(log_a)
        b_last, b_mid = _row(bcum, n - 1), _row(bcum, n // 2)
        qs = q[h] * (HEAD_K ** -0.5)
        scores = jnp.where(causal, bdot(qs * jnp.exp(bcum - b_mid), k[h] * jnp.exp(b_mid - bcum), 1, 1), 0.0)
        o = bdot(scores, v[h], 1, 0) + bdot(qs * jnp.exp(bcum), st[h], 1, 1)
        new.append(st[h] * jnp.exp(b_last) + bdot(v[h], k[h] * jnp.exp(b_last - bcum), 0, 0))
        outs.append(rms(o, wn) * silu(g[h]))
    return jnp.concatenate(outs, axis=1), new


def _gla_ins(cfg, p0, wg, bg, wn, tmap):
    gh = cfg.GH
    ins = []
    for h in range(gh):
        ins.append((p0, (CHUNK, HEAD_K), lambda g, t, h=h: (tmap(t), h)))
    for h in range(gh):
        ins.append((p0, (CHUNK, HEAD_K), lambda g, t, h=h: (tmap(t), gh + h)))
    for h in range(gh):
        ins.append((p0, (CHUNK, HEAD_V), lambda g, t, h=h: (tmap(t), gh + h)))
    for h in range(gh):
        ins.append((p0, (CHUNK, HEAD_V), lambda g, t, h=h: (tmap(t), 2 * gh + h)))
    ins.append((p0, (CHUNK, LANES), lambda g, t: (tmap(t), 10 * gh)))
    for h in range(gh):
        ins.append((wg, (LANES, HEAD_K), lambda g, t, h=h: (0, h)))
    for h in range(gh):
        ins.append((bg, (1, HEAD_K), lambda g, t, h=h: (0, h)))
    ins.append((wn, (1, HEAD_V), lambda g, t: (0, 0)))
    return ins


def _gla_unpack(gh, vals):
    q, k, v, g = (vals[i * gh:(i + 1) * gh] for i in range(4))
    glr = vals[4 * gh]
    wg = vals[4 * gh + 1:5 * gh + 1]
    bg = vals[5 * gh + 1:6 * gh + 1]
    wn = vals[6 * gh + 1]
    return q, k, v, g, glr, wg, bg, wn, vals[6 * gh + 2:]


def gla_fwd(cfg, p0, wg, bg, wn, name):
    gh, nc = cfg.GH, cfg.S // CHUNK

    def fn(ins, st):
        q, k, v, g, glr, wgv, bgv, wnv, _ = _gla_unpack(gh, ins)
        out, new = _gla_core(gh, q, k, v, g, glr, wgv, bgv, wnv, st)
        return [out, jnp.stack(st)], [], new

    return seq_call(name, fn, (1, nc), _gla_ins(cfg, p0, wg, bg, wn, lambda t: t),
                    [((cfg.S, cfg.DV), BF16, (CHUNK, cfg.DV), lambda g, t: (t, 0)),
                     ((nc, gh, HEAD_V, HEAD_K), F32, (None, gh, HEAD_V, HEAD_K), lambda g, t: (t, 0, 0, 0))],
                    carries=[(HEAD_V, HEAD_K)] * gh)


def gla_bwd(cfg, p0, wg, bg, wn, states, dout, name):
    gh, nc = cfg.GH, cfg.S // CHUNK
    rev = lambda t: nc - 1 - t

    def fn(ins, dst):
        q, k, v, g, glr, wgv, bgv, wnv, rest = _gla_unpack(gh, ins)
        st_all, do = rest
        st = [st_all[h] for h in range(gh)]
        _, vjp = jax.vjp(functools.partial(_gla_core, gh), q, k, v, g, glr, wgv, bgv, wnv, st)
        dq, dk, dv, dg, dglr, dwg, dbg, dwn, dstate = vjp((do.astype(F32), list(dst)))
        return ([jnp.concatenate(list(dq) + list(dk) + list(dv) + list(dg), axis=1), dglr],
                [jnp.concatenate(dwg, axis=1), jnp.concatenate(dbg, axis=1), dwn], dstate)

    ins = _gla_ins(cfg, p0, wg, bg, wn, rev)
    ins.append((states, (None, gh, HEAD_V, HEAD_K), lambda g, t: (rev(t), 0, 0, 0)))
    ins.append((dout, (CHUNK, cfg.DV), lambda g, t: (rev(t), 0)))
    wide = 2 * cfg.DK + 2 * cfg.DV
    fixed = lambda g, t: (0, 0)
    return seq_call(name, fn, (1, nc), ins,
                    [((cfg.S, wide), BF16, (CHUNK, wide), lambda g, t: (rev(t), 0)),
                     ((cfg.S, LANES), BF16, (CHUNK, LANES), lambda g, t: (rev(t), 0))],
                    accs=[((LANES, cfg.DK), (LANES, cfg.DK), fixed), ((1, cfg.DK), (1, cfg.DK), fixed),
                          ((1, HEAD_V), (1, HEAD_V), fixed)],
                    carries=[(HEAD_V, HEAD_K)] * gh)


def _lru_core(xc, gate, wa, wi, ba, bi, lam, h_in):
    r = jax.nn.sigmoid(bdot(xc, wa, 1, 0) + ba)
    i = jax.nn.sigmoid(bdot(xc, wi, 1, 0) + bi)
    log_a = LRU_C * r * log_sigmoid(lam)
    a = jnp.exp(log_a)
    u = jnp.sqrt(-_expm1(2.0 * log_a)) * (i * xc)
    first = _rows_iota(a.shape) == 0
    h = lin_scan(a, u + jnp.where(first, a * h_in, 0.0))
    return h * gelu_tanh(gate), _row(h, a.shape[0] - 1)


def _lru_tile(cfg):
    return min(cfg.S, 512)


def _lru_ins(cfg, xc, p0, wa, wi, ba, bi, lam, tmap):
    tt, gh = _lru_tile(cfg), cfg.GH
    vec = lambda g, t: (0, g)
    return [(xc, (tt, LRU_BLOCK), lambda g, t: (tmap(t), g)),
            (p0, (tt, LRU_BLOCK), lambda g, t: (tmap(t), 8 * gh + g)),
            (wa, (None, LRU_BLOCK, LRU_BLOCK), lambda g, t: (g, 0, 0)),
            (wi, (None, LRU_BLOCK, LRU_BLOCK), lambda g, t: (g, 0, 0)),
            (ba, (1, LRU_BLOCK), vec), (bi, (1, LRU_BLOCK), vec), (lam, (1, LRU_BLOCK), vec)]


def lru_fwd(cfg, xc, p0, wa, wi, ba, bi, lam, name):
    tt, nb = _lru_tile(cfg), cfg.NB
    nt = cfg.S // tt

    def fn(ins, c):
        out, h_last = _lru_core(*ins, c[0])
        return [out, c[0]], [], [h_last]

    return seq_call(name, fn, (nb, nt), _lru_ins(cfg, xc, p0, wa, wi, ba, bi, lam, lambda t: t),
                    [((cfg.S, cfg.W), BF16, (tt, LRU_BLOCK), lambda g, t: (t, g)),
                     ((nb, nt, 1, LRU_BLOCK), F32, (None, None, 1, LRU_BLOCK), lambda g, t: (g, t, 0, 0))],
                    carries=[(1, LRU_BLOCK)])


def lru_bwd(cfg, xc, p0, wa, wi, ba, bi, lam, states, dout, name):
    tt, nb = _lru_tile(cfg), cfg.NB
    nt = cfg.S // tt
    rev = lambda t: nt - 1 - t

    def fn(ins, c):
        *fwd_ins, h_in, do = ins
        _, vjp = jax.vjp(_lru_core, *fwd_ins, h_in)
        dxc, dgate, dwa, dwi, dba, dbi, dlam, dh = vjp((do.astype(F32), c[0]))
        return [dxc, dgate], [dwa, dwi, dba, dbi, dlam], [dh]

    ins = _lru_ins(cfg, xc, p0, wa, wi, ba, bi, lam, rev)
    ins.append((states, (None, None, 1, LRU_BLOCK), lambda g, t: (g, rev(t), 0, 0)))
    ins.append((dout, (tt, LRU_BLOCK), lambda g, t: (rev(t), g)))
    mat = ((nb, LRU_BLOCK, LRU_BLOCK), (None, LRU_BLOCK, LRU_BLOCK), lambda g, t: (g, 0, 0))
    vec = ((1, cfg.W), (1, LRU_BLOCK), lambda g, t: (0, g))
    return seq_call(name, fn, (nb, nt), ins,
                    [((cfg.S, cfg.W), F32, (tt, LRU_BLOCK), lambda g, t: (rev(t), g)),
                     ((cfg.S, cfg.W), BF16, (tt, LRU_BLOCK), lambda g, t: (rev(t), g))],
                    accs=[mat, mat, vec, vec, vec], carries=[(1, LRU_BLOCK)])


def _ssd_core(xc, bc, cc, z, dt_raw, dt_bias, a_log, d_skip, gn, st):
    n = xc.shape[0]
    x, bm, cm = silu(xc), silu(bc), silu(cc)
    dt = softplus(dt_raw + dt_bias)
    acs = cumsum_rows(dt * (-jnp.exp(a_log)))
    acs_t = acs.T
    acs_e, dt_e = _expand_heads(acs), _expand_heads(dt)
    last_e = _expand_heads(_row(acs, n - 1))
    causal = _lower_tri(n)
    cb = bdot(cm, bm, 1, 1)
    xdt = x * dt_e
    y_diag = []
    for h, xh in enumerate(_split_heads(xdt)):
        seg = acs[:, h:h + 1] - acs_t[h:h + 1, :]
        decay = jnp.where(causal, jnp.exp(jnp.minimum(seg, 0.0)), 0.0)
        y_diag.append(bdot(cb * decay, xh, 1, 0))
    y = jnp.concatenate(y_diag, axis=1) + bdot(cm, st, 1, 0) * jnp.exp(acs_e)
    new = st * jnp.exp(last_e) + bdot(bm, xdt * jnp.exp(last_e - acs_e), 0, 0)
    y = (y + _expand_heads(d_skip) * x) * silu(z)
    return rms(y, gn), new


def _ssd_ins(cfg, xc, p1, dt_raw, dt_bias, a_log, d_skip, gn, tmap):
    ng = cfg.NG
    vec = lambda g, t: (g, 0, 0)
    return [(xc, (CHUNK, SSD_GW), lambda g, t: (tmap(t), g)),
            (xc, (CHUNK, SSD_N), lambda g, t: (tmap(t), 4 * ng + g)),
            (xc, (CHUNK, SSD_N), lambda g, t: (tmap(t), 5 * ng + g)),
            (p1, (CHUNK, SSD_GW), lambda g, t: (tmap(t), g)),
            (dt_raw, (None, CHUNK, LANES), lambda g, t: (g, tmap(t), 0)),
            (dt_bias, (None, 1, LANES), vec), (a_log, (None, 1, LANES), vec), (d_skip, (None, 1, LANES), vec),
            (gn, (1, SSD_GW), lambda g, t: (0, g))]


def ssd_fwd(cfg, xc, p1, dt_raw, dt_bias, a_log, d_skip, gn, name):
    ng, nc = cfg.NG, cfg.S // CHUNK

    def fn(ins, c):
        out, new = _ssd_core(*ins, c[0])
        return [out, c[0]], [], [new]

    return seq_call(name, fn, (ng, nc), _ssd_ins(cfg, xc, p1, dt_raw, dt_bias, a_log, d_skip, gn, lambda t: t),
                    [((cfg.S, cfg.DI), BF16, (CHUNK, SSD_GW), lambda g, t: (t, g)),
                     ((ng, nc, SSD_N, SSD_GW), F32, (None, None, SSD_N, SSD_GW), lambda g, t: (g, t, 0, 0))],
                    carries=[(SSD_N, SSD_GW)])


def ssd_bwd(cfg, xc, p1, dt_raw, dt_bias, a_log, d_skip, gn, states, dout, name):
    ng, nc = cfg.NG, cfg.S // CHUNK
    rev = lambda t: nc - 1 - t

    def fn(ins, c):
        *fwd_ins, st, do = ins
        _, vjp = jax.vjp(_ssd_core, *fwd_ins, st)
        dxc, dbc, dcc, dz, ddt, dbias, dalog, dskip, dgn, dst = vjp((do.astype(F32), c[0]))
        return [dxc, dbc, dcc, dz, ddt], [dbias, dalog, dskip, dgn], [dst]

    ins = _ssd_ins(cfg, xc, p1, dt_raw, dt_bias, a_log, d_skip, gn, rev)
    ins.append((states, (None, None, SSD_N, SSD_GW), lambda g, t: (g, rev(t), 0, 0)))
    ins.append((dout, (CHUNK, SSD_GW), lambda g, t: (rev(t), g)))
    col = lambda g, t: (rev(t), g)
    vec = ((ng, 1, LANES), (None, 1, LANES), lambda g, t: (g, 0, 0))
    return seq_call(name, fn, (ng, nc), ins,
                    [((cfg.S, cfg.DI), F32, (CHUNK, SSD_GW), col),
                     ((cfg.S, ng * SSD_N), F32, (CHUNK, SSD_N), col),
                     ((cfg.S, ng * SSD_N), F32, (CHUNK, SSD_N), col),
                     ((cfg.S, cfg.DI), BF16, (CHUNK, SSD_GW), col),
                     ((ng, cfg.S, LANES), F32, (None, CHUNK, LANES), lambda g, t: (g, rev(t), 0))],
                    accs=[vec, vec, vec, ((1, cfg.DI), (1, SSD_GW), lambda g, t: (0, g))],
                    carries=[(SSD_N, SSD_GW)])


PACK_ALIGN = SUBLANES * LANES
PACK_ROWS = 256


def pack(arrays):
    pieces = []
    for a in arrays:
        flat = a.reshape(-1).astype(F32)
        pad = _round_up(flat.shape[0], PACK_ALIGN) - flat.shape[0]
        pieces.append(jnp.pad(flat, (0, pad)) if pad else flat)
    flat = jnp.concatenate(pieces)
    pad = _round_up(flat.shape[0], PACK_ROWS * LANES) - flat.shape[0]
    return jnp.pad(flat, (0, pad)).reshape(-1, LANES)


def unpack(buf, shapes):
    lead = buf.shape[:-2]
    flat = buf.reshape(lead + (-1,))
    out, off = [], 0
    for s in shapes:
        n = math.prod(s)
        out.append(flat[..., off:off + n].reshape(lead + tuple(s)))
        off += _round_up(n, PACK_ALIGN)
    return out


def _slots_to_cols(slots):
    return slots.transpose(1, 0, 2).reshape(slots.shape[1], -1)


def _even_in_padded(cfg, w):
    main = 2 * cfg.DK + 2 * cfg.DV
    return jnp.concatenate([w[:, :main], w[:, main + GATE_RANK:], w[:, main:main + GATE_RANK],
                            jnp.zeros((w.shape[0], cfg.EP - cfg.EVEN_IN), w.dtype)], axis=1)


def _even_in_unpadded(cfg, wp):
    main = 2 * cfg.DK + 2 * cfg.DV
    rest = main + 2 * cfg.W
    return jnp.concatenate([wp[:, :main], wp[:, rest:rest + GATE_RANK], wp[:, main:rest]], axis=1)


def _odd_in_padded(cfg, w):
    return jnp.concatenate([w, jnp.zeros((w.shape[0], cfg.OP - cfg.ODD_IN), w.dtype)], axis=1)


def _odd_in_unpadded(cfg, wp):
    return wp[:, :cfg.ODD_IN]


def _group_lanes(cfg, v):
    lead = v.shape[:-1]
    g = jnp.moveaxis(v.reshape(lead + (cfg.NG, SSD_HG)), -2, 0)
    return jnp.pad(g, [(0, 0)] * (g.ndim - 1) + [(0, LANES - SSD_HG)])


def _ungroup_lanes(cfg, g):
    v = jnp.moveaxis(g[..., :SSD_HG], 0, -2)
    return v.reshape(v.shape[:-2] + (cfg.NH,))


def train_step(cfg, p, loss_target):
    S, D = cfg.S, cfg.D
    me = 4 * lax.axis_index("x") + 2 * lax.axis_index("y") + lax.axis_index("c")

    big = ["ev_w_in", "ev_w_out", "od_w_in", "od_w_out", "ffn_w_gate", "ffn_w_up", "ffn_w_down"]
    small_sharded = ["ev_gla_w_gate", "ev_lru_conv_w", "od_norm", "od_conv_w", "od_conv_b", "od_gnorm"]
    replicated = ["ev_norm", "ev_gla_b_gate", "ev_gla_w_onorm", "ev_lru_conv_b", "ev_lru_w_a", "ev_lru_b_a",
                  "ev_lru_w_i", "ev_lru_b_i", "ev_lru_lam", "od_dt_bias", "od_a_log", "od_d_skip", "ffn_norm",
                  "final_norm"]

    gw = dict(zip(big, exchange([p[n].astype(BF16) for n in big], True, "gather_weights")))
    ss_shapes = [p[n].shape for n in small_sharded]
    ss_all = exchange([pack([p[n] for n in small_sharded])], True, "gather_small")[0]
    gs = dict(zip(small_sharded, unpack(ss_all, ss_shapes)))

    w_ev_in = cols_from_slots(gw["ev_w_in"], functools.partial(_even_in_padded, cfg), cfg.EP, "ev_w_in_cols")
    w_ev_out = gw["ev_w_out"].reshape(D, D)
    w_od_in = cols_from_slots(gw["od_w_in"], functools.partial(_odd_in_padded, cfg), cfg.OP, "od_w_in_cols")
    w_od_out = gw["od_w_out"].reshape(cfg.DI, D)
    w_gate, w_up, w_down = gw["ffn_w_gate"], gw["ffn_w_up"], gw["ffn_w_down"]

    gla_wg = jnp.pad(_slots_to_cols(gs["ev_gla_w_gate"][:, 0]), ((0, LANES - GATE_RANK), (0, 0)))
    lru_cw = _slots_to_cols(gs["ev_lru_conv_w"][:, 0])
    od_norm = gs["od_norm"].transpose(1, 0, 2).reshape(1, D)
    od_cw = _slots_to_cols(gs["od_conv_w"][:, 0])
    od_cb = gs["od_conv_b"].transpose(1, 0, 2).reshape(1, cfg.CD)
    od_gn = gs["od_gnorm"].transpose(1, 0, 2).reshape(1, cfg.DI)

    x0 = p["x"][0]
    target = loss_target[0]
    ev_norm = p["ev_norm"]
    bg = p["ev_gla_b_gate"]
    wn = p["ev_gla_w_onorm"]
    lru_cb = p["ev_lru_conv_b"]
    wa, wi = p["ev_lru_w_a"][0], p["ev_lru_w_i"][0]
    ba, bi, lam = p["ev_lru_b_a"], p["ev_lru_b_i"], p["ev_lru_lam"]
    dt_bias, a_log, d_skip = (_group_lanes(cfg, p[n]) for n in ("od_dt_bias", "od_a_log", "od_d_skip"))
    ffn_norm = [p["ffn_norm"][l:l + 1] for l in range(2)]
    final_norm = p["final_norm"].reshape(1, D)

    def ffn_forward(l, x):
        h = norm_fwd(cfg, x, ffn_norm[l], f"ffn{l}_norm")
        gate = matmul(h, w_gate, b_slot="n", b_lead=(l,), name=f"ffn{l}_gate")
        up = matmul(h, w_up, b_slot="n", b_lead=(l,), name=f"ffn{l}_up")
        act = act_fwd(cfg, gate, up, f"ffn{l}_act")
        out = matmul(act, w_down, a_slot="k", b_slot="k", b_lead=(l,), res=x, name=f"ffn{l}_down", tn=512)
        return out, (h, gate, up, act)

    h0 = norm_fwd(cfg, x0, ev_norm, "ev_norm")
    p0 = matmul(h0, w_ev_in, name="ev_in", tn=768)
    gla_out, gla_states = gla_fwd(cfg, p0, gla_wg, bg, wn, "gla_fwd")
    lru_col = 2 * cfg.DK + 2 * cfg.DV
    lru_xc = conv_fwd(cfg, p0, lru_col, cfg.W, lru_cw, lru_cb, "lru_conv")
    lru_out, lru_states = lru_fwd(cfg, lru_xc, p0, wa, wi, ba, bi, lam, "lru_fwd")
    mix = jnp.concatenate([gla_out, lru_out], axis=1)
    x1 = matmul(mix, w_ev_out, res=x0, name="ev_out")
    x2, ffn0_saved = ffn_forward(0, x1)

    h2 = norm_fwd(cfg, x2, od_norm, "od_norm")
    p1 = matmul(h2, w_od_in, name="od_in", tn=768)
    od_xc = conv_fwd(cfg, p1, cfg.DI, cfg.CD, od_cw, od_cb, "od_conv")
    dt_col = cfg.DI + cfg.CD
    dt_raw = _group_lanes(cfg, p1[:, dt_col:dt_col + cfg.NH])
    ssd_out, ssd_states = ssd_fwd(cfg, od_xc, p1, dt_raw, dt_bias, a_log, d_skip, od_gn, "ssd_fwd")
    x3 = matmul(ssd_out, w_od_out, res=x2, name="od_out")
    x4, ffn1_saved = ffn_forward(1, x3)

    loss_part, dx4, d_final_norm = head_fwd_bwd(cfg, x4, final_norm, target, "head")
    loss = lax.psum(loss_part[0, 0], MESH_AXES)

    def ffn_backward(l, x, saved, dx_out):
        h, gate, up, act = saved
        dact = matmul(dx_out, w_down, tb=True, b_slot="n", b_lead=(l,), name=f"ffn{l}_dact")
        d_down = matmul(act, dx_out, ta=True, a_slot="m", out_dtype=BF16, name=f"ffn{l}_dwdown")
        dgate, dup = act_bwd(cfg, gate, up, dact, f"ffn{l}_act_bwd")
        dh = matmul(dgate, w_gate, tb=True, a_slot="k", b_slot="k", b_lead=(l,), name=f"ffn{l}_dh_gate", tn=512)
        dh = matmul(dup, w_up, tb=True, a_slot="k", b_slot="k", b_lead=(l,), res=dh, name=f"ffn{l}_dh_up", tn=512)
        d_gate = matmul(h, dgate, ta=True, b_slot="n", out_dtype=BF16, name=f"ffn{l}_dwgate")
        d_up = matmul(h, dup, ta=True, b_slot="n", out_dtype=BF16, name=f"ffn{l}_dwup")
        dx, dnorm = norm_bwd(cfg, x, ffn_norm[l], dh, dx_out, f"ffn{l}_norm_bwd")
        return dx, dnorm, d_gate, d_up, d_down

    dx3, d_ffn_norm1, d_gate1, d_up1, d_down1 = ffn_backward(1, x3, ffn1_saved, dx4)

    d_ssd_out = matmul(dx3, w_od_out, tb=True, name="od_dmix")
    d_od_out = matmul(ssd_out, dx3, ta=True, out_dtype=BF16, name="od_dwout")
    dxs, dbm, dcm, dz, d_dt_raw, d_dt_bias, d_a_log, d_d_skip, d_od_gn = ssd_bwd(
        cfg, od_xc, p1, dt_raw, dt_bias, a_log, d_skip, od_gn, ssd_states, d_ssd_out, "ssd_bwd")
    d_od_xc = jnp.concatenate([dxs, dbm, dcm], axis=1)
    d_xbc, d_od_cw, d_od_cb = conv_bwd(cfg, p1, cfg.DI, cfg.CD, od_cw, d_od_xc, "od_conv_bwd")
    d_dt = _ungroup_lanes(cfg, d_dt_raw).astype(BF16)
    dp1 = jnp.concatenate([dz, d_xbc, d_dt, jnp.zeros((S, cfg.OP - cfg.ODD_IN), BF16)], axis=1)
    dh2 = matmul(dp1, w_od_in, tb=True, name="od_dh", tk=1536)
    d_od_in = matmul(h2, dp1, ta=True, out_dtype=BF16, name="od_dwin", tn=768)
    dx2, d_od_norm = norm_bwd(cfg, x2, od_norm, dh2, dx3, "od_norm_bwd")

    dx1, d_ffn_norm0, d_gate0, d_up0, d_down0 = ffn_backward(0, x1, ffn0_saved, dx2)

    d_mix = matmul(dx1, w_ev_out, tb=True, name="ev_dmix")
    d_ev_out = matmul(mix, dx1, ta=True, out_dtype=BF16, name="ev_dwout")
    d_qkvg, d_glr, d_gla_wg, d_bg, d_wn = gla_bwd(cfg, p0, gla_wg, bg, wn, gla_states, d_mix[:, :cfg.DV], "gla_bwd")
    d_lru_xc, d_gate_br, d_wa, d_wi, d_ba, d_bi, d_lam = lru_bwd(
        cfg, lru_xc, p0, wa, wi, ba, bi, lam, lru_states, d_mix[:, cfg.DV:], "lru_bwd")
    d_xbr, d_lru_cw, d_lru_cb = conv_bwd(cfg, p0, lru_col, cfg.W, lru_cw, d_lru_xc, "lru_conv_bwd")
    dp0 = jnp.concatenate([d_qkvg, d_xbr, d_gate_br, d_glr, jnp.zeros((S, cfg.EP - lru_col - 2 * cfg.W - LANES), BF16)],
                          axis=1)
    dh0 = matmul(dp0, w_ev_in, tb=True, name="ev_dh", tk=1792)
    d_ev_in = matmul(h0, dp0, ta=True, out_dtype=BF16, name="ev_dwin", tn=768)
    grad_x, d_ev_norm = norm_bwd(cfg, x0, ev_norm, dh0, dx1, "ev_norm_bwd")

    send = {
        "ev_w_in": slots_from_cols(d_ev_in, functools.partial(_even_in_unpadded, cfg), p["ev_w_in"].shape[2],
                                   "ev_dwin_slots"),
        "ev_w_out": d_ev_out.reshape((N_DEV,) + p["ev_w_out"].shape),
        "od_w_in": slots_from_cols(d_od_in, functools.partial(_odd_in_unpadded, cfg), p["od_w_in"].shape[2],
                                   "od_dwin_slots"),
        "od_w_out": d_od_out.reshape((N_DEV,) + p["od_w_out"].shape),
        "ffn_w_gate": jnp.stack([d_gate0, d_gate1], axis=1),
        "ffn_w_up": jnp.stack([d_up0, d_up1], axis=1),
        "ffn_w_down": jnp.stack([d_down0, d_down1], axis=1),
    }
    recv = dict(zip(big, exchange([send[n] for n in big], False, "exchange_grads")))
    out = {"loss": loss, "grad_x": grad_x[None]}
    for n in big:
        out["grad_" + n], out["delta_" + n], out["new_m_" + n], out["new_v_" + n] = adamw_sharded(
            recv[n], p[n], p["m_" + n], p["v_" + n], "adamw_" + n)

    small_full = {
        "ev_gla_w_gate": d_gla_wg[:GATE_RANK][None], "ev_lru_conv_w": d_lru_cw[None], "od_norm": d_od_norm,
        "od_conv_w": d_od_cw[None], "od_conv_b": d_od_cb, "od_gnorm": d_od_gn,
        "ev_norm": d_ev_norm, "ev_gla_b_gate": d_bg, "ev_gla_w_onorm": d_wn, "ev_lru_conv_b": d_lru_cb,
        "ev_lru_w_a": d_wa[None], "ev_lru_b_a": d_ba, "ev_lru_w_i": d_wi[None], "ev_lru_b_i": d_bi,
        "ev_lru_lam": d_lam, "od_dt_bias": _ungroup_lanes(cfg, d_dt_bias), "od_a_log": _ungroup_lanes(cfg, d_a_log),
        "od_d_skip": _ungroup_lanes(cfg, d_d_skip), "ffn_norm": jnp.concatenate([d_ffn_norm0, d_ffn_norm1], axis=0),
        "final_norm": d_final_norm.reshape(D),
    }
    small = small_sharded + replicated
    small_all = exchange([pack([small_full[n] for n in small])], True, "gather_small_grads")[0]
    g_small = dict(zip(small, unpack(reduce_slots(small_all, "sum_small_grads"), [small_full[n].shape for n in small])))
    for n in small_sharded:
        width = p[n].shape[-1]
        g_small[n] = lax.dynamic_slice_in_dim(g_small[n], me * width, width, axis=g_small[n].ndim - 1)
    shapes = [p[n].shape for n in small]
    g_buf = pack([g_small[n] for n in small])
    delta, new_m, new_v = adamw(pack([p[n] for n in small]), g_buf, pack([p["m_" + n] for n in small]),
                                pack([p["v_" + n] for n in small]), "adamw_small")
    for kind, buf in (("grad_", g_buf), ("delta_", delta), ("new_m_", new_m), ("new_v_", new_v)):
        for n, a in zip(small, unpack(buf, shapes)):
            out[kind + n] = a
    return out


WEIGHTS = ['ev_norm', 'ev_w_in', 'ev_gla_w_gate', 'ev_gla_b_gate', 'ev_gla_w_onorm', 'ev_lru_conv_w', 'ev_lru_conv_b',
           'ev_lru_w_a', 'ev_lru_b_a', 'ev_lru_w_i', 'ev_lru_b_i', 'ev_lru_lam', 'ev_w_out', 'od_norm', 'od_w_in',
           'od_conv_w', 'od_conv_b', 'od_dt_bias', 'od_a_log', 'od_d_skip', 'od_gnorm', 'od_w_out', 'ffn_norm',
           'ffn_w_gate', 'ffn_w_up', 'ffn_w_down', 'final_norm']


def kernel(x, ev_norm, ev_w_in, ev_gla_w_gate, ev_gla_b_gate, ev_gla_w_onorm, ev_lru_conv_w, ev_lru_conv_b, ev_lru_w_a, ev_lru_b_a, ev_lru_w_i, ev_lru_b_i, ev_lru_lam, ev_w_out, od_norm, od_w_in, od_conv_w, od_conv_b, od_dt_bias, od_a_log, od_d_skip, od_gnorm, od_w_out, ffn_norm, ffn_w_gate, ffn_w_up, ffn_w_down, final_norm, loss_target, m_ev_norm, m_ev_w_in, m_ev_gla_w_gate, m_ev_gla_b_gate, m_ev_gla_w_onorm, m_ev_lru_conv_w, m_ev_lru_conv_b, m_ev_lru_w_a, m_ev_lru_b_a, m_ev_lru_w_i, m_ev_lru_b_i, m_ev_lru_lam, m_ev_w_out, m_od_norm, m_od_w_in, m_od_conv_w, m_od_conv_b, m_od_dt_bias, m_od_a_log, m_od_d_skip, m_od_gnorm, m_od_w_out, m_ffn_norm, m_ffn_w_gate, m_ffn_w_up, m_ffn_w_down, m_final_norm, v_ev_norm, v_ev_w_in, v_ev_gla_w_gate, v_ev_gla_b_gate, v_ev_gla_w_onorm, v_ev_lru_conv_w, v_ev_lru_conv_b, v_ev_lru_w_a, v_ev_lru_b_a, v_ev_lru_w_i, v_ev_lru_b_i, v_ev_lru_lam, v_ev_w_out, v_od_norm, v_od_w_in, v_od_conv_w, v_od_conv_b, v_od_dt_bias, v_od_a_log, v_od_d_skip, v_od_gnorm, v_od_w_out, v_ffn_norm, v_ffn_w_gate, v_ffn_w_up, v_ffn_w_down, v_final_norm):
    args = dict(locals())
    p = {n: a for n, a in args.items() if n != "loss_target"}
    cfg = Cfg(S=x.shape[1], D=x.shape[2], DFF=ffn_w_gate.shape[2] * N_DEV)
    out = train_step(cfg, p, loss_target)
    return (out["loss"], out["grad_x"], *[out["grad_" + w] for w in WEIGHTS], *[out["delta_" + w] for w in WEIGHTS],
            *[out["new_m_" + w] for w in WEIGHTS], *[out["new_v_" + w] for w in WEIGHTS])
```

```python
import functools
import math
from typing import NamedTuple

import jax
import jax.numpy as jnp
from jax import lax
from jax.experimental import pallas as pl
from jax.experimental.pallas import tpu as pltpu

F32 = jnp.float32
BF16 = jnp.bfloat16
MESH_AXES = ("x", "y", "c")
N_DEV = 8
LANES = 128
SUBLANES = 8
VMEM_LIMIT = 56 * 1024 * 1024

NORM_EPS = 1e-6
CONV_WIDTH = 4
CHUNK = 64
HEAD_K = 128
HEAD_V = 256
GATE_RANK = 16
GATE_NORM = 16.0
LRU_BLOCK = 128
LRU_C = 8.0
SSD_P = 64
SSD_N = 128
SSD_HG = 8
SSD_GW = SSD_HG * SSD_P

ADAM_LR = 0.001
ADAM_B1 = 0.9
ADAM_B2 = 0.999
ADAM_EPS = 1e-08
ADAM_WD = 0.01
ADAM_STEP = 10


class Cfg(NamedTuple):
    S: int
    D: int
    DFF: int

    @property
    def GH(self):
        return self.D // 512

    @property
    def NB(self):
        return self.D // 256

    @property
    def NG(self):
        return self.D // 256

    @property
    def DK(self):
        return HEAD_K * self.GH

    @property
    def DV(self):
        return HEAD_V * self.GH

    @property
    def W(self):
        return LRU_BLOCK * self.NB

    @property
    def DI(self):
        return SSD_GW * self.NG

    @property
    def CD(self):
        return self.DI + 2 * self.NG * SSD_N

    @property
    def NH(self):
        return SSD_HG * self.NG

    @property
    def EVEN_IN(self):
        return 2 * self.DK + 2 * self.DV + GATE_RANK + 2 * self.W

    @property
    def ODD_IN(self):
        return self.DI + self.CD + self.NH

    @property
    def EP(self):
        return _round_up(2 * self.DK + 2 * self.DV + 2 * self.W + LANES, 768)

    @property
    def OP(self):
        return _round_up(self.DI + self.CD + LANES, 768)


def _round_up(n, m):
    return (n + m - 1) // m * m


def _tile(n, pref):
    if n <= pref:
        return n
    t = pref - pref % LANES
    while n % t:
        t -= LANES
    return t


def _cparams(n_axes):
    return pltpu.CompilerParams(dimension_semantics=("arbitrary",) * n_axes, vmem_limit_bytes=VMEM_LIMIT)


def _dg(a, b, ca, cb):
    return lax.dot_general(a.astype(BF16), b.astype(BF16), (((ca,), (cb,)), ((), ())), preferred_element_type=F32)


@functools.partial(jax.custom_vjp, nondiff_argnums=(2, 3))
def bdot(a, b, ca, cb):
    return _dg(a, b, ca, cb)


def _bdot_fwd(a, b, ca, cb):
    return _dg(a, b, ca, cb), (a, b)


def _bdot_bwd(ca, cb, res, g):
    a, b = res
    da = _dg(g, b, 1, 1 - cb) if ca == 1 else _dg(b, g, 1 - cb, 1)
    db = _dg(a, g, 1 - ca, 0) if cb == 0 else _dg(g, a, 0, 1 - ca)
    return da.astype(a.dtype), db.astype(b.dtype)


bdot.defvjp(_bdot_fwd, _bdot_bwd)


def _lower_tri(n):
    r = lax.broadcasted_iota(jnp.int32, (n, n), 0)
    c = lax.broadcasted_iota(jnp.int32, (n, n), 1)
    return c <= r


@jax.custom_vjp
def cumsum_rows(x):
    n = x.shape[0]
    return lax.dot_general(_lower_tri(n).astype(F32), x, (((1,), (0,)), ((), ())),
                           precision=lax.Precision.HIGHEST, preferred_element_type=F32)


def _cumsum_fwd(x):
    return cumsum_rows(x), None


def _cumsum_bwd(_, g):
    n = g.shape[0]
    return (lax.dot_general(_lower_tri(n).astype(F32), g, (((0,), (0,)), ((), ())),
                            precision=lax.Precision.HIGHEST, preferred_element_type=F32),)


cumsum_rows.defvjp(_cumsum_fwd, _cumsum_bwd)


def _row(x, i):
    r = lax.broadcasted_iota(jnp.int32, x.shape, 0)
    return jnp.sum(jnp.where(r == i, x, 0.0), axis=0, keepdims=True)


def _softplus_raw(x):
    return jnp.maximum(x, 0.0) + jnp.log(1.0 + jnp.exp(-jnp.abs(x)))


@jax.custom_vjp
def softplus(x):
    return _softplus_raw(x)


softplus.defvjp(lambda x: (_softplus_raw(x), x), lambda x, g: (g * jax.nn.sigmoid(x),))


@jax.custom_vjp
def log_sigmoid(x):
    return -_softplus_raw(-x)


log_sigmoid.defvjp(lambda x: (-_softplus_raw(-x), x), lambda x, g: (g * jax.nn.sigmoid(-x),))


def silu(x):
    return x * jax.nn.sigmoid(x)


def gelu_tanh(x):
    return 0.5 * x * (1.0 + jnp.tanh(math.sqrt(2.0 / math.pi) * (x + 0.044715 * (x * x * x))))


def _expm1(x):
    series = x * (1.0 + 0.5 * x * (1.0 + (1.0 / 3.0) * x))
    return jnp.where(jnp.abs(x) < 1e-2, series, jnp.exp(x) - 1.0)


def rms(x, w):
    return x * lax.rsqrt(jnp.mean(x * x, axis=-1, keepdims=True) + NORM_EPS) * w


def _rows_iota(shape):
    return lax.broadcasted_iota(jnp.int32, shape, 0)


def _scan_up(a, u):
    n = a.shape[0]
    r = _rows_iota(a.shape)
    d = 1
    while d < n:
        m = r >= d
        a_s = jnp.where(m, pltpu.roll(a, d, 0), 1.0)
        u_s = jnp.where(m, pltpu.roll(u, d, 0), 0.0)
        u = a * u_s + u
        a = a * a_s
        d *= 2
    return u


def _scan_down(a, u):
    n = a.shape[0]
    r = _rows_iota(a.shape)
    d = 1
    while d < n:
        m = r < n - d
        a_s = jnp.where(m, pltpu.roll(a, n - d, 0), 1.0)
        u_s = jnp.where(m, pltpu.roll(u, n - d, 0), 0.0)
        u = a * u_s + u
        a = a * a_s
        d *= 2
    return u


@jax.custom_vjp
def lin_scan(a, u):
    return _scan_up(a, u)


def _lin_scan_fwd(a, u):
    h = _scan_up(a, u)
    return h, (a, h)


def _lin_scan_bwd(res, g):
    a, h = res
    n = a.shape[0]
    r = _rows_iota(a.shape)
    a_next = jnp.where(r < n - 1, pltpu.roll(a, n - 1, 0), 0.0)
    gt = _scan_down(a_next, g)
    h_prev = jnp.where(r >= 1, pltpu.roll(h, 1, 0), 0.0)
    return gt * h_prev, gt


lin_scan.defvjp(_lin_scan_fwd, _lin_scan_bwd)


def _expand_heads(v):
    r = v.shape[0]
    return jnp.concatenate([jnp.broadcast_to(v[:, h:h + 1], (r, SSD_P)) for h in range(SSD_HG)], axis=1)


@jax.custom_vjp
def _split_heads(x):
    return tuple(x[:, h * SSD_P:(h + 1) * SSD_P] for h in range(SSD_HG))


_split_heads.defvjp(lambda x: (_split_heads(x), None), lambda _, gs: (jnp.concatenate(gs, axis=1),))


def matmul(a, b, *, ta=False, tb=False, a_slot=None, b_slot=None, b_lead=(), res=None, after=None, out_dtype=F32,
           name, tm=512, tn=1024, tk=2048):
    lead = tuple(b_lead)
    ra, ca_ = a.shape[-2:]
    rb, cb_ = b.shape[-2:]
    m_st, ka_st = (ca_, ra) if ta else (ra, ca_)
    kb_st, n_st = (cb_, rb) if tb else (rb, cb_)
    kslot = a_slot == "k"
    assert kslot == (b_slot == "k")
    m = m_st * (N_DEV if a_slot == "m" else 1)
    n = n_st * (N_DEV if b_slot == "n" else 1)
    assert ka_st == kb_st, (a.shape, b.shape, ta, tb)
    tm = m_st if a_slot == "m" else _tile(m, tm)
    tn = n_st if b_slot == "n" else _tile(n, tn)
    tk = ka_st if kslot else _tile(ka_st, tk)
    nk = ka_st // tk
    ca, cb = (0 if ta else 1), (1 if tb else 0)
    nl = (None,) * len(lead)

    if a_slot is None:
        a_spec = pl.BlockSpec((tk, tm), lambda i, j, k: (k, i)) if ta else pl.BlockSpec((tm, tk), lambda i, j, k: (i, k))
    elif a_slot == "m":
        a_spec = (pl.BlockSpec((None, tk, tm), lambda i, j, k: (i, k, 0)) if ta
                  else pl.BlockSpec((None, tm, tk), lambda i, j, k: (i, 0, k)))
    else:
        a_spec = (pl.BlockSpec((N_DEV, tk, tm), lambda i, j, k: (0, 0, i)) if ta
                  else pl.BlockSpec((N_DEV, tm, tk), lambda i, j, k: (0, i, 0)))
    if b_slot is None:
        b_spec = (pl.BlockSpec(nl + (tn, tk), lambda i, j, k: lead + (j, k)) if tb
                  else pl.BlockSpec(nl + (tk, tn), lambda i, j, k: lead + (k, j)))
    elif b_slot == "n":
        b_spec = (pl.BlockSpec((None,) + nl + (tn, tk), lambda i, j, k: (j,) + lead + (0, k)) if tb
                  else pl.BlockSpec((None,) + nl + (tk, tn), lambda i, j, k: (j,) + lead + (k, 0)))
    else:
        b_spec = (pl.BlockSpec((N_DEV,) + nl + (tn, tk), lambda i, j, k: (0,) + lead + (j, 0)) if tb
                  else pl.BlockSpec((N_DEV,) + nl + (tk, tn), lambda i, j, k: (0,) + lead + (0, j)))
    if a_slot == "m":
        o_spec, o_shape = pl.BlockSpec((None, tm, tn), lambda i, j, k: (i, 0, j)), (N_DEV, tm, n)
    elif b_slot == "n":
        o_spec, o_shape = pl.BlockSpec((None, tm, tn), lambda i, j, k: (j, i, 0)), (N_DEV, m, tn)
    else:
        o_spec, o_shape = pl.BlockSpec((tm, tn), lambda i, j, k: (i, j)), (m, n)
    assert res is None or (a_slot != "m" and b_slot != "n")

    def dot(x, y):
        return lax.dot_general(x.astype(BF16), y.astype(BF16), (((ca,), (cb,)), ((), ())), preferred_element_type=F32)

    def body(*refs):
        a_ref, b_ref = refs[:2]
        r_ref = refs[2] if res is not None else None
        o_ref = refs[2 + (res is not None) + (after is not None)]

        def finish(acc):
            if r_ref is not None:
                acc = acc + r_ref[...].astype(F32)
            o_ref[...] = acc.astype(o_ref.dtype)

        if kslot:
            acc = dot(a_ref[0], b_ref[0])
            for s in range(1, N_DEV):
                acc = acc + dot(a_ref[s], b_ref[s])
            finish(acc)
        elif nk == 1:
            finish(dot(a_ref[...], b_ref[...]))
        else:
            acc_ref = refs[-1]
            k = pl.program_id(2)

            @pl.when(k == 0)
            def _():
                acc_ref[...] = dot(a_ref[...], b_ref[...])

            @pl.when(k > 0)
            def _():
                acc_ref[...] += dot(a_ref[...], b_ref[...])

            @pl.when(k == nk - 1)
            def _():
                finish(acc_ref[...])

    in_specs = [a_spec, b_spec]
    args = [a, b]
    if res is not None:
        in_specs.append(pl.BlockSpec((tm, tn), lambda i, j, k: (i, j)))
        args.append(res)
    if after is not None:
        in_specs.append(pl.BlockSpec(memory_space=pl.ANY))
        args.append(after)
    return pl.pallas_call(
        body, name=name, grid=(m // tm, n // tn, nk), in_specs=in_specs, out_specs=o_spec,
        out_shape=jax.ShapeDtypeStruct(o_shape, out_dtype),
        scratch_shapes=[pltpu.VMEM((tm, tn), F32)] if nk > 1 else [], compiler_params=_cparams(3),
    )(*args)


def seq_call(name, fn, grid, ins, outs, accs=(), carries=()):
    n_in, n_out, n_acc = len(ins), len(outs), len(accs)

    def body(*refs):
        in_refs = refs[:n_in]
        out_refs = refs[n_in:n_in + n_out]
        acc_refs = refs[n_in + n_out:n_in + n_out + n_acc]
        c_refs = refs[n_in + n_out + n_acc:]

        if acc_refs or c_refs:
            @pl.when(pl.program_id(1) == 0)
            def _():
                for r in tuple(acc_refs) + tuple(c_refs):
                    r[...] = jnp.zeros_like(r)

        o, a, c = fn([r[...] for r in in_refs], [r[...] for r in c_refs])
        for r, v in zip(out_refs, o, strict=True):
            r[...] = v.astype(r.dtype)
        for r, v in zip(acc_refs, a, strict=True):
            r[...] += v
        for r, v in zip(c_refs, c, strict=True):
            r[...] = v

    return pl.pallas_call(
        body, name=name, grid=grid,
        in_specs=[pl.BlockSpec(blk, im) for _, blk, im in ins],
        out_specs=[pl.BlockSpec(blk, im) for _, _, blk, im in outs] + [pl.BlockSpec(blk, im) for _, blk, im in accs],
        out_shape=[jax.ShapeDtypeStruct(s, d) for s, d, _, _ in outs] + [jax.ShapeDtypeStruct(s, F32) for s, _, _ in accs],
        scratch_shapes=[pltpu.VMEM(s, F32) for s in carries], compiler_params=_cparams(2),
    )(*[a for a, _, _ in ins])


def exchange(arrays, gather, name):
    n = len(arrays)

    def body(*refs):
        x_refs, o_refs = refs[:n], refs[n:2 * n]
        send_sems, recv_sems, local_sems = refs[2 * n:]
        pos = [lax.axis_index(ax) for ax in MESH_AXES]
        me = 4 * pos[0] + 2 * pos[1] + pos[2]
        copies = []
        for i in range(n):
            own = pltpu.make_async_copy(x_refs[i] if gather else x_refs[i].at[me], o_refs[i].at[me], local_sems.at[i])
            own.start()
            copies.append(own)
        for k in range(1, N_DEV):
            bits = ((k >> 2) & 1, (k >> 1) & 1, k & 1)
            peer = tuple(1 - p if b else p for p, b in zip(pos, bits))
            peer_id = 4 * peer[0] + 2 * peer[1] + peer[2]
            for i in range(n):
                cp = pltpu.make_async_remote_copy(
                    src_ref=x_refs[i] if gather else x_refs[i].at[peer_id], dst_ref=o_refs[i].at[me],
                    send_sem=send_sems.at[i * (N_DEV - 1) + k - 1], recv_sem=recv_sems.at[i * (N_DEV - 1) + k - 1],
                    device_id=peer, device_id_type=pl.DeviceIdType.MESH)
                cp.start()
                copies.append(cp)
        for cp in copies:
            cp.wait()

    hbm = pl.BlockSpec(memory_space=pltpu.HBM)
    return pl.pallas_call(
        body, name=name, in_specs=[hbm] * n, out_specs=[hbm] * n,
        out_shape=[jax.ShapeDtypeStruct(((N_DEV,) + a.shape) if gather else a.shape, a.dtype) for a in arrays],
        scratch_shapes=[pltpu.SemaphoreType.DMA((n * (N_DEV - 1),)), pltpu.SemaphoreType.DMA((n * (N_DEV - 1),)),
                        pltpu.SemaphoreType.DMA((n,))],
    )(*arrays)


_HBM = pl.BlockSpec(memory_space=pltpu.HBM)
_SEM = pl.BlockSpec(memory_space=pltpu.SEMAPHORE)
N_PEERS = N_DEV - 1


def _mesh_pos():
    pos = [lax.axis_index(ax) for ax in MESH_AXES]
    return pos, 4 * pos[0] + 2 * pos[1] + pos[2]


def _peers(pos):
    out = []
    for k in range(1, N_DEV):
        bits = ((k >> 2) & 1, (k >> 1) & 1, k & 1)
        peer = tuple(1 - p if b else p for p, b in zip(pos, bits))
        out.append((peer, 4 * peer[0] + 2 * peer[1] + peer[2]))
    return out


def _part(x_ref, lead, gather, slot):
    ref = x_ref if lead is None else x_ref.at[lead]
    return ref if gather else ref.at[slot]


def _landing_zones(items, gather, name):
    n = len(items)
    shapes = [(((N_DEV,) + (a.shape if lead is None else a.shape[1:])) if gather else a.shape) for a, lead in items]

    def body(*refs):
        x_refs, o_refs, sems = refs[:n], refs[n:2 * n], refs[2 * n]
        _, me = _mesh_pos()
        copies = [pltpu.make_async_copy(_part(x_refs[i], items[i][1], gather, me), o_refs[i].at[me], sems.at[i])
                  for i in range(n)]
        for cp in copies:
            cp.start()
        for cp in copies:
            cp.wait()

    return pl.pallas_call(
        body, name=name, in_specs=[_HBM] * n, out_specs=[_HBM] * n,
        out_shape=[jax.ShapeDtypeStruct(s, a.dtype) for s, (a, _) in zip(shapes, items)],
        scratch_shapes=[pltpu.SemaphoreType.DMA((n,))])(*[a for a, _ in items])


def _split_copies(items, gather, x_refs, land_refs, send_sems, recv_sems):
    pos, me = _mesh_pos()
    copies = []
    for i, (_, lead) in enumerate(items):
        for k, (peer, peer_id) in enumerate(_peers(pos)):
            copies.append(pltpu.make_async_remote_copy(
                src_ref=_part(x_refs[i], lead, gather, peer_id), dst_ref=land_refs[i].at[me],
                send_sem=send_sems.at[i * N_PEERS + k], recv_sem=recv_sems.at[i * N_PEERS + k],
                device_id=peer, device_id_type=pl.DeviceIdType.MESH))
    return copies


def exchange_start(items, gather, name):
    n = len(items)
    lands = list(_landing_zones(items, gather, name + "_own"))
    xs = [a for a, _ in items]

    def body(*refs):
        x_refs, land_refs = refs[:n], refs[n:2 * n]
        send_sems, recv_sems, token = refs[2 * n], refs[2 * n + 1], refs[-1]
        for cp in _split_copies(items, gather, x_refs, land_refs, send_sems, recv_sems):
            cp.start()
        token[...] = jnp.zeros_like(token)

    outs = pl.pallas_call(
        body, name=name,
        out_shape=(pltpu.SemaphoreType.DMA((n * N_PEERS,)), pltpu.SemaphoreType.DMA((n * N_PEERS,)),
                   *[pltpu.HBM(v.shape, v.dtype) for v in xs + lands], jax.ShapeDtypeStruct((SUBLANES, LANES), F32)),
        in_specs=[_HBM] * (2 * n),
        out_specs=(_SEM, _SEM, *[_HBM] * (2 * n), pl.BlockSpec(memory_space=pltpu.VMEM)),
        input_output_aliases={i: 2 + i for i in range(2 * n)},
        compiler_params=pltpu.CompilerParams(has_side_effects=pltpu.SideEffectType.DATAFLOW_SIDE_EFFECTING),
    )(*[pltpu.with_memory_space_constraint(v, pltpu.HBM) for v in xs + lands])
    handle = (items, gather, outs[0], outs[1], outs[2:2 + n], outs[2 + n:2 + 2 * n])
    return handle, outs[-1]


def exchange_wait(handle, after, name):
    items, gather, send_sems, recv_sems, x_thru, land_thru = handle
    n = len(items)

    def body(*refs):
        x_refs, land_refs = refs[:n], refs[n:2 * n]
        for cp in _split_copies(items, gather, x_refs, land_refs, refs[2 * n], refs[2 * n + 1]):
            cp.wait_send()
            cp.wait_recv()

    outs = pl.pallas_call(
        body, name=name, out_shape=tuple(pltpu.HBM(v.shape, v.dtype) for v in tuple(x_thru) + tuple(land_thru)),
        in_specs=[_HBM] * (2 * n) + [_SEM, _SEM, pl.BlockSpec(memory_space=pl.ANY)], out_specs=tuple([_HBM] * (2 * n)),
        input_output_aliases={i: i for i in range(2 * n)},
        compiler_params=pltpu.CompilerParams(has_side_effects=pltpu.SideEffectType.DATAFLOW_SIDE_EFFECTING),
    )(*x_thru, *land_thru, send_sems, recv_sems, after)
    return list(outs[n:])


def _adam_update(w, g, m, v):
    nm = ADAM_B1 * m + (1.0 - ADAM_B1) * g
    nv = ADAM_B2 * v + (1.0 - ADAM_B2) * (g * g)
    m_hat = nm / (1.0 - ADAM_B1 ** ADAM_STEP)
    v_hat = nv / (1.0 - ADAM_B2 ** ADAM_STEP)
    return -ADAM_LR * (m_hat / (jnp.sqrt(v_hat) + ADAM_EPS) + ADAM_WD * w), nm, nv


def _sum_slots(s_ref):
    acc = s_ref[0].astype(F32)
    for j in range(1, N_DEV):
        acc = acc + s_ref[j].astype(F32)
    return acc


ADAM_BLOCK_BYTES = 10 * 1024 * 1024


def adamw_sharded(recvs, w, m, v, name):
    nl, r, c = w.shape
    assert len(recvs) == nl
    per_row = _round_up(c, LANES) * (nl * N_DEV * recvs[0].dtype.itemsize + 7 * 4)
    tr = r
    while tr * per_row > ADAM_BLOCK_BYTES and tr % 16 == 0:
        tr //= 2

    def body(*refs):
        s_refs = refs[:nl]
        w_ref, m_ref, v_ref, g_ref, d_ref, nm_ref, nv_ref = refs[nl:]
        g = _sum_slots(s_refs[0])
        for l in range(1, nl):
            g = jnp.where(pl.program_id(0) == l, _sum_slots(s_refs[l]), g)
        g_ref[...] = g
        d_ref[...], nm_ref[...], nv_ref[...] = _adam_update(w_ref[...], g, m_ref[...], v_ref[...])

    spec = pl.BlockSpec((None, tr, c), lambda l, i: (l, i, 0))
    shp = jax.ShapeDtypeStruct(w.shape, F32)
    return pl.pallas_call(
        body, name=name, grid=(nl, r // tr),
        in_specs=[pl.BlockSpec((N_DEV, tr, c), lambda l, i: (0, i, 0))] * nl + [spec, spec, spec],
        out_specs=[spec] * 4, out_shape=[shp] * 4, compiler_params=_cparams(2))(*recvs, w, m, v)


def reduce_slots(slots, name):
    _, r, lanes = slots.shape
    tr = _tile(r, 2048)

    def body(s_ref, o_ref):
        o_ref[...] = _sum_slots(s_ref)

    return pl.pallas_call(
        body, name=name, grid=(r // tr,),
        in_specs=[pl.BlockSpec((N_DEV, tr, lanes), lambda i: (0, i, 0))],
        out_specs=pl.BlockSpec((tr, lanes), lambda i: (i, 0)),
        out_shape=jax.ShapeDtypeStruct((r, lanes), F32), compiler_params=_cparams(1),
    )(slots)


def adamw(w, g, m, v, name):
    r, lanes = w.shape
    tr = _tile(r, 2048)

    def body(w_ref, g_ref, m_ref, v_ref, d_ref, nm_ref, nv_ref):
        d_ref[...], nm_ref[...], nv_ref[...] = _adam_update(w_ref[...], g_ref[...], m_ref[...], v_ref[...])

    spec = pl.BlockSpec((tr, lanes), lambda i: (i, 0))
    shp = jax.ShapeDtypeStruct((r, lanes), F32)
    return pl.pallas_call(body, name=name, grid=(r // tr,), in_specs=[spec] * 4, out_specs=[spec] * 3,
                          out_shape=[shp] * 3, compiler_params=_cparams(1))(w, g, m, v)


def cols_from_slots(slots, place, width, name):
    _, rows, c = slots.shape
    tr = _tile(rows, 256)

    def body(s_ref, o_ref):
        o_ref[...] = place(jnp.concatenate([s_ref[j] for j in range(N_DEV)], axis=1))

    return pl.pallas_call(
        body, name=name, grid=(rows // tr,),
        in_specs=[pl.BlockSpec((N_DEV, tr, c), lambda i: (0, i, 0))],
        out_specs=pl.BlockSpec((tr, width), lambda i: (i, 0)),
        out_shape=jax.ShapeDtypeStruct((rows, width), slots.dtype), compiler_params=_cparams(1))(slots)


def slots_from_cols(full, pick, c, name):
    rows, wide = full.shape
    tr = _tile(rows, 256)

    def body(x_ref, o_ref):
        v = pick(x_ref[...])
        for j in range(N_DEV):
            o_ref[j] = v[:, j * c:(j + 1) * c]

    return pl.pallas_call(
        body, name=name, grid=(rows // tr,),
        in_specs=[pl.BlockSpec((tr, wide), lambda i: (i, 0))],
        out_specs=pl.BlockSpec((N_DEV, tr, c), lambda i: (0, i, 0)),
        out_shape=jax.ShapeDtypeStruct((N_DEV, rows, c), full.dtype), compiler_params=_cparams(1))(full)


def _token_tile(cfg):
    return min(cfg.S, 256)


def norm_fwd(cfg, x, w, name):
    ts = _token_tile(cfg)
    d = x.shape[1]

    def fn(ins, _):
        xv, wv = ins
        return [rms(xv, wv)], [], []

    return seq_call(name, fn, (1, cfg.S // ts),
                    [(x, (ts, d), lambda g, t: (t, 0)), (w, (1, d), lambda g, t: (0, 0))],
                    [((cfg.S, d), BF16, (ts, d), lambda g, t: (t, 0))])[0]


def norm_bwd(cfg, x, w, dh, dres, name):
    ts = _token_tile(cfg)
    d = x.shape[1]

    def fn(ins, _):
        xv, wv, dhv, drv = ins
        _, vjp = jax.vjp(rms, xv, wv)
        dx, dw = vjp(dhv.astype(F32))
        return [dx + drv], [dw], []

    row = lambda g, t: (t, 0)
    out = seq_call(name, fn, (1, cfg.S // ts),
                   [(x, (ts, d), row), (w, (1, d), lambda g, t: (0, 0)), (dh, (ts, d), row), (dres, (ts, d), row)],
                   [((cfg.S, d), F32, (ts, d), row)],
                   accs=[((1, d), (1, d), lambda g, t: (0, 0))])
    return out[0], out[1]


def head_fwd_bwd(cfg, x, w, target, name):
    ts = _token_tile(cfg)
    d = x.shape[1]

    def fn(ins, _):
        xv, wv, tv = ins
        y, vjp = jax.vjp(rms, xv, wv)
        err = y - tv
        loss = 0.5 * jnp.sum(err * err) / d
        dx, dw = vjp(err / d)
        return [dx], [jnp.full((SUBLANES, LANES), loss, F32), dw], []

    row = lambda g, t: (t, 0)
    fixed = lambda g, t: (0, 0)
    dx, loss, dw = seq_call(name, fn, (1, cfg.S // ts),
                            [(x, (ts, d), row), (w, (1, d), fixed), (target, (ts, d), row)],
                            [((cfg.S, d), F32, (ts, d), row)],
                            accs=[((SUBLANES, LANES), (SUBLANES, LANES), fixed), ((1, d), (1, d), fixed)])
    return loss, dx, dw


def act_fwd(cfg, gate, up, name):
    ts, c = min(cfg.S, 512), gate.shape[2]

    def fn(ins, _):
        return [silu(ins[0]) * ins[1]], [], []

    blk = lambda g, t: (g, t, 0)
    return seq_call(name, fn, (N_DEV, cfg.S // ts),
                    [(gate, (None, ts, c), blk), (up, (None, ts, c), blk)],
                    [(gate.shape, BF16, (None, ts, c), blk)])[0]


def act_bwd(cfg, gate, up, dact, name):
    ts, c = min(cfg.S, 512), gate.shape[2]

    def fn(ins, _):
        gv, uv, dv = ins
        _, vjp = jax.vjp(lambda a, b: silu(a) * b, gv, uv)
        dg, du = vjp(dv.astype(F32))
        return [dg, du], [], []

    blk = lambda g, t: (g, t, 0)
    shp = (gate.shape, BF16, (None, ts, c), blk)
    return seq_call(name, fn, (N_DEV, cfg.S // ts),
                    [(gate, (None, ts, c), blk), (up, (None, ts, c), blk), (dact, (None, ts, c), blk)], [shp, shp])


CONV_COLS = 256
HALO = SUBLANES


def _shift_down(x, halo, j):
    if j == 0:
        return x
    r8 = _rows_iota(halo.shape)
    top = jnp.where(r8 >= j, pltpu.roll(x[:HALO], j, 0), pltpu.roll(halo, j, 0))
    return jnp.concatenate([top, pltpu.roll(x, j, 0)[HALO:]], axis=0)


def _shift_up(x, halo, j):
    if j == 0:
        return x
    n = x.shape[0]
    r8 = _rows_iota(halo.shape)
    bot = jnp.where(r8 < HALO - j, pltpu.roll(x[n - HALO:], HALO - j, 0), pltpu.roll(halo, HALO - j, 0))
    return jnp.concatenate([pltpu.roll(x, n - j, 0)[:n - HALO], bot], axis=0)


def _conv_tile(cfg):
    return min(cfg.S, 512)


def conv_fwd(cfg, src, col0, width, w, b, name):
    tt, cb = _conv_tile(cfg), CONV_COLS
    c0, hb = col0 // cb, tt // HALO
    nt = cfg.S // tt

    def body(x_ref, h_ref, w_ref, b_ref, o_ref):
        t = pl.program_id(1)
        x = x_ref[...]
        halo = jnp.where(t > 0, h_ref[...], 0.0)
        wv = w_ref[...]
        acc = b_ref[...] + wv[CONV_WIDTH - 1:CONV_WIDTH] * x
        for j in range(1, CONV_WIDTH):
            acc = acc + wv[CONV_WIDTH - 1 - j:CONV_WIDTH - j] * _shift_down(x, halo, j)
        o_ref[...] = acc

    return pl.pallas_call(
        body, name=name, grid=(width // cb, nt),
        in_specs=[pl.BlockSpec((tt, cb), lambda c, t: (t, c0 + c)),
                  pl.BlockSpec((HALO, cb), lambda c, t: (jnp.maximum(t * hb - 1, 0), c0 + c)),
                  pl.BlockSpec((CONV_WIDTH, cb), lambda c, t: (0, c)),
                  pl.BlockSpec((1, cb), lambda c, t: (0, c))],
        out_specs=pl.BlockSpec((tt, cb), lambda c, t: (t, c)),
        out_shape=jax.ShapeDtypeStruct((cfg.S, width), F32), compiler_params=_cparams(2),
    )(src, src, w, b)


def conv_bwd(cfg, src, col0, width, w, dy, name):
    tt, cb = _conv_tile(cfg), CONV_COLS
    c0, hb = col0 // cb, tt // HALO
    nt = cfg.S // tt

    def body(x_ref, h_ref, w_ref, dy_ref, dh_ref, dx_ref, dw_ref, db_ref):
        t = pl.program_id(1)

        @pl.when(t == 0)
        def _():
            dw_ref[...] = jnp.zeros_like(dw_ref)
            db_ref[...] = jnp.zeros_like(db_ref)

        x = x_ref[...]
        halo = jnp.where(t > 0, h_ref[...], 0.0)
        dy = dy_ref[...]
        dhalo = jnp.where(t < nt - 1, dh_ref[...], 0.0)
        wv = w_ref[...]
        dx = wv[CONV_WIDTH - 1:CONV_WIDTH] * dy
        rows = [jnp.sum(dy * x, axis=0, keepdims=True)]
        for j in range(1, CONV_WIDTH):
            dx = dx + wv[CONV_WIDTH - 1 - j:CONV_WIDTH - j] * _shift_up(dy, dhalo, j)
            rows.insert(0, jnp.sum(dy * _shift_down(x, halo, j), axis=0, keepdims=True))
        dx_ref[...] = dx.astype(dx_ref.dtype)
        dw_ref[...] += jnp.concatenate(rows, axis=0)
        db_ref[...] += jnp.sum(dy, axis=0, keepdims=True)

    return pl.pallas_call(
        body, name=name, grid=(width // cb, nt),
        in_specs=[pl.BlockSpec((tt, cb), lambda c, t: (t, c0 + c)),
                  pl.BlockSpec((HALO, cb), lambda c, t: (jnp.maximum(t * hb - 1, 0), c0 + c)),
                  pl.BlockSpec((CONV_WIDTH, cb), lambda c, t: (0, c)),
                  pl.BlockSpec((tt, cb), lambda c, t: (t, c)),
                  pl.BlockSpec((HALO, cb), lambda c, t: (jnp.minimum((t + 1) * hb, nt * hb - 1), c))],
        out_specs=[pl.BlockSpec((tt, cb), lambda c, t: (t, c)),
                   pl.BlockSpec((CONV_WIDTH, cb), lambda c, t: (0, c)),
                   pl.BlockSpec((1, cb), lambda c, t: (0, c))],
        out_shape=[jax.ShapeDtypeStruct((cfg.S, width), BF16), jax.ShapeDtypeStruct((CONV_WIDTH, width), F32),
                   jax.ShapeDtypeStruct((1, width), F32)],
        compiler_params=_cparams(2),
    )(src, src, w, dy, dy)


def _gla_core(gh, q, k, v, g, glr, wg, bg, wn, st):
    n = glr.shape[0]
    causal = _lower_tri(n)
    outs, new = [], []
    for h in range(gh):
        log_a = log_sigmoid(bdot(glr, wg[h], 1, 0) + bg[h]) * (1.0 / GATE_NORM)
        bcum = cumsum_rows(log_a)
        b_last, b_mid = _row(bcum, n - 1), _row(bcum, n // 2)
        qs = q[h] * (HEAD_K ** -0.5)
        scores = jnp.where(causal, bdot(qs * jnp.exp(bcum - b_mid), k[h] * jnp.exp(b_mid - bcum), 1, 1), 0.0)
        o = bdot(scores, v[h], 1, 0) + bdot(qs * jnp.exp(bcum), st[h], 1, 1)
        new.append(st[h] * jnp.exp(b_last) + bdot(v[h], k[h] * jnp.exp(b_last - bcum), 0, 0))
        outs.append(rms(o, wn) * silu(g[h]))
    return jnp.concatenate(outs, axis=1), new


def _gla_ins(cfg, p0, wg, bg, wn, tmap):
    gh = cfg.GH
    ins = []
    for h in range(gh):
        ins.append((p0, (CHUNK, HEAD_K), lambda g, t, h=h: (tmap(t), h)))
    for h in range(gh):
        ins.append((p0, (CHUNK, HEAD_K), lambda g, t, h=h: (tmap(t), gh + h)))
    for h in range(gh):
        ins.append((p0, (CHUNK, HEAD_V), lambda g, t, h=h: (tmap(t), gh + h)))
    for h in range(gh):
        ins.append((p0, (CHUNK, HEAD_V), lambda g, t, h=h: (tmap(t), 2 * gh + h)))
    ins.append((p0, (CHUNK, LANES), lambda g, t: (tmap(t), 10 * gh)))
    for h in range(gh):
        ins.append((wg, (LANES, HEAD_K), lambda g, t, h=h: (0, h)))
    for h in range(gh):
        ins.append((bg, (1, HEAD_K), lambda g, t, h=h: (0, h)))
    ins.append((wn, (1, HEAD_V), lambda g, t: (0, 0)))
    return ins


def _gla_unpack(gh, vals):
    q, k, v, g = (vals[i * gh:(i + 1) * gh] for i in range(4))
    glr = vals[4 * gh]
    wg = vals[4 * gh + 1:5 * gh + 1]
    bg = vals[5 * gh + 1:6 * gh + 1]
    wn = vals[6 * gh + 1]
    return q, k, v, g, glr, wg, bg, wn, vals[6 * gh + 2:]


def gla_fwd(cfg, p0, wg, bg, wn, name):
    gh, nc = cfg.GH, cfg.S // CHUNK

    def fn(ins, st):
        q, k, v, g, glr, wgv, bgv, wnv, _ = _gla_unpack(gh, ins)
        out, new = _gla_core(gh, q, k, v, g, glr, wgv, bgv, wnv, st)
        return [out, jnp.stack(st)], [], new

    return seq_call(name, fn, (1, nc), _gla_ins(cfg, p0, wg, bg, wn, lambda t: t),
                    [((cfg.S, cfg.DV), BF16, (CHUNK, cfg.DV), lambda g, t: (t, 0)),
                     ((nc, gh, HEAD_V, HEAD_K), F32, (None, gh, HEAD_V, HEAD_K), lambda g, t: (t, 0, 0, 0))],
                    carries=[(HEAD_V, HEAD_K)] * gh)


def gla_bwd(cfg, p0, wg, bg, wn, states, dout, name):
    gh, nc = cfg.GH, cfg.S // CHUNK
    rev = lambda t: nc - 1 - t

    def fn(ins, dst):
        q, k, v, g, glr, wgv, bgv, wnv, rest = _gla_unpack(gh, ins)
        st_all, do = rest
        st = [st_all[h] for h in range(gh)]
        _, vjp = jax.vjp(functools.partial(_gla_core, gh), q, k, v, g, glr, wgv, bgv, wnv, st)
        dq, dk, dv, dg, dglr, dwg, dbg, dwn, dstate = vjp((do.astype(F32), list(dst)))
        return ([jnp.concatenate(list(dq) + list(dk) + list(dv) + list(dg), axis=1), dglr],
                [jnp.concatenate(dwg, axis=1), jnp.concatenate(dbg, axis=1), dwn], dstate)

    ins = _gla_ins(cfg, p0, wg, bg, wn, rev)
    ins.append((states, (None, gh, HEAD_V, HEAD_K), lambda g, t: (rev(t), 0, 0, 0)))
    ins.append((dout, (CHUNK, cfg.DV), lambda g, t: (rev(t), 0)))
    wide = 2 * cfg.DK + 2 * cfg.DV
    fixed = lambda g, t: (0, 0)
    return seq_call(name, fn, (1, nc), ins,
                    [((cfg.S, wide), BF16, (CHUNK, wide), lambda g, t: (rev(t), 0)),
                     ((cfg.S, LANES), BF16, (CHUNK, LANES), lambda g, t: (rev(t), 0))],
                    accs=[((LANES, cfg.DK), (LANES, cfg.DK), fixed), ((1, cfg.DK), (1, cfg.DK), fixed),
                          ((1, HEAD_V), (1, HEAD_V), fixed)],
                    carries=[(HEAD_V, HEAD_K)] * gh)


def _lru_core(xc, gate, wa, wi, ba, bi, lam, h_in):
    r = jax.nn.sigmoid(bdot(xc, wa, 1, 0) + ba)
    i = jax.nn.sigmoid(bdot(xc, wi, 1, 0) + bi)
    log_a = LRU_C * r * log_sigmoid(lam)
    a = jnp.exp(log_a)
    u = jnp.sqrt(-_expm1(2.0 * log_a)) * (i * xc)
    first = _rows_iota(a.shape) == 0
    h = lin_scan(a, u + jnp.where(first, a * h_in, 0.0))
    return h * gelu_tanh(gate), _row(h, a.shape[0] - 1)


def _lru_tile(cfg):
    return min(cfg.S, 512)


def _lru_ins(cfg, xc, p0, wa, wi, ba, bi, lam, tmap):
    tt, gh = _lru_tile(cfg), cfg.GH
    vec = lambda g, t: (0, g)
    return [(xc, (tt, LRU_BLOCK), lambda g, t: (tmap(t), g)),
            (p0, (tt, LRU_BLOCK), lambda g, t: (tmap(t), 8 * gh + g)),
            (wa, (None, LRU_BLOCK, LRU_BLOCK), lambda g, t: (g, 0, 0)),
            (wi, (None, LRU_BLOCK, LRU_BLOCK), lambda g, t: (g, 0, 0)),
            (ba, (1, LRU_BLOCK), vec), (bi, (1, LRU_BLOCK), vec), (lam, (1, LRU_BLOCK), vec)]


def lru_fwd(cfg, xc, p0, wa, wi, ba, bi, lam, name):
    tt, nb = _lru_tile(cfg), cfg.NB
    nt = cfg.S // tt

    def fn(ins, c):
        out, h_last = _lru_core(*ins, c[0])
        return [out, c[0]], [], [h_last]

    return seq_call(name, fn, (nb, nt), _lru_ins(cfg, xc, p0, wa, wi, ba, bi, lam, lambda t: t),
                    [((cfg.S, cfg.W), BF16, (tt, LRU_BLOCK), lambda g, t: (t, g)),
                     ((nb, nt, 1, LRU_BLOCK), F32, (None, None, 1, LRU_BLOCK), lambda g, t: (g, t, 0, 0))],
                    carries=[(1, LRU_BLOCK)])


def lru_bwd(cfg, xc, p0, wa, wi, ba, bi, lam, states, dout, name):
    tt, nb = _lru_tile(cfg), cfg.NB
    nt = cfg.S // tt
    rev = lambda t: nt - 1 - t

    def fn(ins, c):
        *fwd_ins, h_in, do = ins
        _, vjp = jax.vjp(_lru_core, *fwd_ins, h_in)
        dxc, dgate, dwa, dwi, dba, dbi, dlam, dh = vjp((do.astype(F32), c[0]))
        return [dxc, dgate], [dwa, dwi, dba, dbi, dlam], [dh]

    ins = _lru_ins(cfg, xc, p0, wa, wi, ba, bi, lam, rev)
    ins.append((states, (None, None, 1, LRU_BLOCK), lambda g, t: (g, rev(t), 0, 0)))
    ins.append((dout, (tt, LRU_BLOCK), lambda g, t: (rev(t), g)))
    mat = ((nb, LRU_BLOCK, LRU_BLOCK), (None, LRU_BLOCK, LRU_BLOCK), lambda g, t: (g, 0, 0))
    vec = ((1, cfg.W), (1, LRU_BLOCK), lambda g, t: (0, g))
    return seq_call(name, fn, (nb, nt), ins,
                    [((cfg.S, cfg.W), F32, (tt, LRU_BLOCK), lambda g, t: (rev(t), g)),
                     ((cfg.S, cfg.W), BF16, (tt, LRU_BLOCK), lambda g, t: (rev(t), g))],
                    accs=[mat, mat, vec, vec, vec], carries=[(1, LRU_BLOCK)])


def _ssd_core(xc, bc, cc, z, dt_raw, dt_bias, a_log, d_skip, gn, st):
    n = xc.shape[0]
    x, bm, cm = silu(xc), silu(bc), silu(cc)
    dt = softplus(dt_raw + dt_bias)
    acs = cumsum_rows(dt * (-jnp.exp(a_log)))
    acs_t = acs.T
    acs_e, dt_e = _expand_heads(acs), _expand_heads(dt)
    last_e = _expand_heads(_row(acs, n - 1))
    causal = _lower_tri(n)
    cb = bdot(cm, bm, 1, 1)
    xdt = x * dt_e
    y_diag = []
    for h, xh in enumerate(_split_heads(xdt)):
        seg = acs[:, h:h + 1] - acs_t[h:h + 1, :]
        decay = jnp.where(causal, jnp.exp(jnp.minimum(seg, 0.0)), 0.0)
        y_diag.append(bdot(cb * decay, xh, 1, 0))
    y = jnp.concatenate(y_diag, axis=1) + bdot(cm, st, 1, 0) * jnp.exp(acs_e)
    new = st * jnp.exp(last_e) + bdot(bm, xdt * jnp.exp(last_e - acs_e), 0, 0)
    y = (y + _expand_heads(d_skip) * x) * silu(z)
    return rms(y, gn), new


def _ssd_ins(cfg, xc, p1, dt_raw, dt_bias, a_log, d_skip, gn, tmap):
    ng = cfg.NG
    vec = lambda g, t: (g, 0, 0)
    return [(xc, (CHUNK, SSD_GW), lambda g, t: (tmap(t), g)),
            (xc, (CHUNK, SSD_N), lambda g, t: (tmap(t), 4 * ng + g)),
            (xc, (CHUNK, SSD_N), lambda g, t: (tmap(t), 5 * ng + g)),
            (p1, (CHUNK, SSD_GW), lambda g, t: (tmap(t), g)),
            (dt_raw, (None, CHUNK, LANES), lambda g, t: (g, tmap(t), 0)),
            (dt_bias, (None, 1, LANES), vec), (a_log, (None, 1, LANES), vec), (d_skip, (None, 1, LANES), vec),
            (gn, (1, SSD_GW), lambda g, t: (0, g))]


def ssd_fwd(cfg, xc, p1, dt_raw, dt_bias, a_log, d_skip, gn, name):
    ng, nc = cfg.NG, cfg.S // CHUNK

    def fn(ins, c):
        out, new = _ssd_core(*ins, c[0])
        return [out, c[0]], [], [new]

    return seq_call(name, fn, (ng, nc), _ssd_ins(cfg, xc, p1, dt_raw, dt_bias, a_log, d_skip, gn, lambda t: t),
                    [((cfg.S, cfg.DI), BF16, (CHUNK, SSD_GW), lambda g, t: (t, g)),
                     ((ng, nc, SSD_N, SSD_GW), F32, (None, None, SSD_N, SSD_GW), lambda g, t: (g, t, 0, 0))],
                    carries=[(SSD_N, SSD_GW)])


def ssd_bwd(cfg, xc, p1, dt_raw, dt_bias, a_log, d_skip, gn, states, dout, name):
    ng, nc = cfg.NG, cfg.S // CHUNK
    rev = lambda t: nc - 1 - t

    def fn(ins, c):
        *fwd_ins, st, do = ins
        _, vjp = jax.vjp(_ssd_core, *fwd_ins, st)
        dxc, dbc, dcc, dz, ddt, dbias, dalog, dskip, dgn, dst = vjp((do.astype(F32), c[0]))
        return [dxc, dbc, dcc, dz, ddt], [dbias, dalog, dskip, dgn], [dst]

    ins = _ssd_ins(cfg, xc, p1, dt_raw, dt_bias, a_log, d_skip, gn, rev)
    ins.append((states, (None, None, SSD_N, SSD_GW), lambda g, t: (g, rev(t), 0, 0)))
    ins.append((dout, (CHUNK, SSD_GW), lambda g, t: (rev(t), g)))
    col = lambda g, t: (rev(t), g)
    vec = ((ng, 1, LANES), (None, 1, LANES), lambda g, t: (g, 0, 0))
    return seq_call(name, fn, (ng, nc), ins,
                    [((cfg.S, cfg.DI), F32, (CHUNK, SSD_GW), col),
                     ((cfg.S, ng * SSD_N), F32, (CHUNK, SSD_N), col),
                     ((cfg.S, ng * SSD_N), F32, (CHUNK, SSD_N), col),
                     ((cfg.S, cfg.DI), BF16, (CHUNK, SSD_GW), col),
                     ((ng, cfg.S, LANES), F32, (None, CHUNK, LANES), lambda g, t: (g, rev(t), 0))],
                    accs=[vec, vec, vec, ((1, cfg.DI), (1, SSD_GW), lambda g, t: (0, g))],
                    carries=[(SSD_N, SSD_GW)])


PACK_ALIGN = SUBLANES * LANES
PACK_ROWS = 256


def pack(arrays):
    pieces = []
    for a in arrays:
        flat = a.reshape(-1).astype(F32)
        pad = _round_up(flat.shape[0], PACK_ALIGN) - flat.shape[0]
        pieces.append(jnp.pad(flat, (0, pad)) if pad else flat)
    flat = jnp.concatenate(pieces)
    pad = _round_up(flat.shape[0], PACK_ROWS * LANES) - flat.shape[0]
    return jnp.pad(flat, (0, pad)).reshape(-1, LANES)


def unpack(buf, shapes):
    lead = buf.shape[:-2]
    flat = buf.reshape(lead + (-1,))
    out, off = [], 0
    for s in shapes:
        n = math.prod(s)
        out.append(flat[..., off:off + n].reshape(lead + tuple(s)))
        off += _round_up(n, PACK_ALIGN)
    return out


def _slots_to_cols(slots):
    return slots.transpose(1, 0, 2).reshape(slots.shape[1], -1)


def _even_in_padded(cfg, w):
    main = 2 * cfg.DK + 2 * cfg.DV
    return jnp.concatenate([w[:, :main], w[:, main + GATE_RANK:], w[:, main:main + GATE_RANK],
                            jnp.zeros((w.shape[0], cfg.EP - cfg.EVEN_IN), w.dtype)], axis=1)


def _even_in_unpadded(cfg, wp):
    main = 2 * cfg.DK + 2 * cfg.DV
    rest = main + 2 * cfg.W
    return jnp.concatenate([wp[:, :main], wp[:, rest:rest + GATE_RANK], wp[:, main:rest]], axis=1)


def _odd_in_padded(cfg, w):
    return jnp.concatenate([w, jnp.zeros((w.shape[0], cfg.OP - cfg.ODD_IN), w.dtype)], axis=1)


def _odd_in_unpadded(cfg, wp):
    return wp[:, :cfg.ODD_IN]


def _group_lanes(cfg, v):
    lead = v.shape[:-1]
    g = jnp.moveaxis(v.reshape(lead + (cfg.NG, SSD_HG)), -2, 0)
    return jnp.pad(g, [(0, 0)] * (g.ndim - 1) + [(0, LANES - SSD_HG)])


def _ungroup_lanes(cfg, g):
    v = jnp.moveaxis(g[..., :SSD_HG], 0, -2)
    return v.reshape(v.shape[:-2] + (cfg.NH,))


def train_step(cfg, p, loss_target):
    S, D = cfg.S, cfg.D
    me = 4 * lax.axis_index("x") + 2 * lax.axis_index("y") + lax.axis_index("c")

    big = ["ev_w_in", "ev_w_out", "od_w_in", "od_w_out", "ffn_w_gate", "ffn_w_up", "ffn_w_down"]
    small_sharded = ["ev_gla_w_gate", "ev_lru_conv_w", "od_norm", "od_conv_w", "od_conv_b", "od_gnorm"]
    replicated = ["ev_norm", "ev_gla_b_gate", "ev_gla_w_onorm", "ev_lru_conv_b", "ev_lru_w_a", "ev_lru_b_a",
                  "ev_lru_w_i", "ev_lru_b_i", "ev_lru_lam", "od_dt_bias", "od_a_log", "od_d_skip", "ffn_norm",
                  "final_norm"]

    wb = {n: p[n].astype(BF16) for n in big}
    ffn_items = lambda l: [(wb["ffn_w_gate"], l), (wb["ffn_w_up"], l), (wb["ffn_w_down"], l)]
    ss_shapes = [p[n].shape for n in small_sharded]
    groups = [[(pack([p[n] for n in small_sharded]), None), (wb["ev_w_in"], 0), (wb["ev_w_out"], 0)], ffn_items(0),
              [(wb["od_w_in"], 0), (wb["od_w_out"], 0)], ffn_items(1)]
    gathers, tokens = zip(*[exchange_start(g, True, f"gather_start{i}") for i, g in enumerate(groups)])
    all_started = tokens[0][:1, :1] + tokens[1][:1, :1] + tokens[2][:1, :1] + tokens[3][:1, :1]

    ss_all, gw_ev_in, gw_ev_out = exchange_wait(gathers[0], all_started, "gather_wait0")
    gs = dict(zip(small_sharded, unpack(ss_all, ss_shapes)))
    w_ev_in = cols_from_slots(gw_ev_in, functools.partial(_even_in_padded, cfg), cfg.EP, "ev_w_in_cols")
    w_ev_out = gw_ev_out.reshape(D, D)

    gla_wg = jnp.pad(_slots_to_cols(gs["ev_gla_w_gate"][:, 0]), ((0, LANES - GATE_RANK), (0, 0)))
    lru_cw = _slots_to_cols(gs["ev_lru_conv_w"][:, 0])
    od_norm = gs["od_norm"].transpose(1, 0, 2).reshape(1, D)
    od_cw = _slots_to_cols(gs["od_conv_w"][:, 0])
    od_cb = gs["od_conv_b"].transpose(1, 0, 2).reshape(1, cfg.CD)
    od_gn = gs["od_gnorm"].transpose(1, 0, 2).reshape(1, cfg.DI)

    x0 = p["x"][0]
    target = loss_target[0]
    ev_norm = p["ev_norm"] + all_started
    bg = p["ev_gla_b_gate"]
    wn = p["ev_gla_w_onorm"]
    lru_cb = p["ev_lru_conv_b"]
    wa, wi = p["ev_lru_w_a"][0], p["ev_lru_w_i"][0]
    ba, bi, lam = p["ev_lru_b_a"], p["ev_lru_b_i"], p["ev_lru_lam"]
    dt_bias, a_log, d_skip = (_group_lanes(cfg, p[n]) for n in ("od_dt_bias", "od_a_log", "od_d_skip"))
    ffn_norm = [p["ffn_norm"][l:l + 1] for l in range(2)]
    final_norm = p["final_norm"].reshape(1, D)

    def ffn_forward(l, x):
        w_gate, w_up, w_down = exchange_wait(gathers[1 + 2 * l], x, f"gather_wait{1 + 2 * l}")
        h = norm_fwd(cfg, x, ffn_norm[l], f"ffn{l}_norm")
        gate = matmul(h, w_gate, b_slot="n", name=f"ffn{l}_gate")
        up = matmul(h, w_up, b_slot="n", name=f"ffn{l}_up")
        act = act_fwd(cfg, gate, up, f"ffn{l}_act")
        out = matmul(act, w_down, a_slot="k", b_slot="k", res=x, name=f"ffn{l}_down", tn=512)
        return out, (h, gate, up, act, w_gate, w_up, w_down)

    h0 = norm_fwd(cfg, x0, ev_norm, "ev_norm")
    p0 = matmul(h0, w_ev_in, name="ev_in", tn=768)
    gla_out, gla_states = gla_fwd(cfg, p0, gla_wg, bg, wn, "gla_fwd")
    lru_col = 2 * cfg.DK + 2 * cfg.DV
    lru_xc = conv_fwd(cfg, p0, lru_col, cfg.W, lru_cw, lru_cb, "lru_conv")
    lru_out, lru_states = lru_fwd(cfg, lru_xc, p0, wa, wi, ba, bi, lam, "lru_fwd")
    mix = jnp.concatenate([gla_out, lru_out], axis=1)
    x1 = matmul(mix, w_ev_out, res=x0, name="ev_out")
    x2, ffn0_saved = ffn_forward(0, x1)

    gw_od_in, gw_od_out = exchange_wait(gathers[2], x2, "gather_wait2")
    w_od_in = cols_from_slots(gw_od_in, functools.partial(_odd_in_padded, cfg), cfg.OP, "od_w_in_cols")
    w_od_out = gw_od_out.reshape(cfg.DI, D)
    h2 = norm_fwd(cfg, x2, od_norm, "od_norm")
    p1 = matmul(h2, w_od_in, name="od_in", tn=768)
    od_xc = conv_fwd(cfg, p1, cfg.DI, cfg.CD, od_cw, od_cb, "od_conv")
    dt_col = cfg.DI + cfg.CD
    dt_raw = _group_lanes(cfg, p1[:, dt_col:dt_col + cfg.NH])
    ssd_out, ssd_states = ssd_fwd(cfg, od_xc, p1, dt_raw, dt_bias, a_log, d_skip, od_gn, "ssd_fwd")
    x3 = matmul(ssd_out, w_od_out, res=x2, name="od_out")
    x4, ffn1_saved = ffn_forward(1, x3)

    loss_part, dx4, d_final_norm = head_fwd_bwd(cfg, x4, final_norm, target, "head")
    loss = lax.psum(loss_part[0, 0], MESH_AXES)

    def ffn_backward(l, x, saved, dx_out, after):
        h, gate, up, act, w_gate, w_up, w_down = saved
        dact = matmul(dx_out, w_down, tb=True, b_slot="n", after=after, name=f"ffn{l}_dact")
        d_down = matmul(act, dx_out, ta=True, a_slot="m", out_dtype=BF16, name=f"ffn{l}_dwdown")
        dgate, dup = act_bwd(cfg, gate, up, dact, f"ffn{l}_act_bwd")
        dh = matmul(dgate, w_gate, tb=True, a_slot="k", b_slot="k", name=f"ffn{l}_dh_gate", tn=512)
        dh = matmul(dup, w_up, tb=True, a_slot="k", b_slot="k", res=dh, name=f"ffn{l}_dh_up", tn=512)
        d_gate = matmul(h, dgate, ta=True, b_slot="n", out_dtype=BF16, name=f"ffn{l}_dwgate")
        d_up = matmul(h, dup, ta=True, b_slot="n", out_dtype=BF16, name=f"ffn{l}_dwup")
        dx, dnorm = norm_bwd(cfg, x, ffn_norm[l], dh, dx_out, f"ffn{l}_norm_bwd")
        sent, token = exchange_start([(d_gate, None), (d_up, None), (d_down, None)], False, f"grads_start_ffn{l}")
        return dx, dnorm, sent, token

    dx3, d_ffn_norm1, sent_ffn1, token = ffn_backward(1, x3, ffn1_saved, dx4, None)

    d_ssd_out = matmul(dx3, w_od_out, tb=True, after=token, name="od_dmix")
    d_od_out = matmul(ssd_out, dx3, ta=True, out_dtype=BF16, name="od_dwout")
    dxs, dbm, dcm, dz, d_dt_raw, d_dt_bias, d_a_log, d_d_skip, d_od_gn = ssd_bwd(
        cfg, od_xc, p1, dt_raw, dt_bias, a_log, d_skip, od_gn, ssd_states, d_ssd_out, "ssd_bwd")
    d_od_xc = jnp.concatenate([dxs, dbm, dcm], axis=1)
    d_xbc, d_od_cw, d_od_cb = conv_bwd(cfg, p1, cfg.DI, cfg.CD, od_cw, d_od_xc, "od_conv_bwd")
    d_dt = _ungroup_lanes(cfg, d_dt_raw).astype(BF16)
    dp1 = jnp.concatenate([dz, d_xbc, d_dt, jnp.zeros((S, cfg.OP - cfg.ODD_IN), BF16)], axis=1)
    dh2 = matmul(dp1, w_od_in, tb=True, name="od_dh", tk=1536)
    d_od_in = matmul(h2, dp1, ta=True, out_dtype=BF16, name="od_dwin", tn=768)
    dx2, d_od_norm = norm_bwd(cfg, x2, od_norm, dh2, dx3, "od_norm_bwd")
    d_od_in_slots = slots_from_cols(d_od_in, functools.partial(_odd_in_unpadded, cfg), p["od_w_in"].shape[2],
                                    "od_dwin_slots")
    sent_od, token = exchange_start([(d_od_in_slots, None), (d_od_out.reshape((N_DEV,) + p["od_w_out"].shape[1:]), None)],
                                    False, "grads_start_od")

    dx1, d_ffn_norm0, sent_ffn0, token = ffn_backward(0, x1, ffn0_saved, dx2, token)

    d_mix = matmul(dx1, w_ev_out, tb=True, after=token, name="ev_dmix")
    d_ev_out = matmul(mix, dx1, ta=True, out_dtype=BF16, name="ev_dwout")
    d_qkvg, d_glr, d_gla_wg, d_bg, d_wn = gla_bwd(cfg, p0, gla_wg, bg, wn, gla_states, d_mix[:, :cfg.DV], "gla_bwd")
    d_lru_xc, d_gate_br, d_wa, d_wi, d_ba, d_bi, d_lam = lru_bwd(
        cfg, lru_xc, p0, wa, wi, ba, bi, lam, lru_states, d_mix[:, cfg.DV:], "lru_bwd")
    d_xbr, d_lru_cw, d_lru_cb = conv_bwd(cfg, p0, lru_col, cfg.W, lru_cw, d_lru_xc, "lru_conv_bwd")
    dp0 = jnp.concatenate([d_qkvg, d_xbr, d_gate_br, d_glr, jnp.zeros((S, cfg.EP - lru_col - 2 * cfg.W - LANES), BF16)],
                          axis=1)
    dh0 = matmul(dp0, w_ev_in, tb=True, name="ev_dh", tk=1792)
    d_ev_in = matmul(h0, dp0, ta=True, out_dtype=BF16, name="ev_dwin", tn=768)
    grad_x, d_ev_norm = norm_bwd(cfg, x0, ev_norm, dh0, dx1, "ev_norm_bwd")

    d_ev_in_slots = slots_from_cols(d_ev_in, functools.partial(_even_in_unpadded, cfg), p["ev_w_in"].shape[2],
                                    "ev_dwin_slots")
    sent_ev, token = exchange_start([(d_ev_in_slots, None), (d_ev_out.reshape((N_DEV,) + p["ev_w_out"].shape[1:]), None)],
                                    False, "grads_start_ev")
    gate1, up1, down1 = exchange_wait(sent_ffn1, token, "grads_wait_ffn1")
    r_od_in, r_od_out = exchange_wait(sent_od, grad_x, "grads_wait_od")
    gate0, up0, down0 = exchange_wait(sent_ffn0, grad_x, "grads_wait_ffn0")
    r_ev_in, r_ev_out = exchange_wait(sent_ev, grad_x, "grads_wait_ev")
    recv = {"ev_w_in": [r_ev_in], "ev_w_out": [r_ev_out], "od_w_in": [r_od_in], "od_w_out": [r_od_out],
            "ffn_w_gate": [gate0, gate1], "ffn_w_up": [up0, up1], "ffn_w_down": [down0, down1]}
    out = {"loss": loss, "grad_x": grad_x[None]}
    for n in big:
        out["grad_" + n], out["delta_" + n], out["new_m_" + n], out["new_v_" + n] = adamw_sharded(
            recv[n], p[n], p["m_" + n], p["v_" + n], "adamw_" + n)

    small_full = {
        "ev_gla_w_gate": d_gla_wg[:GATE_RANK][None], "ev_lru_conv_w": d_lru_cw[None], "od_norm": d_od_norm,
        "od_conv_w": d_od_cw[None], "od_conv_b": d_od_cb, "od_gnorm": d_od_gn,
        "ev_norm": d_ev_norm, "ev_gla_b_gate": d_bg, "ev_gla_w_onorm": d_wn, "ev_lru_conv_b": d_lru_cb,
        "ev_lru_w_a": d_wa[None], "ev_lru_b_a": d_ba, "ev_lru_w_i": d_wi[None], "ev_lru_b_i": d_bi,
        "ev_lru_lam": d_lam, "od_dt_bias": _ungroup_lanes(cfg, d_dt_bias), "od_a_log": _ungroup_lanes(cfg, d_a_log),
        "od_d_skip": _ungroup_lanes(cfg, d_d_skip), "ffn_norm": jnp.concatenate([d_ffn_norm0, d_ffn_norm1], axis=0),
        "final_norm": d_final_norm.reshape(D),
    }
    small = small_sharded + replicated
    small_all = exchange([pack([small_full[n] for n in small])], True, "gather_small_grads")[0]
    g_small = dict(zip(small, unpack(reduce_slots(small_all, "sum_small_grads"), [small_full[n].shape for n in small])))
    for n in small_sharded:
        width = p[n].shape[-1]
        g_small[n] = lax.dynamic_slice_in_dim(g_small[n], me * width, width, axis=g_small[n].ndim - 1)
    shapes = [p[n].shape for n in small]
    g_buf = pack([g_small[n] for n in small])
    delta, new_m, new_v = adamw(pack([p[n] for n in small]), g_buf, pack([p["m_" + n] for n in small]),
                                pack([p["v_" + n] for n in small]), "adamw_small")
    for kind, buf in (("grad_", g_buf), ("delta_", delta), ("new_m_", new_m), ("new_v_", new_v)):
        for n, a in zip(small, unpack(buf, shapes)):
            out[kind + n] = a
    return out


WEIGHTS = ['ev_norm', 'ev_w_in', 'ev_gla_w_gate', 'ev_gla_b_gate', 'ev_gla_w_onorm', 'ev_lru_conv_w', 'ev_lru_conv_b',
           'ev_lru_w_a', 'ev_lru_b_a', 'ev_lru_w_i', 'ev_lru_b_i', 'ev_lru_lam', 'ev_w_out', 'od_norm', 'od_w_in',
           'od_conv_w', 'od_conv_b', 'od_dt_bias', 'od_a_log', 'od_d_skip', 'od_gnorm', 'od_w_out', 'ffn_norm',
           'ffn_w_gate', 'ffn_w_up', 'ffn_w_down', 'final_norm']


def kernel(x, ev_norm, ev_w_in, ev_gla_w_gate, ev_gla_b_gate, ev_gla_w_onorm, ev_lru_conv_w, ev_lru_conv_b, ev_lru_w_a, ev_lru_b_a, ev_lru_w_i, ev_lru_b_i, ev_lru_lam, ev_w_out, od_norm, od_w_in, od_conv_w, od_conv_b, od_dt_bias, od_a_log, od_d_skip, od_gnorm, od_w_out, ffn_norm, ffn_w_gate, ffn_w_up, ffn_w_down, final_norm, loss_target, m_ev_norm, m_ev_w_in, m_ev_gla_w_gate, m_ev_gla_b_gate, m_ev_gla_w_onorm, m_ev_lru_conv_w, m_ev_lru_conv_b, m_ev_lru_w_a, m_ev_lru_b_a, m_ev_lru_w_i, m_ev_lru_b_i, m_ev_lru_lam, m_ev_w_out, m_od_norm, m_od_w_in, m_od_conv_w, m_od_conv_b, m_od_dt_bias, m_od_a_log, m_od_d_skip, m_od_gnorm, m_od_w_out, m_ffn_norm, m_ffn_w_gate, m_ffn_w_up, m_ffn_w_down, m_final_norm, v_ev_norm, v_ev_w_in, v_ev_gla_w_gate, v_ev_gla_b_gate, v_ev_gla_w_onorm, v_ev_lru_conv_w, v_ev_lru_conv_b, v_ev_lru_w_a, v_ev_lru_b_a, v_ev_lru_w_i, v_ev_lru_b_i, v_ev_lru_lam, v_ev_w_out, v_od_norm, v_od_w_in, v_od_conv_w, v_od_conv_b, v_od_dt_bias, v_od_a_log, v_od_d_skip, v_od_gnorm, v_od_w_out, v_ffn_norm, v_ffn_w_gate, v_ffn_w_up, v_ffn_w_down, v_final_norm):
    args = dict(locals())
    p = {n: a for n, a in args.items() if n != "loss_target"}
    cfg = Cfg(S=x.shape[1], D=x.shape[2], DFF=ffn_w_gate.shape[2] * N_DEV)
    out = train_step(cfg, p, loss_target)
    return (out["loss"], out["grad_x"], *[out["grad_" + w] for w in WEIGHTS], *[out["delta_" + w] for w in WEIGHTS],
            *[out["new_m_" + w] for w in WEIGHTS], *[out["new_v_" + w] for w in WEIGHTS])
```

```python
import functools
import math
from typing import NamedTuple

import jax
import jax.numpy as jnp
from jax import lax
from jax.experimental import pallas as pl
from jax.experimental.pallas import tpu as pltpu

F32 = jnp.float32
BF16 = jnp.bfloat16
MESH_AXES = ("x", "y", "c")
N_DEV = 8
LANES = 128
SUBLANES = 8
VMEM_LIMIT = 56 * 1024 * 1024

NORM_EPS = 1e-6
CONV_WIDTH = 4
CHUNK = 64
HEAD_K = 128
HEAD_V = 256
GATE_RANK = 16
GATE_NORM = 16.0
LRU_BLOCK = 128
LRU_C = 8.0
SSD_P = 64
SSD_N = 128
SSD_HG = 8
SSD_GW = SSD_HG * SSD_P

ADAM_LR = 0.001
ADAM_B1 = 0.9
ADAM_B2 = 0.999
ADAM_EPS = 1e-08
ADAM_WD = 0.01
ADAM_STEP = 10


class Cfg(NamedTuple):
    S: int
    D: int
    DFF: int

    @property
    def GH(self):
        return self.D // 512

    @property
    def NB(self):
        return self.D // 256

    @property
    def NG(self):
        return self.D // 256

    @property
    def DK(self):
        return HEAD_K * self.GH

    @property
    def DV(self):
        return HEAD_V * self.GH

    @property
    def W(self):
        return LRU_BLOCK * self.NB

    @property
    def DI(self):
        return SSD_GW * self.NG

    @property
    def CD(self):
        return self.DI + 2 * self.NG * SSD_N

    @property
    def NH(self):
        return SSD_HG * self.NG

    @property
    def EVEN_IN(self):
        return 2 * self.DK + 2 * self.DV + GATE_RANK + 2 * self.W

    @property
    def ODD_IN(self):
        return self.DI + self.CD + self.NH

    @property
    def EP(self):
        return _round_up(2 * self.DK + 2 * self.DV + 2 * self.W + LANES, 768)

    @property
    def OP(self):
        return _round_up(self.DI + self.CD + LANES, 768)


def _round_up(n, m):
    return (n + m - 1) // m * m


def _tile(n, pref):
    if n <= pref:
        return n
    t = pref - pref % LANES
    while n % t:
        t -= LANES
    return t


def _cparams(n_axes):
    return pltpu.CompilerParams(dimension_semantics=("arbitrary",) * n_axes, vmem_limit_bytes=VMEM_LIMIT)


def _dg(a, b, ca, cb):
    return lax.dot_general(a.astype(BF16), b.astype(BF16), (((ca,), (cb,)), ((), ())), preferred_element_type=F32)


@functools.partial(jax.custom_vjp, nondiff_argnums=(2, 3))
def bdot(a, b, ca, cb):
    return _dg(a, b, ca, cb)


def _bdot_fwd(a, b, ca, cb):
    return _dg(a, b, ca, cb), (a, b)


def _bdot_bwd(ca, cb, res, g):
    a, b = res
    da = _dg(g, b, 1, 1 - cb) if ca == 1 else _dg(b, g, 1 - cb, 1)
    db = _dg(a, g, 1 - ca, 0) if cb == 0 else _dg(g, a, 0, 1 - ca)
    return da.astype(a.dtype), db.astype(b.dtype)


bdot.defvjp(_bdot_fwd, _bdot_bwd)


def _lower_tri(n):
    r = lax.broadcasted_iota(jnp.int32, (n, n), 0)
    c = lax.broadcasted_iota(jnp.int32, (n, n), 1)
    return c <= r


@jax.custom_vjp
def cumsum_rows(x):
    n = x.shape[0]
    return lax.dot_general(_lower_tri(n).astype(F32), x, (((1,), (0,)), ((), ())),
                           precision=lax.Precision.HIGHEST, preferred_element_type=F32)


def _cumsum_fwd(x):
    return cumsum_rows(x), None


def _cumsum_bwd(_, g):
    n = g.shape[0]
    return (lax.dot_general(_lower_tri(n).astype(F32), g, (((0,), (0,)), ((), ())),
                            precision=lax.Precision.HIGHEST, preferred_element_type=F32),)


cumsum_rows.defvjp(_cumsum_fwd, _cumsum_bwd)


def _row(x, i):
    r = lax.broadcasted_iota(jnp.int32, x.shape, 0)
    return jnp.sum(jnp.where(r == i, x, 0.0), axis=0, keepdims=True)


def _softplus_raw(x):
    return jnp.maximum(x, 0.0) + jnp.log(1.0 + jnp.exp(-jnp.abs(x)))


@jax.custom_vjp
def softplus(x):
    return _softplus_raw(x)


softplus.defvjp(lambda x: (_softplus_raw(x), x), lambda x, g: (g * jax.nn.sigmoid(x),))


@jax.custom_vjp
def log_sigmoid(x):
    return -_softplus_raw(-x)


log_sigmoid.defvjp(lambda x: (-_softplus_raw(-x), x), lambda x, g: (g * jax.nn.sigmoid(-x),))


def silu(x):
    return x * jax.nn.sigmoid(x)


def gelu_tanh(x):
    return 0.5 * x * (1.0 + jnp.tanh(math.sqrt(2.0 / math.pi) * (x + 0.044715 * (x * x * x))))


def _expm1(x):
    series = x * (1.0 + 0.5 * x * (1.0 + (1.0 / 3.0) * x))
    return jnp.where(jnp.abs(x) < 1e-2, series, jnp.exp(x) - 1.0)


def rms(x, w):
    return x * lax.rsqrt(jnp.mean(x * x, axis=-1, keepdims=True) + NORM_EPS) * w


def _rows_iota(shape):
    return lax.broadcasted_iota(jnp.int32, shape, 0)


def _scan_up(a, u):
    n = a.shape[0]
    r = _rows_iota(a.shape)
    d = 1
    while d < n:
        m = r >= d
        a_s = jnp.where(m, pltpu.roll(a, d, 0), 1.0)
        u_s = jnp.where(m, pltpu.roll(u, d, 0), 0.0)
        u = a * u_s + u
        a = a * a_s
        d *= 2
    return u


def _scan_down(a, u):
    n = a.shape[0]
    r = _rows_iota(a.shape)
    d = 1
    while d < n:
        m = r < n - d
        a_s = jnp.where(m, pltpu.roll(a, n - d, 0), 1.0)
        u_s = jnp.where(m, pltpu.roll(u, n - d, 0), 0.0)
        u = a * u_s + u
        a = a * a_s
        d *= 2
    return u


@jax.custom_vjp
def lin_scan(a, u):
    return _scan_up(a, u)


def _lin_scan_fwd(a, u):
    h = _scan_up(a, u)
    return h, (a, h)


def _lin_scan_bwd(res, g):
    a, h = res
    n = a.shape[0]
    r = _rows_iota(a.shape)
    a_next = jnp.where(r < n - 1, pltpu.roll(a, n - 1, 0), 0.0)
    gt = _scan_down(a_next, g)
    h_prev = jnp.where(r >= 1, pltpu.roll(h, 1, 0), 0.0)
    return gt * h_prev, gt


lin_scan.defvjp(_lin_scan_fwd, _lin_scan_bwd)


def _expand_heads(v):
    r = v.shape[0]
    return jnp.concatenate([jnp.broadcast_to(v[:, h:h + 1], (r, SSD_P)) for h in range(SSD_HG)], axis=1)


@jax.custom_vjp
def _split_heads(x):
    return tuple(x[:, h * SSD_P:(h + 1) * SSD_P] for h in range(SSD_HG))


_split_heads.defvjp(lambda x: (_split_heads(x), None), lambda _, gs: (jnp.concatenate(gs, axis=1),))


def matmul(a, b, *, ta=False, tb=False, a_slot=None, b_slot=None, b_lead=(), res=None, after=None, out_dtype=F32,
           name, tm=1024, tn=1024, tk=2048):
    lead = tuple(b_lead)
    ra, ca_ = a.shape[-2:]
    rb, cb_ = b.shape[-2:]
    m_st, ka_st = (ca_, ra) if ta else (ra, ca_)
    kb_st, n_st = (cb_, rb) if tb else (rb, cb_)
    kslot = a_slot == "k"
    assert kslot == (b_slot == "k")
    m = m_st * (N_DEV if a_slot == "m" else 1)
    n = n_st * (N_DEV if b_slot == "n" else 1)
    assert ka_st == kb_st, (a.shape, b.shape, ta, tb)
    tm = m_st if a_slot == "m" else _tile(m, tm)
    tn = n_st if b_slot == "n" else _tile(n, tn)
    tk = ka_st if kslot else _tile(ka_st, tk)
    nk = ka_st // tk
    ca, cb = (0 if ta else 1), (1 if tb else 0)
    nl = (None,) * len(lead)

    if a_slot is None:
        a_spec = pl.BlockSpec((tk, tm), lambda i, j, k: (k, i)) if ta else pl.BlockSpec((tm, tk), lambda i, j, k: (i, k))
    elif a_slot == "m":
        a_spec = (pl.BlockSpec((None, tk, tm), lambda i, j, k: (i, k, 0)) if ta
                  else pl.BlockSpec((None, tm, tk), lambda i, j, k: (i, 0, k)))
    else:
        a_spec = (pl.BlockSpec((N_DEV, tk, tm), lambda i, j, k: (0, 0, i)) if ta
                  else pl.BlockSpec((N_DEV, tm, tk), lambda i, j, k: (0, i, 0)))
    if b_slot is None:
        b_spec = (pl.BlockSpec(nl + (tn, tk), lambda i, j, k: lead + (j, k)) if tb
                  else pl.BlockSpec(nl + (tk, tn), lambda i, j, k: lead + (k, j)))
    elif b_slot == "n":
        b_spec = (pl.BlockSpec((None,) + nl + (tn, tk), lambda i, j, k: (j,) + lead + (0, k)) if tb
                  else pl.BlockSpec((None,) + nl + (tk, tn), lambda i, j, k: (j,) + lead + (k, 0)))
    else:
        b_spec = (pl.BlockSpec((N_DEV,) + nl + (tn, tk), lambda i, j, k: (0,) + lead + (j, 0)) if tb
                  else pl.BlockSpec((N_DEV,) + nl + (tk, tn), lambda i, j, k: (0,) + lead + (0, j)))
    if a_slot == "m":
        o_spec, o_shape = pl.BlockSpec((None, tm, tn), lambda i, j, k: (i, 0, j)), (N_DEV, tm, n)
    elif b_slot == "n":
        o_spec, o_shape = pl.BlockSpec((None, tm, tn), lambda i, j, k: (j, i, 0)), (N_DEV, m, tn)
    else:
        o_spec, o_shape = pl.BlockSpec((tm, tn), lambda i, j, k: (i, j)), (m, n)
    assert res is None or (a_slot != "m" and b_slot != "n")

    def dot(x, y):
        return lax.dot_general(x.astype(BF16), y.astype(BF16), (((ca,), (cb,)), ((), ())), preferred_element_type=F32)

    def body(*refs):
        a_ref, b_ref = refs[:2]
        r_ref = refs[2] if res is not None else None
        o_ref = refs[2 + (res is not None) + (after is not None)]

        def finish(acc):
            if r_ref is not None:
                acc = acc + r_ref[...].astype(F32)
            o_ref[...] = acc.astype(o_ref.dtype)

        if kslot:
            acc = dot(a_ref[0], b_ref[0])
            for s in range(1, N_DEV):
                acc = acc + dot(a_ref[s], b_ref[s])
            finish(acc)
        elif nk == 1:
            finish(dot(a_ref[...], b_ref[...]))
        else:
            acc_ref = refs[-1]
            k = pl.program_id(2)

            @pl.when(k == 0)
            def _():
                acc_ref[...] = dot(a_ref[...], b_ref[...])

            @pl.when(k > 0)
            def _():
                acc_ref[...] += dot(a_ref[...], b_ref[...])

            @pl.when(k == nk - 1)
            def _():
                finish(acc_ref[...])

    in_specs = [a_spec, b_spec]
    args = [a, b]
    if res is not None:
        in_specs.append(pl.BlockSpec((tm, tn), lambda i, j, k: (i, j)))
        args.append(res)
    if after is not None:
        in_specs.append(pl.BlockSpec(memory_space=pl.ANY))
        args.append(after)
    return pl.pallas_call(
        body, name=name, grid=(m // tm, n // tn, nk), in_specs=in_specs, out_specs=o_spec,
        out_shape=jax.ShapeDtypeStruct(o_shape, out_dtype),
        scratch_shapes=[pltpu.VMEM((tm, tn), F32)] if nk > 1 else [], compiler_params=_cparams(3),
    )(*args)


def seq_call(name, fn, grid, ins, outs, accs=(), carries=()):
    n_in, n_out, n_acc = len(ins), len(outs), len(accs)

    def body(*refs):
        in_refs = refs[:n_in]
        out_refs = refs[n_in:n_in + n_out]
        acc_refs = refs[n_in + n_out:n_in + n_out + n_acc]
        c_refs = refs[n_in + n_out + n_acc:]

        if acc_refs or c_refs:
            @pl.when(pl.program_id(1) == 0)
            def _():
                for r in tuple(acc_refs) + tuple(c_refs):
                    r[...] = jnp.zeros_like(r)

        o, a, c = fn([r[...] for r in in_refs], [r[...] for r in c_refs])
        for r, v in zip(out_refs, o, strict=True):
            r[...] = v.astype(r.dtype)
        for r, v in zip(acc_refs, a, strict=True):
            r[...] += v
        for r, v in zip(c_refs, c, strict=True):
            r[...] = v

    return pl.pallas_call(
        body, name=name, grid=grid,
        in_specs=[pl.BlockSpec(blk, im) for _, blk, im in ins],
        out_specs=[pl.BlockSpec(blk, im) for _, _, blk, im in outs] + [pl.BlockSpec(blk, im) for _, blk, im in accs],
        out_shape=[jax.ShapeDtypeStruct(s, d) for s, d, _, _ in outs] + [jax.ShapeDtypeStruct(s, F32) for s, _, _ in accs],
        scratch_shapes=[pltpu.VMEM(s, F32) for s in carries], compiler_params=_cparams(2),
    )(*[a for a, _, _ in ins])


def exchange(arrays, gather, name, after=None):
    n = len(arrays)
    extra = [] if after is None else [after]

    def body(*refs):
        x_refs, o_refs = refs[:n], refs[n + len(extra):2 * n + len(extra)]
        send_sems, recv_sems, local_sems = refs[2 * n + len(extra):]
        pos = [lax.axis_index(ax) for ax in MESH_AXES]
        me = 4 * pos[0] + 2 * pos[1] + pos[2]
        copies = []
        for i in range(n):
            own = pltpu.make_async_copy(x_refs[i] if gather else x_refs[i].at[me], o_refs[i].at[me], local_sems.at[i])
            own.start()
            copies.append(own)
        for k in range(1, N_DEV):
            bits = ((k >> 2) & 1, (k >> 1) & 1, k & 1)
            peer = tuple(1 - p if b else p for p, b in zip(pos, bits))
            peer_id = 4 * peer[0] + 2 * peer[1] + peer[2]
            for i in range(n):
                cp = pltpu.make_async_remote_copy(
                    src_ref=x_refs[i] if gather else x_refs[i].at[peer_id], dst_ref=o_refs[i].at[me],
                    send_sem=send_sems.at[i * (N_DEV - 1) + k - 1], recv_sem=recv_sems.at[i * (N_DEV - 1) + k - 1],
                    device_id=peer, device_id_type=pl.DeviceIdType.MESH)
                cp.start()
                copies.append(cp)
        for cp in copies:
            cp.wait()

    hbm = pl.BlockSpec(memory_space=pltpu.HBM)
    return pl.pallas_call(
        body, name=name, in_specs=[hbm] * n + [pl.BlockSpec(memory_space=pl.ANY)] * len(extra), out_specs=[hbm] * n,
        out_shape=[jax.ShapeDtypeStruct(((N_DEV,) + a.shape) if gather else a.shape, a.dtype) for a in arrays],
        scratch_shapes=[pltpu.SemaphoreType.DMA((n * (N_DEV - 1),)), pltpu.SemaphoreType.DMA((n * (N_DEV - 1),)),
                        pltpu.SemaphoreType.DMA((n,))],
    )(*arrays, *extra)


_HBM = pl.BlockSpec(memory_space=pltpu.HBM)
_SEM = pl.BlockSpec(memory_space=pltpu.SEMAPHORE)
N_PEERS = N_DEV - 1


def _mesh_pos():
    pos = [lax.axis_index(ax) for ax in MESH_AXES]
    return pos, 4 * pos[0] + 2 * pos[1] + pos[2]


def _peers(pos):
    out = []
    for k in range(1, N_DEV):
        bits = ((k >> 2) & 1, (k >> 1) & 1, k & 1)
        peer = tuple(1 - p if b else p for p, b in zip(pos, bits))
        out.append((peer, 4 * peer[0] + 2 * peer[1] + peer[2]))
    return out


def _part(x_ref, lead, gather, slot):
    ref = x_ref if lead is None else x_ref.at[lead]
    return ref if gather else ref.at[slot]


def _landing_zones(items, gather, name):
    n = len(items)
    shapes = [(((N_DEV,) + (a.shape if lead is None else a.shape[1:])) if gather else a.shape) for a, lead in items]

    def body(*refs):
        x_refs, o_refs, sems = refs[:n], refs[n:2 * n], refs[2 * n]
        _, me = _mesh_pos()
        copies = [pltpu.make_async_copy(_part(x_refs[i], items[i][1], gather, me), o_refs[i].at[me], sems.at[i])
                  for i in range(n)]
        for cp in copies:
            cp.start()
        for cp in copies:
            cp.wait()

    return pl.pallas_call(
        body, name=name, in_specs=[_HBM] * n, out_specs=[_HBM] * n,
        out_shape=[jax.ShapeDtypeStruct(s, a.dtype) for s, (a, _) in zip(shapes, items)],
        scratch_shapes=[pltpu.SemaphoreType.DMA((n,))])(*[a for a, _ in items])


def _split_copies(items, gather, x_refs, land_refs, send_sems, recv_sems):
    pos, me = _mesh_pos()
    copies = []
    for i, (_, lead) in enumerate(items):
        for k, (peer, peer_id) in enumerate(_peers(pos)):
            copies.append(pltpu.make_async_remote_copy(
                src_ref=_part(x_refs[i], lead, gather, peer_id), dst_ref=land_refs[i].at[me],
                send_sem=send_sems.at[i * N_PEERS + k], recv_sem=recv_sems.at[i * N_PEERS + k],
                device_id=peer, device_id_type=pl.DeviceIdType.MESH))
    return copies


def exchange_start(items, gather, lands, name):
    n = len(items)
    lands = list(lands)
    xs = [a for a, _ in items]

    def body(*refs):
        x_refs, land_refs = refs[:n], refs[n:2 * n]
        send_sems, recv_sems, token = refs[2 * n], refs[2 * n + 1], refs[-1]
        for cp in _split_copies(items, gather, x_refs, land_refs, send_sems, recv_sems):
            cp.start()
        token[...] = jnp.zeros_like(token)

    outs = pl.pallas_call(
        body, name=name,
        out_shape=(pltpu.SemaphoreType.DMA((n * N_PEERS,)), pltpu.SemaphoreType.DMA((n * N_PEERS,)),
                   *[pltpu.HBM(v.shape, v.dtype) for v in xs + lands], jax.ShapeDtypeStruct((SUBLANES, LANES), F32)),
        in_specs=[_HBM] * (2 * n),
        out_specs=(_SEM, _SEM, *[_HBM] * (2 * n), pl.BlockSpec(memory_space=pltpu.VMEM)),
        input_output_aliases={i: 2 + i for i in range(2 * n)},
        compiler_params=pltpu.CompilerParams(has_side_effects=pltpu.SideEffectType.DATAFLOW_SIDE_EFFECTING),
    )(*[pltpu.with_memory_space_constraint(v, pltpu.HBM) for v in xs + lands])
    handle = (items, gather, outs[0], outs[1], outs[2:2 + n], outs[2 + n:2 + 2 * n])
    return handle, outs[-1]


def exchange_wait(handle, after, name):
    items, gather, send_sems, recv_sems, x_thru, land_thru = handle
    n = len(items)

    def body(*refs):
        x_refs, land_refs = refs[:n], refs[n:2 * n]
        for cp in _split_copies(items, gather, x_refs, land_refs, refs[2 * n], refs[2 * n + 1]):
            cp.wait_send()
            cp.wait_recv()

    outs = pl.pallas_call(
        body, name=name, out_shape=tuple(pltpu.HBM(v.shape, v.dtype) for v in tuple(x_thru) + tuple(land_thru)),
        in_specs=[_HBM] * (2 * n) + [_SEM, _SEM, pl.BlockSpec(memory_space=pl.ANY)], out_specs=tuple([_HBM] * (2 * n)),
        input_output_aliases={i: i for i in range(2 * n)},
        compiler_params=pltpu.CompilerParams(has_side_effects=pltpu.SideEffectType.DATAFLOW_SIDE_EFFECTING),
    )(*x_thru, *land_thru, send_sems, recv_sems, after)
    return list(outs[:n]), list(outs[n:])


def _adam_update(w, g, m, v):
    nm = ADAM_B1 * m + (1.0 - ADAM_B1) * g
    nv = ADAM_B2 * v + (1.0 - ADAM_B2) * (g * g)
    m_hat = nm / (1.0 - ADAM_B1 ** ADAM_STEP)
    v_hat = nv / (1.0 - ADAM_B2 ** ADAM_STEP)
    return -ADAM_LR * (m_hat / (jnp.sqrt(v_hat) + ADAM_EPS) + ADAM_WD * w), nm, nv


def _sum_slots(s_ref):
    acc = s_ref[0].astype(F32)
    for j in range(1, N_DEV):
        acc = acc + s_ref[j].astype(F32)
    return acc


ADAM_BLOCK_BYTES = 10 * 1024 * 1024


def adamw_sharded(recvs, sends, me, w, m, v, name):
    nl, r, c = w.shape
    assert len(recvs) == nl and len(sends) == nl
    per_row = _round_up(c, LANES) * (nl * (N_DEV + 1) * recvs[0].dtype.itemsize + 7 * 4)
    tr = r
    while tr * per_row > ADAM_BLOCK_BYTES and tr % 16 == 0:
        tr //= 2

    def body(me_ref, *refs):
        s_refs, o_refs = refs[:nl], refs[nl:2 * nl]
        w_ref, m_ref, v_ref, g_ref, d_ref, nm_ref, nv_ref = refs[2 * nl:]
        mine = me_ref[0]

        def total(l):
            acc = jnp.where(mine == 0, o_refs[l][...], s_refs[l][0]).astype(F32)
            for j in range(1, N_DEV):
                acc = acc + jnp.where(mine == j, o_refs[l][...], s_refs[l][j]).astype(F32)
            return acc

        g = total(0)
        for l in range(1, nl):
            g = jnp.where(pl.program_id(0) == l, total(l), g)
        g_ref[...] = g
        d_ref[...], nm_ref[...], nv_ref[...] = _adam_update(w_ref[...], g, m_ref[...], v_ref[...])

    spec = pl.BlockSpec((None, tr, c), lambda l, i, me_ref: (l, i, 0))
    shp = jax.ShapeDtypeStruct(w.shape, F32)
    grid_spec = pltpu.PrefetchScalarGridSpec(
        num_scalar_prefetch=1, grid=(nl, r // tr),
        in_specs=([pl.BlockSpec((N_DEV, tr, c), lambda l, i, me_ref: (0, i, 0))] * nl
                  + [pl.BlockSpec((None, tr, c), lambda l, i, me_ref: (me_ref[0], i, 0))] * nl + [spec, spec, spec]),
        out_specs=[spec] * 4)
    return pl.pallas_call(body, name=name, grid_spec=grid_spec, out_shape=[shp] * 4, compiler_params=_cparams(2))(
        jnp.reshape(me, (1,)).astype(jnp.int32), *recvs, *sends, w, m, v)


def reduce_slots(slots, name):
    _, r, lanes = slots.shape
    tr = _tile(r, 2048)

    def body(s_ref, o_ref):
        o_ref[...] = _sum_slots(s_ref)

    return pl.pallas_call(
        body, name=name, grid=(r // tr,),
        in_specs=[pl.BlockSpec((N_DEV, tr, lanes), lambda i: (0, i, 0))],
        out_specs=pl.BlockSpec((tr, lanes), lambda i: (i, 0)),
        out_shape=jax.ShapeDtypeStruct((r, lanes), F32), compiler_params=_cparams(1),
    )(slots)


def adamw(w, g, m, v, name):
    r, lanes = w.shape
    tr = _tile(r, 2048)

    def body(w_ref, g_ref, m_ref, v_ref, d_ref, nm_ref, nv_ref):
        d_ref[...], nm_ref[...], nv_ref[...] = _adam_update(w_ref[...], g_ref[...], m_ref[...], v_ref[...])

    spec = pl.BlockSpec((tr, lanes), lambda i: (i, 0))
    shp = jax.ShapeDtypeStruct((r, lanes), F32)
    return pl.pallas_call(body, name=name, grid=(r // tr,), in_specs=[spec] * 4, out_specs=[spec] * 3,
                          out_shape=[shp] * 3, compiler_params=_cparams(1))(w, g, m, v)


def cols_from_slots(slots, place, width, name):
    _, rows, c = slots.shape
    tr = _tile(rows, 256)

    def body(s_ref, o_ref):
        o_ref[...] = place(jnp.concatenate([s_ref[j] for j in range(N_DEV)], axis=1))

    return pl.pallas_call(
        body, name=name, grid=(rows // tr,),
        in_specs=[pl.BlockSpec((N_DEV, tr, c), lambda i: (0, i, 0))],
        out_specs=pl.BlockSpec((tr, width), lambda i: (i, 0)),
        out_shape=jax.ShapeDtypeStruct((rows, width), slots.dtype), compiler_params=_cparams(1))(slots)


def slots_from_cols(full, pick, c, name):
    rows, wide = full.shape
    tr = _tile(rows, 256)

    def body(x_ref, o_ref):
        v = pick(x_ref[...])
        for j in range(N_DEV):
            o_ref[j] = v[:, j * c:(j + 1) * c]

    return pl.pallas_call(
        body, name=name, grid=(rows // tr,),
        in_specs=[pl.BlockSpec((tr, wide), lambda i: (i, 0))],
        out_specs=pl.BlockSpec((N_DEV, tr, c), lambda i: (0, i, 0)),
        out_shape=jax.ShapeDtypeStruct((N_DEV, rows, c), full.dtype), compiler_params=_cparams(1))(full)


def _token_tile(cfg):
    return min(cfg.S, 256)


def norm_fwd(cfg, x, w, name):
    ts = _token_tile(cfg)
    d = x.shape[1]

    def fn(ins, _):
        xv, wv = ins
        return [rms(xv, wv)], [], []

    return seq_call(name, fn, (1, cfg.S // ts),
                    [(x, (ts, d), lambda g, t: (t, 0)), (w, (1, d), lambda g, t: (0, 0))],
                    [((cfg.S, d), BF16, (ts, d), lambda g, t: (t, 0))])[0]


def norm_bwd(cfg, x, w, dh, dres, name):
    ts = _token_tile(cfg)
    d = x.shape[1]

    def fn(ins, _):
        xv, wv, dhv, drv = ins
        _, vjp = jax.vjp(rms, xv, wv)
        dx, dw = vjp(dhv.astype(F32))
        return [dx + drv], [dw], []

    row = lambda g, t: (t, 0)
    out = seq_call(name, fn, (1, cfg.S // ts),
                   [(x, (ts, d), row), (w, (1, d), lambda g, t: (0, 0)), (dh, (ts, d), row), (dres, (ts, d), row)],
                   [((cfg.S, d), F32, (ts, d), row)],
                   accs=[((1, d), (1, d), lambda g, t: (0, 0))])
    return out[0], out[1]


def head_fwd_bwd(cfg, x, w, target, name):
    ts = _token_tile(cfg)
    d = x.shape[1]

    def fn(ins, _):
        xv, wv, tv = ins
        y, vjp = jax.vjp(rms, xv, wv)
        err = y - tv
        loss = 0.5 * jnp.sum(err * err) / d
        dx, dw = vjp(err / d)
        return [dx], [jnp.full((SUBLANES, LANES), loss, F32), dw], []

    row = lambda g, t: (t, 0)
    fixed = lambda g, t: (0, 0)
    dx, loss, dw = seq_call(name, fn, (1, cfg.S // ts),
                            [(x, (ts, d), row), (w, (1, d), fixed), (target, (ts, d), row)],
                            [((cfg.S, d), F32, (ts, d), row)],
                            accs=[((SUBLANES, LANES), (SUBLANES, LANES), fixed), ((1, d), (1, d), fixed)])
    return loss, dx, dw


def act_fwd(cfg, gate, up, name):
    ts, c = min(cfg.S, 512), gate.shape[2]

    def fn(ins, _):
        return [silu(ins[0]) * ins[1]], [], []

    blk = lambda g, t: (g, t, 0)
    return seq_call(name, fn, (N_DEV, cfg.S // ts),
                    [(gate, (None, ts, c), blk), (up, (None, ts, c), blk)],
                    [(gate.shape, BF16, (None, ts, c), blk)])[0]


def act_bwd(cfg, gate, up, dact, name):
    ts, c = min(cfg.S, 512), gate.shape[2]

    def fn(ins, _):
        gv, uv, dv = ins
        _, vjp = jax.vjp(lambda a, b: silu(a) * b, gv, uv)
        dg, du = vjp(dv.astype(F32))
        return [dg, du], [], []

    blk = lambda g, t: (g, t, 0)
    shp = (gate.shape, BF16, (None, ts, c), blk)
    return seq_call(name, fn, (N_DEV, cfg.S // ts),
                    [(gate, (None, ts, c), blk), (up, (None, ts, c), blk), (dact, (None, ts, c), blk)], [shp, shp])


CONV_COLS = 256
HALO = SUBLANES


def _shift_down(x, halo, j):
    if j == 0:
        return x
    r8 = _rows_iota(halo.shape)
    top = jnp.where(r8 >= j, pltpu.roll(x[:HALO], j, 0), pltpu.roll(halo, j, 0))
    return jnp.concatenate([top, pltpu.roll(x, j, 0)[HALO:]], axis=0)


def _shift_up(x, halo, j):
    if j == 0:
        return x
    n = x.shape[0]
    r8 = _rows_iota(halo.shape)
    bot = jnp.where(r8 < HALO - j, pltpu.roll(x[n - HALO:], HALO - j, 0), pltpu.roll(halo, HALO - j, 0))
    return jnp.concatenate([pltpu.roll(x, n - j, 0)[:n - HALO], bot], axis=0)


def _conv_tile(cfg):
    return min(cfg.S, 512)


def conv_fwd(cfg, src, col0, width, w, b, name):
    tt, cb = _conv_tile(cfg), CONV_COLS
    c0, hb = col0 // cb, tt // HALO
    nt = cfg.S // tt

    def body(x_ref, h_ref, w_ref, b_ref, o_ref):
        t = pl.program_id(1)
        x = x_ref[...]
        halo = jnp.where(t > 0, h_ref[...], 0.0)
        wv = w_ref[...]
        acc = b_ref[...] + wv[CONV_WIDTH - 1:CONV_WIDTH] * x
        for j in range(1, CONV_WIDTH):
            acc = acc + wv[CONV_WIDTH - 1 - j:CONV_WIDTH - j] * _shift_down(x, halo, j)
        o_ref[...] = acc

    return pl.pallas_call(
        body, name=name, grid=(width // cb, nt),
        in_specs=[pl.BlockSpec((tt, cb), lambda c, t: (t, c0 + c)),
                  pl.BlockSpec((HALO, cb), lambda c, t: (jnp.maximum(t * hb - 1, 0), c0 + c)),
                  pl.BlockSpec((CONV_WIDTH, cb), lambda c, t: (0, c)),
                  pl.BlockSpec((1, cb), lambda c, t: (0, c))],
        out_specs=pl.BlockSpec((tt, cb), lambda c, t: (t, c)),
        out_shape=jax.ShapeDtypeStruct((cfg.S, width), F32), compiler_params=_cparams(2),
    )(src, src, w, b)


def conv_bwd(cfg, src, col0, width, w, dy, name):
    tt, cb = _conv_tile(cfg), CONV_COLS
    c0, hb = col0 // cb, tt // HALO
    nt = cfg.S // tt

    def body(x_ref, h_ref, w_ref, dy_ref, dh_ref, dx_ref, dw_ref, db_ref):
        t = pl.program_id(1)

        @pl.when(t == 0)
        def _():
            dw_ref[...] = jnp.zeros_like(dw_ref)
            db_ref[...] = jnp.zeros_like(db_ref)

        x = x_ref[...]
        halo = jnp.where(t > 0, h_ref[...], 0.0)
        dy = dy_ref[...]
        dhalo = jnp.where(t < nt - 1, dh_ref[...], 0.0)
        wv = w_ref[...]
        dx = wv[CONV_WIDTH - 1:CONV_WIDTH] * dy
        rows = [jnp.sum(dy * x, axis=0, keepdims=True)]
        for j in range(1, CONV_WIDTH):
            dx = dx + wv[CONV_WIDTH - 1 - j:CONV_WIDTH - j] * _shift_up(dy, dhalo, j)
            rows.insert(0, jnp.sum(dy * _shift_down(x, halo, j), axis=0, keepdims=True))
        dx_ref[...] = dx.astype(dx_ref.dtype)
        dw_ref[...] += jnp.concatenate(rows, axis=0)
        db_ref[...] += jnp.sum(dy, axis=0, keepdims=True)

    return pl.pallas_call(
        body, name=name, grid=(width // cb, nt),
        in_specs=[pl.BlockSpec((tt, cb), lambda c, t: (t, c0 + c)),
                  pl.BlockSpec((HALO, cb), lambda c, t: (jnp.maximum(t * hb - 1, 0), c0 + c)),
                  pl.BlockSpec((CONV_WIDTH, cb), lambda c, t: (0, c)),
                  pl.BlockSpec((tt, cb), lambda c, t: (t, c)),
                  pl.BlockSpec((HALO, cb), lambda c, t: (jnp.minimum((t + 1) * hb, nt * hb - 1), c))],
        out_specs=[pl.BlockSpec((tt, cb), lambda c, t: (t, c)),
                   pl.BlockSpec((CONV_WIDTH, cb), lambda c, t: (0, c)),
                   pl.BlockSpec((1, cb), lambda c, t: (0, c))],
        out_shape=[jax.ShapeDtypeStruct((cfg.S, width), BF16), jax.ShapeDtypeStruct((CONV_WIDTH, width), F32),
                   jax.ShapeDtypeStruct((1, width), F32)],
        compiler_params=_cparams(2),
    )(src, src, w, dy, dy)


def _gla_core(gh, q, k, v, g, glr, wg, bg, wn, st):
    n = glr.shape[0]
    causal = _lower_tri(n)
    outs, new = [], []
    for h in range(gh):
        log_a = log_sigmoid(bdot(glr, wg[h], 1, 0) + bg[h]) * (1.0 / GATE_NORM)
        bcum = cumsum_rows(log_a)
        b_last, b_mid = _row(bcum, n - 1), _row(bcum, n // 2)
        qs = q[h] * (HEAD_K ** -0.5)
        scores = jnp.where(causal, bdot(qs * jnp.exp(bcum - b_mid), k[h] * jnp.exp(b_mid - bcum), 1, 1), 0.0)
        o = bdot(scores, v[h], 1, 0) + bdot(qs * jnp.exp(bcum), st[h], 1, 1)
        new.append(st[h] * jnp.exp(b_last) + bdot(v[h], k[h] * jnp.exp(b_last - bcum), 0, 0))
        outs.append(rms(o, wn) * silu(g[h]))
    return jnp.concatenate(outs, axis=1), new


def _gla_ins(cfg, p0, wg, bg, wn, tmap):
    gh = cfg.GH
    ins = []
    for h in range(gh):
        ins.append((p0, (CHUNK, HEAD_K), lambda g, t, h=h: (tmap(t), h)))
    for h in range(gh):
        ins.append((p0, (CHUNK, HEAD_K), lambda g, t, h=h: (tmap(t), gh + h)))
    for h in range(gh):
        ins.append((p0, (CHUNK, HEAD_V), lambda g, t, h=h: (tmap(t), gh + h)))
    for h in range(gh):
        ins.append((p0, (CHUNK, HEAD_V), lambda g, t, h=h: (tmap(t), 2 * gh + h)))
    ins.append((p0, (CHUNK, LANES), lambda g, t: (tmap(t), 10 * gh)))
    for h in range(gh):
        ins.append((wg, (LANES, HEAD_K), lambda g, t, h=h: (0, h)))
    for h in range(gh):
        ins.append((bg, (1, HEAD_K), lambda g, t, h=h: (0, h)))
    ins.append((wn, (1, HEAD_V), lambda g, t: (0, 0)))
    return ins


def _gla_unpack(gh, vals):
    q, k, v, g = (vals[i * gh:(i + 1) * gh] for i in range(4))
    glr = vals[4 * gh]
    wg = vals[4 * gh + 1:5 * gh + 1]
    bg = vals[5 * gh + 1:6 * gh + 1]
    wn = vals[6 * gh + 1]
    return q, k, v, g, glr, wg, bg, wn, vals[6 * gh + 2:]


def gla_fwd(cfg, p0, wg, bg, wn, name):
    gh, nc = cfg.GH, cfg.S // CHUNK

    def fn(ins, st):
        q, k, v, g, glr, wgv, bgv, wnv, _ = _gla_unpack(gh, ins)
        out, new = _gla_core(gh, q, k, v, g, glr, wgv, bgv, wnv, st)
        return [out, jnp.stack(st)], [], new

    return seq_call(name, fn, (1, nc), _gla_ins(cfg, p0, wg, bg, wn, lambda t: t),
                    [((cfg.S, cfg.DV), BF16, (CHUNK, cfg.DV), lambda g, t: (t, 0)),
                     ((nc, gh, HEAD_V, HEAD_K), F32, (None, gh, HEAD_V, HEAD_K), lambda g, t: (t, 0, 0, 0))],
                    carries=[(HEAD_V, HEAD_K)] * gh)


def gla_bwd(cfg, p0, wg, bg, wn, states, dout, name):
    gh, nc = cfg.GH, cfg.S // CHUNK
    rev = lambda t: nc - 1 - t

    def fn(ins, dst):
        q, k, v, g, glr, wgv, bgv, wnv, rest = _gla_unpack(gh, ins)
        st_all, do = rest
        st = [st_all[h] for h in range(gh)]
        _, vjp = jax.vjp(functools.partial(_gla_core, gh), q, k, v, g, glr, wgv, bgv, wnv, st)
        dq, dk, dv, dg, dglr, dwg, dbg, dwn, dstate = vjp((do.astype(F32), list(dst)))
        return ([jnp.concatenate(list(dq) + list(dk) + list(dv) + list(dg), axis=1), dglr],
                [jnp.concatenate(dwg, axis=1), jnp.concatenate(dbg, axis=1), dwn], dstate)

    ins = _gla_ins(cfg, p0, wg, bg, wn, rev)
    ins.append((states, (None, gh, HEAD_V, HEAD_K), lambda g, t: (rev(t), 0, 0, 0)))
    ins.append((dout, (CHUNK, cfg.DV), lambda g, t: (rev(t), 0)))
    wide = 2 * cfg.DK + 2 * cfg.DV
    fixed = lambda g, t: (0, 0)
    return seq_call(name, fn, (1, nc), ins,
                    [((cfg.S, wide), BF16, (CHUNK, wide), lambda g, t: (rev(t), 0)),
                     ((cfg.S, LANES), BF16, (CHUNK, LANES), lambda g, t: (rev(t), 0))],
                    accs=[((LANES, cfg.DK), (LANES, cfg.DK), fixed), ((1, cfg.DK), (1, cfg.DK), fixed),
                          ((1, HEAD_V), (1, HEAD_V), fixed)],
                    carries=[(HEAD_V, HEAD_K)] * gh)


def _lru_core(xc, gate, wa, wi, ba, bi, lam, h_in):
    r = jax.nn.sigmoid(bdot(xc, wa, 1, 0) + ba)
    i = jax.nn.sigmoid(bdot(xc, wi, 1, 0) + bi)
    log_a = LRU_C * r * log_sigmoid(lam)
    a = jnp.exp(log_a)
    u = jnp.sqrt(-_expm1(2.0 * log_a)) * (i * xc)
    first = _rows_iota(a.shape) == 0
    h = lin_scan(a, u + jnp.where(first, a * h_in, 0.0))
    return h * gelu_tanh(gate), _row(h, a.shape[0] - 1)


def _lru_tile(cfg):
    return min(cfg.S, 512)


def _lru_ins(cfg, xc, p0, wa, wi, ba, bi, lam, tmap):
    tt, gh = _lru_tile(cfg), cfg.GH
    vec = lambda g, t: (0, g)
    return [(xc, (tt, LRU_BLOCK), lambda g, t: (tmap(t), g)),
            (p0, (tt, LRU_BLOCK), lambda g, t: (tmap(t), 8 * gh + g)),
            (wa, (None, LRU_BLOCK, LRU_BLOCK), lambda g, t: (g, 0, 0)),
            (wi, (None, LRU_BLOCK, LRU_BLOCK), lambda g, t: (g, 0, 0)),
            (ba, (1, LRU_BLOCK), vec), (bi, (1, LRU_BLOCK), vec), (lam, (1, LRU_BLOCK), vec)]


def lru_fwd(cfg, xc, p0, wa, wi, ba, bi, lam, name):
    tt, nb = _lru_tile(cfg), cfg.NB
    nt = cfg.S // tt

    def fn(ins, c):
        out, h_last = _lru_core(*ins, c[0])
        return [out, c[0]], [], [h_last]

    return seq_call(name, fn, (nb, nt), _lru_ins(cfg, xc, p0, wa, wi, ba, bi, lam, lambda t: t),
                    [((cfg.S, cfg.W), BF16, (tt, LRU_BLOCK), lambda g, t: (t, g)),
                     ((nb, nt, 1, LRU_BLOCK), F32, (None, None, 1, LRU_BLOCK), lambda g, t: (g, t, 0, 0))],
                    carries=[(1, LRU_BLOCK)])


def lru_bwd(cfg, xc, p0, wa, wi, ba, bi, lam, states, dout, name):
    tt, nb = _lru_tile(cfg), cfg.NB
    nt = cfg.S // tt
    rev = lambda t: nt - 1 - t

    def fn(ins, c):
        *fwd_ins, h_in, do = ins
        _, vjp = jax.vjp(_lru_core, *fwd_ins, h_in)
        dxc, dgate, dwa, dwi, dba, dbi, dlam, dh = vjp((do.astype(F32), c[0]))
        return [dxc, dgate], [dwa, dwi, dba, dbi, dlam], [dh]

    ins = _lru_ins(cfg, xc, p0, wa, wi, ba, bi, lam, rev)
    ins.append((states, (None, None, 1, LRU_BLOCK), lambda g, t: (g, rev(t), 0, 0)))
    ins.append((dout, (tt, LRU_BLOCK), lambda g, t: (rev(t), g)))
    mat = ((nb, LRU_BLOCK, LRU_BLOCK), (None, LRU_BLOCK, LRU_BLOCK), lambda g, t: (g, 0, 0))
    vec = ((1, cfg.W), (1, LRU_BLOCK), lambda g, t: (0, g))
    return seq_call(name, fn, (nb, nt), ins,
                    [((cfg.S, cfg.W), F32, (tt, LRU_BLOCK), lambda g, t: (rev(t), g)),
                     ((cfg.S, cfg.W), BF16, (tt, LRU_BLOCK), lambda g, t: (rev(t), g))],
                    accs=[mat, mat, vec, vec, vec], carries=[(1, LRU_BLOCK)])


def _ssd_core(xc, bc, cc, z, dt_raw, dt_bias, a_log, d_skip, gn, st):
    n = xc.shape[0]
    x, bm, cm = silu(xc), silu(bc), silu(cc)
    dt = softplus(dt_raw + dt_bias)
    acs = cumsum_rows(dt * (-jnp.exp(a_log)))
    acs_t = acs.T
    acs_e, dt_e = _expand_heads(acs), _expand_heads(dt)
    last_e = _expand_heads(_row(acs, n - 1))
    causal = _lower_tri(n)
    cb = bdot(cm, bm, 1, 1)
    xdt = x * dt_e
    y_diag = []
    for h, xh in enumerate(_split_heads(xdt)):
        seg = acs[:, h:h + 1] - acs_t[h:h + 1, :]
        decay = jnp.where(causal, jnp.exp(jnp.minimum(seg, 0.0)), 0.0)
        y_diag.append(bdot(cb * decay, xh, 1, 0))
    y = jnp.concatenate(y_diag, axis=1) + bdot(cm, st, 1, 0) * jnp.exp(acs_e)
    new = st * jnp.exp(last_e) + bdot(bm, xdt * jnp.exp(last_e - acs_e), 0, 0)
    y = (y + _expand_heads(d_skip) * x) * silu(z)
    return rms(y, gn), new


def _ssd_ins(cfg, xc, p1, dt_raw, dt_bias, a_log, d_skip, gn, tmap):
    ng = cfg.NG
    vec = lambda g, t: (g, 0, 0)
    return [(xc, (CHUNK, SSD_GW), lambda g, t: (tmap(t), g)),
            (xc, (CHUNK, SSD_N), lambda g, t: (tmap(t), 4 * ng + g)),
            (xc, (CHUNK, SSD_N), lambda g, t: (tmap(t), 5 * ng + g)),
            (p1, (CHUNK, SSD_GW), lambda g, t: (tmap(t), g)),
            (dt_raw, (None, CHUNK, LANES), lambda g, t: (g, tmap(t), 0)),
            (dt_bias, (None, 1, LANES), vec), (a_log, (None, 1, LANES), vec), (d_skip, (None, 1, LANES), vec),
            (gn, (1, SSD_GW), lambda g, t: (0, g))]


def ssd_fwd(cfg, xc, p1, dt_raw, dt_bias, a_log, d_skip, gn, name):
    ng, nc = cfg.NG, cfg.S // CHUNK

    def fn(ins, c):
        out, new = _ssd_core(*ins, c[0])
        return [out, c[0]], [], [new]

    return seq_call(name, fn, (ng, nc), _ssd_ins(cfg, xc, p1, dt_raw, dt_bias, a_log, d_skip, gn, lambda t: t),
                    [((cfg.S, cfg.DI), BF16, (CHUNK, SSD_GW), lambda g, t: (t, g)),
                     ((ng, nc, SSD_N, SSD_GW), F32, (None, None, SSD_N, SSD_GW), lambda g, t: (g, t, 0, 0))],
                    carries=[(SSD_N, SSD_GW)])


def ssd_bwd(cfg, xc, p1, dt_raw, dt_bias, a_log, d_skip, gn, states, dout, name):
    ng, nc = cfg.NG, cfg.S // CHUNK
    rev = lambda t: nc - 1 - t

    def fn(ins, c):
        *fwd_ins, st, do = ins
        _, vjp = jax.vjp(_ssd_core, *fwd_ins, st)
        dxc, dbc, dcc, dz, ddt, dbias, dalog, dskip, dgn, dst = vjp((do.astype(F32), c[0]))
        return [dxc, dbc, dcc, dz, ddt], [dbias, dalog, dskip, dgn], [dst]

    ins = _ssd_ins(cfg, xc, p1, dt_raw, dt_bias, a_log, d_skip, gn, rev)
    ins.append((states, (None, None, SSD_N, SSD_GW), lambda g, t: (g, rev(t), 0, 0)))
    ins.append((dout, (CHUNK, SSD_GW), lambda g, t: (rev(t), g)))
    col = lambda g, t: (rev(t), g)
    vec = ((ng, 1, LANES), (None, 1, LANES), lambda g, t: (g, 0, 0))
    return seq_call(name, fn, (ng, nc), ins,
                    [((cfg.S, cfg.DI), F32, (CHUNK, SSD_GW), col),
                     ((cfg.S, ng * SSD_N), F32, (CHUNK, SSD_N), col),
                     ((cfg.S, ng * SSD_N), F32, (CHUNK, SSD_N), col),
                     ((cfg.S, cfg.DI), BF16, (CHUNK, SSD_GW), col),
                     ((ng, cfg.S, LANES), F32, (None, CHUNK, LANES), lambda g, t: (g, rev(t), 0))],
                    accs=[vec, vec, vec, ((1, cfg.DI), (1, SSD_GW), lambda g, t: (0, g))],
                    carries=[(SSD_N, SSD_GW)])


PACK_ALIGN = SUBLANES * LANES
PACK_ROWS = 256


def pack(arrays):
    pieces = []
    for a in arrays:
        flat = a.reshape(-1).astype(F32)
        pad = _round_up(flat.shape[0], PACK_ALIGN) - flat.shape[0]
        pieces.append(jnp.pad(flat, (0, pad)) if pad else flat)
    flat = jnp.concatenate(pieces)
    pad = _round_up(flat.shape[0], PACK_ROWS * LANES) - flat.shape[0]
    return jnp.pad(flat, (0, pad)).reshape(-1, LANES)


def unpack(buf, shapes):
    lead = buf.shape[:-2]
    flat = buf.reshape(lead + (-1,))
    out, off = [], 0
    for s in shapes:
        n = math.prod(s)
        out.append(flat[..., off:off + n].reshape(lead + tuple(s)))
        off += _round_up(n, PACK_ALIGN)
    return out


def _slots_to_cols(slots):
    return slots.transpose(1, 0, 2).reshape(slots.shape[1], -1)


def _even_in_padded(cfg, w):
    main = 2 * cfg.DK + 2 * cfg.DV
    return jnp.concatenate([w[:, :main], w[:, main + GATE_RANK:], w[:, main:main + GATE_RANK],
                            jnp.zeros((w.shape[0], cfg.EP - cfg.EVEN_IN), w.dtype)], axis=1)


def _even_in_unpadded(cfg, wp):
    main = 2 * cfg.DK + 2 * cfg.DV
    rest = main + 2 * cfg.W
    return jnp.concatenate([wp[:, :main], wp[:, rest:rest + GATE_RANK], wp[:, main:rest]], axis=1)


def _odd_in_padded(cfg, w):
    return jnp.concatenate([w, jnp.zeros((w.shape[0], cfg.OP - cfg.ODD_IN), w.dtype)], axis=1)


def _odd_in_unpadded(cfg, wp):
    return wp[:, :cfg.ODD_IN]


def _group_lanes(cfg, v):
    lead = v.shape[:-1]
    g = jnp.moveaxis(v.reshape(lead + (cfg.NG, SSD_HG)), -2, 0)
    return jnp.pad(g, [(0, 0)] * (g.ndim - 1) + [(0, LANES - SSD_HG)])


def _ungroup_lanes(cfg, g):
    v = jnp.moveaxis(g[..., :SSD_HG], 0, -2)
    return v.reshape(v.shape[:-2] + (cfg.NH,))


def train_step(cfg, p, loss_target):
    S, D = cfg.S, cfg.D
    me = 4 * lax.axis_index("x") + 2 * lax.axis_index("y") + lax.axis_index("c")

    big = ["ev_w_in", "ev_w_out", "od_w_in", "od_w_out", "ffn_w_gate", "ffn_w_up", "ffn_w_down"]
    small_sharded = ["ev_gla_w_gate", "ev_lru_conv_w", "od_norm", "od_conv_w", "od_conv_b", "od_gnorm"]
    replicated = ["ev_norm", "ev_gla_b_gate", "ev_gla_w_onorm", "ev_lru_conv_b", "ev_lru_w_a", "ev_lru_b_a",
                  "ev_lru_w_i", "ev_lru_b_i", "ev_lru_lam", "od_dt_bias", "od_a_log", "od_d_skip", "ffn_norm",
                  "final_norm"]

    wb = {n: p[n].astype(BF16) for n in big}
    ffn_items = lambda l: [(wb["ffn_w_gate"], l), (wb["ffn_w_up"], l), (wb["ffn_w_down"], l)]
    ss_shapes = [p[n].shape for n in small_sharded]
    groups = [[(pack([p[n] for n in small_sharded]), None), (wb["ev_w_in"], 0), (wb["ev_w_out"], 0)], ffn_items(0),
              [(wb["od_w_in"], 0), (wb["od_w_out"], 0)], ffn_items(1)]
    lands = list(_landing_zones([it for g in groups for it in g], True, "gather_own"))
    gathers, tokens = [], []
    for i, g in enumerate(groups):
        handle, token = exchange_start(g, True, lands[:len(g)], f"gather_start{i}")
        del lands[:len(g)]
        gathers.append(handle)
        tokens.append(token)
    all_started = tokens[0][:1, :1] + tokens[1][:1, :1] + tokens[2][:1, :1] + tokens[3][:1, :1]

    ss_all, gw_ev_in, gw_ev_out = exchange_wait(gathers[0], all_started, "gather_wait0")[1]
    gs = dict(zip(small_sharded, unpack(ss_all, ss_shapes)))
    w_ev_in = cols_from_slots(gw_ev_in, functools.partial(_even_in_padded, cfg), cfg.EP, "ev_w_in_cols")
    w_ev_out = gw_ev_out.reshape(D, D)

    gla_wg = jnp.pad(_slots_to_cols(gs["ev_gla_w_gate"][:, 0]), ((0, LANES - GATE_RANK), (0, 0)))
    lru_cw = _slots_to_cols(gs["ev_lru_conv_w"][:, 0])
    od_norm = gs["od_norm"].transpose(1, 0, 2).reshape(1, D)
    od_cw = _slots_to_cols(gs["od_conv_w"][:, 0])
    od_cb = gs["od_conv_b"].transpose(1, 0, 2).reshape(1, cfg.CD)
    od_gn = gs["od_gnorm"].transpose(1, 0, 2).reshape(1, cfg.DI)

    x0 = p["x"][0]
    target = loss_target[0]
    ev_norm = p["ev_norm"] + all_started
    bg = p["ev_gla_b_gate"]
    wn = p["ev_gla_w_onorm"]
    lru_cb = p["ev_lru_conv_b"]
    wa, wi = p["ev_lru_w_a"][0], p["ev_lru_w_i"][0]
    ba, bi, lam = p["ev_lru_b_a"], p["ev_lru_b_i"], p["ev_lru_lam"]
    dt_bias, a_log, d_skip = (_group_lanes(cfg, p[n]) for n in ("od_dt_bias", "od_a_log", "od_d_skip"))
    ffn_norm = [p["ffn_norm"][l:l + 1] for l in range(2)]
    final_norm = p["final_norm"].reshape(1, D)

    def ffn_forward(l, x):
        w_gate, w_up, w_down = exchange_wait(gathers[1 + 2 * l], x, f"gather_wait{1 + 2 * l}")[1]
        h = norm_fwd(cfg, x, ffn_norm[l], f"ffn{l}_norm")
        gate = matmul(h, w_gate, b_slot="n", name=f"ffn{l}_gate")
        up = matmul(h, w_up, b_slot="n", name=f"ffn{l}_up")
        act = act_fwd(cfg, gate, up, f"ffn{l}_act")
        out = matmul(act, w_down, a_slot="k", b_slot="k", res=x, name=f"ffn{l}_down", tn=512)
        return out, (h, gate, up, act, w_gate, w_up, w_down)

    h0 = norm_fwd(cfg, x0, ev_norm, "ev_norm")
    p0 = matmul(h0, w_ev_in, name="ev_in", tn=768)
    gla_out, gla_states = gla_fwd(cfg, p0, gla_wg, bg, wn, "gla_fwd")
    lru_col = 2 * cfg.DK + 2 * cfg.DV
    lru_xc = conv_fwd(cfg, p0, lru_col, cfg.W, lru_cw, lru_cb, "lru_conv")
    lru_out, lru_states = lru_fwd(cfg, lru_xc, p0, wa, wi, ba, bi, lam, "lru_fwd")
    mix = jnp.concatenate([gla_out, lru_out], axis=1)
    x1 = matmul(mix, w_ev_out, res=x0, name="ev_out")
    x2, ffn0_saved = ffn_forward(0, x1)

    gw_od_in, gw_od_out = exchange_wait(gathers[2], x2, "gather_wait2")[1]
    w_od_in = cols_from_slots(gw_od_in, functools.partial(_odd_in_padded, cfg), cfg.OP, "od_w_in_cols")
    w_od_out = gw_od_out.reshape(cfg.DI, D)
    h2 = norm_fwd(cfg, x2, od_norm, "od_norm")
    p1 = matmul(h2, w_od_in, name="od_in", tn=768)
    od_xc = conv_fwd(cfg, p1, cfg.DI, cfg.CD, od_cw, od_cb, "od_conv")
    dt_col = cfg.DI + cfg.CD
    dt_raw = _group_lanes(cfg, p1[:, dt_col:dt_col + cfg.NH])
    ssd_out, ssd_states = ssd_fwd(cfg, od_xc, p1, dt_raw, dt_bias, a_log, d_skip, od_gn, "ssd_fwd")
    x3 = matmul(ssd_out, w_od_out, res=x2, name="od_out")
    x4, ffn1_saved = ffn_forward(1, x3)

    loss_part, dx4, d_final_norm = head_fwd_bwd(cfg, x4, final_norm, target, "head")
    loss = lax.psum(loss_part[0, 0], MESH_AXES)

    def ffn_backward(l, x, saved, dx_out, after):
        h, gate, up, act, w_gate, w_up, w_down = saved
        dact = matmul(dx_out, w_down, tb=True, b_slot="n", after=after, name=f"ffn{l}_dact")
        d_down = matmul(act, dx_out, ta=True, a_slot="m", out_dtype=BF16, name=f"ffn{l}_dwdown")
        dgate, dup = act_bwd(cfg, gate, up, dact, f"ffn{l}_act_bwd")
        dh = matmul(dgate, w_gate, tb=True, a_slot="k", b_slot="k", name=f"ffn{l}_dh_gate", tn=512)
        dh = matmul(dup, w_up, tb=True, a_slot="k", b_slot="k", res=dh, name=f"ffn{l}_dh_up", tn=512)
        d_gate = matmul(h, dgate, ta=True, b_slot="n", out_dtype=BF16, name=f"ffn{l}_dwgate")
        d_up = matmul(h, dup, ta=True, b_slot="n", out_dtype=BF16, name=f"ffn{l}_dwup")
        dx, dnorm = norm_bwd(cfg, x, ffn_norm[l], dh, dx_out, f"ffn{l}_norm_bwd")
        sent, token = start_grads([d_gate, d_up, d_down], f"grads_start_ffn{l}")
        return dx, dnorm, sent, token

    def start_grads(arrays, name):
        return exchange_start([(a, None) for a in arrays], False, [lax.empty(a.shape, a.dtype) for a in arrays], name)

    dx3, d_ffn_norm1, sent_ffn1, token = ffn_backward(1, x3, ffn1_saved, dx4, None)

    d_ssd_out = matmul(dx3, w_od_out, tb=True, after=token, name="od_dmix")
    d_od_out = matmul(ssd_out, dx3, ta=True, out_dtype=BF16, name="od_dwout")
    dxs, dbm, dcm, dz, d_dt_raw, d_dt_bias, d_a_log, d_d_skip, d_od_gn = ssd_bwd(
        cfg, od_xc, p1, dt_raw, dt_bias, a_log, d_skip, od_gn, ssd_states, d_ssd_out, "ssd_bwd")
    conv_parts, col = [], 0
    for part, dy in (("x", dxs), ("b", dbm), ("c", dcm)):
        width = dy.shape[1]
        conv_parts.append(conv_bwd(cfg, p1, cfg.DI + col, width, od_cw[:, col:col + width], dy, "od_conv_bwd_" + part))
        col += width
    d_od_cw = jnp.concatenate([c[1] for c in conv_parts], axis=1)
    d_od_cb = jnp.concatenate([c[2] for c in conv_parts], axis=1)
    d_dt = _ungroup_lanes(cfg, d_dt_raw).astype(BF16)
    dp1 = jnp.concatenate([dz] + [c[0] for c in conv_parts] + [d_dt, jnp.zeros((S, cfg.OP - cfg.ODD_IN), BF16)], axis=1)
    dh2 = matmul(dp1, w_od_in, tb=True, name="od_dh", tk=1536)
    d_od_in = matmul(h2, dp1, ta=True, out_dtype=BF16, name="od_dwin", tn=768)
    dx2, d_od_norm = norm_bwd(cfg, x2, od_norm, dh2, dx3, "od_norm_bwd")
    d_od_in_slots = slots_from_cols(d_od_in, functools.partial(_odd_in_unpadded, cfg), p["od_w_in"].shape[2],
                                    "od_dwin_slots")
    sent_od, token = start_grads([d_od_in_slots, d_od_out.reshape((N_DEV,) + p["od_w_out"].shape[1:])], "grads_start_od")

    dx1, d_ffn_norm0, sent_ffn0, token = ffn_backward(0, x1, ffn0_saved, dx2, token)

    d_mix = matmul(dx1, w_ev_out, tb=True, after=token, name="ev_dmix")
    d_ev_out = matmul(mix, dx1, ta=True, out_dtype=BF16, name="ev_dwout")
    d_qkvg, d_glr, d_gla_wg, d_bg, d_wn = gla_bwd(cfg, p0, gla_wg, bg, wn, gla_states, d_mix[:, :cfg.DV], "gla_bwd")
    d_lru_xc, d_gate_br, d_wa, d_wi, d_ba, d_bi, d_lam = lru_bwd(
        cfg, lru_xc, p0, wa, wi, ba, bi, lam, lru_states, d_mix[:, cfg.DV:], "lru_bwd")
    d_xbr, d_lru_cw, d_lru_cb = conv_bwd(cfg, p0, lru_col, cfg.W, lru_cw, d_lru_xc, "lru_conv_bwd")
    dp0 = jnp.concatenate([d_qkvg, d_xbr, d_gate_br, d_glr, jnp.zeros((S, cfg.EP - lru_col - 2 * cfg.W - LANES), BF16)],
                          axis=1)
    d_ev_in = matmul(h0, dp0, ta=True, out_dtype=BF16, name="ev_dwin", tn=768)
    d_ev_in_slots = slots_from_cols(d_ev_in, functools.partial(_even_in_unpadded, cfg), p["ev_w_in"].shape[2],
                                    "ev_dwin_slots")
    sent_ev, token = start_grads([d_ev_in_slots, d_ev_out.reshape((N_DEV,) + p["ev_w_out"].shape[1:])], "grads_start_ev")
    dh0 = matmul(dp0, w_ev_in, tb=True, after=token, name="ev_dh", tk=1792)
    grad_x, d_ev_norm = norm_bwd(cfg, x0, ev_norm, dh0, dx1, "ev_norm_bwd")

    out = {"loss": loss, "grad_x": grad_x[None]}

    def update(names, sent, after, wait_name):
        s, r = exchange_wait(sent[0], after, wait_name + "0")
        sends, recvs = [[a] for a in s], [[a] for a in r]
        for extra in sent[1:]:
            s, r = exchange_wait(extra, after, wait_name + "1")
            for i in range(len(names)):
                sends[i].append(s[i])
                recvs[i].append(r[i])
        for i, n in enumerate(names):
            out["grad_" + n], out["delta_" + n], out["new_m_" + n], out["new_v_" + n] = adamw_sharded(
                recvs[i], sends[i], me, p[n], p["m_" + n], p["v_" + n], "adamw_" + n)
        return out["new_v_" + names[-1]]

    done = update(["od_w_in", "od_w_out"], [sent_od], grad_x, "grads_wait_od")
    done = update(["ffn_w_gate", "ffn_w_up", "ffn_w_down"], [sent_ffn0, sent_ffn1], done, "grads_wait_ffn")
    done = update(["ev_w_in", "ev_w_out"], [sent_ev], done, "grads_wait_ev")

    small_full = {
        "ev_gla_w_gate": d_gla_wg[:GATE_RANK][None], "ev_lru_conv_w": d_lru_cw[None], "od_norm": d_od_norm,
        "od_conv_w": d_od_cw[None], "od_conv_b": d_od_cb, "od_gnorm": d_od_gn,
        "ev_norm": d_ev_norm, "ev_gla_b_gate": d_bg, "ev_gla_w_onorm": d_wn, "ev_lru_conv_b": d_lru_cb,
        "ev_lru_w_a": d_wa[None], "ev_lru_b_a": d_ba, "ev_lru_w_i": d_wi[None], "ev_lru_b_i": d_bi,
        "ev_lru_lam": d_lam, "od_dt_bias": _ungroup_lanes(cfg, d_dt_bias), "od_a_log": _ungroup_lanes(cfg, d_a_log),
        "od_d_skip": _ungroup_lanes(cfg, d_d_skip), "ffn_norm": jnp.concatenate([d_ffn_norm0, d_ffn_norm1], axis=0),
        "final_norm": d_final_norm.reshape(D),
    }
    small = small_sharded + replicated
    small_all = exchange([pack([small_full[n] for n in small])], True, "gather_small_grads", after=done)[0]
    g_small = dict(zip(small, unpack(reduce_slots(small_all, "sum_small_grads"), [small_full[n].shape for n in small])))
    for n in small_sharded:
        width = p[n].shape[-1]
        g_small[n] = lax.dynamic_slice_in_dim(g_small[n], me * width, width, axis=g_small[n].ndim - 1)
    shapes = [p[n].shape for n in small]
    g_buf = pack([g_small[n] for n in small])
    delta, new_m, new_v = adamw(pack([p[n] for n in small]), g_buf, pack([p["m_" + n] for n in small]),
                                pack([p["v_" + n] for n in small]), "adamw_small")
    for kind, buf in (("grad_", g_buf), ("delta_", delta), ("new_m_", new_m), ("new_v_", new_v)):
        for n, a in zip(small, unpack(buf, shapes)):
            out[kind + n] = a
    return out


WEIGHTS = ['ev_norm', 'ev_w_in', 'ev_gla_w_gate', 'ev_gla_b_gate', 'ev_gla_w_onorm', 'ev_lru_conv_w', 'ev_lru_conv_b',
           'ev_lru_w_a', 'ev_lru_b_a', 'ev_lru_w_i', 'ev_lru_b_i', 'ev_lru_lam', 'ev_w_out', 'od_norm', 'od_w_in',
           'od_conv_w', 'od_conv_b', 'od_dt_bias', 'od_a_log', 'od_d_skip', 'od_gnorm', 'od_w_out', 'ffn_norm',
           'ffn_w_gate', 'ffn_w_up', 'ffn_w_down', 'final_norm']


def kernel(x, ev_norm, ev_w_in, ev_gla_w_gate, ev_gla_b_gate, ev_gla_w_onorm, ev_lru_conv_w, ev_lru_conv_b, ev_lru_w_a, ev_lru_b_a, ev_lru_w_i, ev_lru_b_i, ev_lru_lam, ev_w_out, od_norm, od_w_in, od_conv_w, od_conv_b, od_dt_bias, od_a_log, od_d_skip, od_gnorm, od_w_out, ffn_norm, ffn_w_gate, ffn_w_up, ffn_w_down, final_norm, loss_target, m_ev_norm, m_ev_w_in, m_ev_gla_w_gate, m_ev_gla_b_gate, m_ev_gla_w_onorm, m_ev_lru_conv_w, m_ev_lru_conv_b, m_ev_lru_w_a, m_ev_lru_b_a, m_ev_lru_w_i, m_ev_lru_b_i, m_ev_lru_lam, m_ev_w_out, m_od_norm, m_od_w_in, m_od_conv_w, m_od_conv_b, m_od_dt_bias, m_od_a_log, m_od_d_skip, m_od_gnorm, m_od_w_out, m_ffn_norm, m_ffn_w_gate, m_ffn_w_up, m_ffn_w_down, m_final_norm, v_ev_norm, v_ev_w_in, v_ev_gla_w_gate, v_ev_gla_b_gate, v_ev_gla_w_onorm, v_ev_lru_conv_w, v_ev_lru_conv_b, v_ev_lru_w_a, v_ev_lru_b_a, v_ev_lru_w_i, v_ev_lru_b_i, v_ev_lru_lam, v_ev_w_out, v_od_norm, v_od_w_in, v_od_conv_w, v_od_conv_b, v_od_dt_bias, v_od_a_log, v_od_d_skip, v_od_gnorm, v_od_w_out, v_ffn_norm, v_ffn_w_gate, v_ffn_w_up, v_ffn_w_down, v_final_norm):
    args = dict(locals())
    p = {n: a for n, a in args.items() if n != "loss_target"}
    cfg = Cfg(S=x.shape[1], D=x.shape[2], DFF=ffn_w_gate.shape[2] * N_DEV)
    out = train_step(cfg, p, loss_target)
    return (out["loss"], out["grad_x"], *[out["grad_" + w] for w in WEIGHTS], *[out["delta_" + w] for w in WEIGHTS],
            *[out["new_m_" + w] for w in WEIGHTS], *[out["new_v_" + w] for w in WEIGHTS])
```

```python
import functools
import math
from typing import NamedTuple

import jax
import jax.numpy as jnp
from jax import lax
from jax.experimental import pallas as pl
from jax.experimental.pallas import tpu as pltpu

F32 = jnp.float32
BF16 = jnp.bfloat16
MESH_AXES = ("x", "y", "c")
N_DEV = 8
LANES = 128
SUBLANES = 8
VMEM_LIMIT = 56 * 1024 * 1024

NORM_EPS = 1e-6
CONV_WIDTH = 4
CHUNK = 64
HEAD_K = 128
HEAD_V = 256
GATE_RANK = 16
GATE_NORM = 16.0
LRU_BLOCK = 128
LRU_C = 8.0
SSD_P = 64
SSD_N = 128
SSD_HG = 8
SSD_GW = SSD_HG * SSD_P

ADAM_LR = 0.001
ADAM_B1 = 0.9
ADAM_B2 = 0.999
ADAM_EPS = 1e-08
ADAM_WD = 0.01
ADAM_STEP = 10


class Cfg(NamedTuple):
    S: int
    D: int
    DFF: int

    @property
    def GH(self):
        return self.D // 512

    @property
    def NB(self):
        return self.D // 256

    @property
    def NG(self):
        return self.D // 256

    @property
    def DK(self):
        return HEAD_K * self.GH

    @property
    def DV(self):
        return HEAD_V * self.GH

    @property
    def W(self):
        return LRU_BLOCK * self.NB

    @property
    def DI(self):
        return SSD_GW * self.NG

    @property
    def CD(self):
        return self.DI + 2 * self.NG * SSD_N

    @property
    def NH(self):
        return SSD_HG * self.NG

    @property
    def EVEN_IN(self):
        return 2 * self.DK + 2 * self.DV + GATE_RANK + 2 * self.W

    @property
    def ODD_IN(self):
        return self.DI + self.CD + self.NH

    @property
    def EP(self):
        return _round_up(2 * self.DK + 2 * self.DV + 2 * self.W + LANES, 768)

    @property
    def OP(self):
        return _round_up(self.DI + self.CD + LANES, 768)


def _round_up(n, m):
    return (n + m - 1) // m * m


def _tile(n, pref):
    if n <= pref:
        return n
    t = pref - pref % LANES
    while n % t:
        t -= LANES
    return t


def _cparams(n_axes):
    return pltpu.CompilerParams(dimension_semantics=("arbitrary",) * n_axes, vmem_limit_bytes=VMEM_LIMIT)


def _dg(a, b, ca, cb):
    return lax.dot_general(a.astype(BF16), b.astype(BF16), (((ca,), (cb,)), ((), ())), preferred_element_type=F32)


@functools.partial(jax.custom_vjp, nondiff_argnums=(2, 3))
def bdot(a, b, ca, cb):
    return _dg(a, b, ca, cb)


def _bdot_fwd(a, b, ca, cb):
    return _dg(a, b, ca, cb), (a, b)


def _bdot_bwd(ca, cb, res, g):
    a, b = res
    da = _dg(g, b, 1, 1 - cb) if ca == 1 else _dg(b, g, 1 - cb, 1)
    db = _dg(a, g, 1 - ca, 0) if cb == 0 else _dg(g, a, 0, 1 - ca)
    return da.astype(a.dtype), db.astype(b.dtype)


bdot.defvjp(_bdot_fwd, _bdot_bwd)


def _lower_tri(n):
    r = lax.broadcasted_iota(jnp.int32, (n, n), 0)
    c = lax.broadcasted_iota(jnp.int32, (n, n), 1)
    return c <= r


@jax.custom_vjp
def cumsum_rows(x):
    n = x.shape[0]
    return lax.dot_general(_lower_tri(n).astype(F32), x, (((1,), (0,)), ((), ())),
                           precision=lax.Precision.HIGHEST, preferred_element_type=F32)


def _cumsum_fwd(x):
    return cumsum_rows(x), None


def _cumsum_bwd(_, g):
    n = g.shape[0]
    return (lax.dot_general(_lower_tri(n).astype(F32), g, (((0,), (0,)), ((), ())),
                            precision=lax.Precision.HIGHEST, preferred_element_type=F32),)


cumsum_rows.defvjp(_cumsum_fwd, _cumsum_bwd)


def _row(x, i):
    r = lax.broadcasted_iota(jnp.int32, x.shape, 0)
    return jnp.sum(jnp.where(r == i, x, 0.0), axis=0, keepdims=True)


def _softplus_raw(x):
    return jnp.maximum(x, 0.0) + jnp.log(1.0 + jnp.exp(-jnp.abs(x)))


@jax.custom_vjp
def softplus(x):
    return _softplus_raw(x)


softplus.defvjp(lambda x: (_softplus_raw(x), x), lambda x, g: (g * jax.nn.sigmoid(x),))


@jax.custom_vjp
def log_sigmoid(x):
    return -_softplus_raw(-x)


log_sigmoid.defvjp(lambda x: (-_softplus_raw(-x), x), lambda x, g: (g * jax.nn.sigmoid(-x),))


def silu(x):
    return x * jax.nn.sigmoid(x)


def gelu_tanh(x):
    return 0.5 * x * (1.0 + jnp.tanh(math.sqrt(2.0 / math.pi) * (x + 0.044715 * (x * x * x))))


def _expm1(x):
    series = x * (1.0 + 0.5 * x * (1.0 + (1.0 / 3.0) * x))
    return jnp.where(jnp.abs(x) < 1e-2, series, jnp.exp(x) - 1.0)


def rms(x, w):
    return x * lax.rsqrt(jnp.mean(x * x, axis=-1, keepdims=True) + NORM_EPS) * w


def _rows_iota(shape):
    return lax.broadcasted_iota(jnp.int32, shape, 0)


def _scan_up(a, u):
    n = a.shape[0]
    r = _rows_iota(a.shape)
    d = 1
    while d < n:
        m = r >= d
        a_s = jnp.where(m, pltpu.roll(a, d, 0), 1.0)
        u_s = jnp.where(m, pltpu.roll(u, d, 0), 0.0)
        u = a * u_s + u
        a = a * a_s
        d *= 2
    return u


def _scan_down(a, u):
    n = a.shape[0]
    r = _rows_iota(a.shape)
    d = 1
    while d < n:
        m = r < n - d
        a_s = jnp.where(m, pltpu.roll(a, n - d, 0), 1.0)
        u_s = jnp.where(m, pltpu.roll(u, n - d, 0), 0.0)
        u = a * u_s + u
        a = a * a_s
        d *= 2
    return u


@jax.custom_vjp
def lin_scan(a, u):
    return _scan_up(a, u)


def _lin_scan_fwd(a, u):
    h = _scan_up(a, u)
    return h, (a, h)


def _lin_scan_bwd(res, g):
    a, h = res
    n = a.shape[0]
    r = _rows_iota(a.shape)
    a_next = jnp.where(r < n - 1, pltpu.roll(a, n - 1, 0), 0.0)
    gt = _scan_down(a_next, g)
    h_prev = jnp.where(r >= 1, pltpu.roll(h, 1, 0), 0.0)
    return gt * h_prev, gt


lin_scan.defvjp(_lin_scan_fwd, _lin_scan_bwd)


def _expand_heads(v):
    r = v.shape[0]
    return jnp.concatenate([jnp.broadcast_to(v[:, h:h + 1], (r, SSD_P)) for h in range(SSD_HG)], axis=1)


@jax.custom_vjp
def _split_heads(x):
    return tuple(x[:, h * SSD_P:(h + 1) * SSD_P] for h in range(SSD_HG))


_split_heads.defvjp(lambda x: (_split_heads(x), None), lambda _, gs: (jnp.concatenate(gs, axis=1),))


def matmul(a, b, *, ta=False, tb=False, a_slot=None, b_slot=None, b_lead=(), res=None, after=None, out_dtype=F32,
           name, tm=1024, tn=1024, tk=2048):
    lead = tuple(b_lead)
    ra, ca_ = a.shape[-2:]
    rb, cb_ = b.shape[-2:]
    m_st, ka_st = (ca_, ra) if ta else (ra, ca_)
    kb_st, n_st = (cb_, rb) if tb else (rb, cb_)
    kslot = a_slot == "k"
    assert kslot == (b_slot == "k")
    m = m_st * (N_DEV if a_slot == "m" else 1)
    n = n_st * (N_DEV if b_slot == "n" else 1)
    assert ka_st == kb_st, (a.shape, b.shape, ta, tb)
    tm = m_st if a_slot == "m" else _tile(m, tm)
    tn = n_st if b_slot == "n" else _tile(n, tn)
    tk = ka_st if kslot else _tile(ka_st, tk)
    nk = ka_st // tk
    ca, cb = (0 if ta else 1), (1 if tb else 0)
    nl = (None,) * len(lead)

    if a_slot is None:
        a_spec = pl.BlockSpec((tk, tm), lambda i, j, k: (k, i)) if ta else pl.BlockSpec((tm, tk), lambda i, j, k: (i, k))
    elif a_slot == "m":
        a_spec = (pl.BlockSpec((None, tk, tm), lambda i, j, k: (i, k, 0)) if ta
                  else pl.BlockSpec((None, tm, tk), lambda i, j, k: (i, 0, k)))
    else:
        a_spec = (pl.BlockSpec((N_DEV, tk, tm), lambda i, j, k: (0, 0, i)) if ta
                  else pl.BlockSpec((N_DEV, tm, tk), lambda i, j, k: (0, i, 0)))
    if b_slot is None:
        b_spec = (pl.BlockSpec(nl + (tn, tk), lambda i, j, k: lead + (j, k)) if tb
                  else pl.BlockSpec(nl + (tk, tn), lambda i, j, k: lead + (k, j)))
    elif b_slot == "n":
        b_spec = (pl.BlockSpec((None,) + nl + (tn, tk), lambda i, j, k: (j,) + lead + (0, k)) if tb
                  else pl.BlockSpec((None,) + nl + (tk, tn), lambda i, j, k: (j,) + lead + (k, 0)))
    else:
        b_spec = (pl.BlockSpec((N_DEV,) + nl + (tn, tk), lambda i, j, k: (0,) + lead + (j, 0)) if tb
                  else pl.BlockSpec((N_DEV,) + nl + (tk, tn), lambda i, j, k: (0,) + lead + (0, j)))
    if a_slot == "m":
        o_spec, o_shape = pl.BlockSpec((None, tm, tn), lambda i, j, k: (i, 0, j)), (N_DEV, tm, n)
    elif b_slot == "n":
        o_spec, o_shape = pl.BlockSpec((None, tm, tn), lambda i, j, k: (j, i, 0)), (N_DEV, m, tn)
    else:
        o_spec, o_shape = pl.BlockSpec((tm, tn), lambda i, j, k: (i, j)), (m, n)
    assert res is None or (a_slot != "m" and b_slot != "n")

    def dot(x, y):
        return lax.dot_general(x.astype(BF16), y.astype(BF16), (((ca,), (cb,)), ((), ())), preferred_element_type=F32)

    def body(*refs):
        a_ref, b_ref = refs[:2]
        r_ref = refs[2] if res is not None else None
        o_ref = refs[2 + (res is not None) + (after is not None)]

        def finish(acc):
            if r_ref is not None:
                acc = acc + r_ref[...].astype(F32)
            o_ref[...] = acc.astype(o_ref.dtype)

        if kslot:
            acc = dot(a_ref[0], b_ref[0])
            for s in range(1, N_DEV):
                acc = acc + dot(a_ref[s], b_ref[s])
            finish(acc)
        elif nk == 1:
            finish(dot(a_ref[...], b_ref[...]))
        else:
            acc_ref = refs[-1]
            k = pl.program_id(2)

            @pl.when(k == 0)
            def _():
                acc_ref[...] = dot(a_ref[...], b_ref[...])

            @pl.when(k > 0)
            def _():
                acc_ref[...] += dot(a_ref[...], b_ref[...])

            @pl.when(k == nk - 1)
            def _():
                finish(acc_ref[...])

    in_specs = [a_spec, b_spec]
    args = [a, b]
    if res is not None:
        in_specs.append(pl.BlockSpec((tm, tn), lambda i, j, k: (i, j)))
        args.append(res)
    if after is not None:
        in_specs.append(pl.BlockSpec(memory_space=pl.ANY))
        args.append(after)
    return pl.pallas_call(
        body, name=name, grid=(m // tm, n // tn, nk), in_specs=in_specs, out_specs=o_spec,
        out_shape=jax.ShapeDtypeStruct(o_shape, out_dtype),
        scratch_shapes=[pltpu.VMEM((tm, tn), F32)] if nk > 1 else [], compiler_params=_cparams(3),
    )(*args)


def seq_call(name, fn, grid, ins, outs, accs=(), carries=()):
    n_in, n_out, n_acc = len(ins), len(outs), len(accs)

    def body(*refs):
        in_refs = refs[:n_in]
        out_refs = refs[n_in:n_in + n_out]
        acc_refs = refs[n_in + n_out:n_in + n_out + n_acc]
        c_refs = refs[n_in + n_out + n_acc:]

        if acc_refs or c_refs:
            @pl.when(pl.program_id(1) == 0)
            def _():
                for r in tuple(acc_refs) + tuple(c_refs):
                    r[...] = jnp.zeros_like(r)

        o, a, c = fn([r[...] for r in in_refs], [r[...] for r in c_refs])
        for r, v in zip(out_refs, o, strict=True):
            r[...] = v.astype(r.dtype)
        for r, v in zip(acc_refs, a, strict=True):
            r[...] += v
        for r, v in zip(c_refs, c, strict=True):
            r[...] = v

    return pl.pallas_call(
        body, name=name, grid=grid,
        in_specs=[pl.BlockSpec(blk, im) for _, blk, im in ins],
        out_specs=[pl.BlockSpec(blk, im) for _, _, blk, im in outs] + [pl.BlockSpec(blk, im) for _, blk, im in accs],
        out_shape=[jax.ShapeDtypeStruct(s, d) for s, d, _, _ in outs] + [jax.ShapeDtypeStruct(s, F32) for s, _, _ in accs],
        scratch_shapes=[pltpu.VMEM(s, F32) for s in carries], compiler_params=_cparams(2),
    )(*[a for a, _, _ in ins])


def exchange(arrays, gather, name, after=None):
    n = len(arrays)
    extra = [] if after is None else [after]

    def body(*refs):
        x_refs, o_refs = refs[:n], refs[n + len(extra):2 * n + len(extra)]
        send_sems, recv_sems, local_sems = refs[2 * n + len(extra):]
        pos = [lax.axis_index(ax) for ax in MESH_AXES]
        me = 4 * pos[0] + 2 * pos[1] + pos[2]
        copies = []
        for i in range(n):
            own = pltpu.make_async_copy(x_refs[i] if gather else x_refs[i].at[me], o_refs[i].at[me], local_sems.at[i])
            own.start()
            copies.append(own)
        for k in range(1, N_DEV):
            bits = ((k >> 2) & 1, (k >> 1) & 1, k & 1)
            peer = tuple(1 - p if b else p for p, b in zip(pos, bits))
            peer_id = 4 * peer[0] + 2 * peer[1] + peer[2]
            for i in range(n):
                cp = pltpu.make_async_remote_copy(
                    src_ref=x_refs[i] if gather else x_refs[i].at[peer_id], dst_ref=o_refs[i].at[me],
                    send_sem=send_sems.at[i * (N_DEV - 1) + k - 1], recv_sem=recv_sems.at[i * (N_DEV - 1) + k - 1],
                    device_id=peer, device_id_type=pl.DeviceIdType.MESH)
                cp.start()
                copies.append(cp)
        for cp in copies:
            cp.wait()

    hbm = pl.BlockSpec(memory_space=pltpu.HBM)
    return pl.pallas_call(
        body, name=name, in_specs=[hbm] * n + [pl.BlockSpec(memory_space=pl.ANY)] * len(extra), out_specs=[hbm] * n,
        out_shape=[jax.ShapeDtypeStruct(((N_DEV,) + a.shape) if gather else a.shape, a.dtype) for a in arrays],
        scratch_shapes=[pltpu.SemaphoreType.DMA((n * (N_DEV - 1),)), pltpu.SemaphoreType.DMA((n * (N_DEV - 1),)),
                        pltpu.SemaphoreType.DMA((n,))],
    )(*arrays, *extra)


_HBM = pl.BlockSpec(memory_space=pltpu.HBM)
_SEM = pl.BlockSpec(memory_space=pltpu.SEMAPHORE)
N_PEERS = N_DEV - 1


def _mesh_pos():
    pos = [lax.axis_index(ax) for ax in MESH_AXES]
    return pos, 4 * pos[0] + 2 * pos[1] + pos[2]


def _peers(pos):
    out = []
    for k in range(1, N_DEV):
        bits = ((k >> 2) & 1, (k >> 1) & 1, k & 1)
        peer = tuple(1 - p if b else p for p, b in zip(pos, bits))
        out.append((peer, 4 * peer[0] + 2 * peer[1] + peer[2]))
    return out


def _part(x_ref, lead, gather, slot):
    ref = x_ref if lead is None else x_ref.at[lead]
    return ref if gather else ref.at[slot]


OWN_BLOCK_BYTES = 2 * 1024 * 1024


def _landing_zone(a, lead, me, name):
    r, c = a.shape[-2:]
    tr = r
    while tr * _round_up(c, LANES) * a.dtype.itemsize > OWN_BLOCK_BYTES and tr % 32 == 0:
        tr //= 2

    def body(me_ref, x_ref, o_ref):
        o_ref[...] = x_ref[...]

    x_spec = (pl.BlockSpec((tr, c), lambda i, me_ref: (i, 0)) if lead is None
              else pl.BlockSpec((None, tr, c), lambda i, me_ref: (lead, i, 0)))
    grid_spec = pltpu.PrefetchScalarGridSpec(
        num_scalar_prefetch=1, grid=(r // tr,), in_specs=[x_spec],
        out_specs=pl.BlockSpec((None, tr, c), lambda i, me_ref: (me_ref[0], i, 0)))
    return pl.pallas_call(body, name=name, grid_spec=grid_spec, out_shape=jax.ShapeDtypeStruct((N_DEV, r, c), a.dtype),
                          compiler_params=_cparams(1))(jnp.reshape(me, (1,)).astype(jnp.int32), a)


def _split_copies(items, gather, x_refs, land_refs, send_sems, recv_sems):
    pos, me = _mesh_pos()
    copies = []
    for i, (_, lead) in enumerate(items):
        for k, (peer, peer_id) in enumerate(_peers(pos)):
            copies.append(pltpu.make_async_remote_copy(
                src_ref=_part(x_refs[i], lead, gather, peer_id), dst_ref=land_refs[i].at[me],
                send_sem=send_sems.at[i * N_PEERS + k], recv_sem=recv_sems.at[i * N_PEERS + k],
                device_id=peer, device_id_type=pl.DeviceIdType.MESH))
    return copies


def exchange_start(items, gather, lands, name):
    n = len(items)
    lands = list(lands)
    xs = [a for a, _ in items]

    def body(*refs):
        x_refs, land_refs = refs[:n], refs[n:2 * n]
        send_sems, recv_sems, token = refs[2 * n], refs[2 * n + 1], refs[-1]
        for cp in _split_copies(items, gather, x_refs, land_refs, send_sems, recv_sems):
            cp.start()
        token[...] = jnp.zeros_like(token)

    outs = pl.pallas_call(
        body, name=name,
        out_shape=(pltpu.SemaphoreType.DMA((n * N_PEERS,)), pltpu.SemaphoreType.DMA((n * N_PEERS,)),
                   *[pltpu.HBM(v.shape, v.dtype) for v in xs + lands], jax.ShapeDtypeStruct((SUBLANES, LANES), F32)),
        in_specs=[_HBM] * (2 * n),
        out_specs=(_SEM, _SEM, *[_HBM] * (2 * n), pl.BlockSpec(memory_space=pltpu.VMEM)),
        input_output_aliases={i: 2 + i for i in range(2 * n)},
        compiler_params=pltpu.CompilerParams(has_side_effects=pltpu.SideEffectType.DATAFLOW_SIDE_EFFECTING),
    )(*[pltpu.with_memory_space_constraint(v, pltpu.HBM) for v in xs + lands])
    handle = (items, gather, outs[0], outs[1], outs[2:2 + n], outs[2 + n:2 + 2 * n])
    return handle, outs[-1]


def exchange_wait(handle, after, name):
    items, gather, send_sems, recv_sems, x_thru, land_thru = handle
    n = len(items)

    def body(*refs):
        x_refs, land_refs = refs[:n], refs[n:2 * n]
        for cp in _split_copies(items, gather, x_refs, land_refs, refs[2 * n], refs[2 * n + 1]):
            cp.wait_send()
            cp.wait_recv()

    outs = pl.pallas_call(
        body, name=name, out_shape=tuple(pltpu.HBM(v.shape, v.dtype) for v in tuple(x_thru) + tuple(land_thru)),
        in_specs=[_HBM] * (2 * n) + [_SEM, _SEM, pl.BlockSpec(memory_space=pl.ANY)], out_specs=tuple([_HBM] * (2 * n)),
        input_output_aliases={i: i for i in range(2 * n)},
        compiler_params=pltpu.CompilerParams(has_side_effects=pltpu.SideEffectType.DATAFLOW_SIDE_EFFECTING),
    )(*x_thru, *land_thru, send_sems, recv_sems, after)
    return list(outs[:n]), list(outs[n:])


def _adam_update(w, g, m, v):
    nm = ADAM_B1 * m + (1.0 - ADAM_B1) * g
    nv = ADAM_B2 * v + (1.0 - ADAM_B2) * (g * g)
    m_hat = nm / (1.0 - ADAM_B1 ** ADAM_STEP)
    v_hat = nv / (1.0 - ADAM_B2 ** ADAM_STEP)
    return -ADAM_LR * (m_hat / (jnp.sqrt(v_hat) + ADAM_EPS) + ADAM_WD * w), nm, nv


def _sum_slots(s_ref):
    acc = s_ref[0].astype(F32)
    for j in range(1, N_DEV):
        acc = acc + s_ref[j].astype(F32)
    return acc


ADAM_BLOCK_BYTES = 10 * 1024 * 1024


def adamw_sharded(recvs, sends, me, w, m, v, name):
    nl, r, c = w.shape
    assert len(recvs) == nl and len(sends) == nl
    per_row = _round_up(c, LANES) * (nl * (N_DEV + 1) * recvs[0].dtype.itemsize + 7 * 4)
    tr = r
    while tr * per_row > ADAM_BLOCK_BYTES and tr % 16 == 0:
        tr //= 2

    def body(me_ref, *refs):
        s_refs, o_refs = refs[:nl], refs[nl:2 * nl]
        w_ref, m_ref, v_ref, g_ref, d_ref, nm_ref, nv_ref = refs[2 * nl:]
        mine = me_ref[0]

        def total(l):
            acc = jnp.where(mine == 0, o_refs[l][...], s_refs[l][0]).astype(F32)
            for j in range(1, N_DEV):
                acc = acc + jnp.where(mine == j, o_refs[l][...], s_refs[l][j]).astype(F32)
            return acc

        g = total(0)
        for l in range(1, nl):
            g = jnp.where(pl.program_id(0) == l, total(l), g)
        g_ref[...] = g
        d_ref[...], nm_ref[...], nv_ref[...] = _adam_update(w_ref[...], g, m_ref[...], v_ref[...])

    spec = pl.BlockSpec((None, tr, c), lambda l, i, me_ref: (l, i, 0))
    shp = jax.ShapeDtypeStruct(w.shape, F32)
    grid_spec = pltpu.PrefetchScalarGridSpec(
        num_scalar_prefetch=1, grid=(nl, r // tr),
        in_specs=([pl.BlockSpec((N_DEV, tr, c), lambda l, i, me_ref: (0, i, 0))] * nl
                  + [pl.BlockSpec((None, tr, c), lambda l, i, me_ref: (me_ref[0], i, 0))] * nl + [spec, spec, spec]),
        out_specs=[spec] * 4)
    return pl.pallas_call(body, name=name, grid_spec=grid_spec, out_shape=[shp] * 4, compiler_params=_cparams(2))(
        jnp.reshape(me, (1,)).astype(jnp.int32), *recvs, *sends, w, m, v)


def reduce_slots(slots, name):
    _, r, lanes = slots.shape
    tr = _tile(r, 2048)

    def body(s_ref, o_ref):
        o_ref[...] = _sum_slots(s_ref)

    return pl.pallas_call(
        body, name=name, grid=(r // tr,),
        in_specs=[pl.BlockSpec((N_DEV, tr, lanes), lambda i: (0, i, 0))],
        out_specs=pl.BlockSpec((tr, lanes), lambda i: (i, 0)),
        out_shape=jax.ShapeDtypeStruct((r, lanes), F32), compiler_params=_cparams(1),
    )(slots)


def adamw(w, g, m, v, name):
    r, lanes = w.shape
    tr = _tile(r, 2048)

    def body(w_ref, g_ref, m_ref, v_ref, d_ref, nm_ref, nv_ref):
        d_ref[...], nm_ref[...], nv_ref[...] = _adam_update(w_ref[...], g_ref[...], m_ref[...], v_ref[...])

    spec = pl.BlockSpec((tr, lanes), lambda i: (i, 0))
    shp = jax.ShapeDtypeStruct((r, lanes), F32)
    return pl.pallas_call(body, name=name, grid=(r // tr,), in_specs=[spec] * 4, out_specs=[spec] * 3,
                          out_shape=[shp] * 3, compiler_params=_cparams(1))(w, g, m, v)


def cols_from_slots(slots, place, width, name):
    _, rows, c = slots.shape
    tr = _tile(rows, 256)

    def body(s_ref, o_ref):
        o_ref[...] = place(jnp.concatenate([s_ref[j] for j in range(N_DEV)], axis=1))

    return pl.pallas_call(
        body, name=name, grid=(rows // tr,),
        in_specs=[pl.BlockSpec((N_DEV, tr, c), lambda i: (0, i, 0))],
        out_specs=pl.BlockSpec((tr, width), lambda i: (i, 0)),
        out_shape=jax.ShapeDtypeStruct((rows, width), slots.dtype), compiler_params=_cparams(1))(slots)


def slots_from_cols(full, pick, c, name):
    rows, wide = full.shape
    tr = _tile(rows, 256)

    def body(x_ref, o_ref):
        v = pick(x_ref[...])
        for j in range(N_DEV):
            o_ref[j] = v[:, j * c:(j + 1) * c]

    return pl.pallas_call(
        body, name=name, grid=(rows // tr,),
        in_specs=[pl.BlockSpec((tr, wide), lambda i: (i, 0))],
        out_specs=pl.BlockSpec((N_DEV, tr, c), lambda i: (0, i, 0)),
        out_shape=jax.ShapeDtypeStruct((N_DEV, rows, c), full.dtype), compiler_params=_cparams(1))(full)


def _token_tile(cfg):
    return min(cfg.S, 256)


def norm_fwd(cfg, x, w, name):
    ts = _token_tile(cfg)
    d = x.shape[1]

    def fn(ins, _):
        xv, wv = ins
        return [rms(xv, wv)], [], []

    return seq_call(name, fn, (1, cfg.S // ts),
                    [(x, (ts, d), lambda g, t: (t, 0)), (w, (1, d), lambda g, t: (0, 0))],
                    [((cfg.S, d), BF16, (ts, d), lambda g, t: (t, 0))])[0]


def norm_bwd(cfg, x, w, dh, dres, name):
    ts = _token_tile(cfg)
    d = x.shape[1]

    def fn(ins, _):
        xv, wv, dhv, drv = ins
        _, vjp = jax.vjp(rms, xv, wv)
        dx, dw = vjp(dhv.astype(F32))
        return [dx + drv], [dw], []

    row = lambda g, t: (t, 0)
    out = seq_call(name, fn, (1, cfg.S // ts),
                   [(x, (ts, d), row), (w, (1, d), lambda g, t: (0, 0)), (dh, (ts, d), row), (dres, (ts, d), row)],
                   [((cfg.S, d), F32, (ts, d), row)],
                   accs=[((1, d), (1, d), lambda g, t: (0, 0))])
    return out[0], out[1]


def head_fwd_bwd(cfg, x, w, target, name):
    ts = _token_tile(cfg)
    d = x.shape[1]

    def fn(ins, _):
        xv, wv, tv = ins
        y, vjp = jax.vjp(rms, xv, wv)
        err = y - tv
        loss = 0.5 * jnp.sum(err * err) / d
        dx, dw = vjp(err / d)
        return [dx], [jnp.full((SUBLANES, LANES), loss, F32), dw], []

    row = lambda g, t: (t, 0)
    fixed = lambda g, t: (0, 0)
    dx, loss, dw = seq_call(name, fn, (1, cfg.S // ts),
                            [(x, (ts, d), row), (w, (1, d), fixed), (target, (ts, d), row)],
                            [((cfg.S, d), F32, (ts, d), row)],
                            accs=[((SUBLANES, LANES), (SUBLANES, LANES), fixed), ((1, d), (1, d), fixed)])
    return loss, dx, dw


def act_fwd(cfg, gate, up, name):
    ts, c = min(cfg.S, 512), gate.shape[2]

    def fn(ins, _):
        return [silu(ins[0]) * ins[1]], [], []

    blk = lambda g, t: (g, t, 0)
    return seq_call(name, fn, (N_DEV, cfg.S // ts),
                    [(gate, (None, ts, c), blk), (up, (None, ts, c), blk)],
                    [(gate.shape, BF16, (None, ts, c), blk)])[0]


def act_bwd(cfg, gate, up, dact, name):
    ts, c = min(cfg.S, 512), gate.shape[2]

    def fn(ins, _):
        gv, uv, dv = ins
        _, vjp = jax.vjp(lambda a, b: silu(a) * b, gv, uv)
        dg, du = vjp(dv.astype(F32))
        return [dg, du], [], []

    blk = lambda g, t: (g, t, 0)
    shp = (gate.shape, BF16, (None, ts, c), blk)
    return seq_call(name, fn, (N_DEV, cfg.S // ts),
                    [(gate, (None, ts, c), blk), (up, (None, ts, c), blk), (dact, (None, ts, c), blk)], [shp, shp])


CONV_COLS = 256
HALO = SUBLANES


def _shift_down(x, halo, j):
    if j == 0:
        return x
    r8 = _rows_iota(halo.shape)
    top = jnp.where(r8 >= j, pltpu.roll(x[:HALO], j, 0), pltpu.roll(halo, j, 0))
    return jnp.concatenate([top, pltpu.roll(x, j, 0)[HALO:]], axis=0)


def _shift_up(x, halo, j):
    if j == 0:
        return x
    n = x.shape[0]
    r8 = _rows_iota(halo.shape)
    bot = jnp.where(r8 < HALO - j, pltpu.roll(x[n - HALO:], HALO - j, 0), pltpu.roll(halo, HALO - j, 0))
    return jnp.concatenate([pltpu.roll(x, n - j, 0)[:n - HALO], bot], axis=0)


def _conv_tile(cfg):
    return min(cfg.S, 512)


def conv_fwd(cfg, src, col0, width, w, b, name):
    tt, cb = _conv_tile(cfg), CONV_COLS
    c0, hb = col0 // cb, tt // HALO
    nt = cfg.S // tt

    def body(x_ref, h_ref, w_ref, b_ref, o_ref):
        t = pl.program_id(1)
        x = x_ref[...]
        halo = jnp.where(t > 0, h_ref[...], 0.0)
        wv = w_ref[...]
        acc = b_ref[...] + wv[CONV_WIDTH - 1:CONV_WIDTH] * x
        for j in range(1, CONV_WIDTH):
            acc = acc + wv[CONV_WIDTH - 1 - j:CONV_WIDTH - j] * _shift_down(x, halo, j)
        o_ref[...] = acc

    return pl.pallas_call(
        body, name=name, grid=(width // cb, nt),
        in_specs=[pl.BlockSpec((tt, cb), lambda c, t: (t, c0 + c)),
                  pl.BlockSpec((HALO, cb), lambda c, t: (jnp.maximum(t * hb - 1, 0), c0 + c)),
                  pl.BlockSpec((CONV_WIDTH, cb), lambda c, t: (0, c)),
                  pl.BlockSpec((1, cb), lambda c, t: (0, c))],
        out_specs=pl.BlockSpec((tt, cb), lambda c, t: (t, c)),
        out_shape=jax.ShapeDtypeStruct((cfg.S, width), F32), compiler_params=_cparams(2),
    )(src, src, w, b)


def conv_bwd(cfg, src, col0, width, w, dy, name):
    tt, cb = _conv_tile(cfg), CONV_COLS
    c0, hb = col0 // cb, tt // HALO
    nt = cfg.S // tt

    def body(x_ref, h_ref, w_ref, dy_ref, dh_ref, dx_ref, dw_ref, db_ref):
        t = pl.program_id(1)

        @pl.when(t == 0)
        def _():
            dw_ref[...] = jnp.zeros_like(dw_ref)
            db_ref[...] = jnp.zeros_like(db_ref)

        x = x_ref[...]
        halo = jnp.where(t > 0, h_ref[...], 0.0)
        dy = dy_ref[...]
        dhalo = jnp.where(t < nt - 1, dh_ref[...], 0.0)
        wv = w_ref[...]
        dx = wv[CONV_WIDTH - 1:CONV_WIDTH] * dy
        rows = [jnp.sum(dy * x, axis=0, keepdims=True)]
        for j in range(1, CONV_WIDTH):
            dx = dx + wv[CONV_WIDTH - 1 - j:CONV_WIDTH - j] * _shift_up(dy, dhalo, j)
            rows.insert(0, jnp.sum(dy * _shift_down(x, halo, j), axis=0, keepdims=True))
        dx_ref[...] = dx.astype(dx_ref.dtype)
        dw_ref[...] += jnp.concatenate(rows, axis=0)
        db_ref[...] += jnp.sum(dy, axis=0, keepdims=True)

    return pl.pallas_call(
        body, name=name, grid=(width // cb, nt),
        in_specs=[pl.BlockSpec((tt, cb), lambda c, t: (t, c0 + c)),
                  pl.BlockSpec((HALO, cb), lambda c, t: (jnp.maximum(t * hb - 1, 0), c0 + c)),
                  pl.BlockSpec((CONV_WIDTH, cb), lambda c, t: (0, c)),
                  pl.BlockSpec((tt, cb), lambda c, t: (t, c)),
                  pl.BlockSpec((HALO, cb), lambda c, t: (jnp.minimum((t + 1) * hb, nt * hb - 1), c))],
        out_specs=[pl.BlockSpec((tt, cb), lambda c, t: (t, c)),
                   pl.BlockSpec((CONV_WIDTH, cb), lambda c, t: (0, c)),
                   pl.BlockSpec((1, cb), lambda c, t: (0, c))],
        out_shape=[jax.ShapeDtypeStruct((cfg.S, width), BF16), jax.ShapeDtypeStruct((CONV_WIDTH, width), F32),
                   jax.ShapeDtypeStruct((1, width), F32)],
        compiler_params=_cparams(2),
    )(src, src, w, dy, dy)


def _gla_core(gh, q, k, v, g, glr, wg, bg, wn, st):
    n = glr.shape[0]
    causal = _lower_tri(n)
    outs, new = [], []
    for h in range(gh):
        log_a = log_sigmoid(bdot(glr, wg[h], 1, 0) + bg[h]) * (1.0 / GATE_NORM)
        bcum = cumsum_rows(log_a)
        b_last, b_mid = _row(bcum, n - 1), _row(bcum, n // 2)
        qs = q[h] * (HEAD_K ** -0.5)
        scores = jnp.where(causal, bdot(qs * jnp.exp(bcum - b_mid), k[h] * jnp.exp(b_mid - bcum), 1, 1), 0.0)
        o = bdot(scores, v[h], 1, 0) + bdot(qs * jnp.exp(bcum), st[h], 1, 1)
        new.append(st[h] * jnp.exp(b_last) + bdot(v[h], k[h] * jnp.exp(b_last - bcum), 0, 0))
        outs.append(rms(o, wn) * silu(g[h]))
    return jnp.concatenate(outs, axis=1), new


def _gla_ins(cfg, p0, wg, bg, wn, tmap):
    gh = cfg.GH
    ins = []
    for h in range(gh):
        ins.append((p0, (CHUNK, HEAD_K), lambda g, t, h=h: (tmap(t), h)))
    for h in range(gh):
        ins.append((p0, (CHUNK, HEAD_K), lambda g, t, h=h: (tmap(t), gh + h)))
    for h in range(gh):
        ins.append((p0, (CHUNK, HEAD_V), lambda g, t, h=h: (tmap(t), gh + h)))
    for h in range(gh):
        ins.append((p0, (CHUNK, HEAD_V), lambda g, t, h=h: (tmap(t), 2 * gh + h)))
    ins.append((p0, (CHUNK, LANES), lambda g, t: (tmap(t), 10 * gh)))
    for h in range(gh):
        ins.append((wg, (LANES, HEAD_K), lambda g, t, h=h: (0, h)))
    for h in range(gh):
        ins.append((bg, (1, HEAD_K), lambda g, t, h=h: (0, h)))
    ins.append((wn, (1, HEAD_V), lambda g, t: (0, 0)))
    return ins


def _gla_unpack(gh, vals):
    q, k, v, g = (vals[i * gh:(i + 1) * gh] for i in range(4))
    glr = vals[4 * gh]
    wg = vals[4 * gh + 1:5 * gh + 1]
    bg = vals[5 * gh + 1:6 * gh + 1]
    wn = vals[6 * gh + 1]
    return q, k, v, g, glr, wg, bg, wn, vals[6 * gh + 2:]


def gla_fwd(cfg, p0, wg, bg, wn, name):
    gh, nc = cfg.GH, cfg.S // CHUNK

    def fn(ins, st):
        q, k, v, g, glr, wgv, bgv, wnv, _ = _gla_unpack(gh, ins)
        out, new = _gla_core(gh, q, k, v, g, glr, wgv, bgv, wnv, st)
        return [out, jnp.stack(st)], [], new

    return seq_call(name, fn, (1, nc), _gla_ins(cfg, p0, wg, bg, wn, lambda t: t),
                    [((cfg.S, cfg.DV), BF16, (CHUNK, cfg.DV), lambda g, t: (t, 0)),
                     ((nc, gh, HEAD_V, HEAD_K), F32, (None, gh, HEAD_V, HEAD_K), lambda g, t: (t, 0, 0, 0))],
                    carries=[(HEAD_V, HEAD_K)] * gh)


def gla_bwd(cfg, p0, wg, bg, wn, states, dout, name):
    gh, nc = cfg.GH, cfg.S // CHUNK
    rev = lambda t: nc - 1 - t

    def fn(ins, dst):
        q, k, v, g, glr, wgv, bgv, wnv, rest = _gla_unpack(gh, ins)
        st_all, do = rest
        st = [st_all[h] for h in range(gh)]
        _, vjp = jax.vjp(functools.partial(_gla_core, gh), q, k, v, g, glr, wgv, bgv, wnv, st)
        dq, dk, dv, dg, dglr, dwg, dbg, dwn, dstate = vjp((do.astype(F32), list(dst)))
        return ([jnp.concatenate(list(dq) + list(dk) + list(dv) + list(dg), axis=1), dglr],
                [jnp.concatenate(dwg, axis=1), jnp.concatenate(dbg, axis=1), dwn], dstate)

    ins = _gla_ins(cfg, p0, wg, bg, wn, rev)
    ins.append((states, (None, gh, HEAD_V, HEAD_K), lambda g, t: (rev(t), 0, 0, 0)))
    ins.append((dout, (CHUNK, cfg.DV), lambda g, t: (rev(t), 0)))
    wide = 2 * cfg.DK + 2 * cfg.DV
    fixed = lambda g, t: (0, 0)
    return seq_call(name, fn, (1, nc), ins,
                    [((cfg.S, wide), BF16, (CHUNK, wide), lambda g, t: (rev(t), 0)),
                     ((cfg.S, LANES), BF16, (CHUNK, LANES), lambda g, t: (rev(t), 0))],
                    accs=[((LANES, cfg.DK), (LANES, cfg.DK), fixed), ((1, cfg.DK), (1, cfg.DK), fixed),
                          ((1, HEAD_V), (1, HEAD_V), fixed)],
                    carries=[(HEAD_V, HEAD_K)] * gh)


def _lru_core(xc, gate, wa, wi, ba, bi, lam, h_in):
    r = jax.nn.sigmoid(bdot(xc, wa, 1, 0) + ba)
    i = jax.nn.sigmoid(bdot(xc, wi, 1, 0) + bi)
    log_a = LRU_C * r * log_sigmoid(lam)
    a = jnp.exp(log_a)
    u = jnp.sqrt(-_expm1(2.0 * log_a)) * (i * xc)
    first = _rows_iota(a.shape) == 0
    h = lin_scan(a, u + jnp.where(first, a * h_in, 0.0))
    return h * gelu_tanh(gate), _row(h, a.shape[0] - 1)


def _lru_tile(cfg):
    return min(cfg.S, 512)


def _lru_ins(cfg, xc, p0, wa, wi, ba, bi, lam, tmap):
    tt, gh = _lru_tile(cfg), cfg.GH
    vec = lambda g, t: (0, g)
    return [(xc, (tt, LRU_BLOCK), lambda g, t: (tmap(t), g)),
            (p0, (tt, LRU_BLOCK), lambda g, t: (tmap(t), 8 * gh + g)),
            (wa, (None, LRU_BLOCK, LRU_BLOCK), lambda g, t: (g, 0, 0)),
            (wi, (None, LRU_BLOCK, LRU_BLOCK), lambda g, t: (g, 0, 0)),
            (ba, (1, LRU_BLOCK), vec), (bi, (1, LRU_BLOCK), vec), (lam, (1, LRU_BLOCK), vec)]


def lru_fwd(cfg, xc, p0, wa, wi, ba, bi, lam, name):
    tt, nb = _lru_tile(cfg), cfg.NB
    nt = cfg.S // tt

    def fn(ins, c):
        out, h_last = _lru_core(*ins, c[0])
        return [out, c[0]], [], [h_last]

    return seq_call(name, fn, (nb, nt), _lru_ins(cfg, xc, p0, wa, wi, ba, bi, lam, lambda t: t),
                    [((cfg.S, cfg.W), BF16, (tt, LRU_BLOCK), lambda g, t: (t, g)),
                     ((nb, nt, 1, LRU_BLOCK), F32, (None, None, 1, LRU_BLOCK), lambda g, t: (g, t, 0, 0))],
                    carries=[(1, LRU_BLOCK)])


def lru_bwd(cfg, xc, p0, wa, wi, ba, bi, lam, states, dout, name):
    tt, nb = _lru_tile(cfg), cfg.NB
    nt = cfg.S // tt
    rev = lambda t: nt - 1 - t

    def fn(ins, c):
        *fwd_ins, h_in, do = ins
        _, vjp = jax.vjp(_lru_core, *fwd_ins, h_in)
        dxc, dgate, dwa, dwi, dba, dbi, dlam, dh = vjp((do.astype(F32), c[0]))
        return [dxc, dgate], [dwa, dwi, dba, dbi, dlam], [dh]

    ins = _lru_ins(cfg, xc, p0, wa, wi, ba, bi, lam, rev)
    ins.append((states, (None, None, 1, LRU_BLOCK), lambda g, t: (g, rev(t), 0, 0)))
    ins.append((dout, (tt, LRU_BLOCK), lambda g, t: (rev(t), g)))
    mat = ((nb, LRU_BLOCK, LRU_BLOCK), (None, LRU_BLOCK, LRU_BLOCK), lambda g, t: (g, 0, 0))
    vec = ((1, cfg.W), (1, LRU_BLOCK), lambda g, t: (0, g))
    return seq_call(name, fn, (nb, nt), ins,
                    [((cfg.S, cfg.W), F32, (tt, LRU_BLOCK), lambda g, t: (rev(t), g)),
                     ((cfg.S, cfg.W), BF16, (tt, LRU_BLOCK), lambda g, t: (rev(t), g))],
                    accs=[mat, mat, vec, vec, vec], carries=[(1, LRU_BLOCK)])


def _ssd_core(xc, bc, cc, z, dt_raw, dt_bias, a_log, d_skip, gn, st):
    n = xc.shape[0]
    x, bm, cm = silu(xc), silu(bc), silu(cc)
    dt = softplus(dt_raw + dt_bias)
    acs = cumsum_rows(dt * (-jnp.exp(a_log)))
    acs_t = acs.T
    acs_e, dt_e = _expand_heads(acs), _expand_heads(dt)
    last_e = _expand_heads(_row(acs, n - 1))
    causal = _lower_tri(n)
    cb = bdot(cm, bm, 1, 1)
    xdt = x * dt_e
    y_diag = []
    for h, xh in enumerate(_split_heads(xdt)):
        seg = acs[:, h:h + 1] - acs_t[h:h + 1, :]
        decay = jnp.where(causal, jnp.exp(jnp.minimum(seg, 0.0)), 0.0)
        y_diag.append(bdot(cb * decay, xh, 1, 0))
    y = jnp.concatenate(y_diag, axis=1) + bdot(cm, st, 1, 0) * jnp.exp(acs_e)
    new = st * jnp.exp(last_e) + bdot(bm, xdt * jnp.exp(last_e - acs_e), 0, 0)
    y = (y + _expand_heads(d_skip) * x) * silu(z)
    return rms(y, gn), new


def _ssd_ins(cfg, xc, p1, dt_raw, dt_bias, a_log, d_skip, gn, tmap):
    ng = cfg.NG
    vec = lambda g, t: (g, 0, 0)
    return [(xc, (CHUNK, SSD_GW), lambda g, t: (tmap(t), g)),
            (xc, (CHUNK, SSD_N), lambda g, t: (tmap(t), 4 * ng + g)),
            (xc, (CHUNK, SSD_N), lambda g, t: (tmap(t), 5 * ng + g)),
            (p1, (CHUNK, SSD_GW), lambda g, t: (tmap(t), g)),
            (dt_raw, (None, CHUNK, LANES), lambda g, t: (g, tmap(t), 0)),
            (dt_bias, (None, 1, LANES), vec), (a_log, (None, 1, LANES), vec), (d_skip, (None, 1, LANES), vec),
            (gn, (1, SSD_GW), lambda g, t: (0, g))]


def ssd_fwd(cfg, xc, p1, dt_raw, dt_bias, a_log, d_skip, gn, name):
    ng, nc = cfg.NG, cfg.S // CHUNK

    def fn(ins, c):
        out, new = _ssd_core(*ins, c[0])
        return [out, c[0]], [], [new]

    return seq_call(name, fn, (ng, nc), _ssd_ins(cfg, xc, p1, dt_raw, dt_bias, a_log, d_skip, gn, lambda t: t),
                    [((cfg.S, cfg.DI), BF16, (CHUNK, SSD_GW), lambda g, t: (t, g)),
                     ((ng, nc, SSD_N, SSD_GW), F32, (None, None, SSD_N, SSD_GW), lambda g, t: (g, t, 0, 0))],
                    carries=[(SSD_N, SSD_GW)])


def ssd_bwd(cfg, xc, p1, dt_raw, dt_bias, a_log, d_skip, gn, states, dout, name):
    ng, nc = cfg.NG, cfg.S // CHUNK
    rev = lambda t: nc - 1 - t

    def fn(ins, c):
        *fwd_ins, st, do = ins
        _, vjp = jax.vjp(_ssd_core, *fwd_ins, st)
        dxc, dbc, dcc, dz, ddt, dbias, dalog, dskip, dgn, dst = vjp((do.astype(F32), c[0]))
        return [dxc, dbc, dcc, dz, ddt], [dbias, dalog, dskip, dgn], [dst]

    ins = _ssd_ins(cfg, xc, p1, dt_raw, dt_bias, a_log, d_skip, gn, rev)
    ins.append((states, (None, None, SSD_N, SSD_GW), lambda g, t: (g, rev(t), 0, 0)))
    ins.append((dout, (CHUNK, SSD_GW), lambda g, t: (rev(t), g)))
    col = lambda g, t: (rev(t), g)
    vec = ((ng, 1, LANES), (None, 1, LANES), lambda g, t: (g, 0, 0))
    return seq_call(name, fn, (ng, nc), ins,
                    [((cfg.S, cfg.DI), F32, (CHUNK, SSD_GW), col),
                     ((cfg.S, ng * SSD_N), F32, (CHUNK, SSD_N), col),
                     ((cfg.S, ng * SSD_N), F32, (CHUNK, SSD_N), col),
                     ((cfg.S, cfg.DI), BF16, (CHUNK, SSD_GW), col),
                     ((ng, cfg.S, LANES), F32, (None, CHUNK, LANES), lambda g, t: (g, rev(t), 0))],
                    accs=[vec, vec, vec, ((1, cfg.DI), (1, SSD_GW), lambda g, t: (0, g))],
                    carries=[(SSD_N, SSD_GW)])


PACK_ALIGN = SUBLANES * LANES
PACK_ROWS = 256


def pack(arrays):
    pieces = []
    for a in arrays:
        flat = a.reshape(-1).astype(F32)
        pad = _round_up(flat.shape[0], PACK_ALIGN) - flat.shape[0]
        pieces.append(jnp.pad(flat, (0, pad)) if pad else flat)
    flat = jnp.concatenate(pieces)
    pad = _round_up(flat.shape[0], PACK_ROWS * LANES) - flat.shape[0]
    return jnp.pad(flat, (0, pad)).reshape(-1, LANES)


def unpack(buf, shapes):
    lead = buf.shape[:-2]
    flat = buf.reshape(lead + (-1,))
    out, off = [], 0
    for s in shapes:
        n = math.prod(s)
        out.append(flat[..., off:off + n].reshape(lead + tuple(s)))
        off += _round_up(n, PACK_ALIGN)
    return out


def _slots_to_cols(slots):
    return slots.transpose(1, 0, 2).reshape(slots.shape[1], -1)


def _even_in_padded(cfg, w):
    main = 2 * cfg.DK + 2 * cfg.DV
    return jnp.concatenate([w[:, :main], w[:, main + GATE_RANK:], w[:, main:main + GATE_RANK],
                            jnp.zeros((w.shape[0], cfg.EP - cfg.EVEN_IN), w.dtype)], axis=1)


def _even_in_unpadded(cfg, wp):
    main = 2 * cfg.DK + 2 * cfg.DV
    rest = main + 2 * cfg.W
    return jnp.concatenate([wp[:, :main], wp[:, rest:rest + GATE_RANK], wp[:, main:rest]], axis=1)


def _odd_in_padded(cfg, w):
    return jnp.concatenate([w, jnp.zeros((w.shape[0], cfg.OP - cfg.ODD_IN), w.dtype)], axis=1)


def _odd_in_unpadded(cfg, wp):
    return wp[:, :cfg.ODD_IN]


def _group_lanes(cfg, v):
    lead = v.shape[:-1]
    g = jnp.moveaxis(v.reshape(lead + (cfg.NG, SSD_HG)), -2, 0)
    return jnp.pad(g, [(0, 0)] * (g.ndim - 1) + [(0, LANES - SSD_HG)])


def _ungroup_lanes(cfg, g):
    v = jnp.moveaxis(g[..., :SSD_HG], 0, -2)
    return v.reshape(v.shape[:-2] + (cfg.NH,))


def train_step(cfg, p, loss_target):
    S, D = cfg.S, cfg.D
    me = 4 * lax.axis_index("x") + 2 * lax.axis_index("y") + lax.axis_index("c")

    big = ["ev_w_in", "ev_w_out", "od_w_in", "od_w_out", "ffn_w_gate", "ffn_w_up", "ffn_w_down"]
    small_sharded = ["ev_gla_w_gate", "ev_lru_conv_w", "od_norm", "od_conv_w", "od_conv_b", "od_gnorm"]
    replicated = ["ev_norm", "ev_gla_b_gate", "ev_gla_w_onorm", "ev_lru_conv_b", "ev_lru_w_a", "ev_lru_b_a",
                  "ev_lru_w_i", "ev_lru_b_i", "ev_lru_lam", "od_dt_bias", "od_a_log", "od_d_skip", "ffn_norm",
                  "final_norm"]

    transposed = ("ffn_w_gate", "ffn_w_up")
    view = lambda n, a: jnp.swapaxes(a, 1, 2) if n in transposed else a
    wb = {n: view(n, p[n]).astype(BF16) for n in big}
    ffn_items = lambda l: [(wb["ffn_w_gate"], l), (wb["ffn_w_up"], l), (wb["ffn_w_down"], l)]
    ss_shapes = [p[n].shape for n in small_sharded]
    groups = [[(pack([p[n] for n in small_sharded]), None), (wb["ev_w_in"], 0), (wb["ev_w_out"], 0)], ffn_items(0),
              [(wb["od_w_in"], 0), (wb["od_w_out"], 0)], ffn_items(1)]
    gathers, tokens = [], []
    for i, g in enumerate(groups):
        lands = [_landing_zone(a, lead, me, f"gather_own{i}_{j}") for j, (a, lead) in enumerate(g)]
        handle, token = exchange_start(g, True, lands, f"gather_start{i}")
        gathers.append(handle)
        tokens.append(token)
    all_started = tokens[0][:1, :1] + tokens[1][:1, :1] + tokens[2][:1, :1] + tokens[3][:1, :1]

    ss_all, gw_ev_in, gw_ev_out = exchange_wait(gathers[0], all_started, "gather_wait0")[1]
    gs = dict(zip(small_sharded, unpack(ss_all, ss_shapes)))
    w_ev_in = cols_from_slots(gw_ev_in, functools.partial(_even_in_padded, cfg), cfg.EP, "ev_w_in_cols")
    w_ev_out = gw_ev_out.reshape(D, D)

    gla_wg = jnp.pad(_slots_to_cols(gs["ev_gla_w_gate"][:, 0]), ((0, LANES - GATE_RANK), (0, 0)))
    lru_cw = _slots_to_cols(gs["ev_lru_conv_w"][:, 0])
    od_norm = gs["od_norm"].transpose(1, 0, 2).reshape(1, D)
    od_cw = _slots_to_cols(gs["od_conv_w"][:, 0])
    od_cb = gs["od_conv_b"].transpose(1, 0, 2).reshape(1, cfg.CD)
    od_gn = gs["od_gnorm"].transpose(1, 0, 2).reshape(1, cfg.DI)

    x0 = p["x"][0]
    target = loss_target[0]
    ev_norm = p["ev_norm"] + all_started
    bg = p["ev_gla_b_gate"]
    wn = p["ev_gla_w_onorm"]
    lru_cb = p["ev_lru_conv_b"]
    wa, wi = p["ev_lru_w_a"][0], p["ev_lru_w_i"][0]
    ba, bi, lam = p["ev_lru_b_a"], p["ev_lru_b_i"], p["ev_lru_lam"]
    dt_bias, a_log, d_skip = (_group_lanes(cfg, p[n]) for n in ("od_dt_bias", "od_a_log", "od_d_skip"))
    ffn_norm = [p["ffn_norm"][l:l + 1] for l in range(2)]
    final_norm = p["final_norm"].reshape(1, D)

    def ffn_forward(l, x):
        w_gate, w_up, w_down = exchange_wait(gathers[1 + 2 * l], x, f"gather_wait{1 + 2 * l}")[1]
        h = norm_fwd(cfg, x, ffn_norm[l], f"ffn{l}_norm")
        gate = matmul(h, w_gate, tb=True, b_slot="n", name=f"ffn{l}_gate")
        up = matmul(h, w_up, tb=True, b_slot="n", name=f"ffn{l}_up")
        act = act_fwd(cfg, gate, up, f"ffn{l}_act")
        out = matmul(act, w_down, a_slot="k", b_slot="k", res=x, name=f"ffn{l}_down", tn=512)
        return out, (h, gate, up, act, w_gate, w_up, w_down)

    h0 = norm_fwd(cfg, x0, ev_norm, "ev_norm")
    p0 = matmul(h0, w_ev_in, name="ev_in", tn=768)
    gla_out, gla_states = gla_fwd(cfg, p0, gla_wg, bg, wn, "gla_fwd")
    lru_col = 2 * cfg.DK + 2 * cfg.DV
    lru_xc = conv_fwd(cfg, p0, lru_col, cfg.W, lru_cw, lru_cb, "lru_conv")
    lru_out, lru_states = lru_fwd(cfg, lru_xc, p0, wa, wi, ba, bi, lam, "lru_fwd")
    mix = jnp.concatenate([gla_out, lru_out], axis=1)
    x1 = matmul(mix, w_ev_out, res=x0, name="ev_out")
    x2, ffn0_saved = ffn_forward(0, x1)

    gw_od_in, gw_od_out = exchange_wait(gathers[2], x2, "gather_wait2")[1]
    w_od_in = cols_from_slots(gw_od_in, functools.partial(_odd_in_padded, cfg), cfg.OP, "od_w_in_cols")
    w_od_out = gw_od_out.reshape(cfg.DI, D)
    h2 = norm_fwd(cfg, x2, od_norm, "od_norm")
    p1 = matmul(h2, w_od_in, name="od_in", tn=768)
    od_xc = conv_fwd(cfg, p1, cfg.DI, cfg.CD, od_cw, od_cb, "od_conv")
    dt_col = cfg.DI + cfg.CD
    dt_raw = _group_lanes(cfg, p1[:, dt_col:dt_col + cfg.NH])
    ssd_out, ssd_states = ssd_fwd(cfg, od_xc, p1, dt_raw, dt_bias, a_log, d_skip, od_gn, "ssd_fwd")
    x3 = matmul(ssd_out, w_od_out, res=x2, name="od_out")
    x4, ffn1_saved = ffn_forward(1, x3)

    loss_part, dx4, d_final_norm = head_fwd_bwd(cfg, x4, final_norm, target, "head")
    loss = lax.psum(loss_part[0, 0], MESH_AXES)

    def ffn_backward(l, x, saved, dx_out, after):
        h, gate, up, act, w_gate, w_up, w_down = saved
        dact = matmul(dx_out, w_down, tb=True, b_slot="n", after=after, name=f"ffn{l}_dact")
        d_down = matmul(act, dx_out, ta=True, a_slot="m", out_dtype=BF16, name=f"ffn{l}_dwdown")
        dgate, dup = act_bwd(cfg, gate, up, dact, f"ffn{l}_act_bwd")
        dh = matmul(dgate, w_gate, a_slot="k", b_slot="k", name=f"ffn{l}_dh_gate", tn=512)
        dh = matmul(dup, w_up, a_slot="k", b_slot="k", res=dh, name=f"ffn{l}_dh_up", tn=512)
        d_gate = matmul(dgate, h, ta=True, a_slot="m", out_dtype=BF16, name=f"ffn{l}_dwgate")
        d_up = matmul(dup, h, ta=True, a_slot="m", out_dtype=BF16, name=f"ffn{l}_dwup")
        dx, dnorm = norm_bwd(cfg, x, ffn_norm[l], dh, dx_out, f"ffn{l}_norm_bwd")
        sent, token = start_grads([d_gate, d_up, d_down], f"grads_start_ffn{l}")
        return dx, dnorm, sent, token

    def start_grads(arrays, name):
        return exchange_start([(a, None) for a in arrays], False, [lax.empty(a.shape, a.dtype) for a in arrays], name)

    dx3, d_ffn_norm1, sent_ffn1, token = ffn_backward(1, x3, ffn1_saved, dx4, None)

    d_ssd_out = matmul(dx3, w_od_out, tb=True, after=token, name="od_dmix")
    d_od_out = matmul(ssd_out, dx3, ta=True, out_dtype=BF16, name="od_dwout")
    dxs, dbm, dcm, dz, d_dt_raw, d_dt_bias, d_a_log, d_d_skip, d_od_gn = ssd_bwd(
        cfg, od_xc, p1, dt_raw, dt_bias, a_log, d_skip, od_gn, ssd_states, d_ssd_out, "ssd_bwd")
    conv_parts, col = [], 0
    for part, dy in (("x", dxs), ("b", dbm), ("c", dcm)):
        width = dy.shape[1]
        conv_parts.append(conv_bwd(cfg, p1, cfg.DI + col, width, od_cw[:, col:col + width], dy, "od_conv_bwd_" + part))
        col += width
    d_od_cw = jnp.concatenate([c[1] for c in conv_parts], axis=1)
    d_od_cb = jnp.concatenate([c[2] for c in conv_parts], axis=1)
    d_dt = _ungroup_lanes(cfg, d_dt_raw).astype(BF16)
    dp1 = jnp.concatenate([dz] + [c[0] for c in conv_parts] + [d_dt, jnp.zeros((S, cfg.OP - cfg.ODD_IN), BF16)], axis=1)
    dh2 = matmul(dp1, w_od_in, tb=True, name="od_dh", tk=1536)
    d_od_in = matmul(h2, dp1, ta=True, out_dtype=BF16, name="od_dwin", tn=768)
    dx2, d_od_norm = norm_bwd(cfg, x2, od_norm, dh2, dx3, "od_norm_bwd")
    d_od_in_slots = slots_from_cols(d_od_in, functools.partial(_odd_in_unpadded, cfg), p["od_w_in"].shape[2],
                                    "od_dwin_slots")
    sent_od, token = start_grads([d_od_in_slots, d_od_out.reshape((N_DEV,) + p["od_w_out"].shape[1:])], "grads_start_od")

    dx1, d_ffn_norm0, sent_ffn0, token = ffn_backward(0, x1, ffn0_saved, dx2, token)

    d_mix = matmul(dx1, w_ev_out, tb=True, after=token, name="ev_dmix")
    d_ev_out = matmul(mix, dx1, ta=True, out_dtype=BF16, name="ev_dwout")
    d_qkvg, d_glr, d_gla_wg, d_bg, d_wn = gla_bwd(cfg, p0, gla_wg, bg, wn, gla_states, d_mix[:, :cfg.DV], "gla_bwd")
    d_lru_xc, d_gate_br, d_wa, d_wi, d_ba, d_bi, d_lam = lru_bwd(
        cfg, lru_xc, p0, wa, wi, ba, bi, lam, lru_states, d_mix[:, cfg.DV:], "lru_bwd")
    d_xbr, d_lru_cw, d_lru_cb = conv_bwd(cfg, p0, lru_col, cfg.W, lru_cw, d_lru_xc, "lru_conv_bwd")
    dp0 = jnp.concatenate([d_qkvg, d_xbr, d_gate_br, d_glr, jnp.zeros((S, cfg.EP - lru_col - 2 * cfg.W - LANES), BF16)],
                          axis=1)
    d_ev_in = matmul(h0, dp0, ta=True, out_dtype=BF16, name="ev_dwin", tn=768)
    d_ev_in_slots = slots_from_cols(d_ev_in, functools.partial(_even_in_unpadded, cfg), p["ev_w_in"].shape[2],
                                    "ev_dwin_slots")
    sent_ev, token = start_grads([d_ev_in_slots, d_ev_out.reshape((N_DEV,) + p["ev_w_out"].shape[1:])], "grads_start_ev")
    dh0 = matmul(dp0, w_ev_in, tb=True, after=token, name="ev_dh", tk=1792)
    grad_x, d_ev_norm = norm_bwd(cfg, x0, ev_norm, dh0, dx1, "ev_norm_bwd")

    out = {"loss": loss, "grad_x": grad_x[None]}

    def update(names, sent, after, wait_name):
        s, r = exchange_wait(sent[0], after, wait_name + "0")
        sends, recvs = [[a] for a in s], [[a] for a in r]
        for extra in sent[1:]:
            s, r = exchange_wait(extra, after, wait_name + "1")
            for i in range(len(names)):
                sends[i].append(s[i])
                recvs[i].append(r[i])
        for i, n in enumerate(names):
            res = adamw_sharded(recvs[i], sends[i], me, view(n, p[n]), view(n, p["m_" + n]), view(n, p["v_" + n]),
                                "adamw_" + n)
            out["grad_" + n], out["delta_" + n], out["new_m_" + n], out["new_v_" + n] = (view(n, a) for a in res)
        return res[-1]

    small_full = {
        "ev_gla_w_gate": d_gla_wg[:GATE_RANK][None], "ev_lru_conv_w": d_lru_cw[None], "od_norm": d_od_norm,
        "od_conv_w": d_od_cw[None], "od_conv_b": d_od_cb, "od_gnorm": d_od_gn,
        "ev_norm": d_ev_norm, "ev_gla_b_gate": d_bg, "ev_gla_w_onorm": d_wn, "ev_lru_conv_b": d_lru_cb,
        "ev_lru_w_a": d_wa[None], "ev_lru_b_a": d_ba, "ev_lru_w_i": d_wi[None], "ev_lru_b_i": d_bi,
        "ev_lru_lam": d_lam, "od_dt_bias": _ungroup_lanes(cfg, d_dt_bias), "od_a_log": _ungroup_lanes(cfg, d_a_log),
        "od_d_skip": _ungroup_lanes(cfg, d_d_skip), "ffn_norm": jnp.concatenate([d_ffn_norm0, d_ffn_norm1], axis=0),
        "final_norm": d_final_norm.reshape(D),
    }
    small = small_sharded + replicated
    small_packed = pack([small_full[n] for n in small])
    sent_small, token = exchange_start([(small_packed, None)], True,
                                       [_landing_zone(small_packed, None, me, "gather_small_grads_own")],
                                       "gather_small_grads")

    done = update(["od_w_in", "od_w_out"], [sent_od], token, "grads_wait_od")
    done = update(["ffn_w_gate", "ffn_w_up", "ffn_w_down"], [sent_ffn0, sent_ffn1], done, "grads_wait_ffn")
    done = update(["ev_w_in", "ev_w_out"], [sent_ev], done, "grads_wait_ev")

    small_all = exchange_wait(sent_small, done, "gather_small_grads_wait")[1][0]
    g_small = dict(zip(small, unpack(reduce_slots(small_all, "sum_small_grads"), [small_full[n].shape for n in small])))
    for n in small_sharded:
        width = p[n].shape[-1]
        g_small[n] = lax.dynamic_slice_in_dim(g_small[n], me * width, width, axis=g_small[n].ndim - 1)
    shapes = [p[n].shape for n in small]
    g_buf = pack([g_small[n] for n in small])
    delta, new_m, new_v = adamw(pack([p[n] for n in small]), g_buf, pack([p["m_" + n] for n in small]),
                                pack([p["v_" + n] for n in small]), "adamw_small")
    for kind, buf in (("grad_", g_buf), ("delta_", delta), ("new_m_", new_m), ("new_v_", new_v)):
        for n, a in zip(small, unpack(buf, shapes)):
            out[kind + n] = a
    return out


WEIGHTS = ['ev_norm', 'ev_w_in', 'ev_gla_w_gate', 'ev_gla_b_gate', 'ev_gla_w_onorm', 'ev_lru_conv_w', 'ev_lru_conv_b',
           'ev_lru_w_a', 'ev_lru_b_a', 'ev_lru_w_i', 'ev_lru_b_i', 'ev_lru_lam', 'ev_w_out', 'od_norm', 'od_w_in',
           'od_conv_w', 'od_conv_b', 'od_dt_bias', 'od_a_log', 'od_d_skip', 'od_gnorm', 'od_w_out', 'ffn_norm',
           'ffn_w_gate', 'ffn_w_up', 'ffn_w_down', 'final_norm']


def kernel(x, ev_norm, ev_w_in, ev_gla_w_gate, ev_gla_b_gate, ev_gla_w_onorm, ev_lru_conv_w, ev_lru_conv_b, ev_lru_w_a, ev_lru_b_a, ev_lru_w_i, ev_lru_b_i, ev_lru_lam, ev_w_out, od_norm, od_w_in, od_conv_w, od_conv_b, od_dt_bias, od_a_log, od_d_skip, od_gnorm, od_w_out, ffn_norm, ffn_w_gate, ffn_w_up, ffn_w_down, final_norm, loss_target, m_ev_norm, m_ev_w_in, m_ev_gla_w_gate, m_ev_gla_b_gate, m_ev_gla_w_onorm, m_ev_lru_conv_w, m_ev_lru_conv_b, m_ev_lru_w_a, m_ev_lru_b_a, m_ev_lru_w_i, m_ev_lru_b_i, m_ev_lru_lam, m_ev_w_out, m_od_norm, m_od_w_in, m_od_conv_w, m_od_conv_b, m_od_dt_bias, m_od_a_log, m_od_d_skip, m_od_gnorm, m_od_w_out, m_ffn_norm, m_ffn_w_gate, m_ffn_w_up, m_ffn_w_down, m_final_norm, v_ev_norm, v_ev_w_in, v_ev_gla_w_gate, v_ev_gla_b_gate, v_ev_gla_w_onorm, v_ev_lru_conv_w, v_ev_lru_conv_b, v_ev_lru_w_a, v_ev_lru_b_a, v_ev_lru_w_i, v_ev_lru_b_i, v_ev_lru_lam, v_ev_w_out, v_od_norm, v_od_w_in, v_od_conv_w, v_od_conv_b, v_od_dt_bias, v_od_a_log, v_od_d_skip, v_od_gnorm, v_od_w_out, v_ffn_norm, v_ffn_w_gate, v_ffn_w_up, v_ffn_w_down, v_final_norm):
    args = dict(locals())
    p = {n: a for n, a in args.items() if n != "loss_target"}
    cfg = Cfg(S=x.shape[1], D=x.shape[2], DFF=ffn_w_gate.shape[2] * N_DEV)
    out = train_step(cfg, p, loss_target)
    return (out["loss"], out["grad_x"], *[out["grad_" + w] for w in WEIGHTS], *[out["delta_" + w] for w in WEIGHTS],
            *[out["new_m_" + w] for w in WEIGHTS], *[out["new_v_" + w] for w in WEIGHTS])
```

```python
import functools
import math
from typing import NamedTuple

import jax
import jax.numpy as jnp
from jax import lax
from jax.experimental import pallas as pl
from jax.experimental.pallas import tpu as pltpu

F32 = jnp.float32
BF16 = jnp.bfloat16
MESH_AXES = ("x", "y", "c")
N_DEV = 8
LANES = 128
SUBLANES = 8
VMEM_LIMIT = 56 * 1024 * 1024

NORM_EPS = 1e-6
CONV_WIDTH = 4
CHUNK = 64
HEAD_K = 128
HEAD_V = 256
GATE_RANK = 16
GATE_NORM = 16.0
LRU_BLOCK = 128
LRU_C = 8.0
SSD_P = 64
SSD_N = 128
SSD_HG = 8
SSD_GW = SSD_HG * SSD_P

ADAM_LR = 0.001
ADAM_B1 = 0.9
ADAM_B2 = 0.999
ADAM_EPS = 1e-08
ADAM_WD = 0.01
ADAM_STEP = 10


class Cfg(NamedTuple):
    S: int
    D: int
    DFF: int

    @property
    def GH(self):
        return self.D // 512

    @property
    def NB(self):
        return self.D // 256

    @property
    def NG(self):
        return self.D // 256

    @property
    def DK(self):
        return HEAD_K * self.GH

    @property
    def DV(self):
        return HEAD_V * self.GH

    @property
    def W(self):
        return LRU_BLOCK * self.NB

    @property
    def DI(self):
        return SSD_GW * self.NG

    @property
    def CD(self):
        return self.DI + 2 * self.NG * SSD_N

    @property
    def NH(self):
        return SSD_HG * self.NG

    @property
    def EVEN_IN(self):
        return 2 * self.DK + 2 * self.DV + GATE_RANK + 2 * self.W

    @property
    def ODD_IN(self):
        return self.DI + self.CD + self.NH

    @property
    def EP(self):
        return _round_up(2 * self.DK + 2 * self.DV + 2 * self.W + LANES, 768)

    @property
    def OP(self):
        return _round_up(self.DI + self.CD + LANES, 768)


def _round_up(n, m):
    return (n + m - 1) // m * m


def _tile(n, pref):
    if n <= pref:
        return n
    t = pref - pref % LANES
    while n % t:
        t -= LANES
    return t


def _cparams(n_axes):
    return pltpu.CompilerParams(dimension_semantics=("arbitrary",) * n_axes, vmem_limit_bytes=VMEM_LIMIT)


def _dg(a, b, ca, cb):
    return lax.dot_general(a.astype(BF16), b.astype(BF16), (((ca,), (cb,)), ((), ())), preferred_element_type=F32)


@functools.partial(jax.custom_vjp, nondiff_argnums=(2, 3))
def bdot(a, b, ca, cb):
    return _dg(a, b, ca, cb)


def _bdot_fwd(a, b, ca, cb):
    return _dg(a, b, ca, cb), (a, b)


def _bdot_bwd(ca, cb, res, g):
    a, b = res
    da = _dg(g, b, 1, 1 - cb) if ca == 1 else _dg(b, g, 1 - cb, 1)
    db = _dg(a, g, 1 - ca, 0) if cb == 0 else _dg(g, a, 0, 1 - ca)
    return da.astype(a.dtype), db.astype(b.dtype)


bdot.defvjp(_bdot_fwd, _bdot_bwd)


def _lower_tri(n):
    r = lax.broadcasted_iota(jnp.int32, (n, n), 0)
    c = lax.broadcasted_iota(jnp.int32, (n, n), 1)
    return c <= r


@jax.custom_vjp
def cumsum_rows(x):
    n = x.shape[0]
    return lax.dot_general(_lower_tri(n).astype(F32), x, (((1,), (0,)), ((), ())),
                           precision=lax.Precision.HIGHEST, preferred_element_type=F32)


def _cumsum_fwd(x):
    return cumsum_rows(x), None


def _cumsum_bwd(_, g):
    n = g.shape[0]
    return (lax.dot_general(_lower_tri(n).astype(F32), g, (((0,), (0,)), ((), ())),
                            precision=lax.Precision.HIGHEST, preferred_element_type=F32),)


cumsum_rows.defvjp(_cumsum_fwd, _cumsum_bwd)


def _row(x, i):
    r = lax.broadcasted_iota(jnp.int32, x.shape, 0)
    return jnp.sum(jnp.where(r == i, x, 0.0), axis=0, keepdims=True)


def _softplus_raw(x):
    return jnp.maximum(x, 0.0) + jnp.log(1.0 + jnp.exp(-jnp.abs(x)))


@jax.custom_vjp
def softplus(x):
    return _softplus_raw(x)


softplus.defvjp(lambda x: (_softplus_raw(x), x), lambda x, g: (g * jax.nn.sigmoid(x),))


@jax.custom_vjp
def log_sigmoid(x):
    return -_softplus_raw(-x)


log_sigmoid.defvjp(lambda x: (-_softplus_raw(-x), x), lambda x, g: (g * jax.nn.sigmoid(-x),))


def silu(x):
    return x * jax.nn.sigmoid(x)


def gelu_tanh(x):
    return 0.5 * x * (1.0 + jnp.tanh(math.sqrt(2.0 / math.pi) * (x + 0.044715 * (x * x * x))))


def _expm1(x):
    series = x * (1.0 + 0.5 * x * (1.0 + (1.0 / 3.0) * x))
    return jnp.where(jnp.abs(x) < 1e-2, series, jnp.exp(x) - 1.0)


def rms(x, w):
    return x * lax.rsqrt(jnp.mean(x * x, axis=-1, keepdims=True) + NORM_EPS) * w


def _rows_iota(shape):
    return lax.broadcasted_iota(jnp.int32, shape, 0)


def _scan_up(a, u):
    n = a.shape[0]
    r = _rows_iota(a.shape)
    d = 1
    while d < n:
        m = r >= d
        a_s = jnp.where(m, pltpu.roll(a, d, 0), 1.0)
        u_s = jnp.where(m, pltpu.roll(u, d, 0), 0.0)
        u = a * u_s + u
        a = a * a_s
        d *= 2
    return u


def _scan_down(a, u):
    n = a.shape[0]
    r = _rows_iota(a.shape)
    d = 1
    while d < n:
        m = r < n - d
        a_s = jnp.where(m, pltpu.roll(a, n - d, 0), 1.0)
        u_s = jnp.where(m, pltpu.roll(u, n - d, 0), 0.0)
        u = a * u_s + u
        a = a * a_s
        d *= 2
    return u


@jax.custom_vjp
def lin_scan(a, u):
    return _scan_up(a, u)


def _lin_scan_fwd(a, u):
    h = _scan_up(a, u)
    return h, (a, h)


def _lin_scan_bwd(res, g):
    a, h = res
    n = a.shape[0]
    r = _rows_iota(a.shape)
    a_next = jnp.where(r < n - 1, pltpu.roll(a, n - 1, 0), 0.0)
    gt = _scan_down(a_next, g)
    h_prev = jnp.where(r >= 1, pltpu.roll(h, 1, 0), 0.0)
    return gt * h_prev, gt


lin_scan.defvjp(_lin_scan_fwd, _lin_scan_bwd)


def _expand_heads(v):
    r = v.shape[0]
    return jnp.concatenate([jnp.broadcast_to(v[:, h:h + 1], (r, SSD_P)) for h in range(SSD_HG)], axis=1)


@jax.custom_vjp
def _split_heads(x):
    return tuple(x[:, h * SSD_P:(h + 1) * SSD_P] for h in range(SSD_HG))


_split_heads.defvjp(lambda x: (_split_heads(x), None), lambda _, gs: (jnp.concatenate(gs, axis=1),))


def matmul(a, b, *, ta=False, tb=False, a_slot=None, b_slot=None, b_lead=(), res=None, after=None, out_dtype=F32,
           name, tm=1024, tn=1024, tk=2048):
    lead = tuple(b_lead)
    ra, ca_ = a.shape[-2:]
    rb, cb_ = b.shape[-2:]
    m_st, ka_st = (ca_, ra) if ta else (ra, ca_)
    kb_st, n_st = (cb_, rb) if tb else (rb, cb_)
    kslot = a_slot == "k"
    assert kslot == (b_slot == "k")
    m = m_st * (N_DEV if a_slot == "m" else 1)
    n = n_st * (N_DEV if b_slot == "n" else 1)
    assert ka_st == kb_st, (a.shape, b.shape, ta, tb)
    tm = m_st if a_slot == "m" else _tile(m, tm)
    tn = n_st if b_slot == "n" else _tile(n, tn)
    tk = ka_st if kslot else _tile(ka_st, tk)
    nk = ka_st // tk
    ca, cb = (0 if ta else 1), (1 if tb else 0)
    nl = (None,) * len(lead)

    if a_slot is None:
        a_spec = pl.BlockSpec((tk, tm), lambda i, j, k: (k, i)) if ta else pl.BlockSpec((tm, tk), lambda i, j, k: (i, k))
    elif a_slot == "m":
        a_spec = (pl.BlockSpec((None, tk, tm), lambda i, j, k: (i, k, 0)) if ta
                  else pl.BlockSpec((None, tm, tk), lambda i, j, k: (i, 0, k)))
    else:
        a_spec = (pl.BlockSpec((N_DEV, tk, tm), lambda i, j, k: (0, 0, i)) if ta
                  else pl.BlockSpec((N_DEV, tm, tk), lambda i, j, k: (0, i, 0)))
    if b_slot is None:
        b_spec = (pl.BlockSpec(nl + (tn, tk), lambda i, j, k: lead + (j, k)) if tb
                  else pl.BlockSpec(nl + (tk, tn), lambda i, j, k: lead + (k, j)))
    elif b_slot == "n":
        b_spec = (pl.BlockSpec((None,) + nl + (tn, tk), lambda i, j, k: (j,) + lead + (0, k)) if tb
                  else pl.BlockSpec((None,) + nl + (tk, tn), lambda i, j, k: (j,) + lead + (k, 0)))
    else:
        b_spec = (pl.BlockSpec((N_DEV,) + nl + (tn, tk), lambda i, j, k: (0,) + lead + (j, 0)) if tb
                  else pl.BlockSpec((N_DEV,) + nl + (tk, tn), lambda i, j, k: (0,) + lead + (0, j)))
    if a_slot == "m":
        o_spec, o_shape = pl.BlockSpec((None, tm, tn), lambda i, j, k: (i, 0, j)), (N_DEV, tm, n)
    elif b_slot == "n":
        o_spec, o_shape = pl.BlockSpec((None, tm, tn), lambda i, j, k: (j, i, 0)), (N_DEV, m, tn)
    else:
        o_spec, o_shape = pl.BlockSpec((tm, tn), lambda i, j, k: (i, j)), (m, n)
    assert res is None or (a_slot != "m" and b_slot != "n")

    def dot(x, y):
        return lax.dot_general(x.astype(BF16), y.astype(BF16), (((ca,), (cb,)), ((), ())), preferred_element_type=F32)

    def body(*refs):
        a_ref, b_ref = refs[:2]
        r_ref = refs[2] if res is not None else None
        o_ref = refs[2 + (res is not None) + (after is not None)]

        def finish(acc):
            if r_ref is not None:
                acc = acc + r_ref[...].astype(F32)
            o_ref[...] = acc.astype(o_ref.dtype)

        if kslot:
            acc = dot(a_ref[0], b_ref[0])
            for s in range(1, N_DEV):
                acc = acc + dot(a_ref[s], b_ref[s])
            finish(acc)
        elif nk == 1:
            finish(dot(a_ref[...], b_ref[...]))
        else:
            acc_ref = refs[-1]
            k = pl.program_id(2)

            @pl.when(k == 0)
            def _():
                acc_ref[...] = dot(a_ref[...], b_ref[...])

            @pl.when(k > 0)
            def _():
                acc_ref[...] += dot(a_ref[...], b_ref[...])

            @pl.when(k == nk - 1)
            def _():
                finish(acc_ref[...])

    in_specs = [a_spec, b_spec]
    args = [a, b]
    if res is not None:
        in_specs.append(pl.BlockSpec((tm, tn), lambda i, j, k: (i, j)))
        args.append(res)
    if after is not None:
        in_specs.append(pl.BlockSpec(memory_space=pl.ANY))
        args.append(after)
    return pl.pallas_call(
        body, name=name, grid=(m // tm, n // tn, nk), in_specs=in_specs, out_specs=o_spec,
        out_shape=jax.ShapeDtypeStruct(o_shape, out_dtype),
        scratch_shapes=[pltpu.VMEM((tm, tn), F32)] if nk > 1 else [], compiler_params=_cparams(3),
    )(*args)


def seq_call(name, fn, grid, ins, outs, accs=(), carries=()):
    n_in, n_out, n_acc = len(ins), len(outs), len(accs)

    def body(*refs):
        in_refs = refs[:n_in]
        out_refs = refs[n_in:n_in + n_out]
        acc_refs = refs[n_in + n_out:n_in + n_out + n_acc]
        c_refs = refs[n_in + n_out + n_acc:]

        if acc_refs or c_refs:
            @pl.when(pl.program_id(1) == 0)
            def _():
                for r in tuple(acc_refs) + tuple(c_refs):
                    r[...] = jnp.zeros_like(r)

        o, a, c = fn([r[...] for r in in_refs], [r[...] for r in c_refs])
        for r, v in zip(out_refs, o, strict=True):
            r[...] = v.astype(r.dtype)
        for r, v in zip(acc_refs, a, strict=True):
            r[...] += v
        for r, v in zip(c_refs, c, strict=True):
            r[...] = v

    return pl.pallas_call(
        body, name=name, grid=grid,
        in_specs=[pl.BlockSpec(blk, im) for _, blk, im in ins],
        out_specs=[pl.BlockSpec(blk, im) for _, _, blk, im in outs] + [pl.BlockSpec(blk, im) for _, blk, im in accs],
        out_shape=[jax.ShapeDtypeStruct(s, d) for s, d, _, _ in outs] + [jax.ShapeDtypeStruct(s, F32) for s, _, _ in accs],
        scratch_shapes=[pltpu.VMEM(s, F32) for s in carries], compiler_params=_cparams(2),
    )(*[a for a, _, _ in ins])


def exchange(arrays, gather, name, after=None):
    n = len(arrays)
    extra = [] if after is None else [after]

    def body(*refs):
        x_refs, o_refs = refs[:n], refs[n + len(extra):2 * n + len(extra)]
        send_sems, recv_sems, local_sems = refs[2 * n + len(extra):]
        pos = [lax.axis_index(ax) for ax in MESH_AXES]
        me = 4 * pos[0] + 2 * pos[1] + pos[2]
        copies = []
        for i in range(n):
            own = pltpu.make_async_copy(x_refs[i] if gather else x_refs[i].at[me], o_refs[i].at[me], local_sems.at[i])
            own.start()
            copies.append(own)
        for k in range(1, N_DEV):
            bits = ((k >> 2) & 1, (k >> 1) & 1, k & 1)
            peer = tuple(1 - p if b else p for p, b in zip(pos, bits))
            peer_id = 4 * peer[0] + 2 * peer[1] + peer[2]
            for i in range(n):
                cp = pltpu.make_async_remote_copy(
                    src_ref=x_refs[i] if gather else x_refs[i].at[peer_id], dst_ref=o_refs[i].at[me],
                    send_sem=send_sems.at[i * (N_DEV - 1) + k - 1], recv_sem=recv_sems.at[i * (N_DEV - 1) + k - 1],
                    device_id=peer, device_id_type=pl.DeviceIdType.MESH)
                cp.start()
                copies.append(cp)
        for cp in copies:
            cp.wait()

    hbm = pl.BlockSpec(memory_space=pltpu.HBM)
    return pl.pallas_call(
        body, name=name, in_specs=[hbm] * n + [pl.BlockSpec(memory_space=pl.ANY)] * len(extra), out_specs=[hbm] * n,
        out_shape=[jax.ShapeDtypeStruct(((N_DEV,) + a.shape) if gather else a.shape, a.dtype) for a in arrays],
        scratch_shapes=[pltpu.SemaphoreType.DMA((n * (N_DEV - 1),)), pltpu.SemaphoreType.DMA((n * (N_DEV - 1),)),
                        pltpu.SemaphoreType.DMA((n,))],
    )(*arrays, *extra)


_HBM = pl.BlockSpec(memory_space=pltpu.HBM)
_SEM = pl.BlockSpec(memory_space=pltpu.SEMAPHORE)
N_PEERS = N_DEV - 1


def _mesh_pos():
    pos = [lax.axis_index(ax) for ax in MESH_AXES]
    return pos, 4 * pos[0] + 2 * pos[1] + pos[2]


def _peers(pos):
    out = []
    for k in range(1, N_DEV):
        bits = ((k >> 2) & 1, (k >> 1) & 1, k & 1)
        peer = tuple(1 - p if b else p for p, b in zip(pos, bits))
        out.append((peer, 4 * peer[0] + 2 * peer[1] + peer[2]))
    return out


def _part(x_ref, lead, gather, slot):
    ref = x_ref if lead is None else x_ref.at[lead]
    return ref if gather else ref.at[slot]


OWN_BLOCK_BYTES = 2 * 1024 * 1024


def _landing_zone(a, lead, me, name):
    r, c = a.shape[-2:]
    tr = r
    while tr * _round_up(c, LANES) * a.dtype.itemsize > OWN_BLOCK_BYTES and tr % 32 == 0:
        tr //= 2

    def body(me_ref, x_ref, o_ref):
        o_ref[...] = x_ref[...]

    x_spec = (pl.BlockSpec((tr, c), lambda i, me_ref: (i, 0)) if lead is None
              else pl.BlockSpec((None, tr, c), lambda i, me_ref: (lead, i, 0)))
    grid_spec = pltpu.PrefetchScalarGridSpec(
        num_scalar_prefetch=1, grid=(r // tr,), in_specs=[x_spec],
        out_specs=pl.BlockSpec((None, tr, c), lambda i, me_ref: (me_ref[0], i, 0)))
    return pl.pallas_call(body, name=name, grid_spec=grid_spec, out_shape=jax.ShapeDtypeStruct((N_DEV, r, c), a.dtype),
                          compiler_params=_cparams(1))(jnp.reshape(me, (1,)).astype(jnp.int32), a)


def _split_copies(items, gather, x_refs, land_refs, send_sems, recv_sems):
    pos, me = _mesh_pos()
    copies = []
    for i, (_, lead) in enumerate(items):
        for k, (peer, peer_id) in enumerate(_peers(pos)):
            copies.append(pltpu.make_async_remote_copy(
                src_ref=_part(x_refs[i], lead, gather, peer_id), dst_ref=land_refs[i].at[me],
                send_sem=send_sems.at[i * N_PEERS + k], recv_sem=recv_sems.at[i * N_PEERS + k],
                device_id=peer, device_id_type=pl.DeviceIdType.MESH))
    return copies


def exchange_start(items, gather, lands, name):
    n = len(items)
    lands = list(lands)
    xs = [a for a, _ in items]

    def body(*refs):
        x_refs, land_refs = refs[:n], refs[n:2 * n]
        send_sems, recv_sems, token = refs[2 * n], refs[2 * n + 1], refs[-1]
        for cp in _split_copies(items, gather, x_refs, land_refs, send_sems, recv_sems):
            cp.start()
        token[...] = jnp.zeros_like(token)

    outs = pl.pallas_call(
        body, name=name,
        out_shape=(pltpu.SemaphoreType.DMA((n * N_PEERS,)), pltpu.SemaphoreType.DMA((n * N_PEERS,)),
                   *[pltpu.HBM(v.shape, v.dtype) for v in xs + lands], jax.ShapeDtypeStruct((SUBLANES, LANES), F32)),
        in_specs=[_HBM] * (2 * n),
        out_specs=(_SEM, _SEM, *[_HBM] * (2 * n), pl.BlockSpec(memory_space=pltpu.VMEM)),
        input_output_aliases={i: 2 + i for i in range(2 * n)},
        compiler_params=pltpu.CompilerParams(has_side_effects=pltpu.SideEffectType.DATAFLOW_SIDE_EFFECTING),
    )(*[pltpu.with_memory_space_constraint(v, pltpu.HBM) for v in xs + lands])
    handle = (items, gather, outs[0], outs[1], outs[2:2 + n], outs[2 + n:2 + 2 * n])
    return handle, outs[-1]


def exchange_wait(handle, after, name):
    items, gather, send_sems, recv_sems, x_thru, land_thru = handle
    n = len(items)

    def body(*refs):
        x_refs, land_refs = refs[:n], refs[n:2 * n]
        for cp in _split_copies(items, gather, x_refs, land_refs, refs[2 * n], refs[2 * n + 1]):
            cp.wait_send()
            cp.wait_recv()

    outs = pl.pallas_call(
        body, name=name, out_shape=tuple(pltpu.HBM(v.shape, v.dtype) for v in tuple(x_thru) + tuple(land_thru)),
        in_specs=[_HBM] * (2 * n) + [_SEM, _SEM, pl.BlockSpec(memory_space=pl.ANY)], out_specs=tuple([_HBM] * (2 * n)),
        input_output_aliases={i: i for i in range(2 * n)},
        compiler_params=pltpu.CompilerParams(has_side_effects=pltpu.SideEffectType.DATAFLOW_SIDE_EFFECTING),
    )(*x_thru, *land_thru, send_sems, recv_sems, after)
    return list(outs[:n]), list(outs[n:])


def _adam_update(w, g, m, v):
    nm = ADAM_B1 * m + (1.0 - ADAM_B1) * g
    nv = ADAM_B2 * v + (1.0 - ADAM_B2) * (g * g)
    m_hat = nm / (1.0 - ADAM_B1 ** ADAM_STEP)
    v_hat = nv / (1.0 - ADAM_B2 ** ADAM_STEP)
    return -ADAM_LR * (m_hat / (jnp.sqrt(v_hat) + ADAM_EPS) + ADAM_WD * w), nm, nv


def _sum_slots(s_ref):
    acc = s_ref[0].astype(F32)
    for j in range(1, N_DEV):
        acc = acc + s_ref[j].astype(F32)
    return acc


ADAM_BLOCK_BYTES = 10 * 1024 * 1024


def adamw_sharded(recvs, sends, me, w, m, v, name):
    nl, r, c = w.shape
    assert len(recvs) == nl and len(sends) == nl
    per_row = _round_up(c, LANES) * (nl * (N_DEV + 1) * recvs[0].dtype.itemsize + 7 * 4)
    tr = r
    while tr * per_row > ADAM_BLOCK_BYTES and tr % 16 == 0:
        tr //= 2

    def body(me_ref, *refs):
        s_refs, o_refs = refs[:nl], refs[nl:2 * nl]
        w_ref, m_ref, v_ref, g_ref, d_ref, nm_ref, nv_ref = refs[2 * nl:]
        mine = me_ref[0]

        def total(l):
            acc = jnp.where(mine == 0, o_refs[l][...], s_refs[l][0]).astype(F32)
            for j in range(1, N_DEV):
                acc = acc + jnp.where(mine == j, o_refs[l][...], s_refs[l][j]).astype(F32)
            return acc

        g = total(0)
        for l in range(1, nl):
            g = jnp.where(pl.program_id(0) == l, total(l), g)
        g_ref[...] = g
        d_ref[...], nm_ref[...], nv_ref[...] = _adam_update(w_ref[...], g, m_ref[...], v_ref[...])

    spec = pl.BlockSpec((None, tr, c), lambda l, i, me_ref: (l, i, 0))
    shp = jax.ShapeDtypeStruct(w.shape, F32)
    grid_spec = pltpu.PrefetchScalarGridSpec(
        num_scalar_prefetch=1, grid=(nl, r // tr),
        in_specs=([pl.BlockSpec((N_DEV, tr, c), lambda l, i, me_ref: (0, i, 0))] * nl
                  + [pl.BlockSpec((None, tr, c), lambda l, i, me_ref: (me_ref[0], i, 0))] * nl + [spec, spec, spec]),
        out_specs=[spec] * 4)
    return pl.pallas_call(body, name=name, grid_spec=grid_spec, out_shape=[shp] * 4, compiler_params=_cparams(2))(
        jnp.reshape(me, (1,)).astype(jnp.int32), *recvs, *sends, w, m, v)


def reduce_slots(slots, name):
    _, r, lanes = slots.shape
    tr = _tile(r, 2048)

    def body(s_ref, o_ref):
        o_ref[...] = _sum_slots(s_ref)

    return pl.pallas_call(
        body, name=name, grid=(r // tr,),
        in_specs=[pl.BlockSpec((N_DEV, tr, lanes), lambda i: (0, i, 0))],
        out_specs=pl.BlockSpec((tr, lanes), lambda i: (i, 0)),
        out_shape=jax.ShapeDtypeStruct((r, lanes), F32), compiler_params=_cparams(1),
    )(slots)


def adamw(w, g, m, v, name):
    r, lanes = w.shape
    tr = _tile(r, 2048)

    def body(w_ref, g_ref, m_ref, v_ref, d_ref, nm_ref, nv_ref):
        d_ref[...], nm_ref[...], nv_ref[...] = _adam_update(w_ref[...], g_ref[...], m_ref[...], v_ref[...])

    spec = pl.BlockSpec((tr, lanes), lambda i: (i, 0))
    shp = jax.ShapeDtypeStruct((r, lanes), F32)
    return pl.pallas_call(body, name=name, grid=(r // tr,), in_specs=[spec] * 4, out_specs=[spec] * 3,
                          out_shape=[shp] * 3, compiler_params=_cparams(1))(w, g, m, v)


def cols_from_slots(slots, place, width, name):
    _, rows, c = slots.shape
    tr = _tile(rows, 256)

    def body(s_ref, o_ref):
        o_ref[...] = place(jnp.concatenate([s_ref[j] for j in range(N_DEV)], axis=1))

    return pl.pallas_call(
        body, name=name, grid=(rows // tr,),
        in_specs=[pl.BlockSpec((N_DEV, tr, c), lambda i: (0, i, 0))],
        out_specs=pl.BlockSpec((tr, width), lambda i: (i, 0)),
        out_shape=jax.ShapeDtypeStruct((rows, width), slots.dtype), compiler_params=_cparams(1))(slots)


def slots_from_cols(full, pick, c, name):
    rows, wide = full.shape
    tr = _tile(rows, 256)

    def body(x_ref, o_ref):
        v = pick(x_ref[...])
        for j in range(N_DEV):
            o_ref[j] = v[:, j * c:(j + 1) * c]

    return pl.pallas_call(
        body, name=name, grid=(rows // tr,),
        in_specs=[pl.BlockSpec((tr, wide), lambda i: (i, 0))],
        out_specs=pl.BlockSpec((N_DEV, tr, c), lambda i: (0, i, 0)),
        out_shape=jax.ShapeDtypeStruct((N_DEV, rows, c), full.dtype), compiler_params=_cparams(1))(full)


def _token_tile(cfg):
    return min(cfg.S, 256)


def norm_fwd(cfg, x, w, name):
    ts = _token_tile(cfg)
    d = x.shape[1]

    def fn(ins, _):
        xv, wv = ins
        return [rms(xv, wv)], [], []

    return seq_call(name, fn, (1, cfg.S // ts),
                    [(x, (ts, d), lambda g, t: (t, 0)), (w, (1, d), lambda g, t: (0, 0))],
                    [((cfg.S, d), BF16, (ts, d), lambda g, t: (t, 0))])[0]


def norm_bwd(cfg, x, w, dh, dres, name):
    ts = _token_tile(cfg)
    d = x.shape[1]

    def fn(ins, _):
        xv, wv, dhv, drv = ins
        _, vjp = jax.vjp(rms, xv, wv)
        dx, dw = vjp(dhv.astype(F32))
        return [dx + drv], [dw], []

    row = lambda g, t: (t, 0)
    out = seq_call(name, fn, (1, cfg.S // ts),
                   [(x, (ts, d), row), (w, (1, d), lambda g, t: (0, 0)), (dh, (ts, d), row), (dres, (ts, d), row)],
                   [((cfg.S, d), F32, (ts, d), row)],
                   accs=[((1, d), (1, d), lambda g, t: (0, 0))])
    return out[0], out[1]


def head_fwd_bwd(cfg, x, w, target, name):
    ts = _token_tile(cfg)
    d = x.shape[1]

    def fn(ins, _):
        xv, wv, tv = ins
        y, vjp = jax.vjp(rms, xv, wv)
        err = y - tv
        loss = 0.5 * jnp.sum(err * err) / d
        dx, dw = vjp(err / d)
        return [dx], [jnp.full((SUBLANES, LANES), loss, F32), dw], []

    row = lambda g, t: (t, 0)
    fixed = lambda g, t: (0, 0)
    dx, loss, dw = seq_call(name, fn, (1, cfg.S // ts),
                            [(x, (ts, d), row), (w, (1, d), fixed), (target, (ts, d), row)],
                            [((cfg.S, d), F32, (ts, d), row)],
                            accs=[((SUBLANES, LANES), (SUBLANES, LANES), fixed), ((1, d), (1, d), fixed)])
    return loss, dx, dw


FFN_ROWS = 1024


def ffn_gate_up(h, w_gate, w_up, name):
    s, d = h.shape
    c = w_gate.shape[1]
    tm = _tile(s, FFN_ROWS)

    def body(h_ref, wg_ref, wu_ref, g_ref, u_ref, a_ref):
        hv = h_ref[...]
        g = _dg(hv, wg_ref[...], 1, 1)
        u = _dg(hv, wu_ref[...], 1, 1)
        g_ref[...] = g
        u_ref[...] = u
        a_ref[...] = (silu(g) * u).astype(a_ref.dtype)

    w_spec = pl.BlockSpec((None, c, d), lambda i, j: (j, 0, 0))
    o_spec = pl.BlockSpec((None, tm, c), lambda i, j: (j, i, 0))
    shp = (N_DEV, s, c)
    return pl.pallas_call(
        body, name=name, grid=(s // tm, N_DEV), in_specs=[pl.BlockSpec((tm, d), lambda i, j: (i, 0)), w_spec, w_spec],
        out_specs=[o_spec] * 3,
        out_shape=[jax.ShapeDtypeStruct(shp, F32), jax.ShapeDtypeStruct(shp, F32), jax.ShapeDtypeStruct(shp, BF16)],
        compiler_params=_cparams(2))(h, w_gate, w_up)


def ffn_dgate_dup(dx, w_down, gate, up, after, name):
    s, d = dx.shape
    c = w_down.shape[1]
    tm = _tile(s, FFN_ROWS)
    extra = [] if after is None else [after]

    def body(dx_ref, wd_ref, g_ref, u_ref, *rest):
        dg_ref, du_ref = rest[len(extra):]
        dact = _dg(dx_ref[...], wd_ref[...], 1, 1)
        _, vjp = jax.vjp(lambda a, b: silu(a) * b, g_ref[...], u_ref[...])
        dg, du = vjp(dact)
        dg_ref[...] = dg.astype(dg_ref.dtype)
        du_ref[...] = du.astype(du_ref.dtype)

    blk = pl.BlockSpec((None, tm, c), lambda i, j: (j, i, 0))
    shp = jax.ShapeDtypeStruct((N_DEV, s, c), BF16)
    return pl.pallas_call(
        body, name=name, grid=(s // tm, N_DEV),
        in_specs=[pl.BlockSpec((tm, d), lambda i, j: (i, 0)), pl.BlockSpec((None, c, d), lambda i, j: (j, 0, 0)), blk, blk]
        + [pl.BlockSpec(memory_space=pl.ANY)] * len(extra),
        out_specs=[blk, blk], out_shape=[shp, shp], compiler_params=_cparams(2))(dx, w_down, gate, up, *extra)


CONV_COLS = 256
HALO = SUBLANES


def _shift_down(x, halo, j):
    if j == 0:
        return x
    r8 = _rows_iota(halo.shape)
    top = jnp.where(r8 >= j, pltpu.roll(x[:HALO], j, 0), pltpu.roll(halo, j, 0))
    return jnp.concatenate([top, pltpu.roll(x, j, 0)[HALO:]], axis=0)


def _shift_up(x, halo, j):
    if j == 0:
        return x
    n = x.shape[0]
    r8 = _rows_iota(halo.shape)
    bot = jnp.where(r8 < HALO - j, pltpu.roll(x[n - HALO:], HALO - j, 0), pltpu.roll(halo, HALO - j, 0))
    return jnp.concatenate([pltpu.roll(x, n - j, 0)[:n - HALO], bot], axis=0)


def _conv_tile(cfg):
    return min(cfg.S, 512)


def conv_fwd(cfg, src, col0, width, w, b, name):
    tt, cb = _conv_tile(cfg), CONV_COLS
    c0, hb = col0 // cb, tt // HALO
    nt = cfg.S // tt

    def body(x_ref, h_ref, w_ref, b_ref, o_ref):
        t = pl.program_id(1)
        x = x_ref[...]
        halo = jnp.where(t > 0, h_ref[...], 0.0)
        wv = w_ref[...]
        acc = b_ref[...] + wv[CONV_WIDTH - 1:CONV_WIDTH] * x
        for j in range(1, CONV_WIDTH):
            acc = acc + wv[CONV_WIDTH - 1 - j:CONV_WIDTH - j] * _shift_down(x, halo, j)
        o_ref[...] = acc

    return pl.pallas_call(
        body, name=name, grid=(width // cb, nt),
        in_specs=[pl.BlockSpec((tt, cb), lambda c, t: (t, c0 + c)),
                  pl.BlockSpec((HALO, cb), lambda c, t: (jnp.maximum(t * hb - 1, 0), c0 + c)),
                  pl.BlockSpec((CONV_WIDTH, cb), lambda c, t: (0, c)),
                  pl.BlockSpec((1, cb), lambda c, t: (0, c))],
        out_specs=pl.BlockSpec((tt, cb), lambda c, t: (t, c)),
        out_shape=jax.ShapeDtypeStruct((cfg.S, width), F32), compiler_params=_cparams(2),
    )(src, src, w, b)


def conv_bwd(cfg, src, col0, width, w, dy, name):
    tt, cb = _conv_tile(cfg), CONV_COLS
    c0, hb = col0 // cb, tt // HALO
    nt = cfg.S // tt

    def body(x_ref, h_ref, w_ref, dy_ref, dh_ref, dx_ref, dw_ref, db_ref):
        t = pl.program_id(1)

        @pl.when(t == 0)
        def _():
            dw_ref[...] = jnp.zeros_like(dw_ref)
            db_ref[...] = jnp.zeros_like(db_ref)

        x = x_ref[...]
        halo = jnp.where(t > 0, h_ref[...], 0.0)
        dy = dy_ref[...]
        dhalo = jnp.where(t < nt - 1, dh_ref[...], 0.0)
        wv = w_ref[...]
        dx = wv[CONV_WIDTH - 1:CONV_WIDTH] * dy
        rows = [jnp.sum(dy * x, axis=0, keepdims=True)]
        for j in range(1, CONV_WIDTH):
            dx = dx + wv[CONV_WIDTH - 1 - j:CONV_WIDTH - j] * _shift_up(dy, dhalo, j)
            rows.insert(0, jnp.sum(dy * _shift_down(x, halo, j), axis=0, keepdims=True))
        dx_ref[...] = dx.astype(dx_ref.dtype)
        dw_ref[...] += jnp.concatenate(rows, axis=0)
        db_ref[...] += jnp.sum(dy, axis=0, keepdims=True)

    return pl.pallas_call(
        body, name=name, grid=(width // cb, nt),
        in_specs=[pl.BlockSpec((tt, cb), lambda c, t: (t, c0 + c)),
                  pl.BlockSpec((HALO, cb), lambda c, t: (jnp.maximum(t * hb - 1, 0), c0 + c)),
                  pl.BlockSpec((CONV_WIDTH, cb), lambda c, t: (0, c)),
                  pl.BlockSpec((tt, cb), lambda c, t: (t, c)),
                  pl.BlockSpec((HALO, cb), lambda c, t: (jnp.minimum((t + 1) * hb, nt * hb - 1), c))],
        out_specs=[pl.BlockSpec((tt, cb), lambda c, t: (t, c)),
                   pl.BlockSpec((CONV_WIDTH, cb), lambda c, t: (0, c)),
                   pl.BlockSpec((1, cb), lambda c, t: (0, c))],
        out_shape=[jax.ShapeDtypeStruct((cfg.S, width), BF16), jax.ShapeDtypeStruct((CONV_WIDTH, width), F32),
                   jax.ShapeDtypeStruct((1, width), F32)],
        compiler_params=_cparams(2),
    )(src, src, w, dy, dy)


def _gla_core(gh, q, k, v, g, glr, wg, bg, wn, st):
    n = glr.shape[0]
    causal = _lower_tri(n)
    outs, new = [], []
    for h in range(gh):
        log_a = log_sigmoid(bdot(glr, wg[h], 1, 0) + bg[h]) * (1.0 / GATE_NORM)
        bcum = cumsum_rows(log_a)
        b_last, b_mid = _row(bcum, n - 1), _row(bcum, n // 2)
        qs = q[h] * (HEAD_K ** -0.5)
        scores = jnp.where(causal, bdot(qs * jnp.exp(bcum - b_mid), k[h] * jnp.exp(b_mid - bcum), 1, 1), 0.0)
        o = bdot(scores, v[h], 1, 0) + bdot(qs * jnp.exp(bcum), st[h], 1, 1)
        new.append(st[h] * jnp.exp(b_last) + bdot(v[h], k[h] * jnp.exp(b_last - bcum), 0, 0))
        outs.append(rms(o, wn) * silu(g[h]))
    return jnp.concatenate(outs, axis=1), new


def _gla_ins(cfg, p0, wg, bg, wn, tmap):
    gh = cfg.GH
    ins = []
    for h in range(gh):
        ins.append((p0, (CHUNK, HEAD_K), lambda g, t, h=h: (tmap(t), h)))
    for h in range(gh):
        ins.append((p0, (CHUNK, HEAD_K), lambda g, t, h=h: (tmap(t), gh + h)))
    for h in range(gh):
        ins.append((p0, (CHUNK, HEAD_V), lambda g, t, h=h: (tmap(t), gh + h)))
    for h in range(gh):
        ins.append((p0, (CHUNK, HEAD_V), lambda g, t, h=h: (tmap(t), 2 * gh + h)))
    ins.append((p0, (CHUNK, LANES), lambda g, t: (tmap(t), 10 * gh)))
    for h in range(gh):
        ins.append((wg, (LANES, HEAD_K), lambda g, t, h=h: (0, h)))
    for h in range(gh):
        ins.append((bg, (1, HEAD_K), lambda g, t, h=h: (0, h)))
    ins.append((wn, (1, HEAD_V), lambda g, t: (0, 0)))
    return ins


def _gla_unpack(gh, vals):
    q, k, v, g = (vals[i * gh:(i + 1) * gh] for i in range(4))
    glr = vals[4 * gh]
    wg = vals[4 * gh + 1:5 * gh + 1]
    bg = vals[5 * gh + 1:6 * gh + 1]
    wn = vals[6 * gh + 1]
    return q, k, v, g, glr, wg, bg, wn, vals[6 * gh + 2:]


def gla_fwd(cfg, p0, wg, bg, wn, name):
    gh, nc = cfg.GH, cfg.S // CHUNK

    def fn(ins, st):
        q, k, v, g, glr, wgv, bgv, wnv, _ = _gla_unpack(gh, ins)
        out, new = _gla_core(gh, q, k, v, g, glr, wgv, bgv, wnv, st)
        return [out, jnp.stack(st)], [], new

    return seq_call(name, fn, (1, nc), _gla_ins(cfg, p0, wg, bg, wn, lambda t: t),
                    [((cfg.S, cfg.DV), BF16, (CHUNK, cfg.DV), lambda g, t: (t, 0)),
                     ((nc, gh, HEAD_V, HEAD_K), F32, (None, gh, HEAD_V, HEAD_K), lambda g, t: (t, 0, 0, 0))],
                    carries=[(HEAD_V, HEAD_K)] * gh)


def gla_bwd(cfg, p0, wg, bg, wn, states, dout, name):
    gh, nc = cfg.GH, cfg.S // CHUNK
    rev = lambda t: nc - 1 - t

    def fn(ins, dst):
        q, k, v, g, glr, wgv, bgv, wnv, rest = _gla_unpack(gh, ins)
        st_all, do = rest
        st = [st_all[h] for h in range(gh)]
        _, vjp = jax.vjp(functools.partial(_gla_core, gh), q, k, v, g, glr, wgv, bgv, wnv, st)
        dq, dk, dv, dg, dglr, dwg, dbg, dwn, dstate = vjp((do.astype(F32), list(dst)))
        return ([jnp.concatenate(list(dq) + list(dk) + list(dv) + list(dg), axis=1), dglr],
                [jnp.concatenate(dwg, axis=1), jnp.concatenate(dbg, axis=1), dwn], dstate)

    ins = _gla_ins(cfg, p0, wg, bg, wn, rev)
    ins.append((states, (None, gh, HEAD_V, HEAD_K), lambda g, t: (rev(t), 0, 0, 0)))
    ins.append((dout, (CHUNK, cfg.DV), lambda g, t: (rev(t), 0)))
    wide = 2 * cfg.DK + 2 * cfg.DV
    fixed = lambda g, t: (0, 0)
    return seq_call(name, fn, (1, nc), ins,
                    [((cfg.S, wide), BF16, (CHUNK, wide), lambda g, t: (rev(t), 0)),
                     ((cfg.S, LANES), BF16, (CHUNK, LANES), lambda g, t: (rev(t), 0))],
                    accs=[((LANES, cfg.DK), (LANES, cfg.DK), fixed), ((1, cfg.DK), (1, cfg.DK), fixed),
                          ((1, HEAD_V), (1, HEAD_V), fixed)],
                    carries=[(HEAD_V, HEAD_K)] * gh)


def _lru_core(xc, gate, wa, wi, ba, bi, lam, h_in):
    r = jax.nn.sigmoid(bdot(xc, wa, 1, 0) + ba)
    i = jax.nn.sigmoid(bdot(xc, wi, 1, 0) + bi)
    log_a = LRU_C * r * log_sigmoid(lam)
    a = jnp.exp(log_a)
    u = jnp.sqrt(-_expm1(2.0 * log_a)) * (i * xc)
    first = _rows_iota(a.shape) == 0
    h = lin_scan(a, u + jnp.where(first, a * h_in, 0.0))
    return h * gelu_tanh(gate), _row(h, a.shape[0] - 1)


def _lru_tile(cfg):
    return min(cfg.S, 512)


def _lru_ins(cfg, xc, p0, wa, wi, ba, bi, lam, tmap):
    tt, gh = _lru_tile(cfg), cfg.GH
    vec = lambda g, t: (0, g)
    return [(xc, (tt, LRU_BLOCK), lambda g, t: (tmap(t), g)),
            (p0, (tt, LRU_BLOCK), lambda g, t: (tmap(t), 8 * gh + g)),
            (wa, (None, LRU_BLOCK, LRU_BLOCK), lambda g, t: (g, 0, 0)),
            (wi, (None, LRU_BLOCK, LRU_BLOCK), lambda g, t: (g, 0, 0)),
            (ba, (1, LRU_BLOCK), vec), (bi, (1, LRU_BLOCK), vec), (lam, (1, LRU_BLOCK), vec)]


def lru_fwd(cfg, xc, p0, wa, wi, ba, bi, lam, name):
    tt, nb = _lru_tile(cfg), cfg.NB
    nt = cfg.S // tt

    def fn(ins, c):
        out, h_last = _lru_core(*ins, c[0])
        return [out, c[0]], [], [h_last]

    return seq_call(name, fn, (nb, nt), _lru_ins(cfg, xc, p0, wa, wi, ba, bi, lam, lambda t: t),
                    [((cfg.S, cfg.W), BF16, (tt, LRU_BLOCK), lambda g, t: (t, g)),
                     ((nb, nt, 1, LRU_BLOCK), F32, (None, None, 1, LRU_BLOCK), lambda g, t: (g, t, 0, 0))],
                    carries=[(1, LRU_BLOCK)])


def lru_bwd(cfg, xc, p0, wa, wi, ba, bi, lam, states, dout, name):
    tt, nb = _lru_tile(cfg), cfg.NB
    nt = cfg.S // tt
    rev = lambda t: nt - 1 - t

    def fn(ins, c):
        *fwd_ins, h_in, do = ins
        _, vjp = jax.vjp(_lru_core, *fwd_ins, h_in)
        dxc, dgate, dwa, dwi, dba, dbi, dlam, dh = vjp((do.astype(F32), c[0]))
        return [dxc, dgate], [dwa, dwi, dba, dbi, dlam], [dh]

    ins = _lru_ins(cfg, xc, p0, wa, wi, ba, bi, lam, rev)
    ins.append((states, (None, None, 1, LRU_BLOCK), lambda g, t: (g, rev(t), 0, 0)))
    ins.append((dout, (tt, LRU_BLOCK), lambda g, t: (rev(t), g)))
    mat = ((nb, LRU_BLOCK, LRU_BLOCK), (None, LRU_BLOCK, LRU_BLOCK), lambda g, t: (g, 0, 0))
    vec = ((1, cfg.W), (1, LRU_BLOCK), lambda g, t: (0, g))
    return seq_call(name, fn, (nb, nt), ins,
                    [((cfg.S, cfg.W), F32, (tt, LRU_BLOCK), lambda g, t: (rev(t), g)),
                     ((cfg.S, cfg.W), BF16, (tt, LRU_BLOCK), lambda g, t: (rev(t), g))],
                    accs=[mat, mat, vec, vec, vec], carries=[(1, LRU_BLOCK)])


def _ssd_core(xc, bc, cc, z, dt_raw, dt_bias, a_log, d_skip, gn, st):
    n = xc.shape[0]
    x, bm, cm = silu(xc), silu(bc), silu(cc)
    dt = softplus(dt_raw + dt_bias)
    acs = cumsum_rows(dt * (-jnp.exp(a_log)))
    acs_t = acs.T
    acs_e, dt_e = _expand_heads(acs), _expand_heads(dt)
    last_e = _expand_heads(_row(acs, n - 1))
    causal = _lower_tri(n)
    cb = bdot(cm, bm, 1, 1)
    xdt = x * dt_e
    y_diag = []
    for h, xh in enumerate(_split_heads(xdt)):
        seg = acs[:, h:h + 1] - acs_t[h:h + 1, :]
        decay = jnp.where(causal, jnp.exp(jnp.minimum(seg, 0.0)), 0.0)
        y_diag.append(bdot(cb * decay, xh, 1, 0))
    y = jnp.concatenate(y_diag, axis=1) + bdot(cm, st, 1, 0) * jnp.exp(acs_e)
    new = st * jnp.exp(last_e) + bdot(bm, xdt * jnp.exp(last_e - acs_e), 0, 0)
    y = (y + _expand_heads(d_skip) * x) * silu(z)
    return rms(y, gn), new


SSD_TILED = 5


def _ssd_tile(cfg):
    return min(cfg.S, 4 * CHUNK)


def _chunk_rows(v, s):
    return v[s * CHUNK:(s + 1) * CHUNK]


def _ssd_ins(cfg, xc, p1, dt_raw, dt_bias, a_log, d_skip, gn, tmap):
    ng, tt = cfg.NG, _ssd_tile(cfg)
    vec = lambda g, t: (g, 0, 0)
    return [(xc, (tt, SSD_GW), lambda g, t: (tmap(t), g)),
            (xc, (tt, SSD_N), lambda g, t: (tmap(t), 4 * ng + g)),
            (xc, (tt, SSD_N), lambda g, t: (tmap(t), 5 * ng + g)),
            (p1, (tt, SSD_GW), lambda g, t: (tmap(t), g)),
            (dt_raw, (None, tt, LANES), lambda g, t: (g, tmap(t), 0)),
            (dt_bias, (None, 1, LANES), vec), (a_log, (None, 1, LANES), vec), (d_skip, (None, 1, LANES), vec),
            (gn, (1, SSD_GW), lambda g, t: (0, g))]


def ssd_fwd(cfg, xc, p1, dt_raw, dt_bias, a_log, d_skip, gn, name):
    ng, nc, tt = cfg.NG, cfg.S // CHUNK, _ssd_tile(cfg)
    nsub = tt // CHUNK

    def fn(ins, c):
        tiled, params = ins[:SSD_TILED], ins[SSD_TILED:]
        st, outs, entered = c[0], [], []
        for s in range(nsub):
            entered.append(st)
            out, st = _ssd_core(*[_chunk_rows(v, s) for v in tiled], *params, st)
            outs.append(out)
        return [jnp.concatenate(outs, axis=0), jnp.stack(entered)], [], [st]

    return seq_call(name, fn, (ng, cfg.S // tt), _ssd_ins(cfg, xc, p1, dt_raw, dt_bias, a_log, d_skip, gn, lambda t: t),
                    [((cfg.S, cfg.DI), BF16, (tt, SSD_GW), lambda g, t: (t, g)),
                     ((ng, nc, SSD_N, SSD_GW), F32, (None, nsub, SSD_N, SSD_GW), lambda g, t: (g, t, 0, 0))],
                    carries=[(SSD_N, SSD_GW)])


def ssd_bwd(cfg, xc, p1, dt_raw, dt_bias, a_log, d_skip, gn, states, dout, name):
    ng, tt = cfg.NG, _ssd_tile(cfg)
    nsub, nt = tt // CHUNK, cfg.S // tt
    rev = lambda t: nt - 1 - t

    def fn(ins, c):
        tiled, params = ins[:SSD_TILED], ins[SSD_TILED:SSD_TILED + 4]
        st_all, do = ins[SSD_TILED + 4:]
        dst, pieces, acc = c[0], [None] * nsub, None
        for s in reversed(range(nsub)):
            _, vjp = jax.vjp(_ssd_core, *[_chunk_rows(v, s) for v in tiled], *params, st_all[s])
            grads = vjp((_chunk_rows(do, s).astype(F32), dst))
            pieces[s], dparams, dst = grads[:SSD_TILED], grads[SSD_TILED:SSD_TILED + 4], grads[SSD_TILED + 4]
            acc = dparams if acc is None else [x + y for x, y in zip(acc, dparams)]
        return [jnp.concatenate([p[i] for p in pieces], axis=0) for i in range(SSD_TILED)], list(acc), [dst]

    ins = _ssd_ins(cfg, xc, p1, dt_raw, dt_bias, a_log, d_skip, gn, rev)
    ins.append((states, (None, nsub, SSD_N, SSD_GW), lambda g, t: (g, rev(t), 0, 0)))
    ins.append((dout, (tt, SSD_GW), lambda g, t: (rev(t), g)))
    col = lambda g, t: (rev(t), g)
    vec = ((ng, 1, LANES), (None, 1, LANES), lambda g, t: (g, 0, 0))
    return seq_call(name, fn, (ng, nt), ins,
                    [((cfg.S, cfg.DI), F32, (tt, SSD_GW), col),
                     ((cfg.S, ng * SSD_N), F32, (tt, SSD_N), col),
                     ((cfg.S, ng * SSD_N), F32, (tt, SSD_N), col),
                     ((cfg.S, cfg.DI), BF16, (tt, SSD_GW), col),
                     ((ng, cfg.S, LANES), F32, (None, tt, LANES), lambda g, t: (g, rev(t), 0))],
                    accs=[vec, vec, vec, ((1, cfg.DI), (1, SSD_GW), lambda g, t: (0, g))],
                    carries=[(SSD_N, SSD_GW)])


PACK_ALIGN = SUBLANES * LANES
PACK_ROWS = 256


def pack(arrays):
    pieces = []
    for a in arrays:
        flat = a.reshape(-1).astype(F32)
        pad = _round_up(flat.shape[0], PACK_ALIGN) - flat.shape[0]
        pieces.append(jnp.pad(flat, (0, pad)) if pad else flat)
    flat = jnp.concatenate(pieces)
    pad = _round_up(flat.shape[0], PACK_ROWS * LANES) - flat.shape[0]
    return jnp.pad(flat, (0, pad)).reshape(-1, LANES)


def unpack(buf, shapes):
    lead = buf.shape[:-2]
    flat = buf.reshape(lead + (-1,))
    out, off = [], 0
    for s in shapes:
        n = math.prod(s)
        out.append(flat[..., off:off + n].reshape(lead + tuple(s)))
        off += _round_up(n, PACK_ALIGN)
    return out


def _slots_to_cols(slots):
    return slots.transpose(1, 0, 2).reshape(slots.shape[1], -1)


def _even_in_padded(cfg, w):
    main = 2 * cfg.DK + 2 * cfg.DV
    return jnp.concatenate([w[:, :main], w[:, main + GATE_RANK:], w[:, main:main + GATE_RANK],
                            jnp.zeros((w.shape[0], cfg.EP - cfg.EVEN_IN), w.dtype)], axis=1)


def _even_in_unpadded(cfg, wp):
    main = 2 * cfg.DK + 2 * cfg.DV
    rest = main + 2 * cfg.W
    return jnp.concatenate([wp[:, :main], wp[:, rest:rest + GATE_RANK], wp[:, main:rest]], axis=1)


def _odd_in_padded(cfg, w):
    return jnp.concatenate([w, jnp.zeros((w.shape[0], cfg.OP - cfg.ODD_IN), w.dtype)], axis=1)


def _odd_in_unpadded(cfg, wp):
    return wp[:, :cfg.ODD_IN]


def _group_lanes(cfg, v):
    lead = v.shape[:-1]
    g = jnp.moveaxis(v.reshape(lead + (cfg.NG, SSD_HG)), -2, 0)
    return jnp.pad(g, [(0, 0)] * (g.ndim - 1) + [(0, LANES - SSD_HG)])


def _ungroup_lanes(cfg, g):
    v = jnp.moveaxis(g[..., :SSD_HG], 0, -2)
    return v.reshape(v.shape[:-2] + (cfg.NH,))


def train_step(cfg, p, loss_target):
    S, D = cfg.S, cfg.D
    me = 4 * lax.axis_index("x") + 2 * lax.axis_index("y") + lax.axis_index("c")

    big = ["ev_w_in", "ev_w_out", "od_w_in", "od_w_out", "ffn_w_gate", "ffn_w_up", "ffn_w_down"]
    small_sharded = ["ev_gla_w_gate", "ev_lru_conv_w", "od_norm", "od_conv_w", "od_conv_b", "od_gnorm"]
    replicated = ["ev_norm", "ev_gla_b_gate", "ev_gla_w_onorm", "ev_lru_conv_b", "ev_lru_w_a", "ev_lru_b_a",
                  "ev_lru_w_i", "ev_lru_b_i", "ev_lru_lam", "od_dt_bias", "od_a_log", "od_d_skip", "ffn_norm",
                  "final_norm"]

    transposed = ("ffn_w_gate", "ffn_w_up")
    view = lambda n, a: jnp.swapaxes(a, 1, 2) if n in transposed else a
    wb = {n: view(n, p[n]).astype(BF16) for n in big}
    ffn_items = lambda l: [(wb["ffn_w_gate"], l), (wb["ffn_w_up"], l), (wb["ffn_w_down"], l)]
    ss_shapes = [p[n].shape for n in small_sharded]
    groups = [[(pack([p[n] for n in small_sharded]), None), (wb["ev_w_in"], 0), (wb["ev_w_out"], 0)], ffn_items(0),
              [(wb["od_w_in"], 0), (wb["od_w_out"], 0)], ffn_items(1)]
    gathers, tokens = [], []
    for i, g in enumerate(groups):
        lands = [_landing_zone(a, lead, me, f"gather_own{i}_{j}") for j, (a, lead) in enumerate(g)]
        handle, token = exchange_start(g, True, lands, f"gather_start{i}")
        gathers.append(handle)
        tokens.append(token)
    all_started = tokens[0][:1, :1] + tokens[1][:1, :1] + tokens[2][:1, :1] + tokens[3][:1, :1]

    ss_all, gw_ev_in, gw_ev_out = exchange_wait(gathers[0], all_started, "gather_wait0")[1]
    gs = dict(zip(small_sharded, unpack(ss_all, ss_shapes)))
    w_ev_in = cols_from_slots(gw_ev_in, functools.partial(_even_in_padded, cfg), cfg.EP, "ev_w_in_cols")
    w_ev_out = gw_ev_out.reshape(D, D)

    gla_wg = jnp.pad(_slots_to_cols(gs["ev_gla_w_gate"][:, 0]), ((0, LANES - GATE_RANK), (0, 0)))
    lru_cw = _slots_to_cols(gs["ev_lru_conv_w"][:, 0])
    od_norm = gs["od_norm"].transpose(1, 0, 2).reshape(1, D)
    od_cw = _slots_to_cols(gs["od_conv_w"][:, 0])
    od_cb = gs["od_conv_b"].transpose(1, 0, 2).reshape(1, cfg.CD)
    od_gn = gs["od_gnorm"].transpose(1, 0, 2).reshape(1, cfg.DI)

    x0 = p["x"][0]
    target = loss_target[0]
    ev_norm = p["ev_norm"] + all_started
    bg = p["ev_gla_b_gate"]
    wn = p["ev_gla_w_onorm"]
    lru_cb = p["ev_lru_conv_b"]
    wa, wi = p["ev_lru_w_a"][0], p["ev_lru_w_i"][0]
    ba, bi, lam = p["ev_lru_b_a"], p["ev_lru_b_i"], p["ev_lru_lam"]
    dt_bias, a_log, d_skip = (_group_lanes(cfg, p[n]) for n in ("od_dt_bias", "od_a_log", "od_d_skip"))
    ffn_norm = [p["ffn_norm"][l:l + 1] for l in range(2)]
    final_norm = p["final_norm"].reshape(1, D)

    def ffn_forward(l, x):
        w_gate, w_up, w_down = exchange_wait(gathers[1 + 2 * l], x, f"gather_wait{1 + 2 * l}")[1]
        h = norm_fwd(cfg, x, ffn_norm[l], f"ffn{l}_norm")
        gate, up, act = ffn_gate_up(h, w_gate, w_up, f"ffn{l}_gate_up")
        out = matmul(act, w_down, a_slot="k", b_slot="k", res=x, name=f"ffn{l}_down", tn=512)
        return out, (h, gate, up, act, w_gate, w_up, w_down)

    h0 = norm_fwd(cfg, x0, ev_norm, "ev_norm")
    p0 = matmul(h0, w_ev_in, name="ev_in", tn=768)
    gla_out, gla_states = gla_fwd(cfg, p0, gla_wg, bg, wn, "gla_fwd")
    lru_col = 2 * cfg.DK + 2 * cfg.DV
    lru_xc = conv_fwd(cfg, p0, lru_col, cfg.W, lru_cw, lru_cb, "lru_conv")
    lru_out, lru_states = lru_fwd(cfg, lru_xc, p0, wa, wi, ba, bi, lam, "lru_fwd")
    mix = jnp.concatenate([gla_out, lru_out], axis=1)
    x1 = matmul(mix, w_ev_out, res=x0, name="ev_out")
    x2, ffn0_saved = ffn_forward(0, x1)

    gw_od_in, gw_od_out = exchange_wait(gathers[2], x2, "gather_wait2")[1]
    w_od_in = cols_from_slots(gw_od_in, functools.partial(_odd_in_padded, cfg), cfg.OP, "od_w_in_cols")
    w_od_out = gw_od_out.reshape(cfg.DI, D)
    h2 = norm_fwd(cfg, x2, od_norm, "od_norm")
    p1 = matmul(h2, w_od_in, name="od_in", tn=768)
    od_xc = conv_fwd(cfg, p1, cfg.DI, cfg.CD, od_cw, od_cb, "od_conv")
    dt_col = cfg.DI + cfg.CD
    dt_raw = _group_lanes(cfg, p1[:, dt_col:dt_col + cfg.NH])
    ssd_out, ssd_states = ssd_fwd(cfg, od_xc, p1, dt_raw, dt_bias, a_log, d_skip, od_gn, "ssd_fwd")
    x3 = matmul(ssd_out, w_od_out, res=x2, name="od_out")
    x4, ffn1_saved = ffn_forward(1, x3)

    loss_part, dx4, d_final_norm = head_fwd_bwd(cfg, x4, final_norm, target, "head")
    loss = lax.psum(loss_part[0, 0], MESH_AXES)

    def ffn_backward(l, x, saved, dx_out, after):
        h, gate, up, act, w_gate, w_up, w_down = saved
        dgate, dup = ffn_dgate_dup(dx_out, w_down, gate, up, after, f"ffn{l}_dgate_dup")
        d_down = matmul(act, dx_out, ta=True, a_slot="m", out_dtype=BF16, name=f"ffn{l}_dwdown")
        dh = matmul(dgate, w_gate, a_slot="k", b_slot="k", name=f"ffn{l}_dh_gate", tn=512)
        dh = matmul(dup, w_up, a_slot="k", b_slot="k", res=dh, name=f"ffn{l}_dh_up", tn=512)
        d_gate = matmul(dgate, h, ta=True, a_slot="m", out_dtype=BF16, name=f"ffn{l}_dwgate")
        d_up = matmul(dup, h, ta=True, a_slot="m", out_dtype=BF16, name=f"ffn{l}_dwup")
        dx, dnorm = norm_bwd(cfg, x, ffn_norm[l], dh, dx_out, f"ffn{l}_norm_bwd")
        sent, token = start_grads([d_gate, d_up, d_down], f"grads_start_ffn{l}")
        return dx, dnorm, sent, token

    def start_grads(arrays, name):
        return exchange_start([(a, None) for a in arrays], False, [lax.empty(a.shape, a.dtype) for a in arrays], name)

    dx3, d_ffn_norm1, sent_ffn1, token = ffn_backward(1, x3, ffn1_saved, dx4, None)

    d_ssd_out = matmul(dx3, w_od_out, tb=True, after=token, name="od_dmix")
    d_od_out = matmul(ssd_out, dx3, ta=True, out_dtype=BF16, name="od_dwout")
    dxs, dbm, dcm, dz, d_dt_raw, d_dt_bias, d_a_log, d_d_skip, d_od_gn = ssd_bwd(
        cfg, od_xc, p1, dt_raw, dt_bias, a_log, d_skip, od_gn, ssd_states, d_ssd_out, "ssd_bwd")
    conv_parts, col = [], 0
    for part, dy in (("x", dxs), ("b", dbm), ("c", dcm)):
        width = dy.shape[1]
        conv_parts.append(conv_bwd(cfg, p1, cfg.DI + col, width, od_cw[:, col:col + width], dy, "od_conv_bwd_" + part))
        col += width
    d_od_cw = jnp.concatenate([c[1] for c in conv_parts], axis=1)
    d_od_cb = jnp.concatenate([c[2] for c in conv_parts], axis=1)
    d_dt = _ungroup_lanes(cfg, d_dt_raw).astype(BF16)
    dp1 = jnp.concatenate([dz] + [c[0] for c in conv_parts] + [d_dt, jnp.zeros((S, cfg.OP - cfg.ODD_IN), BF16)], axis=1)
    dh2 = matmul(dp1, w_od_in, tb=True, name="od_dh", tk=1536)
    d_od_in = matmul(h2, dp1, ta=True, out_dtype=BF16, name="od_dwin", tn=768)
    dx2, d_od_norm = norm_bwd(cfg, x2, od_norm, dh2, dx3, "od_norm_bwd")
    d_od_in_slots = slots_from_cols(d_od_in, functools.partial(_odd_in_unpadded, cfg), p["od_w_in"].shape[2],
                                    "od_dwin_slots")
    sent_od, token = start_grads([d_od_in_slots, d_od_out.reshape((N_DEV,) + p["od_w_out"].shape[1:])], "grads_start_od")

    dx1, d_ffn_norm0, sent_ffn0, token = ffn_backward(0, x1, ffn0_saved, dx2, token)

    d_mix = matmul(dx1, w_ev_out, tb=True, after=token, name="ev_dmix")
    d_ev_out = matmul(mix, dx1, ta=True, out_dtype=BF16, name="ev_dwout")
    d_qkvg, d_glr, d_gla_wg, d_bg, d_wn = gla_bwd(cfg, p0, gla_wg, bg, wn, gla_states, d_mix[:, :cfg.DV], "gla_bwd")
    d_lru_xc, d_gate_br, d_wa, d_wi, d_ba, d_bi, d_lam = lru_bwd(
        cfg, lru_xc, p0, wa, wi, ba, bi, lam, lru_states, d_mix[:, cfg.DV:], "lru_bwd")
    d_xbr, d_lru_cw, d_lru_cb = conv_bwd(cfg, p0, lru_col, cfg.W, lru_cw, d_lru_xc, "lru_conv_bwd")
    dp0 = jnp.concatenate([d_qkvg, d_xbr, d_gate_br, d_glr, jnp.zeros((S, cfg.EP - lru_col - 2 * cfg.W - LANES), BF16)],
                          axis=1)
    d_ev_in = matmul(h0, dp0, ta=True, out_dtype=BF16, name="ev_dwin", tn=768)
    d_ev_in_slots = slots_from_cols(d_ev_in, functools.partial(_even_in_unpadded, cfg), p["ev_w_in"].shape[2],
                                    "ev_dwin_slots")
    sent_ev, token = start_grads([d_ev_in_slots, d_ev_out.reshape((N_DEV,) + p["ev_w_out"].shape[1:])], "grads_start_ev")
    dh0 = matmul(dp0, w_ev_in, tb=True, after=token, name="ev_dh", tk=1792)
    grad_x, d_ev_norm = norm_bwd(cfg, x0, ev_norm, dh0, dx1, "ev_norm_bwd")

    out = {"loss": loss, "grad_x": grad_x[None]}

    def update(names, sent, after, wait_name):
        s, r = exchange_wait(sent[0], after, wait_name + "0")
        sends, recvs = [[a] for a in s], [[a] for a in r]
        for extra in sent[1:]:
            s, r = exchange_wait(extra, after, wait_name + "1")
            for i in range(len(names)):
                sends[i].append(s[i])
                recvs[i].append(r[i])
        for i, n in enumerate(names):
            res = adamw_sharded(recvs[i], sends[i], me, view(n, p[n]), view(n, p["m_" + n]), view(n, p["v_" + n]),
                                "adamw_" + n)
            out["grad_" + n], out["delta_" + n], out["new_m_" + n], out["new_v_" + n] = (view(n, a) for a in res)
        return res[-1]

    small_full = {
        "ev_gla_w_gate": d_gla_wg[:GATE_RANK][None], "ev_lru_conv_w": d_lru_cw[None], "od_norm": d_od_norm,
        "od_conv_w": d_od_cw[None], "od_conv_b": d_od_cb, "od_gnorm": d_od_gn,
        "ev_norm": d_ev_norm, "ev_gla_b_gate": d_bg, "ev_gla_w_onorm": d_wn, "ev_lru_conv_b": d_lru_cb,
        "ev_lru_w_a": d_wa[None], "ev_lru_b_a": d_ba, "ev_lru_w_i": d_wi[None], "ev_lru_b_i": d_bi,
        "ev_lru_lam": d_lam, "od_dt_bias": _ungroup_lanes(cfg, d_dt_bias), "od_a_log": _ungroup_lanes(cfg, d_a_log),
        "od_d_skip": _ungroup_lanes(cfg, d_d_skip), "ffn_norm": jnp.concatenate([d_ffn_norm0, d_ffn_norm1], axis=0),
        "final_norm": d_final_norm.reshape(D),
    }
    small = small_sharded + replicated
    small_packed = pack([small_full[n] for n in small])
    sent_small, token = exchange_start([(small_packed, None)], True,
                                       [_landing_zone(small_packed, None, me, "gather_small_grads_own")],
                                       "gather_small_grads")

    done = update(["od_w_in", "od_w_out"], [sent_od], token, "grads_wait_od")
    done = update(["ffn_w_gate", "ffn_w_up", "ffn_w_down"], [sent_ffn0, sent_ffn1], done, "grads_wait_ffn")
    done = update(["ev_w_in", "ev_w_out"], [sent_ev], done, "grads_wait_ev")

    small_all = exchange_wait(sent_small, done, "gather_small_grads_wait")[1][0]
    g_small = dict(zip(small, unpack(reduce_slots(small_all, "sum_small_grads"), [small_full[n].shape for n in small])))
    for n in small_sharded:
        width = p[n].shape[-1]
        g_small[n] = lax.dynamic_slice_in_dim(g_small[n], me * width, width, axis=g_small[n].ndim - 1)
    shapes = [p[n].shape for n in small]
    g_buf = pack([g_small[n] for n in small])
    delta, new_m, new_v = adamw(pack([p[n] for n in small]), g_buf, pack([p["m_" + n] for n in small]),
                                pack([p["v_" + n] for n in small]), "adamw_small")
    for kind, buf in (("grad_", g_buf), ("delta_", delta), ("new_m_", new_m), ("new_v_", new_v)):
        for n, a in zip(small, unpack(buf, shapes)):
            out[kind + n] = a
    return out


WEIGHTS = ['ev_norm', 'ev_w_in', 'ev_gla_w_gate', 'ev_gla_b_gate', 'ev_gla_w_onorm', 'ev_lru_conv_w', 'ev_lru_conv_b',
           'ev_lru_w_a', 'ev_lru_b_a', 'ev_lru_w_i', 'ev_lru_b_i', 'ev_lru_lam', 'ev_w_out', 'od_norm', 'od_w_in',
           'od_conv_w', 'od_conv_b', 'od_dt_bias', 'od_a_log', 'od_d_skip', 'od_gnorm', 'od_w_out', 'ffn_norm',
           'ffn_w_gate', 'ffn_w_up', 'ffn_w_down', 'final_norm']


def kernel(x, ev_norm, ev_w_in, ev_gla_w_gate, ev_gla_b_gate, ev_gla_w_onorm, ev_lru_conv_w, ev_lru_conv_b, ev_lru_w_a, ev_lru_b_a, ev_lru_w_i, ev_lru_b_i, ev_lru_lam, ev_w_out, od_norm, od_w_in, od_conv_w, od_conv_b, od_dt_bias, od_a_log, od_d_skip, od_gnorm, od_w_out, ffn_norm, ffn_w_gate, ffn_w_up, ffn_w_down, final_norm, loss_target, m_ev_norm, m_ev_w_in, m_ev_gla_w_gate, m_ev_gla_b_gate, m_ev_gla_w_onorm, m_ev_lru_conv_w, m_ev_lru_conv_b, m_ev_lru_w_a, m_ev_lru_b_a, m_ev_lru_w_i, m_ev_lru_b_i, m_ev_lru_lam, m_ev_w_out, m_od_norm, m_od_w_in, m_od_conv_w, m_od_conv_b, m_od_dt_bias, m_od_a_log, m_od_d_skip, m_od_gnorm, m_od_w_out, m_ffn_norm, m_ffn_w_gate, m_ffn_w_up, m_ffn_w_down, m_final_norm, v_ev_norm, v_ev_w_in, v_ev_gla_w_gate, v_ev_gla_b_gate, v_ev_gla_w_onorm, v_ev_lru_conv_w, v_ev_lru_conv_b, v_ev_lru_w_a, v_ev_lru_b_a, v_ev_lru_w_i, v_ev_lru_b_i, v_ev_lru_lam, v_ev_w_out, v_od_norm, v_od_w_in, v_od_conv_w, v_od_conv_b, v_od_dt_bias, v_od_a_log, v_od_d_skip, v_od_gnorm, v_od_w_out, v_ffn_norm, v_ffn_w_gate, v_ffn_w_up, v_ffn_w_down, v_final_norm):
    args = dict(locals())
    p = {n: a for n, a in args.items() if n != "loss_target"}
    cfg = Cfg(S=x.shape[1], D=x.shape[2], DFF=ffn_w_gate.shape[2] * N_DEV)
    out = train_step(cfg, p, loss_target)
    return (out["loss"], out["grad_x"], *[out["grad_" + w] for w in WEIGHTS], *[out["delta_" + w] for w in WEIGHTS],
            *[out["new_m_" + w] for w in WEIGHTS], *[out["new_v_" + w] for w in WEIGHTS])
```

```python
import functools
import math
from typing import NamedTuple

import jax
import jax.numpy as jnp
from jax import lax
from jax.experimental import pallas as pl
from jax.experimental.pallas import tpu as pltpu

F32 = jnp.float32
BF16 = jnp.bfloat16
MESH_AXES = ("x", "y", "c")
N_DEV = 8
LANES = 128
SUBLANES = 8
VMEM_LIMIT = 56 * 1024 * 1024

NORM_EPS = 1e-6
CONV_WIDTH = 4
CHUNK = 64
HEAD_K = 128
HEAD_V = 256
GATE_RANK = 16
GATE_NORM = 16.0
LRU_BLOCK = 128
LRU_C = 8.0
SSD_P = 64
SSD_N = 128
SSD_HG = 8
SSD_GW = SSD_HG * SSD_P

ADAM_LR = 0.001
ADAM_B1 = 0.9
ADAM_B2 = 0.999
ADAM_EPS = 1e-08
ADAM_WD = 0.01
ADAM_STEP = 10


class Cfg(NamedTuple):
    S: int
    D: int
    DFF: int

    @property
    def GH(self):
        return self.D // 512

    @property
    def NB(self):
        return self.D // 256

    @property
    def NG(self):
        return self.D // 256

    @property
    def DK(self):
        return HEAD_K * self.GH

    @property
    def DV(self):
        return HEAD_V * self.GH

    @property
    def W(self):
        return LRU_BLOCK * self.NB

    @property
    def DI(self):
        return SSD_GW * self.NG

    @property
    def CD(self):
        return self.DI + 2 * self.NG * SSD_N

    @property
    def NH(self):
        return SSD_HG * self.NG

    @property
    def EVEN_IN(self):
        return 2 * self.DK + 2 * self.DV + GATE_RANK + 2 * self.W

    @property
    def ODD_IN(self):
        return self.DI + self.CD + self.NH

    @property
    def EP(self):
        return _round_up(2 * self.DK + 2 * self.DV + 2 * self.W + LANES, 768)

    @property
    def OP(self):
        return _round_up(self.DI + self.CD + LANES, 768)


def _round_up(n, m):
    return (n + m - 1) // m * m


def _tile(n, pref):
    if n <= pref:
        return n
    t = pref - pref % LANES
    while n % t:
        t -= LANES
    return t


def _cparams(n_axes):
    return pltpu.CompilerParams(dimension_semantics=("arbitrary",) * n_axes, vmem_limit_bytes=VMEM_LIMIT)


def _dg(a, b, ca, cb):
    return lax.dot_general(a.astype(BF16), b.astype(BF16), (((ca,), (cb,)), ((), ())), preferred_element_type=F32)


@functools.partial(jax.custom_vjp, nondiff_argnums=(2, 3))
def bdot(a, b, ca, cb):
    return _dg(a, b, ca, cb)


def _bdot_fwd(a, b, ca, cb):
    return _dg(a, b, ca, cb), (a, b)


def _bdot_bwd(ca, cb, res, g):
    a, b = res
    da = _dg(g, b, 1, 1 - cb) if ca == 1 else _dg(b, g, 1 - cb, 1)
    db = _dg(a, g, 1 - ca, 0) if cb == 0 else _dg(g, a, 0, 1 - ca)
    return da.astype(a.dtype), db.astype(b.dtype)


bdot.defvjp(_bdot_fwd, _bdot_bwd)


def _lower_tri(n):
    r = lax.broadcasted_iota(jnp.int32, (n, n), 0)
    c = lax.broadcasted_iota(jnp.int32, (n, n), 1)
    return c <= r


def _running_sum(x, reverse):
    n = x.shape[0]
    r = lax.broadcasted_iota(jnp.int32, x.shape, 0)
    d = 1
    while d < n:
        if reverse:
            x = x + jnp.where(r < n - d, pltpu.roll(x, n - d, 0), 0.0)
        else:
            x = x + jnp.where(r >= d, pltpu.roll(x, d, 0), 0.0)
        d *= 2
    return x


@jax.custom_vjp
def cumsum_rows(x):
    return _running_sum(x, False)


cumsum_rows.defvjp(lambda x: (_running_sum(x, False), None), lambda _, g: (_running_sum(g, True),))


def _tri_dot(x, transposed):
    n = x.shape[0]
    return lax.dot_general(_lower_tri(n).astype(F32), x, (((0 if transposed else 1,), (0,)), ((), ())),
                           precision=lax.Precision.HIGHEST, preferred_element_type=F32)


@jax.custom_vjp
def cumsum_rows_mxu(x):
    return _tri_dot(x, False)


cumsum_rows_mxu.defvjp(lambda x: (_tri_dot(x, False), None), lambda _, g: (_tri_dot(g, True),))


def _row(x, i):
    r = lax.broadcasted_iota(jnp.int32, x.shape, 0)
    return jnp.sum(jnp.where(r == i, x, 0.0), axis=0, keepdims=True)


def _softplus_raw(x):
    return jnp.maximum(x, 0.0) + jnp.log(1.0 + jnp.exp(-jnp.abs(x)))


@jax.custom_vjp
def softplus(x):
    return _softplus_raw(x)


softplus.defvjp(lambda x: (_softplus_raw(x), x), lambda x, g: (g * jax.nn.sigmoid(x),))


@jax.custom_vjp
def log_sigmoid(x):
    return -_softplus_raw(-x)


log_sigmoid.defvjp(lambda x: (-_softplus_raw(-x), x), lambda x, g: (g * jax.nn.sigmoid(-x),))


def silu(x):
    return x * jax.nn.sigmoid(x)


def gelu_tanh(x):
    return 0.5 * x * (1.0 + jnp.tanh(math.sqrt(2.0 / math.pi) * (x + 0.044715 * (x * x * x))))


def _expm1(x):
    series = x * (1.0 + 0.5 * x * (1.0 + (1.0 / 3.0) * x))
    return jnp.where(jnp.abs(x) < 1e-2, series, jnp.exp(x) - 1.0)


def rms(x, w):
    return x * lax.rsqrt(jnp.mean(x * x, axis=-1, keepdims=True) + NORM_EPS) * w


def _rows_iota(shape):
    return lax.broadcasted_iota(jnp.int32, shape, 0)


def _scan_up(a, u):
    n = a.shape[0]
    r = _rows_iota(a.shape)
    d = 1
    while d < n:
        m = r >= d
        a_s = jnp.where(m, pltpu.roll(a, d, 0), 1.0)
        u_s = jnp.where(m, pltpu.roll(u, d, 0), 0.0)
        u = a * u_s + u
        a = a * a_s
        d *= 2
    return u


def _scan_down(a, u):
    n = a.shape[0]
    r = _rows_iota(a.shape)
    d = 1
    while d < n:
        m = r < n - d
        a_s = jnp.where(m, pltpu.roll(a, n - d, 0), 1.0)
        u_s = jnp.where(m, pltpu.roll(u, n - d, 0), 0.0)
        u = a * u_s + u
        a = a * a_s
        d *= 2
    return u


@jax.custom_vjp
def lin_scan(a, u):
    return _scan_up(a, u)


def _lin_scan_fwd(a, u):
    h = _scan_up(a, u)
    return h, (a, h)


def _lin_scan_bwd(res, g):
    a, h = res
    n = a.shape[0]
    r = _rows_iota(a.shape)
    a_next = jnp.where(r < n - 1, pltpu.roll(a, n - 1, 0), 0.0)
    gt = _scan_down(a_next, g)
    h_prev = jnp.where(r >= 1, pltpu.roll(h, 1, 0), 0.0)
    return gt * h_prev, gt


lin_scan.defvjp(_lin_scan_fwd, _lin_scan_bwd)


def _expand_heads(v):
    r = v.shape[0]
    return jnp.concatenate([jnp.broadcast_to(v[:, h:h + 1], (r, SSD_P)) for h in range(SSD_HG)], axis=1)


@jax.custom_vjp
def _split_heads(x):
    return tuple(x[:, h * SSD_P:(h + 1) * SSD_P] for h in range(SSD_HG))


_split_heads.defvjp(lambda x: (_split_heads(x), None), lambda _, gs: (jnp.concatenate(gs, axis=1),))


def matmul(a, b, *, ta=False, tb=False, a_slot=None, b_slot=None, b_lead=(), res=None, after=None, out_dtype=F32,
           name, tm=1024, tn=1024, tk=2048):
    lead = tuple(b_lead)
    ra, ca_ = a.shape[-2:]
    rb, cb_ = b.shape[-2:]
    m_st, ka_st = (ca_, ra) if ta else (ra, ca_)
    kb_st, n_st = (cb_, rb) if tb else (rb, cb_)
    kslot = a_slot == "k"
    assert kslot == (b_slot == "k")
    m = m_st * (N_DEV if a_slot == "m" else 1)
    n = n_st * (N_DEV if b_slot == "n" else 1)
    assert ka_st == kb_st, (a.shape, b.shape, ta, tb)
    tm = m_st if a_slot == "m" else _tile(m, tm)
    tn = n_st if b_slot == "n" else _tile(n, tn)
    tk = ka_st if kslot else _tile(ka_st, tk)
    nk = ka_st // tk
    ca, cb = (0 if ta else 1), (1 if tb else 0)
    nl = (None,) * len(lead)

    if a_slot is None:
        a_spec = pl.BlockSpec((tk, tm), lambda i, j, k: (k, i)) if ta else pl.BlockSpec((tm, tk), lambda i, j, k: (i, k))
    elif a_slot == "m":
        a_spec = (pl.BlockSpec((None, tk, tm), lambda i, j, k: (i, k, 0)) if ta
                  else pl.BlockSpec((None, tm, tk), lambda i, j, k: (i, 0, k)))
    else:
        a_spec = (pl.BlockSpec((N_DEV, tk, tm), lambda i, j, k: (0, 0, i)) if ta
                  else pl.BlockSpec((N_DEV, tm, tk), lambda i, j, k: (0, i, 0)))
    if b_slot is None:
        b_spec = (pl.BlockSpec(nl + (tn, tk), lambda i, j, k: lead + (j, k)) if tb
                  else pl.BlockSpec(nl + (tk, tn), lambda i, j, k: lead + (k, j)))
    elif b_slot == "n":
        b_spec = (pl.BlockSpec((None,) + nl + (tn, tk), lambda i, j, k: (j,) + lead + (0, k)) if tb
                  else pl.BlockSpec((None,) + nl + (tk, tn), lambda i, j, k: (j,) + lead + (k, 0)))
    else:
        b_spec = (pl.BlockSpec((N_DEV,) + nl + (tn, tk), lambda i, j, k: (0,) + lead + (j, 0)) if tb
                  else pl.BlockSpec((N_DEV,) + nl + (tk, tn), lambda i, j, k: (0,) + lead + (0, j)))
    if a_slot == "m":
        o_spec, o_shape = pl.BlockSpec((None, tm, tn), lambda i, j, k: (i, 0, j)), (N_DEV, tm, n)
    elif b_slot == "n":
        o_spec, o_shape = pl.BlockSpec((None, tm, tn), lambda i, j, k: (j, i, 0)), (N_DEV, m, tn)
    else:
        o_spec, o_shape = pl.BlockSpec((tm, tn), lambda i, j, k: (i, j)), (m, n)
    assert res is None or (a_slot != "m" and b_slot != "n")

    def dot(x, y):
        return lax.dot_general(x.astype(BF16), y.astype(BF16), (((ca,), (cb,)), ((), ())), preferred_element_type=F32)

    def body(*refs):
        a_ref, b_ref = refs[:2]
        r_ref = refs[2] if res is not None else None
        o_ref = refs[2 + (res is not None) + (after is not None)]

        def finish(acc):
            if r_ref is not None:
                acc = acc + r_ref[...].astype(F32)
            o_ref[...] = acc.astype(o_ref.dtype)

        if kslot:
            acc = dot(a_ref[0], b_ref[0])
            for s in range(1, N_DEV):
                acc = acc + dot(a_ref[s], b_ref[s])
            finish(acc)
        elif nk == 1:
            finish(dot(a_ref[...], b_ref[...]))
        else:
            acc_ref = refs[-1]
            k = pl.program_id(2)

            @pl.when(k == 0)
            def _():
                acc_ref[...] = dot(a_ref[...], b_ref[...])

            @pl.when(k > 0)
            def _():
                acc_ref[...] += dot(a_ref[...], b_ref[...])

            @pl.when(k == nk - 1)
            def _():
                finish(acc_ref[...])

    in_specs = [a_spec, b_spec]
    args = [a, b]
    if res is not None:
        in_specs.append(pl.BlockSpec((tm, tn), lambda i, j, k: (i, j)))
        args.append(res)
    if after is not None:
        in_specs.append(pl.BlockSpec(memory_space=pl.ANY))
        args.append(after)
    return pl.pallas_call(
        body, name=name, grid=(m // tm, n // tn, nk), in_specs=in_specs, out_specs=o_spec,
        out_shape=jax.ShapeDtypeStruct(o_shape, out_dtype),
        scratch_shapes=[pltpu.VMEM((tm, tn), F32)] if nk > 1 else [], compiler_params=_cparams(3),
    )(*args)


def seq_call(name, fn, grid, ins, outs, accs=(), carries=()):
    n_in, n_out, n_acc = len(ins), len(outs), len(accs)

    def body(*refs):
        in_refs = refs[:n_in]
        out_refs = refs[n_in:n_in + n_out]
        acc_refs = refs[n_in + n_out:n_in + n_out + n_acc]
        c_refs = refs[n_in + n_out + n_acc:]

        if acc_refs or c_refs:
            @pl.when(pl.program_id(1) == 0)
            def _():
                for r in tuple(acc_refs) + tuple(c_refs):
                    r[...] = jnp.zeros_like(r)

        o, a, c = fn([r[...] for r in in_refs], [r[...] for r in c_refs])
        for r, v in zip(out_refs, o, strict=True):
            r[...] = v.astype(r.dtype)
        for r, v in zip(acc_refs, a, strict=True):
            r[...] += v
        for r, v in zip(c_refs, c, strict=True):
            r[...] = v

    return pl.pallas_call(
        body, name=name, grid=grid,
        in_specs=[pl.BlockSpec(blk, im) for _, blk, im in ins],
        out_specs=[pl.BlockSpec(blk, im) for _, _, blk, im in outs] + [pl.BlockSpec(blk, im) for _, blk, im in accs],
        out_shape=[jax.ShapeDtypeStruct(s, d) for s, d, _, _ in outs] + [jax.ShapeDtypeStruct(s, F32) for s, _, _ in accs],
        scratch_shapes=[pltpu.VMEM(s, F32) for s in carries], compiler_params=_cparams(2),
    )(*[a for a, _, _ in ins])


def exchange(arrays, gather, name, after=None):
    n = len(arrays)
    extra = [] if after is None else [after]

    def body(*refs):
        x_refs, o_refs = refs[:n], refs[n + len(extra):2 * n + len(extra)]
        send_sems, recv_sems, local_sems = refs[2 * n + len(extra):]
        pos = [lax.axis_index(ax) for ax in MESH_AXES]
        me = 4 * pos[0] + 2 * pos[1] + pos[2]
        copies = []
        for i in range(n):
            own = pltpu.make_async_copy(x_refs[i] if gather else x_refs[i].at[me], o_refs[i].at[me], local_sems.at[i])
            own.start()
            copies.append(own)
        for k in range(1, N_DEV):
            bits = ((k >> 2) & 1, (k >> 1) & 1, k & 1)
            peer = tuple(1 - p if b else p for p, b in zip(pos, bits))
            peer_id = 4 * peer[0] + 2 * peer[1] + peer[2]
            for i in range(n):
                cp = pltpu.make_async_remote_copy(
                    src_ref=x_refs[i] if gather else x_refs[i].at[peer_id], dst_ref=o_refs[i].at[me],
                    send_sem=send_sems.at[i * (N_DEV - 1) + k - 1], recv_sem=recv_sems.at[i * (N_DEV - 1) + k - 1],
                    device_id=peer, device_id_type=pl.DeviceIdType.MESH)
                cp.start()
                copies.append(cp)
        for cp in copies:
            cp.wait()

    hbm = pl.BlockSpec(memory_space=pltpu.HBM)
    return pl.pallas_call(
        body, name=name, in_specs=[hbm] * n + [pl.BlockSpec(memory_space=pl.ANY)] * len(extra), out_specs=[hbm] * n,
        out_shape=[jax.ShapeDtypeStruct(((N_DEV,) + a.shape) if gather else a.shape, a.dtype) for a in arrays],
        scratch_shapes=[pltpu.SemaphoreType.DMA((n * (N_DEV - 1),)), pltpu.SemaphoreType.DMA((n * (N_DEV - 1),)),
                        pltpu.SemaphoreType.DMA((n,))],
    )(*arrays, *extra)


_HBM = pl.BlockSpec(memory_space=pltpu.HBM)
_SEM = pl.BlockSpec(memory_space=pltpu.SEMAPHORE)
N_PEERS = N_DEV - 1


def _mesh_pos():
    pos = [lax.axis_index(ax) for ax in MESH_AXES]
    return pos, 4 * pos[0] + 2 * pos[1] + pos[2]


def _peers(pos):
    out = []
    for k in range(1, N_DEV):
        bits = ((k >> 2) & 1, (k >> 1) & 1, k & 1)
        peer = tuple(1 - p if b else p for p, b in zip(pos, bits))
        out.append((peer, 4 * peer[0] + 2 * peer[1] + peer[2]))
    return out


def _part(x_ref, lead, gather, slot):
    ref = x_ref if lead is None else x_ref.at[lead]
    return ref if gather else ref.at[slot]


OWN_BLOCK_BYTES = 2 * 1024 * 1024


def _landing_zone(a, lead, me, name):
    r, c = a.shape[-2:]
    tr = r
    while tr * _round_up(c, LANES) * a.dtype.itemsize > OWN_BLOCK_BYTES and tr % 32 == 0:
        tr //= 2

    def body(me_ref, x_ref, o_ref):
        o_ref[...] = x_ref[...]

    x_spec = (pl.BlockSpec((tr, c), lambda i, me_ref: (i, 0)) if lead is None
              else pl.BlockSpec((None, tr, c), lambda i, me_ref: (lead, i, 0)))
    grid_spec = pltpu.PrefetchScalarGridSpec(
        num_scalar_prefetch=1, grid=(r // tr,), in_specs=[x_spec],
        out_specs=pl.BlockSpec((None, tr, c), lambda i, me_ref: (me_ref[0], i, 0)))
    return pl.pallas_call(body, name=name, grid_spec=grid_spec, out_shape=jax.ShapeDtypeStruct((N_DEV, r, c), a.dtype),
                          compiler_params=_cparams(1))(jnp.reshape(me, (1,)).astype(jnp.int32), a)


def _split_copies(items, gather, x_refs, land_refs, send_sems, recv_sems):
    pos, me = _mesh_pos()
    copies = []
    for i, (_, lead) in enumerate(items):
        for k, (peer, peer_id) in enumerate(_peers(pos)):
            copies.append(pltpu.make_async_remote_copy(
                src_ref=_part(x_refs[i], lead, gather, peer_id), dst_ref=land_refs[i].at[me],
                send_sem=send_sems.at[i * N_PEERS + k], recv_sem=recv_sems.at[i * N_PEERS + k],
                device_id=peer, device_id_type=pl.DeviceIdType.MESH))
    return copies


def exchange_start(items, gather, lands, name):
    n = len(items)
    lands = list(lands)
    xs = [a for a, _ in items]

    def body(*refs):
        x_refs, land_refs = refs[:n], refs[n:2 * n]
        send_sems, recv_sems, token = refs[2 * n], refs[2 * n + 1], refs[-1]
        for cp in _split_copies(items, gather, x_refs, land_refs, send_sems, recv_sems):
            cp.start()
        token[...] = jnp.zeros_like(token)

    outs = pl.pallas_call(
        body, name=name,
        out_shape=(pltpu.SemaphoreType.DMA((n * N_PEERS,)), pltpu.SemaphoreType.DMA((n * N_PEERS,)),
                   *[pltpu.HBM(v.shape, v.dtype) for v in xs + lands], jax.ShapeDtypeStruct((SUBLANES, LANES), F32)),
        in_specs=[_HBM] * (2 * n),
        out_specs=(_SEM, _SEM, *[_HBM] * (2 * n), pl.BlockSpec(memory_space=pltpu.VMEM)),
        input_output_aliases={i: 2 + i for i in range(2 * n)},
        compiler_params=pltpu.CompilerParams(has_side_effects=pltpu.SideEffectType.DATAFLOW_SIDE_EFFECTING),
    )(*[pltpu.with_memory_space_constraint(v, pltpu.HBM) for v in xs + lands])
    handle = (items, gather, outs[0], outs[1], outs[2:2 + n], outs[2 + n:2 + 2 * n])
    return handle, outs[-1]


def exchange_wait(handle, after, name):
    items, gather, send_sems, recv_sems, x_thru, land_thru = handle
    n = len(items)

    def body(*refs):
        x_refs, land_refs = refs[:n], refs[n:2 * n]
        for cp in _split_copies(items, gather, x_refs, land_refs, refs[2 * n], refs[2 * n + 1]):
            cp.wait_send()
            cp.wait_recv()

    outs = pl.pallas_call(
        body, name=name, out_shape=tuple(pltpu.HBM(v.shape, v.dtype) for v in tuple(x_thru) + tuple(land_thru)),
        in_specs=[_HBM] * (2 * n) + [_SEM, _SEM, pl.BlockSpec(memory_space=pl.ANY)], out_specs=tuple([_HBM] * (2 * n)),
        input_output_aliases={i: i for i in range(2 * n)},
        compiler_params=pltpu.CompilerParams(has_side_effects=pltpu.SideEffectType.DATAFLOW_SIDE_EFFECTING),
    )(*x_thru, *land_thru, send_sems, recv_sems, after)
    return list(outs[:n]), list(outs[n:])


def _adam_update(w, g, m, v):
    nm = ADAM_B1 * m + (1.0 - ADAM_B1) * g
    nv = ADAM_B2 * v + (1.0 - ADAM_B2) * (g * g)
    m_hat = nm / (1.0 - ADAM_B1 ** ADAM_STEP)
    v_hat = nv / (1.0 - ADAM_B2 ** ADAM_STEP)
    return -ADAM_LR * (m_hat / (jnp.sqrt(v_hat) + ADAM_EPS) + ADAM_WD * w), nm, nv


def _sum_slots(s_ref):
    acc = s_ref[0].astype(F32)
    for j in range(1, N_DEV):
        acc = acc + s_ref[j].astype(F32)
    return acc


ADAM_BLOCK_BYTES = 10 * 1024 * 1024


def adamw_sharded(recvs, sends, me, w, m, v, name):
    nl, r, c = w.shape
    assert len(recvs) == nl and len(sends) == nl
    per_row = _round_up(c, LANES) * (nl * (N_DEV + 1) * recvs[0].dtype.itemsize + 7 * 4)
    tr = r
    while tr * per_row > ADAM_BLOCK_BYTES and tr % 16 == 0:
        tr //= 2

    def body(me_ref, *refs):
        s_refs, o_refs = refs[:nl], refs[nl:2 * nl]
        w_ref, m_ref, v_ref, g_ref, d_ref, nm_ref, nv_ref = refs[2 * nl:]
        mine = me_ref[0]

        def total(l):
            acc = jnp.where(mine == 0, o_refs[l][...], s_refs[l][0]).astype(F32)
            for j in range(1, N_DEV):
                acc = acc + jnp.where(mine == j, o_refs[l][...], s_refs[l][j]).astype(F32)
            return acc

        g = total(0)
        for l in range(1, nl):
            g = jnp.where(pl.program_id(0) == l, total(l), g)
        g_ref[...] = g
        d_ref[...], nm_ref[...], nv_ref[...] = _adam_update(w_ref[...], g, m_ref[...], v_ref[...])

    spec = pl.BlockSpec((None, tr, c), lambda l, i, me_ref: (l, i, 0))
    shp = jax.ShapeDtypeStruct(w.shape, F32)
    grid_spec = pltpu.PrefetchScalarGridSpec(
        num_scalar_prefetch=1, grid=(nl, r // tr),
        in_specs=([pl.BlockSpec((N_DEV, tr, c), lambda l, i, me_ref: (0, i, 0))] * nl
                  + [pl.BlockSpec((None, tr, c), lambda l, i, me_ref: (me_ref[0], i, 0))] * nl + [spec, spec, spec]),
        out_specs=[spec] * 4)
    return pl.pallas_call(body, name=name, grid_spec=grid_spec, out_shape=[shp] * 4, compiler_params=_cparams(2))(
        jnp.reshape(me, (1,)).astype(jnp.int32), *recvs, *sends, w, m, v)


def reduce_slots(slots, name):
    _, r, lanes = slots.shape
    tr = _tile(r, 2048)

    def body(s_ref, o_ref):
        o_ref[...] = _sum_slots(s_ref)

    return pl.pallas_call(
        body, name=name, grid=(r // tr,),
        in_specs=[pl.BlockSpec((N_DEV, tr, lanes), lambda i: (0, i, 0))],
        out_specs=pl.BlockSpec((tr, lanes), lambda i: (i, 0)),
        out_shape=jax.ShapeDtypeStruct((r, lanes), F32), compiler_params=_cparams(1),
    )(slots)


def adamw(w, g, m, v, name):
    r, lanes = w.shape
    tr = _tile(r, 2048)

    def body(w_ref, g_ref, m_ref, v_ref, d_ref, nm_ref, nv_ref):
        d_ref[...], nm_ref[...], nv_ref[...] = _adam_update(w_ref[...], g_ref[...], m_ref[...], v_ref[...])

    spec = pl.BlockSpec((tr, lanes), lambda i: (i, 0))
    shp = jax.ShapeDtypeStruct((r, lanes), F32)
    return pl.pallas_call(body, name=name, grid=(r // tr,), in_specs=[spec] * 4, out_specs=[spec] * 3,
                          out_shape=[shp] * 3, compiler_params=_cparams(1))(w, g, m, v)


def cols_from_slots(slots, place, width, name):
    _, rows, c = slots.shape
    tr = _tile(rows, 256)

    def body(s_ref, o_ref):
        o_ref[...] = place(jnp.concatenate([s_ref[j] for j in range(N_DEV)], axis=1))

    return pl.pallas_call(
        body, name=name, grid=(rows // tr,),
        in_specs=[pl.BlockSpec((N_DEV, tr, c), lambda i: (0, i, 0))],
        out_specs=pl.BlockSpec((tr, width), lambda i: (i, 0)),
        out_shape=jax.ShapeDtypeStruct((rows, width), slots.dtype), compiler_params=_cparams(1))(slots)


def slots_from_cols(full, pick, c, name):
    rows, wide = full.shape
    tr = _tile(rows, 256)

    def body(x_ref, o_ref):
        v = pick(x_ref[...])
        for j in range(N_DEV):
            o_ref[j] = v[:, j * c:(j + 1) * c]

    return pl.pallas_call(
        body, name=name, grid=(rows // tr,),
        in_specs=[pl.BlockSpec((tr, wide), lambda i: (i, 0))],
        out_specs=pl.BlockSpec((N_DEV, tr, c), lambda i: (0, i, 0)),
        out_shape=jax.ShapeDtypeStruct((N_DEV, rows, c), full.dtype), compiler_params=_cparams(1))(full)


def _token_tile(cfg):
    return min(cfg.S, 256)


def norm_fwd(cfg, x, w, name):
    ts = _token_tile(cfg)
    d = x.shape[1]

    def fn(ins, _):
        xv, wv = ins
        return [rms(xv, wv)], [], []

    return seq_call(name, fn, (1, cfg.S // ts),
                    [(x, (ts, d), lambda g, t: (t, 0)), (w, (1, d), lambda g, t: (0, 0))],
                    [((cfg.S, d), BF16, (ts, d), lambda g, t: (t, 0))])[0]


def norm_bwd(cfg, x, w, dh, dres, name):
    ts = _token_tile(cfg)
    d = x.shape[1]

    def fn(ins, _):
        xv, wv, dhv, drv = ins
        _, vjp = jax.vjp(rms, xv, wv)
        dx, dw = vjp(dhv.astype(F32))
        return [dx + drv], [dw], []

    row = lambda g, t: (t, 0)
    out = seq_call(name, fn, (1, cfg.S // ts),
                   [(x, (ts, d), row), (w, (1, d), lambda g, t: (0, 0)), (dh, (ts, d), row), (dres, (ts, d), row)],
                   [((cfg.S, d), F32, (ts, d), row)],
                   accs=[((1, d), (1, d), lambda g, t: (0, 0))])
    return out[0], out[1]


def head_fwd_bwd(cfg, x, w, target, name):
    ts = _token_tile(cfg)
    d = x.shape[1]

    def fn(ins, _):
        xv, wv, tv = ins
        y, vjp = jax.vjp(rms, xv, wv)
        err = y - tv
        loss = 0.5 * jnp.sum(err * err) / d
        dx, dw = vjp(err / d)
        return [dx], [jnp.full((SUBLANES, LANES), loss, F32), dw], []

    row = lambda g, t: (t, 0)
    fixed = lambda g, t: (0, 0)
    dx, loss, dw = seq_call(name, fn, (1, cfg.S // ts),
                            [(x, (ts, d), row), (w, (1, d), fixed), (target, (ts, d), row)],
                            [((cfg.S, d), F32, (ts, d), row)],
                            accs=[((SUBLANES, LANES), (SUBLANES, LANES), fixed), ((1, d), (1, d), fixed)])
    return loss, dx, dw


FFN_ROWS = 1024


def ffn_gate_up(h, w_gate, w_up, name):
    s, d = h.shape
    c = w_gate.shape[1]
    tm = _tile(s, FFN_ROWS)

    def body(h_ref, wg_ref, wu_ref, g_ref, u_ref, a_ref):
        hv = h_ref[...]
        g = _dg(hv, wg_ref[...], 1, 1)
        u = _dg(hv, wu_ref[...], 1, 1)
        g_ref[...] = g
        u_ref[...] = u
        a_ref[...] = (silu(g) * u).astype(a_ref.dtype)

    w_spec = pl.BlockSpec((None, c, d), lambda i, j: (j, 0, 0))
    o_spec = pl.BlockSpec((None, tm, c), lambda i, j: (j, i, 0))
    shp = (N_DEV, s, c)
    return pl.pallas_call(
        body, name=name, grid=(s // tm, N_DEV), in_specs=[pl.BlockSpec((tm, d), lambda i, j: (i, 0)), w_spec, w_spec],
        out_specs=[o_spec] * 3,
        out_shape=[jax.ShapeDtypeStruct(shp, F32), jax.ShapeDtypeStruct(shp, F32), jax.ShapeDtypeStruct(shp, BF16)],
        compiler_params=_cparams(2))(h, w_gate, w_up)


def ffn_dgate_dup(dx, w_down, gate, up, after, name):
    s, d = dx.shape
    c = w_down.shape[1]
    tm = _tile(s, FFN_ROWS)
    extra = [] if after is None else [after]

    def body(dx_ref, wd_ref, g_ref, u_ref, *rest):
        dg_ref, du_ref = rest[len(extra):]
        dact = _dg(dx_ref[...], wd_ref[...], 1, 1)
        _, vjp = jax.vjp(lambda a, b: silu(a) * b, g_ref[...], u_ref[...])
        dg, du = vjp(dact)
        dg_ref[...] = dg.astype(dg_ref.dtype)
        du_ref[...] = du.astype(du_ref.dtype)

    blk = pl.BlockSpec((None, tm, c), lambda i, j: (j, i, 0))
    shp = jax.ShapeDtypeStruct((N_DEV, s, c), BF16)
    return pl.pallas_call(
        body, name=name, grid=(s // tm, N_DEV),
        in_specs=[pl.BlockSpec((tm, d), lambda i, j: (i, 0)), pl.BlockSpec((None, c, d), lambda i, j: (j, 0, 0)), blk, blk]
        + [pl.BlockSpec(memory_space=pl.ANY)] * len(extra),
        out_specs=[blk, blk], out_shape=[shp, shp], compiler_params=_cparams(2))(dx, w_down, gate, up, *extra)


CONV_COLS = 256
HALO = SUBLANES


def _shift_down(x, halo, j):
    if j == 0:
        return x
    r8 = _rows_iota(halo.shape)
    top = jnp.where(r8 >= j, pltpu.roll(x[:HALO], j, 0), pltpu.roll(halo, j, 0))
    return jnp.concatenate([top, pltpu.roll(x, j, 0)[HALO:]], axis=0)


def _shift_up(x, halo, j):
    if j == 0:
        return x
    n = x.shape[0]
    r8 = _rows_iota(halo.shape)
    bot = jnp.where(r8 < HALO - j, pltpu.roll(x[n - HALO:], HALO - j, 0), pltpu.roll(halo, HALO - j, 0))
    return jnp.concatenate([pltpu.roll(x, n - j, 0)[:n - HALO], bot], axis=0)


def _conv_tile(cfg):
    return min(cfg.S, 512)


def conv_fwd(cfg, src, col0, width, w, b, name):
    tt, cb = _conv_tile(cfg), CONV_COLS
    c0, hb = col0 // cb, tt // HALO
    nt = cfg.S // tt

    def body(x_ref, h_ref, w_ref, b_ref, o_ref):
        t = pl.program_id(1)
        x = x_ref[...]
        halo = jnp.where(t > 0, h_ref[...], 0.0)
        wv = w_ref[...]
        acc = b_ref[...] + wv[CONV_WIDTH - 1:CONV_WIDTH] * x
        for j in range(1, CONV_WIDTH):
            acc = acc + wv[CONV_WIDTH - 1 - j:CONV_WIDTH - j] * _shift_down(x, halo, j)
        o_ref[...] = acc

    return pl.pallas_call(
        body, name=name, grid=(width // cb, nt),
        in_specs=[pl.BlockSpec((tt, cb), lambda c, t: (t, c0 + c)),
                  pl.BlockSpec((HALO, cb), lambda c, t: (jnp.maximum(t * hb - 1, 0), c0 + c)),
                  pl.BlockSpec((CONV_WIDTH, cb), lambda c, t: (0, c)),
                  pl.BlockSpec((1, cb), lambda c, t: (0, c))],
        out_specs=pl.BlockSpec((tt, cb), lambda c, t: (t, c)),
        out_shape=jax.ShapeDtypeStruct((cfg.S, width), F32), compiler_params=_cparams(2),
    )(src, src, w, b)


def conv_bwd(cfg, src, col0, width, w, dy, name):
    tt, cb = _conv_tile(cfg), CONV_COLS
    c0, hb = col0 // cb, tt // HALO
    nt = cfg.S // tt

    def body(x_ref, h_ref, w_ref, dy_ref, dh_ref, dx_ref, dw_ref, db_ref):
        t = pl.program_id(1)

        @pl.when(t == 0)
        def _():
            dw_ref[...] = jnp.zeros_like(dw_ref)
            db_ref[...] = jnp.zeros_like(db_ref)

        x = x_ref[...]
        halo = jnp.where(t > 0, h_ref[...], 0.0)
        dy = dy_ref[...]
        dhalo = jnp.where(t < nt - 1, dh_ref[...], 0.0)
        wv = w_ref[...]
        dx = wv[CONV_WIDTH - 1:CONV_WIDTH] * dy
        rows = [jnp.sum(dy * x, axis=0, keepdims=True)]
        for j in range(1, CONV_WIDTH):
            dx = dx + wv[CONV_WIDTH - 1 - j:CONV_WIDTH - j] * _shift_up(dy, dhalo, j)
            rows.insert(0, jnp.sum(dy * _shift_down(x, halo, j), axis=0, keepdims=True))
        dx_ref[...] = dx.astype(dx_ref.dtype)
        dw_ref[...] += jnp.concatenate(rows, axis=0)
        db_ref[...] += jnp.sum(dy, axis=0, keepdims=True)

    return pl.pallas_call(
        body, name=name, grid=(width // cb, nt),
        in_specs=[pl.BlockSpec((tt, cb), lambda c, t: (t, c0 + c)),
                  pl.BlockSpec((HALO, cb), lambda c, t: (jnp.maximum(t * hb - 1, 0), c0 + c)),
                  pl.BlockSpec((CONV_WIDTH, cb), lambda c, t: (0, c)),
                  pl.BlockSpec((tt, cb), lambda c, t: (t, c)),
                  pl.BlockSpec((HALO, cb), lambda c, t: (jnp.minimum((t + 1) * hb, nt * hb - 1), c))],
        out_specs=[pl.BlockSpec((tt, cb), lambda c, t: (t, c)),
                   pl.BlockSpec((CONV_WIDTH, cb), lambda c, t: (0, c)),
                   pl.BlockSpec((1, cb), lambda c, t: (0, c))],
        out_shape=[jax.ShapeDtypeStruct((cfg.S, width), BF16), jax.ShapeDtypeStruct((CONV_WIDTH, width), F32),
                   jax.ShapeDtypeStruct((1, width), F32)],
        compiler_params=_cparams(2),
    )(src, src, w, dy, dy)


def _gla_core(gh, q, k, v, g, glr, wg, bg, wn, st):
    n = glr.shape[0]
    causal = _lower_tri(n)
    outs, new = [], []
    for h in range(gh):
        log_a = log_sigmoid(bdot(glr, wg[h], 1, 0) + bg[h]) * (1.0 / GATE_NORM)
        bcum = cumsum_rows(log_a)
        b_last, b_mid = _row(bcum, n - 1), _row(bcum, n // 2)
        qs = q[h] * (HEAD_K ** -0.5)
        scores = jnp.where(causal, bdot(qs * jnp.exp(bcum - b_mid), k[h] * jnp.exp(b_mid - bcum), 1, 1), 0.0)
        o = bdot(scores, v[h], 1, 0) + bdot(qs * jnp.exp(bcum), st[h], 1, 1)
        new.append(st[h] * jnp.exp(b_last) + bdot(v[h], k[h] * jnp.exp(b_last - bcum), 0, 0))
        outs.append(rms(o, wn) * silu(g[h]))
    return jnp.concatenate(outs, axis=1), new


def _gla_ins(cfg, p0, wg, bg, wn, tmap):
    gh = cfg.GH
    ins = []
    for h in range(gh):
        ins.append((p0, (CHUNK, HEAD_K), lambda g, t, h=h: (tmap(t), h)))
    for h in range(gh):
        ins.append((p0, (CHUNK, HEAD_K), lambda g, t, h=h: (tmap(t), gh + h)))
    for h in range(gh):
        ins.append((p0, (CHUNK, HEAD_V), lambda g, t, h=h: (tmap(t), gh + h)))
    for h in range(gh):
        ins.append((p0, (CHUNK, HEAD_V), lambda g, t, h=h: (tmap(t), 2 * gh + h)))
    ins.append((p0, (CHUNK, LANES), lambda g, t: (tmap(t), 10 * gh)))
    for h in range(gh):
        ins.append((wg, (LANES, HEAD_K), lambda g, t, h=h: (0, h)))
    for h in range(gh):
        ins.append((bg, (1, HEAD_K), lambda g, t, h=h: (0, h)))
    ins.append((wn, (1, HEAD_V), lambda g, t: (0, 0)))
    return ins


def _gla_unpack(gh, vals):
    q, k, v, g = (vals[i * gh:(i + 1) * gh] for i in range(4))
    glr = vals[4 * gh]
    wg = vals[4 * gh + 1:5 * gh + 1]
    bg = vals[5 * gh + 1:6 * gh + 1]
    wn = vals[6 * gh + 1]
    return q, k, v, g, glr, wg, bg, wn, vals[6 * gh + 2:]


def gla_fwd(cfg, p0, wg, bg, wn, name):
    gh, nc = cfg.GH, cfg.S // CHUNK

    def fn(ins, st):
        q, k, v, g, glr, wgv, bgv, wnv, _ = _gla_unpack(gh, ins)
        out, new = _gla_core(gh, q, k, v, g, glr, wgv, bgv, wnv, st)
        return [out, jnp.stack(st)], [], new

    return seq_call(name, fn, (1, nc), _gla_ins(cfg, p0, wg, bg, wn, lambda t: t),
                    [((cfg.S, cfg.DV), BF16, (CHUNK, cfg.DV), lambda g, t: (t, 0)),
                     ((nc, gh, HEAD_V, HEAD_K), F32, (None, gh, HEAD_V, HEAD_K), lambda g, t: (t, 0, 0, 0))],
                    carries=[(HEAD_V, HEAD_K)] * gh)


def gla_bwd(cfg, p0, wg, bg, wn, states, dout, name):
    gh, nc = cfg.GH, cfg.S // CHUNK
    rev = lambda t: nc - 1 - t

    def fn(ins, dst):
        q, k, v, g, glr, wgv, bgv, wnv, rest = _gla_unpack(gh, ins)
        st_all, do = rest
        st = [st_all[h] for h in range(gh)]
        _, vjp = jax.vjp(functools.partial(_gla_core, gh), q, k, v, g, glr, wgv, bgv, wnv, st)
        dq, dk, dv, dg, dglr, dwg, dbg, dwn, dstate = vjp((do.astype(F32), list(dst)))
        return ([jnp.concatenate(list(dq) + list(dk) + list(dv) + list(dg), axis=1), dglr],
                [jnp.concatenate(dwg, axis=1), jnp.concatenate(dbg, axis=1), dwn], dstate)

    ins = _gla_ins(cfg, p0, wg, bg, wn, rev)
    ins.append((states, (None, gh, HEAD_V, HEAD_K), lambda g, t: (rev(t), 0, 0, 0)))
    ins.append((dout, (CHUNK, cfg.DV), lambda g, t: (rev(t), 0)))
    wide = 2 * cfg.DK + 2 * cfg.DV
    fixed = lambda g, t: (0, 0)
    return seq_call(name, fn, (1, nc), ins,
                    [((cfg.S, wide), BF16, (CHUNK, wide), lambda g, t: (rev(t), 0)),
                     ((cfg.S, LANES), BF16, (CHUNK, LANES), lambda g, t: (rev(t), 0))],
                    accs=[((LANES, cfg.DK), (LANES, cfg.DK), fixed), ((1, cfg.DK), (1, cfg.DK), fixed),
                          ((1, HEAD_V), (1, HEAD_V), fixed)],
                    carries=[(HEAD_V, HEAD_K)] * gh)


def _lru_core(xc, gate, wa, wi, ba, bi, lam, h_in):
    r = jax.nn.sigmoid(bdot(xc, wa, 1, 0) + ba)
    i = jax.nn.sigmoid(bdot(xc, wi, 1, 0) + bi)
    log_a = LRU_C * r * log_sigmoid(lam)
    a = jnp.exp(log_a)
    u = jnp.sqrt(-_expm1(2.0 * log_a)) * (i * xc)
    first = _rows_iota(a.shape) == 0
    h = lin_scan(a, u + jnp.where(first, a * h_in, 0.0))
    return h * gelu_tanh(gate), _row(h, a.shape[0] - 1)


def _lru_tile(cfg):
    return min(cfg.S, 512)


def _lru_ins(cfg, xc, p0, wa, wi, ba, bi, lam, tmap):
    tt, gh = _lru_tile(cfg), cfg.GH
    vec = lambda g, t: (0, g)
    return [(xc, (tt, LRU_BLOCK), lambda g, t: (tmap(t), g)),
            (p0, (tt, LRU_BLOCK), lambda g, t: (tmap(t), 8 * gh + g)),
            (wa, (None, LRU_BLOCK, LRU_BLOCK), lambda g, t: (g, 0, 0)),
            (wi, (None, LRU_BLOCK, LRU_BLOCK), lambda g, t: (g, 0, 0)),
            (ba, (1, LRU_BLOCK), vec), (bi, (1, LRU_BLOCK), vec), (lam, (1, LRU_BLOCK), vec)]


def lru_fwd(cfg, xc, p0, wa, wi, ba, bi, lam, name):
    tt, nb = _lru_tile(cfg), cfg.NB
    nt = cfg.S // tt

    def fn(ins, c):
        out, h_last = _lru_core(*ins, c[0])
        return [out, c[0]], [], [h_last]

    return seq_call(name, fn, (nb, nt), _lru_ins(cfg, xc, p0, wa, wi, ba, bi, lam, lambda t: t),
                    [((cfg.S, cfg.W), BF16, (tt, LRU_BLOCK), lambda g, t: (t, g)),
                     ((nb, nt, 1, LRU_BLOCK), F32, (None, None, 1, LRU_BLOCK), lambda g, t: (g, t, 0, 0))],
                    carries=[(1, LRU_BLOCK)])


def lru_bwd(cfg, xc, p0, wa, wi, ba, bi, lam, states, dout, name):
    tt, nb = _lru_tile(cfg), cfg.NB
    nt = cfg.S // tt
    rev = lambda t: nt - 1 - t

    def fn(ins, c):
        *fwd_ins, h_in, do = ins
        _, vjp = jax.vjp(_lru_core, *fwd_ins, h_in)
        dxc, dgate, dwa, dwi, dba, dbi, dlam, dh = vjp((do.astype(F32), c[0]))
        return [dxc, dgate], [dwa, dwi, dba, dbi, dlam], [dh]

    ins = _lru_ins(cfg, xc, p0, wa, wi, ba, bi, lam, rev)
    ins.append((states, (None, None, 1, LRU_BLOCK), lambda g, t: (g, rev(t), 0, 0)))
    ins.append((dout, (tt, LRU_BLOCK), lambda g, t: (rev(t), g)))
    mat = ((nb, LRU_BLOCK, LRU_BLOCK), (None, LRU_BLOCK, LRU_BLOCK), lambda g, t: (g, 0, 0))
    vec = ((1, cfg.W), (1, LRU_BLOCK), lambda g, t: (0, g))
    return seq_call(name, fn, (nb, nt), ins,
                    [((cfg.S, cfg.W), F32, (tt, LRU_BLOCK), lambda g, t: (rev(t), g)),
                     ((cfg.S, cfg.W), BF16, (tt, LRU_BLOCK), lambda g, t: (rev(t), g))],
                    accs=[mat, mat, vec, vec, vec], carries=[(1, LRU_BLOCK)])


def _ssd_core(xc, bc, cc, z, dt_raw, dt_bias, a_log, d_skip, gn, st):
    n = xc.shape[0]
    x, bm, cm = silu(xc), silu(bc), silu(cc)
    dt = softplus(dt_raw + dt_bias)
    acs = cumsum_rows_mxu(dt * (-jnp.exp(a_log)))
    acs_t = acs.T
    acs_e, dt_e = _expand_heads(acs), _expand_heads(dt)
    last_e = _expand_heads(_row(acs, n - 1))
    causal = _lower_tri(n)
    cb = bdot(cm, bm, 1, 1)
    xdt = x * dt_e
    y_diag = []
    for h, xh in enumerate(_split_heads(xdt)):
        seg = acs[:, h:h + 1] - acs_t[h:h + 1, :]
        decay = jnp.where(causal, jnp.exp(jnp.minimum(seg, 0.0)), 0.0)
        y_diag.append(bdot(cb * decay, xh, 1, 0))
    y = jnp.concatenate(y_diag, axis=1) + bdot(cm, st, 1, 0) * jnp.exp(acs_e)
    new = st * jnp.exp(last_e) + bdot(bm, xdt * jnp.exp(last_e - acs_e), 0, 0)
    y = (y + _expand_heads(d_skip) * x) * silu(z)
    return rms(y, gn), new


SSD_TILED = 5


SSD_FWD_CHUNKS = 4
SSD_BWD_CHUNKS = 1


def _ssd_tile(cfg, chunks):
    return min(cfg.S, chunks * CHUNK)


def _chunk_rows(v, s):
    return v[s * CHUNK:(s + 1) * CHUNK]


def _ssd_ins(cfg, tt, xc, p1, dt_raw, dt_bias, a_log, d_skip, gn, tmap):
    ng = cfg.NG
    vec = lambda g, t: (g, 0, 0)
    return [(xc, (tt, SSD_GW), lambda g, t: (tmap(t), g)),
            (xc, (tt, SSD_N), lambda g, t: (tmap(t), 4 * ng + g)),
            (xc, (tt, SSD_N), lambda g, t: (tmap(t), 5 * ng + g)),
            (p1, (tt, SSD_GW), lambda g, t: (tmap(t), g)),
            (dt_raw, (None, tt, LANES), lambda g, t: (g, tmap(t), 0)),
            (dt_bias, (None, 1, LANES), vec), (a_log, (None, 1, LANES), vec), (d_skip, (None, 1, LANES), vec),
            (gn, (1, SSD_GW), lambda g, t: (0, g))]


def ssd_fwd(cfg, xc, p1, dt_raw, dt_bias, a_log, d_skip, gn, name):
    ng, nc, tt = cfg.NG, cfg.S // CHUNK, _ssd_tile(cfg, SSD_FWD_CHUNKS)
    nsub = tt // CHUNK

    def fn(ins, c):
        tiled, params = ins[:SSD_TILED], ins[SSD_TILED:]
        st, outs, entered = c[0], [], []
        for s in range(nsub):
            entered.append(st)
            out, st = _ssd_core(*[_chunk_rows(v, s) for v in tiled], *params, st)
            outs.append(out)
        return [jnp.concatenate(outs, axis=0), jnp.stack(entered)], [], [st]

    return seq_call(name, fn, (ng, cfg.S // tt), _ssd_ins(cfg, tt, xc, p1, dt_raw, dt_bias, a_log, d_skip, gn, lambda t: t),
                    [((cfg.S, cfg.DI), BF16, (tt, SSD_GW), lambda g, t: (t, g)),
                     ((ng, nc, SSD_N, SSD_GW), F32, (None, nsub, SSD_N, SSD_GW), lambda g, t: (g, t, 0, 0))],
                    carries=[(SSD_N, SSD_GW)])


def ssd_bwd(cfg, xc, p1, dt_raw, dt_bias, a_log, d_skip, gn, states, dout, name):
    ng, tt = cfg.NG, _ssd_tile(cfg, SSD_BWD_CHUNKS)
    nsub, nt = tt // CHUNK, cfg.S // tt
    rev = lambda t: nt - 1 - t

    def fn(ins, c):
        tiled, params = ins[:SSD_TILED], ins[SSD_TILED:SSD_TILED + 4]
        st_all, do = ins[SSD_TILED + 4:]
        dst, pieces, acc = c[0], [None] * nsub, None
        for s in reversed(range(nsub)):
            _, vjp = jax.vjp(_ssd_core, *[_chunk_rows(v, s) for v in tiled], *params, st_all[s])
            grads = vjp((_chunk_rows(do, s).astype(F32), dst))
            pieces[s], dparams, dst = grads[:SSD_TILED], grads[SSD_TILED:SSD_TILED + 4], grads[SSD_TILED + 4]
            acc = dparams if acc is None else [x + y for x, y in zip(acc, dparams)]
        return [jnp.concatenate([p[i] for p in pieces], axis=0) for i in range(SSD_TILED)], list(acc), [dst]

    ins = _ssd_ins(cfg, tt, xc, p1, dt_raw, dt_bias, a_log, d_skip, gn, rev)
    ins.append((states, (None, nsub, SSD_N, SSD_GW), lambda g, t: (g, rev(t), 0, 0)))
    ins.append((dout, (tt, SSD_GW), lambda g, t: (rev(t), g)))
    col = lambda g, t: (rev(t), g)
    vec = ((ng, 1, LANES), (None, 1, LANES), lambda g, t: (g, 0, 0))
    return seq_call(name, fn, (ng, nt), ins,
                    [((cfg.S, cfg.DI), F32, (tt, SSD_GW), col),
                     ((cfg.S, ng * SSD_N), F32, (tt, SSD_N), col),
                     ((cfg.S, ng * SSD_N), F32, (tt, SSD_N), col),
                     ((cfg.S, cfg.DI), BF16, (tt, SSD_GW), col),
                     ((ng, cfg.S, LANES), F32, (None, tt, LANES), lambda g, t: (g, rev(t), 0))],
                    accs=[vec, vec, vec, ((1, cfg.DI), (1, SSD_GW), lambda g, t: (0, g))],
                    carries=[(SSD_N, SSD_GW)])


PACK_ALIGN = SUBLANES * LANES
PACK_ROWS = 256


def pack(arrays):
    pieces = []
    for a in arrays:
        flat = a.reshape(-1).astype(F32)
        pad = _round_up(flat.shape[0], PACK_ALIGN) - flat.shape[0]
        pieces.append(jnp.pad(flat, (0, pad)) if pad else flat)
    flat = jnp.concatenate(pieces)
    pad = _round_up(flat.shape[0], PACK_ROWS * LANES) - flat.shape[0]
    return jnp.pad(flat, (0, pad)).reshape(-1, LANES)


def unpack(buf, shapes):
    lead = buf.shape[:-2]
    flat = buf.reshape(lead + (-1,))
    out, off = [], 0
    for s in shapes:
        n = math.prod(s)
        out.append(flat[..., off:off + n].reshape(lead + tuple(s)))
        off += _round_up(n, PACK_ALIGN)
    return out


def _slots_to_cols(slots):
    return slots.transpose(1, 0, 2).reshape(slots.shape[1], -1)


def _even_in_padded(cfg, w):
    main = 2 * cfg.DK + 2 * cfg.DV
    return jnp.concatenate([w[:, :main], w[:, main + GATE_RANK:], w[:, main:main + GATE_RANK],
                            jnp.zeros((w.shape[0], cfg.EP - cfg.EVEN_IN), w.dtype)], axis=1)


def _even_in_unpadded(cfg, wp):
    main = 2 * cfg.DK + 2 * cfg.DV
    rest = main + 2 * cfg.W
    return jnp.concatenate([wp[:, :main], wp[:, rest:rest + GATE_RANK], wp[:, main:rest]], axis=1)


def _odd_in_padded(cfg, w):
    return jnp.concatenate([w, jnp.zeros((w.shape[0], cfg.OP - cfg.ODD_IN), w.dtype)], axis=1)


def _odd_in_unpadded(cfg, wp):
    return wp[:, :cfg.ODD_IN]


def _group_lanes(cfg, v):
    lead = v.shape[:-1]
    g = jnp.moveaxis(v.reshape(lead + (cfg.NG, SSD_HG)), -2, 0)
    return jnp.pad(g, [(0, 0)] * (g.ndim - 1) + [(0, LANES - SSD_HG)])


def _ungroup_lanes(cfg, g):
    v = jnp.moveaxis(g[..., :SSD_HG], 0, -2)
    return v.reshape(v.shape[:-2] + (cfg.NH,))


def train_step(cfg, p, loss_target):
    S, D = cfg.S, cfg.D
    me = 4 * lax.axis_index("x") + 2 * lax.axis_index("y") + lax.axis_index("c")

    big = ["ev_w_in", "ev_w_out", "od_w_in", "od_w_out", "ffn_w_gate", "ffn_w_up", "ffn_w_down"]
    small_sharded = ["ev_gla_w_gate", "ev_lru_conv_w", "od_norm", "od_conv_w", "od_conv_b", "od_gnorm"]
    replicated = ["ev_norm", "ev_gla_b_gate", "ev_gla_w_onorm", "ev_lru_conv_b", "ev_lru_w_a", "ev_lru_b_a",
                  "ev_lru_w_i", "ev_lru_b_i", "ev_lru_lam", "od_dt_bias", "od_a_log", "od_d_skip", "ffn_norm",
                  "final_norm"]

    transposed = ("ffn_w_gate", "ffn_w_up")
    view = lambda n, a: jnp.swapaxes(a, 1, 2) if n in transposed else a
    wb = {n: view(n, p[n]).astype(BF16) for n in big}
    ffn_items = lambda l: [(wb["ffn_w_gate"], l), (wb["ffn_w_up"], l), (wb["ffn_w_down"], l)]
    ss_shapes = [p[n].shape for n in small_sharded]
    groups = [[(pack([p[n] for n in small_sharded]), None), (wb["ev_w_in"], 0), (wb["ev_w_out"], 0)], ffn_items(0),
              [(wb["od_w_in"], 0), (wb["od_w_out"], 0)], ffn_items(1)]
    gathers, tokens = [], []
    for i, g in enumerate(groups):
        lands = [_landing_zone(a, lead, me, f"gather_own{i}_{j}") for j, (a, lead) in enumerate(g)]
        handle, token = exchange_start(g, True, lands, f"gather_start{i}")
        gathers.append(handle)
        tokens.append(token)
    all_started = tokens[0][:1, :1] + tokens[1][:1, :1] + tokens[2][:1, :1] + tokens[3][:1, :1]

    ss_all, gw_ev_in, gw_ev_out = exchange_wait(gathers[0], all_started, "gather_wait0")[1]
    gs = dict(zip(small_sharded, unpack(ss_all, ss_shapes)))
    w_ev_in = cols_from_slots(gw_ev_in, functools.partial(_even_in_padded, cfg), cfg.EP, "ev_w_in_cols")
    w_ev_out = gw_ev_out.reshape(D, D)

    gla_wg = jnp.pad(_slots_to_cols(gs["ev_gla_w_gate"][:, 0]), ((0, LANES - GATE_RANK), (0, 0)))
    lru_cw = _slots_to_cols(gs["ev_lru_conv_w"][:, 0])
    od_norm = gs["od_norm"].transpose(1, 0, 2).reshape(1, D)
    od_cw = _slots_to_cols(gs["od_conv_w"][:, 0])
    od_cb = gs["od_conv_b"].transpose(1, 0, 2).reshape(1, cfg.CD)
    od_gn = gs["od_gnorm"].transpose(1, 0, 2).reshape(1, cfg.DI)

    x0 = p["x"][0]
    target = loss_target[0]
    ev_norm = p["ev_norm"] + all_started
    bg = p["ev_gla_b_gate"]
    wn = p["ev_gla_w_onorm"]
    lru_cb = p["ev_lru_conv_b"]
    wa, wi = p["ev_lru_w_a"][0], p["ev_lru_w_i"][0]
    ba, bi, lam = p["ev_lru_b_a"], p["ev_lru_b_i"], p["ev_lru_lam"]
    dt_bias, a_log, d_skip = (_group_lanes(cfg, p[n]) for n in ("od_dt_bias", "od_a_log", "od_d_skip"))
    ffn_norm = [p["ffn_norm"][l:l + 1] for l in range(2)]
    final_norm = p["final_norm"].reshape(1, D)

    def ffn_forward(l, x):
        w_gate, w_up, w_down = exchange_wait(gathers[1 + 2 * l], x, f"gather_wait{1 + 2 * l}")[1]
        h = norm_fwd(cfg, x, ffn_norm[l], f"ffn{l}_norm")
        gate, up, act = ffn_gate_up(h, w_gate, w_up, f"ffn{l}_gate_up")
        out = matmul(act, w_down, a_slot="k", b_slot="k", res=x, name=f"ffn{l}_down", tn=512)
        return out, (h, gate, up, act, w_gate, w_up, w_down)

    h0 = norm_fwd(cfg, x0, ev_norm, "ev_norm")
    p0 = matmul(h0, w_ev_in, name="ev_in", tn=768)
    gla_out, gla_states = gla_fwd(cfg, p0, gla_wg, bg, wn, "gla_fwd")
    lru_col = 2 * cfg.DK + 2 * cfg.DV
    lru_xc = conv_fwd(cfg, p0, lru_col, cfg.W, lru_cw, lru_cb, "lru_conv")
    lru_out, lru_states = lru_fwd(cfg, lru_xc, p0, wa, wi, ba, bi, lam, "lru_fwd")
    mix = jnp.concatenate([gla_out, lru_out], axis=1)
    x1 = matmul(mix, w_ev_out, res=x0, name="ev_out")
    x2, ffn0_saved = ffn_forward(0, x1)

    gw_od_in, gw_od_out = exchange_wait(gathers[2], x2, "gather_wait2")[1]
    w_od_in = cols_from_slots(gw_od_in, functools.partial(_odd_in_padded, cfg), cfg.OP, "od_w_in_cols")
    w_od_out = gw_od_out.reshape(cfg.DI, D)
    h2 = norm_fwd(cfg, x2, od_norm, "od_norm")
    p1 = matmul(h2, w_od_in, name="od_in", tn=768)
    od_xc = conv_fwd(cfg, p1, cfg.DI, cfg.CD, od_cw, od_cb, "od_conv")
    dt_col = cfg.DI + cfg.CD
    dt_raw = _group_lanes(cfg, p1[:, dt_col:dt_col + cfg.NH])
    ssd_out, ssd_states = ssd_fwd(cfg, od_xc, p1, dt_raw, dt_bias, a_log, d_skip, od_gn, "ssd_fwd")
    x3 = matmul(ssd_out, w_od_out, res=x2, name="od_out")
    x4, ffn1_saved = ffn_forward(1, x3)

    loss_part, dx4, d_final_norm = head_fwd_bwd(cfg, x4, final_norm, target, "head")
    loss = lax.psum(loss_part[0, 0], MESH_AXES)

    def ffn_backward(l, x, saved, dx_out, after):
        h, gate, up, act, w_gate, w_up, w_down = saved
        dgate, dup = ffn_dgate_dup(dx_out, w_down, gate, up, after, f"ffn{l}_dgate_dup")
        d_down = matmul(act, dx_out, ta=True, a_slot="m", out_dtype=BF16, name=f"ffn{l}_dwdown")
        dh = matmul(dgate, w_gate, a_slot="k", b_slot="k", name=f"ffn{l}_dh_gate", tn=512)
        dh = matmul(dup, w_up, a_slot="k", b_slot="k", res=dh, name=f"ffn{l}_dh_up", tn=512)
        d_gate = matmul(dgate, h, ta=True, a_slot="m", out_dtype=BF16, name=f"ffn{l}_dwgate")
        d_up = matmul(dup, h, ta=True, a_slot="m", out_dtype=BF16, name=f"ffn{l}_dwup")
        dx, dnorm = norm_bwd(cfg, x, ffn_norm[l], dh, dx_out, f"ffn{l}_norm_bwd")
        sent, token = start_grads([d_gate, d_up, d_down], f"grads_start_ffn{l}")
        return dx, dnorm, sent, token

    def start_grads(arrays, name):
        return exchange_start([(a, None) for a in arrays], False, [lax.empty(a.shape, a.dtype) for a in arrays], name)

    dx3, d_ffn_norm1, sent_ffn1, token = ffn_backward(1, x3, ffn1_saved, dx4, None)

    d_ssd_out = matmul(dx3, w_od_out, tb=True, after=token, name="od_dmix")
    d_od_out = matmul(ssd_out, dx3, ta=True, out_dtype=BF16, name="od_dwout")
    dxs, dbm, dcm, dz, d_dt_raw, d_dt_bias, d_a_log, d_d_skip, d_od_gn = ssd_bwd(
        cfg, od_xc, p1, dt_raw, dt_bias, a_log, d_skip, od_gn, ssd_states, d_ssd_out, "ssd_bwd")
    conv_parts, col = [], 0
    for part, dy in (("x", dxs), ("b", dbm), ("c", dcm)):
        width = dy.shape[1]
        conv_parts.append(conv_bwd(cfg, p1, cfg.DI + col, width, od_cw[:, col:col + width], dy, "od_conv_bwd_" + part))
        col += width
    d_od_cw = jnp.concatenate([c[1] for c in conv_parts], axis=1)
    d_od_cb = jnp.concatenate([c[2] for c in conv_parts], axis=1)
    d_dt = _ungroup_lanes(cfg, d_dt_raw).astype(BF16)
    dp1 = jnp.concatenate([dz] + [c[0] for c in conv_parts] + [d_dt, jnp.zeros((S, cfg.OP - cfg.ODD_IN), BF16)], axis=1)
    dh2 = matmul(dp1, w_od_in, tb=True, name="od_dh", tk=1536)
    d_od_in = matmul(h2, dp1, ta=True, out_dtype=BF16, name="od_dwin", tn=768)
    dx2, d_od_norm = norm_bwd(cfg, x2, od_norm, dh2, dx3, "od_norm_bwd")
    d_od_in_slots = slots_from_cols(d_od_in, functools.partial(_odd_in_unpadded, cfg), p["od_w_in"].shape[2],
                                    "od_dwin_slots")
    sent_od, token = start_grads([d_od_in_slots, d_od_out.reshape((N_DEV,) + p["od_w_out"].shape[1:])], "grads_start_od")

    dx1, d_ffn_norm0, sent_ffn0, token = ffn_backward(0, x1, ffn0_saved, dx2, token)

    d_mix = matmul(dx1, w_ev_out, tb=True, after=token, name="ev_dmix")
    d_ev_out = matmul(mix, dx1, ta=True, out_dtype=BF16, name="ev_dwout")
    d_qkvg, d_glr, d_gla_wg, d_bg, d_wn = gla_bwd(cfg, p0, gla_wg, bg, wn, gla_states, d_mix[:, :cfg.DV], "gla_bwd")
    d_lru_xc, d_gate_br, d_wa, d_wi, d_ba, d_bi, d_lam = lru_bwd(
        cfg, lru_xc, p0, wa, wi, ba, bi, lam, lru_states, d_mix[:, cfg.DV:], "lru_bwd")
    d_xbr, d_lru_cw, d_lru_cb = conv_bwd(cfg, p0, lru_col, cfg.W, lru_cw, d_lru_xc, "lru_conv_bwd")
    dp0 = jnp.concatenate([d_qkvg, d_xbr, d_gate_br, d_glr, jnp.zeros((S, cfg.EP - lru_col - 2 * cfg.W - LANES), BF16)],
                          axis=1)
    d_ev_in = matmul(h0, dp0, ta=True, out_dtype=BF16, name="ev_dwin", tn=768)
    d_ev_in_slots = slots_from_cols(d_ev_in, functools.partial(_even_in_unpadded, cfg), p["ev_w_in"].shape[2],
                                    "ev_dwin_slots")
    sent_ev, token = start_grads([d_ev_in_slots, d_ev_out.reshape((N_DEV,) + p["ev_w_out"].shape[1:])], "grads_start_ev")
    dh0 = matmul(dp0, w_ev_in, tb=True, after=token, name="ev_dh", tk=1792)
    grad_x, d_ev_norm = norm_bwd(cfg, x0, ev_norm, dh0, dx1, "ev_norm_bwd")

    out = {"loss": loss, "grad_x": grad_x[None]}

    def update(names, sent, after, wait_name):
        s, r = exchange_wait(sent[0], after, wait_name + "0")
        sends, recvs = [[a] for a in s], [[a] for a in r]
        for extra in sent[1:]:
            s, r = exchange_wait(extra, after, wait_name + "1")
            for i in range(len(names)):
                sends[i].append(s[i])
                recvs[i].append(r[i])
        for i, n in enumerate(names):
            res = adamw_sharded(recvs[i], sends[i], me, view(n, p[n]), view(n, p["m_" + n]), view(n, p["v_" + n]),
                                "adamw_" + n)
            out["grad_" + n], out["delta_" + n], out["new_m_" + n], out["new_v_" + n] = (view(n, a) for a in res)
        return res[-1]

    small_full = {
        "ev_gla_w_gate": d_gla_wg[:GATE_RANK][None], "ev_lru_conv_w": d_lru_cw[None], "od_norm": d_od_norm,
        "od_conv_w": d_od_cw[None], "od_conv_b": d_od_cb, "od_gnorm": d_od_gn,
        "ev_norm": d_ev_norm, "ev_gla_b_gate": d_bg, "ev_gla_w_onorm": d_wn, "ev_lru_conv_b": d_lru_cb,
        "ev_lru_w_a": d_wa[None], "ev_lru_b_a": d_ba, "ev_lru_w_i": d_wi[None], "ev_lru_b_i": d_bi,
        "ev_lru_lam": d_lam, "od_dt_bias": _ungroup_lanes(cfg, d_dt_bias), "od_a_log": _ungroup_lanes(cfg, d_a_log),
        "od_d_skip": _ungroup_lanes(cfg, d_d_skip), "ffn_norm": jnp.concatenate([d_ffn_norm0, d_ffn_norm1], axis=0),
        "final_norm": d_final_norm.reshape(D),
    }
    small = small_sharded + replicated
    small_packed = pack([small_full[n] for n in small])
    sent_small, token = exchange_start([(small_packed, None)], True,
                                       [_landing_zone(small_packed, None, me, "gather_small_grads_own")],
                                       "gather_small_grads")

    done = update(["od_w_in", "od_w_out"], [sent_od], token, "grads_wait_od")
    done = update(["ffn_w_gate", "ffn_w_up", "ffn_w_down"], [sent_ffn0, sent_ffn1], done, "grads_wait_ffn")
    done = update(["ev_w_in", "ev_w_out"], [sent_ev], done, "grads_wait_ev")

    small_all = exchange_wait(sent_small, done, "gather_small_grads_wait")[1][0]
    g_small = dict(zip(small, unpack(reduce_slots(small_all, "sum_small_grads"), [small_full[n].shape for n in small])))
    for n in small_sharded:
        width = p[n].shape[-1]
        g_small[n] = lax.dynamic_slice_in_dim(g_small[n], me * width, width, axis=g_small[n].ndim - 1)
    shapes = [p[n].shape for n in small]
    g_buf = pack([g_small[n] for n in small])
    delta, new_m, new_v = adamw(pack([p[n] for n in small]), g_buf, pack([p["m_" + n] for n in small]),
                                pack([p["v_" + n] for n in small]), "adamw_small")
    for kind, buf in (("grad_", g_buf), ("delta_", delta), ("new_m_", new_m), ("new_v_", new_v)):
        for n, a in zip(small, unpack(buf, shapes)):
            out[kind + n] = a
    return out


WEIGHTS = ['ev_norm', 'ev_w_in', 'ev_gla_w_gate', 'ev_gla_b_gate', 'ev_gla_w_onorm', 'ev_lru_conv_w', 'ev_lru_conv_b',
           'ev_lru_w_a', 'ev_lru_b_a', 'ev_lru_w_i', 'ev_lru_b_i', 'ev_lru_lam', 'ev_w_out', 'od_norm', 'od_w_in',
           'od_conv_w', 'od_conv_b', 'od_dt_bias', 'od_a_log', 'od_d_skip', 'od_gnorm', 'od_w_out', 'ffn_norm',
           'ffn_w_gate', 'ffn_w_up', 'ffn_w_down', 'final_norm']


def kernel(x, ev_norm, ev_w_in, ev_gla_w_gate, ev_gla_b_gate, ev_gla_w_onorm, ev_lru_conv_w, ev_lru_conv_b, ev_lru_w_a, ev_lru_b_a, ev_lru_w_i, ev_lru_b_i, ev_lru_lam, ev_w_out, od_norm, od_w_in, od_conv_w, od_conv_b, od_dt_bias, od_a_log, od_d_skip, od_gnorm, od_w_out, ffn_norm, ffn_w_gate, ffn_w_up, ffn_w_down, final_norm, loss_target, m_ev_norm, m_ev_w_in, m_ev_gla_w_gate, m_ev_gla_b_gate, m_ev_gla_w_onorm, m_ev_lru_conv_w, m_ev_lru_conv_b, m_ev_lru_w_a, m_ev_lru_b_a, m_ev_lru_w_i, m_ev_lru_b_i, m_ev_lru_lam, m_ev_w_out, m_od_norm, m_od_w_in, m_od_conv_w, m_od_conv_b, m_od_dt_bias, m_od_a_log, m_od_d_skip, m_od_gnorm, m_od_w_out, m_ffn_norm, m_ffn_w_gate, m_ffn_w_up, m_ffn_w_down, m_final_norm, v_ev_norm, v_ev_w_in, v_ev_gla_w_gate, v_ev_gla_b_gate, v_ev_gla_w_onorm, v_ev_lru_conv_w, v_ev_lru_conv_b, v_ev_lru_w_a, v_ev_lru_b_a, v_ev_lru_w_i, v_ev_lru_b_i, v_ev_lru_lam, v_ev_w_out, v_od_norm, v_od_w_in, v_od_conv_w, v_od_conv_b, v_od_dt_bias, v_od_a_log, v_od_d_skip, v_od_gnorm, v_od_w_out, v_ffn_norm, v_ffn_w_gate, v_ffn_w_up, v_ffn_w_down, v_final_norm):
    args = dict(locals())
    p = {n: a for n, a in args.items() if n != "loss_target"}
    cfg = Cfg(S=x.shape[1], D=x.shape[2], DFF=ffn_w_gate.shape[2] * N_DEV)
    out = train_step(cfg, p, loss_target)
    return (out["loss"], out["grad_x"], *[out["grad_" + w] for w in WEIGHTS], *[out["delta_" + w] for w in WEIGHTS],
            *[out["new_m_" + w] for w in WEIGHTS], *[out["new_v_" + w] for w in WEIGHTS])
```

```python
import functools
import math
from typing import NamedTuple

import jax
import jax.numpy as jnp
from jax import lax
from jax.experimental import pallas as pl
from jax.experimental.pallas import tpu as pltpu

F32 = jnp.float32
BF16 = jnp.bfloat16
MESH_AXES = ("x", "y", "c")
N_DEV = 8
LANES = 128
SUBLANES = 8
VMEM_LIMIT = 56 * 1024 * 1024

NORM_EPS = 1e-6
CONV_WIDTH = 4
CHUNK = 64
HEAD_K = 128
HEAD_V = 256
GATE_RANK = 16
GATE_NORM = 16.0
LRU_BLOCK = 128
LRU_C = 8.0
SSD_P = 64
SSD_N = 128
SSD_HG = 8
SSD_GW = SSD_HG * SSD_P

ADAM_LR = 0.001
ADAM_B1 = 0.9
ADAM_B2 = 0.999
ADAM_EPS = 1e-08
ADAM_WD = 0.01
ADAM_STEP = 10


class Cfg(NamedTuple):
    S: int
    D: int
    DFF: int

    @property
    def GH(self):
        return self.D // 512

    @property
    def NB(self):
        return self.D // 256

    @property
    def NG(self):
        return self.D // 256

    @property
    def DK(self):
        return HEAD_K * self.GH

    @property
    def DV(self):
        return HEAD_V * self.GH

    @property
    def W(self):
        return LRU_BLOCK * self.NB

    @property
    def DI(self):
        return SSD_GW * self.NG

    @property
    def CD(self):
        return self.DI + 2 * self.NG * SSD_N

    @property
    def NH(self):
        return SSD_HG * self.NG

    @property
    def EVEN_IN(self):
        return 2 * self.DK + 2 * self.DV + GATE_RANK + 2 * self.W

    @property
    def ODD_IN(self):
        return self.DI + self.CD + self.NH

    @property
    def EP(self):
        return _round_up(2 * self.DK + 2 * self.DV + 2 * self.W + LANES, 768)

    @property
    def OP(self):
        return _round_up(self.DI + self.CD + LANES, 768)


def _round_up(n, m):
    return (n + m - 1) // m * m


def _tile(n, pref):
    if n <= pref:
        return n
    t = pref - pref % LANES
    while n % t:
        t -= LANES
    return t


def _cparams(n_axes):
    return pltpu.CompilerParams(dimension_semantics=("arbitrary",) * n_axes, vmem_limit_bytes=VMEM_LIMIT)


def _dg(a, b, ca, cb):
    return lax.dot_general(a.astype(BF16), b.astype(BF16), (((ca,), (cb,)), ((), ())), preferred_element_type=F32)


@functools.partial(jax.custom_vjp, nondiff_argnums=(2, 3))
def bdot(a, b, ca, cb):
    return _dg(a, b, ca, cb)


def _bdot_fwd(a, b, ca, cb):
    return _dg(a, b, ca, cb), (a, b)


def _bdot_bwd(ca, cb, res, g):
    a, b = res
    da = _dg(g, b, 1, 1 - cb) if ca == 1 else _dg(b, g, 1 - cb, 1)
    db = _dg(a, g, 1 - ca, 0) if cb == 0 else _dg(g, a, 0, 1 - ca)
    return da.astype(a.dtype), db.astype(b.dtype)


bdot.defvjp(_bdot_fwd, _bdot_bwd)


def _lower_tri(n):
    r = lax.broadcasted_iota(jnp.int32, (n, n), 0)
    c = lax.broadcasted_iota(jnp.int32, (n, n), 1)
    return c <= r


def _running_sum(x, reverse):
    n = x.shape[0]
    r = lax.broadcasted_iota(jnp.int32, x.shape, 0)
    d = 1
    while d < n:
        if reverse:
            x = x + jnp.where(r < n - d, pltpu.roll(x, n - d, 0), 0.0)
        else:
            x = x + jnp.where(r >= d, pltpu.roll(x, d, 0), 0.0)
        d *= 2
    return x


@jax.custom_vjp
def cumsum_rows(x):
    return _running_sum(x, False)


cumsum_rows.defvjp(lambda x: (_running_sum(x, False), None), lambda _, g: (_running_sum(g, True),))


def _tri_dot(x, transposed):
    n = x.shape[0]
    return lax.dot_general(_lower_tri(n).astype(F32), x, (((0 if transposed else 1,), (0,)), ((), ())),
                           precision=lax.Precision.HIGHEST, preferred_element_type=F32)


@jax.custom_vjp
def cumsum_rows_mxu(x):
    return _tri_dot(x, False)


cumsum_rows_mxu.defvjp(lambda x: (_tri_dot(x, False), None), lambda _, g: (_tri_dot(g, True),))


def _row(x, i):
    r = lax.broadcasted_iota(jnp.int32, x.shape, 0)
    return jnp.sum(jnp.where(r == i, x, 0.0), axis=0, keepdims=True)


def _softplus_raw(x):
    return jnp.maximum(x, 0.0) + jnp.log(1.0 + jnp.exp(-jnp.abs(x)))


@jax.custom_vjp
def softplus(x):
    return _softplus_raw(x)


softplus.defvjp(lambda x: (_softplus_raw(x), x), lambda x, g: (g * jax.nn.sigmoid(x),))


@jax.custom_vjp
def log_sigmoid(x):
    return -_softplus_raw(-x)


log_sigmoid.defvjp(lambda x: (-_softplus_raw(-x), x), lambda x, g: (g * jax.nn.sigmoid(-x),))


def silu(x):
    return x * jax.nn.sigmoid(x)


def gelu_tanh(x):
    return 0.5 * x * (1.0 + jnp.tanh(math.sqrt(2.0 / math.pi) * (x + 0.044715 * (x * x * x))))


def _expm1(x):
    series = x * (1.0 + 0.5 * x * (1.0 + (1.0 / 3.0) * x))
    return jnp.where(jnp.abs(x) < 1e-2, series, jnp.exp(x) - 1.0)


def rms(x, w):
    return x * lax.rsqrt(jnp.mean(x * x, axis=-1, keepdims=True) + NORM_EPS) * w


def _rows_iota(shape):
    return lax.broadcasted_iota(jnp.int32, shape, 0)


def _scan_up(a, u):
    n = a.shape[0]
    r = _rows_iota(a.shape)
    d = 1
    while d < n:
        m = r >= d
        a_s = jnp.where(m, pltpu.roll(a, d, 0), 1.0)
        u_s = jnp.where(m, pltpu.roll(u, d, 0), 0.0)
        u = a * u_s + u
        a = a * a_s
        d *= 2
    return u


def _scan_down(a, u):
    n = a.shape[0]
    r = _rows_iota(a.shape)
    d = 1
    while d < n:
        m = r < n - d
        a_s = jnp.where(m, pltpu.roll(a, n - d, 0), 1.0)
        u_s = jnp.where(m, pltpu.roll(u, n - d, 0), 0.0)
        u = a * u_s + u
        a = a * a_s
        d *= 2
    return u


@jax.custom_vjp
def lin_scan(a, u):
    return _scan_up(a, u)


def _lin_scan_fwd(a, u):
    h = _scan_up(a, u)
    return h, (a, h)


def _lin_scan_bwd(res, g):
    a, h = res
    n = a.shape[0]
    r = _rows_iota(a.shape)
    a_next = jnp.where(r < n - 1, pltpu.roll(a, n - 1, 0), 0.0)
    gt = _scan_down(a_next, g)
    h_prev = jnp.where(r >= 1, pltpu.roll(h, 1, 0), 0.0)
    return gt * h_prev, gt


lin_scan.defvjp(_lin_scan_fwd, _lin_scan_bwd)


def _expand_heads(v):
    r = v.shape[0]
    return jnp.concatenate([jnp.broadcast_to(v[:, h:h + 1], (r, SSD_P)) for h in range(SSD_HG)], axis=1)


@jax.custom_vjp
def _split_heads(x):
    return tuple(x[:, h * SSD_P:(h + 1) * SSD_P] for h in range(SSD_HG))


_split_heads.defvjp(lambda x: (_split_heads(x), None), lambda _, gs: (jnp.concatenate(gs, axis=1),))


def matmul(a, b, *, ta=False, tb=False, a_slot=None, b_slot=None, b_lead=(), res=None, after=None, out_dtype=F32,
           name, tm=1024, tn=1024, tk=2048):
    lead = tuple(b_lead)
    ra, ca_ = a.shape[-2:]
    rb, cb_ = b.shape[-2:]
    m_st, ka_st = (ca_, ra) if ta else (ra, ca_)
    kb_st, n_st = (cb_, rb) if tb else (rb, cb_)
    kslot = a_slot == "k"
    assert kslot == (b_slot == "k")
    m = m_st * (N_DEV if a_slot == "m" else 1)
    n = n_st * (N_DEV if b_slot == "n" else 1)
    assert ka_st == kb_st, (a.shape, b.shape, ta, tb)
    tm = m_st if a_slot == "m" else _tile(m, tm)
    tn = n_st if b_slot == "n" else _tile(n, tn)
    tk = ka_st if kslot else _tile(ka_st, tk)
    nk = ka_st // tk
    ca, cb = (0 if ta else 1), (1 if tb else 0)
    nl = (None,) * len(lead)

    if a_slot is None:
        a_spec = pl.BlockSpec((tk, tm), lambda i, j, k: (k, i)) if ta else pl.BlockSpec((tm, tk), lambda i, j, k: (i, k))
    elif a_slot == "m":
        a_spec = (pl.BlockSpec((None, tk, tm), lambda i, j, k: (i, k, 0)) if ta
                  else pl.BlockSpec((None, tm, tk), lambda i, j, k: (i, 0, k)))
    else:
        a_spec = (pl.BlockSpec((N_DEV, tk, tm), lambda i, j, k: (0, 0, i)) if ta
                  else pl.BlockSpec((N_DEV, tm, tk), lambda i, j, k: (0, i, 0)))
    if b_slot is None:
        b_spec = (pl.BlockSpec(nl + (tn, tk), lambda i, j, k: lead + (j, k)) if tb
                  else pl.BlockSpec(nl + (tk, tn), lambda i, j, k: lead + (k, j)))
    elif b_slot == "n":
        b_spec = (pl.BlockSpec((None,) + nl + (tn, tk), lambda i, j, k: (j,) + lead + (0, k)) if tb
                  else pl.BlockSpec((None,) + nl + (tk, tn), lambda i, j, k: (j,) + lead + (k, 0)))
    else:
        b_spec = (pl.BlockSpec((N_DEV,) + nl + (tn, tk), lambda i, j, k: (0,) + lead + (j, 0)) if tb
                  else pl.BlockSpec((N_DEV,) + nl + (tk, tn), lambda i, j, k: (0,) + lead + (0, j)))
    if a_slot == "m":
        o_spec, o_shape = pl.BlockSpec((None, tm, tn), lambda i, j, k: (i, 0, j)), (N_DEV, tm, n)
    elif b_slot == "n":
        o_spec, o_shape = pl.BlockSpec((None, tm, tn), lambda i, j, k: (j, i, 0)), (N_DEV, m, tn)
    else:
        o_spec, o_shape = pl.BlockSpec((tm, tn), lambda i, j, k: (i, j)), (m, n)
    assert res is None or (a_slot != "m" and b_slot != "n")

    def dot(x, y):
        return lax.dot_general(x.astype(BF16), y.astype(BF16), (((ca,), (cb,)), ((), ())), preferred_element_type=F32)

    def body(*refs):
        a_ref, b_ref = refs[:2]
        r_ref = refs[2] if res is not None else None
        o_ref = refs[2 + (res is not None) + (after is not None)]

        def finish(acc):
            if r_ref is not None:
                acc = acc + r_ref[...].astype(F32)
            o_ref[...] = acc.astype(o_ref.dtype)

        if kslot:
            acc = dot(a_ref[0], b_ref[0])
            for s in range(1, N_DEV):
                acc = acc + dot(a_ref[s], b_ref[s])
            finish(acc)
        elif nk == 1:
            finish(dot(a_ref[...], b_ref[...]))
        else:
            acc_ref = refs[-1]
            k = pl.program_id(2)

            @pl.when(k == 0)
            def _():
                acc_ref[...] = dot(a_ref[...], b_ref[...])

            @pl.when(k > 0)
            def _():
                acc_ref[...] += dot(a_ref[...], b_ref[...])

            @pl.when(k == nk - 1)
            def _():
                finish(acc_ref[...])

    in_specs = [a_spec, b_spec]
    args = [a, b]
    if res is not None:
        in_specs.append(pl.BlockSpec((tm, tn), lambda i, j, k: (i, j)))
        args.append(res)
    if after is not None:
        in_specs.append(pl.BlockSpec(memory_space=pl.ANY))
        args.append(after)
    return pl.pallas_call(
        body, name=name, grid=(m // tm, n // tn, nk), in_specs=in_specs, out_specs=o_spec,
        out_shape=jax.ShapeDtypeStruct(o_shape, out_dtype),
        scratch_shapes=[pltpu.VMEM((tm, tn), F32)] if nk > 1 else [], compiler_params=_cparams(3),
    )(*args)


def seq_call(name, fn, grid, ins, outs, accs=(), carries=()):
    n_in, n_out, n_acc = len(ins), len(outs), len(accs)

    def body(*refs):
        in_refs = refs[:n_in]
        out_refs = refs[n_in:n_in + n_out]
        acc_refs = refs[n_in + n_out:n_in + n_out + n_acc]
        c_refs = refs[n_in + n_out + n_acc:]

        if acc_refs or c_refs:
            @pl.when(pl.program_id(1) == 0)
            def _():
                for r in tuple(acc_refs) + tuple(c_refs):
                    r[...] = jnp.zeros_like(r)

        o, a, c = fn([r[...] for r in in_refs], [r[...] for r in c_refs])
        for r, v in zip(out_refs, o, strict=True):
            r[...] = v.astype(r.dtype)
        for r, v in zip(acc_refs, a, strict=True):
            r[...] += v
        for r, v in zip(c_refs, c, strict=True):
            r[...] = v

    return pl.pallas_call(
        body, name=name, grid=grid,
        in_specs=[pl.BlockSpec(blk, im) for _, blk, im in ins],
        out_specs=[pl.BlockSpec(blk, im) for _, _, blk, im in outs] + [pl.BlockSpec(blk, im) for _, blk, im in accs],
        out_shape=[jax.ShapeDtypeStruct(s, d) for s, d, _, _ in outs] + [jax.ShapeDtypeStruct(s, F32) for s, _, _ in accs],
        scratch_shapes=[pltpu.VMEM(s, F32) for s in carries], compiler_params=_cparams(2),
    )(*[a for a, _, _ in ins])


def exchange(arrays, gather, name, after=None):
    n = len(arrays)
    extra = [] if after is None else [after]

    def body(*refs):
        x_refs, o_refs = refs[:n], refs[n + len(extra):2 * n + len(extra)]
        send_sems, recv_sems, local_sems = refs[2 * n + len(extra):]
        pos = [lax.axis_index(ax) for ax in MESH_AXES]
        me = 4 * pos[0] + 2 * pos[1] + pos[2]
        copies = []
        for i in range(n):
            own = pltpu.make_async_copy(x_refs[i] if gather else x_refs[i].at[me], o_refs[i].at[me], local_sems.at[i])
            own.start()
            copies.append(own)
        for k in range(1, N_DEV):
            bits = ((k >> 2) & 1, (k >> 1) & 1, k & 1)
            peer = tuple(1 - p if b else p for p, b in zip(pos, bits))
            peer_id = 4 * peer[0] + 2 * peer[1] + peer[2]
            for i in range(n):
                cp = pltpu.make_async_remote_copy(
                    src_ref=x_refs[i] if gather else x_refs[i].at[peer_id], dst_ref=o_refs[i].at[me],
                    send_sem=send_sems.at[i * (N_DEV - 1) + k - 1], recv_sem=recv_sems.at[i * (N_DEV - 1) + k - 1],
                    device_id=peer, device_id_type=pl.DeviceIdType.MESH)
                cp.start()
                copies.append(cp)
        for cp in copies:
            cp.wait()

    hbm = pl.BlockSpec(memory_space=pltpu.HBM)
    return pl.pallas_call(
        body, name=name, in_specs=[hbm] * n + [pl.BlockSpec(memory_space=pl.ANY)] * len(extra), out_specs=[hbm] * n,
        out_shape=[jax.ShapeDtypeStruct(((N_DEV,) + a.shape) if gather else a.shape, a.dtype) for a in arrays],
        scratch_shapes=[pltpu.SemaphoreType.DMA((n * (N_DEV - 1),)), pltpu.SemaphoreType.DMA((n * (N_DEV - 1),)),
                        pltpu.SemaphoreType.DMA((n,))],
    )(*arrays, *extra)


_HBM = pl.BlockSpec(memory_space=pltpu.HBM)
_SEM = pl.BlockSpec(memory_space=pltpu.SEMAPHORE)
N_PEERS = N_DEV - 1


def _mesh_pos():
    pos = [lax.axis_index(ax) for ax in MESH_AXES]
    return pos, 4 * pos[0] + 2 * pos[1] + pos[2]


ALL_PEERS = (1, 2, 3, 4, 5, 6, 7)
NEAR_PEERS = (1, 2, 4, 6)
RELAYED = (2, 4, 6)


def _peers(pos, masks=ALL_PEERS):
    out = []
    for k in masks:
        bits = ((k >> 2) & 1, (k >> 1) & 1, k & 1)
        peer = tuple(1 - p if b else p for p, b in zip(pos, bits))
        out.append((peer, 4 * peer[0] + 2 * peer[1] + peer[2]))
    return out


def _part(x_ref, lead, gather, slot):
    ref = x_ref if lead is None else x_ref.at[lead]
    return ref if gather else ref.at[slot]


OWN_BLOCK_BYTES = 2 * 1024 * 1024


def _landing_zone(a, lead, me, name):
    r, c = a.shape[-2:]
    tr = r
    while tr * _round_up(c, LANES) * a.dtype.itemsize > OWN_BLOCK_BYTES and tr % 32 == 0:
        tr //= 2

    def body(me_ref, x_ref, o_ref):
        o_ref[...] = x_ref[...]

    x_spec = (pl.BlockSpec((tr, c), lambda i, me_ref: (i, 0)) if lead is None
              else pl.BlockSpec((None, tr, c), lambda i, me_ref: (lead, i, 0)))
    grid_spec = pltpu.PrefetchScalarGridSpec(
        num_scalar_prefetch=1, grid=(r // tr,), in_specs=[x_spec],
        out_specs=pl.BlockSpec((None, tr, c), lambda i, me_ref: (me_ref[0], i, 0)))
    return pl.pallas_call(body, name=name, grid_spec=grid_spec, out_shape=jax.ShapeDtypeStruct((N_DEV, r, c), a.dtype),
                          compiler_params=_cparams(1))(jnp.reshape(me, (1,)).astype(jnp.int32), a)


def _split_copies(items, gather, masks, x_refs, land_refs, send_sems, recv_sems):
    pos, me = _mesh_pos()
    copies = []
    for i, (_, lead) in enumerate(items):
        for k, (peer, peer_id) in enumerate(_peers(pos, masks)):
            copies.append(pltpu.make_async_remote_copy(
                src_ref=_part(x_refs[i], lead, gather, peer_id), dst_ref=land_refs[i].at[me],
                send_sem=send_sems.at[i * len(masks) + k], recv_sem=recv_sems.at[i * len(masks) + k],
                device_id=peer, device_id_type=pl.DeviceIdType.MESH))
    return copies


_SPLIT_CALL = dict(compiler_params=pltpu.CompilerParams(has_side_effects=pltpu.SideEffectType.DATAFLOW_SIDE_EFFECTING))


def exchange_start(items, gather, lands, name, masks=ALL_PEERS):
    n = len(items)
    lands = list(lands)
    xs = [a for a, _ in items]

    def body(*refs):
        x_refs, land_refs = refs[:n], refs[n:2 * n]
        send_sems, recv_sems, token = refs[2 * n], refs[2 * n + 1], refs[-1]
        for cp in _split_copies(items, gather, masks, x_refs, land_refs, send_sems, recv_sems):
            cp.start()
        token[...] = jnp.zeros_like(token)

    outs = pl.pallas_call(
        body, name=name,
        out_shape=(pltpu.SemaphoreType.DMA((n * len(masks),)), pltpu.SemaphoreType.DMA((n * len(masks),)),
                   *[pltpu.HBM(v.shape, v.dtype) for v in xs + lands], jax.ShapeDtypeStruct((SUBLANES, LANES), F32)),
        in_specs=[_HBM] * (2 * n),
        out_specs=(_SEM, _SEM, *[_HBM] * (2 * n), pl.BlockSpec(memory_space=pltpu.VMEM)),
        input_output_aliases={i: 2 + i for i in range(2 * n)}, **_SPLIT_CALL,
    )(*[pltpu.with_memory_space_constraint(v, pltpu.HBM) for v in xs + lands])
    handle = (items, gather, masks, outs[0], outs[1], outs[2:2 + n], outs[2 + n:2 + 2 * n])
    return handle, outs[-1]


def _relay_copies(n, land_refs, send_sems, recv_sems):
    pos, _ = _mesh_pos()
    sibling = (pos[0], pos[1], 1 - pos[2])
    copies = []
    for i in range(n):
        for k, (_, peer_id) in enumerate(_peers(pos, RELAYED)):
            slot = land_refs[i].at[peer_id]
            copies.append(pltpu.make_async_remote_copy(
                src_ref=slot, dst_ref=slot, send_sem=send_sems.at[i * len(RELAYED) + k],
                recv_sem=recv_sems.at[i * len(RELAYED) + k], device_id=sibling, device_id_type=pl.DeviceIdType.MESH))
    return copies


def relay(lands, name):
    n = len(lands)
    sems = n * len(RELAYED)

    def start(*refs):
        for cp in _relay_copies(n, refs[:n], refs[n], refs[n + 1]):
            cp.start()

    outs = pl.pallas_call(
        start, name=name + "_start",
        out_shape=(pltpu.SemaphoreType.DMA((sems,)), pltpu.SemaphoreType.DMA((sems,)),
                   *[pltpu.HBM(v.shape, v.dtype) for v in lands]),
        in_specs=[_HBM] * n, out_specs=(_SEM, _SEM, *[_HBM] * n), input_output_aliases={i: 2 + i for i in range(n)},
        **_SPLIT_CALL)(*[pltpu.with_memory_space_constraint(v, pltpu.HBM) for v in lands])

    def wait(*refs):
        for cp in _relay_copies(n, refs[:n], refs[n], refs[n + 1]):
            cp.wait_send()
            cp.wait_recv()

    return list(pl.pallas_call(
        wait, name=name + "_wait", out_shape=tuple(pltpu.HBM(v.shape, v.dtype) for v in lands),
        in_specs=[_HBM] * n + [_SEM, _SEM], out_specs=tuple([_HBM] * n), input_output_aliases={i: i for i in range(n)},
        **_SPLIT_CALL)(*outs[2:], outs[0], outs[1]))


def exchange_wait(handle, after, name):
    items, gather, masks, send_sems, recv_sems, x_thru, land_thru = handle
    n = len(items)

    def body(*refs):
        x_refs, land_refs = refs[:n], refs[n:2 * n]
        for cp in _split_copies(items, gather, masks, x_refs, land_refs, refs[2 * n], refs[2 * n + 1]):
            cp.wait_send()
            cp.wait_recv()

    outs = pl.pallas_call(
        body, name=name, out_shape=tuple(pltpu.HBM(v.shape, v.dtype) for v in tuple(x_thru) + tuple(land_thru)),
        in_specs=[_HBM] * (2 * n) + [_SEM, _SEM, pl.BlockSpec(memory_space=pl.ANY)], out_specs=tuple([_HBM] * (2 * n)),
        input_output_aliases={i: i for i in range(2 * n)},
        compiler_params=pltpu.CompilerParams(has_side_effects=pltpu.SideEffectType.DATAFLOW_SIDE_EFFECTING),
    )(*x_thru, *land_thru, send_sems, recv_sems, after)
    return list(outs[:n]), list(outs[n:])


def _adam_update(w, g, m, v):
    nm = ADAM_B1 * m + (1.0 - ADAM_B1) * g
    nv = ADAM_B2 * v + (1.0 - ADAM_B2) * (g * g)
    m_hat = nm / (1.0 - ADAM_B1 ** ADAM_STEP)
    v_hat = nv / (1.0 - ADAM_B2 ** ADAM_STEP)
    return -ADAM_LR * (m_hat / (jnp.sqrt(v_hat) + ADAM_EPS) + ADAM_WD * w), nm, nv


def _sum_slots(s_ref):
    acc = s_ref[0].astype(F32)
    for j in range(1, N_DEV):
        acc = acc + s_ref[j].astype(F32)
    return acc


ADAM_BLOCK_BYTES = 10 * 1024 * 1024


def adamw_sharded(recvs, sends, me, w, m, v, name):
    nl, r, c = w.shape
    assert len(recvs) == nl and len(sends) == nl
    per_row = _round_up(c, LANES) * (nl * (N_DEV + 1) * recvs[0].dtype.itemsize + 7 * 4)
    tr = r
    while tr * per_row > ADAM_BLOCK_BYTES and tr % 16 == 0:
        tr //= 2

    def body(me_ref, *refs):
        s_refs, o_refs = refs[:nl], refs[nl:2 * nl]
        w_ref, m_ref, v_ref, g_ref, d_ref, nm_ref, nv_ref = refs[2 * nl:]
        mine = me_ref[0]

        def total(l):
            acc = jnp.where(mine == 0, o_refs[l][...], s_refs[l][0]).astype(F32)
            for j in range(1, N_DEV):
                acc = acc + jnp.where(mine == j, o_refs[l][...], s_refs[l][j]).astype(F32)
            return acc

        g = total(0)
        for l in range(1, nl):
            g = jnp.where(pl.program_id(0) == l, total(l), g)
        g_ref[...] = g
        d_ref[...], nm_ref[...], nv_ref[...] = _adam_update(w_ref[...], g, m_ref[...], v_ref[...])

    spec = pl.BlockSpec((None, tr, c), lambda l, i, me_ref: (l, i, 0))
    shp = jax.ShapeDtypeStruct(w.shape, F32)
    grid_spec = pltpu.PrefetchScalarGridSpec(
        num_scalar_prefetch=1, grid=(nl, r // tr),
        in_specs=([pl.BlockSpec((N_DEV, tr, c), lambda l, i, me_ref: (0, i, 0))] * nl
                  + [pl.BlockSpec((None, tr, c), lambda l, i, me_ref: (me_ref[0], i, 0))] * nl + [spec, spec, spec]),
        out_specs=[spec] * 4)
    return pl.pallas_call(body, name=name, grid_spec=grid_spec, out_shape=[shp] * 4, compiler_params=_cparams(2))(
        jnp.reshape(me, (1,)).astype(jnp.int32), *recvs, *sends, w, m, v)


def reduce_slots(slots, name):
    _, r, lanes = slots.shape
    tr = _tile(r, 2048)

    def body(s_ref, o_ref):
        o_ref[...] = _sum_slots(s_ref)

    return pl.pallas_call(
        body, name=name, grid=(r // tr,),
        in_specs=[pl.BlockSpec((N_DEV, tr, lanes), lambda i: (0, i, 0))],
        out_specs=pl.BlockSpec((tr, lanes), lambda i: (i, 0)),
        out_shape=jax.ShapeDtypeStruct((r, lanes), F32), compiler_params=_cparams(1),
    )(slots)


def adamw(w, g, m, v, name):
    r, lanes = w.shape
    tr = _tile(r, 2048)

    def body(w_ref, g_ref, m_ref, v_ref, d_ref, nm_ref, nv_ref):
        d_ref[...], nm_ref[...], nv_ref[...] = _adam_update(w_ref[...], g_ref[...], m_ref[...], v_ref[...])

    spec = pl.BlockSpec((tr, lanes), lambda i: (i, 0))
    shp = jax.ShapeDtypeStruct((r, lanes), F32)
    return pl.pallas_call(body, name=name, grid=(r // tr,), in_specs=[spec] * 4, out_specs=[spec] * 3,
                          out_shape=[shp] * 3, compiler_params=_cparams(1))(w, g, m, v)


def cols_from_slots(slots, place, width, name):
    _, rows, c = slots.shape
    tr = _tile(rows, 256)

    def body(s_ref, o_ref):
        o_ref[...] = place(jnp.concatenate([s_ref[j] for j in range(N_DEV)], axis=1))

    return pl.pallas_call(
        body, name=name, grid=(rows // tr,),
        in_specs=[pl.BlockSpec((N_DEV, tr, c), lambda i: (0, i, 0))],
        out_specs=pl.BlockSpec((tr, width), lambda i: (i, 0)),
        out_shape=jax.ShapeDtypeStruct((rows, width), slots.dtype), compiler_params=_cparams(1))(slots)


def slots_from_cols(full, pick, c, name):
    rows, wide = full.shape
    tr = _tile(rows, 256)

    def body(x_ref, o_ref):
        v = pick(x_ref[...])
        for j in range(N_DEV):
            o_ref[j] = v[:, j * c:(j + 1) * c]

    return pl.pallas_call(
        body, name=name, grid=(rows // tr,),
        in_specs=[pl.BlockSpec((tr, wide), lambda i: (i, 0))],
        out_specs=pl.BlockSpec((N_DEV, tr, c), lambda i: (0, i, 0)),
        out_shape=jax.ShapeDtypeStruct((N_DEV, rows, c), full.dtype), compiler_params=_cparams(1))(full)


def _token_tile(cfg):
    return min(cfg.S, 256)


def norm_fwd(cfg, x, w, name):
    ts = _token_tile(cfg)
    d = x.shape[1]

    def fn(ins, _):
        xv, wv = ins
        return [rms(xv, wv)], [], []

    return seq_call(name, fn, (1, cfg.S // ts),
                    [(x, (ts, d), lambda g, t: (t, 0)), (w, (1, d), lambda g, t: (0, 0))],
                    [((cfg.S, d), BF16, (ts, d), lambda g, t: (t, 0))])[0]


def norm_bwd(cfg, x, w, dh, dres, name):
    ts = _token_tile(cfg)
    d = x.shape[1]

    def fn(ins, _):
        xv, wv, dhv, drv = ins
        _, vjp = jax.vjp(rms, xv, wv)
        dx, dw = vjp(dhv.astype(F32))
        return [dx + drv], [dw], []

    row = lambda g, t: (t, 0)
    out = seq_call(name, fn, (1, cfg.S // ts),
                   [(x, (ts, d), row), (w, (1, d), lambda g, t: (0, 0)), (dh, (ts, d), row), (dres, (ts, d), row)],
                   [((cfg.S, d), F32, (ts, d), row)],
                   accs=[((1, d), (1, d), lambda g, t: (0, 0))])
    return out[0], out[1]


def head_fwd_bwd(cfg, x, w, target, name):
    ts = _token_tile(cfg)
    d = x.shape[1]

    def fn(ins, _):
        xv, wv, tv = ins
        y, vjp = jax.vjp(rms, xv, wv)
        err = y - tv
        loss = 0.5 * jnp.sum(err * err) / d
        dx, dw = vjp(err / d)
        return [dx], [jnp.full((SUBLANES, LANES), loss, F32), dw], []

    row = lambda g, t: (t, 0)
    fixed = lambda g, t: (0, 0)
    dx, loss, dw = seq_call(name, fn, (1, cfg.S // ts),
                            [(x, (ts, d), row), (w, (1, d), fixed), (target, (ts, d), row)],
                            [((cfg.S, d), F32, (ts, d), row)],
                            accs=[((SUBLANES, LANES), (SUBLANES, LANES), fixed), ((1, d), (1, d), fixed)])
    return loss, dx, dw


FFN_ROWS = 1024


def ffn_gate_up(h, w_gate, w_up, name):
    s, d = h.shape
    c = w_gate.shape[1]
    tm = _tile(s, FFN_ROWS)

    def body(h_ref, wg_ref, wu_ref, g_ref, u_ref, a_ref):
        hv = h_ref[...]
        g = _dg(hv, wg_ref[...], 1, 1)
        u = _dg(hv, wu_ref[...], 1, 1)
        g_ref[...] = g
        u_ref[...] = u
        a_ref[...] = (silu(g) * u).astype(a_ref.dtype)

    w_spec = pl.BlockSpec((None, c, d), lambda i, j: (j, 0, 0))
    o_spec = pl.BlockSpec((None, tm, c), lambda i, j: (j, i, 0))
    shp = (N_DEV, s, c)
    return pl.pallas_call(
        body, name=name, grid=(s // tm, N_DEV), in_specs=[pl.BlockSpec((tm, d), lambda i, j: (i, 0)), w_spec, w_spec],
        out_specs=[o_spec] * 3,
        out_shape=[jax.ShapeDtypeStruct(shp, F32), jax.ShapeDtypeStruct(shp, F32), jax.ShapeDtypeStruct(shp, BF16)],
        compiler_params=_cparams(2))(h, w_gate, w_up)


def ffn_dgate_dup(dx, w_down, gate, up, after, name):
    s, d = dx.shape
    c = w_down.shape[1]
    tm = _tile(s, FFN_ROWS)
    extra = [] if after is None else [after]

    def body(dx_ref, wd_ref, g_ref, u_ref, *rest):
        dg_ref, du_ref = rest[len(extra):]
        dact = _dg(dx_ref[...], wd_ref[...], 1, 1)
        _, vjp = jax.vjp(lambda a, b: silu(a) * b, g_ref[...], u_ref[...])
        dg, du = vjp(dact)
        dg_ref[...] = dg.astype(dg_ref.dtype)
        du_ref[...] = du.astype(du_ref.dtype)

    blk = pl.BlockSpec((None, tm, c), lambda i, j: (j, i, 0))
    shp = jax.ShapeDtypeStruct((N_DEV, s, c), BF16)
    return pl.pallas_call(
        body, name=name, grid=(s // tm, N_DEV),
        in_specs=[pl.BlockSpec((tm, d), lambda i, j: (i, 0)), pl.BlockSpec((None, c, d), lambda i, j: (j, 0, 0)), blk, blk]
        + [pl.BlockSpec(memory_space=pl.ANY)] * len(extra),
        out_specs=[blk, blk], out_shape=[shp, shp], compiler_params=_cparams(2))(dx, w_down, gate, up, *extra)


CONV_COLS = 256
HALO = SUBLANES


def _shift_down(x, halo, j):
    if j == 0:
        return x
    r8 = _rows_iota(halo.shape)
    top = jnp.where(r8 >= j, pltpu.roll(x[:HALO], j, 0), pltpu.roll(halo, j, 0))
    return jnp.concatenate([top, pltpu.roll(x, j, 0)[HALO:]], axis=0)


def _shift_up(x, halo, j):
    if j == 0:
        return x
    n = x.shape[0]
    r8 = _rows_iota(halo.shape)
    bot = jnp.where(r8 < HALO - j, pltpu.roll(x[n - HALO:], HALO - j, 0), pltpu.roll(halo, HALO - j, 0))
    return jnp.concatenate([pltpu.roll(x, n - j, 0)[:n - HALO], bot], axis=0)


def _conv_tile(cfg):
    return min(cfg.S, 512)


def conv_fwd(cfg, src, col0, width, w, b, name):
    tt, cb = _conv_tile(cfg), CONV_COLS
    c0, hb = col0 // cb, tt // HALO
    nt = cfg.S // tt

    def body(x_ref, h_ref, w_ref, b_ref, o_ref):
        t = pl.program_id(1)
        x = x_ref[...]
        halo = jnp.where(t > 0, h_ref[...], 0.0)
        wv = w_ref[...]
        acc = b_ref[...] + wv[CONV_WIDTH - 1:CONV_WIDTH] * x
        for j in range(1, CONV_WIDTH):
            acc = acc + wv[CONV_WIDTH - 1 - j:CONV_WIDTH - j] * _shift_down(x, halo, j)
        o_ref[...] = acc

    return pl.pallas_call(
        body, name=name, grid=(width // cb, nt),
        in_specs=[pl.BlockSpec((tt, cb), lambda c, t: (t, c0 + c)),
                  pl.BlockSpec((HALO, cb), lambda c, t: (jnp.maximum(t * hb - 1, 0), c0 + c)),
                  pl.BlockSpec((CONV_WIDTH, cb), lambda c, t: (0, c)),
                  pl.BlockSpec((1, cb), lambda c, t: (0, c))],
        out_specs=pl.BlockSpec((tt, cb), lambda c, t: (t, c)),
        out_shape=jax.ShapeDtypeStruct((cfg.S, width), F32), compiler_params=_cparams(2),
    )(src, src, w, b)


def conv_bwd(cfg, src, col0, width, w, dy, name):
    tt, cb = _conv_tile(cfg), CONV_COLS
    c0, hb = col0 // cb, tt // HALO
    nt = cfg.S // tt

    def body(x_ref, h_ref, w_ref, dy_ref, dh_ref, dx_ref, dw_ref, db_ref):
        t = pl.program_id(1)

        @pl.when(t == 0)
        def _():
            dw_ref[...] = jnp.zeros_like(dw_ref)
            db_ref[...] = jnp.zeros_like(db_ref)

        x = x_ref[...]
        halo = jnp.where(t > 0, h_ref[...], 0.0)
        dy = dy_ref[...]
        dhalo = jnp.where(t < nt - 1, dh_ref[...], 0.0)
        wv = w_ref[...]
        dx = wv[CONV_WIDTH - 1:CONV_WIDTH] * dy
        rows = [jnp.sum(dy * x, axis=0, keepdims=True)]
        for j in range(1, CONV_WIDTH):
            dx = dx + wv[CONV_WIDTH - 1 - j:CONV_WIDTH - j] * _shift_up(dy, dhalo, j)
            rows.insert(0, jnp.sum(dy * _shift_down(x, halo, j), axis=0, keepdims=True))
        dx_ref[...] = dx.astype(dx_ref.dtype)
        dw_ref[...] += jnp.concatenate(rows, axis=0)
        db_ref[...] += jnp.sum(dy, axis=0, keepdims=True)

    return pl.pallas_call(
        body, name=name, grid=(width // cb, nt),
        in_specs=[pl.BlockSpec((tt, cb), lambda c, t: (t, c0 + c)),
                  pl.BlockSpec((HALO, cb), lambda c, t: (jnp.maximum(t * hb - 1, 0), c0 + c)),
                  pl.BlockSpec((CONV_WIDTH, cb), lambda c, t: (0, c)),
                  pl.BlockSpec((tt, cb), lambda c, t: (t, c)),
                  pl.BlockSpec((HALO, cb), lambda c, t: (jnp.minimum((t + 1) * hb, nt * hb - 1), c))],
        out_specs=[pl.BlockSpec((tt, cb), lambda c, t: (t, c)),
                   pl.BlockSpec((CONV_WIDTH, cb), lambda c, t: (0, c)),
                   pl.BlockSpec((1, cb), lambda c, t: (0, c))],
        out_shape=[jax.ShapeDtypeStruct((cfg.S, width), BF16), jax.ShapeDtypeStruct((CONV_WIDTH, width), F32),
                   jax.ShapeDtypeStruct((1, width), F32)],
        compiler_params=_cparams(2),
    )(src, src, w, dy, dy)


def _gla_core(gh, q, k, v, g, glr, wg, bg, wn, st):
    n = glr.shape[0]
    causal = _lower_tri(n)
    outs, new = [], []
    for h in range(gh):
        log_a = log_sigmoid(bdot(glr, wg[h], 1, 0) + bg[h]) * (1.0 / GATE_NORM)
        bcum = cumsum_rows(log_a)
        b_last, b_mid = _row(bcum, n - 1), _row(bcum, n // 2)
        qs = q[h] * (HEAD_K ** -0.5)
        scores = jnp.where(causal, bdot(qs * jnp.exp(bcum - b_mid), k[h] * jnp.exp(b_mid - bcum), 1, 1), 0.0)
        o = bdot(scores, v[h], 1, 0) + bdot(qs * jnp.exp(bcum), st[h], 1, 1)
        new.append(st[h] * jnp.exp(b_last) + bdot(v[h], k[h] * jnp.exp(b_last - bcum), 0, 0))
        outs.append(rms(o, wn) * silu(g[h]))
    return jnp.concatenate(outs, axis=1), new


def _gla_ins(cfg, p0, wg, bg, wn, tmap):
    gh = cfg.GH
    ins = []
    for h in range(gh):
        ins.append((p0, (CHUNK, HEAD_K), lambda g, t, h=h: (tmap(t), h)))
    for h in range(gh):
        ins.append((p0, (CHUNK, HEAD_K), lambda g, t, h=h: (tmap(t), gh + h)))
    for h in range(gh):
        ins.append((p0, (CHUNK, HEAD_V), lambda g, t, h=h: (tmap(t), gh + h)))
    for h in range(gh):
        ins.append((p0, (CHUNK, HEAD_V), lambda g, t, h=h: (tmap(t), 2 * gh + h)))
    ins.append((p0, (CHUNK, LANES), lambda g, t: (tmap(t), 10 * gh)))
    for h in range(gh):
        ins.append((wg, (LANES, HEAD_K), lambda g, t, h=h: (0, h)))
    for h in range(gh):
        ins.append((bg, (1, HEAD_K), lambda g, t, h=h: (0, h)))
    ins.append((wn, (1, HEAD_V), lambda g, t: (0, 0)))
    return ins


def _gla_unpack(gh, vals):
    q, k, v, g = (vals[i * gh:(i + 1) * gh] for i in range(4))
    glr = vals[4 * gh]
    wg = vals[4 * gh + 1:5 * gh + 1]
    bg = vals[5 * gh + 1:6 * gh + 1]
    wn = vals[6 * gh + 1]
    return q, k, v, g, glr, wg, bg, wn, vals[6 * gh + 2:]


def gla_fwd(cfg, p0, wg, bg, wn, name):
    gh, nc = cfg.GH, cfg.S // CHUNK

    def fn(ins, st):
        q, k, v, g, glr, wgv, bgv, wnv, _ = _gla_unpack(gh, ins)
        out, new = _gla_core(gh, q, k, v, g, glr, wgv, bgv, wnv, st)
        return [out, jnp.stack(st)], [], new

    return seq_call(name, fn, (1, nc), _gla_ins(cfg, p0, wg, bg, wn, lambda t: t),
                    [((cfg.S, cfg.DV), BF16, (CHUNK, cfg.DV), lambda g, t: (t, 0)),
                     ((nc, gh, HEAD_V, HEAD_K), F32, (None, gh, HEAD_V, HEAD_K), lambda g, t: (t, 0, 0, 0))],
                    carries=[(HEAD_V, HEAD_K)] * gh)


def gla_bwd(cfg, p0, wg, bg, wn, states, dout, name):
    gh, nc = cfg.GH, cfg.S // CHUNK
    rev = lambda t: nc - 1 - t

    def fn(ins, dst):
        q, k, v, g, glr, wgv, bgv, wnv, rest = _gla_unpack(gh, ins)
        st_all, do = rest
        st = [st_all[h] for h in range(gh)]
        _, vjp = jax.vjp(functools.partial(_gla_core, gh), q, k, v, g, glr, wgv, bgv, wnv, st)
        dq, dk, dv, dg, dglr, dwg, dbg, dwn, dstate = vjp((do.astype(F32), list(dst)))
        return ([jnp.concatenate(list(dq) + list(dk) + list(dv) + list(dg), axis=1), dglr],
                [jnp.concatenate(dwg, axis=1), jnp.concatenate(dbg, axis=1), dwn], dstate)

    ins = _gla_ins(cfg, p0, wg, bg, wn, rev)
    ins.append((states, (None, gh, HEAD_V, HEAD_K), lambda g, t: (rev(t), 0, 0, 0)))
    ins.append((dout, (CHUNK, cfg.DV), lambda g, t: (rev(t), 0)))
    wide = 2 * cfg.DK + 2 * cfg.DV
    fixed = lambda g, t: (0, 0)
    return seq_call(name, fn, (1, nc), ins,
                    [((cfg.S, wide), BF16, (CHUNK, wide), lambda g, t: (rev(t), 0)),
                     ((cfg.S, LANES), BF16, (CHUNK, LANES), lambda g, t: (rev(t), 0))],
                    accs=[((LANES, cfg.DK), (LANES, cfg.DK), fixed), ((1, cfg.DK), (1, cfg.DK), fixed),
                          ((1, HEAD_V), (1, HEAD_V), fixed)],
                    carries=[(HEAD_V, HEAD_K)] * gh)


def _lru_core(xc, gate, wa, wi, ba, bi, lam, h_in):
    r = jax.nn.sigmoid(bdot(xc, wa, 1, 0) + ba)
    i = jax.nn.sigmoid(bdot(xc, wi, 1, 0) + bi)
    log_a = LRU_C * r * log_sigmoid(lam)
    a = jnp.exp(log_a)
    u = jnp.sqrt(-_expm1(2.0 * log_a)) * (i * xc)
    first = _rows_iota(a.shape) == 0
    h = lin_scan(a, u + jnp.where(first, a * h_in, 0.0))
    return h * gelu_tanh(gate), _row(h, a.shape[0] - 1)


def _lru_tile(cfg):
    return min(cfg.S, 512)


def _lru_ins(cfg, xc, p0, wa, wi, ba, bi, lam, tmap):
    tt, gh = _lru_tile(cfg), cfg.GH
    vec = lambda g, t: (0, g)
    return [(xc, (tt, LRU_BLOCK), lambda g, t: (tmap(t), g)),
            (p0, (tt, LRU_BLOCK), lambda g, t: (tmap(t), 8 * gh + g)),
            (wa, (None, LRU_BLOCK, LRU_BLOCK), lambda g, t: (g, 0, 0)),
            (wi, (None, LRU_BLOCK, LRU_BLOCK), lambda g, t: (g, 0, 0)),
            (ba, (1, LRU_BLOCK), vec), (bi, (1, LRU_BLOCK), vec), (lam, (1, LRU_BLOCK), vec)]


def lru_fwd(cfg, xc, p0, wa, wi, ba, bi, lam, name):
    tt, nb = _lru_tile(cfg), cfg.NB
    nt = cfg.S // tt

    def fn(ins, c):
        out, h_last = _lru_core(*ins, c[0])
        return [out, c[0]], [], [h_last]

    return seq_call(name, fn, (nb, nt), _lru_ins(cfg, xc, p0, wa, wi, ba, bi, lam, lambda t: t),
                    [((cfg.S, cfg.W), BF16, (tt, LRU_BLOCK), lambda g, t: (t, g)),
                     ((nb, nt, 1, LRU_BLOCK), F32, (None, None, 1, LRU_BLOCK), lambda g, t: (g, t, 0, 0))],
                    carries=[(1, LRU_BLOCK)])


def lru_bwd(cfg, xc, p0, wa, wi, ba, bi, lam, states, dout, name):
    tt, nb = _lru_tile(cfg), cfg.NB
    nt = cfg.S // tt
    rev = lambda t: nt - 1 - t

    def fn(ins, c):
        *fwd_ins, h_in, do = ins
        _, vjp = jax.vjp(_lru_core, *fwd_ins, h_in)
        dxc, dgate, dwa, dwi, dba, dbi, dlam, dh = vjp((do.astype(F32), c[0]))
        return [dxc, dgate], [dwa, dwi, dba, dbi, dlam], [dh]

    ins = _lru_ins(cfg, xc, p0, wa, wi, ba, bi, lam, rev)
    ins.append((states, (None, None, 1, LRU_BLOCK), lambda g, t: (g, rev(t), 0, 0)))
    ins.append((dout, (tt, LRU_BLOCK), lambda g, t: (rev(t), g)))
    mat = ((nb, LRU_BLOCK, LRU_BLOCK), (None, LRU_BLOCK, LRU_BLOCK), lambda g, t: (g, 0, 0))
    vec = ((1, cfg.W), (1, LRU_BLOCK), lambda g, t: (0, g))
    return seq_call(name, fn, (nb, nt), ins,
                    [((cfg.S, cfg.W), F32, (tt, LRU_BLOCK), lambda g, t: (rev(t), g)),
                     ((cfg.S, cfg.W), BF16, (tt, LRU_BLOCK), lambda g, t: (rev(t), g))],
                    accs=[mat, mat, vec, vec, vec], carries=[(1, LRU_BLOCK)])


def _ssd_core(xc, bc, cc, z, dt_raw, dt_bias, a_log, d_skip, gn, st):
    n = xc.shape[0]
    x, bm, cm = silu(xc), silu(bc), silu(cc)
    dt = softplus(dt_raw + dt_bias)
    acs = cumsum_rows_mxu(dt * (-jnp.exp(a_log)))
    acs_t = acs.T
    acs_e, dt_e = _expand_heads(acs), _expand_heads(dt)
    last_e = _expand_heads(_row(acs, n - 1))
    causal = _lower_tri(n)
    cb = bdot(cm, bm, 1, 1)
    xdt = x * dt_e
    y_diag = []
    for h, xh in enumerate(_split_heads(xdt)):
        seg = acs[:, h:h + 1] - acs_t[h:h + 1, :]
        decay = jnp.where(causal, jnp.exp(jnp.minimum(seg, 0.0)), 0.0)
        y_diag.append(bdot(cb * decay, xh, 1, 0))
    y = jnp.concatenate(y_diag, axis=1) + bdot(cm, st, 1, 0) * jnp.exp(acs_e)
    new = st * jnp.exp(last_e) + bdot(bm, xdt * jnp.exp(last_e - acs_e), 0, 0)
    y = (y + _expand_heads(d_skip) * x) * silu(z)
    return rms(y, gn), new


SSD_TILED = 5


SSD_FWD_CHUNKS = 4
SSD_BWD_CHUNKS = 1


def _ssd_tile(cfg, chunks):
    return min(cfg.S, chunks * CHUNK)


def _chunk_rows(v, s):
    return v[s * CHUNK:(s + 1) * CHUNK]


def _ssd_ins(cfg, tt, xc, p1, dt_raw, dt_bias, a_log, d_skip, gn, tmap):
    ng = cfg.NG
    vec = lambda g, t: (g, 0, 0)
    return [(xc, (tt, SSD_GW), lambda g, t: (tmap(t), g)),
            (xc, (tt, SSD_N), lambda g, t: (tmap(t), 4 * ng + g)),
            (xc, (tt, SSD_N), lambda g, t: (tmap(t), 5 * ng + g)),
            (p1, (tt, SSD_GW), lambda g, t: (tmap(t), g)),
            (dt_raw, (None, tt, LANES), lambda g, t: (g, tmap(t), 0)),
            (dt_bias, (None, 1, LANES), vec), (a_log, (None, 1, LANES), vec), (d_skip, (None, 1, LANES), vec),
            (gn, (1, SSD_GW), lambda g, t: (0, g))]


def ssd_fwd(cfg, xc, p1, dt_raw, dt_bias, a_log, d_skip, gn, name):
    ng, nc, tt = cfg.NG, cfg.S // CHUNK, _ssd_tile(cfg, SSD_FWD_CHUNKS)
    nsub = tt // CHUNK

    def fn(ins, c):
        tiled, params = ins[:SSD_TILED], ins[SSD_TILED:]
        st, outs, entered = c[0], [], []
        for s in range(nsub):
            entered.append(st)
            out, st = _ssd_core(*[_chunk_rows(v, s) for v in tiled], *params, st)
            outs.append(out)
        return [jnp.concatenate(outs, axis=0), jnp.stack(entered)], [], [st]

    return seq_call(name, fn, (ng, cfg.S // tt), _ssd_ins(cfg, tt, xc, p1, dt_raw, dt_bias, a_log, d_skip, gn, lambda t: t),
                    [((cfg.S, cfg.DI), BF16, (tt, SSD_GW), lambda g, t: (t, g)),
                     ((ng, nc, SSD_N, SSD_GW), F32, (None, nsub, SSD_N, SSD_GW), lambda g, t: (g, t, 0, 0))],
                    carries=[(SSD_N, SSD_GW)])


def ssd_bwd(cfg, xc, p1, dt_raw, dt_bias, a_log, d_skip, gn, states, dout, name):
    ng, tt = cfg.NG, _ssd_tile(cfg, SSD_BWD_CHUNKS)
    nsub, nt = tt // CHUNK, cfg.S // tt
    rev = lambda t: nt - 1 - t

    def fn(ins, c):
        tiled, params = ins[:SSD_TILED], ins[SSD_TILED:SSD_TILED + 4]
        st_all, do = ins[SSD_TILED + 4:]
        dst, pieces, acc = c[0], [None] * nsub, None
        for s in reversed(range(nsub)):
            _, vjp = jax.vjp(_ssd_core, *[_chunk_rows(v, s) for v in tiled], *params, st_all[s])
            grads = vjp((_chunk_rows(do, s).astype(F32), dst))
            pieces[s], dparams, dst = grads[:SSD_TILED], grads[SSD_TILED:SSD_TILED + 4], grads[SSD_TILED + 4]
            acc = dparams if acc is None else [x + y for x, y in zip(acc, dparams)]
        return [jnp.concatenate([p[i] for p in pieces], axis=0) for i in range(SSD_TILED)], list(acc), [dst]

    ins = _ssd_ins(cfg, tt, xc, p1, dt_raw, dt_bias, a_log, d_skip, gn, rev)
    ins.append((states, (None, nsub, SSD_N, SSD_GW), lambda g, t: (g, rev(t), 0, 0)))
    ins.append((dout, (tt, SSD_GW), lambda g, t: (rev(t), g)))
    col = lambda g, t: (rev(t), g)
    vec = ((ng, 1, LANES), (None, 1, LANES), lambda g, t: (g, 0, 0))
    return seq_call(name, fn, (ng, nt), ins,
                    [((cfg.S, cfg.DI), F32, (tt, SSD_GW), col),
                     ((cfg.S, ng * SSD_N), F32, (tt, SSD_N), col),
                     ((cfg.S, ng * SSD_N), F32, (tt, SSD_N), col),
                     ((cfg.S, cfg.DI), BF16, (tt, SSD_GW), col),
                     ((ng, cfg.S, LANES), F32, (None, tt, LANES), lambda g, t: (g, rev(t), 0))],
                    accs=[vec, vec, vec, ((1, cfg.DI), (1, SSD_GW), lambda g, t: (0, g))],
                    carries=[(SSD_N, SSD_GW)])


PACK_ALIGN = SUBLANES * LANES
PACK_ROWS = 256


def pack(arrays):
    pieces = []
    for a in arrays:
        flat = a.reshape(-1).astype(F32)
        pad = _round_up(flat.shape[0], PACK_ALIGN) - flat.shape[0]
        pieces.append(jnp.pad(flat, (0, pad)) if pad else flat)
    flat = jnp.concatenate(pieces)
    pad = _round_up(flat.shape[0], PACK_ROWS * LANES) - flat.shape[0]
    return jnp.pad(flat, (0, pad)).reshape(-1, LANES)


def unpack(buf, shapes):
    lead = buf.shape[:-2]
    flat = buf.reshape(lead + (-1,))
    out, off = [], 0
    for s in shapes:
        n = math.prod(s)
        out.append(flat[..., off:off + n].reshape(lead + tuple(s)))
        off += _round_up(n, PACK_ALIGN)
    return out


def _slots_to_cols(slots):
    return slots.transpose(1, 0, 2).reshape(slots.shape[1], -1)


def _even_in_padded(cfg, w):
    main = 2 * cfg.DK + 2 * cfg.DV
    return jnp.concatenate([w[:, :main], w[:, main + GATE_RANK:], w[:, main:main + GATE_RANK],
                            jnp.zeros((w.shape[0], cfg.EP - cfg.EVEN_IN), w.dtype)], axis=1)


def _even_in_unpadded(cfg, wp):
    main = 2 * cfg.DK + 2 * cfg.DV
    rest = main + 2 * cfg.W
    return jnp.concatenate([wp[:, :main], wp[:, rest:rest + GATE_RANK], wp[:, main:rest]], axis=1)


def _odd_in_padded(cfg, w):
    return jnp.concatenate([w, jnp.zeros((w.shape[0], cfg.OP - cfg.ODD_IN), w.dtype)], axis=1)


def _odd_in_unpadded(cfg, wp):
    return wp[:, :cfg.ODD_IN]


def _group_lanes(cfg, v):
    lead = v.shape[:-1]
    g = jnp.moveaxis(v.reshape(lead + (cfg.NG, SSD_HG)), -2, 0)
    return jnp.pad(g, [(0, 0)] * (g.ndim - 1) + [(0, LANES - SSD_HG)])


def _ungroup_lanes(cfg, g):
    v = jnp.moveaxis(g[..., :SSD_HG], 0, -2)
    return v.reshape(v.shape[:-2] + (cfg.NH,))


def train_step(cfg, p, loss_target):
    S, D = cfg.S, cfg.D
    me = 4 * lax.axis_index("x") + 2 * lax.axis_index("y") + lax.axis_index("c")

    big = ["ev_w_in", "ev_w_out", "od_w_in", "od_w_out", "ffn_w_gate", "ffn_w_up", "ffn_w_down"]
    small_sharded = ["ev_gla_w_gate", "ev_lru_conv_w", "od_norm", "od_conv_w", "od_conv_b", "od_gnorm"]
    replicated = ["ev_norm", "ev_gla_b_gate", "ev_gla_w_onorm", "ev_lru_conv_b", "ev_lru_w_a", "ev_lru_b_a",
                  "ev_lru_w_i", "ev_lru_b_i", "ev_lru_lam", "od_dt_bias", "od_a_log", "od_d_skip", "ffn_norm",
                  "final_norm"]

    transposed = ("ffn_w_gate", "ffn_w_up")
    view = lambda n, a: jnp.swapaxes(a, 1, 2) if n in transposed else a
    wb = {n: view(n, p[n]).astype(BF16) for n in big}
    ffn_items = lambda l: [(wb["ffn_w_gate"], l), (wb["ffn_w_up"], l), (wb["ffn_w_down"], l)]
    ss_shapes = [p[n].shape for n in small_sharded]
    groups = [[(pack([p[n] for n in small_sharded]), None), (wb["ev_w_in"], 0), (wb["ev_w_out"], 0)], ffn_items(0),
              [(wb["od_w_in"], 0), (wb["od_w_out"], 0)], ffn_items(1)]
    gathers, tokens = [], []
    for i, g in enumerate(groups):
        lands = [_landing_zone(a, lead, me, f"gather_own{i}_{j}") for j, (a, lead) in enumerate(g)]
        handle, token = exchange_start(g, True, lands, f"gather_start{i}", NEAR_PEERS)
        gathers.append(handle)
        tokens.append(token)
    all_started = tokens[0][:1, :1] + tokens[1][:1, :1] + tokens[2][:1, :1] + tokens[3][:1, :1]

    def gathered(i, after):
        return relay(exchange_wait(gathers[i], after, f"gather_wait{i}")[1], f"gather_relay{i}")

    ss_all, gw_ev_in, gw_ev_out = gathered(0, all_started)
    gs = dict(zip(small_sharded, unpack(ss_all, ss_shapes)))
    w_ev_in = cols_from_slots(gw_ev_in, functools.partial(_even_in_padded, cfg), cfg.EP, "ev_w_in_cols")
    w_ev_out = gw_ev_out.reshape(D, D)

    gla_wg = jnp.pad(_slots_to_cols(gs["ev_gla_w_gate"][:, 0]), ((0, LANES - GATE_RANK), (0, 0)))
    lru_cw = _slots_to_cols(gs["ev_lru_conv_w"][:, 0])
    od_norm = gs["od_norm"].transpose(1, 0, 2).reshape(1, D)
    od_cw = _slots_to_cols(gs["od_conv_w"][:, 0])
    od_cb = gs["od_conv_b"].transpose(1, 0, 2).reshape(1, cfg.CD)
    od_gn = gs["od_gnorm"].transpose(1, 0, 2).reshape(1, cfg.DI)

    x0 = p["x"][0]
    target = loss_target[0]
    ev_norm = p["ev_norm"] + all_started
    bg = p["ev_gla_b_gate"]
    wn = p["ev_gla_w_onorm"]
    lru_cb = p["ev_lru_conv_b"]
    wa, wi = p["ev_lru_w_a"][0], p["ev_lru_w_i"][0]
    ba, bi, lam = p["ev_lru_b_a"], p["ev_lru_b_i"], p["ev_lru_lam"]
    dt_bias, a_log, d_skip = (_group_lanes(cfg, p[n]) for n in ("od_dt_bias", "od_a_log", "od_d_skip"))
    ffn_norm = [p["ffn_norm"][l:l + 1] for l in range(2)]
    final_norm = p["final_norm"].reshape(1, D)

    def ffn_forward(l, x):
        w_gate, w_up, w_down = gathered(1 + 2 * l, x)
        h = norm_fwd(cfg, x, ffn_norm[l], f"ffn{l}_norm")
        gate, up, act = ffn_gate_up(h, w_gate, w_up, f"ffn{l}_gate_up")
        out = matmul(act, w_down, a_slot="k", b_slot="k", res=x, name=f"ffn{l}_down", tn=512)
        return out, (h, gate, up, act, w_gate, w_up, w_down)

    h0 = norm_fwd(cfg, x0, ev_norm, "ev_norm")
    p0 = matmul(h0, w_ev_in, name="ev_in", tn=768)
    gla_out, gla_states = gla_fwd(cfg, p0, gla_wg, bg, wn, "gla_fwd")
    lru_col = 2 * cfg.DK + 2 * cfg.DV
    lru_xc = conv_fwd(cfg, p0, lru_col, cfg.W, lru_cw, lru_cb, "lru_conv")
    lru_out, lru_states = lru_fwd(cfg, lru_xc, p0, wa, wi, ba, bi, lam, "lru_fwd")
    mix = jnp.concatenate([gla_out, lru_out], axis=1)
    x1 = matmul(mix, w_ev_out, res=x0, name="ev_out")
    x2, ffn0_saved = ffn_forward(0, x1)

    gw_od_in, gw_od_out = gathered(2, x2)
    w_od_in = cols_from_slots(gw_od_in, functools.partial(_odd_in_padded, cfg), cfg.OP, "od_w_in_cols")
    w_od_out = gw_od_out.reshape(cfg.DI, D)
    h2 = norm_fwd(cfg, x2, od_norm, "od_norm")
    p1 = matmul(h2, w_od_in, name="od_in", tn=768)
    od_xc = conv_fwd(cfg, p1, cfg.DI, cfg.CD, od_cw, od_cb, "od_conv")
    dt_col = cfg.DI + cfg.CD
    dt_raw = _group_lanes(cfg, p1[:, dt_col:dt_col + cfg.NH])
    ssd_out, ssd_states = ssd_fwd(cfg, od_xc, p1, dt_raw, dt_bias, a_log, d_skip, od_gn, "ssd_fwd")
    x3 = matmul(ssd_out, w_od_out, res=x2, name="od_out")
    x4, ffn1_saved = ffn_forward(1, x3)

    loss_part, dx4, d_final_norm = head_fwd_bwd(cfg, x4, final_norm, target, "head")
    loss = lax.psum(loss_part[0, 0], MESH_AXES)

    def ffn_backward(l, x, saved, dx_out, after):
        h, gate, up, act, w_gate, w_up, w_down = saved
        dgate, dup = ffn_dgate_dup(dx_out, w_down, gate, up, after, f"ffn{l}_dgate_dup")
        d_down = matmul(act, dx_out, ta=True, a_slot="m", out_dtype=BF16, name=f"ffn{l}_dwdown")
        dh = matmul(dgate, w_gate, a_slot="k", b_slot="k", name=f"ffn{l}_dh_gate", tn=512)
        dh = matmul(dup, w_up, a_slot="k", b_slot="k", res=dh, name=f"ffn{l}_dh_up", tn=512)
        d_gate = matmul(dgate, h, ta=True, a_slot="m", out_dtype=BF16, name=f"ffn{l}_dwgate")
        d_up = matmul(dup, h, ta=True, a_slot="m", out_dtype=BF16, name=f"ffn{l}_dwup")
        dx, dnorm = norm_bwd(cfg, x, ffn_norm[l], dh, dx_out, f"ffn{l}_norm_bwd")
        sent, token = start_grads([d_gate, d_up, d_down], f"grads_start_ffn{l}")
        return dx, dnorm, sent, token

    def start_grads(arrays, name):
        return exchange_start([(a, None) for a in arrays], False, [lax.empty(a.shape, a.dtype) for a in arrays], name)

    dx3, d_ffn_norm1, sent_ffn1, token = ffn_backward(1, x3, ffn1_saved, dx4, None)

    d_ssd_out = matmul(dx3, w_od_out, tb=True, after=token, name="od_dmix")
    d_od_out = matmul(ssd_out, dx3, ta=True, out_dtype=BF16, name="od_dwout")
    dxs, dbm, dcm, dz, d_dt_raw, d_dt_bias, d_a_log, d_d_skip, d_od_gn = ssd_bwd(
        cfg, od_xc, p1, dt_raw, dt_bias, a_log, d_skip, od_gn, ssd_states, d_ssd_out, "ssd_bwd")
    conv_parts, col = [], 0
    for part, dy in (("x", dxs), ("b", dbm), ("c", dcm)):
        width = dy.shape[1]
        conv_parts.append(conv_bwd(cfg, p1, cfg.DI + col, width, od_cw[:, col:col + width], dy, "od_conv_bwd_" + part))
        col += width
    d_od_cw = jnp.concatenate([c[1] for c in conv_parts], axis=1)
    d_od_cb = jnp.concatenate([c[2] for c in conv_parts], axis=1)
    d_dt = _ungroup_lanes(cfg, d_dt_raw).astype(BF16)
    dp1 = jnp.concatenate([dz] + [c[0] for c in conv_parts] + [d_dt, jnp.zeros((S, cfg.OP - cfg.ODD_IN), BF16)], axis=1)
    dh2 = matmul(dp1, w_od_in, tb=True, name="od_dh", tk=1536)
    d_od_in = matmul(h2, dp1, ta=True, out_dtype=BF16, name="od_dwin", tn=768)
    dx2, d_od_norm = norm_bwd(cfg, x2, od_norm, dh2, dx3, "od_norm_bwd")
    d_od_in_slots = slots_from_cols(d_od_in, functools.partial(_odd_in_unpadded, cfg), p["od_w_in"].shape[2],
                                    "od_dwin_slots")
    sent_od, token = start_grads([d_od_in_slots, d_od_out.reshape((N_DEV,) + p["od_w_out"].shape[1:])], "grads_start_od")

    dx1, d_ffn_norm0, sent_ffn0, token = ffn_backward(0, x1, ffn0_saved, dx2, token)

    d_mix = matmul(dx1, w_ev_out, tb=True, after=token, name="ev_dmix")
    d_ev_out = matmul(mix, dx1, ta=True, out_dtype=BF16, name="ev_dwout")
    d_qkvg, d_glr, d_gla_wg, d_bg, d_wn = gla_bwd(cfg, p0, gla_wg, bg, wn, gla_states, d_mix[:, :cfg.DV], "gla_bwd")
    d_lru_xc, d_gate_br, d_wa, d_wi, d_ba, d_bi, d_lam = lru_bwd(
        cfg, lru_xc, p0, wa, wi, ba, bi, lam, lru_states, d_mix[:, cfg.DV:], "lru_bwd")
    d_xbr, d_lru_cw, d_lru_cb = conv_bwd(cfg, p0, lru_col, cfg.W, lru_cw, d_lru_xc, "lru_conv_bwd")
    dp0 = jnp.concatenate([d_qkvg, d_xbr, d_gate_br, d_glr, jnp.zeros((S, cfg.EP - lru_col - 2 * cfg.W - LANES), BF16)],
                          axis=1)
    d_ev_in = matmul(h0, dp0, ta=True, out_dtype=BF16, name="ev_dwin", tn=768)
    d_ev_in_slots = slots_from_cols(d_ev_in, functools.partial(_even_in_unpadded, cfg), p["ev_w_in"].shape[2],
                                    "ev_dwin_slots")
    sent_ev, token = start_grads([d_ev_in_slots, d_ev_out.reshape((N_DEV,) + p["ev_w_out"].shape[1:])], "grads_start_ev")
    dh0 = matmul(dp0, w_ev_in, tb=True, after=token, name="ev_dh", tk=1792)
    grad_x, d_ev_norm = norm_bwd(cfg, x0, ev_norm, dh0, dx1, "ev_norm_bwd")

    out = {"loss": loss, "grad_x": grad_x[None]}

    def update(names, sent, after, wait_name):
        s, r = exchange_wait(sent[0], after, wait_name + "0")
        sends, recvs = [[a] for a in s], [[a] for a in r]
        for extra in sent[1:]:
            s, r = exchange_wait(extra, after, wait_name + "1")
            for i in range(len(names)):
                sends[i].append(s[i])
                recvs[i].append(r[i])
        for i, n in enumerate(names):
            res = adamw_sharded(recvs[i], sends[i], me, view(n, p[n]), view(n, p["m_" + n]), view(n, p["v_" + n]),
                                "adamw_" + n)
            out["grad_" + n], out["delta_" + n], out["new_m_" + n], out["new_v_" + n] = (view(n, a) for a in res)
        return res[-1]

    small_full = {
        "ev_gla_w_gate": d_gla_wg[:GATE_RANK][None], "ev_lru_conv_w": d_lru_cw[None], "od_norm": d_od_norm,
        "od_conv_w": d_od_cw[None], "od_conv_b": d_od_cb, "od_gnorm": d_od_gn,
        "ev_norm": d_ev_norm, "ev_gla_b_gate": d_bg, "ev_gla_w_onorm": d_wn, "ev_lru_conv_b": d_lru_cb,
        "ev_lru_w_a": d_wa[None], "ev_lru_b_a": d_ba, "ev_lru_w_i": d_wi[None], "ev_lru_b_i": d_bi,
        "ev_lru_lam": d_lam, "od_dt_bias": _ungroup_lanes(cfg, d_dt_bias), "od_a_log": _ungroup_lanes(cfg, d_a_log),
        "od_d_skip": _ungroup_lanes(cfg, d_d_skip), "ffn_norm": jnp.concatenate([d_ffn_norm0, d_ffn_norm1], axis=0),
        "final_norm": d_final_norm.reshape(D),
    }
    small = small_sharded + replicated
    small_packed = pack([small_full[n] for n in small])
    sent_small, token = exchange_start([(small_packed, None)], True,
                                       [_landing_zone(small_packed, None, me, "gather_small_grads_own")],
                                       "gather_small_grads")

    done = update(["od_w_in", "od_w_out"], [sent_od], token, "grads_wait_od")
    done = update(["ffn_w_gate", "ffn_w_up", "ffn_w_down"], [sent_ffn0, sent_ffn1], done, "grads_wait_ffn")
    done = update(["ev_w_in", "ev_w_out"], [sent_ev], done, "grads_wait_ev")

    small_all = exchange_wait(sent_small, done, "gather_small_grads_wait")[1][0]
    g_small = dict(zip(small, unpack(reduce_slots(small_all, "sum_small_grads"), [small_full[n].shape for n in small])))
    for n in small_sharded:
        width = p[n].shape[-1]
        g_small[n] = lax.dynamic_slice_in_dim(g_small[n], me * width, width, axis=g_small[n].ndim - 1)
    shapes = [p[n].shape for n in small]
    g_buf = pack([g_small[n] for n in small])
    delta, new_m, new_v = adamw(pack([p[n] for n in small]), g_buf, pack([p["m_" + n] for n in small]),
                                pack([p["v_" + n] for n in small]), "adamw_small")
    for kind, buf in (("grad_", g_buf), ("delta_", delta), ("new_m_", new_m), ("new_v_", new_v)):
        for n, a in zip(small, unpack(buf, shapes)):
            out[kind + n] = a
    return out


WEIGHTS = ['ev_norm', 'ev_w_in', 'ev_gla_w_gate', 'ev_gla_b_gate', 'ev_gla_w_onorm', 'ev_lru_conv_w', 'ev_lru_conv_b',
           'ev_lru_w_a', 'ev_lru_b_a', 'ev_lru_w_i', 'ev_lru_b_i', 'ev_lru_lam', 'ev_w_out', 'od_norm', 'od_w_in',
           'od_conv_w', 'od_conv_b', 'od_dt_bias', 'od_a_log', 'od_d_skip', 'od_gnorm', 'od_w_out', 'ffn_norm',
           'ffn_w_gate', 'ffn_w_up', 'ffn_w_down', 'final_norm']


def kernel(x, ev_norm, ev_w_in, ev_gla_w_gate, ev_gla_b_gate, ev_gla_w_onorm, ev_lru_conv_w, ev_lru_conv_b, ev_lru_w_a, ev_lru_b_a, ev_lru_w_i, ev_lru_b_i, ev_lru_lam, ev_w_out, od_norm, od_w_in, od_conv_w, od_conv_b, od_dt_bias, od_a_log, od_d_skip, od_gnorm, od_w_out, ffn_norm, ffn_w_gate, ffn_w_up, ffn_w_down, final_norm, loss_target, m_ev_norm, m_ev_w_in, m_ev_gla_w_gate, m_ev_gla_b_gate, m_ev_gla_w_onorm, m_ev_lru_conv_w, m_ev_lru_conv_b, m_ev_lru_w_a, m_ev_lru_b_a, m_ev_lru_w_i, m_ev_lru_b_i, m_ev_lru_lam, m_ev_w_out, m_od_norm, m_od_w_in, m_od_conv_w, m_od_conv_b, m_od_dt_bias, m_od_a_log, m_od_d_skip, m_od_gnorm, m_od_w_out, m_ffn_norm, m_ffn_w_gate, m_ffn_w_up, m_ffn_w_down, m_final_norm, v_ev_norm, v_ev_w_in, v_ev_gla_w_gate, v_ev_gla_b_gate, v_ev_gla_w_onorm, v_ev_lru_conv_w, v_ev_lru_conv_b, v_ev_lru_w_a, v_ev_lru_b_a, v_ev_lru_w_i, v_ev_lru_b_i, v_ev_lru_lam, v_ev_w_out, v_od_norm, v_od_w_in, v_od_conv_w, v_od_conv_b, v_od_dt_bias, v_od_a_log, v_od_d_skip, v_od_gnorm, v_od_w_out, v_ffn_norm, v_ffn_w_gate, v_ffn_w_up, v_ffn_w_down, v_final_norm):
    args = dict(locals())
    p = {n: a for n, a in args.items() if n != "loss_target"}
    cfg = Cfg(S=x.shape[1], D=x.shape[2], DFF=ffn_w_gate.shape[2] * N_DEV)
    out = train_step(cfg, p, loss_target)
    return (out["loss"], out["grad_x"], *[out["grad_" + w] for w in WEIGHTS], *[out["delta_" + w] for w in WEIGHTS],
            *[out["new_m_" + w] for w in WEIGHTS], *[out["new_v_" + w] for w in WEIGHTS])
```

```python
import functools
import math
from typing import NamedTuple

import jax
import jax.numpy as jnp
from jax import lax
from jax.experimental import pallas as pl
from jax.experimental.pallas import tpu as pltpu

F32 = jnp.float32
BF16 = jnp.bfloat16
MESH_AXES = ("x", "y", "c")
N_DEV = 8
LANES = 128
SUBLANES = 8
VMEM_LIMIT = 56 * 1024 * 1024

NORM_EPS = 1e-6
CONV_WIDTH = 4
CHUNK = 64
HEAD_K = 128
HEAD_V = 256
GATE_RANK = 16
GATE_NORM = 16.0
LRU_BLOCK = 128
LRU_C = 8.0
SSD_P = 64
SSD_N = 128
SSD_HG = 8
SSD_GW = SSD_HG * SSD_P

ADAM_LR = 0.001
ADAM_B1 = 0.9
ADAM_B2 = 0.999
ADAM_EPS = 1e-08
ADAM_WD = 0.01
ADAM_STEP = 10


class Cfg(NamedTuple):
    S: int
    D: int
    DFF: int

    @property
    def GH(self):
        return self.D // 512

    @property
    def NB(self):
        return self.D // 256

    @property
    def NG(self):
        return self.D // 256

    @property
    def DK(self):
        return HEAD_K * self.GH

    @property
    def DV(self):
        return HEAD_V * self.GH

    @property
    def W(self):
        return LRU_BLOCK * self.NB

    @property
    def DI(self):
        return SSD_GW * self.NG

    @property
    def CD(self):
        return self.DI + 2 * self.NG * SSD_N

    @property
    def NH(self):
        return SSD_HG * self.NG

    @property
    def EVEN_IN(self):
        return 2 * self.DK + 2 * self.DV + GATE_RANK + 2 * self.W

    @property
    def ODD_IN(self):
        return self.DI + self.CD + self.NH

    @property
    def EP(self):
        return _round_up(2 * self.DK + 2 * self.DV + 2 * self.W + LANES, 768)

    @property
    def OP(self):
        return _round_up(self.DI + self.CD + LANES, 768)


def _round_up(n, m):
    return (n + m - 1) // m * m


def _tile(n, pref):
    if n <= pref:
        return n
    t = pref - pref % LANES
    while n % t:
        t -= LANES
    return t


def _cparams(n_axes):
    return pltpu.CompilerParams(dimension_semantics=("arbitrary",) * n_axes, vmem_limit_bytes=VMEM_LIMIT)


def _dg(a, b, ca, cb):
    return lax.dot_general(a.astype(BF16), b.astype(BF16), (((ca,), (cb,)), ((), ())), preferred_element_type=F32)


@functools.partial(jax.custom_vjp, nondiff_argnums=(2, 3))
def bdot(a, b, ca, cb):
    return _dg(a, b, ca, cb)


def _bdot_fwd(a, b, ca, cb):
    return _dg(a, b, ca, cb), (a, b)


def _bdot_bwd(ca, cb, res, g):
    a, b = res
    da = _dg(g, b, 1, 1 - cb) if ca == 1 else _dg(b, g, 1 - cb, 1)
    db = _dg(a, g, 1 - ca, 0) if cb == 0 else _dg(g, a, 0, 1 - ca)
    return da.astype(a.dtype), db.astype(b.dtype)


bdot.defvjp(_bdot_fwd, _bdot_bwd)


def _lower_tri(n):
    r = lax.broadcasted_iota(jnp.int32, (n, n), 0)
    c = lax.broadcasted_iota(jnp.int32, (n, n), 1)
    return c <= r


def _running_sum(x, reverse):
    n = x.shape[0]
    r = lax.broadcasted_iota(jnp.int32, x.shape, 0)
    d = 1
    while d < n:
        if reverse:
            x = x + jnp.where(r < n - d, pltpu.roll(x, n - d, 0), 0.0)
        else:
            x = x + jnp.where(r >= d, pltpu.roll(x, d, 0), 0.0)
        d *= 2
    return x


@jax.custom_vjp
def cumsum_rows(x):
    return _running_sum(x, False)


cumsum_rows.defvjp(lambda x: (_running_sum(x, False), None), lambda _, g: (_running_sum(g, True),))


def _tri_dot(x, transposed):
    n = x.shape[0]
    return lax.dot_general(_lower_tri(n).astype(F32), x, (((0 if transposed else 1,), (0,)), ((), ())),
                           precision=lax.Precision.HIGHEST, preferred_element_type=F32)


@jax.custom_vjp
def cumsum_rows_mxu(x):
    return _tri_dot(x, False)


cumsum_rows_mxu.defvjp(lambda x: (_tri_dot(x, False), None), lambda _, g: (_tri_dot(g, True),))


def _row(x, i):
    r = lax.broadcasted_iota(jnp.int32, x.shape, 0)
    return jnp.sum(jnp.where(r == i, x, 0.0), axis=0, keepdims=True)


def _softplus_raw(x):
    return jnp.maximum(x, 0.0) + jnp.log(1.0 + jnp.exp(-jnp.abs(x)))


@jax.custom_vjp
def softplus(x):
    return _softplus_raw(x)


softplus.defvjp(lambda x: (_softplus_raw(x), x), lambda x, g: (g * jax.nn.sigmoid(x),))


@jax.custom_vjp
def log_sigmoid(x):
    return -_softplus_raw(-x)


log_sigmoid.defvjp(lambda x: (-_softplus_raw(-x), x), lambda x, g: (g * jax.nn.sigmoid(-x),))


def silu(x):
    return x * jax.nn.sigmoid(x)


def gelu_tanh(x):
    return 0.5 * x * (1.0 + jnp.tanh(math.sqrt(2.0 / math.pi) * (x + 0.044715 * (x * x * x))))


def _expm1(x):
    series = x * (1.0 + 0.5 * x * (1.0 + (1.0 / 3.0) * x))
    return jnp.where(jnp.abs(x) < 1e-2, series, jnp.exp(x) - 1.0)


def rms(x, w):
    return x * lax.rsqrt(jnp.mean(x * x, axis=-1, keepdims=True) + NORM_EPS) * w


def _rows_iota(shape):
    return lax.broadcasted_iota(jnp.int32, shape, 0)


def _scan_up(a, u):
    n = a.shape[0]
    r = _rows_iota(a.shape)
    d = 1
    while d < n:
        m = r >= d
        a_s = jnp.where(m, pltpu.roll(a, d, 0), 1.0)
        u_s = jnp.where(m, pltpu.roll(u, d, 0), 0.0)
        u = a * u_s + u
        a = a * a_s
        d *= 2
    return u


def _scan_down(a, u):
    n = a.shape[0]
    r = _rows_iota(a.shape)
    d = 1
    while d < n:
        m = r < n - d
        a_s = jnp.where(m, pltpu.roll(a, n - d, 0), 1.0)
        u_s = jnp.where(m, pltpu.roll(u, n - d, 0), 0.0)
        u = a * u_s + u
        a = a * a_s
        d *= 2
    return u


@jax.custom_vjp
def lin_scan(a, u):
    return _scan_up(a, u)


def _lin_scan_fwd(a, u):
    h = _scan_up(a, u)
    return h, (a, h)


def _lin_scan_bwd(res, g):
    a, h = res
    n = a.shape[0]
    r = _rows_iota(a.shape)
    a_next = jnp.where(r < n - 1, pltpu.roll(a, n - 1, 0), 0.0)
    gt = _scan_down(a_next, g)
    h_prev = jnp.where(r >= 1, pltpu.roll(h, 1, 0), 0.0)
    return gt * h_prev, gt


lin_scan.defvjp(_lin_scan_fwd, _lin_scan_bwd)


def _expand_heads(v):
    r = v.shape[0]
    return jnp.concatenate([jnp.broadcast_to(v[:, h:h + 1], (r, SSD_P)) for h in range(SSD_HG)], axis=1)


@jax.custom_vjp
def _split_heads(x):
    return tuple(x[:, h * SSD_P:(h + 1) * SSD_P] for h in range(SSD_HG))


_split_heads.defvjp(lambda x: (_split_heads(x), None), lambda _, gs: (jnp.concatenate(gs, axis=1),))


def matmul(a, b, *, ta=False, tb=False, a_slot=None, b_slot=None, b_lead=(), res=None, after=None, out_dtype=F32,
           name, tm=1024, tn=1024, tk=2048):
    lead = tuple(b_lead)
    ra, ca_ = a.shape[-2:]
    rb, cb_ = b.shape[-2:]
    m_st, ka_st = (ca_, ra) if ta else (ra, ca_)
    kb_st, n_st = (cb_, rb) if tb else (rb, cb_)
    kslot = a_slot == "k"
    assert kslot == (b_slot == "k")
    m = m_st * (N_DEV if a_slot == "m" else 1)
    n = n_st * (N_DEV if b_slot == "n" else 1)
    assert ka_st == kb_st, (a.shape, b.shape, ta, tb)
    tm = m_st if a_slot == "m" else _tile(m, tm)
    tn = n_st if b_slot == "n" else _tile(n, tn)
    tk = ka_st if kslot else _tile(ka_st, tk)
    nk = ka_st // tk
    ca, cb = (0 if ta else 1), (1 if tb else 0)
    nl = (None,) * len(lead)

    if a_slot is None:
        a_spec = pl.BlockSpec((tk, tm), lambda i, j, k: (k, i)) if ta else pl.BlockSpec((tm, tk), lambda i, j, k: (i, k))
    elif a_slot == "m":
        a_spec = (pl.BlockSpec((None, tk, tm), lambda i, j, k: (i, k, 0)) if ta
                  else pl.BlockSpec((None, tm, tk), lambda i, j, k: (i, 0, k)))
    else:
        a_spec = (pl.BlockSpec((N_DEV, tk, tm), lambda i, j, k: (0, 0, i)) if ta
                  else pl.BlockSpec((N_DEV, tm, tk), lambda i, j, k: (0, i, 0)))
    if b_slot is None:
        b_spec = (pl.BlockSpec(nl + (tn, tk), lambda i, j, k: lead + (j, k)) if tb
                  else pl.BlockSpec(nl + (tk, tn), lambda i, j, k: lead + (k, j)))
    elif b_slot == "n":
        b_spec = (pl.BlockSpec((None,) + nl + (tn, tk), lambda i, j, k: (j,) + lead + (0, k)) if tb
                  else pl.BlockSpec((None,) + nl + (tk, tn), lambda i, j, k: (j,) + lead + (k, 0)))
    else:
        b_spec = (pl.BlockSpec((N_DEV,) + nl + (tn, tk), lambda i, j, k: (0,) + lead + (j, 0)) if tb
                  else pl.BlockSpec((N_DEV,) + nl + (tk, tn), lambda i, j, k: (0,) + lead + (0, j)))
    if a_slot == "m":
        o_spec, o_shape = pl.BlockSpec((None, tm, tn), lambda i, j, k: (i, 0, j)), (N_DEV, tm, n)
    elif b_slot == "n":
        o_spec, o_shape = pl.BlockSpec((None, tm, tn), lambda i, j, k: (j, i, 0)), (N_DEV, m, tn)
    else:
        o_spec, o_shape = pl.BlockSpec((tm, tn), lambda i, j, k: (i, j)), (m, n)
    assert res is None or (a_slot != "m" and b_slot != "n")

    def dot(x, y):
        return lax.dot_general(x.astype(BF16), y.astype(BF16), (((ca,), (cb,)), ((), ())), preferred_element_type=F32)

    def body(*refs):
        a_ref, b_ref = refs[:2]
        r_ref = refs[2] if res is not None else None
        o_ref = refs[2 + (res is not None) + (after is not None)]

        def finish(acc):
            if r_ref is not None:
                acc = acc + r_ref[...].astype(F32)
            o_ref[...] = acc.astype(o_ref.dtype)

        if kslot:
            acc = dot(a_ref[0], b_ref[0])
            for s in range(1, N_DEV):
                acc = acc + dot(a_ref[s], b_ref[s])
            finish(acc)
        elif nk == 1:
            finish(dot(a_ref[...], b_ref[...]))
        else:
            acc_ref = refs[-1]
            k = pl.program_id(2)

            @pl.when(k == 0)
            def _():
                acc_ref[...] = dot(a_ref[...], b_ref[...])

            @pl.when(k > 0)
            def _():
                acc_ref[...] += dot(a_ref[...], b_ref[...])

            @pl.when(k == nk - 1)
            def _():
                finish(acc_ref[...])

    in_specs = [a_spec, b_spec]
    args = [a, b]
    if res is not None:
        in_specs.append(pl.BlockSpec((tm, tn), lambda i, j, k: (i, j)))
        args.append(res)
    if after is not None:
        in_specs.append(pl.BlockSpec(memory_space=pl.ANY))
        args.append(after)
    return pl.pallas_call(
        body, name=name, grid=(m // tm, n // tn, nk), in_specs=in_specs, out_specs=o_spec,
        out_shape=jax.ShapeDtypeStruct(o_shape, out_dtype),
        scratch_shapes=[pltpu.VMEM((tm, tn), F32)] if nk > 1 else [], compiler_params=_cparams(3),
    )(*args)


def seq_call(name, fn, grid, ins, outs, accs=(), carries=()):
    n_in, n_out, n_acc = len(ins), len(outs), len(accs)

    def body(*refs):
        in_refs = refs[:n_in]
        out_refs = refs[n_in:n_in + n_out]
        acc_refs = refs[n_in + n_out:n_in + n_out + n_acc]
        c_refs = refs[n_in + n_out + n_acc:]

        if acc_refs or c_refs:
            @pl.when(pl.program_id(1) == 0)
            def _():
                for r in tuple(acc_refs) + tuple(c_refs):
                    r[...] = jnp.zeros_like(r)

        o, a, c = fn([r[...] for r in in_refs], [r[...] for r in c_refs])
        for r, v in zip(out_refs, o, strict=True):
            r[...] = v.astype(r.dtype)
        for r, v in zip(acc_refs, a, strict=True):
            r[...] += v
        for r, v in zip(c_refs, c, strict=True):
            r[...] = v

    return pl.pallas_call(
        body, name=name, grid=grid,
        in_specs=[pl.BlockSpec(blk, im) for _, blk, im in ins],
        out_specs=[pl.BlockSpec(blk, im) for _, _, blk, im in outs] + [pl.BlockSpec(blk, im) for _, blk, im in accs],
        out_shape=[jax.ShapeDtypeStruct(s, d) for s, d, _, _ in outs] + [jax.ShapeDtypeStruct(s, F32) for s, _, _ in accs],
        scratch_shapes=[pltpu.VMEM(s, F32) for s in carries], compiler_params=_cparams(2),
    )(*[a for a, _, _ in ins])


def exchange(arrays, gather, name, after=None):
    n = len(arrays)
    extra = [] if after is None else [after]

    def body(*refs):
        x_refs, o_refs = refs[:n], refs[n + len(extra):2 * n + len(extra)]
        send_sems, recv_sems, local_sems = refs[2 * n + len(extra):]
        pos = [lax.axis_index(ax) for ax in MESH_AXES]
        me = 4 * pos[0] + 2 * pos[1] + pos[2]
        copies = []
        for i in range(n):
            own = pltpu.make_async_copy(x_refs[i] if gather else x_refs[i].at[me], o_refs[i].at[me], local_sems.at[i])
            own.start()
            copies.append(own)
        for k in range(1, N_DEV):
            bits = ((k >> 2) & 1, (k >> 1) & 1, k & 1)
            peer = tuple(1 - p if b else p for p, b in zip(pos, bits))
            peer_id = 4 * peer[0] + 2 * peer[1] + peer[2]
            for i in range(n):
                cp = pltpu.make_async_remote_copy(
                    src_ref=x_refs[i] if gather else x_refs[i].at[peer_id], dst_ref=o_refs[i].at[me],
                    send_sem=send_sems.at[i * (N_DEV - 1) + k - 1], recv_sem=recv_sems.at[i * (N_DEV - 1) + k - 1],
                    device_id=peer, device_id_type=pl.DeviceIdType.MESH)
                cp.start()
                copies.append(cp)
        for cp in copies:
            cp.wait()

    hbm = pl.BlockSpec(memory_space=pltpu.HBM)
    return pl.pallas_call(
        body, name=name, in_specs=[hbm] * n + [pl.BlockSpec(memory_space=pl.ANY)] * len(extra), out_specs=[hbm] * n,
        out_shape=[jax.ShapeDtypeStruct(((N_DEV,) + a.shape) if gather else a.shape, a.dtype) for a in arrays],
        scratch_shapes=[pltpu.SemaphoreType.DMA((n * (N_DEV - 1),)), pltpu.SemaphoreType.DMA((n * (N_DEV - 1),)),
                        pltpu.SemaphoreType.DMA((n,))],
    )(*arrays, *extra)


_HBM = pl.BlockSpec(memory_space=pltpu.HBM)
_SEM = pl.BlockSpec(memory_space=pltpu.SEMAPHORE)
N_PEERS = N_DEV - 1


def _mesh_pos():
    pos = [lax.axis_index(ax) for ax in MESH_AXES]
    return pos, 4 * pos[0] + 2 * pos[1] + pos[2]


ALL_PEERS = (1, 2, 3, 4, 5, 6, 7)
NEAR_PEERS = (1, 2, 4, 6)
RELAYED = (2, 4, 6)


def _peers(pos, masks=ALL_PEERS):
    out = []
    for k in masks:
        bits = ((k >> 2) & 1, (k >> 1) & 1, k & 1)
        peer = tuple(1 - p if b else p for p, b in zip(pos, bits))
        out.append((peer, 4 * peer[0] + 2 * peer[1] + peer[2]))
    return out


def _part(x_ref, lead, gather, slot):
    ref = x_ref if lead is None else x_ref.at[lead]
    return ref if gather else ref.at[slot]


OWN_BLOCK_BYTES = 2 * 1024 * 1024


def _landing_zone(a, lead, me, name):
    r, c = a.shape[-2:]
    tr = r
    while tr * _round_up(c, LANES) * a.dtype.itemsize > OWN_BLOCK_BYTES and tr % 32 == 0:
        tr //= 2

    def body(me_ref, x_ref, o_ref):
        o_ref[...] = x_ref[...]

    x_spec = (pl.BlockSpec((tr, c), lambda i, me_ref: (i, 0)) if lead is None
              else pl.BlockSpec((None, tr, c), lambda i, me_ref: (lead, i, 0)))
    grid_spec = pltpu.PrefetchScalarGridSpec(
        num_scalar_prefetch=1, grid=(r // tr,), in_specs=[x_spec],
        out_specs=pl.BlockSpec((None, tr, c), lambda i, me_ref: (me_ref[0], i, 0)))
    return pl.pallas_call(body, name=name, grid_spec=grid_spec, out_shape=jax.ShapeDtypeStruct((N_DEV, r, c), a.dtype),
                          compiler_params=_cparams(1))(jnp.reshape(me, (1,)).astype(jnp.int32), a)


def _split_copies(items, gather, masks, x_refs, land_refs, send_sems, recv_sems):
    pos, me = _mesh_pos()
    copies = []
    for i, (_, lead) in enumerate(items):
        for k, (peer, peer_id) in enumerate(_peers(pos, masks)):
            copies.append(pltpu.make_async_remote_copy(
                src_ref=_part(x_refs[i], lead, gather, peer_id), dst_ref=land_refs[i].at[me],
                send_sem=send_sems.at[i * len(masks) + k], recv_sem=recv_sems.at[i * len(masks) + k],
                device_id=peer, device_id_type=pl.DeviceIdType.MESH))
    return copies


_SPLIT_CALL = dict(compiler_params=pltpu.CompilerParams(has_side_effects=pltpu.SideEffectType.DATAFLOW_SIDE_EFFECTING))


def exchange_start(items, gather, lands, name, masks=ALL_PEERS):
    n = len(items)
    lands = list(lands)
    xs = [a for a, _ in items]

    def body(*refs):
        x_refs, land_refs = refs[:n], refs[n:2 * n]
        send_sems, recv_sems, token = refs[2 * n], refs[2 * n + 1], refs[-1]
        for cp in _split_copies(items, gather, masks, x_refs, land_refs, send_sems, recv_sems):
            cp.start()
        token[...] = jnp.zeros_like(token)

    outs = pl.pallas_call(
        body, name=name,
        out_shape=(pltpu.SemaphoreType.DMA((n * len(masks),)), pltpu.SemaphoreType.DMA((n * len(masks),)),
                   *[pltpu.HBM(v.shape, v.dtype) for v in xs + lands], jax.ShapeDtypeStruct((SUBLANES, LANES), F32)),
        in_specs=[_HBM] * (2 * n),
        out_specs=(_SEM, _SEM, *[_HBM] * (2 * n), pl.BlockSpec(memory_space=pltpu.VMEM)),
        input_output_aliases={i: 2 + i for i in range(2 * n)}, **_SPLIT_CALL,
    )(*[pltpu.with_memory_space_constraint(v, pltpu.HBM) for v in xs + lands])
    handle = (items, gather, masks, outs[0], outs[1], outs[2:2 + n], outs[2 + n:2 + 2 * n])
    return handle, outs[-1]


def _relay_copies(n, land_refs, send_sems, recv_sems):
    pos, _ = _mesh_pos()
    sibling = (pos[0], pos[1], 1 - pos[2])
    copies = []
    for i in range(n):
        for k, (_, peer_id) in enumerate(_peers(pos, RELAYED)):
            slot = land_refs[i].at[peer_id]
            copies.append(pltpu.make_async_remote_copy(
                src_ref=slot, dst_ref=slot, send_sem=send_sems.at[i * len(RELAYED) + k],
                recv_sem=recv_sems.at[i * len(RELAYED) + k], device_id=sibling, device_id_type=pl.DeviceIdType.MESH))
    return copies


def relay(lands, name):
    n = len(lands)
    sems = n * len(RELAYED)

    def start(*refs):
        for cp in _relay_copies(n, refs[:n], refs[n], refs[n + 1]):
            cp.start()

    outs = pl.pallas_call(
        start, name=name + "_start",
        out_shape=(pltpu.SemaphoreType.DMA((sems,)), pltpu.SemaphoreType.DMA((sems,)),
                   *[pltpu.HBM(v.shape, v.dtype) for v in lands]),
        in_specs=[_HBM] * n, out_specs=(_SEM, _SEM, *[_HBM] * n), input_output_aliases={i: 2 + i for i in range(n)},
        **_SPLIT_CALL)(*[pltpu.with_memory_space_constraint(v, pltpu.HBM) for v in lands])

    def wait(*refs):
        for cp in _relay_copies(n, refs[:n], refs[n], refs[n + 1]):
            cp.wait_send()
            cp.wait_recv()

    return list(pl.pallas_call(
        wait, name=name + "_wait", out_shape=tuple(pltpu.HBM(v.shape, v.dtype) for v in lands),
        in_specs=[_HBM] * n + [_SEM, _SEM], out_specs=tuple([_HBM] * n), input_output_aliases={i: i for i in range(n)},
        **_SPLIT_CALL)(*outs[2:], outs[0], outs[1]))


def exchange_wait(handle, after, name):
    items, gather, masks, send_sems, recv_sems, x_thru, land_thru = handle
    n = len(items)

    def body(*refs):
        x_refs, land_refs = refs[:n], refs[n:2 * n]
        for cp in _split_copies(items, gather, masks, x_refs, land_refs, refs[2 * n], refs[2 * n + 1]):
            cp.wait_send()
            cp.wait_recv()

    outs = pl.pallas_call(
        body, name=name, out_shape=tuple(pltpu.HBM(v.shape, v.dtype) for v in tuple(x_thru) + tuple(land_thru)),
        in_specs=[_HBM] * (2 * n) + [_SEM, _SEM, pl.BlockSpec(memory_space=pl.ANY)], out_specs=tuple([_HBM] * (2 * n)),
        input_output_aliases={i: i for i in range(2 * n)},
        compiler_params=pltpu.CompilerParams(has_side_effects=pltpu.SideEffectType.DATAFLOW_SIDE_EFFECTING),
    )(*x_thru, *land_thru, send_sems, recv_sems, after)
    return list(outs[:n]), list(outs[n:])


def _adam_update(w, g, m, v):
    nm = ADAM_B1 * m + (1.0 - ADAM_B1) * g
    nv = ADAM_B2 * v + (1.0 - ADAM_B2) * (g * g)
    m_hat = nm / (1.0 - ADAM_B1 ** ADAM_STEP)
    v_hat = nv / (1.0 - ADAM_B2 ** ADAM_STEP)
    return -ADAM_LR * (m_hat / (jnp.sqrt(v_hat) + ADAM_EPS) + ADAM_WD * w), nm, nv


def _sum_slots(s_ref):
    acc = s_ref[0].astype(F32)
    for j in range(1, N_DEV):
        acc = acc + s_ref[j].astype(F32)
    return acc


ADAM_BLOCK_BYTES = 10 * 1024 * 1024


def adamw_sharded(recvs, sends, me, w, m, v, name):
    nl, r, c = w.shape
    assert len(recvs) == nl and len(sends) == nl
    per_row = _round_up(c, LANES) * (nl * (N_DEV + 1) * recvs[0].dtype.itemsize + 7 * 4)
    tr = r
    while tr * per_row > ADAM_BLOCK_BYTES and tr % 16 == 0:
        tr //= 2

    def body(me_ref, *refs):
        s_refs, o_refs = refs[:nl], refs[nl:2 * nl]
        w_ref, m_ref, v_ref, g_ref, d_ref, nm_ref, nv_ref = refs[2 * nl:]
        mine = me_ref[0]

        def total(l):
            acc = jnp.where(mine == 0, o_refs[l][...], s_refs[l][0]).astype(F32)
            for j in range(1, N_DEV):
                acc = acc + jnp.where(mine == j, o_refs[l][...], s_refs[l][j]).astype(F32)
            return acc

        g = total(0)
        for l in range(1, nl):
            g = jnp.where(pl.program_id(0) == l, total(l), g)
        g_ref[...] = g
        d_ref[...], nm_ref[...], nv_ref[...] = _adam_update(w_ref[...], g, m_ref[...], v_ref[...])

    spec = pl.BlockSpec((None, tr, c), lambda l, i, me_ref: (l, i, 0))
    shp = jax.ShapeDtypeStruct(w.shape, F32)
    grid_spec = pltpu.PrefetchScalarGridSpec(
        num_scalar_prefetch=1, grid=(nl, r // tr),
        in_specs=([pl.BlockSpec((N_DEV, tr, c), lambda l, i, me_ref: (0, i, 0))] * nl
                  + [pl.BlockSpec((None, tr, c), lambda l, i, me_ref: (me_ref[0], i, 0))] * nl + [spec, spec, spec]),
        out_specs=[spec] * 4)
    return pl.pallas_call(body, name=name, grid_spec=grid_spec, out_shape=[shp] * 4, compiler_params=_cparams(2))(
        jnp.reshape(me, (1,)).astype(jnp.int32), *recvs, *sends, w, m, v)


def reduce_slots(slots, name):
    _, r, lanes = slots.shape
    tr = _tile(r, 2048)

    def body(s_ref, o_ref):
        o_ref[...] = _sum_slots(s_ref)

    return pl.pallas_call(
        body, name=name, grid=(r // tr,),
        in_specs=[pl.BlockSpec((N_DEV, tr, lanes), lambda i: (0, i, 0))],
        out_specs=pl.BlockSpec((tr, lanes), lambda i: (i, 0)),
        out_shape=jax.ShapeDtypeStruct((r, lanes), F32), compiler_params=_cparams(1),
    )(slots)


def adamw(w, g, m, v, name):
    r, lanes = w.shape
    tr = _tile(r, 2048)

    def body(w_ref, g_ref, m_ref, v_ref, d_ref, nm_ref, nv_ref):
        d_ref[...], nm_ref[...], nv_ref[...] = _adam_update(w_ref[...], g_ref[...], m_ref[...], v_ref[...])

    spec = pl.BlockSpec((tr, lanes), lambda i: (i, 0))
    shp = jax.ShapeDtypeStruct((r, lanes), F32)
    return pl.pallas_call(body, name=name, grid=(r // tr,), in_specs=[spec] * 4, out_specs=[spec] * 3,
                          out_shape=[shp] * 3, compiler_params=_cparams(1))(w, g, m, v)


def cols_from_slots(slots, place, width, name):
    _, rows, c = slots.shape
    tr = _tile(rows, 256)

    def body(s_ref, o_ref):
        o_ref[...] = place(jnp.concatenate([s_ref[j] for j in range(N_DEV)], axis=1))

    return pl.pallas_call(
        body, name=name, grid=(rows // tr,),
        in_specs=[pl.BlockSpec((N_DEV, tr, c), lambda i: (0, i, 0))],
        out_specs=pl.BlockSpec((tr, width), lambda i: (i, 0)),
        out_shape=jax.ShapeDtypeStruct((rows, width), slots.dtype), compiler_params=_cparams(1))(slots)


def slots_from_cols(full, pick, c, name):
    rows, wide = full.shape
    tr = _tile(rows, 256)

    def body(x_ref, o_ref):
        v = pick(x_ref[...])
        for j in range(N_DEV):
            o_ref[j] = v[:, j * c:(j + 1) * c]

    return pl.pallas_call(
        body, name=name, grid=(rows // tr,),
        in_specs=[pl.BlockSpec((tr, wide), lambda i: (i, 0))],
        out_specs=pl.BlockSpec((N_DEV, tr, c), lambda i: (0, i, 0)),
        out_shape=jax.ShapeDtypeStruct((N_DEV, rows, c), full.dtype), compiler_params=_cparams(1))(full)


def _token_tile(cfg):
    return min(cfg.S, 256)


def norm_fwd(cfg, x, w, name):
    ts = _token_tile(cfg)
    d = x.shape[1]

    def fn(ins, _):
        xv, wv = ins
        return [rms(xv, wv)], [], []

    return seq_call(name, fn, (1, cfg.S // ts),
                    [(x, (ts, d), lambda g, t: (t, 0)), (w, (1, d), lambda g, t: (0, 0))],
                    [((cfg.S, d), BF16, (ts, d), lambda g, t: (t, 0))])[0]


def norm_bwd(cfg, x, w, dh, dres, name):
    ts = _token_tile(cfg)
    d = x.shape[1]

    def fn(ins, _):
        xv, wv, dhv, drv = ins
        _, vjp = jax.vjp(rms, xv, wv)
        dx, dw = vjp(dhv.astype(F32))
        return [dx + drv], [dw], []

    row = lambda g, t: (t, 0)
    out = seq_call(name, fn, (1, cfg.S // ts),
                   [(x, (ts, d), row), (w, (1, d), lambda g, t: (0, 0)), (dh, (ts, d), row), (dres, (ts, d), row)],
                   [((cfg.S, d), F32, (ts, d), row)],
                   accs=[((1, d), (1, d), lambda g, t: (0, 0))])
    return out[0], out[1]


def head_fwd_bwd(cfg, x, w, target, name):
    ts = _token_tile(cfg)
    d = x.shape[1]

    def fn(ins, _):
        xv, wv, tv = ins
        y, vjp = jax.vjp(rms, xv, wv)
        err = y - tv
        loss = 0.5 * jnp.sum(err * err) / d
        dx, dw = vjp(err / d)
        return [dx], [jnp.full((SUBLANES, LANES), loss, F32), dw], []

    row = lambda g, t: (t, 0)
    fixed = lambda g, t: (0, 0)
    dx, loss, dw = seq_call(name, fn, (1, cfg.S // ts),
                            [(x, (ts, d), row), (w, (1, d), fixed), (target, (ts, d), row)],
                            [((cfg.S, d), F32, (ts, d), row)],
                            accs=[((SUBLANES, LANES), (SUBLANES, LANES), fixed), ((1, d), (1, d), fixed)])
    return loss, dx, dw


FFN_ROWS = 1024


def ffn_gate_up(h, w_gate, w_up, name):
    s, d = h.shape
    c = w_gate.shape[1]
    tm = _tile(s, FFN_ROWS)

    def body(h_ref, wg_ref, wu_ref, g_ref, u_ref, a_ref):
        hv = h_ref[...]
        g = _dg(hv, wg_ref[...], 1, 1)
        u = _dg(hv, wu_ref[...], 1, 1)
        g_ref[...] = g
        u_ref[...] = u
        a_ref[...] = (silu(g) * u).astype(a_ref.dtype)

    w_spec = pl.BlockSpec((None, c, d), lambda i, j: (j, 0, 0))
    o_spec = pl.BlockSpec((None, tm, c), lambda i, j: (j, i, 0))
    shp = (N_DEV, s, c)
    return pl.pallas_call(
        body, name=name, grid=(s // tm, N_DEV), in_specs=[pl.BlockSpec((tm, d), lambda i, j: (i, 0)), w_spec, w_spec],
        out_specs=[o_spec] * 3,
        out_shape=[jax.ShapeDtypeStruct(shp, F32), jax.ShapeDtypeStruct(shp, F32), jax.ShapeDtypeStruct(shp, BF16)],
        compiler_params=_cparams(2))(h, w_gate, w_up)


def ffn_dgate_dup(dx, w_down, gate, up, after, name):
    s, d = dx.shape
    c = w_down.shape[1]
    tm = _tile(s, FFN_ROWS)
    extra = [] if after is None else [after]

    def body(dx_ref, wd_ref, g_ref, u_ref, *rest):
        dg_ref, du_ref = rest[len(extra):]
        dact = _dg(dx_ref[...], wd_ref[...], 1, 1)
        _, vjp = jax.vjp(lambda a, b: silu(a) * b, g_ref[...], u_ref[...])
        dg, du = vjp(dact)
        dg_ref[...] = dg.astype(dg_ref.dtype)
        du_ref[...] = du.astype(du_ref.dtype)

    blk = pl.BlockSpec((None, tm, c), lambda i, j: (j, i, 0))
    shp = jax.ShapeDtypeStruct((N_DEV, s, c), BF16)
    return pl.pallas_call(
        body, name=name, grid=(s // tm, N_DEV),
        in_specs=[pl.BlockSpec((tm, d), lambda i, j: (i, 0)), pl.BlockSpec((None, c, d), lambda i, j: (j, 0, 0)), blk, blk]
        + [pl.BlockSpec(memory_space=pl.ANY)] * len(extra),
        out_specs=[blk, blk], out_shape=[shp, shp], compiler_params=_cparams(2))(dx, w_down, gate, up, *extra)


CONV_COLS = (512, 256)
HALO = SUBLANES


def _conv_cols(col0, width):
    return next(c for c in CONV_COLS if col0 % c == 0 and width % c == 0)


def _shift_down(x, halo, j):
    if j == 0:
        return x
    r8 = _rows_iota(halo.shape)
    top = jnp.where(r8 >= j, pltpu.roll(x[:HALO], j, 0), pltpu.roll(halo, j, 0))
    return jnp.concatenate([top, pltpu.roll(x, j, 0)[HALO:]], axis=0)


def _shift_up(x, halo, j):
    if j == 0:
        return x
    n = x.shape[0]
    r8 = _rows_iota(halo.shape)
    bot = jnp.where(r8 < HALO - j, pltpu.roll(x[n - HALO:], HALO - j, 0), pltpu.roll(halo, HALO - j, 0))
    return jnp.concatenate([pltpu.roll(x, n - j, 0)[:n - HALO], bot], axis=0)


def _conv_tile(cfg):
    return min(cfg.S, 1024)


def conv_fwd(cfg, src, col0, width, w, b, name):
    tt, cb = _conv_tile(cfg), _conv_cols(col0, width)
    c0, hb = col0 // cb, tt // HALO
    nt = cfg.S // tt

    def body(x_ref, h_ref, w_ref, b_ref, o_ref):
        t = pl.program_id(1)
        x = x_ref[...]
        halo = jnp.where(t > 0, h_ref[...], 0.0)
        wv = w_ref[...]
        acc = b_ref[...] + wv[CONV_WIDTH - 1:CONV_WIDTH] * x
        for j in range(1, CONV_WIDTH):
            acc = acc + wv[CONV_WIDTH - 1 - j:CONV_WIDTH - j] * _shift_down(x, halo, j)
        o_ref[...] = acc

    return pl.pallas_call(
        body, name=name, grid=(width // cb, nt),
        in_specs=[pl.BlockSpec((tt, cb), lambda c, t: (t, c0 + c)),
                  pl.BlockSpec((HALO, cb), lambda c, t: (jnp.maximum(t * hb - 1, 0), c0 + c)),
                  pl.BlockSpec((CONV_WIDTH, cb), lambda c, t: (0, c)),
                  pl.BlockSpec((1, cb), lambda c, t: (0, c))],
        out_specs=pl.BlockSpec((tt, cb), lambda c, t: (t, c)),
        out_shape=jax.ShapeDtypeStruct((cfg.S, width), F32), compiler_params=_cparams(2),
    )(src, src, w, b)


def conv_bwd(cfg, src, col0, width, w, dy, name):
    tt, cb = _conv_tile(cfg), _conv_cols(col0, width)
    c0, hb = col0 // cb, tt // HALO
    nt = cfg.S // tt

    def body(x_ref, h_ref, w_ref, dy_ref, dh_ref, dx_ref, dw_ref, db_ref):
        t = pl.program_id(1)

        @pl.when(t == 0)
        def _():
            dw_ref[...] = jnp.zeros_like(dw_ref)
            db_ref[...] = jnp.zeros_like(db_ref)

        x = x_ref[...]
        halo = jnp.where(t > 0, h_ref[...], 0.0)
        dy = dy_ref[...]
        dhalo = jnp.where(t < nt - 1, dh_ref[...], 0.0)
        wv = w_ref[...]
        dx = wv[CONV_WIDTH - 1:CONV_WIDTH] * dy
        rows = [jnp.sum(dy * x, axis=0, keepdims=True)]
        for j in range(1, CONV_WIDTH):
            dx = dx + wv[CONV_WIDTH - 1 - j:CONV_WIDTH - j] * _shift_up(dy, dhalo, j)
            rows.insert(0, jnp.sum(dy * _shift_down(x, halo, j), axis=0, keepdims=True))
        dx_ref[...] = dx.astype(dx_ref.dtype)
        dw_ref[...] += jnp.concatenate(rows, axis=0)
        db_ref[...] += jnp.sum(dy, axis=0, keepdims=True)

    return pl.pallas_call(
        body, name=name, grid=(width // cb, nt),
        in_specs=[pl.BlockSpec((tt, cb), lambda c, t: (t, c0 + c)),
                  pl.BlockSpec((HALO, cb), lambda c, t: (jnp.maximum(t * hb - 1, 0), c0 + c)),
                  pl.BlockSpec((CONV_WIDTH, cb), lambda c, t: (0, c)),
                  pl.BlockSpec((tt, cb), lambda c, t: (t, c)),
                  pl.BlockSpec((HALO, cb), lambda c, t: (jnp.minimum((t + 1) * hb, nt * hb - 1), c))],
        out_specs=[pl.BlockSpec((tt, cb), lambda c, t: (t, c)),
                   pl.BlockSpec((CONV_WIDTH, cb), lambda c, t: (0, c)),
                   pl.BlockSpec((1, cb), lambda c, t: (0, c))],
        out_shape=[jax.ShapeDtypeStruct((cfg.S, width), BF16), jax.ShapeDtypeStruct((CONV_WIDTH, width), F32),
                   jax.ShapeDtypeStruct((1, width), F32)],
        compiler_params=_cparams(2),
    )(src, src, w, dy, dy)


def _gla_core(gh, q, k, v, g, glr, wg, bg, wn, st):
    n = glr.shape[0]
    causal = _lower_tri(n)
    outs, new = [], []
    for h in range(gh):
        log_a = log_sigmoid(bdot(glr, wg[h], 1, 0) + bg[h]) * (1.0 / GATE_NORM)
        bcum = cumsum_rows(log_a)
        b_last, b_mid = _row(bcum, n - 1), _row(bcum, n // 2)
        qs = q[h] * (HEAD_K ** -0.5)
        scores = jnp.where(causal, bdot(qs * jnp.exp(bcum - b_mid), k[h] * jnp.exp(b_mid - bcum), 1, 1), 0.0)
        o = bdot(scores, v[h], 1, 0) + bdot(qs * jnp.exp(bcum), st[h], 1, 1)
        new.append(st[h] * jnp.exp(b_last) + bdot(v[h], k[h] * jnp.exp(b_last - bcum), 0, 0))
        outs.append(rms(o, wn) * silu(g[h]))
    return jnp.concatenate(outs, axis=1), new


def _gla_ins(cfg, p0, wg, bg, wn, tmap):
    gh = cfg.GH
    ins = []
    for h in range(gh):
        ins.append((p0, (CHUNK, HEAD_K), lambda g, t, h=h: (tmap(t), h)))
    for h in range(gh):
        ins.append((p0, (CHUNK, HEAD_K), lambda g, t, h=h: (tmap(t), gh + h)))
    for h in range(gh):
        ins.append((p0, (CHUNK, HEAD_V), lambda g, t, h=h: (tmap(t), gh + h)))
    for h in range(gh):
        ins.append((p0, (CHUNK, HEAD_V), lambda g, t, h=h: (tmap(t), 2 * gh + h)))
    ins.append((p0, (CHUNK, LANES), lambda g, t: (tmap(t), 10 * gh)))
    for h in range(gh):
        ins.append((wg, (LANES, HEAD_K), lambda g, t, h=h: (0, h)))
    for h in range(gh):
        ins.append((bg, (1, HEAD_K), lambda g, t, h=h: (0, h)))
    ins.append((wn, (1, HEAD_V), lambda g, t: (0, 0)))
    return ins


def _gla_unpack(gh, vals):
    q, k, v, g = (vals[i * gh:(i + 1) * gh] for i in range(4))
    glr = vals[4 * gh]
    wg = vals[4 * gh + 1:5 * gh + 1]
    bg = vals[5 * gh + 1:6 * gh + 1]
    wn = vals[6 * gh + 1]
    return q, k, v, g, glr, wg, bg, wn, vals[6 * gh + 2:]


def gla_fwd(cfg, p0, wg, bg, wn, name):
    gh, nc = cfg.GH, cfg.S // CHUNK

    def fn(ins, st):
        q, k, v, g, glr, wgv, bgv, wnv, _ = _gla_unpack(gh, ins)
        out, new = _gla_core(gh, q, k, v, g, glr, wgv, bgv, wnv, st)
        return [out, jnp.stack(st)], [], new

    return seq_call(name, fn, (1, nc), _gla_ins(cfg, p0, wg, bg, wn, lambda t: t),
                    [((cfg.S, cfg.DV), BF16, (CHUNK, cfg.DV), lambda g, t: (t, 0)),
                     ((nc, gh, HEAD_V, HEAD_K), F32, (None, gh, HEAD_V, HEAD_K), lambda g, t: (t, 0, 0, 0))],
                    carries=[(HEAD_V, HEAD_K)] * gh)


def gla_bwd(cfg, p0, wg, bg, wn, states, dout, name):
    gh, nc = cfg.GH, cfg.S // CHUNK
    rev = lambda t: nc - 1 - t

    def fn(ins, dst):
        q, k, v, g, glr, wgv, bgv, wnv, rest = _gla_unpack(gh, ins)
        st_all, do = rest
        st = [st_all[h] for h in range(gh)]
        _, vjp = jax.vjp(functools.partial(_gla_core, gh), q, k, v, g, glr, wgv, bgv, wnv, st)
        dq, dk, dv, dg, dglr, dwg, dbg, dwn, dstate = vjp((do.astype(F32), list(dst)))
        return ([jnp.concatenate(list(dq) + list(dk) + list(dv) + list(dg), axis=1), dglr],
                [jnp.concatenate(dwg, axis=1), jnp.concatenate(dbg, axis=1), dwn], dstate)

    ins = _gla_ins(cfg, p0, wg, bg, wn, rev)
    ins.append((states, (None, gh, HEAD_V, HEAD_K), lambda g, t: (rev(t), 0, 0, 0)))
    ins.append((dout, (CHUNK, cfg.DV), lambda g, t: (rev(t), 0)))
    wide = 2 * cfg.DK + 2 * cfg.DV
    fixed = lambda g, t: (0, 0)
    return seq_call(name, fn, (1, nc), ins,
                    [((cfg.S, wide), BF16, (CHUNK, wide), lambda g, t: (rev(t), 0)),
                     ((cfg.S, LANES), BF16, (CHUNK, LANES), lambda g, t: (rev(t), 0))],
                    accs=[((LANES, cfg.DK), (LANES, cfg.DK), fixed), ((1, cfg.DK), (1, cfg.DK), fixed),
                          ((1, HEAD_V), (1, HEAD_V), fixed)],
                    carries=[(HEAD_V, HEAD_K)] * gh)


def _lru_core(xc, gate, wa, wi, ba, bi, lam, h_in):
    r = jax.nn.sigmoid(bdot(xc, wa, 1, 0) + ba)
    i = jax.nn.sigmoid(bdot(xc, wi, 1, 0) + bi)
    log_a = LRU_C * r * log_sigmoid(lam)
    a = jnp.exp(log_a)
    u = jnp.sqrt(-_expm1(2.0 * log_a)) * (i * xc)
    first = _rows_iota(a.shape) == 0
    h = lin_scan(a, u + jnp.where(first, a * h_in, 0.0))
    return h * gelu_tanh(gate), _row(h, a.shape[0] - 1)


def _lru_tile(cfg):
    return min(cfg.S, 512)


def _lru_ins(cfg, xc, p0, wa, wi, ba, bi, lam, tmap):
    tt, gh = _lru_tile(cfg), cfg.GH
    vec = lambda g, t: (0, g)
    return [(xc, (tt, LRU_BLOCK), lambda g, t: (tmap(t), g)),
            (p0, (tt, LRU_BLOCK), lambda g, t: (tmap(t), 8 * gh + g)),
            (wa, (None, LRU_BLOCK, LRU_BLOCK), lambda g, t: (g, 0, 0)),
            (wi, (None, LRU_BLOCK, LRU_BLOCK), lambda g, t: (g, 0, 0)),
            (ba, (1, LRU_BLOCK), vec), (bi, (1, LRU_BLOCK), vec), (lam, (1, LRU_BLOCK), vec)]


def lru_fwd(cfg, xc, p0, wa, wi, ba, bi, lam, name):
    tt, nb = _lru_tile(cfg), cfg.NB
    nt = cfg.S // tt

    def fn(ins, c):
        out, h_last = _lru_core(*ins, c[0])
        return [out, c[0]], [], [h_last]

    return seq_call(name, fn, (nb, nt), _lru_ins(cfg, xc, p0, wa, wi, ba, bi, lam, lambda t: t),
                    [((cfg.S, cfg.W), BF16, (tt, LRU_BLOCK), lambda g, t: (t, g)),
                     ((nb, nt, 1, LRU_BLOCK), F32, (None, None, 1, LRU_BLOCK), lambda g, t: (g, t, 0, 0))],
                    carries=[(1, LRU_BLOCK)])


def lru_bwd(cfg, xc, p0, wa, wi, ba, bi, lam, states, dout, name):
    tt, nb = _lru_tile(cfg), cfg.NB
    nt = cfg.S // tt
    rev = lambda t: nt - 1 - t

    def fn(ins, c):
        *fwd_ins, h_in, do = ins
        _, vjp = jax.vjp(_lru_core, *fwd_ins, h_in)
        dxc, dgate, dwa, dwi, dba, dbi, dlam, dh = vjp((do.astype(F32), c[0]))
        return [dxc, dgate], [dwa, dwi, dba, dbi, dlam], [dh]

    ins = _lru_ins(cfg, xc, p0, wa, wi, ba, bi, lam, rev)
    ins.append((states, (None, None, 1, LRU_BLOCK), lambda g, t: (g, rev(t), 0, 0)))
    ins.append((dout, (tt, LRU_BLOCK), lambda g, t: (rev(t), g)))
    mat = ((nb, LRU_BLOCK, LRU_BLOCK), (None, LRU_BLOCK, LRU_BLOCK), lambda g, t: (g, 0, 0))
    vec = ((1, cfg.W), (1, LRU_BLOCK), lambda g, t: (0, g))
    return seq_call(name, fn, (nb, nt), ins,
                    [((cfg.S, cfg.W), F32, (tt, LRU_BLOCK), lambda g, t: (rev(t), g)),
                     ((cfg.S, cfg.W), BF16, (tt, LRU_BLOCK), lambda g, t: (rev(t), g))],
                    accs=[mat, mat, vec, vec, vec], carries=[(1, LRU_BLOCK)])


def _ssd_core(xc, bc, cc, z, dt_raw, dt_bias, a_log, d_skip, gn, st):
    n = xc.shape[0]
    x, bm, cm = silu(xc), silu(bc), silu(cc)
    dt = softplus(dt_raw + dt_bias)
    acs = cumsum_rows_mxu(dt * (-jnp.exp(a_log)))
    acs_t = acs.T
    acs_e, dt_e = _expand_heads(acs), _expand_heads(dt)
    last_e = _expand_heads(_row(acs, n - 1))
    causal = _lower_tri(n)
    cb = bdot(cm, bm, 1, 1)
    xdt = x * dt_e
    y_diag = []
    for h, xh in enumerate(_split_heads(xdt)):
        seg = acs[:, h:h + 1] - acs_t[h:h + 1, :]
        decay = jnp.where(causal, jnp.exp(jnp.minimum(seg, 0.0)), 0.0)
        y_diag.append(bdot(cb * decay, xh, 1, 0))
    y = jnp.concatenate(y_diag, axis=1) + bdot(cm, st, 1, 0) * jnp.exp(acs_e)
    new = st * jnp.exp(last_e) + bdot(bm, xdt * jnp.exp(last_e - acs_e), 0, 0)
    y = (y + _expand_heads(d_skip) * x) * silu(z)
    return rms(y, gn), new


SSD_TILED = 5


SSD_FWD_CHUNKS = 4
SSD_BWD_CHUNKS = 1


def _ssd_tile(cfg, chunks):
    return min(cfg.S, chunks * CHUNK)


def _chunk_rows(v, s):
    return v[s * CHUNK:(s + 1) * CHUNK]


def _ssd_ins(cfg, tt, xc, p1, dt_raw, dt_bias, a_log, d_skip, gn, tmap):
    ng = cfg.NG
    vec = lambda g, t: (g, 0, 0)
    return [(xc, (tt, SSD_GW), lambda g, t: (tmap(t), g)),
            (xc, (tt, SSD_N), lambda g, t: (tmap(t), 4 * ng + g)),
            (xc, (tt, SSD_N), lambda g, t: (tmap(t), 5 * ng + g)),
            (p1, (tt, SSD_GW), lambda g, t: (tmap(t), g)),
            (dt_raw, (None, tt, LANES), lambda g, t: (g, tmap(t), 0)),
            (dt_bias, (None, 1, LANES), vec), (a_log, (None, 1, LANES), vec), (d_skip, (None, 1, LANES), vec),
            (gn, (1, SSD_GW), lambda g, t: (0, g))]


def ssd_fwd(cfg, xc, p1, dt_raw, dt_bias, a_log, d_skip, gn, name):
    ng, nc, tt = cfg.NG, cfg.S // CHUNK, _ssd_tile(cfg, SSD_FWD_CHUNKS)
    nsub = tt // CHUNK

    def fn(ins, c):
        tiled, params = ins[:SSD_TILED], ins[SSD_TILED:]
        st, outs, entered = c[0], [], []
        for s in range(nsub):
            entered.append(st)
            out, st = _ssd_core(*[_chunk_rows(v, s) for v in tiled], *params, st)
            outs.append(out)
        return [jnp.concatenate(outs, axis=0), jnp.stack(entered)], [], [st]

    return seq_call(name, fn, (ng, cfg.S // tt), _ssd_ins(cfg, tt, xc, p1, dt_raw, dt_bias, a_log, d_skip, gn, lambda t: t),
                    [((cfg.S, cfg.DI), BF16, (tt, SSD_GW), lambda g, t: (t, g)),
                     ((ng, nc, SSD_N, SSD_GW), F32, (None, nsub, SSD_N, SSD_GW), lambda g, t: (g, t, 0, 0))],
                    carries=[(SSD_N, SSD_GW)])


def ssd_bwd(cfg, xc, p1, dt_raw, dt_bias, a_log, d_skip, gn, states, dout, name):
    ng, tt = cfg.NG, _ssd_tile(cfg, SSD_BWD_CHUNKS)
    nsub, nt = tt // CHUNK, cfg.S // tt
    rev = lambda t: nt - 1 - t

    def fn(ins, c):
        tiled, params = ins[:SSD_TILED], ins[SSD_TILED:SSD_TILED + 4]
        st_all, do = ins[SSD_TILED + 4:]
        dst, pieces, acc = c[0], [None] * nsub, None
        for s in reversed(range(nsub)):
            _, vjp = jax.vjp(_ssd_core, *[_chunk_rows(v, s) for v in tiled], *params, st_all[s])
            grads = vjp((_chunk_rows(do, s).astype(F32), dst))
            pieces[s], dparams, dst = grads[:SSD_TILED], grads[SSD_TILED:SSD_TILED + 4], grads[SSD_TILED + 4]
            acc = dparams if acc is None else [x + y for x, y in zip(acc, dparams)]
        return [jnp.concatenate([p[i] for p in pieces], axis=0) for i in range(SSD_TILED)], list(acc), [dst]

    ins = _ssd_ins(cfg, tt, xc, p1, dt_raw, dt_bias, a_log, d_skip, gn, rev)
    ins.append((states, (None, nsub, SSD_N, SSD_GW), lambda g, t: (g, rev(t), 0, 0)))
    ins.append((dout, (tt, SSD_GW), lambda g, t: (rev(t), g)))
    col = lambda g, t: (rev(t), g)
    vec = ((ng, 1, LANES), (None, 1, LANES), lambda g, t: (g, 0, 0))
    return seq_call(name, fn, (ng, nt), ins,
                    [((cfg.S, cfg.DI), F32, (tt, SSD_GW), col),
                     ((cfg.S, ng * SSD_N), F32, (tt, SSD_N), col),
                     ((cfg.S, ng * SSD_N), F32, (tt, SSD_N), col),
                     ((cfg.S, cfg.DI), BF16, (tt, SSD_GW), col),
                     ((ng, cfg.S, LANES), F32, (None, tt, LANES), lambda g, t: (g, rev(t), 0))],
                    accs=[vec, vec, vec, ((1, cfg.DI), (1, SSD_GW), lambda g, t: (0, g))],
                    carries=[(SSD_N, SSD_GW)])


PACK_ALIGN = SUBLANES * LANES
PACK_ROWS = 256


def pack(arrays):
    pieces = []
    for a in arrays:
        flat = a.reshape(-1).astype(F32)
        pad = _round_up(flat.shape[0], PACK_ALIGN) - flat.shape[0]
        pieces.append(jnp.pad(flat, (0, pad)) if pad else flat)
    flat = jnp.concatenate(pieces)
    pad = _round_up(flat.shape[0], PACK_ROWS * LANES) - flat.shape[0]
    return jnp.pad(flat, (0, pad)).reshape(-1, LANES)


def unpack(buf, shapes):
    lead = buf.shape[:-2]
    flat = buf.reshape(lead + (-1,))
    out, off = [], 0
    for s in shapes:
        n = math.prod(s)
        out.append(flat[..., off:off + n].reshape(lead + tuple(s)))
        off += _round_up(n, PACK_ALIGN)
    return out


def _slots_to_cols(slots):
    return slots.transpose(1, 0, 2).reshape(slots.shape[1], -1)


def _even_in_padded(cfg, w):
    main = 2 * cfg.DK + 2 * cfg.DV
    return jnp.concatenate([w[:, :main], w[:, main + GATE_RANK:], w[:, main:main + GATE_RANK],
                            jnp.zeros((w.shape[0], cfg.EP - cfg.EVEN_IN), w.dtype)], axis=1)


def _even_in_unpadded(cfg, wp):
    main = 2 * cfg.DK + 2 * cfg.DV
    rest = main + 2 * cfg.W
    return jnp.concatenate([wp[:, :main], wp[:, rest:rest + GATE_RANK], wp[:, main:rest]], axis=1)


def _odd_in_padded(cfg, w):
    return jnp.concatenate([w, jnp.zeros((w.shape[0], cfg.OP - cfg.ODD_IN), w.dtype)], axis=1)


def _odd_in_unpadded(cfg, wp):
    return wp[:, :cfg.ODD_IN]


def _group_lanes(cfg, v):
    lead = v.shape[:-1]
    g = jnp.moveaxis(v.reshape(lead + (cfg.NG, SSD_HG)), -2, 0)
    return jnp.pad(g, [(0, 0)] * (g.ndim - 1) + [(0, LANES - SSD_HG)])


def _ungroup_lanes(cfg, g):
    v = jnp.moveaxis(g[..., :SSD_HG], 0, -2)
    return v.reshape(v.shape[:-2] + (cfg.NH,))


def train_step(cfg, p, loss_target):
    S, D = cfg.S, cfg.D
    me = 4 * lax.axis_index("x") + 2 * lax.axis_index("y") + lax.axis_index("c")

    big = ["ev_w_in", "ev_w_out", "od_w_in", "od_w_out", "ffn_w_gate", "ffn_w_up", "ffn_w_down"]
    small_sharded = ["ev_gla_w_gate", "ev_lru_conv_w", "od_norm", "od_conv_w", "od_conv_b", "od_gnorm"]
    replicated = ["ev_norm", "ev_gla_b_gate", "ev_gla_w_onorm", "ev_lru_conv_b", "ev_lru_w_a", "ev_lru_b_a",
                  "ev_lru_w_i", "ev_lru_b_i", "ev_lru_lam", "od_dt_bias", "od_a_log", "od_d_skip", "ffn_norm",
                  "final_norm"]

    transposed = ("ffn_w_gate", "ffn_w_up")
    view = lambda n, a: jnp.swapaxes(a, 1, 2) if n in transposed else a
    wb = {n: view(n, p[n]).astype(BF16) for n in big}
    ffn_items = lambda l: [(wb["ffn_w_gate"], l), (wb["ffn_w_up"], l), (wb["ffn_w_down"], l)]
    ss_shapes = [p[n].shape for n in small_sharded]
    groups = [[(pack([p[n] for n in small_sharded]), None), (wb["ev_w_in"], 0), (wb["ev_w_out"], 0)], ffn_items(0),
              [(wb["od_w_in"], 0), (wb["od_w_out"], 0)], ffn_items(1)]
    gathers, tokens = [], []
    for i, g in enumerate(groups):
        lands = [_landing_zone(a, lead, me, f"gather_own{i}_{j}") for j, (a, lead) in enumerate(g)]
        handle, token = exchange_start(g, True, lands, f"gather_start{i}", NEAR_PEERS)
        gathers.append(handle)
        tokens.append(token)
    all_started = tokens[0][:1, :1] + tokens[1][:1, :1] + tokens[2][:1, :1] + tokens[3][:1, :1]

    def gathered(i, after):
        return relay(exchange_wait(gathers[i], after, f"gather_wait{i}")[1], f"gather_relay{i}")

    ss_all, gw_ev_in, gw_ev_out = gathered(0, all_started)
    gs = dict(zip(small_sharded, unpack(ss_all, ss_shapes)))
    w_ev_in = cols_from_slots(gw_ev_in, functools.partial(_even_in_padded, cfg), cfg.EP, "ev_w_in_cols")
    w_ev_out = gw_ev_out.reshape(D, D)

    gla_wg = jnp.pad(_slots_to_cols(gs["ev_gla_w_gate"][:, 0]), ((0, LANES - GATE_RANK), (0, 0)))
    lru_cw = _slots_to_cols(gs["ev_lru_conv_w"][:, 0])
    od_norm = gs["od_norm"].transpose(1, 0, 2).reshape(1, D)
    od_cw = _slots_to_cols(gs["od_conv_w"][:, 0])
    od_cb = gs["od_conv_b"].transpose(1, 0, 2).reshape(1, cfg.CD)
    od_gn = gs["od_gnorm"].transpose(1, 0, 2).reshape(1, cfg.DI)

    x0 = p["x"][0]
    target = loss_target[0]
    ev_norm = p["ev_norm"] + all_started
    bg = p["ev_gla_b_gate"]
    wn = p["ev_gla_w_onorm"]
    lru_cb = p["ev_lru_conv_b"]
    wa, wi = p["ev_lru_w_a"][0], p["ev_lru_w_i"][0]
    ba, bi, lam = p["ev_lru_b_a"], p["ev_lru_b_i"], p["ev_lru_lam"]
    dt_bias, a_log, d_skip = (_group_lanes(cfg, p[n]) for n in ("od_dt_bias", "od_a_log", "od_d_skip"))
    ffn_norm = [p["ffn_norm"][l:l + 1] for l in range(2)]
    final_norm = p["final_norm"].reshape(1, D)

    def ffn_forward(l, x):
        w_gate, w_up, w_down = gathered(1 + 2 * l, x)
        h = norm_fwd(cfg, x, ffn_norm[l], f"ffn{l}_norm")
        gate, up, act = ffn_gate_up(h, w_gate, w_up, f"ffn{l}_gate_up")
        out = matmul(act, w_down, a_slot="k", b_slot="k", res=x, name=f"ffn{l}_down", tn=512)
        return out, (h, gate, up, act, w_gate, w_up, w_down)

    h0 = norm_fwd(cfg, x0, ev_norm, "ev_norm")
    p0 = matmul(h0, w_ev_in, name="ev_in", tn=768)
    gla_out, gla_states = gla_fwd(cfg, p0, gla_wg, bg, wn, "gla_fwd")
    lru_col = 2 * cfg.DK + 2 * cfg.DV
    lru_xc = conv_fwd(cfg, p0, lru_col, cfg.W, lru_cw, lru_cb, "lru_conv")
    lru_out, lru_states = lru_fwd(cfg, lru_xc, p0, wa, wi, ba, bi, lam, "lru_fwd")
    mix = jnp.concatenate([gla_out, lru_out], axis=1)
    x1 = matmul(mix, w_ev_out, res=x0, name="ev_out")
    x2, ffn0_saved = ffn_forward(0, x1)

    gw_od_in, gw_od_out = gathered(2, x2)
    w_od_in = cols_from_slots(gw_od_in, functools.partial(_odd_in_padded, cfg), cfg.OP, "od_w_in_cols")
    w_od_out = gw_od_out.reshape(cfg.DI, D)
    h2 = norm_fwd(cfg, x2, od_norm, "od_norm")
    p1 = matmul(h2, w_od_in, name="od_in", tn=768)
    od_xc = conv_fwd(cfg, p1, cfg.DI, cfg.CD, od_cw, od_cb, "od_conv")
    dt_col = cfg.DI + cfg.CD
    dt_raw = _group_lanes(cfg, p1[:, dt_col:dt_col + cfg.NH])
    ssd_out, ssd_states = ssd_fwd(cfg, od_xc, p1, dt_raw, dt_bias, a_log, d_skip, od_gn, "ssd_fwd")
    x3 = matmul(ssd_out, w_od_out, res=x2, name="od_out")
    x4, ffn1_saved = ffn_forward(1, x3)

    loss_part, dx4, d_final_norm = head_fwd_bwd(cfg, x4, final_norm, target, "head")
    loss = lax.psum(loss_part[0, 0], MESH_AXES)

    def ffn_backward(l, x, saved, dx_out, after):
        h, gate, up, act, w_gate, w_up, w_down = saved
        dgate, dup = ffn_dgate_dup(dx_out, w_down, gate, up, after, f"ffn{l}_dgate_dup")
        d_down = matmul(act, dx_out, ta=True, a_slot="m", out_dtype=BF16, name=f"ffn{l}_dwdown")
        sent_down, token = start_grads([d_down], f"grads_start_ffn{l}_down")
        d_gate = matmul(dgate, h, ta=True, a_slot="m", after=token, out_dtype=BF16, name=f"ffn{l}_dwgate")
        d_up = matmul(dup, h, ta=True, a_slot="m", out_dtype=BF16, name=f"ffn{l}_dwup")
        sent_gate_up, token = start_grads([d_gate, d_up], f"grads_start_ffn{l}")
        dh = matmul(dgate, w_gate, a_slot="k", b_slot="k", after=token, name=f"ffn{l}_dh_gate", tn=512)
        dh = matmul(dup, w_up, a_slot="k", b_slot="k", res=dh, name=f"ffn{l}_dh_up", tn=512)
        dx, dnorm = norm_bwd(cfg, x, ffn_norm[l], dh, dx_out, f"ffn{l}_norm_bwd")
        return dx, dnorm, (sent_down, sent_gate_up)

    def start_grads(arrays, name):
        return exchange_start([(a, None) for a in arrays], False, [lax.empty(a.shape, a.dtype) for a in arrays], name)

    dx3, d_ffn_norm1, sent_ffn1 = ffn_backward(1, x3, ffn1_saved, dx4, None)

    d_ssd_out = matmul(dx3, w_od_out, tb=True, name="od_dmix")
    d_od_out = matmul(ssd_out, dx3, ta=True, out_dtype=BF16, name="od_dwout")
    dxs, dbm, dcm, dz, d_dt_raw, d_dt_bias, d_a_log, d_d_skip, d_od_gn = ssd_bwd(
        cfg, od_xc, p1, dt_raw, dt_bias, a_log, d_skip, od_gn, ssd_states, d_ssd_out, "ssd_bwd")
    conv_parts, col = [], 0
    for part, dy in (("x", dxs), ("b", dbm), ("c", dcm)):
        width = dy.shape[1]
        conv_parts.append(conv_bwd(cfg, p1, cfg.DI + col, width, od_cw[:, col:col + width], dy, "od_conv_bwd_" + part))
        col += width
    d_od_cw = jnp.concatenate([c[1] for c in conv_parts], axis=1)
    d_od_cb = jnp.concatenate([c[2] for c in conv_parts], axis=1)
    d_dt = _ungroup_lanes(cfg, d_dt_raw).astype(BF16)
    dp1 = jnp.concatenate([dz] + [c[0] for c in conv_parts] + [d_dt, jnp.zeros((S, cfg.OP - cfg.ODD_IN), BF16)], axis=1)
    dh2 = matmul(dp1, w_od_in, tb=True, name="od_dh", tk=1536)
    d_od_in = matmul(h2, dp1, ta=True, out_dtype=BF16, name="od_dwin", tn=768)
    dx2, d_od_norm = norm_bwd(cfg, x2, od_norm, dh2, dx3, "od_norm_bwd")
    d_od_in_slots = slots_from_cols(d_od_in, functools.partial(_odd_in_unpadded, cfg), p["od_w_in"].shape[2],
                                    "od_dwin_slots")
    sent_od, token = start_grads([d_od_in_slots, d_od_out.reshape((N_DEV,) + p["od_w_out"].shape[1:])], "grads_start_od")

    dx1, d_ffn_norm0, sent_ffn0 = ffn_backward(0, x1, ffn0_saved, dx2, token)

    d_ev_out = matmul(mix, dx1, ta=True, out_dtype=BF16, name="ev_dwout")
    sent_ev_out, token = start_grads([d_ev_out.reshape((N_DEV,) + p["ev_w_out"].shape[1:])], "grads_start_ev_out")
    d_mix = matmul(dx1, w_ev_out, tb=True, after=token, name="ev_dmix")
    d_qkvg, d_glr, d_gla_wg, d_bg, d_wn = gla_bwd(cfg, p0, gla_wg, bg, wn, gla_states, d_mix[:, :cfg.DV], "gla_bwd")
    d_lru_xc, d_gate_br, d_wa, d_wi, d_ba, d_bi, d_lam = lru_bwd(
        cfg, lru_xc, p0, wa, wi, ba, bi, lam, lru_states, d_mix[:, cfg.DV:], "lru_bwd")
    d_xbr, d_lru_cw, d_lru_cb = conv_bwd(cfg, p0, lru_col, cfg.W, lru_cw, d_lru_xc, "lru_conv_bwd")
    dp0 = jnp.concatenate([d_qkvg, d_xbr, d_gate_br, d_glr, jnp.zeros((S, cfg.EP - lru_col - 2 * cfg.W - LANES), BF16)],
                          axis=1)
    d_ev_in = matmul(h0, dp0, ta=True, out_dtype=BF16, name="ev_dwin", tn=768)
    d_ev_in_slots = slots_from_cols(d_ev_in, functools.partial(_even_in_unpadded, cfg), p["ev_w_in"].shape[2],
                                    "ev_dwin_slots")
    sent_ev_in, token = start_grads([d_ev_in_slots], "grads_start_ev_in")
    dh0 = matmul(dp0, w_ev_in, tb=True, after=token, name="ev_dh", tk=1792)
    grad_x, d_ev_norm = norm_bwd(cfg, x0, ev_norm, dh0, dx1, "ev_norm_bwd")

    out = {"loss": loss, "grad_x": grad_x[None]}

    def update(names, sent, after, wait_name):
        s, r = exchange_wait(sent[0], after, wait_name + "0")
        sends, recvs = [[a] for a in s], [[a] for a in r]
        for extra in sent[1:]:
            s, r = exchange_wait(extra, after, wait_name + "1")
            for i in range(len(names)):
                sends[i].append(s[i])
                recvs[i].append(r[i])
        for i, n in enumerate(names):
            res = adamw_sharded(recvs[i], sends[i], me, view(n, p[n]), view(n, p["m_" + n]), view(n, p["v_" + n]),
                                "adamw_" + n)
            out["grad_" + n], out["delta_" + n], out["new_m_" + n], out["new_v_" + n] = (view(n, a) for a in res)
        return res[-1]

    small_full = {
        "ev_gla_w_gate": d_gla_wg[:GATE_RANK][None], "ev_lru_conv_w": d_lru_cw[None], "od_norm": d_od_norm,
        "od_conv_w": d_od_cw[None], "od_conv_b": d_od_cb, "od_gnorm": d_od_gn,
        "ev_norm": d_ev_norm, "ev_gla_b_gate": d_bg, "ev_gla_w_onorm": d_wn, "ev_lru_conv_b": d_lru_cb,
        "ev_lru_w_a": d_wa[None], "ev_lru_b_a": d_ba, "ev_lru_w_i": d_wi[None], "ev_lru_b_i": d_bi,
        "ev_lru_lam": d_lam, "od_dt_bias": _ungroup_lanes(cfg, d_dt_bias), "od_a_log": _ungroup_lanes(cfg, d_a_log),
        "od_d_skip": _ungroup_lanes(cfg, d_d_skip), "ffn_norm": jnp.concatenate([d_ffn_norm0, d_ffn_norm1], axis=0),
        "final_norm": d_final_norm.reshape(D),
    }
    small = small_sharded + replicated
    small_packed = pack([small_full[n] for n in small])
    sent_small, token = exchange_start([(small_packed, None)], True,
                                       [_landing_zone(small_packed, None, me, "gather_small_grads_own")],
                                       "gather_small_grads")

    done = update(["od_w_in", "od_w_out"], [sent_od], token, "grads_wait_od")
    done = update(["ffn_w_down"], [sent_ffn0[0], sent_ffn1[0]], done, "grads_wait_ffn_down")
    done = update(["ffn_w_gate", "ffn_w_up"], [sent_ffn0[1], sent_ffn1[1]], done, "grads_wait_ffn")
    done = update(["ev_w_out"], [sent_ev_out], done, "grads_wait_ev_out")
    done = update(["ev_w_in"], [sent_ev_in], done, "grads_wait_ev_in")

    small_all = exchange_wait(sent_small, done, "gather_small_grads_wait")[1][0]
    g_small = dict(zip(small, unpack(reduce_slots(small_all, "sum_small_grads"), [small_full[n].shape for n in small])))
    for n in small_sharded:
        width = p[n].shape[-1]
        g_small[n] = lax.dynamic_slice_in_dim(g_small[n], me * width, width, axis=g_small[n].ndim - 1)
    shapes = [p[n].shape for n in small]
    g_buf = pack([g_small[n] for n in small])
    delta, new_m, new_v = adamw(pack([p[n] for n in small]), g_buf, pack([p["m_" + n] for n in small]),
                                pack([p["v_" + n] for n in small]), "adamw_small")
    for kind, buf in (("grad_", g_buf), ("delta_", delta), ("new_m_", new_m), ("new_v_", new_v)):
        for n, a in zip(small, unpack(buf, shapes)):
            out[kind + n] = a
    return out


WEIGHTS = ['ev_norm', 'ev_w_in', 'ev_gla_w_gate', 'ev_gla_b_gate', 'ev_gla_w_onorm', 'ev_lru_conv_w', 'ev_lru_conv_b',
           'ev_lru_w_a', 'ev_lru_b_a', 'ev_lru_w_i', 'ev_lru_b_i', 'ev_lru_lam', 'ev_w_out', 'od_norm', 'od_w_in',
           'od_conv_w', 'od_conv_b', 'od_dt_bias', 'od_a_log', 'od_d_skip', 'od_gnorm', 'od_w_out', 'ffn_norm',
           'ffn_w_gate', 'ffn_w_up', 'ffn_w_down', 'final_norm']


def kernel(x, ev_norm, ev_w_in, ev_gla_w_gate, ev_gla_b_gate, ev_gla_w_onorm, ev_lru_conv_w, ev_lru_conv_b, ev_lru_w_a, ev_lru_b_a, ev_lru_w_i, ev_lru_b_i, ev_lru_lam, ev_w_out, od_norm, od_w_in, od_conv_w, od_conv_b, od_dt_bias, od_a_log, od_d_skip, od_gnorm, od_w_out, ffn_norm, ffn_w_gate, ffn_w_up, ffn_w_down, final_norm, loss_target, m_ev_norm, m_ev_w_in, m_ev_gla_w_gate, m_ev_gla_b_gate, m_ev_gla_w_onorm, m_ev_lru_conv_w, m_ev_lru_conv_b, m_ev_lru_w_a, m_ev_lru_b_a, m_ev_lru_w_i, m_ev_lru_b_i, m_ev_lru_lam, m_ev_w_out, m_od_norm, m_od_w_in, m_od_conv_w, m_od_conv_b, m_od_dt_bias, m_od_a_log, m_od_d_skip, m_od_gnorm, m_od_w_out, m_ffn_norm, m_ffn_w_gate, m_ffn_w_up, m_ffn_w_down, m_final_norm, v_ev_norm, v_ev_w_in, v_ev_gla_w_gate, v_ev_gla_b_gate, v_ev_gla_w_onorm, v_ev_lru_conv_w, v_ev_lru_conv_b, v_ev_lru_w_a, v_ev_lru_b_a, v_ev_lru_w_i, v_ev_lru_b_i, v_ev_lru_lam, v_ev_w_out, v_od_norm, v_od_w_in, v_od_conv_w, v_od_conv_b, v_od_dt_bias, v_od_a_log, v_od_d_skip, v_od_gnorm, v_od_w_out, v_ffn_norm, v_ffn_w_gate, v_ffn_w_up, v_ffn_w_down, v_final_norm):
    args = dict(locals())
    p = {n: a for n, a in args.items() if n != "loss_target"}
    cfg = Cfg(S=x.shape[1], D=x.shape[2], DFF=ffn_w_gate.shape[2] * N_DEV)
    out = train_step(cfg, p, loss_target)
    return (out["loss"], out["grad_x"], *[out["grad_" + w] for w in WEIGHTS], *[out["delta_" + w] for w in WEIGHTS],
            *[out["new_m_" + w] for w in WEIGHTS], *[out["new_v_" + w] for w in WEIGHTS])
```

```python
import functools
import math
from typing import NamedTuple

import jax
import jax.numpy as jnp
from jax import lax
from jax.experimental import pallas as pl
from jax.experimental.pallas import tpu as pltpu

F32 = jnp.float32
BF16 = jnp.bfloat16
MESH_AXES = ("x", "y", "c")
N_DEV = 8
LANES = 128
SUBLANES = 8
VMEM_LIMIT = 56 * 1024 * 1024

NORM_EPS = 1e-6
CONV_WIDTH = 4
CHUNK = 64
HEAD_K = 128
HEAD_V = 256
GATE_RANK = 16
GATE_NORM = 16.0
LRU_BLOCK = 128
LRU_C = 8.0
SSD_P = 64
SSD_N = 128
SSD_HG = 8
SSD_GW = SSD_HG * SSD_P

ADAM_LR = 0.001
ADAM_B1 = 0.9
ADAM_B2 = 0.999
ADAM_EPS = 1e-08
ADAM_WD = 0.01
ADAM_STEP = 10


class Cfg(NamedTuple):
    S: int
    D: int
    DFF: int

    @property
    def GH(self):
        return self.D // 512

    @property
    def NB(self):
        return self.D // 256

    @property
    def NG(self):
        return self.D // 256

    @property
    def DK(self):
        return HEAD_K * self.GH

    @property
    def DV(self):
        return HEAD_V * self.GH

    @property
    def W(self):
        return LRU_BLOCK * self.NB

    @property
    def DI(self):
        return SSD_GW * self.NG

    @property
    def CD(self):
        return self.DI + 2 * self.NG * SSD_N

    @property
    def NH(self):
        return SSD_HG * self.NG

    @property
    def EVEN_IN(self):
        return 2 * self.DK + 2 * self.DV + GATE_RANK + 2 * self.W

    @property
    def ODD_IN(self):
        return self.DI + self.CD + self.NH

    @property
    def EP(self):
        return _round_up(2 * self.DK + 2 * self.DV + 2 * self.W + LANES, 768)

    @property
    def OP(self):
        return _round_up(self.DI + self.CD + LANES, 768)


def _round_up(n, m):
    return (n + m - 1) // m * m


def _tile(n, pref):
    if n <= pref:
        return n
    t = pref - pref % LANES
    while n % t:
        t -= LANES
    return t


def _cparams(n_axes):
    return pltpu.CompilerParams(dimension_semantics=("arbitrary",) * n_axes, vmem_limit_bytes=VMEM_LIMIT)


def _dg(a, b, ca, cb):
    return lax.dot_general(a.astype(BF16), b.astype(BF16), (((ca,), (cb,)), ((), ())), preferred_element_type=F32)


@functools.partial(jax.custom_vjp, nondiff_argnums=(2, 3))
def bdot(a, b, ca, cb):
    return _dg(a, b, ca, cb)


def _bdot_fwd(a, b, ca, cb):
    return _dg(a, b, ca, cb), (a, b)


def _bdot_bwd(ca, cb, res, g):
    a, b = res
    da = _dg(g, b, 1, 1 - cb) if ca == 1 else _dg(b, g, 1 - cb, 1)
    db = _dg(a, g, 1 - ca, 0) if cb == 0 else _dg(g, a, 0, 1 - ca)
    return da.astype(a.dtype), db.astype(b.dtype)


bdot.defvjp(_bdot_fwd, _bdot_bwd)


def _lower_tri(n):
    r = lax.broadcasted_iota(jnp.int32, (n, n), 0)
    c = lax.broadcasted_iota(jnp.int32, (n, n), 1)
    return c <= r


def _running_sum(x, reverse):
    n = x.shape[0]
    r = lax.broadcasted_iota(jnp.int32, x.shape, 0)
    d = 1
    while d < n:
        if reverse:
            x = x + jnp.where(r < n - d, pltpu.roll(x, n - d, 0), 0.0)
        else:
            x = x + jnp.where(r >= d, pltpu.roll(x, d, 0), 0.0)
        d *= 2
    return x


@jax.custom_vjp
def cumsum_rows(x):
    return _running_sum(x, False)


cumsum_rows.defvjp(lambda x: (_running_sum(x, False), None), lambda _, g: (_running_sum(g, True),))


def _tri_dot(x, transposed):
    n = x.shape[0]
    return lax.dot_general(_lower_tri(n).astype(F32), x, (((0 if transposed else 1,), (0,)), ((), ())),
                           precision=lax.Precision.HIGHEST, preferred_element_type=F32)


@jax.custom_vjp
def cumsum_rows_mxu(x):
    return _tri_dot(x, False)


cumsum_rows_mxu.defvjp(lambda x: (_tri_dot(x, False), None), lambda _, g: (_tri_dot(g, True),))


def _row(x, i):
    r = lax.broadcasted_iota(jnp.int32, x.shape, 0)
    return jnp.sum(jnp.where(r == i, x, 0.0), axis=0, keepdims=True)


def _softplus_raw(x):
    return jnp.maximum(x, 0.0) + jnp.log(1.0 + jnp.exp(-jnp.abs(x)))


@jax.custom_vjp
def softplus(x):
    return _softplus_raw(x)


softplus.defvjp(lambda x: (_softplus_raw(x), x), lambda x, g: (g * jax.nn.sigmoid(x),))


@jax.custom_vjp
def log_sigmoid(x):
    return -_softplus_raw(-x)


log_sigmoid.defvjp(lambda x: (-_softplus_raw(-x), x), lambda x, g: (g * jax.nn.sigmoid(-x),))


def silu(x):
    return x * jax.nn.sigmoid(x)


def gelu_tanh(x):
    return 0.5 * x * (1.0 + jnp.tanh(math.sqrt(2.0 / math.pi) * (x + 0.044715 * (x * x * x))))


def _expm1(x):
    series = x * (1.0 + 0.5 * x * (1.0 + (1.0 / 3.0) * x))
    return jnp.where(jnp.abs(x) < 1e-2, series, jnp.exp(x) - 1.0)


def rms(x, w):
    return x * lax.rsqrt(jnp.mean(x * x, axis=-1, keepdims=True) + NORM_EPS) * w


def _rows_iota(shape):
    return lax.broadcasted_iota(jnp.int32, shape, 0)


def _scan_up(a, u):
    n = a.shape[0]
    r = _rows_iota(a.shape)
    d = 1
    while d < n:
        m = r >= d
        a_s = jnp.where(m, pltpu.roll(a, d, 0), 1.0)
        u_s = jnp.where(m, pltpu.roll(u, d, 0), 0.0)
        u = a * u_s + u
        a = a * a_s
        d *= 2
    return u


def _scan_down(a, u):
    n = a.shape[0]
    r = _rows_iota(a.shape)
    d = 1
    while d < n:
        m = r < n - d
        a_s = jnp.where(m, pltpu.roll(a, n - d, 0), 1.0)
        u_s = jnp.where(m, pltpu.roll(u, n - d, 0), 0.0)
        u = a * u_s + u
        a = a * a_s
        d *= 2
    return u


@jax.custom_vjp
def lin_scan(a, u):
    return _scan_up(a, u)


def _lin_scan_fwd(a, u):
    h = _scan_up(a, u)
    return h, (a, h)


def _lin_scan_bwd(res, g):
    a, h = res
    n = a.shape[0]
    r = _rows_iota(a.shape)
    a_next = jnp.where(r < n - 1, pltpu.roll(a, n - 1, 0), 0.0)
    gt = _scan_down(a_next, g)
    h_prev = jnp.where(r >= 1, pltpu.roll(h, 1, 0), 0.0)
    return gt * h_prev, gt


lin_scan.defvjp(_lin_scan_fwd, _lin_scan_bwd)


def _expand_heads(v):
    r = v.shape[0]
    return jnp.concatenate([jnp.broadcast_to(v[:, h:h + 1], (r, SSD_P)) for h in range(SSD_HG)], axis=1)


@jax.custom_vjp
def _split_heads(x):
    return tuple(x[:, h * SSD_P:(h + 1) * SSD_P] for h in range(SSD_HG))


_split_heads.defvjp(lambda x: (_split_heads(x), None), lambda _, gs: (jnp.concatenate(gs, axis=1),))


def matmul(a, b, *, ta=False, tb=False, a_slot=None, b_slot=None, b_lead=(), res=None, after=None, out_dtype=F32,
           name, tm=1024, tn=1024, tk=2048):
    lead = tuple(b_lead)
    ra, ca_ = a.shape[-2:]
    rb, cb_ = b.shape[-2:]
    m_st, ka_st = (ca_, ra) if ta else (ra, ca_)
    kb_st, n_st = (cb_, rb) if tb else (rb, cb_)
    kslot = a_slot == "k"
    assert kslot == (b_slot == "k")
    m = m_st * (N_DEV if a_slot == "m" else 1)
    n = n_st * (N_DEV if b_slot == "n" else 1)
    assert ka_st == kb_st, (a.shape, b.shape, ta, tb)
    tm = m_st if a_slot == "m" else _tile(m, tm)
    tn = n_st if b_slot == "n" else _tile(n, tn)
    tk = ka_st if kslot else _tile(ka_st, tk)
    nk = ka_st // tk
    ca, cb = (0 if ta else 1), (1 if tb else 0)
    nl = (None,) * len(lead)

    if a_slot is None:
        a_spec = pl.BlockSpec((tk, tm), lambda i, j, k: (k, i)) if ta else pl.BlockSpec((tm, tk), lambda i, j, k: (i, k))
    elif a_slot == "m":
        a_spec = (pl.BlockSpec((None, tk, tm), lambda i, j, k: (i, k, 0)) if ta
                  else pl.BlockSpec((None, tm, tk), lambda i, j, k: (i, 0, k)))
    else:
        a_spec = (pl.BlockSpec((N_DEV, tk, tm), lambda i, j, k: (0, 0, i)) if ta
                  else pl.BlockSpec((N_DEV, tm, tk), lambda i, j, k: (0, i, 0)))
    if b_slot is None:
        b_spec = (pl.BlockSpec(nl + (tn, tk), lambda i, j, k: lead + (j, k)) if tb
                  else pl.BlockSpec(nl + (tk, tn), lambda i, j, k: lead + (k, j)))
    elif b_slot == "n":
        b_spec = (pl.BlockSpec((None,) + nl + (tn, tk), lambda i, j, k: (j,) + lead + (0, k)) if tb
                  else pl.BlockSpec((None,) + nl + (tk, tn), lambda i, j, k: (j,) + lead + (k, 0)))
    else:
        b_spec = (pl.BlockSpec((N_DEV,) + nl + (tn, tk), lambda i, j, k: (0,) + lead + (j, 0)) if tb
                  else pl.BlockSpec((N_DEV,) + nl + (tk, tn), lambda i, j, k: (0,) + lead + (0, j)))
    if a_slot == "m":
        o_spec, o_shape = pl.BlockSpec((None, tm, tn), lambda i, j, k: (i, 0, j)), (N_DEV, tm, n)
    elif b_slot == "n":
        o_spec, o_shape = pl.BlockSpec((None, tm, tn), lambda i, j, k: (j, i, 0)), (N_DEV, m, tn)
    else:
        o_spec, o_shape = pl.BlockSpec((tm, tn), lambda i, j, k: (i, j)), (m, n)
    assert res is None or (a_slot != "m" and b_slot != "n")

    def dot(x, y):
        return lax.dot_general(x.astype(BF16), y.astype(BF16), (((ca,), (cb,)), ((), ())), preferred_element_type=F32)

    def body(*refs):
        a_ref, b_ref = refs[:2]
        r_ref = refs[2] if res is not None else None
        o_ref = refs[2 + (res is not None) + (after is not None)]

        def finish(acc):
            if r_ref is not None:
                acc = acc + r_ref[...].astype(F32)
            o_ref[...] = acc.astype(o_ref.dtype)

        if kslot:
            acc = dot(a_ref[0], b_ref[0])
            for s in range(1, N_DEV):
                acc = acc + dot(a_ref[s], b_ref[s])
            finish(acc)
        elif nk == 1:
            finish(dot(a_ref[...], b_ref[...]))
        else:
            acc_ref = refs[-1]
            k = pl.program_id(2)

            @pl.when(k == 0)
            def _():
                acc_ref[...] = dot(a_ref[...], b_ref[...])

            @pl.when(k > 0)
            def _():
                acc_ref[...] += dot(a_ref[...], b_ref[...])

            @pl.when(k == nk - 1)
            def _():
                finish(acc_ref[...])

    in_specs = [a_spec, b_spec]
    args = [a, b]
    if res is not None:
        in_specs.append(pl.BlockSpec((tm, tn), lambda i, j, k: (i, j)))
        args.append(res)
    if after is not None:
        in_specs.append(pl.BlockSpec(memory_space=pl.ANY))
        args.append(after)
    return pl.pallas_call(
        body, name=name, grid=(m // tm, n // tn, nk), in_specs=in_specs, out_specs=o_spec,
        out_shape=jax.ShapeDtypeStruct(o_shape, out_dtype),
        scratch_shapes=[pltpu.VMEM((tm, tn), F32)] if nk > 1 else [], compiler_params=_cparams(3),
    )(*args)


def seq_call(name, fn, grid, ins, outs, accs=(), carries=()):
    n_in, n_out, n_acc = len(ins), len(outs), len(accs)

    def body(*refs):
        in_refs = refs[:n_in]
        out_refs = refs[n_in:n_in + n_out]
        acc_refs = refs[n_in + n_out:n_in + n_out + n_acc]
        c_refs = refs[n_in + n_out + n_acc:]

        if acc_refs or c_refs:
            @pl.when(pl.program_id(1) == 0)
            def _():
                for r in tuple(acc_refs) + tuple(c_refs):
                    r[...] = jnp.zeros_like(r)

        o, a, c = fn([r[...] for r in in_refs], [r[...] for r in c_refs])
        for r, v in zip(out_refs, o, strict=True):
            r[...] = v.astype(r.dtype)
        for r, v in zip(acc_refs, a, strict=True):
            r[...] += v
        for r, v in zip(c_refs, c, strict=True):
            r[...] = v

    return pl.pallas_call(
        body, name=name, grid=grid,
        in_specs=[pl.BlockSpec(blk, im) for _, blk, im in ins],
        out_specs=[pl.BlockSpec(blk, im) for _, _, blk, im in outs] + [pl.BlockSpec(blk, im) for _, blk, im in accs],
        out_shape=[jax.ShapeDtypeStruct(s, d) for s, d, _, _ in outs] + [jax.ShapeDtypeStruct(s, F32) for s, _, _ in accs],
        scratch_shapes=[pltpu.VMEM(s, F32) for s in carries], compiler_params=_cparams(2),
    )(*[a for a, _, _ in ins])


def exchange(arrays, gather, name, after=None):
    n = len(arrays)
    extra = [] if after is None else [after]

    def body(*refs):
        x_refs, o_refs = refs[:n], refs[n + len(extra):2 * n + len(extra)]
        send_sems, recv_sems, local_sems = refs[2 * n + len(extra):]
        pos = [lax.axis_index(ax) for ax in MESH_AXES]
        me = 4 * pos[0] + 2 * pos[1] + pos[2]
        copies = []
        for i in range(n):
            own = pltpu.make_async_copy(x_refs[i] if gather else x_refs[i].at[me], o_refs[i].at[me], local_sems.at[i])
            own.start()
            copies.append(own)
        for k in range(1, N_DEV):
            bits = ((k >> 2) & 1, (k >> 1) & 1, k & 1)
            peer = tuple(1 - p if b else p for p, b in zip(pos, bits))
            peer_id = 4 * peer[0] + 2 * peer[1] + peer[2]
            for i in range(n):
                cp = pltpu.make_async_remote_copy(
                    src_ref=x_refs[i] if gather else x_refs[i].at[peer_id], dst_ref=o_refs[i].at[me],
                    send_sem=send_sems.at[i * (N_DEV - 1) + k - 1], recv_sem=recv_sems.at[i * (N_DEV - 1) + k - 1],
                    device_id=peer, device_id_type=pl.DeviceIdType.MESH)
                cp.start()
                copies.append(cp)
        for cp in copies:
            cp.wait()

    hbm = pl.BlockSpec(memory_space=pltpu.HBM)
    return pl.pallas_call(
        body, name=name, in_specs=[hbm] * n + [pl.BlockSpec(memory_space=pl.ANY)] * len(extra), out_specs=[hbm] * n,
        out_shape=[jax.ShapeDtypeStruct(((N_DEV,) + a.shape) if gather else a.shape, a.dtype) for a in arrays],
        scratch_shapes=[pltpu.SemaphoreType.DMA((n * (N_DEV - 1),)), pltpu.SemaphoreType.DMA((n * (N_DEV - 1),)),
                        pltpu.SemaphoreType.DMA((n,))],
    )(*arrays, *extra)


_HBM = pl.BlockSpec(memory_space=pltpu.HBM)
_SEM = pl.BlockSpec(memory_space=pltpu.SEMAPHORE)
N_PEERS = N_DEV - 1


def _mesh_pos():
    pos = [lax.axis_index(ax) for ax in MESH_AXES]
    return pos, 4 * pos[0] + 2 * pos[1] + pos[2]


ALL_PEERS = (1, 2, 3, 4, 5, 6, 7)
NEAR_PEERS = (1, 2, 4, 6)
RELAYED = (2, 4, 6)


def _peers(pos, masks=ALL_PEERS):
    out = []
    for k in masks:
        bits = ((k >> 2) & 1, (k >> 1) & 1, k & 1)
        peer = tuple(1 - p if b else p for p, b in zip(pos, bits))
        out.append((peer, 4 * peer[0] + 2 * peer[1] + peer[2]))
    return out


def _part(x_ref, lead, gather, slot):
    ref = x_ref if lead is None else x_ref.at[lead]
    return ref if gather else ref.at[slot]


OWN_BLOCK_BYTES = 2 * 1024 * 1024


def _landing_zone(a, lead, me, name):
    r, c = a.shape[-2:]
    tr = r
    while tr * _round_up(c, LANES) * a.dtype.itemsize > OWN_BLOCK_BYTES and tr % 32 == 0:
        tr //= 2

    def body(me_ref, x_ref, o_ref):
        o_ref[...] = x_ref[...]

    x_spec = (pl.BlockSpec((tr, c), lambda i, me_ref: (i, 0)) if lead is None
              else pl.BlockSpec((None, tr, c), lambda i, me_ref: (lead, i, 0)))
    grid_spec = pltpu.PrefetchScalarGridSpec(
        num_scalar_prefetch=1, grid=(r // tr,), in_specs=[x_spec],
        out_specs=pl.BlockSpec((None, tr, c), lambda i, me_ref: (me_ref[0], i, 0)))
    return pl.pallas_call(body, name=name, grid_spec=grid_spec, out_shape=jax.ShapeDtypeStruct((N_DEV, r, c), a.dtype),
                          compiler_params=_cparams(1))(jnp.reshape(me, (1,)).astype(jnp.int32), a)


def _split_copies(items, gather, masks, x_refs, land_refs, send_sems, recv_sems):
    pos, me = _mesh_pos()
    copies = []
    for i, (_, lead) in enumerate(items):
        for k, (peer, peer_id) in enumerate(_peers(pos, masks)):
            copies.append(pltpu.make_async_remote_copy(
                src_ref=_part(x_refs[i], lead, gather, peer_id), dst_ref=land_refs[i].at[me],
                send_sem=send_sems.at[i * len(masks) + k], recv_sem=recv_sems.at[i * len(masks) + k],
                device_id=peer, device_id_type=pl.DeviceIdType.MESH))
    return copies


_SPLIT_CALL = dict(compiler_params=pltpu.CompilerParams(has_side_effects=pltpu.SideEffectType.DATAFLOW_SIDE_EFFECTING))


def exchange_start(items, gather, lands, name, masks=ALL_PEERS):
    n = len(items)
    lands = list(lands)
    xs = [a for a, _ in items]

    def body(*refs):
        x_refs, land_refs = refs[:n], refs[n:2 * n]
        send_sems, recv_sems, token = refs[2 * n], refs[2 * n + 1], refs[-1]
        for cp in _split_copies(items, gather, masks, x_refs, land_refs, send_sems, recv_sems):
            cp.start()
        token[...] = jnp.zeros_like(token)

    outs = pl.pallas_call(
        body, name=name,
        out_shape=(pltpu.SemaphoreType.DMA((n * len(masks),)), pltpu.SemaphoreType.DMA((n * len(masks),)),
                   *[pltpu.HBM(v.shape, v.dtype) for v in xs + lands], jax.ShapeDtypeStruct((SUBLANES, LANES), F32)),
        in_specs=[_HBM] * (2 * n),
        out_specs=(_SEM, _SEM, *[_HBM] * (2 * n), pl.BlockSpec(memory_space=pltpu.VMEM)),
        input_output_aliases={i: 2 + i for i in range(2 * n)}, **_SPLIT_CALL,
    )(*[pltpu.with_memory_space_constraint(v, pltpu.HBM) for v in xs + lands])
    handle = (items, gather, masks, outs[0], outs[1], outs[2:2 + n], outs[2 + n:2 + 2 * n])
    return handle, outs[-1]


def _relay_copies(n, land_refs, send_sems, recv_sems):
    pos, _ = _mesh_pos()
    sibling = (pos[0], pos[1], 1 - pos[2])
    copies = []
    for i in range(n):
        for k, (_, peer_id) in enumerate(_peers(pos, RELAYED)):
            slot = land_refs[i].at[peer_id]
            copies.append(pltpu.make_async_remote_copy(
                src_ref=slot, dst_ref=slot, send_sem=send_sems.at[i * len(RELAYED) + k],
                recv_sem=recv_sems.at[i * len(RELAYED) + k], device_id=sibling, device_id_type=pl.DeviceIdType.MESH))
    return copies


def relay(lands, name):
    n = len(lands)
    sems = n * len(RELAYED)

    def start(*refs):
        for cp in _relay_copies(n, refs[:n], refs[n], refs[n + 1]):
            cp.start()

    outs = pl.pallas_call(
        start, name=name + "_start",
        out_shape=(pltpu.SemaphoreType.DMA((sems,)), pltpu.SemaphoreType.DMA((sems,)),
                   *[pltpu.HBM(v.shape, v.dtype) for v in lands]),
        in_specs=[_HBM] * n, out_specs=(_SEM, _SEM, *[_HBM] * n), input_output_aliases={i: 2 + i for i in range(n)},
        **_SPLIT_CALL)(*[pltpu.with_memory_space_constraint(v, pltpu.HBM) for v in lands])

    def wait(*refs):
        for cp in _relay_copies(n, refs[:n], refs[n], refs[n + 1]):
            cp.wait_send()
            cp.wait_recv()

    return list(pl.pallas_call(
        wait, name=name + "_wait", out_shape=tuple(pltpu.HBM(v.shape, v.dtype) for v in lands),
        in_specs=[_HBM] * n + [_SEM, _SEM], out_specs=tuple([_HBM] * n), input_output_aliases={i: i for i in range(n)},
        **_SPLIT_CALL)(*outs[2:], outs[0], outs[1]))


def exchange_wait(handle, after, name):
    items, gather, masks, send_sems, recv_sems, x_thru, land_thru = handle
    n = len(items)

    def body(*refs):
        x_refs, land_refs = refs[:n], refs[n:2 * n]
        for cp in _split_copies(items, gather, masks, x_refs, land_refs, refs[2 * n], refs[2 * n + 1]):
            cp.wait_send()
            cp.wait_recv()

    outs = pl.pallas_call(
        body, name=name, out_shape=tuple(pltpu.HBM(v.shape, v.dtype) for v in tuple(x_thru) + tuple(land_thru)),
        in_specs=[_HBM] * (2 * n) + [_SEM, _SEM, pl.BlockSpec(memory_space=pl.ANY)], out_specs=tuple([_HBM] * (2 * n)),
        input_output_aliases={i: i for i in range(2 * n)},
        compiler_params=pltpu.CompilerParams(has_side_effects=pltpu.SideEffectType.DATAFLOW_SIDE_EFFECTING),
    )(*x_thru, *land_thru, send_sems, recv_sems, after)
    return list(outs[:n]), list(outs[n:])


def _adam_update(w, g, m, v):
    nm = ADAM_B1 * m + (1.0 - ADAM_B1) * g
    nv = ADAM_B2 * v + (1.0 - ADAM_B2) * (g * g)
    m_hat = nm / (1.0 - ADAM_B1 ** ADAM_STEP)
    v_hat = nv / (1.0 - ADAM_B2 ** ADAM_STEP)
    return -ADAM_LR * (m_hat / (jnp.sqrt(v_hat) + ADAM_EPS) + ADAM_WD * w), nm, nv


def _sum_slots(s_ref):
    acc = s_ref[0].astype(F32)
    for j in range(1, N_DEV):
        acc = acc + s_ref[j].astype(F32)
    return acc


ADAM_BLOCK_BYTES = 10 * 1024 * 1024


def adamw_sharded(recvs, sends, me, w, m, v, name, transposed=False):
    nl = w.shape[0]
    r, c = recvs[0].shape[1:]
    assert w.shape[1:] == ((c, r) if transposed else (r, c))
    assert len(recvs) == nl and len(sends) == nl
    per_row = _round_up(c, LANES) * (nl * (N_DEV + 1) * recvs[0].dtype.itemsize + 7 * 4)
    tr = r
    while tr * per_row > ADAM_BLOCK_BYTES and tr % 16 == 0:
        tr //= 2

    def body(me_ref, *refs):
        s_refs, o_refs = refs[:nl], refs[nl:2 * nl]
        w_ref, m_ref, v_ref, g_ref, d_ref, nm_ref, nv_ref = refs[2 * nl:]
        mine = me_ref[0]

        def total(l):
            acc = jnp.where(mine == 0, o_refs[l][...], s_refs[l][0]).astype(F32)
            for j in range(1, N_DEV):
                acc = acc + jnp.where(mine == j, o_refs[l][...], s_refs[l][j]).astype(F32)
            return acc

        g = total(0)
        for l in range(1, nl):
            g = jnp.where(pl.program_id(0) == l, total(l), g)
        if transposed:
            g = g.T
        g_ref[...] = g
        d_ref[...], nm_ref[...], nv_ref[...] = _adam_update(w_ref[...], g, m_ref[...], v_ref[...])

    spec = (pl.BlockSpec((None, c, tr), lambda l, i, me_ref: (l, 0, i)) if transposed
            else pl.BlockSpec((None, tr, c), lambda l, i, me_ref: (l, i, 0)))
    shp = jax.ShapeDtypeStruct(w.shape, F32)
    grid_spec = pltpu.PrefetchScalarGridSpec(
        num_scalar_prefetch=1, grid=(nl, r // tr),
        in_specs=([pl.BlockSpec((N_DEV, tr, c), lambda l, i, me_ref: (0, i, 0))] * nl
                  + [pl.BlockSpec((None, tr, c), lambda l, i, me_ref: (me_ref[0], i, 0))] * nl + [spec, spec, spec]),
        out_specs=[spec] * 4)
    return pl.pallas_call(body, name=name, grid_spec=grid_spec, out_shape=[shp] * 4, compiler_params=_cparams(2))(
        jnp.reshape(me, (1,)).astype(jnp.int32), *recvs, *sends, w, m, v)


def reduce_slots(slots, name):
    _, r, lanes = slots.shape
    tr = _tile(r, 2048)

    def body(s_ref, o_ref):
        o_ref[...] = _sum_slots(s_ref)

    return pl.pallas_call(
        body, name=name, grid=(r // tr,),
        in_specs=[pl.BlockSpec((N_DEV, tr, lanes), lambda i: (0, i, 0))],
        out_specs=pl.BlockSpec((tr, lanes), lambda i: (i, 0)),
        out_shape=jax.ShapeDtypeStruct((r, lanes), F32), compiler_params=_cparams(1),
    )(slots)


def adamw(w, g, m, v, name):
    r, lanes = w.shape
    tr = _tile(r, 2048)

    def body(w_ref, g_ref, m_ref, v_ref, d_ref, nm_ref, nv_ref):
        d_ref[...], nm_ref[...], nv_ref[...] = _adam_update(w_ref[...], g_ref[...], m_ref[...], v_ref[...])

    spec = pl.BlockSpec((tr, lanes), lambda i: (i, 0))
    shp = jax.ShapeDtypeStruct((r, lanes), F32)
    return pl.pallas_call(body, name=name, grid=(r // tr,), in_specs=[spec] * 4, out_specs=[spec] * 3,
                          out_shape=[shp] * 3, compiler_params=_cparams(1))(w, g, m, v)


def cols_from_slots(slots, place, width, name):
    _, rows, c = slots.shape
    tr = _tile(rows, 256)

    def body(s_ref, o_ref):
        o_ref[...] = place(jnp.concatenate([s_ref[j] for j in range(N_DEV)], axis=1))

    return pl.pallas_call(
        body, name=name, grid=(rows // tr,),
        in_specs=[pl.BlockSpec((N_DEV, tr, c), lambda i: (0, i, 0))],
        out_specs=pl.BlockSpec((tr, width), lambda i: (i, 0)),
        out_shape=jax.ShapeDtypeStruct((rows, width), slots.dtype), compiler_params=_cparams(1))(slots)


def slots_from_cols(full, pick, c, name):
    rows, wide = full.shape
    tr = _tile(rows, 256)

    def body(x_ref, o_ref):
        v = pick(x_ref[...])
        for j in range(N_DEV):
            o_ref[j] = v[:, j * c:(j + 1) * c]

    return pl.pallas_call(
        body, name=name, grid=(rows // tr,),
        in_specs=[pl.BlockSpec((tr, wide), lambda i: (i, 0))],
        out_specs=pl.BlockSpec((N_DEV, tr, c), lambda i: (0, i, 0)),
        out_shape=jax.ShapeDtypeStruct((N_DEV, rows, c), full.dtype), compiler_params=_cparams(1))(full)


def _token_tile(cfg):
    return min(cfg.S, 256)


def norm_fwd(cfg, x, w, name):
    ts = _token_tile(cfg)
    d = x.shape[1]

    def fn(ins, _):
        xv, wv = ins
        return [rms(xv, wv)], [], []

    return seq_call(name, fn, (1, cfg.S // ts),
                    [(x, (ts, d), lambda g, t: (t, 0)), (w, (1, d), lambda g, t: (0, 0))],
                    [((cfg.S, d), BF16, (ts, d), lambda g, t: (t, 0))])[0]


def norm_bwd(cfg, x, w, dh, dres, name):
    ts = _token_tile(cfg)
    d = x.shape[1]

    def fn(ins, _):
        xv, wv, dhv, drv = ins
        _, vjp = jax.vjp(rms, xv, wv)
        dx, dw = vjp(dhv.astype(F32))
        return [dx + drv], [dw], []

    row = lambda g, t: (t, 0)
    out = seq_call(name, fn, (1, cfg.S // ts),
                   [(x, (ts, d), row), (w, (1, d), lambda g, t: (0, 0)), (dh, (ts, d), row), (dres, (ts, d), row)],
                   [((cfg.S, d), F32, (ts, d), row)],
                   accs=[((1, d), (1, d), lambda g, t: (0, 0))])
    return out[0], out[1]


def head_fwd_bwd(cfg, x, w, target, name):
    ts = _token_tile(cfg)
    d = x.shape[1]

    def fn(ins, _):
        xv, wv, tv = ins
        y, vjp = jax.vjp(rms, xv, wv)
        err = y - tv
        loss = 0.5 * jnp.sum(err * err) / d
        dx, dw = vjp(err / d)
        return [dx], [jnp.full((SUBLANES, LANES), loss, F32), dw], []

    row = lambda g, t: (t, 0)
    fixed = lambda g, t: (0, 0)
    dx, loss, dw = seq_call(name, fn, (1, cfg.S // ts),
                            [(x, (ts, d), row), (w, (1, d), fixed), (target, (ts, d), row)],
                            [((cfg.S, d), F32, (ts, d), row)],
                            accs=[((SUBLANES, LANES), (SUBLANES, LANES), fixed), ((1, d), (1, d), fixed)])
    return loss, dx, dw


FFN_ROWS = 1024


def ffn_gate_up(h, w_gate, w_up, name):
    s, d = h.shape
    c = w_gate.shape[1]
    tm = _tile(s, FFN_ROWS)

    def body(h_ref, wg_ref, wu_ref, g_ref, u_ref, a_ref):
        hv = h_ref[...]
        g = _dg(hv, wg_ref[...], 1, 1)
        u = _dg(hv, wu_ref[...], 1, 1)
        g_ref[...] = g
        u_ref[...] = u
        a_ref[...] = (silu(g) * u).astype(a_ref.dtype)

    w_spec = pl.BlockSpec((None, c, d), lambda i, j: (j, 0, 0))
    o_spec = pl.BlockSpec((None, tm, c), lambda i, j: (j, i, 0))
    shp = (N_DEV, s, c)
    return pl.pallas_call(
        body, name=name, grid=(s // tm, N_DEV), in_specs=[pl.BlockSpec((tm, d), lambda i, j: (i, 0)), w_spec, w_spec],
        out_specs=[o_spec] * 3,
        out_shape=[jax.ShapeDtypeStruct(shp, F32), jax.ShapeDtypeStruct(shp, F32), jax.ShapeDtypeStruct(shp, BF16)],
        compiler_params=_cparams(2))(h, w_gate, w_up)


def ffn_dgate_dup(dx, w_down, gate, up, after, name):
    s, d = dx.shape
    c = w_down.shape[1]
    tm = _tile(s, FFN_ROWS)
    extra = [] if after is None else [after]

    def body(dx_ref, wd_ref, g_ref, u_ref, *rest):
        dg_ref, du_ref = rest[len(extra):]
        dact = _dg(dx_ref[...], wd_ref[...], 1, 1)
        _, vjp = jax.vjp(lambda a, b: silu(a) * b, g_ref[...], u_ref[...])
        dg, du = vjp(dact)
        dg_ref[...] = dg.astype(dg_ref.dtype)
        du_ref[...] = du.astype(du_ref.dtype)

    blk = pl.BlockSpec((None, tm, c), lambda i, j: (j, i, 0))
    shp = jax.ShapeDtypeStruct((N_DEV, s, c), BF16)
    return pl.pallas_call(
        body, name=name, grid=(s // tm, N_DEV),
        in_specs=[pl.BlockSpec((tm, d), lambda i, j: (i, 0)), pl.BlockSpec((None, c, d), lambda i, j: (j, 0, 0)), blk, blk]
        + [pl.BlockSpec(memory_space=pl.ANY)] * len(extra),
        out_specs=[blk, blk], out_shape=[shp, shp], compiler_params=_cparams(2))(dx, w_down, gate, up, *extra)


CONV_COLS = (512, 256)
HALO = SUBLANES


def _conv_cols(col0, width):
    return next(c for c in CONV_COLS if col0 % c == 0 and width % c == 0)


def _shift_down(x, halo, j):
    if j == 0:
        return x
    r8 = _rows_iota(halo.shape)
    top = jnp.where(r8 >= j, pltpu.roll(x[:HALO], j, 0), pltpu.roll(halo, j, 0))
    return jnp.concatenate([top, pltpu.roll(x, j, 0)[HALO:]], axis=0)


def _shift_up(x, halo, j):
    if j == 0:
        return x
    n = x.shape[0]
    r8 = _rows_iota(halo.shape)
    bot = jnp.where(r8 < HALO - j, pltpu.roll(x[n - HALO:], HALO - j, 0), pltpu.roll(halo, HALO - j, 0))
    return jnp.concatenate([pltpu.roll(x, n - j, 0)[:n - HALO], bot], axis=0)


def _conv_tile(cfg):
    return min(cfg.S, 1024)


def conv_fwd(cfg, src, col0, width, w, b, name):
    tt, cb = _conv_tile(cfg), _conv_cols(col0, width)
    c0, hb = col0 // cb, tt // HALO
    nt = cfg.S // tt

    def body(x_ref, h_ref, w_ref, b_ref, o_ref):
        t = pl.program_id(1)
        x = x_ref[...]
        halo = jnp.where(t > 0, h_ref[...], 0.0)
        wv = w_ref[...]
        acc = b_ref[...] + wv[CONV_WIDTH - 1:CONV_WIDTH] * x
        for j in range(1, CONV_WIDTH):
            acc = acc + wv[CONV_WIDTH - 1 - j:CONV_WIDTH - j] * _shift_down(x, halo, j)
        o_ref[...] = acc

    return pl.pallas_call(
        body, name=name, grid=(width // cb, nt),
        in_specs=[pl.BlockSpec((tt, cb), lambda c, t: (t, c0 + c)),
                  pl.BlockSpec((HALO, cb), lambda c, t: (jnp.maximum(t * hb - 1, 0), c0 + c)),
                  pl.BlockSpec((CONV_WIDTH, cb), lambda c, t: (0, c)),
                  pl.BlockSpec((1, cb), lambda c, t: (0, c))],
        out_specs=pl.BlockSpec((tt, cb), lambda c, t: (t, c)),
        out_shape=jax.ShapeDtypeStruct((cfg.S, width), F32), compiler_params=_cparams(2),
    )(src, src, w, b)


def conv_bwd(cfg, src, col0, width, w, dy, name, into=None, into_col0=0):
    tt, cb = _conv_tile(cfg), _conv_cols(col0, width)
    c0, hb = col0 // cb, tt // HALO
    nt = cfg.S // tt
    extra = [] if into is None else [into]
    assert into_col0 % cb == 0
    o0 = into_col0 // cb

    def body(x_ref, h_ref, w_ref, dy_ref, dh_ref, *rest):
        dx_ref, dw_ref, db_ref = rest[len(extra):]
        t = pl.program_id(1)

        @pl.when(t == 0)
        def _():
            dw_ref[...] = jnp.zeros_like(dw_ref)
            db_ref[...] = jnp.zeros_like(db_ref)

        x = x_ref[...]
        halo = jnp.where(t > 0, h_ref[...], 0.0)
        dy = dy_ref[...]
        dhalo = jnp.where(t < nt - 1, dh_ref[...], 0.0)
        wv = w_ref[...]
        dx = wv[CONV_WIDTH - 1:CONV_WIDTH] * dy
        rows = [jnp.sum(dy * x, axis=0, keepdims=True)]
        for j in range(1, CONV_WIDTH):
            dx = dx + wv[CONV_WIDTH - 1 - j:CONV_WIDTH - j] * _shift_up(dy, dhalo, j)
            rows.insert(0, jnp.sum(dy * _shift_down(x, halo, j), axis=0, keepdims=True))
        dx_ref[...] = dx.astype(dx_ref.dtype)
        dw_ref[...] += jnp.concatenate(rows, axis=0)
        db_ref[...] += jnp.sum(dy, axis=0, keepdims=True)

    return pl.pallas_call(
        body, name=name, grid=(width // cb, nt),
        in_specs=[pl.BlockSpec((tt, cb), lambda c, t: (t, c0 + c)),
                  pl.BlockSpec((HALO, cb), lambda c, t: (jnp.maximum(t * hb - 1, 0), c0 + c)),
                  pl.BlockSpec((CONV_WIDTH, cb), lambda c, t: (0, c)),
                  pl.BlockSpec((tt, cb), lambda c, t: (t, c)),
                  pl.BlockSpec((HALO, cb), lambda c, t: (jnp.minimum((t + 1) * hb, nt * hb - 1), c))]
        + [pl.BlockSpec(memory_space=pl.ANY)] * len(extra),
        out_specs=[pl.BlockSpec((tt, cb), lambda c, t: (t, o0 + c)),
                   pl.BlockSpec((CONV_WIDTH, cb), lambda c, t: (0, c)),
                   pl.BlockSpec((1, cb), lambda c, t: (0, c))],
        out_shape=[jax.ShapeDtypeStruct((cfg.S, width) if into is None else into.shape, BF16),
                   jax.ShapeDtypeStruct((CONV_WIDTH, width), F32), jax.ShapeDtypeStruct((1, width), F32)],
        input_output_aliases={5: 0} if extra else {}, compiler_params=_cparams(2),
    )(src, src, w, dy, dy, *extra)


def write_cols(cfg, into, piece, col0, name):
    tt, width = _conv_tile(cfg), piece.shape[1]
    assert col0 % width == 0 and into.dtype == piece.dtype

    def body(p_ref, _, o_ref):
        o_ref[...] = p_ref[...]

    return pl.pallas_call(
        body, name=name, grid=(cfg.S // tt,),
        in_specs=[pl.BlockSpec((tt, width), lambda t: (t, 0)), pl.BlockSpec(memory_space=pl.ANY)],
        out_specs=pl.BlockSpec((tt, width), lambda t: (t, col0 // width)),
        out_shape=jax.ShapeDtypeStruct(into.shape, into.dtype), input_output_aliases={1: 0},
        compiler_params=_cparams(1))(piece, into)


def _gla_core(gh, q, k, v, g, glr, wg, bg, wn, st):
    n = glr.shape[0]
    causal = _lower_tri(n)
    outs, new = [], []
    for h in range(gh):
        log_a = log_sigmoid(bdot(glr, wg[h], 1, 0) + bg[h]) * (1.0 / GATE_NORM)
        bcum = cumsum_rows(log_a)
        b_last, b_mid = _row(bcum, n - 1), _row(bcum, n // 2)
        qs = q[h] * (HEAD_K ** -0.5)
        scores = jnp.where(causal, bdot(qs * jnp.exp(bcum - b_mid), k[h] * jnp.exp(b_mid - bcum), 1, 1), 0.0)
        o = bdot(scores, v[h], 1, 0) + bdot(qs * jnp.exp(bcum), st[h], 1, 1)
        new.append(st[h] * jnp.exp(b_last) + bdot(v[h], k[h] * jnp.exp(b_last - bcum), 0, 0))
        outs.append(rms(o, wn) * silu(g[h]))
    return jnp.concatenate(outs, axis=1), new


def _gla_ins(cfg, p0, wg, bg, wn, tmap):
    gh = cfg.GH
    ins = []
    for h in range(gh):
        ins.append((p0, (CHUNK, HEAD_K), lambda g, t, h=h: (tmap(t), h)))
    for h in range(gh):
        ins.append((p0, (CHUNK, HEAD_K), lambda g, t, h=h: (tmap(t), gh + h)))
    for h in range(gh):
        ins.append((p0, (CHUNK, HEAD_V), lambda g, t, h=h: (tmap(t), gh + h)))
    for h in range(gh):
        ins.append((p0, (CHUNK, HEAD_V), lambda g, t, h=h: (tmap(t), 2 * gh + h)))
    ins.append((p0, (CHUNK, LANES), lambda g, t: (tmap(t), 10 * gh)))
    for h in range(gh):
        ins.append((wg, (LANES, HEAD_K), lambda g, t, h=h: (0, h)))
    for h in range(gh):
        ins.append((bg, (1, HEAD_K), lambda g, t, h=h: (0, h)))
    ins.append((wn, (1, HEAD_V), lambda g, t: (0, 0)))
    return ins


def _gla_unpack(gh, vals):
    q, k, v, g = (vals[i * gh:(i + 1) * gh] for i in range(4))
    glr = vals[4 * gh]
    wg = vals[4 * gh + 1:5 * gh + 1]
    bg = vals[5 * gh + 1:6 * gh + 1]
    wn = vals[6 * gh + 1]
    return q, k, v, g, glr, wg, bg, wn, vals[6 * gh + 2:]


def gla_fwd(cfg, p0, wg, bg, wn, name):
    gh, nc = cfg.GH, cfg.S // CHUNK

    def fn(ins, st):
        q, k, v, g, glr, wgv, bgv, wnv, _ = _gla_unpack(gh, ins)
        out, new = _gla_core(gh, q, k, v, g, glr, wgv, bgv, wnv, st)
        return [out, jnp.stack(st)], [], new

    return seq_call(name, fn, (1, nc), _gla_ins(cfg, p0, wg, bg, wn, lambda t: t),
                    [((cfg.S, cfg.DV), BF16, (CHUNK, cfg.DV), lambda g, t: (t, 0)),
                     ((nc, gh, HEAD_V, HEAD_K), F32, (None, gh, HEAD_V, HEAD_K), lambda g, t: (t, 0, 0, 0))],
                    carries=[(HEAD_V, HEAD_K)] * gh)


def gla_bwd(cfg, p0, wg, bg, wn, states, dout, name):
    gh, nc = cfg.GH, cfg.S // CHUNK
    rev = lambda t: nc - 1 - t

    def fn(ins, dst):
        q, k, v, g, glr, wgv, bgv, wnv, rest = _gla_unpack(gh, ins)
        st_all, do = rest
        st = [st_all[h] for h in range(gh)]
        _, vjp = jax.vjp(functools.partial(_gla_core, gh), q, k, v, g, glr, wgv, bgv, wnv, st)
        dq, dk, dv, dg, dglr, dwg, dbg, dwn, dstate = vjp((do.astype(F32), list(dst)))
        return ([jnp.concatenate(list(dq) + list(dk) + list(dv) + list(dg), axis=1), dglr],
                [jnp.concatenate(dwg, axis=1), jnp.concatenate(dbg, axis=1), dwn], dstate)

    ins = _gla_ins(cfg, p0, wg, bg, wn, rev)
    ins.append((states, (None, gh, HEAD_V, HEAD_K), lambda g, t: (rev(t), 0, 0, 0)))
    ins.append((dout, (CHUNK, cfg.DV), lambda g, t: (rev(t), 0)))
    wide = 2 * cfg.DK + 2 * cfg.DV
    fixed = lambda g, t: (0, 0)
    return seq_call(name, fn, (1, nc), ins,
                    [((cfg.S, wide), BF16, (CHUNK, wide), lambda g, t: (rev(t), 0)),
                     ((cfg.S, LANES), BF16, (CHUNK, LANES), lambda g, t: (rev(t), 0))],
                    accs=[((LANES, cfg.DK), (LANES, cfg.DK), fixed), ((1, cfg.DK), (1, cfg.DK), fixed),
                          ((1, HEAD_V), (1, HEAD_V), fixed)],
                    carries=[(HEAD_V, HEAD_K)] * gh)


def _lru_core(xc, gate, wa, wi, ba, bi, lam, h_in):
    r = jax.nn.sigmoid(bdot(xc, wa, 1, 0) + ba)
    i = jax.nn.sigmoid(bdot(xc, wi, 1, 0) + bi)
    log_a = LRU_C * r * log_sigmoid(lam)
    a = jnp.exp(log_a)
    u = jnp.sqrt(-_expm1(2.0 * log_a)) * (i * xc)
    first = _rows_iota(a.shape) == 0
    h = lin_scan(a, u + jnp.where(first, a * h_in, 0.0))
    return h * gelu_tanh(gate), _row(h, a.shape[0] - 1)


def _lru_tile(cfg):
    return min(cfg.S, 512)


def _lru_ins(cfg, xc, p0, wa, wi, ba, bi, lam, tmap):
    tt, gh = _lru_tile(cfg), cfg.GH
    vec = lambda g, t: (0, g)
    return [(xc, (tt, LRU_BLOCK), lambda g, t: (tmap(t), g)),
            (p0, (tt, LRU_BLOCK), lambda g, t: (tmap(t), 8 * gh + g)),
            (wa, (None, LRU_BLOCK, LRU_BLOCK), lambda g, t: (g, 0, 0)),
            (wi, (None, LRU_BLOCK, LRU_BLOCK), lambda g, t: (g, 0, 0)),
            (ba, (1, LRU_BLOCK), vec), (bi, (1, LRU_BLOCK), vec), (lam, (1, LRU_BLOCK), vec)]


def lru_fwd(cfg, xc, p0, wa, wi, ba, bi, lam, name):
    tt, nb = _lru_tile(cfg), cfg.NB
    nt = cfg.S // tt

    def fn(ins, c):
        out, h_last = _lru_core(*ins, c[0])
        return [out, c[0]], [], [h_last]

    return seq_call(name, fn, (nb, nt), _lru_ins(cfg, xc, p0, wa, wi, ba, bi, lam, lambda t: t),
                    [((cfg.S, cfg.W), BF16, (tt, LRU_BLOCK), lambda g, t: (t, g)),
                     ((nb, nt, 1, LRU_BLOCK), F32, (None, None, 1, LRU_BLOCK), lambda g, t: (g, t, 0, 0))],
                    carries=[(1, LRU_BLOCK)])


def lru_bwd(cfg, xc, p0, wa, wi, ba, bi, lam, states, dout, name):
    tt, nb = _lru_tile(cfg), cfg.NB
    nt = cfg.S // tt
    rev = lambda t: nt - 1 - t

    def fn(ins, c):
        *fwd_ins, h_in, do = ins
        _, vjp = jax.vjp(_lru_core, *fwd_ins, h_in)
        dxc, dgate, dwa, dwi, dba, dbi, dlam, dh = vjp((do.astype(F32), c[0]))
        return [dxc, dgate], [dwa, dwi, dba, dbi, dlam], [dh]

    ins = _lru_ins(cfg, xc, p0, wa, wi, ba, bi, lam, rev)
    ins.append((states, (None, None, 1, LRU_BLOCK), lambda g, t: (g, rev(t), 0, 0)))
    ins.append((dout, (tt, LRU_BLOCK), lambda g, t: (rev(t), g)))
    mat = ((nb, LRU_BLOCK, LRU_BLOCK), (None, LRU_BLOCK, LRU_BLOCK), lambda g, t: (g, 0, 0))
    vec = ((1, cfg.W), (1, LRU_BLOCK), lambda g, t: (0, g))
    return seq_call(name, fn, (nb, nt), ins,
                    [((cfg.S, cfg.W), F32, (tt, LRU_BLOCK), lambda g, t: (rev(t), g)),
                     ((cfg.S, cfg.W), BF16, (tt, LRU_BLOCK), lambda g, t: (rev(t), g))],
                    accs=[mat, mat, vec, vec, vec], carries=[(1, LRU_BLOCK)])


def _ssd_core(xc, bc, cc, z, dt_raw, dt_bias, a_log, d_skip, gn, st):
    n = xc.shape[0]
    x, bm, cm = silu(xc), silu(bc), silu(cc)
    dt = softplus(dt_raw + dt_bias)
    acs = cumsum_rows_mxu(dt * (-jnp.exp(a_log)))
    acs_t = acs.T
    acs_e, dt_e = _expand_heads(acs), _expand_heads(dt)
    last_e = _expand_heads(_row(acs, n - 1))
    causal = _lower_tri(n)
    cb = bdot(cm, bm, 1, 1)
    xdt = x * dt_e
    y_diag = []
    for h, xh in enumerate(_split_heads(xdt)):
        seg = acs[:, h:h + 1] - acs_t[h:h + 1, :]
        decay = jnp.where(causal, jnp.exp(jnp.minimum(seg, 0.0)), 0.0)
        y_diag.append(bdot(cb * decay, xh, 1, 0))
    y = jnp.concatenate(y_diag, axis=1) + bdot(cm, st, 1, 0) * jnp.exp(acs_e)
    new = st * jnp.exp(last_e) + bdot(bm, xdt * jnp.exp(last_e - acs_e), 0, 0)
    y = (y + _expand_heads(d_skip) * x) * silu(z)
    return rms(y, gn), new


SSD_TILED = 5


SSD_FWD_CHUNKS = 4
SSD_BWD_CHUNKS = 1


def _ssd_tile(cfg, chunks):
    return min(cfg.S, chunks * CHUNK)


def _chunk_rows(v, s):
    return v[s * CHUNK:(s + 1) * CHUNK]


def _ssd_ins(cfg, tt, xc, p1, dt_raw, dt_bias, a_log, d_skip, gn, tmap):
    ng = cfg.NG
    vec = lambda g, t: (g, 0, 0)
    return [(xc, (tt, SSD_GW), lambda g, t: (tmap(t), g)),
            (xc, (tt, SSD_N), lambda g, t: (tmap(t), 4 * ng + g)),
            (xc, (tt, SSD_N), lambda g, t: (tmap(t), 5 * ng + g)),
            (p1, (tt, SSD_GW), lambda g, t: (tmap(t), g)),
            (dt_raw, (None, tt, LANES), lambda g, t: (g, tmap(t), 0)),
            (dt_bias, (None, 1, LANES), vec), (a_log, (None, 1, LANES), vec), (d_skip, (None, 1, LANES), vec),
            (gn, (1, SSD_GW), lambda g, t: (0, g))]


def ssd_fwd(cfg, xc, p1, dt_raw, dt_bias, a_log, d_skip, gn, name):
    ng, nc, tt = cfg.NG, cfg.S // CHUNK, _ssd_tile(cfg, SSD_FWD_CHUNKS)
    nsub = tt // CHUNK

    def fn(ins, c):
        tiled, params = ins[:SSD_TILED], ins[SSD_TILED:]
        st, outs, entered = c[0], [], []
        for s in range(nsub):
            entered.append(st)
            out, st = _ssd_core(*[_chunk_rows(v, s) for v in tiled], *params, st)
            outs.append(out)
        return [jnp.concatenate(outs, axis=0), jnp.stack(entered)], [], [st]

    return seq_call(name, fn, (ng, cfg.S // tt), _ssd_ins(cfg, tt, xc, p1, dt_raw, dt_bias, a_log, d_skip, gn, lambda t: t),
                    [((cfg.S, cfg.DI), BF16, (tt, SSD_GW), lambda g, t: (t, g)),
                     ((ng, nc, SSD_N, SSD_GW), F32, (None, nsub, SSD_N, SSD_GW), lambda g, t: (g, t, 0, 0))],
                    carries=[(SSD_N, SSD_GW)])


def ssd_bwd(cfg, xc, p1, dt_raw, dt_bias, a_log, d_skip, gn, states, dout, name):
    ng, tt = cfg.NG, _ssd_tile(cfg, SSD_BWD_CHUNKS)
    nsub, nt = tt // CHUNK, cfg.S // tt
    rev = lambda t: nt - 1 - t

    def fn(ins, c):
        tiled, params = ins[:SSD_TILED], ins[SSD_TILED:SSD_TILED + 4]
        st_all, do = ins[SSD_TILED + 4:]
        dst, pieces, acc = c[0], [None] * nsub, None
        for s in reversed(range(nsub)):
            _, vjp = jax.vjp(_ssd_core, *[_chunk_rows(v, s) for v in tiled], *params, st_all[s])
            grads = vjp((_chunk_rows(do, s).astype(F32), dst))
            pieces[s], dparams, dst = grads[:SSD_TILED], grads[SSD_TILED:SSD_TILED + 4], grads[SSD_TILED + 4]
            acc = dparams if acc is None else [x + y for x, y in zip(acc, dparams)]
        return [jnp.concatenate([p[i] for p in pieces], axis=0) for i in range(SSD_TILED)], list(acc), [dst]

    ins = _ssd_ins(cfg, tt, xc, p1, dt_raw, dt_bias, a_log, d_skip, gn, rev)
    ins.append((states, (None, nsub, SSD_N, SSD_GW), lambda g, t: (g, rev(t), 0, 0)))
    ins.append((dout, (tt, SSD_GW), lambda g, t: (rev(t), g)))
    col = lambda g, t: (rev(t), g)
    vec = ((ng, 1, LANES), (None, 1, LANES), lambda g, t: (g, 0, 0))
    return seq_call(name, fn, (ng, nt), ins,
                    [((cfg.S, cfg.DI), F32, (tt, SSD_GW), col),
                     ((cfg.S, ng * SSD_N), F32, (tt, SSD_N), col),
                     ((cfg.S, ng * SSD_N), F32, (tt, SSD_N), col),
                     ((cfg.S, cfg.OP), BF16, (tt, SSD_GW), col),
                     ((ng, cfg.S, LANES), F32, (None, tt, LANES), lambda g, t: (g, rev(t), 0))],
                    accs=[vec, vec, vec, ((1, cfg.DI), (1, SSD_GW), lambda g, t: (0, g))],
                    carries=[(SSD_N, SSD_GW)])


PACK_ALIGN = SUBLANES * LANES
PACK_ROWS = 256


def pack(arrays):
    pieces = []
    for a in arrays:
        flat = a.reshape(-1).astype(F32)
        pad = _round_up(flat.shape[0], PACK_ALIGN) - flat.shape[0]
        pieces.append(jnp.pad(flat, (0, pad)) if pad else flat)
    flat = jnp.concatenate(pieces)
    pad = _round_up(flat.shape[0], PACK_ROWS * LANES) - flat.shape[0]
    return jnp.pad(flat, (0, pad)).reshape(-1, LANES)


def unpack(buf, shapes):
    lead = buf.shape[:-2]
    flat = buf.reshape(lead + (-1,))
    out, off = [], 0
    for s in shapes:
        n = math.prod(s)
        out.append(flat[..., off:off + n].reshape(lead + tuple(s)))
        off += _round_up(n, PACK_ALIGN)
    return out


def _slots_to_cols(slots):
    return slots.transpose(1, 0, 2).reshape(slots.shape[1], -1)


def _even_in_padded(cfg, w):
    main = 2 * cfg.DK + 2 * cfg.DV
    return jnp.concatenate([w[:, :main], w[:, main + GATE_RANK:], w[:, main:main + GATE_RANK],
                            jnp.zeros((w.shape[0], cfg.EP - cfg.EVEN_IN), w.dtype)], axis=1)


def _even_in_unpadded(cfg, wp):
    main = 2 * cfg.DK + 2 * cfg.DV
    rest = main + 2 * cfg.W
    return jnp.concatenate([wp[:, :main], wp[:, rest:rest + GATE_RANK], wp[:, main:rest]], axis=1)


def _odd_in_padded(cfg, w):
    return jnp.concatenate([w, jnp.zeros((w.shape[0], cfg.OP - cfg.ODD_IN), w.dtype)], axis=1)


def _odd_in_unpadded(cfg, wp):
    return wp[:, :cfg.ODD_IN]


def _group_lanes(cfg, v):
    lead = v.shape[:-1]
    g = jnp.moveaxis(v.reshape(lead + (cfg.NG, SSD_HG)), -2, 0)
    return jnp.pad(g, [(0, 0)] * (g.ndim - 1) + [(0, LANES - SSD_HG)])


def _ungroup_lanes(cfg, g):
    v = jnp.moveaxis(g[..., :SSD_HG], 0, -2)
    return v.reshape(v.shape[:-2] + (cfg.NH,))


def train_step(cfg, p, loss_target):
    S, D = cfg.S, cfg.D
    me = 4 * lax.axis_index("x") + 2 * lax.axis_index("y") + lax.axis_index("c")

    big = ["ev_w_in", "ev_w_out", "od_w_in", "od_w_out", "ffn_w_gate", "ffn_w_up", "ffn_w_down"]
    small_sharded = ["ev_gla_w_gate", "ev_lru_conv_w", "od_norm", "od_conv_w", "od_conv_b", "od_gnorm"]
    replicated = ["ev_norm", "ev_gla_b_gate", "ev_gla_w_onorm", "ev_lru_conv_b", "ev_lru_w_a", "ev_lru_b_a",
                  "ev_lru_w_i", "ev_lru_b_i", "ev_lru_lam", "od_dt_bias", "od_a_log", "od_d_skip", "ffn_norm",
                  "final_norm"]

    transposed = ("ffn_w_gate", "ffn_w_up")
    view = lambda n, a: jnp.swapaxes(a, 1, 2) if n in transposed else a
    wb = {n: view(n, p[n]).astype(BF16) for n in big}
    ffn_items = lambda l: [(wb["ffn_w_gate"], l), (wb["ffn_w_up"], l), (wb["ffn_w_down"], l)]
    ss_shapes = [p[n].shape for n in small_sharded]
    groups = [[(pack([p[n] for n in small_sharded]), None), (wb["ev_w_in"], 0), (wb["ev_w_out"], 0)], ffn_items(0),
              [(wb["od_w_in"], 0), (wb["od_w_out"], 0)], ffn_items(1)]
    gathers, tokens = [], []
    for i, g in enumerate(groups):
        lands = [_landing_zone(a, lead, me, f"gather_own{i}_{j}") for j, (a, lead) in enumerate(g)]
        handle, token = exchange_start(g, True, lands, f"gather_start{i}", NEAR_PEERS)
        gathers.append(handle)
        tokens.append(token)
    all_started = tokens[0][:1, :1] + tokens[1][:1, :1] + tokens[2][:1, :1] + tokens[3][:1, :1]

    def gathered(i, after):
        return relay(exchange_wait(gathers[i], after, f"gather_wait{i}")[1], f"gather_relay{i}")

    ss_all, gw_ev_in, gw_ev_out = gathered(0, all_started)
    gs = dict(zip(small_sharded, unpack(ss_all, ss_shapes)))
    w_ev_in = cols_from_slots(gw_ev_in, functools.partial(_even_in_padded, cfg), cfg.EP, "ev_w_in_cols")
    w_ev_out = gw_ev_out.reshape(D, D)

    gla_wg = jnp.pad(_slots_to_cols(gs["ev_gla_w_gate"][:, 0]), ((0, LANES - GATE_RANK), (0, 0)))
    lru_cw = _slots_to_cols(gs["ev_lru_conv_w"][:, 0])
    od_norm = gs["od_norm"].transpose(1, 0, 2).reshape(1, D)
    od_cw = _slots_to_cols(gs["od_conv_w"][:, 0])
    od_cb = gs["od_conv_b"].transpose(1, 0, 2).reshape(1, cfg.CD)
    od_gn = gs["od_gnorm"].transpose(1, 0, 2).reshape(1, cfg.DI)

    x0 = p["x"][0]
    target = loss_target[0]
    ev_norm = p["ev_norm"] + all_started
    bg = p["ev_gla_b_gate"]
    wn = p["ev_gla_w_onorm"]
    lru_cb = p["ev_lru_conv_b"]
    wa, wi = p["ev_lru_w_a"][0], p["ev_lru_w_i"][0]
    ba, bi, lam = p["ev_lru_b_a"], p["ev_lru_b_i"], p["ev_lru_lam"]
    dt_bias, a_log, d_skip = (_group_lanes(cfg, p[n]) for n in ("od_dt_bias", "od_a_log", "od_d_skip"))
    ffn_norm = [p["ffn_norm"][l:l + 1] for l in range(2)]
    final_norm = p["final_norm"].reshape(1, D)

    def ffn_forward(l, x):
        w_gate, w_up, w_down = gathered(1 + 2 * l, x)
        h = norm_fwd(cfg, x, ffn_norm[l], f"ffn{l}_norm")
        gate, up, act = ffn_gate_up(h, w_gate, w_up, f"ffn{l}_gate_up")
        out = matmul(act, w_down, a_slot="k", b_slot="k", res=x, name=f"ffn{l}_down", tn=512)
        return out, (h, gate, up, act, w_gate, w_up, w_down)

    h0 = norm_fwd(cfg, x0, ev_norm, "ev_norm")
    p0 = matmul(h0, w_ev_in, name="ev_in", tn=768)
    gla_out, gla_states = gla_fwd(cfg, p0, gla_wg, bg, wn, "gla_fwd")
    lru_col = 2 * cfg.DK + 2 * cfg.DV
    lru_xc = conv_fwd(cfg, p0, lru_col, cfg.W, lru_cw, lru_cb, "lru_conv")
    lru_out, lru_states = lru_fwd(cfg, lru_xc, p0, wa, wi, ba, bi, lam, "lru_fwd")
    mix = jnp.concatenate([gla_out, lru_out], axis=1)
    x1 = matmul(mix, w_ev_out, res=x0, name="ev_out")
    x2, ffn0_saved = ffn_forward(0, x1)

    gw_od_in, gw_od_out = gathered(2, x2)
    w_od_in = cols_from_slots(gw_od_in, functools.partial(_odd_in_padded, cfg), cfg.OP, "od_w_in_cols")
    w_od_out = gw_od_out.reshape(cfg.DI, D)
    h2 = norm_fwd(cfg, x2, od_norm, "od_norm")
    p1 = matmul(h2, w_od_in, name="od_in", tn=768)
    od_xc = conv_fwd(cfg, p1, cfg.DI, cfg.CD, od_cw, od_cb, "od_conv")
    dt_col = cfg.DI + cfg.CD
    dt_raw = _group_lanes(cfg, p1[:, dt_col:dt_col + cfg.NH])
    ssd_out, ssd_states = ssd_fwd(cfg, od_xc, p1, dt_raw, dt_bias, a_log, d_skip, od_gn, "ssd_fwd")
    x3 = matmul(ssd_out, w_od_out, res=x2, name="od_out")
    x4, ffn1_saved = ffn_forward(1, x3)

    loss_part, dx4, d_final_norm = head_fwd_bwd(cfg, x4, final_norm, target, "head")
    loss = lax.psum(loss_part[0, 0], MESH_AXES)

    def ffn_backward(l, x, saved, dx_out, after):
        h, gate, up, act, w_gate, w_up, w_down = saved
        dgate, dup = ffn_dgate_dup(dx_out, w_down, gate, up, after, f"ffn{l}_dgate_dup")
        d_down = matmul(act, dx_out, ta=True, a_slot="m", out_dtype=BF16, name=f"ffn{l}_dwdown")
        sent_down, token = start_grads([d_down], f"grads_start_ffn{l}_down")
        d_gate = matmul(dgate, h, ta=True, a_slot="m", after=token, out_dtype=BF16, name=f"ffn{l}_dwgate")
        d_up = matmul(dup, h, ta=True, a_slot="m", out_dtype=BF16, name=f"ffn{l}_dwup")
        sent_gate_up, token = start_grads([d_gate, d_up], f"grads_start_ffn{l}")
        dh = matmul(dgate, w_gate, a_slot="k", b_slot="k", after=token, name=f"ffn{l}_dh_gate", tn=512)
        dh = matmul(dup, w_up, a_slot="k", b_slot="k", res=dh, name=f"ffn{l}_dh_up", tn=512)
        dx, dnorm = norm_bwd(cfg, x, ffn_norm[l], dh, dx_out, f"ffn{l}_norm_bwd")
        return dx, dnorm, (sent_down, sent_gate_up)

    def start_grads(arrays, name):
        return exchange_start([(a, None) for a in arrays], False, [lax.empty(a.shape, a.dtype) for a in arrays], name)

    dx3, d_ffn_norm1, sent_ffn1 = ffn_backward(1, x3, ffn1_saved, dx4, None)

    d_ssd_out = matmul(dx3, w_od_out, tb=True, name="od_dmix")
    d_od_out = matmul(ssd_out, dx3, ta=True, out_dtype=BF16, name="od_dwout")
    dxs, dbm, dcm, dz, d_dt_raw, d_dt_bias, d_a_log, d_d_skip, d_od_gn = ssd_bwd(
        cfg, od_xc, p1, dt_raw, dt_bias, a_log, d_skip, od_gn, ssd_states, d_ssd_out, "ssd_bwd")
    dp1, conv_parts, col = dz, [], 0
    for part, dy in (("x", dxs), ("b", dbm), ("c", dcm)):
        width = dy.shape[1]
        dp1, dcw, dcb = conv_bwd(cfg, p1, cfg.DI + col, width, od_cw[:, col:col + width], dy, "od_conv_bwd_" + part,
                                 into=dp1, into_col0=cfg.DI + col)
        conv_parts.append((dcw, dcb))
        col += width
    d_od_cw = jnp.concatenate([c[0] for c in conv_parts], axis=1)
    d_od_cb = jnp.concatenate([c[1] for c in conv_parts], axis=1)
    d_dt = _ungroup_lanes(cfg, d_dt_raw).astype(BF16)
    tail = jnp.concatenate([d_dt, jnp.zeros((S, cfg.OP - cfg.ODD_IN), BF16)], axis=1)
    dp1 = write_cols(cfg, dp1, tail, cfg.DI + cfg.CD, "od_dt_cols")
    dh2 = matmul(dp1, w_od_in, tb=True, name="od_dh", tk=1536)
    d_od_in = matmul(h2, dp1, ta=True, out_dtype=BF16, name="od_dwin", tn=768)
    dx2, d_od_norm = norm_bwd(cfg, x2, od_norm, dh2, dx3, "od_norm_bwd")
    d_od_in_slots = slots_from_cols(d_od_in, functools.partial(_odd_in_unpadded, cfg), p["od_w_in"].shape[2],
                                    "od_dwin_slots")
    sent_od, token = start_grads([d_od_in_slots, d_od_out.reshape((N_DEV,) + p["od_w_out"].shape[1:])], "grads_start_od")

    dx1, d_ffn_norm0, sent_ffn0 = ffn_backward(0, x1, ffn0_saved, dx2, token)

    d_ev_out = matmul(mix, dx1, ta=True, out_dtype=BF16, name="ev_dwout")
    sent_ev_out, token = start_grads([d_ev_out.reshape((N_DEV,) + p["ev_w_out"].shape[1:])], "grads_start_ev_out")
    d_mix = matmul(dx1, w_ev_out, tb=True, after=token, name="ev_dmix")
    d_qkvg, d_glr, d_gla_wg, d_bg, d_wn = gla_bwd(cfg, p0, gla_wg, bg, wn, gla_states, d_mix[:, :cfg.DV], "gla_bwd")
    d_lru_xc, d_gate_br, d_wa, d_wi, d_ba, d_bi, d_lam = lru_bwd(
        cfg, lru_xc, p0, wa, wi, ba, bi, lam, lru_states, d_mix[:, cfg.DV:], "lru_bwd")
    d_xbr, d_lru_cw, d_lru_cb = conv_bwd(cfg, p0, lru_col, cfg.W, lru_cw, d_lru_xc, "lru_conv_bwd")
    dp0 = jnp.concatenate([d_qkvg, d_xbr, d_gate_br, d_glr, jnp.zeros((S, cfg.EP - lru_col - 2 * cfg.W - LANES), BF16)],
                          axis=1)
    d_ev_in = matmul(h0, dp0, ta=True, out_dtype=BF16, name="ev_dwin", tn=768)
    d_ev_in_slots = slots_from_cols(d_ev_in, functools.partial(_even_in_unpadded, cfg), p["ev_w_in"].shape[2],
                                    "ev_dwin_slots")
    sent_ev_in, token = start_grads([d_ev_in_slots], "grads_start_ev_in")
    dh0 = matmul(dp0, w_ev_in, tb=True, after=token, name="ev_dh", tk=1792)
    grad_x, d_ev_norm = norm_bwd(cfg, x0, ev_norm, dh0, dx1, "ev_norm_bwd")

    out = {"loss": loss, "grad_x": grad_x[None]}

    def update(names, sent, after, wait_name):
        s, r = exchange_wait(sent[0], after, wait_name + "0")
        sends, recvs = [[a] for a in s], [[a] for a in r]
        for extra in sent[1:]:
            s, r = exchange_wait(extra, after, wait_name + "1")
            for i in range(len(names)):
                sends[i].append(s[i])
                recvs[i].append(r[i])
        for i, n in enumerate(names):
            flip = n in ("ev_w_in", "od_w_in")
            shard = lambda a: jnp.swapaxes(a, 1, 2) if flip else view(n, a)
            res = adamw_sharded(recvs[i], sends[i], me, shard(p[n]), shard(p["m_" + n]), shard(p["v_" + n]),
                                "adamw_" + n, transposed=flip)
            out["grad_" + n], out["delta_" + n], out["new_m_" + n], out["new_v_" + n] = (shard(a) for a in res)
        return res[-1]

    small_full = {
        "ev_gla_w_gate": d_gla_wg[:GATE_RANK][None], "ev_lru_conv_w": d_lru_cw[None], "od_norm": d_od_norm,
        "od_conv_w": d_od_cw[None], "od_conv_b": d_od_cb, "od_gnorm": d_od_gn,
        "ev_norm": d_ev_norm, "ev_gla_b_gate": d_bg, "ev_gla_w_onorm": d_wn, "ev_lru_conv_b": d_lru_cb,
        "ev_lru_w_a": d_wa[None], "ev_lru_b_a": d_ba, "ev_lru_w_i": d_wi[None], "ev_lru_b_i": d_bi,
        "ev_lru_lam": d_lam, "od_dt_bias": _ungroup_lanes(cfg, d_dt_bias), "od_a_log": _ungroup_lanes(cfg, d_a_log),
        "od_d_skip": _ungroup_lanes(cfg, d_d_skip), "ffn_norm": jnp.concatenate([d_ffn_norm0, d_ffn_norm1], axis=0),
        "final_norm": d_final_norm.reshape(D),
    }
    small = small_sharded + replicated
    small_packed = pack([small_full[n] for n in small])
    sent_small, token = exchange_start([(small_packed, None)], True,
                                       [_landing_zone(small_packed, None, me, "gather_small_grads_own")],
                                       "gather_small_grads")

    done = update(["od_w_in", "od_w_out"], [sent_od], token, "grads_wait_od")
    done = update(["ffn_w_down"], [sent_ffn0[0], sent_ffn1[0]], done, "grads_wait_ffn_down")
    done = update(["ffn_w_gate", "ffn_w_up"], [sent_ffn0[1], sent_ffn1[1]], done, "grads_wait_ffn")
    done = update(["ev_w_out"], [sent_ev_out], done, "grads_wait_ev_out")
    done = update(["ev_w_in"], [sent_ev_in], done, "grads_wait_ev_in")

    small_all = exchange_wait(sent_small, done, "gather_small_grads_wait")[1][0]
    g_small = dict(zip(small, unpack(reduce_slots(small_all, "sum_small_grads"), [small_full[n].shape for n in small])))
    for n in small_sharded:
        width = p[n].shape[-1]
        g_small[n] = lax.dynamic_slice_in_dim(g_small[n], me * width, width, axis=g_small[n].ndim - 1)
    shapes = [p[n].shape for n in small]
    g_buf = pack([g_small[n] for n in small])
    delta, new_m, new_v = adamw(pack([p[n] for n in small]), g_buf, pack([p["m_" + n] for n in small]),
                                pack([p["v_" + n] for n in small]), "adamw_small")
    for kind, buf in (("grad_", g_buf), ("delta_", delta), ("new_m_", new_m), ("new_v_", new_v)):
        for n, a in zip(small, unpack(buf, shapes)):
            out[kind + n] = a
    return out


WEIGHTS = ['ev_norm', 'ev_w_in', 'ev_gla_w_gate', 'ev_gla_b_gate', 'ev_gla_w_onorm', 'ev_lru_conv_w', 'ev_lru_conv_b',
           'ev_lru_w_a', 'ev_lru_b_a', 'ev_lru_w_i', 'ev_lru_b_i', 'ev_lru_lam', 'ev_w_out', 'od_norm', 'od_w_in',
           'od_conv_w', 'od_conv_b', 'od_dt_bias', 'od_a_log', 'od_d_skip', 'od_gnorm', 'od_w_out', 'ffn_norm',
           'ffn_w_gate', 'ffn_w_up', 'ffn_w_down', 'final_norm']


def kernel(x, ev_norm, ev_w_in, ev_gla_w_gate, ev_gla_b_gate, ev_gla_w_onorm, ev_lru_conv_w, ev_lru_conv_b, ev_lru_w_a, ev_lru_b_a, ev_lru_w_i, ev_lru_b_i, ev_lru_lam, ev_w_out, od_norm, od_w_in, od_conv_w, od_conv_b, od_dt_bias, od_a_log, od_d_skip, od_gnorm, od_w_out, ffn_norm, ffn_w_gate, ffn_w_up, ffn_w_down, final_norm, loss_target, m_ev_norm, m_ev_w_in, m_ev_gla_w_gate, m_ev_gla_b_gate, m_ev_gla_w_onorm, m_ev_lru_conv_w, m_ev_lru_conv_b, m_ev_lru_w_a, m_ev_lru_b_a, m_ev_lru_w_i, m_ev_lru_b_i, m_ev_lru_lam, m_ev_w_out, m_od_norm, m_od_w_in, m_od_conv_w, m_od_conv_b, m_od_dt_bias, m_od_a_log, m_od_d_skip, m_od_gnorm, m_od_w_out, m_ffn_norm, m_ffn_w_gate, m_ffn_w_up, m_ffn_w_down, m_final_norm, v_ev_norm, v_ev_w_in, v_ev_gla_w_gate, v_ev_gla_b_gate, v_ev_gla_w_onorm, v_ev_lru_conv_w, v_ev_lru_conv_b, v_ev_lru_w_a, v_ev_lru_b_a, v_ev_lru_w_i, v_ev_lru_b_i, v_ev_lru_lam, v_ev_w_out, v_od_norm, v_od_w_in, v_od_conv_w, v_od_conv_b, v_od_dt_bias, v_od_a_log, v_od_d_skip, v_od_gnorm, v_od_w_out, v_ffn_norm, v_ffn_w_gate, v_ffn_w_up, v_ffn_w_down, v_final_norm):
    args = dict(locals())
    p = {n: a for n, a in args.items() if n != "loss_target"}
    cfg = Cfg(S=x.shape[1], D=x.shape[2], DFF=ffn_w_gate.shape[2] * N_DEV)
    out = train_step(cfg, p, loss_target)
    return (out["loss"], out["grad_x"], *[out["grad_" + w] for w in WEIGHTS], *[out["delta_" + w] for w in WEIGHTS],
            *[out["new_m_" + w] for w in WEIGHTS], *[out["new_v_" + w] for w in WEIGHTS])
```

```python
import functools
import math
from typing import NamedTuple

import jax
import jax.numpy as jnp
from jax import lax
from jax.experimental import pallas as pl
from jax.experimental.pallas import tpu as pltpu

F32 = jnp.float32
BF16 = jnp.bfloat16
MESH_AXES = ("x", "y", "c")
N_DEV = 8
LANES = 128
SUBLANES = 8
VMEM_LIMIT = 56 * 1024 * 1024

NORM_EPS = 1e-6
CONV_WIDTH = 4
CHUNK = 64
HEAD_K = 128
HEAD_V = 256
GATE_RANK = 16
GATE_NORM = 16.0
LRU_BLOCK = 128
LRU_C = 8.0
SSD_P = 64
SSD_N = 128
SSD_HG = 8
SSD_GW = SSD_HG * SSD_P

ADAM_LR = 0.001
ADAM_B1 = 0.9
ADAM_B2 = 0.999
ADAM_EPS = 1e-08
ADAM_WD = 0.01
ADAM_STEP = 10


class Cfg(NamedTuple):
    S: int
    D: int
    DFF: int

    @property
    def GH(self):
        return self.D // 512

    @property
    def NB(self):
        return self.D // 256

    @property
    def NG(self):
        return self.D // 256

    @property
    def DK(self):
        return HEAD_K * self.GH

    @property
    def DV(self):
        return HEAD_V * self.GH

    @property
    def W(self):
        return LRU_BLOCK * self.NB

    @property
    def DI(self):
        return SSD_GW * self.NG

    @property
    def CD(self):
        return self.DI + 2 * self.NG * SSD_N

    @property
    def NH(self):
        return SSD_HG * self.NG

    @property
    def EVEN_IN(self):
        return 2 * self.DK + 2 * self.DV + GATE_RANK + 2 * self.W

    @property
    def ODD_IN(self):
        return self.DI + self.CD + self.NH

    @property
    def EP(self):
        return _round_up(2 * self.DK + 2 * self.DV + 2 * self.W + LANES, 768)

    @property
    def OP(self):
        return _round_up(self.DI + self.CD + LANES, 768)


def _round_up(n, m):
    return (n + m - 1) // m * m


def _tile(n, pref):
    if n <= pref:
        return n
    t = pref - pref % LANES
    while n % t:
        t -= LANES
    return t


def _cparams(n_axes):
    return pltpu.CompilerParams(dimension_semantics=("arbitrary",) * n_axes, vmem_limit_bytes=VMEM_LIMIT)


def _dg(a, b, ca, cb):
    return lax.dot_general(a.astype(BF16), b.astype(BF16), (((ca,), (cb,)), ((), ())), preferred_element_type=F32)


@functools.partial(jax.custom_vjp, nondiff_argnums=(2, 3))
def bdot(a, b, ca, cb):
    return _dg(a, b, ca, cb)


def _bdot_fwd(a, b, ca, cb):
    return _dg(a, b, ca, cb), (a, b)


def _bdot_bwd(ca, cb, res, g):
    a, b = res
    da = _dg(g, b, 1, 1 - cb) if ca == 1 else _dg(b, g, 1 - cb, 1)
    db = _dg(a, g, 1 - ca, 0) if cb == 0 else _dg(g, a, 0, 1 - ca)
    return da.astype(a.dtype), db.astype(b.dtype)


bdot.defvjp(_bdot_fwd, _bdot_bwd)


def _lower_tri(n):
    r = lax.broadcasted_iota(jnp.int32, (n, n), 0)
    c = lax.broadcasted_iota(jnp.int32, (n, n), 1)
    return c <= r


def _running_sum(x, reverse):
    n = x.shape[0]
    r = lax.broadcasted_iota(jnp.int32, x.shape, 0)
    d = 1
    while d < n:
        if reverse:
            x = x + jnp.where(r < n - d, pltpu.roll(x, n - d, 0), 0.0)
        else:
            x = x + jnp.where(r >= d, pltpu.roll(x, d, 0), 0.0)
        d *= 2
    return x


@jax.custom_vjp
def cumsum_rows(x):
    return _running_sum(x, False)


cumsum_rows.defvjp(lambda x: (_running_sum(x, False), None), lambda _, g: (_running_sum(g, True),))


def _tri_dot(x, transposed):
    n = x.shape[0]
    return lax.dot_general(_lower_tri(n).astype(F32), x, (((0 if transposed else 1,), (0,)), ((), ())),
                           precision=lax.Precision.HIGHEST, preferred_element_type=F32)


@jax.custom_vjp
def cumsum_rows_mxu(x):
    return _tri_dot(x, False)


cumsum_rows_mxu.defvjp(lambda x: (_tri_dot(x, False), None), lambda _, g: (_tri_dot(g, True),))


def _row(x, i):
    r = lax.broadcasted_iota(jnp.int32, x.shape, 0)
    return jnp.sum(jnp.where(r == i, x, 0.0), axis=0, keepdims=True)


def _softplus_raw(x):
    return jnp.maximum(x, 0.0) + jnp.log(1.0 + jnp.exp(-jnp.abs(x)))


@jax.custom_vjp
def softplus(x):
    return _softplus_raw(x)


softplus.defvjp(lambda x: (_softplus_raw(x), x), lambda x, g: (g * jax.nn.sigmoid(x),))


@jax.custom_vjp
def log_sigmoid(x):
    return -_softplus_raw(-x)


log_sigmoid.defvjp(lambda x: (-_softplus_raw(-x), x), lambda x, g: (g * jax.nn.sigmoid(-x),))


def silu(x):
    return x * jax.nn.sigmoid(x)


def gelu_tanh(x):
    return 0.5 * x * (1.0 + jnp.tanh(math.sqrt(2.0 / math.pi) * (x + 0.044715 * (x * x * x))))


def _expm1(x):
    series = x * (1.0 + 0.5 * x * (1.0 + (1.0 / 3.0) * x))
    return jnp.where(jnp.abs(x) < 1e-2, series, jnp.exp(x) - 1.0)


def rms(x, w):
    return x * lax.rsqrt(jnp.mean(x * x, axis=-1, keepdims=True) + NORM_EPS) * w


def _rows_iota(shape):
    return lax.broadcasted_iota(jnp.int32, shape, 0)


def _scan_up(a, u):
    n = a.shape[0]
    r = _rows_iota(a.shape)
    d = 1
    while d < n:
        m = r >= d
        a_s = jnp.where(m, pltpu.roll(a, d, 0), 1.0)
        u_s = jnp.where(m, pltpu.roll(u, d, 0), 0.0)
        u = a * u_s + u
        a = a * a_s
        d *= 2
    return u


def _scan_down(a, u):
    n = a.shape[0]
    r = _rows_iota(a.shape)
    d = 1
    while d < n:
        m = r < n - d
        a_s = jnp.where(m, pltpu.roll(a, n - d, 0), 1.0)
        u_s = jnp.where(m, pltpu.roll(u, n - d, 0), 0.0)
        u = a * u_s + u
        a = a * a_s
        d *= 2
    return u


@jax.custom_vjp
def lin_scan(a, u):
    return _scan_up(a, u)


def _lin_scan_fwd(a, u):
    h = _scan_up(a, u)
    return h, (a, h)


def _lin_scan_bwd(res, g):
    a, h = res
    n = a.shape[0]
    r = _rows_iota(a.shape)
    a_next = jnp.where(r < n - 1, pltpu.roll(a, n - 1, 0), 0.0)
    gt = _scan_down(a_next, g)
    h_prev = jnp.where(r >= 1, pltpu.roll(h, 1, 0), 0.0)
    return gt * h_prev, gt


lin_scan.defvjp(_lin_scan_fwd, _lin_scan_bwd)


def _expand_heads(v):
    r = v.shape[0]
    return jnp.concatenate([jnp.broadcast_to(v[:, h:h + 1], (r, SSD_P)) for h in range(SSD_HG)], axis=1)


@jax.custom_vjp
def _split_heads(x):
    return tuple(x[:, h * SSD_P:(h + 1) * SSD_P] for h in range(SSD_HG))


_split_heads.defvjp(lambda x: (_split_heads(x), None), lambda _, gs: (jnp.concatenate(gs, axis=1),))


def matmul(a, b, *, ta=False, tb=False, a_slot=None, b_slot=None, b_lead=(), res=None, after=None, out_dtype=F32,
           name, tm=1024, tn=1024, tk=2048):
    lead = tuple(b_lead)
    ra, ca_ = a.shape[-2:]
    rb, cb_ = b.shape[-2:]
    m_st, ka_st = (ca_, ra) if ta else (ra, ca_)
    kb_st, n_st = (cb_, rb) if tb else (rb, cb_)
    kslot = a_slot == "k"
    assert kslot == (b_slot == "k")
    m = m_st * (N_DEV if a_slot == "m" else 1)
    n = n_st * (N_DEV if b_slot == "n" else 1)
    assert ka_st == kb_st, (a.shape, b.shape, ta, tb)
    tm = m_st if a_slot == "m" else _tile(m, tm)
    tn = n_st if b_slot == "n" else _tile(n, tn)
    tk = ka_st if kslot else _tile(ka_st, tk)
    nk = ka_st // tk
    ca, cb = (0 if ta else 1), (1 if tb else 0)
    nl = (None,) * len(lead)

    if a_slot is None:
        a_spec = pl.BlockSpec((tk, tm), lambda i, j, k: (k, i)) if ta else pl.BlockSpec((tm, tk), lambda i, j, k: (i, k))
    elif a_slot == "m":
        a_spec = (pl.BlockSpec((None, tk, tm), lambda i, j, k: (i, k, 0)) if ta
                  else pl.BlockSpec((None, tm, tk), lambda i, j, k: (i, 0, k)))
    else:
        a_spec = (pl.BlockSpec((N_DEV, tk, tm), lambda i, j, k: (0, 0, i)) if ta
                  else pl.BlockSpec((N_DEV, tm, tk), lambda i, j, k: (0, i, 0)))
    if b_slot is None:
        b_spec = (pl.BlockSpec(nl + (tn, tk), lambda i, j, k: lead + (j, k)) if tb
                  else pl.BlockSpec(nl + (tk, tn), lambda i, j, k: lead + (k, j)))
    elif b_slot == "n":
        b_spec = (pl.BlockSpec((None,) + nl + (tn, tk), lambda i, j, k: (j,) + lead + (0, k)) if tb
                  else pl.BlockSpec((None,) + nl + (tk, tn), lambda i, j, k: (j,) + lead + (k, 0)))
    else:
        b_spec = (pl.BlockSpec((N_DEV,) + nl + (tn, tk), lambda i, j, k: (0,) + lead + (j, 0)) if tb
                  else pl.BlockSpec((N_DEV,) + nl + (tk, tn), lambda i, j, k: (0,) + lead + (0, j)))
    if a_slot == "m":
        o_spec, o_shape = pl.BlockSpec((None, tm, tn), lambda i, j, k: (i, 0, j)), (N_DEV, tm, n)
    elif b_slot == "n":
        o_spec, o_shape = pl.BlockSpec((None, tm, tn), lambda i, j, k: (j, i, 0)), (N_DEV, m, tn)
    else:
        o_spec, o_shape = pl.BlockSpec((tm, tn), lambda i, j, k: (i, j)), (m, n)
    assert res is None or (a_slot != "m" and b_slot != "n")

    def dot(x, y):
        return lax.dot_general(x.astype(BF16), y.astype(BF16), (((ca,), (cb,)), ((), ())), preferred_element_type=F32)

    def body(*refs):
        a_ref, b_ref = refs[:2]
        r_ref = refs[2] if res is not None else None
        o_ref = refs[2 + (res is not None) + (after is not None)]

        def finish(acc):
            if r_ref is not None:
                acc = acc + r_ref[...].astype(F32)
            o_ref[...] = acc.astype(o_ref.dtype)

        if kslot:
            acc = dot(a_ref[0], b_ref[0])
            for s in range(1, N_DEV):
                acc = acc + dot(a_ref[s], b_ref[s])
            finish(acc)
        elif nk == 1:
            finish(dot(a_ref[...], b_ref[...]))
        else:
            acc_ref = refs[-1]
            k = pl.program_id(2)

            @pl.when(k == 0)
            def _():
                acc_ref[...] = dot(a_ref[...], b_ref[...])

            @pl.when(k > 0)
            def _():
                acc_ref[...] += dot(a_ref[...], b_ref[...])

            @pl.when(k == nk - 1)
            def _():
                finish(acc_ref[...])

    in_specs = [a_spec, b_spec]
    args = [a, b]
    if res is not None:
        in_specs.append(pl.BlockSpec((tm, tn), lambda i, j, k: (i, j)))
        args.append(res)
    if after is not None:
        in_specs.append(pl.BlockSpec(memory_space=pl.ANY))
        args.append(after)
    return pl.pallas_call(
        body, name=name, grid=(m // tm, n // tn, nk), in_specs=in_specs, out_specs=o_spec,
        out_shape=jax.ShapeDtypeStruct(o_shape, out_dtype),
        scratch_shapes=[pltpu.VMEM((tm, tn), F32)] if nk > 1 else [], compiler_params=_cparams(3),
    )(*args)


def seq_call(name, fn, grid, ins, outs, accs=(), carries=()):
    n_in, n_out, n_acc = len(ins), len(outs), len(accs)

    def body(*refs):
        in_refs = refs[:n_in]
        out_refs = refs[n_in:n_in + n_out]
        acc_refs = refs[n_in + n_out:n_in + n_out + n_acc]
        c_refs = refs[n_in + n_out + n_acc:]

        if acc_refs or c_refs:
            @pl.when(pl.program_id(1) == 0)
            def _():
                for r in tuple(acc_refs) + tuple(c_refs):
                    r[...] = jnp.zeros_like(r)

        o, a, c = fn([r[...] for r in in_refs], [r[...] for r in c_refs])
        for r, v in zip(out_refs, o, strict=True):
            r[...] = v.astype(r.dtype)
        for r, v in zip(acc_refs, a, strict=True):
            r[...] += v
        for r, v in zip(c_refs, c, strict=True):
            r[...] = v

    return pl.pallas_call(
        body, name=name, grid=grid,
        in_specs=[pl.BlockSpec(blk, im) for _, blk, im in ins],
        out_specs=[pl.BlockSpec(blk, im) for _, _, blk, im in outs] + [pl.BlockSpec(blk, im) for _, blk, im in accs],
        out_shape=[jax.ShapeDtypeStruct(s, d) for s, d, _, _ in outs] + [jax.ShapeDtypeStruct(s, F32) for s, _, _ in accs],
        scratch_shapes=[pltpu.VMEM(s, F32) for s in carries], compiler_params=_cparams(2),
    )(*[a for a, _, _ in ins])


def exchange(arrays, gather, name, after=None):
    n = len(arrays)
    extra = [] if after is None else [after]

    def body(*refs):
        x_refs, o_refs = refs[:n], refs[n + len(extra):2 * n + len(extra)]
        send_sems, recv_sems, local_sems = refs[2 * n + len(extra):]
        pos = [lax.axis_index(ax) for ax in MESH_AXES]
        me = 4 * pos[0] + 2 * pos[1] + pos[2]
        copies = []
        for i in range(n):
            own = pltpu.make_async_copy(x_refs[i] if gather else x_refs[i].at[me], o_refs[i].at[me], local_sems.at[i])
            own.start()
            copies.append(own)
        for k in range(1, N_DEV):
            bits = ((k >> 2) & 1, (k >> 1) & 1, k & 1)
            peer = tuple(1 - p if b else p for p, b in zip(pos, bits))
            peer_id = 4 * peer[0] + 2 * peer[1] + peer[2]
            for i in range(n):
                cp = pltpu.make_async_remote_copy(
                    src_ref=x_refs[i] if gather else x_refs[i].at[peer_id], dst_ref=o_refs[i].at[me],
                    send_sem=send_sems.at[i * (N_DEV - 1) + k - 1], recv_sem=recv_sems.at[i * (N_DEV - 1) + k - 1],
                    device_id=peer, device_id_type=pl.DeviceIdType.MESH)
                cp.start()
                copies.append(cp)
        for cp in copies:
            cp.wait()

    hbm = pl.BlockSpec(memory_space=pltpu.HBM)
    return pl.pallas_call(
        body, name=name, in_specs=[hbm] * n + [pl.BlockSpec(memory_space=pl.ANY)] * len(extra), out_specs=[hbm] * n,
        out_shape=[jax.ShapeDtypeStruct(((N_DEV,) + a.shape) if gather else a.shape, a.dtype) for a in arrays],
        scratch_shapes=[pltpu.SemaphoreType.DMA((n * (N_DEV - 1),)), pltpu.SemaphoreType.DMA((n * (N_DEV - 1),)),
                        pltpu.SemaphoreType.DMA((n,))],
    )(*arrays, *extra)


_HBM = pl.BlockSpec(memory_space=pltpu.HBM)
_SEM = pl.BlockSpec(memory_space=pltpu.SEMAPHORE)
N_PEERS = N_DEV - 1


def _mesh_pos():
    pos = [lax.axis_index(ax) for ax in MESH_AXES]
    return pos, 4 * pos[0] + 2 * pos[1] + pos[2]


ALL_PEERS = (1, 2, 3, 4, 5, 6, 7)
NEAR_PEERS = (1, 2, 4, 6)
RELAYED = (2, 4, 6)


def _peers(pos, masks=ALL_PEERS):
    out = []
    for k in masks:
        bits = ((k >> 2) & 1, (k >> 1) & 1, k & 1)
        peer = tuple(1 - p if b else p for p, b in zip(pos, bits))
        out.append((peer, 4 * peer[0] + 2 * peer[1] + peer[2]))
    return out


def _part(x_ref, lead, gather, slot):
    ref = x_ref if lead is None else x_ref.at[lead]
    return ref if gather else ref.at[slot]


OWN_BLOCK_BYTES = 2 * 1024 * 1024


def _landing_zone(a, lead, me, name):
    r, c = a.shape[-2:]
    tr = r
    while tr * _round_up(c, LANES) * a.dtype.itemsize > OWN_BLOCK_BYTES and tr % 32 == 0:
        tr //= 2

    def body(me_ref, x_ref, o_ref):
        o_ref[...] = x_ref[...]

    x_spec = (pl.BlockSpec((tr, c), lambda i, me_ref: (i, 0)) if lead is None
              else pl.BlockSpec((None, tr, c), lambda i, me_ref: (lead, i, 0)))
    grid_spec = pltpu.PrefetchScalarGridSpec(
        num_scalar_prefetch=1, grid=(r // tr,), in_specs=[x_spec],
        out_specs=pl.BlockSpec((None, tr, c), lambda i, me_ref: (me_ref[0], i, 0)))
    return pl.pallas_call(body, name=name, grid_spec=grid_spec, out_shape=jax.ShapeDtypeStruct((N_DEV, r, c), a.dtype),
                          compiler_params=_cparams(1))(jnp.reshape(me, (1,)).astype(jnp.int32), a)


def _split_copies(items, gather, masks, x_refs, land_refs, send_sems, recv_sems):
    pos, me = _mesh_pos()
    copies = []
    for i, (_, lead) in enumerate(items):
        for k, (peer, peer_id) in enumerate(_peers(pos, masks)):
            copies.append(pltpu.make_async_remote_copy(
                src_ref=_part(x_refs[i], lead, gather, peer_id), dst_ref=land_refs[i].at[me],
                send_sem=send_sems.at[i * len(masks) + k], recv_sem=recv_sems.at[i * len(masks) + k],
                device_id=peer, device_id_type=pl.DeviceIdType.MESH))
    return copies


_SPLIT_CALL = dict(compiler_params=pltpu.CompilerParams(has_side_effects=pltpu.SideEffectType.DATAFLOW_SIDE_EFFECTING))


def exchange_start(items, gather, lands, name, masks=ALL_PEERS):
    n = len(items)
    lands = list(lands)
    xs = [a for a, _ in items]

    def body(*refs):
        x_refs, land_refs = refs[:n], refs[n:2 * n]
        send_sems, recv_sems, token = refs[2 * n], refs[2 * n + 1], refs[-1]
        for cp in _split_copies(items, gather, masks, x_refs, land_refs, send_sems, recv_sems):
            cp.start()
        token[...] = jnp.zeros_like(token)

    outs = pl.pallas_call(
        body, name=name,
        out_shape=(pltpu.SemaphoreType.DMA((n * len(masks),)), pltpu.SemaphoreType.DMA((n * len(masks),)),
                   *[pltpu.HBM(v.shape, v.dtype) for v in xs + lands], jax.ShapeDtypeStruct((SUBLANES, LANES), F32)),
        in_specs=[_HBM] * (2 * n),
        out_specs=(_SEM, _SEM, *[_HBM] * (2 * n), pl.BlockSpec(memory_space=pltpu.VMEM)),
        input_output_aliases={i: 2 + i for i in range(2 * n)}, **_SPLIT_CALL,
    )(*[pltpu.with_memory_space_constraint(v, pltpu.HBM) for v in xs + lands])
    handle = (items, gather, masks, outs[0], outs[1], outs[2:2 + n], outs[2 + n:2 + 2 * n])
    return handle, outs[-1]


def _relay_copies(n, land_refs, send_sems, recv_sems):
    pos, _ = _mesh_pos()
    sibling = (pos[0], pos[1], 1 - pos[2])
    copies = []
    for i in range(n):
        for k, (_, peer_id) in enumerate(_peers(pos, RELAYED)):
            slot = land_refs[i].at[peer_id]
            copies.append(pltpu.make_async_remote_copy(
                src_ref=slot, dst_ref=slot, send_sem=send_sems.at[i * len(RELAYED) + k],
                recv_sem=recv_sems.at[i * len(RELAYED) + k], device_id=sibling, device_id_type=pl.DeviceIdType.MESH))
    return copies


def relay(lands, name, work=None):
    n = len(lands)
    sems = n * len(RELAYED)

    def start(*refs):
        for cp in _relay_copies(n, refs[:n], refs[n], refs[n + 1]):
            cp.start()
        refs[-1][...] = jnp.zeros_like(refs[-1])

    outs = pl.pallas_call(
        start, name=name + "_start",
        out_shape=(pltpu.SemaphoreType.DMA((sems,)), pltpu.SemaphoreType.DMA((sems,)),
                   *[pltpu.HBM(v.shape, v.dtype) for v in lands], jax.ShapeDtypeStruct((SUBLANES, LANES), F32)),
        in_specs=[_HBM] * n, out_specs=(_SEM, _SEM, *[_HBM] * n, pl.BlockSpec(memory_space=pltpu.VMEM)),
        input_output_aliases={i: 2 + i for i in range(n)},
        **_SPLIT_CALL)(*[pltpu.with_memory_space_constraint(v, pltpu.HBM) for v in lands])

    def wait(*refs):
        for cp in _relay_copies(n, refs[:n], refs[n], refs[n + 1]):
            cp.wait_send()
            cp.wait_recv()

    done = None if work is None else work(outs[-1])
    extra = [] if work is None else [jax.tree.leaves(done)[0]]
    filled = list(pl.pallas_call(
        wait, name=name + "_wait", out_shape=tuple(pltpu.HBM(v.shape, v.dtype) for v in lands),
        in_specs=[_HBM] * n + [_SEM, _SEM] + [pl.BlockSpec(memory_space=pl.ANY)] * len(extra),
        out_specs=tuple([_HBM] * n), input_output_aliases={i: i for i in range(n)},
        **_SPLIT_CALL)(*outs[2:2 + n], outs[0], outs[1], *extra))
    return filled if work is None else (filled, done)


def exchange_wait(handle, after, name):
    items, gather, masks, send_sems, recv_sems, x_thru, land_thru = handle
    n = len(items)

    def body(*refs):
        x_refs, land_refs = refs[:n], refs[n:2 * n]
        for cp in _split_copies(items, gather, masks, x_refs, land_refs, refs[2 * n], refs[2 * n + 1]):
            cp.wait_send()
            cp.wait_recv()

    outs = pl.pallas_call(
        body, name=name, out_shape=tuple(pltpu.HBM(v.shape, v.dtype) for v in tuple(x_thru) + tuple(land_thru)),
        in_specs=[_HBM] * (2 * n) + [_SEM, _SEM, pl.BlockSpec(memory_space=pl.ANY)], out_specs=tuple([_HBM] * (2 * n)),
        input_output_aliases={i: i for i in range(2 * n)},
        compiler_params=pltpu.CompilerParams(has_side_effects=pltpu.SideEffectType.DATAFLOW_SIDE_EFFECTING),
    )(*x_thru, *land_thru, send_sems, recv_sems, after)
    return list(outs[:n]), list(outs[n:])


def _adam_update(w, g, m, v):
    nm = ADAM_B1 * m + (1.0 - ADAM_B1) * g
    nv = ADAM_B2 * v + (1.0 - ADAM_B2) * (g * g)
    m_hat = nm / (1.0 - ADAM_B1 ** ADAM_STEP)
    v_hat = nv / (1.0 - ADAM_B2 ** ADAM_STEP)
    return -ADAM_LR * (m_hat / (jnp.sqrt(v_hat) + ADAM_EPS) + ADAM_WD * w), nm, nv


def _sum_slots(s_ref):
    acc = s_ref[0].astype(F32)
    for j in range(1, N_DEV):
        acc = acc + s_ref[j].astype(F32)
    return acc


ADAM_BLOCK_BYTES = 10 * 1024 * 1024


def adamw_sharded(recvs, sends, me, w, m, v, name, transposed=False):
    nl = w.shape[0]
    r, c = recvs[0].shape[1:]
    assert w.shape[1:] == ((c, r) if transposed else (r, c))
    assert len(recvs) == nl and len(sends) == nl
    per_row = _round_up(c, LANES) * (nl * (N_DEV + 1) * recvs[0].dtype.itemsize + 7 * 4)
    tr = r
    while tr * per_row > ADAM_BLOCK_BYTES and tr % 16 == 0:
        tr //= 2

    def body(me_ref, *refs):
        s_refs, o_refs = refs[:nl], refs[nl:2 * nl]
        w_ref, m_ref, v_ref, g_ref, d_ref, nm_ref, nv_ref = refs[2 * nl:]
        mine = me_ref[0]

        def total(l):
            acc = jnp.where(mine == 0, o_refs[l][...], s_refs[l][0]).astype(F32)
            for j in range(1, N_DEV):
                acc = acc + jnp.where(mine == j, o_refs[l][...], s_refs[l][j]).astype(F32)
            return acc

        g = total(0)
        for l in range(1, nl):
            g = jnp.where(pl.program_id(0) == l, total(l), g)
        if transposed:
            g = g.T
        g_ref[...] = g
        d_ref[...], nm_ref[...], nv_ref[...] = _adam_update(w_ref[...], g, m_ref[...], v_ref[...])

    spec = (pl.BlockSpec((None, c, tr), lambda l, i, me_ref: (l, 0, i)) if transposed
            else pl.BlockSpec((None, tr, c), lambda l, i, me_ref: (l, i, 0)))
    shp = jax.ShapeDtypeStruct(w.shape, F32)
    grid_spec = pltpu.PrefetchScalarGridSpec(
        num_scalar_prefetch=1, grid=(nl, r // tr),
        in_specs=([pl.BlockSpec((N_DEV, tr, c), lambda l, i, me_ref: (0, i, 0))] * nl
                  + [pl.BlockSpec((None, tr, c), lambda l, i, me_ref: (me_ref[0], i, 0))] * nl + [spec, spec, spec]),
        out_specs=[spec] * 4)
    return pl.pallas_call(body, name=name, grid_spec=grid_spec, out_shape=[shp] * 4, compiler_params=_cparams(2))(
        jnp.reshape(me, (1,)).astype(jnp.int32), *recvs, *sends, w, m, v)


def reduce_slots(slots, name):
    _, r, lanes = slots.shape
    tr = _tile(r, 2048)

    def body(s_ref, o_ref):
        o_ref[...] = _sum_slots(s_ref)

    return pl.pallas_call(
        body, name=name, grid=(r // tr,),
        in_specs=[pl.BlockSpec((N_DEV, tr, lanes), lambda i: (0, i, 0))],
        out_specs=pl.BlockSpec((tr, lanes), lambda i: (i, 0)),
        out_shape=jax.ShapeDtypeStruct((r, lanes), F32), compiler_params=_cparams(1),
    )(slots)


def adamw(w, g, m, v, name):
    r, lanes = w.shape
    tr = _tile(r, 2048)

    def body(w_ref, g_ref, m_ref, v_ref, d_ref, nm_ref, nv_ref):
        d_ref[...], nm_ref[...], nv_ref[...] = _adam_update(w_ref[...], g_ref[...], m_ref[...], v_ref[...])

    spec = pl.BlockSpec((tr, lanes), lambda i: (i, 0))
    shp = jax.ShapeDtypeStruct((r, lanes), F32)
    return pl.pallas_call(body, name=name, grid=(r // tr,), in_specs=[spec] * 4, out_specs=[spec] * 3,
                          out_shape=[shp] * 3, compiler_params=_cparams(1))(w, g, m, v)


def cols_from_slots(slots, place, width, name):
    _, rows, c = slots.shape
    tr = _tile(rows, 256)

    def body(s_ref, o_ref):
        o_ref[...] = place(jnp.concatenate([s_ref[j] for j in range(N_DEV)], axis=1))

    return pl.pallas_call(
        body, name=name, grid=(rows // tr,),
        in_specs=[pl.BlockSpec((N_DEV, tr, c), lambda i: (0, i, 0))],
        out_specs=pl.BlockSpec((tr, width), lambda i: (i, 0)),
        out_shape=jax.ShapeDtypeStruct((rows, width), slots.dtype), compiler_params=_cparams(1))(slots)


def slots_from_cols(full, pick, c, name):
    rows, wide = full.shape
    tr = _tile(rows, 256)

    def body(x_ref, o_ref):
        v = pick(x_ref[...])
        for j in range(N_DEV):
            o_ref[j] = v[:, j * c:(j + 1) * c]

    return pl.pallas_call(
        body, name=name, grid=(rows // tr,),
        in_specs=[pl.BlockSpec((tr, wide), lambda i: (i, 0))],
        out_specs=pl.BlockSpec((N_DEV, tr, c), lambda i: (0, i, 0)),
        out_shape=jax.ShapeDtypeStruct((N_DEV, rows, c), full.dtype), compiler_params=_cparams(1))(full)


def _token_tile(cfg):
    return min(cfg.S, 256)


def norm_fwd(cfg, x, w, name):
    ts = _token_tile(cfg)
    d = x.shape[1]

    def fn(ins, _):
        xv, wv = ins
        return [rms(xv, wv)], [], []

    return seq_call(name, fn, (1, cfg.S // ts),
                    [(x, (ts, d), lambda g, t: (t, 0)), (w, (1, d), lambda g, t: (0, 0))],
                    [((cfg.S, d), BF16, (ts, d), lambda g, t: (t, 0))])[0]


def norm_bwd(cfg, x, w, dh, dres, name):
    ts = _token_tile(cfg)
    d = x.shape[1]

    def fn(ins, _):
        xv, wv, dhv, drv = ins
        _, vjp = jax.vjp(rms, xv, wv)
        dx, dw = vjp(dhv.astype(F32))
        return [dx + drv], [dw], []

    row = lambda g, t: (t, 0)
    out = seq_call(name, fn, (1, cfg.S // ts),
                   [(x, (ts, d), row), (w, (1, d), lambda g, t: (0, 0)), (dh, (ts, d), row), (dres, (ts, d), row)],
                   [((cfg.S, d), F32, (ts, d), row)],
                   accs=[((1, d), (1, d), lambda g, t: (0, 0))])
    return out[0], out[1]


def head_fwd_bwd(cfg, x, w, target, name):
    ts = _token_tile(cfg)
    d = x.shape[1]

    def fn(ins, _):
        xv, wv, tv = ins
        y, vjp = jax.vjp(rms, xv, wv)
        err = y - tv
        loss = 0.5 * jnp.sum(err * err) / d
        dx, dw = vjp(err / d)
        return [dx], [jnp.full((SUBLANES, LANES), loss, F32), dw], []

    row = lambda g, t: (t, 0)
    fixed = lambda g, t: (0, 0)
    dx, loss, dw = seq_call(name, fn, (1, cfg.S // ts),
                            [(x, (ts, d), row), (w, (1, d), fixed), (target, (ts, d), row)],
                            [((cfg.S, d), F32, (ts, d), row)],
                            accs=[((SUBLANES, LANES), (SUBLANES, LANES), fixed), ((1, d), (1, d), fixed)])
    return loss, dx, dw


FFN_ROWS = 1024


def ffn_gate_up(h, w_gate, w_up, name):
    s, d = h.shape
    c = w_gate.shape[1]
    tm = _tile(s, FFN_ROWS)

    def body(h_ref, wg_ref, wu_ref, g_ref, u_ref, a_ref):
        hv = h_ref[...]
        g = _dg(hv, wg_ref[...], 1, 1)
        u = _dg(hv, wu_ref[...], 1, 1)
        g_ref[...] = g
        u_ref[...] = u
        a_ref[...] = (silu(g) * u).astype(a_ref.dtype)

    w_spec = pl.BlockSpec((None, c, d), lambda i, j: (j, 0, 0))
    o_spec = pl.BlockSpec((None, tm, c), lambda i, j: (j, i, 0))
    shp = (N_DEV, s, c)
    return pl.pallas_call(
        body, name=name, grid=(s // tm, N_DEV), in_specs=[pl.BlockSpec((tm, d), lambda i, j: (i, 0)), w_spec, w_spec],
        out_specs=[o_spec] * 3,
        out_shape=[jax.ShapeDtypeStruct(shp, F32), jax.ShapeDtypeStruct(shp, F32), jax.ShapeDtypeStruct(shp, BF16)],
        compiler_params=_cparams(2))(h, w_gate, w_up)


def ffn_dgate_dup(dx, w_down, gate, up, after, name):
    s, d = dx.shape
    c = w_down.shape[1]
    tm = _tile(s, FFN_ROWS)
    extra = [] if after is None else [after]

    def body(dx_ref, wd_ref, g_ref, u_ref, *rest):
        dg_ref, du_ref = rest[len(extra):]
        dact = _dg(dx_ref[...], wd_ref[...], 1, 1)
        _, vjp = jax.vjp(lambda a, b: silu(a) * b, g_ref[...], u_ref[...])
        dg, du = vjp(dact)
        dg_ref[...] = dg.astype(dg_ref.dtype)
        du_ref[...] = du.astype(du_ref.dtype)

    blk = pl.BlockSpec((None, tm, c), lambda i, j: (j, i, 0))
    shp = jax.ShapeDtypeStruct((N_DEV, s, c), BF16)
    return pl.pallas_call(
        body, name=name, grid=(s // tm, N_DEV),
        in_specs=[pl.BlockSpec((tm, d), lambda i, j: (i, 0)), pl.BlockSpec((None, c, d), lambda i, j: (j, 0, 0)), blk, blk]
        + [pl.BlockSpec(memory_space=pl.ANY)] * len(extra),
        out_specs=[blk, blk], out_shape=[shp, shp], compiler_params=_cparams(2))(dx, w_down, gate, up, *extra)


CONV_COLS = (512, 256)
HALO = SUBLANES


def _conv_cols(col0, width):
    return next(c for c in CONV_COLS if col0 % c == 0 and width % c == 0)


def _shift_down(x, halo, j):
    if j == 0:
        return x
    r8 = _rows_iota(halo.shape)
    top = jnp.where(r8 >= j, pltpu.roll(x[:HALO], j, 0), pltpu.roll(halo, j, 0))
    return jnp.concatenate([top, pltpu.roll(x, j, 0)[HALO:]], axis=0)


def _shift_up(x, halo, j):
    if j == 0:
        return x
    n = x.shape[0]
    r8 = _rows_iota(halo.shape)
    bot = jnp.where(r8 < HALO - j, pltpu.roll(x[n - HALO:], HALO - j, 0), pltpu.roll(halo, HALO - j, 0))
    return jnp.concatenate([pltpu.roll(x, n - j, 0)[:n - HALO], bot], axis=0)


def _conv_tile(cfg):
    return min(cfg.S, 1024)


def conv_fwd(cfg, src, col0, width, w, b, name):
    tt, cb = _conv_tile(cfg), _conv_cols(col0, width)
    c0, hb = col0 // cb, tt // HALO
    nt = cfg.S // tt

    def body(x_ref, h_ref, w_ref, b_ref, o_ref):
        t = pl.program_id(1)
        x = x_ref[...]
        halo = jnp.where(t > 0, h_ref[...], 0.0)
        wv = w_ref[...]
        acc = b_ref[...] + wv[CONV_WIDTH - 1:CONV_WIDTH] * x
        for j in range(1, CONV_WIDTH):
            acc = acc + wv[CONV_WIDTH - 1 - j:CONV_WIDTH - j] * _shift_down(x, halo, j)
        o_ref[...] = acc

    return pl.pallas_call(
        body, name=name, grid=(width // cb, nt),
        in_specs=[pl.BlockSpec((tt, cb), lambda c, t: (t, c0 + c)),
                  pl.BlockSpec((HALO, cb), lambda c, t: (jnp.maximum(t * hb - 1, 0), c0 + c)),
                  pl.BlockSpec((CONV_WIDTH, cb), lambda c, t: (0, c)),
                  pl.BlockSpec((1, cb), lambda c, t: (0, c))],
        out_specs=pl.BlockSpec((tt, cb), lambda c, t: (t, c)),
        out_shape=jax.ShapeDtypeStruct((cfg.S, width), F32), compiler_params=_cparams(2),
    )(src, src, w, b)


def conv_bwd(cfg, src, col0, width, w, dy, name, into=None, into_col0=0):
    tt, cb = _conv_tile(cfg), _conv_cols(col0, width)
    c0, hb = col0 // cb, tt // HALO
    nt = cfg.S // tt
    extra = [] if into is None else [into]
    assert into_col0 % cb == 0
    o0 = into_col0 // cb

    def body(x_ref, h_ref, w_ref, dy_ref, dh_ref, *rest):
        dx_ref, dw_ref, db_ref = rest[len(extra):]
        t = pl.program_id(1)

        @pl.when(t == 0)
        def _():
            dw_ref[...] = jnp.zeros_like(dw_ref)
            db_ref[...] = jnp.zeros_like(db_ref)

        x = x_ref[...]
        halo = jnp.where(t > 0, h_ref[...], 0.0)
        dy = dy_ref[...]
        dhalo = jnp.where(t < nt - 1, dh_ref[...], 0.0)
        wv = w_ref[...]
        dx = wv[CONV_WIDTH - 1:CONV_WIDTH] * dy
        rows = [jnp.sum(dy * x, axis=0, keepdims=True)]
        for j in range(1, CONV_WIDTH):
            dx = dx + wv[CONV_WIDTH - 1 - j:CONV_WIDTH - j] * _shift_up(dy, dhalo, j)
            rows.insert(0, jnp.sum(dy * _shift_down(x, halo, j), axis=0, keepdims=True))
        dx_ref[...] = dx.astype(dx_ref.dtype)
        dw_ref[...] += jnp.concatenate(rows, axis=0)
        db_ref[...] += jnp.sum(dy, axis=0, keepdims=True)

    return pl.pallas_call(
        body, name=name, grid=(width // cb, nt),
        in_specs=[pl.BlockSpec((tt, cb), lambda c, t: (t, c0 + c)),
                  pl.BlockSpec((HALO, cb), lambda c, t: (jnp.maximum(t * hb - 1, 0), c0 + c)),
                  pl.BlockSpec((CONV_WIDTH, cb), lambda c, t: (0, c)),
                  pl.BlockSpec((tt, cb), lambda c, t: (t, c)),
                  pl.BlockSpec((HALO, cb), lambda c, t: (jnp.minimum((t + 1) * hb, nt * hb - 1), c))]
        + [pl.BlockSpec(memory_space=pl.ANY)] * len(extra),
        out_specs=[pl.BlockSpec((tt, cb), lambda c, t: (t, o0 + c)),
                   pl.BlockSpec((CONV_WIDTH, cb), lambda c, t: (0, c)),
                   pl.BlockSpec((1, cb), lambda c, t: (0, c))],
        out_shape=[jax.ShapeDtypeStruct((cfg.S, width) if into is None else into.shape, BF16),
                   jax.ShapeDtypeStruct((CONV_WIDTH, width), F32), jax.ShapeDtypeStruct((1, width), F32)],
        input_output_aliases={5: 0} if extra else {}, compiler_params=_cparams(2),
    )(src, src, w, dy, dy, *extra)


def write_cols(cfg, into, piece, col0, name):
    tt, width = _conv_tile(cfg), piece.shape[1]
    assert col0 % width == 0 and into.dtype == piece.dtype

    def body(p_ref, _, o_ref):
        o_ref[...] = p_ref[...]

    return pl.pallas_call(
        body, name=name, grid=(cfg.S // tt,),
        in_specs=[pl.BlockSpec((tt, width), lambda t: (t, 0)), pl.BlockSpec(memory_space=pl.ANY)],
        out_specs=pl.BlockSpec((tt, width), lambda t: (t, col0 // width)),
        out_shape=jax.ShapeDtypeStruct(into.shape, into.dtype), input_output_aliases={1: 0},
        compiler_params=_cparams(1))(piece, into)


def _gla_core(gh, q, k, v, g, glr, wg, bg, wn, st):
    n = glr.shape[0]
    causal = _lower_tri(n)
    outs, new = [], []
    for h in range(gh):
        log_a = log_sigmoid(bdot(glr, wg[h], 1, 0) + bg[h]) * (1.0 / GATE_NORM)
        bcum = cumsum_rows(log_a)
        b_last, b_mid = _row(bcum, n - 1), _row(bcum, n // 2)
        qs = q[h] * (HEAD_K ** -0.5)
        scores = jnp.where(causal, bdot(qs * jnp.exp(bcum - b_mid), k[h] * jnp.exp(b_mid - bcum), 1, 1), 0.0)
        o = bdot(scores, v[h], 1, 0) + bdot(qs * jnp.exp(bcum), st[h], 1, 1)
        new.append(st[h] * jnp.exp(b_last) + bdot(v[h], k[h] * jnp.exp(b_last - bcum), 0, 0))
        outs.append(rms(o, wn) * silu(g[h]))
    return jnp.concatenate(outs, axis=1), new


def _gla_ins(cfg, p0, wg, bg, wn, tmap):
    gh = cfg.GH
    ins = []
    for h in range(gh):
        ins.append((p0, (CHUNK, HEAD_K), lambda g, t, h=h: (tmap(t), h)))
    for h in range(gh):
        ins.append((p0, (CHUNK, HEAD_K), lambda g, t, h=h: (tmap(t), gh + h)))
    for h in range(gh):
        ins.append((p0, (CHUNK, HEAD_V), lambda g, t, h=h: (tmap(t), gh + h)))
    for h in range(gh):
        ins.append((p0, (CHUNK, HEAD_V), lambda g, t, h=h: (tmap(t), 2 * gh + h)))
    ins.append((p0, (CHUNK, LANES), lambda g, t: (tmap(t), 10 * gh)))
    for h in range(gh):
        ins.append((wg, (LANES, HEAD_K), lambda g, t, h=h: (0, h)))
    for h in range(gh):
        ins.append((bg, (1, HEAD_K), lambda g, t, h=h: (0, h)))
    ins.append((wn, (1, HEAD_V), lambda g, t: (0, 0)))
    return ins


def _gla_unpack(gh, vals):
    q, k, v, g = (vals[i * gh:(i + 1) * gh] for i in range(4))
    glr = vals[4 * gh]
    wg = vals[4 * gh + 1:5 * gh + 1]
    bg = vals[5 * gh + 1:6 * gh + 1]
    wn = vals[6 * gh + 1]
    return q, k, v, g, glr, wg, bg, wn, vals[6 * gh + 2:]


def gla_fwd(cfg, p0, wg, bg, wn, name):
    gh, nc = cfg.GH, cfg.S // CHUNK

    def fn(ins, st):
        q, k, v, g, glr, wgv, bgv, wnv, _ = _gla_unpack(gh, ins)
        out, new = _gla_core(gh, q, k, v, g, glr, wgv, bgv, wnv, st)
        return [out, jnp.stack(st)], [], new

    return seq_call(name, fn, (1, nc), _gla_ins(cfg, p0, wg, bg, wn, lambda t: t),
                    [((cfg.S, cfg.DV), BF16, (CHUNK, cfg.DV), lambda g, t: (t, 0)),
                     ((nc, gh, HEAD_V, HEAD_K), F32, (None, gh, HEAD_V, HEAD_K), lambda g, t: (t, 0, 0, 0))],
                    carries=[(HEAD_V, HEAD_K)] * gh)


def gla_bwd(cfg, p0, wg, bg, wn, states, dout, name):
    gh, nc = cfg.GH, cfg.S // CHUNK
    rev = lambda t: nc - 1 - t

    def fn(ins, dst):
        q, k, v, g, glr, wgv, bgv, wnv, rest = _gla_unpack(gh, ins)
        st_all, do = rest
        st = [st_all[h] for h in range(gh)]
        _, vjp = jax.vjp(functools.partial(_gla_core, gh), q, k, v, g, glr, wgv, bgv, wnv, st)
        dq, dk, dv, dg, dglr, dwg, dbg, dwn, dstate = vjp((do.astype(F32), list(dst)))
        return ([jnp.concatenate(list(dq) + list(dk) + list(dv) + list(dg), axis=1), dglr],
                [jnp.concatenate(dwg, axis=1), jnp.concatenate(dbg, axis=1), dwn], dstate)

    ins = _gla_ins(cfg, p0, wg, bg, wn, rev)
    ins.append((states, (None, gh, HEAD_V, HEAD_K), lambda g, t: (rev(t), 0, 0, 0)))
    ins.append((dout, (CHUNK, cfg.DV), lambda g, t: (rev(t), 0)))
    wide = 2 * cfg.DK + 2 * cfg.DV
    fixed = lambda g, t: (0, 0)
    return seq_call(name, fn, (1, nc), ins,
                    [((cfg.S, wide), BF16, (CHUNK, wide), lambda g, t: (rev(t), 0)),
                     ((cfg.S, LANES), BF16, (CHUNK, LANES), lambda g, t: (rev(t), 0))],
                    accs=[((LANES, cfg.DK), (LANES, cfg.DK), fixed), ((1, cfg.DK), (1, cfg.DK), fixed),
                          ((1, HEAD_V), (1, HEAD_V), fixed)],
                    carries=[(HEAD_V, HEAD_K)] * gh)


def _lru_core(xc, gate, wa, wi, ba, bi, lam, h_in):
    r = jax.nn.sigmoid(bdot(xc, wa, 1, 0) + ba)
    i = jax.nn.sigmoid(bdot(xc, wi, 1, 0) + bi)
    log_a = LRU_C * r * log_sigmoid(lam)
    a = jnp.exp(log_a)
    u = jnp.sqrt(-_expm1(2.0 * log_a)) * (i * xc)
    first = _rows_iota(a.shape) == 0
    h = lin_scan(a, u + jnp.where(first, a * h_in, 0.0))
    return h * gelu_tanh(gate), _row(h, a.shape[0] - 1)


def _lru_tile(cfg):
    return min(cfg.S, 512)


def _lru_ins(cfg, xc, p0, wa, wi, ba, bi, lam, tmap):
    tt, gh = _lru_tile(cfg), cfg.GH
    vec = lambda g, t: (0, g)
    return [(xc, (tt, LRU_BLOCK), lambda g, t: (tmap(t), g)),
            (p0, (tt, LRU_BLOCK), lambda g, t: (tmap(t), 8 * gh + g)),
            (wa, (None, LRU_BLOCK, LRU_BLOCK), lambda g, t: (g, 0, 0)),
            (wi, (None, LRU_BLOCK, LRU_BLOCK), lambda g, t: (g, 0, 0)),
            (ba, (1, LRU_BLOCK), vec), (bi, (1, LRU_BLOCK), vec), (lam, (1, LRU_BLOCK), vec)]


def lru_fwd(cfg, xc, p0, wa, wi, ba, bi, lam, name):
    tt, nb = _lru_tile(cfg), cfg.NB
    nt = cfg.S // tt

    def fn(ins, c):
        out, h_last = _lru_core(*ins, c[0])
        return [out, c[0]], [], [h_last]

    return seq_call(name, fn, (nb, nt), _lru_ins(cfg, xc, p0, wa, wi, ba, bi, lam, lambda t: t),
                    [((cfg.S, cfg.W), BF16, (tt, LRU_BLOCK), lambda g, t: (t, g)),
                     ((nb, nt, 1, LRU_BLOCK), F32, (None, None, 1, LRU_BLOCK), lambda g, t: (g, t, 0, 0))],
                    carries=[(1, LRU_BLOCK)])


def lru_bwd(cfg, xc, p0, wa, wi, ba, bi, lam, states, dout, name):
    tt, nb = _lru_tile(cfg), cfg.NB
    nt = cfg.S // tt
    rev = lambda t: nt - 1 - t

    def fn(ins, c):
        *fwd_ins, h_in, do = ins
        _, vjp = jax.vjp(_lru_core, *fwd_ins, h_in)
        dxc, dgate, dwa, dwi, dba, dbi, dlam, dh = vjp((do.astype(F32), c[0]))
        return [dxc, dgate], [dwa, dwi, dba, dbi, dlam], [dh]

    ins = _lru_ins(cfg, xc, p0, wa, wi, ba, bi, lam, rev)
    ins.append((states, (None, None, 1, LRU_BLOCK), lambda g, t: (g, rev(t), 0, 0)))
    ins.append((dout, (tt, LRU_BLOCK), lambda g, t: (rev(t), g)))
    mat = ((nb, LRU_BLOCK, LRU_BLOCK), (None, LRU_BLOCK, LRU_BLOCK), lambda g, t: (g, 0, 0))
    vec = ((1, cfg.W), (1, LRU_BLOCK), lambda g, t: (0, g))
    return seq_call(name, fn, (nb, nt), ins,
                    [((cfg.S, cfg.W), F32, (tt, LRU_BLOCK), lambda g, t: (rev(t), g)),
                     ((cfg.S, cfg.W), BF16, (tt, LRU_BLOCK), lambda g, t: (rev(t), g))],
                    accs=[mat, mat, vec, vec, vec], carries=[(1, LRU_BLOCK)])


def _ssd_core(xc, bc, cc, z, dt_raw, dt_bias, a_log, d_skip, gn, st):
    n = xc.shape[0]
    x, bm, cm = silu(xc), silu(bc), silu(cc)
    dt = softplus(dt_raw + dt_bias)
    acs = cumsum_rows_mxu(dt * (-jnp.exp(a_log)))
    acs_t = acs.T
    acs_e, dt_e = _expand_heads(acs), _expand_heads(dt)
    last_e = _expand_heads(_row(acs, n - 1))
    causal = _lower_tri(n)
    cb = bdot(cm, bm, 1, 1)
    xdt = x * dt_e
    y_diag = []
    for h, xh in enumerate(_split_heads(xdt)):
        seg = acs[:, h:h + 1] - acs_t[h:h + 1, :]
        decay = jnp.where(causal, jnp.exp(jnp.minimum(seg, 0.0)), 0.0)
        y_diag.append(bdot(cb * decay, xh, 1, 0))
    y = jnp.concatenate(y_diag, axis=1) + bdot(cm, st, 1, 0) * jnp.exp(acs_e)
    new = st * jnp.exp(last_e) + bdot(bm, xdt * jnp.exp(last_e - acs_e), 0, 0)
    y = (y + _expand_heads(d_skip) * x) * silu(z)
    return rms(y, gn), new


SSD_TILED = 5


SSD_FWD_CHUNKS = 4
SSD_BWD_CHUNKS = 1


def _ssd_tile(cfg, chunks):
    return min(cfg.S, chunks * CHUNK)


def _chunk_rows(v, s):
    return v[s * CHUNK:(s + 1) * CHUNK]


def _ssd_ins(cfg, tt, xc, p1, dt_raw, dt_bias, a_log, d_skip, gn, tmap):
    ng = cfg.NG
    vec = lambda g, t: (g, 0, 0)
    return [(xc, (tt, SSD_GW), lambda g, t: (tmap(t), g)),
            (xc, (tt, SSD_N), lambda g, t: (tmap(t), 4 * ng + g)),
            (xc, (tt, SSD_N), lambda g, t: (tmap(t), 5 * ng + g)),
            (p1, (tt, SSD_GW), lambda g, t: (tmap(t), g)),
            (dt_raw, (None, tt, LANES), lambda g, t: (g, tmap(t), 0)),
            (dt_bias, (None, 1, LANES), vec), (a_log, (None, 1, LANES), vec), (d_skip, (None, 1, LANES), vec),
            (gn, (1, SSD_GW), lambda g, t: (0, g))]


def ssd_fwd(cfg, xc, p1, dt_raw, dt_bias, a_log, d_skip, gn, name):
    ng, nc, tt = cfg.NG, cfg.S // CHUNK, _ssd_tile(cfg, SSD_FWD_CHUNKS)
    nsub = tt // CHUNK

    def fn(ins, c):
        tiled, params = ins[:SSD_TILED], ins[SSD_TILED:]
        st, outs, entered = c[0], [], []
        for s in range(nsub):
            entered.append(st)
            out, st = _ssd_core(*[_chunk_rows(v, s) for v in tiled], *params, st)
            outs.append(out)
        return [jnp.concatenate(outs, axis=0), jnp.stack(entered)], [], [st]

    return seq_call(name, fn, (ng, cfg.S // tt), _ssd_ins(cfg, tt, xc, p1, dt_raw, dt_bias, a_log, d_skip, gn, lambda t: t),
                    [((cfg.S, cfg.DI), BF16, (tt, SSD_GW), lambda g, t: (t, g)),
                     ((ng, nc, SSD_N, SSD_GW), F32, (None, nsub, SSD_N, SSD_GW), lambda g, t: (g, t, 0, 0))],
                    carries=[(SSD_N, SSD_GW)])


def ssd_bwd(cfg, xc, p1, dt_raw, dt_bias, a_log, d_skip, gn, states, dout, name):
    ng, tt = cfg.NG, _ssd_tile(cfg, SSD_BWD_CHUNKS)
    nsub, nt = tt // CHUNK, cfg.S // tt
    rev = lambda t: nt - 1 - t

    def fn(ins, c):
        tiled, params = ins[:SSD_TILED], ins[SSD_TILED:SSD_TILED + 4]
        st_all, do = ins[SSD_TILED + 4:]
        dst, pieces, acc = c[0], [None] * nsub, None
        for s in reversed(range(nsub)):
            _, vjp = jax.vjp(_ssd_core, *[_chunk_rows(v, s) for v in tiled], *params, st_all[s])
            grads = vjp((_chunk_rows(do, s).astype(F32), dst))
            pieces[s], dparams, dst = grads[:SSD_TILED], grads[SSD_TILED:SSD_TILED + 4], grads[SSD_TILED + 4]
            acc = dparams if acc is None else [x + y for x, y in zip(acc, dparams)]
        return [jnp.concatenate([p[i] for p in pieces], axis=0) for i in range(SSD_TILED)], list(acc), [dst]

    ins = _ssd_ins(cfg, tt, xc, p1, dt_raw, dt_bias, a_log, d_skip, gn, rev)
    ins.append((states, (None, nsub, SSD_N, SSD_GW), lambda g, t: (g, rev(t), 0, 0)))
    ins.append((dout, (tt, SSD_GW), lambda g, t: (rev(t), g)))
    col = lambda g, t: (rev(t), g)
    vec = ((ng, 1, LANES), (None, 1, LANES), lambda g, t: (g, 0, 0))
    return seq_call(name, fn, (ng, nt), ins,
                    [((cfg.S, cfg.DI), F32, (tt, SSD_GW), col),
                     ((cfg.S, ng * SSD_N), F32, (tt, SSD_N), col),
                     ((cfg.S, ng * SSD_N), F32, (tt, SSD_N), col),
                     ((cfg.S, cfg.OP), BF16, (tt, SSD_GW), col),
                     ((ng, cfg.S, LANES), F32, (None, tt, LANES), lambda g, t: (g, rev(t), 0))],
                    accs=[vec, vec, vec, ((1, cfg.DI), (1, SSD_GW), lambda g, t: (0, g))],
                    carries=[(SSD_N, SSD_GW)])


PACK_ALIGN = SUBLANES * LANES
PACK_ROWS = 256


def pack(arrays):
    pieces = []
    for a in arrays:
        flat = a.reshape(-1).astype(F32)
        pad = _round_up(flat.shape[0], PACK_ALIGN) - flat.shape[0]
        pieces.append(jnp.pad(flat, (0, pad)) if pad else flat)
    flat = jnp.concatenate(pieces)
    pad = _round_up(flat.shape[0], PACK_ROWS * LANES) - flat.shape[0]
    return jnp.pad(flat, (0, pad)).reshape(-1, LANES)


def unpack(buf, shapes):
    lead = buf.shape[:-2]
    flat = buf.reshape(lead + (-1,))
    out, off = [], 0
    for s in shapes:
        n = math.prod(s)
        out.append(flat[..., off:off + n].reshape(lead + tuple(s)))
        off += _round_up(n, PACK_ALIGN)
    return out


def _slots_to_cols(slots):
    return slots.transpose(1, 0, 2).reshape(slots.shape[1], -1)


def _even_in_padded(cfg, w):
    main = 2 * cfg.DK + 2 * cfg.DV
    return jnp.concatenate([w[:, :main], w[:, main + GATE_RANK:], w[:, main:main + GATE_RANK],
                            jnp.zeros((w.shape[0], cfg.EP - cfg.EVEN_IN), w.dtype)], axis=1)


def _even_in_unpadded(cfg, wp):
    main = 2 * cfg.DK + 2 * cfg.DV
    rest = main + 2 * cfg.W
    return jnp.concatenate([wp[:, :main], wp[:, rest:rest + GATE_RANK], wp[:, main:rest]], axis=1)


def _odd_in_padded(cfg, w):
    return jnp.concatenate([w, jnp.zeros((w.shape[0], cfg.OP - cfg.ODD_IN), w.dtype)], axis=1)


def _odd_in_unpadded(cfg, wp):
    return wp[:, :cfg.ODD_IN]


def _group_lanes(cfg, v):
    lead = v.shape[:-1]
    g = jnp.moveaxis(v.reshape(lead + (cfg.NG, SSD_HG)), -2, 0)
    return jnp.pad(g, [(0, 0)] * (g.ndim - 1) + [(0, LANES - SSD_HG)])


def _ungroup_lanes(cfg, g):
    v = jnp.moveaxis(g[..., :SSD_HG], 0, -2)
    return v.reshape(v.shape[:-2] + (cfg.NH,))


def train_step(cfg, p, loss_target):
    S, D = cfg.S, cfg.D
    me = 4 * lax.axis_index("x") + 2 * lax.axis_index("y") + lax.axis_index("c")

    big = ["ev_w_in", "ev_w_out", "od_w_in", "od_w_out", "ffn_w_gate", "ffn_w_up", "ffn_w_down"]
    small_sharded = ["ev_gla_w_gate", "ev_lru_conv_w", "od_norm", "od_conv_w", "od_conv_b", "od_gnorm"]
    replicated = ["ev_norm", "ev_gla_b_gate", "ev_gla_w_onorm", "ev_lru_conv_b", "ev_lru_w_a", "ev_lru_b_a",
                  "ev_lru_w_i", "ev_lru_b_i", "ev_lru_lam", "od_dt_bias", "od_a_log", "od_d_skip", "ffn_norm",
                  "final_norm"]

    transposed = ("ffn_w_gate", "ffn_w_up")
    view = lambda n, a: jnp.swapaxes(a, 1, 2) if n in transposed else a
    wb = {n: view(n, p[n]).astype(BF16) for n in big}
    ffn_items = lambda l: [(wb["ffn_w_gate"], l), (wb["ffn_w_up"], l), (wb["ffn_w_down"], l)]
    ss_shapes = [p[n].shape for n in small_sharded]
    groups = [[(pack([p[n] for n in small_sharded]), None), (wb["ev_w_in"], 0), (wb["ev_w_out"], 0)], ffn_items(0),
              [(wb["od_w_in"], 0), (wb["od_w_out"], 0)], ffn_items(1)]
    gathers, tokens = [], []
    for i, g in enumerate(groups):
        lands = [_landing_zone(a, lead, me, f"gather_own{i}_{j}") for j, (a, lead) in enumerate(g)]
        handle, token = exchange_start(g, True, lands, f"gather_start{i}", NEAR_PEERS)
        gathers.append(handle)
        tokens.append(token)
    all_started = tokens[0][:1, :1] + tokens[1][:1, :1] + tokens[2][:1, :1] + tokens[3][:1, :1]

    def gathered(i, after, work):
        return relay(exchange_wait(gathers[i], after, f"gather_wait{i}")[1], f"gather_relay{i}", work)

    x0 = p["x"][0]
    (ss_all, gw_ev_in, gw_ev_out), h0 = gathered(
        0, all_started, lambda token: norm_fwd(cfg, x0, p["ev_norm"] + token[:1, :1], "ev_norm"))
    gs = dict(zip(small_sharded, unpack(ss_all, ss_shapes)))
    w_ev_in = cols_from_slots(gw_ev_in, functools.partial(_even_in_padded, cfg), cfg.EP, "ev_w_in_cols")
    w_ev_out = gw_ev_out.reshape(D, D)

    gla_wg = jnp.pad(_slots_to_cols(gs["ev_gla_w_gate"][:, 0]), ((0, LANES - GATE_RANK), (0, 0)))
    lru_cw = _slots_to_cols(gs["ev_lru_conv_w"][:, 0])
    od_norm = gs["od_norm"].transpose(1, 0, 2).reshape(1, D)
    od_cw = _slots_to_cols(gs["od_conv_w"][:, 0])
    od_cb = gs["od_conv_b"].transpose(1, 0, 2).reshape(1, cfg.CD)
    od_gn = gs["od_gnorm"].transpose(1, 0, 2).reshape(1, cfg.DI)

    target = loss_target[0]
    ev_norm = p["ev_norm"]
    bg = p["ev_gla_b_gate"]
    wn = p["ev_gla_w_onorm"]
    lru_cb = p["ev_lru_conv_b"]
    wa, wi = p["ev_lru_w_a"][0], p["ev_lru_w_i"][0]
    ba, bi, lam = p["ev_lru_b_a"], p["ev_lru_b_i"], p["ev_lru_lam"]
    dt_bias, a_log, d_skip = (_group_lanes(cfg, p[n]) for n in ("od_dt_bias", "od_a_log", "od_d_skip"))
    ffn_norm = [p["ffn_norm"][l:l + 1] for l in range(2)]
    final_norm = p["final_norm"].reshape(1, D)

    def ffn_forward(l, x, weights, next_group):
        w_gate, w_up, w_down = weights
        h = norm_fwd(cfg, x, ffn_norm[l], f"ffn{l}_norm")
        gate, up, act = ffn_gate_up(h, w_gate, w_up, f"ffn{l}_gate_up")
        down = lambda token: matmul(act, w_down, a_slot="k", b_slot="k", res=x, after=token, name=f"ffn{l}_down", tn=512)
        lands, out = (None, down(None)) if next_group is None else gathered(next_group, act, down)
        return out, (h, gate, up, act, w_gate, w_up, w_down), lands

    p0 = matmul(h0, w_ev_in, name="ev_in", tn=768)
    gla_out, gla_states = gla_fwd(cfg, p0, gla_wg, bg, wn, "gla_fwd")
    lru_col = 2 * cfg.DK + 2 * cfg.DV
    lru_xc = conv_fwd(cfg, p0, lru_col, cfg.W, lru_cw, lru_cb, "lru_conv")
    lru_out, lru_states = lru_fwd(cfg, lru_xc, p0, wa, wi, ba, bi, lam, "lru_fwd")
    mix = jnp.concatenate([gla_out, lru_out], axis=1)
    ffn0_weights, x1 = gathered(1, mix, lambda token: matmul(mix, w_ev_out, res=x0, after=token, name="ev_out"))
    x2, ffn0_saved, (gw_od_in, gw_od_out) = ffn_forward(0, x1, ffn0_weights, 2)

    w_od_in = cols_from_slots(gw_od_in, functools.partial(_odd_in_padded, cfg), cfg.OP, "od_w_in_cols")
    w_od_out = gw_od_out.reshape(cfg.DI, D)
    h2 = norm_fwd(cfg, x2, od_norm, "od_norm")
    p1 = matmul(h2, w_od_in, name="od_in", tn=768)
    od_xc = conv_fwd(cfg, p1, cfg.DI, cfg.CD, od_cw, od_cb, "od_conv")
    dt_col = cfg.DI + cfg.CD
    dt_raw = _group_lanes(cfg, p1[:, dt_col:dt_col + cfg.NH])
    ssd_out, ssd_states = ssd_fwd(cfg, od_xc, p1, dt_raw, dt_bias, a_log, d_skip, od_gn, "ssd_fwd")
    ffn1_weights, x3 = gathered(3, ssd_out, lambda token: matmul(ssd_out, w_od_out, res=x2, after=token, name="od_out"))
    x4, ffn1_saved, _ = ffn_forward(1, x3, ffn1_weights, None)

    loss_part, dx4, d_final_norm = head_fwd_bwd(cfg, x4, final_norm, target, "head")
    loss = lax.psum(loss_part[0, 0], MESH_AXES)

    def ffn_backward(l, x, saved, dx_out, after):
        h, gate, up, act, w_gate, w_up, w_down = saved
        dgate, dup = ffn_dgate_dup(dx_out, w_down, gate, up, after, f"ffn{l}_dgate_dup")
        d_down = matmul(act, dx_out, ta=True, a_slot="m", out_dtype=BF16, name=f"ffn{l}_dwdown")
        sent_down, token = start_grads([d_down], f"grads_start_ffn{l}_down")
        d_gate = matmul(dgate, h, ta=True, a_slot="m", after=token, out_dtype=BF16, name=f"ffn{l}_dwgate")
        d_up = matmul(dup, h, ta=True, a_slot="m", out_dtype=BF16, name=f"ffn{l}_dwup")
        sent_gate_up, token = start_grads([d_gate, d_up], f"grads_start_ffn{l}")
        dh = matmul(dgate, w_gate, a_slot="k", b_slot="k", after=token, name=f"ffn{l}_dh_gate", tn=512)
        dh = matmul(dup, w_up, a_slot="k", b_slot="k", res=dh, name=f"ffn{l}_dh_up", tn=512)
        dx, dnorm = norm_bwd(cfg, x, ffn_norm[l], dh, dx_out, f"ffn{l}_norm_bwd")
        return dx, dnorm, (sent_down, sent_gate_up)

    def start_grads(arrays, name):
        return exchange_start([(a, None) for a in arrays], False, [lax.empty(a.shape, a.dtype) for a in arrays], name)

    dx3, d_ffn_norm1, sent_ffn1 = ffn_backward(1, x3, ffn1_saved, dx4, None)

    d_ssd_out = matmul(dx3, w_od_out, tb=True, name="od_dmix")
    d_od_out = matmul(ssd_out, dx3, ta=True, out_dtype=BF16, name="od_dwout")
    dxs, dbm, dcm, dz, d_dt_raw, d_dt_bias, d_a_log, d_d_skip, d_od_gn = ssd_bwd(
        cfg, od_xc, p1, dt_raw, dt_bias, a_log, d_skip, od_gn, ssd_states, d_ssd_out, "ssd_bwd")
    dp1, conv_parts, col = dz, [], 0
    for part, dy in (("x", dxs), ("b", dbm), ("c", dcm)):
        width = dy.shape[1]
        dp1, dcw, dcb = conv_bwd(cfg, p1, cfg.DI + col, width, od_cw[:, col:col + width], dy, "od_conv_bwd_" + part,
                                 into=dp1, into_col0=cfg.DI + col)
        conv_parts.append((dcw, dcb))
        col += width
    d_od_cw = jnp.concatenate([c[0] for c in conv_parts], axis=1)
    d_od_cb = jnp.concatenate([c[1] for c in conv_parts], axis=1)
    d_dt = _ungroup_lanes(cfg, d_dt_raw).astype(BF16)
    tail = jnp.concatenate([d_dt, jnp.zeros((S, cfg.OP - cfg.ODD_IN), BF16)], axis=1)
    dp1 = write_cols(cfg, dp1, tail, cfg.DI + cfg.CD, "od_dt_cols")
    dh2 = matmul(dp1, w_od_in, tb=True, name="od_dh", tk=1536)
    d_od_in = matmul(h2, dp1, ta=True, out_dtype=BF16, name="od_dwin", tn=768)
    dx2, d_od_norm = norm_bwd(cfg, x2, od_norm, dh2, dx3, "od_norm_bwd")
    d_od_in_slots = slots_from_cols(d_od_in, functools.partial(_odd_in_unpadded, cfg), p["od_w_in"].shape[2],
                                    "od_dwin_slots")
    sent_od, token = start_grads([d_od_in_slots, d_od_out.reshape((N_DEV,) + p["od_w_out"].shape[1:])], "grads_start_od")

    dx1, d_ffn_norm0, sent_ffn0 = ffn_backward(0, x1, ffn0_saved, dx2, token)

    d_ev_out = matmul(mix, dx1, ta=True, out_dtype=BF16, name="ev_dwout")
    sent_ev_out, token = start_grads([d_ev_out.reshape((N_DEV,) + p["ev_w_out"].shape[1:])], "grads_start_ev_out")
    d_mix = matmul(dx1, w_ev_out, tb=True, after=token, name="ev_dmix")
    d_qkvg, d_glr, d_gla_wg, d_bg, d_wn = gla_bwd(cfg, p0, gla_wg, bg, wn, gla_states, d_mix[:, :cfg.DV], "gla_bwd")
    d_lru_xc, d_gate_br, d_wa, d_wi, d_ba, d_bi, d_lam = lru_bwd(
        cfg, lru_xc, p0, wa, wi, ba, bi, lam, lru_states, d_mix[:, cfg.DV:], "lru_bwd")
    d_xbr, d_lru_cw, d_lru_cb = conv_bwd(cfg, p0, lru_col, cfg.W, lru_cw, d_lru_xc, "lru_conv_bwd")
    dp0 = jnp.concatenate([d_qkvg, d_xbr, d_gate_br, d_glr, jnp.zeros((S, cfg.EP - lru_col - 2 * cfg.W - LANES), BF16)],
                          axis=1)
    d_ev_in = matmul(h0, dp0, ta=True, out_dtype=BF16, name="ev_dwin", tn=768)
    d_ev_in_slots = slots_from_cols(d_ev_in, functools.partial(_even_in_unpadded, cfg), p["ev_w_in"].shape[2],
                                    "ev_dwin_slots")
    sent_ev_in, token = start_grads([d_ev_in_slots], "grads_start_ev_in")
    dh0 = matmul(dp0, w_ev_in, tb=True, after=token, name="ev_dh", tk=1792)
    grad_x, d_ev_norm = norm_bwd(cfg, x0, ev_norm, dh0, dx1, "ev_norm_bwd")

    out = {"loss": loss, "grad_x": grad_x[None]}

    def update(names, sent, after, wait_name):
        s, r = exchange_wait(sent[0], after, wait_name + "0")
        sends, recvs = [[a] for a in s], [[a] for a in r]
        for extra in sent[1:]:
            s, r = exchange_wait(extra, after, wait_name + "1")
            for i in range(len(names)):
                sends[i].append(s[i])
                recvs[i].append(r[i])
        for i, n in enumerate(names):
            flip = n in ("ev_w_in", "od_w_in")
            shard = lambda a: jnp.swapaxes(a, 1, 2) if flip else view(n, a)
            res = adamw_sharded(recvs[i], sends[i], me, shard(p[n]), shard(p["m_" + n]), shard(p["v_" + n]),
                                "adamw_" + n, transposed=flip)
            out["grad_" + n], out["delta_" + n], out["new_m_" + n], out["new_v_" + n] = (shard(a) for a in res)
        return res[-1]

    small_full = {
        "ev_gla_w_gate": d_gla_wg[:GATE_RANK][None], "ev_lru_conv_w": d_lru_cw[None], "od_norm": d_od_norm,
        "od_conv_w": d_od_cw[None], "od_conv_b": d_od_cb, "od_gnorm": d_od_gn,
        "ev_norm": d_ev_norm, "ev_gla_b_gate": d_bg, "ev_gla_w_onorm": d_wn, "ev_lru_conv_b": d_lru_cb,
        "ev_lru_w_a": d_wa[None], "ev_lru_b_a": d_ba, "ev_lru_w_i": d_wi[None], "ev_lru_b_i": d_bi,
        "ev_lru_lam": d_lam, "od_dt_bias": _ungroup_lanes(cfg, d_dt_bias), "od_a_log": _ungroup_lanes(cfg, d_a_log),
        "od_d_skip": _ungroup_lanes(cfg, d_d_skip), "ffn_norm": jnp.concatenate([d_ffn_norm0, d_ffn_norm1], axis=0),
        "final_norm": d_final_norm.reshape(D),
    }
    small = small_sharded + replicated
    small_packed = pack([small_full[n] for n in small])
    sent_small, token = exchange_start([(small_packed, None)], True,
                                       [_landing_zone(small_packed, None, me, "gather_small_grads_own")],
                                       "gather_small_grads")

    done = update(["od_w_in", "od_w_out"], [sent_od], token, "grads_wait_od")
    done = update(["ffn_w_down"], [sent_ffn0[0], sent_ffn1[0]], done, "grads_wait_ffn_down")
    done = update(["ffn_w_gate", "ffn_w_up"], [sent_ffn0[1], sent_ffn1[1]], done, "grads_wait_ffn")
    done = update(["ev_w_out"], [sent_ev_out], done, "grads_wait_ev_out")
    done = update(["ev_w_in"], [sent_ev_in], done, "grads_wait_ev_in")

    small_all = exchange_wait(sent_small, done, "gather_small_grads_wait")[1][0]
    g_small = dict(zip(small, unpack(reduce_slots(small_all, "sum_small_grads"), [small_full[n].shape for n in small])))
    for n in small_sharded:
        width = p[n].shape[-1]
        g_small[n] = lax.dynamic_slice_in_dim(g_small[n], me * width, width, axis=g_small[n].ndim - 1)
    shapes = [p[n].shape for n in small]
    g_buf = pack([g_small[n] for n in small])
    delta, new_m, new_v = adamw(pack([p[n] for n in small]), g_buf, pack([p["m_" + n] for n in small]),
                                pack([p["v_" + n] for n in small]), "adamw_small")
    for kind, buf in (("grad_", g_buf), ("delta_", delta), ("new_m_", new_m), ("new_v_", new_v)):
        for n, a in zip(small, unpack(buf, shapes)):
            out[kind + n] = a
    return out


WEIGHTS = ['ev_norm', 'ev_w_in', 'ev_gla_w_gate', 'ev_gla_b_gate', 'ev_gla_w_onorm', 'ev_lru_conv_w', 'ev_lru_conv_b',
           'ev_lru_w_a', 'ev_lru_b_a', 'ev_lru_w_i', 'ev_lru_b_i', 'ev_lru_lam', 'ev_w_out', 'od_norm', 'od_w_in',
           'od_conv_w', 'od_conv_b', 'od_dt_bias', 'od_a_log', 'od_d_skip', 'od_gnorm', 'od_w_out', 'ffn_norm',
           'ffn_w_gate', 'ffn_w_up', 'ffn_w_down', 'final_norm']


def kernel(x, ev_norm, ev_w_in, ev_gla_w_gate, ev_gla_b_gate, ev_gla_w_onorm, ev_lru_conv_w, ev_lru_conv_b, ev_lru_w_a, ev_lru_b_a, ev_lru_w_i, ev_lru_b_i, ev_lru_lam, ev_w_out, od_norm, od_w_in, od_conv_w, od_conv_b, od_dt_bias, od_a_log, od_d_skip, od_gnorm, od_w_out, ffn_norm, ffn_w_gate, ffn_w_up, ffn_w_down, final_norm, loss_target, m_ev_norm, m_ev_w_in, m_ev_gla_w_gate, m_ev_gla_b_gate, m_ev_gla_w_onorm, m_ev_lru_conv_w, m_ev_lru_conv_b, m_ev_lru_w_a, m_ev_lru_b_a, m_ev_lru_w_i, m_ev_lru_b_i, m_ev_lru_lam, m_ev_w_out, m_od_norm, m_od_w_in, m_od_conv_w, m_od_conv_b, m_od_dt_bias, m_od_a_log, m_od_d_skip, m_od_gnorm, m_od_w_out, m_ffn_norm, m_ffn_w_gate, m_ffn_w_up, m_ffn_w_down, m_final_norm, v_ev_norm, v_ev_w_in, v_ev_gla_w_gate, v_ev_gla_b_gate, v_ev_gla_w_onorm, v_ev_lru_conv_w, v_ev_lru_conv_b, v_ev_lru_w_a, v_ev_lru_b_a, v_ev_lru_w_i, v_ev_lru_b_i, v_ev_lru_lam, v_ev_w_out, v_od_norm, v_od_w_in, v_od_conv_w, v_od_conv_b, v_od_dt_bias, v_od_a_log, v_od_d_skip, v_od_gnorm, v_od_w_out, v_ffn_norm, v_ffn_w_gate, v_ffn_w_up, v_ffn_w_down, v_final_norm):
    args = dict(locals())
    p = {n: a for n, a in args.items() if n != "loss_target"}
    cfg = Cfg(S=x.shape[1], D=x.shape[2], DFF=ffn_w_gate.shape[2] * N_DEV)
    out = train_step(cfg, p, loss_target)
    return (out["loss"], out["grad_x"], *[out["grad_" + w] for w in WEIGHTS], *[out["delta_" + w] for w in WEIGHTS],
            *[out["new_m_" + w] for w in WEIGHTS], *[out["new_v_" + w] for w in WEIGHTS])
```

```python
import functools
import math
from typing import NamedTuple

import jax
import jax.numpy as jnp
from jax import lax
from jax.experimental import pallas as pl
from jax.experimental.pallas import tpu as pltpu

F32 = jnp.float32
BF16 = jnp.bfloat16
MESH_AXES = ("x", "y", "c")
N_DEV = 8
LANES = 128
SUBLANES = 8
VMEM_LIMIT = 56 * 1024 * 1024

NORM_EPS = 1e-6
CONV_WIDTH = 4
CHUNK = 64
HEAD_K = 128
HEAD_V = 256
GATE_RANK = 16
GATE_NORM = 16.0
LRU_BLOCK = 128
LRU_C = 8.0
SSD_P = 64
SSD_N = 128
SSD_HG = 8
SSD_GW = SSD_HG * SSD_P

ADAM_LR = 0.001
ADAM_B1 = 0.9
ADAM_B2 = 0.999
ADAM_EPS = 1e-08
ADAM_WD = 0.01
ADAM_STEP = 10


class Cfg(NamedTuple):
    S: int
    D: int
    DFF: int

    @property
    def GH(self):
        return self.D // 512

    @property
    def NB(self):
        return self.D // 256

    @property
    def NG(self):
        return self.D // 256

    @property
    def DK(self):
        return HEAD_K * self.GH

    @property
    def DV(self):
        return HEAD_V * self.GH

    @property
    def W(self):
        return LRU_BLOCK * self.NB

    @property
    def DI(self):
        return SSD_GW * self.NG

    @property
    def CD(self):
        return self.DI + 2 * self.NG * SSD_N

    @property
    def NH(self):
        return SSD_HG * self.NG

    @property
    def EVEN_IN(self):
        return 2 * self.DK + 2 * self.DV + GATE_RANK + 2 * self.W

    @property
    def ODD_IN(self):
        return self.DI + self.CD + self.NH

    @property
    def EP(self):
        return _round_up(2 * self.DK + 2 * self.DV + 2 * self.W + LANES, 768)

    @property
    def OP(self):
        return _round_up(self.DI + self.CD + LANES, 768)


def _round_up(n, m):
    return (n + m - 1) // m * m


def _tile(n, pref):
    if n <= pref:
        return n
    t = pref - pref % LANES
    while n % t:
        t -= LANES
    return t


def _cparams(n_axes):
    return pltpu.CompilerParams(dimension_semantics=("arbitrary",) * n_axes, vmem_limit_bytes=VMEM_LIMIT)


def _dg(a, b, ca, cb):
    return lax.dot_general(a.astype(BF16), b.astype(BF16), (((ca,), (cb,)), ((), ())), preferred_element_type=F32)


@functools.partial(jax.custom_vjp, nondiff_argnums=(2, 3))
def bdot(a, b, ca, cb):
    return _dg(a, b, ca, cb)


def _bdot_fwd(a, b, ca, cb):
    return _dg(a, b, ca, cb), (a, b)


def _bdot_bwd(ca, cb, res, g):
    a, b = res
    da = _dg(g, b, 1, 1 - cb) if ca == 1 else _dg(b, g, 1 - cb, 1)
    db = _dg(a, g, 1 - ca, 0) if cb == 0 else _dg(g, a, 0, 1 - ca)
    return da.astype(a.dtype), db.astype(b.dtype)


bdot.defvjp(_bdot_fwd, _bdot_bwd)


def _lower_tri(n):
    r = lax.broadcasted_iota(jnp.int32, (n, n), 0)
    c = lax.broadcasted_iota(jnp.int32, (n, n), 1)
    return c <= r


def _running_sum(x, reverse):
    n = x.shape[0]
    r = lax.broadcasted_iota(jnp.int32, x.shape, 0)
    d = 1
    while d < n:
        if reverse:
            x = x + jnp.where(r < n - d, pltpu.roll(x, n - d, 0), 0.0)
        else:
            x = x + jnp.where(r >= d, pltpu.roll(x, d, 0), 0.0)
        d *= 2
    return x


@jax.custom_vjp
def cumsum_rows(x):
    return _running_sum(x, False)


cumsum_rows.defvjp(lambda x: (_running_sum(x, False), None), lambda _, g: (_running_sum(g, True),))


def _tri_dot(x, transposed):
    n = x.shape[0]
    return lax.dot_general(_lower_tri(n).astype(F32), x, (((0 if transposed else 1,), (0,)), ((), ())),
                           precision=lax.Precision.HIGHEST, preferred_element_type=F32)


@jax.custom_vjp
def cumsum_rows_mxu(x):
    return _tri_dot(x, False)


cumsum_rows_mxu.defvjp(lambda x: (_tri_dot(x, False), None), lambda _, g: (_tri_dot(g, True),))


def _row(x, i):
    r = lax.broadcasted_iota(jnp.int32, x.shape, 0)
    return jnp.sum(jnp.where(r == i, x, 0.0), axis=0, keepdims=True)


def _softplus_raw(x):
    return jnp.maximum(x, 0.0) + jnp.log(1.0 + jnp.exp(-jnp.abs(x)))


@jax.custom_vjp
def softplus(x):
    return _softplus_raw(x)


softplus.defvjp(lambda x: (_softplus_raw(x), x), lambda x, g: (g * jax.nn.sigmoid(x),))


@jax.custom_vjp
def log_sigmoid(x):
    return -_softplus_raw(-x)


log_sigmoid.defvjp(lambda x: (-_softplus_raw(-x), x), lambda x, g: (g * jax.nn.sigmoid(-x),))


def silu(x):
    return x * jax.nn.sigmoid(x)


def gelu_tanh(x):
    return 0.5 * x * (1.0 + jnp.tanh(math.sqrt(2.0 / math.pi) * (x + 0.044715 * (x * x * x))))


def _expm1(x):
    series = x * (1.0 + 0.5 * x * (1.0 + (1.0 / 3.0) * x))
    return jnp.where(jnp.abs(x) < 1e-2, series, jnp.exp(x) - 1.0)


def rms(x, w):
    return x * lax.rsqrt(jnp.mean(x * x, axis=-1, keepdims=True) + NORM_EPS) * w


def _rows_iota(shape):
    return lax.broadcasted_iota(jnp.int32, shape, 0)


def _scan_up(a, u):
    n = a.shape[0]
    r = _rows_iota(a.shape)
    d = 1
    while d < n:
        m = r >= d
        a_s = jnp.where(m, pltpu.roll(a, d, 0), 1.0)
        u_s = jnp.where(m, pltpu.roll(u, d, 0), 0.0)
        u = a * u_s + u
        a = a * a_s
        d *= 2
    return u


def _scan_down(a, u):
    n = a.shape[0]
    r = _rows_iota(a.shape)
    d = 1
    while d < n:
        m = r < n - d
        a_s = jnp.where(m, pltpu.roll(a, n - d, 0), 1.0)
        u_s = jnp.where(m, pltpu.roll(u, n - d, 0), 0.0)
        u = a * u_s + u
        a = a * a_s
        d *= 2
    return u


@jax.custom_vjp
def lin_scan(a, u):
    return _scan_up(a, u)


def _lin_scan_fwd(a, u):
    h = _scan_up(a, u)
    return h, (a, h)


def _lin_scan_bwd(res, g):
    a, h = res
    n = a.shape[0]
    r = _rows_iota(a.shape)
    a_next = jnp.where(r < n - 1, pltpu.roll(a, n - 1, 0), 0.0)
    gt = _scan_down(a_next, g)
    h_prev = jnp.where(r >= 1, pltpu.roll(h, 1, 0), 0.0)
    return gt * h_prev, gt


lin_scan.defvjp(_lin_scan_fwd, _lin_scan_bwd)


def _expand_heads(v):
    r = v.shape[0]
    return jnp.concatenate([jnp.broadcast_to(v[:, h:h + 1], (r, SSD_P)) for h in range(SSD_HG)], axis=1)


@jax.custom_vjp
def _split_heads(x):
    return tuple(x[:, h * SSD_P:(h + 1) * SSD_P] for h in range(SSD_HG))


_split_heads.defvjp(lambda x: (_split_heads(x), None), lambda _, gs: (jnp.concatenate(gs, axis=1),))


def matmul(a, b, *, ta=False, tb=False, a_slot=None, b_slot=None, res=None, after=None, out_dtype=F32,
           name, tm=1024, tn=1024, tk=2048):
    ra, ca_ = a.shape[-2:]
    rb, cb_ = b.shape[-2:]
    m_st, ka_st = (ca_, ra) if ta else (ra, ca_)
    kb_st, n_st = (cb_, rb) if tb else (rb, cb_)
    kslot = a_slot == "k"
    assert kslot == (b_slot == "k")
    m = m_st * (N_DEV if a_slot == "m" else 1)
    n = n_st * (N_DEV if b_slot == "n" else 1)
    assert ka_st == kb_st, (a.shape, b.shape, ta, tb)
    tm = m_st if a_slot == "m" else _tile(m, tm)
    tn = n_st if b_slot == "n" else _tile(n, tn)
    tk = ka_st if kslot else _tile(ka_st, tk)
    nk = ka_st // tk
    ca, cb = (0 if ta else 1), (1 if tb else 0)

    if a_slot is None:
        a_spec = pl.BlockSpec((tk, tm), lambda i, j, k: (k, i)) if ta else pl.BlockSpec((tm, tk), lambda i, j, k: (i, k))
    elif a_slot == "m":
        a_spec = (pl.BlockSpec((None, tk, tm), lambda i, j, k: (i, k, 0)) if ta
                  else pl.BlockSpec((None, tm, tk), lambda i, j, k: (i, 0, k)))
    else:
        a_spec = (pl.BlockSpec((N_DEV, tk, tm), lambda i, j, k: (0, 0, i)) if ta
                  else pl.BlockSpec((N_DEV, tm, tk), lambda i, j, k: (0, i, 0)))
    if b_slot is None:
        b_spec = pl.BlockSpec((tn, tk), lambda i, j, k: (j, k)) if tb else pl.BlockSpec((tk, tn), lambda i, j, k: (k, j))
    elif b_slot == "n":
        b_spec = (pl.BlockSpec((None, tn, tk), lambda i, j, k: (j, 0, k)) if tb
                  else pl.BlockSpec((None, tk, tn), lambda i, j, k: (j, k, 0)))
    else:
        b_spec = (pl.BlockSpec((N_DEV, tn, tk), lambda i, j, k: (0, j, 0)) if tb
                  else pl.BlockSpec((N_DEV, tk, tn), lambda i, j, k: (0, 0, j)))
    if a_slot == "m":
        o_spec, o_shape = pl.BlockSpec((None, tm, tn), lambda i, j, k: (i, 0, j)), (N_DEV, tm, n)
    elif b_slot == "n":
        o_spec, o_shape = pl.BlockSpec((None, tm, tn), lambda i, j, k: (j, i, 0)), (N_DEV, m, tn)
    else:
        o_spec, o_shape = pl.BlockSpec((tm, tn), lambda i, j, k: (i, j)), (m, n)
    assert res is None or (a_slot != "m" and b_slot != "n")

    def dot(x, y):
        return lax.dot_general(x.astype(BF16), y.astype(BF16), (((ca,), (cb,)), ((), ())), preferred_element_type=F32)

    def body(*refs):
        a_ref, b_ref = refs[:2]
        r_ref = refs[2] if res is not None else None
        o_ref = refs[2 + (res is not None) + (after is not None)]

        def finish(acc):
            if r_ref is not None:
                acc = acc + r_ref[...].astype(F32)
            o_ref[...] = acc.astype(o_ref.dtype)

        if kslot:
            acc = dot(a_ref[0], b_ref[0])
            for s in range(1, N_DEV):
                acc = acc + dot(a_ref[s], b_ref[s])
            finish(acc)
        elif nk == 1:
            finish(dot(a_ref[...], b_ref[...]))
        else:
            acc_ref = refs[-1]
            k = pl.program_id(2)

            @pl.when(k == 0)
            def _():
                acc_ref[...] = dot(a_ref[...], b_ref[...])

            @pl.when(k > 0)
            def _():
                acc_ref[...] += dot(a_ref[...], b_ref[...])

            @pl.when(k == nk - 1)
            def _():
                finish(acc_ref[...])

    in_specs = [a_spec, b_spec]
    args = [a, b]
    if res is not None:
        in_specs.append(pl.BlockSpec((tm, tn), lambda i, j, k: (i, j)))
        args.append(res)
    if after is not None:
        in_specs.append(pl.BlockSpec(memory_space=pl.ANY))
        args.append(after)
    return pl.pallas_call(
        body, name=name, grid=(m // tm, n // tn, nk), in_specs=in_specs, out_specs=o_spec,
        out_shape=jax.ShapeDtypeStruct(o_shape, out_dtype),
        scratch_shapes=[pltpu.VMEM((tm, tn), F32)] if nk > 1 else [], compiler_params=_cparams(3),
    )(*args)


def seq_call(name, fn, grid, ins, outs, accs=(), carries=()):
    n_in, n_out, n_acc = len(ins), len(outs), len(accs)

    def body(*refs):
        in_refs = refs[:n_in]
        out_refs = refs[n_in:n_in + n_out]
        acc_refs = refs[n_in + n_out:n_in + n_out + n_acc]
        c_refs = refs[n_in + n_out + n_acc:]

        if acc_refs or c_refs:
            @pl.when(pl.program_id(1) == 0)
            def _():
                for r in tuple(acc_refs) + tuple(c_refs):
                    r[...] = jnp.zeros_like(r)

        o, a, c = fn([r[...] for r in in_refs], [r[...] for r in c_refs])
        for r, v in zip(out_refs, o, strict=True):
            r[...] = v.astype(r.dtype)
        for r, v in zip(acc_refs, a, strict=True):
            r[...] += v
        for r, v in zip(c_refs, c, strict=True):
            r[...] = v

    return pl.pallas_call(
        body, name=name, grid=grid,
        in_specs=[pl.BlockSpec(blk, im) for _, blk, im in ins],
        out_specs=[pl.BlockSpec(blk, im) for _, _, blk, im in outs] + [pl.BlockSpec(blk, im) for _, blk, im in accs],
        out_shape=[jax.ShapeDtypeStruct(s, d) for s, d, _, _ in outs] + [jax.ShapeDtypeStruct(s, F32) for s, _, _ in accs],
        scratch_shapes=[pltpu.VMEM(s, F32) for s in carries], compiler_params=_cparams(2),
    )(*[a for a, _, _ in ins])


_HBM = pl.BlockSpec(memory_space=pltpu.HBM)
_SEM = pl.BlockSpec(memory_space=pltpu.SEMAPHORE)


def _mesh_pos():
    pos = [lax.axis_index(ax) for ax in MESH_AXES]
    return pos, 4 * pos[0] + 2 * pos[1] + pos[2]


ALL_PEERS = (1, 2, 3, 4, 5, 6, 7)
NEAR_PEERS = (1, 2, 4, 6)
RELAYED = (2, 4, 6)


def _peers(pos, masks=ALL_PEERS):
    out = []
    for k in masks:
        bits = ((k >> 2) & 1, (k >> 1) & 1, k & 1)
        peer = tuple(1 - p if b else p for p, b in zip(pos, bits))
        out.append((peer, 4 * peer[0] + 2 * peer[1] + peer[2]))
    return out


def _part(x_ref, lead, gather, slot):
    ref = x_ref if lead is None else x_ref.at[lead]
    return ref if gather else ref.at[slot]


OWN_BLOCK_BYTES = 2 * 1024 * 1024


def _landing_zone(a, lead, me, name):
    r, c = a.shape[-2:]
    tr = r
    while tr * _round_up(c, LANES) * a.dtype.itemsize > OWN_BLOCK_BYTES and tr % 32 == 0:
        tr //= 2

    def body(me_ref, x_ref, o_ref):
        o_ref[...] = x_ref[...]

    x_spec = (pl.BlockSpec((tr, c), lambda i, me_ref: (i, 0)) if lead is None
              else pl.BlockSpec((None, tr, c), lambda i, me_ref: (lead, i, 0)))
    grid_spec = pltpu.PrefetchScalarGridSpec(
        num_scalar_prefetch=1, grid=(r // tr,), in_specs=[x_spec],
        out_specs=pl.BlockSpec((None, tr, c), lambda i, me_ref: (me_ref[0], i, 0)))
    return pl.pallas_call(body, name=name, grid_spec=grid_spec, out_shape=jax.ShapeDtypeStruct((N_DEV, r, c), a.dtype),
                          compiler_params=_cparams(1))(jnp.reshape(me, (1,)).astype(jnp.int32), a)


def _split_copies(items, gather, masks, x_refs, land_refs, send_sems, recv_sems):
    pos, me = _mesh_pos()
    copies = []
    for i, (_, lead) in enumerate(items):
        for k, (peer, peer_id) in enumerate(_peers(pos, masks)):
            copies.append(pltpu.make_async_remote_copy(
                src_ref=_part(x_refs[i], lead, gather, peer_id), dst_ref=land_refs[i].at[me],
                send_sem=send_sems.at[i * len(masks) + k], recv_sem=recv_sems.at[i * len(masks) + k],
                device_id=peer, device_id_type=pl.DeviceIdType.MESH))
    return copies


_SPLIT_CALL = dict(compiler_params=pltpu.CompilerParams(has_side_effects=pltpu.SideEffectType.DATAFLOW_SIDE_EFFECTING))


def exchange_start(items, gather, lands, name, masks=ALL_PEERS):
    n = len(items)
    lands = list(lands)
    xs = [a for a, _ in items]

    def body(*refs):
        x_refs, land_refs = refs[:n], refs[n:2 * n]
        send_sems, recv_sems, token = refs[2 * n], refs[2 * n + 1], refs[-1]
        for cp in _split_copies(items, gather, masks, x_refs, land_refs, send_sems, recv_sems):
            cp.start()
        token[...] = jnp.zeros_like(token)

    outs = pl.pallas_call(
        body, name=name,
        out_shape=(pltpu.SemaphoreType.DMA((n * len(masks),)), pltpu.SemaphoreType.DMA((n * len(masks),)),
                   *[pltpu.HBM(v.shape, v.dtype) for v in xs + lands], jax.ShapeDtypeStruct((SUBLANES, LANES), F32)),
        in_specs=[_HBM] * (2 * n),
        out_specs=(_SEM, _SEM, *[_HBM] * (2 * n), pl.BlockSpec(memory_space=pltpu.VMEM)),
        input_output_aliases={i: 2 + i for i in range(2 * n)}, **_SPLIT_CALL,
    )(*[pltpu.with_memory_space_constraint(v, pltpu.HBM) for v in xs + lands])
    handle = (items, gather, masks, outs[0], outs[1], outs[2:2 + n], outs[2 + n:2 + 2 * n])
    return handle, outs[-1]


def _relay_copies(n, land_refs, send_sems, recv_sems):
    pos, _ = _mesh_pos()
    sibling = (pos[0], pos[1], 1 - pos[2])
    copies = []
    for i in range(n):
        for k, (_, peer_id) in enumerate(_peers(pos, RELAYED)):
            slot = land_refs[i].at[peer_id]
            copies.append(pltpu.make_async_remote_copy(
                src_ref=slot, dst_ref=slot, send_sem=send_sems.at[i * len(RELAYED) + k],
                recv_sem=recv_sems.at[i * len(RELAYED) + k], device_id=sibling, device_id_type=pl.DeviceIdType.MESH))
    return copies


def relay(lands, name, work=None):
    n = len(lands)
    sems = n * len(RELAYED)

    def start(*refs):
        for cp in _relay_copies(n, refs[:n], refs[n], refs[n + 1]):
            cp.start()
        refs[-1][...] = jnp.zeros_like(refs[-1])

    outs = pl.pallas_call(
        start, name=name + "_start",
        out_shape=(pltpu.SemaphoreType.DMA((sems,)), pltpu.SemaphoreType.DMA((sems,)),
                   *[pltpu.HBM(v.shape, v.dtype) for v in lands], jax.ShapeDtypeStruct((SUBLANES, LANES), F32)),
        in_specs=[_HBM] * n, out_specs=(_SEM, _SEM, *[_HBM] * n, pl.BlockSpec(memory_space=pltpu.VMEM)),
        input_output_aliases={i: 2 + i for i in range(n)},
        **_SPLIT_CALL)(*[pltpu.with_memory_space_constraint(v, pltpu.HBM) for v in lands])

    def wait(*refs):
        for cp in _relay_copies(n, refs[:n], refs[n], refs[n + 1]):
            cp.wait_send()
            cp.wait_recv()

    done = None if work is None else work(outs[-1])
    extra = [] if work is None else [jax.tree.leaves(done)[0]]
    filled = list(pl.pallas_call(
        wait, name=name + "_wait", out_shape=tuple(pltpu.HBM(v.shape, v.dtype) for v in lands),
        in_specs=[_HBM] * n + [_SEM, _SEM] + [pl.BlockSpec(memory_space=pl.ANY)] * len(extra),
        out_specs=tuple([_HBM] * n), input_output_aliases={i: i for i in range(n)},
        **_SPLIT_CALL)(*outs[2:2 + n], outs[0], outs[1], *extra))
    return filled if work is None else (filled, done)


def exchange_wait(handle, after, name):
    items, gather, masks, send_sems, recv_sems, x_thru, land_thru = handle
    n = len(items)

    def body(*refs):
        x_refs, land_refs = refs[:n], refs[n:2 * n]
        for cp in _split_copies(items, gather, masks, x_refs, land_refs, refs[2 * n], refs[2 * n + 1]):
            cp.wait_send()
            cp.wait_recv()

    outs = pl.pallas_call(
        body, name=name, out_shape=tuple(pltpu.HBM(v.shape, v.dtype) for v in tuple(x_thru) + tuple(land_thru)),
        in_specs=[_HBM] * (2 * n) + [_SEM, _SEM, pl.BlockSpec(memory_space=pl.ANY)], out_specs=tuple([_HBM] * (2 * n)),
        input_output_aliases={i: i for i in range(2 * n)},
        compiler_params=pltpu.CompilerParams(has_side_effects=pltpu.SideEffectType.DATAFLOW_SIDE_EFFECTING),
    )(*x_thru, *land_thru, send_sems, recv_sems, after)
    return list(outs[:n]), list(outs[n:])


def _adam_update(w, g, m, v):
    nm = ADAM_B1 * m + (1.0 - ADAM_B1) * g
    nv = ADAM_B2 * v + (1.0 - ADAM_B2) * (g * g)
    m_hat = nm / (1.0 - ADAM_B1 ** ADAM_STEP)
    v_hat = nv / (1.0 - ADAM_B2 ** ADAM_STEP)
    return -ADAM_LR * (m_hat / (jnp.sqrt(v_hat) + ADAM_EPS) + ADAM_WD * w), nm, nv


def _sum_slots(s_ref):
    acc = s_ref[0].astype(F32)
    for j in range(1, N_DEV):
        acc = acc + s_ref[j].astype(F32)
    return acc


ADAM_BLOCK_BYTES = 10 * 1024 * 1024


def adamw_sharded(recvs, sends, me, w, m, v, name, transposed=False):
    nl = w.shape[0]
    r, c = recvs[0].shape[1:]
    assert w.shape[1:] == ((c, r) if transposed else (r, c))
    assert len(recvs) == nl and len(sends) == nl
    per_row = _round_up(c, LANES) * (nl * (N_DEV + 1) * recvs[0].dtype.itemsize + 7 * 4)
    tr = r
    while tr * per_row > ADAM_BLOCK_BYTES and tr % 16 == 0:
        tr //= 2

    def body(me_ref, *refs):
        s_refs, o_refs = refs[:nl], refs[nl:2 * nl]
        w_ref, m_ref, v_ref, g_ref, d_ref, nm_ref, nv_ref = refs[2 * nl:]
        mine = me_ref[0]

        def total(l):
            acc = jnp.where(mine == 0, o_refs[l][...], s_refs[l][0]).astype(F32)
            for j in range(1, N_DEV):
                acc = acc + jnp.where(mine == j, o_refs[l][...], s_refs[l][j]).astype(F32)
            return acc

        g = total(0)
        for l in range(1, nl):
            g = jnp.where(pl.program_id(0) == l, total(l), g)
        if transposed:
            g = g.T
        g_ref[...] = g
        d_ref[...], nm_ref[...], nv_ref[...] = _adam_update(w_ref[...], g, m_ref[...], v_ref[...])

    spec = (pl.BlockSpec((None, c, tr), lambda l, i, me_ref: (l, 0, i)) if transposed
            else pl.BlockSpec((None, tr, c), lambda l, i, me_ref: (l, i, 0)))
    shp = jax.ShapeDtypeStruct(w.shape, F32)
    grid_spec = pltpu.PrefetchScalarGridSpec(
        num_scalar_prefetch=1, grid=(nl, r // tr),
        in_specs=([pl.BlockSpec((N_DEV, tr, c), lambda l, i, me_ref: (0, i, 0))] * nl
                  + [pl.BlockSpec((None, tr, c), lambda l, i, me_ref: (me_ref[0], i, 0))] * nl + [spec, spec, spec]),
        out_specs=[spec] * 4)
    return pl.pallas_call(body, name=name, grid_spec=grid_spec, out_shape=[shp] * 4, compiler_params=_cparams(2))(
        jnp.reshape(me, (1,)).astype(jnp.int32), *recvs, *sends, w, m, v)


def reduce_slots(slots, name):
    _, r, lanes = slots.shape
    tr = _tile(r, 2048)

    def body(s_ref, o_ref):
        o_ref[...] = _sum_slots(s_ref)

    return pl.pallas_call(
        body, name=name, grid=(r // tr,),
        in_specs=[pl.BlockSpec((N_DEV, tr, lanes), lambda i: (0, i, 0))],
        out_specs=pl.BlockSpec((tr, lanes), lambda i: (i, 0)),
        out_shape=jax.ShapeDtypeStruct((r, lanes), F32), compiler_params=_cparams(1),
    )(slots)


def adamw(w, g, m, v, name):
    r, lanes = w.shape
    tr = _tile(r, 2048)

    def body(w_ref, g_ref, m_ref, v_ref, d_ref, nm_ref, nv_ref):
        d_ref[...], nm_ref[...], nv_ref[...] = _adam_update(w_ref[...], g_ref[...], m_ref[...], v_ref[...])

    spec = pl.BlockSpec((tr, lanes), lambda i: (i, 0))
    shp = jax.ShapeDtypeStruct((r, lanes), F32)
    return pl.pallas_call(body, name=name, grid=(r // tr,), in_specs=[spec] * 4, out_specs=[spec] * 3,
                          out_shape=[shp] * 3, compiler_params=_cparams(1))(w, g, m, v)


def cols_from_slots(slots, place, width, name):
    _, rows, c = slots.shape
    tr = _tile(rows, 256)

    def body(s_ref, o_ref):
        o_ref[...] = place(jnp.concatenate([s_ref[j] for j in range(N_DEV)], axis=1))

    return pl.pallas_call(
        body, name=name, grid=(rows // tr,),
        in_specs=[pl.BlockSpec((N_DEV, tr, c), lambda i: (0, i, 0))],
        out_specs=pl.BlockSpec((tr, width), lambda i: (i, 0)),
        out_shape=jax.ShapeDtypeStruct((rows, width), slots.dtype), compiler_params=_cparams(1))(slots)


def slots_from_cols(full, pick, c, name):
    rows, wide = full.shape
    tr = _tile(rows, 256)

    def body(x_ref, o_ref):
        v = pick(x_ref[...])
        for j in range(N_DEV):
            o_ref[j] = v[:, j * c:(j + 1) * c]

    return pl.pallas_call(
        body, name=name, grid=(rows // tr,),
        in_specs=[pl.BlockSpec((tr, wide), lambda i: (i, 0))],
        out_specs=pl.BlockSpec((N_DEV, tr, c), lambda i: (0, i, 0)),
        out_shape=jax.ShapeDtypeStruct((N_DEV, rows, c), full.dtype), compiler_params=_cparams(1))(full)


def _token_tile(cfg):
    return min(cfg.S, 256)


def norm_fwd(cfg, x, w, name):
    ts = _token_tile(cfg)
    d = x.shape[1]

    def fn(ins, _):
        xv, wv = ins
        return [rms(xv, wv)], [], []

    return seq_call(name, fn, (1, cfg.S // ts),
                    [(x, (ts, d), lambda g, t: (t, 0)), (w, (1, d), lambda g, t: (0, 0))],
                    [((cfg.S, d), BF16, (ts, d), lambda g, t: (t, 0))])[0]


def norm_bwd(cfg, x, w, dh, dres, name):
    ts = _token_tile(cfg)
    d = x.shape[1]

    def fn(ins, _):
        xv, wv, dhv, drv = ins
        _, vjp = jax.vjp(rms, xv, wv)
        dx, dw = vjp(dhv.astype(F32))
        return [dx + drv], [dw], []

    row = lambda g, t: (t, 0)
    out = seq_call(name, fn, (1, cfg.S // ts),
                   [(x, (ts, d), row), (w, (1, d), lambda g, t: (0, 0)), (dh, (ts, d), row), (dres, (ts, d), row)],
                   [((cfg.S, d), F32, (ts, d), row)],
                   accs=[((1, d), (1, d), lambda g, t: (0, 0))])
    return out[0], out[1]


def head_fwd_bwd(cfg, x, w, target, name):
    ts = _token_tile(cfg)
    d = x.shape[1]

    def fn(ins, _):
        xv, wv, tv = ins
        y, vjp = jax.vjp(rms, xv, wv)
        err = y - tv
        loss = 0.5 * jnp.sum(err * err) / d
        dx, dw = vjp(err / d)
        return [dx], [jnp.full((SUBLANES, LANES), loss, F32), dw], []

    row = lambda g, t: (t, 0)
    fixed = lambda g, t: (0, 0)
    dx, loss, dw = seq_call(name, fn, (1, cfg.S // ts),
                            [(x, (ts, d), row), (w, (1, d), fixed), (target, (ts, d), row)],
                            [((cfg.S, d), F32, (ts, d), row)],
                            accs=[((SUBLANES, LANES), (SUBLANES, LANES), fixed), ((1, d), (1, d), fixed)])
    return loss, dx, dw


FFN_ROWS = 1024


def ffn_gate_up(h, w_gate, w_up, name):
    s, d = h.shape
    c = w_gate.shape[1]
    tm = _tile(s, FFN_ROWS)

    def body(h_ref, wg_ref, wu_ref, g_ref, u_ref, a_ref):
        hv = h_ref[...]
        g = _dg(hv, wg_ref[...], 1, 1)
        u = _dg(hv, wu_ref[...], 1, 1)
        g_ref[...] = g
        u_ref[...] = u
        a_ref[...] = (silu(g) * u).astype(a_ref.dtype)

    w_spec = pl.BlockSpec((None, c, d), lambda i, j: (j, 0, 0))
    o_spec = pl.BlockSpec((None, tm, c), lambda i, j: (j, i, 0))
    shp = (N_DEV, s, c)
    return pl.pallas_call(
        body, name=name, grid=(s // tm, N_DEV), in_specs=[pl.BlockSpec((tm, d), lambda i, j: (i, 0)), w_spec, w_spec],
        out_specs=[o_spec] * 3,
        out_shape=[jax.ShapeDtypeStruct(shp, F32), jax.ShapeDtypeStruct(shp, F32), jax.ShapeDtypeStruct(shp, BF16)],
        compiler_params=_cparams(2))(h, w_gate, w_up)


def ffn_dgate_dup(dx, w_down, gate, up, after, name):
    s, d = dx.shape
    c = w_down.shape[1]
    tm = _tile(s, FFN_ROWS)
    extra = [] if after is None else [after]

    def body(dx_ref, wd_ref, g_ref, u_ref, *rest):
        dg_ref, du_ref = rest[len(extra):]
        dact = _dg(dx_ref[...], wd_ref[...], 1, 1)
        _, vjp = jax.vjp(lambda a, b: silu(a) * b, g_ref[...], u_ref[...])
        dg, du = vjp(dact)
        dg_ref[...] = dg.astype(dg_ref.dtype)
        du_ref[...] = du.astype(du_ref.dtype)

    blk = pl.BlockSpec((None, tm, c), lambda i, j: (j, i, 0))
    shp = jax.ShapeDtypeStruct((N_DEV, s, c), BF16)
    return pl.pallas_call(
        body, name=name, grid=(s // tm, N_DEV),
        in_specs=[pl.BlockSpec((tm, d), lambda i, j: (i, 0)), pl.BlockSpec((None, c, d), lambda i, j: (j, 0, 0)), blk, blk]
        + [pl.BlockSpec(memory_space=pl.ANY)] * len(extra),
        out_specs=[blk, blk], out_shape=[shp, shp], compiler_params=_cparams(2))(dx, w_down, gate, up, *extra)


CONV_COLS = (512, 256)
HALO = SUBLANES


def _conv_cols(col0, width):
    return next(c for c in CONV_COLS if col0 % c == 0 and width % c == 0)


def _shift_down(x, halo, j):
    if j == 0:
        return x
    r8 = _rows_iota(halo.shape)
    top = jnp.where(r8 >= j, pltpu.roll(x[:HALO], j, 0), pltpu.roll(halo, j, 0))
    return jnp.concatenate([top, pltpu.roll(x, j, 0)[HALO:]], axis=0)


def _shift_up(x, halo, j):
    if j == 0:
        return x
    n = x.shape[0]
    r8 = _rows_iota(halo.shape)
    bot = jnp.where(r8 < HALO - j, pltpu.roll(x[n - HALO:], HALO - j, 0), pltpu.roll(halo, HALO - j, 0))
    return jnp.concatenate([pltpu.roll(x, n - j, 0)[:n - HALO], bot], axis=0)


def _conv_tile(cfg):
    return min(cfg.S, 1024)


def conv_fwd(cfg, src, col0, width, w, b, name):
    tt, cb = _conv_tile(cfg), _conv_cols(col0, width)
    c0, hb = col0 // cb, tt // HALO
    nt = cfg.S // tt

    def body(x_ref, h_ref, w_ref, b_ref, o_ref):
        t = pl.program_id(1)
        x = x_ref[...]
        halo = jnp.where(t > 0, h_ref[...], 0.0)
        wv = w_ref[...]
        acc = b_ref[...] + wv[CONV_WIDTH - 1:CONV_WIDTH] * x
        for j in range(1, CONV_WIDTH):
            acc = acc + wv[CONV_WIDTH - 1 - j:CONV_WIDTH - j] * _shift_down(x, halo, j)
        o_ref[...] = acc

    return pl.pallas_call(
        body, name=name, grid=(width // cb, nt),
        in_specs=[pl.BlockSpec((tt, cb), lambda c, t: (t, c0 + c)),
                  pl.BlockSpec((HALO, cb), lambda c, t: (jnp.maximum(t * hb - 1, 0), c0 + c)),
                  pl.BlockSpec((CONV_WIDTH, cb), lambda c, t: (0, c)),
                  pl.BlockSpec((1, cb), lambda c, t: (0, c))],
        out_specs=pl.BlockSpec((tt, cb), lambda c, t: (t, c)),
        out_shape=jax.ShapeDtypeStruct((cfg.S, width), F32), compiler_params=_cparams(2),
    )(src, src, w, b)


def conv_bwd(cfg, src, col0, width, w, dy, name, into=None, into_col0=0):
    tt, cb = _conv_tile(cfg), _conv_cols(col0, width)
    c0, hb = col0 // cb, tt // HALO
    nt = cfg.S // tt
    extra = [] if into is None else [into]
    assert into_col0 % cb == 0
    o0 = into_col0 // cb

    def body(x_ref, h_ref, w_ref, dy_ref, dh_ref, *rest):
        dx_ref, dw_ref, db_ref = rest[len(extra):]
        t = pl.program_id(1)

        @pl.when(t == 0)
        def _():
            dw_ref[...] = jnp.zeros_like(dw_ref)
            db_ref[...] = jnp.zeros_like(db_ref)

        x = x_ref[...]
        halo = jnp.where(t > 0, h_ref[...], 0.0)
        dy = dy_ref[...]
        dhalo = jnp.where(t < nt - 1, dh_ref[...], 0.0)
        wv = w_ref[...]
        dx = wv[CONV_WIDTH - 1:CONV_WIDTH] * dy
        rows = [jnp.sum(dy * x, axis=0, keepdims=True)]
        for j in range(1, CONV_WIDTH):
            dx = dx + wv[CONV_WIDTH - 1 - j:CONV_WIDTH - j] * _shift_up(dy, dhalo, j)
            rows.insert(0, jnp.sum(dy * _shift_down(x, halo, j), axis=0, keepdims=True))
        dx_ref[...] = dx.astype(dx_ref.dtype)
        dw_ref[...] += jnp.concatenate(rows, axis=0)
        db_ref[...] += jnp.sum(dy, axis=0, keepdims=True)

    return pl.pallas_call(
        body, name=name, grid=(width // cb, nt),
        in_specs=[pl.BlockSpec((tt, cb), lambda c, t: (t, c0 + c)),
                  pl.BlockSpec((HALO, cb), lambda c, t: (jnp.maximum(t * hb - 1, 0), c0 + c)),
                  pl.BlockSpec((CONV_WIDTH, cb), lambda c, t: (0, c)),
                  pl.BlockSpec((tt, cb), lambda c, t: (t, c)),
                  pl.BlockSpec((HALO, cb), lambda c, t: (jnp.minimum((t + 1) * hb, nt * hb - 1), c))]
        + [pl.BlockSpec(memory_space=pl.ANY)] * len(extra),
        out_specs=[pl.BlockSpec((tt, cb), lambda c, t: (t, o0 + c)),
                   pl.BlockSpec((CONV_WIDTH, cb), lambda c, t: (0, c)),
                   pl.BlockSpec((1, cb), lambda c, t: (0, c))],
        out_shape=[jax.ShapeDtypeStruct((cfg.S, width) if into is None else into.shape, BF16),
                   jax.ShapeDtypeStruct((CONV_WIDTH, width), F32), jax.ShapeDtypeStruct((1, width), F32)],
        input_output_aliases={5: 0} if extra else {}, compiler_params=_cparams(2),
    )(src, src, w, dy, dy, *extra)


def write_cols(cfg, into, piece, col0, name):
    tt, width = _conv_tile(cfg), piece.shape[1]
    assert col0 % width == 0 and into.dtype == piece.dtype

    def body(p_ref, _, o_ref):
        o_ref[...] = p_ref[...]

    return pl.pallas_call(
        body, name=name, grid=(cfg.S // tt,),
        in_specs=[pl.BlockSpec((tt, width), lambda t: (t, 0)), pl.BlockSpec(memory_space=pl.ANY)],
        out_specs=pl.BlockSpec((tt, width), lambda t: (t, col0 // width)),
        out_shape=jax.ShapeDtypeStruct(into.shape, into.dtype), input_output_aliases={1: 0},
        compiler_params=_cparams(1))(piece, into)


def _gla_core(gh, q, k, v, g, glr, wg, bg, wn, st):
    n = glr.shape[0]
    causal = _lower_tri(n)
    outs, new = [], []
    for h in range(gh):
        log_a = log_sigmoid(bdot(glr, wg[h], 1, 0) + bg[h]) * (1.0 / GATE_NORM)
        bcum = cumsum_rows(log_a)
        b_last, b_mid = _row(bcum, n - 1), _row(bcum, n // 2)
        qs = q[h] * (HEAD_K ** -0.5)
        scores = jnp.where(causal, bdot(qs * jnp.exp(bcum - b_mid), k[h] * jnp.exp(b_mid - bcum), 1, 1), 0.0)
        o = bdot(scores, v[h], 1, 0) + bdot(qs * jnp.exp(bcum), st[h], 1, 1)
        new.append(st[h] * jnp.exp(b_last) + bdot(v[h], k[h] * jnp.exp(b_last - bcum), 0, 0))
        outs.append(rms(o, wn) * silu(g[h]))
    return jnp.concatenate(outs, axis=1), new


def _gla_ins(cfg, p0, wg, bg, wn, tmap):
    gh = cfg.GH
    ins = []
    for h in range(gh):
        ins.append((p0, (CHUNK, HEAD_K), lambda g, t, h=h: (tmap(t), h)))
    for h in range(gh):
        ins.append((p0, (CHUNK, HEAD_K), lambda g, t, h=h: (tmap(t), gh + h)))
    for h in range(gh):
        ins.append((p0, (CHUNK, HEAD_V), lambda g, t, h=h: (tmap(t), gh + h)))
    for h in range(gh):
        ins.append((p0, (CHUNK, HEAD_V), lambda g, t, h=h: (tmap(t), 2 * gh + h)))
    ins.append((p0, (CHUNK, LANES), lambda g, t: (tmap(t), 10 * gh)))
    for h in range(gh):
        ins.append((wg, (LANES, HEAD_K), lambda g, t, h=h: (0, h)))
    for h in range(gh):
        ins.append((bg, (1, HEAD_K), lambda g, t, h=h: (0, h)))
    ins.append((wn, (1, HEAD_V), lambda g, t: (0, 0)))
    return ins


def _gla_unpack(gh, vals):
    q, k, v, g = (vals[i * gh:(i + 1) * gh] for i in range(4))
    glr = vals[4 * gh]
    wg = vals[4 * gh + 1:5 * gh + 1]
    bg = vals[5 * gh + 1:6 * gh + 1]
    wn = vals[6 * gh + 1]
    return q, k, v, g, glr, wg, bg, wn, vals[6 * gh + 2:]


def gla_fwd(cfg, p0, wg, bg, wn, name):
    gh, nc = cfg.GH, cfg.S // CHUNK

    def fn(ins, st):
        q, k, v, g, glr, wgv, bgv, wnv, _ = _gla_unpack(gh, ins)
        out, new = _gla_core(gh, q, k, v, g, glr, wgv, bgv, wnv, st)
        return [out, jnp.stack(st)], [], new

    return seq_call(name, fn, (1, nc), _gla_ins(cfg, p0, wg, bg, wn, lambda t: t),
                    [((cfg.S, cfg.DV), BF16, (CHUNK, cfg.DV), lambda g, t: (t, 0)),
                     ((nc, gh, HEAD_V, HEAD_K), F32, (None, gh, HEAD_V, HEAD_K), lambda g, t: (t, 0, 0, 0))],
                    carries=[(HEAD_V, HEAD_K)] * gh)


def gla_bwd(cfg, p0, wg, bg, wn, states, dout, name):
    gh, nc = cfg.GH, cfg.S // CHUNK
    rev = lambda t: nc - 1 - t

    def fn(ins, dst):
        q, k, v, g, glr, wgv, bgv, wnv, rest = _gla_unpack(gh, ins)
        st_all, do = rest
        st = [st_all[h] for h in range(gh)]
        _, vjp = jax.vjp(functools.partial(_gla_core, gh), q, k, v, g, glr, wgv, bgv, wnv, st)
        dq, dk, dv, dg, dglr, dwg, dbg, dwn, dstate = vjp((do.astype(F32), list(dst)))
        return ([jnp.concatenate(list(dq) + list(dk) + list(dv) + list(dg), axis=1), dglr],
                [jnp.concatenate(dwg, axis=1), jnp.concatenate(dbg, axis=1), dwn], dstate)

    ins = _gla_ins(cfg, p0, wg, bg, wn, rev)
    ins.append((states, (None, gh, HEAD_V, HEAD_K), lambda g, t: (rev(t), 0, 0, 0)))
    ins.append((dout, (CHUNK, cfg.DV), lambda g, t: (rev(t), 0)))
    wide = 2 * cfg.DK + 2 * cfg.DV
    fixed = lambda g, t: (0, 0)
    return seq_call(name, fn, (1, nc), ins,
                    [((cfg.S, wide), BF16, (CHUNK, wide), lambda g, t: (rev(t), 0)),
                     ((cfg.S, LANES), BF16, (CHUNK, LANES), lambda g, t: (rev(t), 0))],
                    accs=[((LANES, cfg.DK), (LANES, cfg.DK), fixed), ((1, cfg.DK), (1, cfg.DK), fixed),
                          ((1, HEAD_V), (1, HEAD_V), fixed)],
                    carries=[(HEAD_V, HEAD_K)] * gh)


def _lru_core(xc, gate, wa, wi, ba, bi, lam, h_in):
    r = jax.nn.sigmoid(bdot(xc, wa, 1, 0) + ba)
    i = jax.nn.sigmoid(bdot(xc, wi, 1, 0) + bi)
    log_a = LRU_C * r * log_sigmoid(lam)
    a = jnp.exp(log_a)
    u = jnp.sqrt(-_expm1(2.0 * log_a)) * (i * xc)
    first = _rows_iota(a.shape) == 0
    h = lin_scan(a, u + jnp.where(first, a * h_in, 0.0))
    return h * gelu_tanh(gate), _row(h, a.shape[0] - 1)


def _lru_tile(cfg):
    return min(cfg.S, 512)


def _lru_ins(cfg, xc, p0, wa, wi, ba, bi, lam, tmap):
    tt, gh = _lru_tile(cfg), cfg.GH
    vec = lambda g, t: (0, g)
    return [(xc, (tt, LRU_BLOCK), lambda g, t: (tmap(t), g)),
            (p0, (tt, LRU_BLOCK), lambda g, t: (tmap(t), 8 * gh + g)),
            (wa, (None, LRU_BLOCK, LRU_BLOCK), lambda g, t: (g, 0, 0)),
            (wi, (None, LRU_BLOCK, LRU_BLOCK), lambda g, t: (g, 0, 0)),
            (ba, (1, LRU_BLOCK), vec), (bi, (1, LRU_BLOCK), vec), (lam, (1, LRU_BLOCK), vec)]


def lru_fwd(cfg, xc, p0, wa, wi, ba, bi, lam, name):
    tt, nb = _lru_tile(cfg), cfg.NB
    nt = cfg.S // tt

    def fn(ins, c):
        out, h_last = _lru_core(*ins, c[0])
        return [out, c[0]], [], [h_last]

    return seq_call(name, fn, (nb, nt), _lru_ins(cfg, xc, p0, wa, wi, ba, bi, lam, lambda t: t),
                    [((cfg.S, cfg.W), BF16, (tt, LRU_BLOCK), lambda g, t: (t, g)),
                     ((nb, nt, 1, LRU_BLOCK), F32, (None, None, 1, LRU_BLOCK), lambda g, t: (g, t, 0, 0))],
                    carries=[(1, LRU_BLOCK)])


def lru_bwd(cfg, xc, p0, wa, wi, ba, bi, lam, states, dout, name):
    tt, nb = _lru_tile(cfg), cfg.NB
    nt = cfg.S // tt
    rev = lambda t: nt - 1 - t

    def fn(ins, c):
        *fwd_ins, h_in, do = ins
        _, vjp = jax.vjp(_lru_core, *fwd_ins, h_in)
        dxc, dgate, dwa, dwi, dba, dbi, dlam, dh = vjp((do.astype(F32), c[0]))
        return [dxc, dgate], [dwa, dwi, dba, dbi, dlam], [dh]

    ins = _lru_ins(cfg, xc, p0, wa, wi, ba, bi, lam, rev)
    ins.append((states, (None, None, 1, LRU_BLOCK), lambda g, t: (g, rev(t), 0, 0)))
    ins.append((dout, (tt, LRU_BLOCK), lambda g, t: (rev(t), g)))
    mat = ((nb, LRU_BLOCK, LRU_BLOCK), (None, LRU_BLOCK, LRU_BLOCK), lambda g, t: (g, 0, 0))
    vec = ((1, cfg.W), (1, LRU_BLOCK), lambda g, t: (0, g))
    return seq_call(name, fn, (nb, nt), ins,
                    [((cfg.S, cfg.W), F32, (tt, LRU_BLOCK), lambda g, t: (rev(t), g)),
                     ((cfg.S, cfg.W), BF16, (tt, LRU_BLOCK), lambda g, t: (rev(t), g))],
                    accs=[mat, mat, vec, vec, vec], carries=[(1, LRU_BLOCK)])


def _ssd_core(xc, bc, cc, z, dt_raw, dt_bias, a_log, d_skip, gn, st):
    n = xc.shape[0]
    x, bm, cm = silu(xc), silu(bc), silu(cc)
    dt = softplus(dt_raw + dt_bias)
    acs = cumsum_rows_mxu(dt * (-jnp.exp(a_log)))
    acs_t = acs.T
    acs_e, dt_e = _expand_heads(acs), _expand_heads(dt)
    last_e = _expand_heads(_row(acs, n - 1))
    causal = _lower_tri(n)
    cb = bdot(cm, bm, 1, 1)
    xdt = x * dt_e
    y_diag = []
    for h, xh in enumerate(_split_heads(xdt)):
        seg = acs[:, h:h + 1] - acs_t[h:h + 1, :]
        decay = jnp.where(causal, jnp.exp(jnp.minimum(seg, 0.0)), 0.0)
        y_diag.append(bdot(cb * decay, xh, 1, 0))
    y = jnp.concatenate(y_diag, axis=1) + bdot(cm, st, 1, 0) * jnp.exp(acs_e)
    new = st * jnp.exp(last_e) + bdot(bm, xdt * jnp.exp(last_e - acs_e), 0, 0)
    y = (y + _expand_heads(d_skip) * x) * silu(z)
    return rms(y, gn), new


SSD_TILED = 5


SSD_FWD_CHUNKS = 8
SSD_BWD_CHUNKS = 1


def _ssd_tile(cfg, chunks):
    return min(cfg.S, chunks * CHUNK)


def _chunk_rows(v, s):
    return v[s * CHUNK:(s + 1) * CHUNK]


def _ssd_ins(cfg, tt, xc, p1, dt_raw, dt_bias, a_log, d_skip, gn, tmap):
    ng = cfg.NG
    vec = lambda g, t: (g, 0, 0)
    return [(xc, (tt, SSD_GW), lambda g, t: (tmap(t), g)),
            (xc, (tt, SSD_N), lambda g, t: (tmap(t), 4 * ng + g)),
            (xc, (tt, SSD_N), lambda g, t: (tmap(t), 5 * ng + g)),
            (p1, (tt, SSD_GW), lambda g, t: (tmap(t), g)),
            (dt_raw, (None, tt, LANES), lambda g, t: (g, tmap(t), 0)),
            (dt_bias, (None, 1, LANES), vec), (a_log, (None, 1, LANES), vec), (d_skip, (None, 1, LANES), vec),
            (gn, (1, SSD_GW), lambda g, t: (0, g))]


def ssd_fwd(cfg, xc, p1, dt_raw, dt_bias, a_log, d_skip, gn, name):
    ng, nc, tt = cfg.NG, cfg.S // CHUNK, _ssd_tile(cfg, SSD_FWD_CHUNKS)
    nsub = tt // CHUNK

    def fn(ins, c):
        tiled, params = ins[:SSD_TILED], ins[SSD_TILED:]
        st, outs, entered = c[0], [], []
        for s in range(nsub):
            entered.append(st)
            out, st = _ssd_core(*[_chunk_rows(v, s) for v in tiled], *params, st)
            outs.append(out)
        return [jnp.concatenate(outs, axis=0), jnp.stack(entered)], [], [st]

    return seq_call(name, fn, (ng, cfg.S // tt), _ssd_ins(cfg, tt, xc, p1, dt_raw, dt_bias, a_log, d_skip, gn, lambda t: t),
                    [((cfg.S, cfg.DI), BF16, (tt, SSD_GW), lambda g, t: (t, g)),
                     ((ng, nc, SSD_N, SSD_GW), F32, (None, nsub, SSD_N, SSD_GW), lambda g, t: (g, t, 0, 0))],
                    carries=[(SSD_N, SSD_GW)])


def ssd_bwd(cfg, xc, p1, dt_raw, dt_bias, a_log, d_skip, gn, states, dout, name):
    ng, tt = cfg.NG, _ssd_tile(cfg, SSD_BWD_CHUNKS)
    nsub, nt = tt // CHUNK, cfg.S // tt
    rev = lambda t: nt - 1 - t

    def fn(ins, c):
        tiled, params = ins[:SSD_TILED], ins[SSD_TILED:SSD_TILED + 4]
        st_all, do = ins[SSD_TILED + 4:]
        dst, pieces, acc = c[0], [None] * nsub, None
        for s in reversed(range(nsub)):
            _, vjp = jax.vjp(_ssd_core, *[_chunk_rows(v, s) for v in tiled], *params, st_all[s])
            grads = vjp((_chunk_rows(do, s).astype(F32), dst))
            pieces[s], dparams, dst = grads[:SSD_TILED], grads[SSD_TILED:SSD_TILED + 4], grads[SSD_TILED + 4]
            acc = dparams if acc is None else [x + y for x, y in zip(acc, dparams)]
        return [jnp.concatenate([p[i] for p in pieces], axis=0) for i in range(SSD_TILED)], list(acc), [dst]

    ins = _ssd_ins(cfg, tt, xc, p1, dt_raw, dt_bias, a_log, d_skip, gn, rev)
    ins.append((states, (None, nsub, SSD_N, SSD_GW), lambda g, t: (g, rev(t), 0, 0)))
    ins.append((dout, (tt, SSD_GW), lambda g, t: (rev(t), g)))
    col = lambda g, t: (rev(t), g)
    vec = ((ng, 1, LANES), (None, 1, LANES), lambda g, t: (g, 0, 0))
    return seq_call(name, fn, (ng, nt), ins,
                    [((cfg.S, cfg.DI), F32, (tt, SSD_GW), col),
                     ((cfg.S, ng * SSD_N), F32, (tt, SSD_N), col),
                     ((cfg.S, ng * SSD_N), F32, (tt, SSD_N), col),
                     ((cfg.S, cfg.OP), BF16, (tt, SSD_GW), col),
                     ((ng, cfg.S, LANES), F32, (None, tt, LANES), lambda g, t: (g, rev(t), 0))],
                    accs=[vec, vec, vec, ((1, cfg.DI), (1, SSD_GW), lambda g, t: (0, g))],
                    carries=[(SSD_N, SSD_GW)])


PACK_ALIGN = SUBLANES * LANES
PACK_ROWS = 256


def pack(arrays):
    pieces = []
    for a in arrays:
        flat = a.reshape(-1).astype(F32)
        pad = _round_up(flat.shape[0], PACK_ALIGN) - flat.shape[0]
        pieces.append(jnp.pad(flat, (0, pad)) if pad else flat)
    flat = jnp.concatenate(pieces)
    pad = _round_up(flat.shape[0], PACK_ROWS * LANES) - flat.shape[0]
    return jnp.pad(flat, (0, pad)).reshape(-1, LANES)


def unpack(buf, shapes):
    lead = buf.shape[:-2]
    flat = buf.reshape(lead + (-1,))
    out, off = [], 0
    for s in shapes:
        n = math.prod(s)
        out.append(flat[..., off:off + n].reshape(lead + tuple(s)))
        off += _round_up(n, PACK_ALIGN)
    return out


def _slots_to_cols(slots):
    return slots.transpose(1, 0, 2).reshape(slots.shape[1], -1)


def _even_in_padded(cfg, w):
    main = 2 * cfg.DK + 2 * cfg.DV
    return jnp.concatenate([w[:, :main], w[:, main + GATE_RANK:], w[:, main:main + GATE_RANK],
                            jnp.zeros((w.shape[0], cfg.EP - cfg.EVEN_IN), w.dtype)], axis=1)


def _even_in_unpadded(cfg, wp):
    main = 2 * cfg.DK + 2 * cfg.DV
    rest = main + 2 * cfg.W
    return jnp.concatenate([wp[:, :main], wp[:, rest:rest + GATE_RANK], wp[:, main:rest]], axis=1)


def _odd_in_padded(cfg, w):
    return jnp.concatenate([w, jnp.zeros((w.shape[0], cfg.OP - cfg.ODD_IN), w.dtype)], axis=1)


def _odd_in_unpadded(cfg, wp):
    return wp[:, :cfg.ODD_IN]


def _group_lanes(cfg, v):
    lead = v.shape[:-1]
    g = jnp.moveaxis(v.reshape(lead + (cfg.NG, SSD_HG)), -2, 0)
    return jnp.pad(g, [(0, 0)] * (g.ndim - 1) + [(0, LANES - SSD_HG)])


def _ungroup_lanes(cfg, g):
    v = jnp.moveaxis(g[..., :SSD_HG], 0, -2)
    return v.reshape(v.shape[:-2] + (cfg.NH,))


def train_step(cfg, p, loss_target):
    S, D = cfg.S, cfg.D
    me = 4 * lax.axis_index("x") + 2 * lax.axis_index("y") + lax.axis_index("c")

    big = ["ev_w_in", "ev_w_out", "od_w_in", "od_w_out", "ffn_w_gate", "ffn_w_up", "ffn_w_down"]
    small_sharded = ["ev_gla_w_gate", "ev_lru_conv_w", "od_norm", "od_conv_w", "od_conv_b", "od_gnorm"]
    replicated = ["ev_norm", "ev_gla_b_gate", "ev_gla_w_onorm", "ev_lru_conv_b", "ev_lru_w_a", "ev_lru_b_a",
                  "ev_lru_w_i", "ev_lru_b_i", "ev_lru_lam", "od_dt_bias", "od_a_log", "od_d_skip", "ffn_norm",
                  "final_norm"]

    transposed = ("ffn_w_gate", "ffn_w_up")
    view = lambda n, a: jnp.swapaxes(a, 1, 2) if n in transposed else a
    wb = {n: view(n, p[n]).astype(BF16) for n in big}
    ffn_items = lambda l: [(wb["ffn_w_gate"], l), (wb["ffn_w_up"], l), (wb["ffn_w_down"], l)]
    ss_shapes = [p[n].shape for n in small_sharded]
    groups = [[(pack([p[n] for n in small_sharded]), None), (wb["ev_w_in"], 0), (wb["ev_w_out"], 0)], ffn_items(0),
              [(wb["od_w_in"], 0), (wb["od_w_out"], 0)], ffn_items(1)]
    gathers, tokens = [], []
    for i, g in enumerate(groups):
        lands = [_landing_zone(a, lead, me, f"gather_own{i}_{j}") for j, (a, lead) in enumerate(g)]
        handle, token = exchange_start(g, True, lands, f"gather_start{i}", NEAR_PEERS)
        gathers.append(handle)
        tokens.append(token)
    all_started = tokens[0][:1, :1] + tokens[1][:1, :1] + tokens[2][:1, :1] + tokens[3][:1, :1]

    def gathered(i, after, work):
        return relay(exchange_wait(gathers[i], after, f"gather_wait{i}")[1], f"gather_relay{i}", work)

    x0 = p["x"][0]
    (ss_all, gw_ev_in, gw_ev_out), h0 = gathered(
        0, all_started, lambda token: norm_fwd(cfg, x0, p["ev_norm"] + token[:1, :1], "ev_norm"))
    gs = dict(zip(small_sharded, unpack(ss_all, ss_shapes)))
    w_ev_in = cols_from_slots(gw_ev_in, functools.partial(_even_in_padded, cfg), cfg.EP, "ev_w_in_cols")
    w_ev_out = gw_ev_out.reshape(D, D)

    gla_wg = jnp.pad(_slots_to_cols(gs["ev_gla_w_gate"][:, 0]), ((0, LANES - GATE_RANK), (0, 0)))
    lru_cw = _slots_to_cols(gs["ev_lru_conv_w"][:, 0])
    od_norm = gs["od_norm"].transpose(1, 0, 2).reshape(1, D)
    od_cw = _slots_to_cols(gs["od_conv_w"][:, 0])
    od_cb = gs["od_conv_b"].transpose(1, 0, 2).reshape(1, cfg.CD)
    od_gn = gs["od_gnorm"].transpose(1, 0, 2).reshape(1, cfg.DI)

    target = loss_target[0]
    ev_norm = p["ev_norm"]
    bg = p["ev_gla_b_gate"]
    wn = p["ev_gla_w_onorm"]
    lru_cb = p["ev_lru_conv_b"]
    wa, wi = p["ev_lru_w_a"][0], p["ev_lru_w_i"][0]
    ba, bi, lam = p["ev_lru_b_a"], p["ev_lru_b_i"], p["ev_lru_lam"]
    dt_bias, a_log, d_skip = (_group_lanes(cfg, p[n]) for n in ("od_dt_bias", "od_a_log", "od_d_skip"))
    ffn_norm = [p["ffn_norm"][l:l + 1] for l in range(2)]
    final_norm = p["final_norm"].reshape(1, D)

    def ffn_forward(l, x, weights, next_group):
        w_gate, w_up, w_down = weights
        h = norm_fwd(cfg, x, ffn_norm[l], f"ffn{l}_norm")
        gate, up, act = ffn_gate_up(h, w_gate, w_up, f"ffn{l}_gate_up")
        down = lambda token: matmul(act, w_down, a_slot="k", b_slot="k", res=x, after=token, name=f"ffn{l}_down", tn=512)
        lands, out = (None, down(None)) if next_group is None else gathered(next_group, act, down)
        return out, (h, gate, up, act, w_gate, w_up, w_down), lands

    p0 = matmul(h0, w_ev_in, name="ev_in", tn=768)
    gla_out, gla_states = gla_fwd(cfg, p0, gla_wg, bg, wn, "gla_fwd")
    lru_col = 2 * cfg.DK + 2 * cfg.DV
    lru_xc = conv_fwd(cfg, p0, lru_col, cfg.W, lru_cw, lru_cb, "lru_conv")
    lru_out, lru_states = lru_fwd(cfg, lru_xc, p0, wa, wi, ba, bi, lam, "lru_fwd")
    mix = jnp.concatenate([gla_out, lru_out], axis=1)
    ffn0_weights, x1 = gathered(1, mix, lambda token: matmul(mix, w_ev_out, res=x0, after=token, name="ev_out"))
    x2, ffn0_saved, (gw_od_in, gw_od_out) = ffn_forward(0, x1, ffn0_weights, 2)

    w_od_in = cols_from_slots(gw_od_in, functools.partial(_odd_in_padded, cfg), cfg.OP, "od_w_in_cols")
    w_od_out = gw_od_out.reshape(cfg.DI, D)
    h2 = norm_fwd(cfg, x2, od_norm, "od_norm")
    p1 = matmul(h2, w_od_in, name="od_in", tn=768)
    od_xc = conv_fwd(cfg, p1, cfg.DI, cfg.CD, od_cw, od_cb, "od_conv")
    dt_col = cfg.DI + cfg.CD
    dt_raw = _group_lanes(cfg, p1[:, dt_col:dt_col + cfg.NH])
    ssd_out, ssd_states = ssd_fwd(cfg, od_xc, p1, dt_raw, dt_bias, a_log, d_skip, od_gn, "ssd_fwd")
    ffn1_weights, x3 = gathered(3, ssd_out, lambda token: matmul(ssd_out, w_od_out, res=x2, after=token, name="od_out"))
    x4, ffn1_saved, _ = ffn_forward(1, x3, ffn1_weights, None)

    loss_part, dx4, d_final_norm = head_fwd_bwd(cfg, x4, final_norm, target, "head")
    loss = lax.psum(loss_part[0, 0], MESH_AXES)

    def ffn_backward(l, x, saved, dx_out, after):
        h, gate, up, act, w_gate, w_up, w_down = saved
        dgate, dup = ffn_dgate_dup(dx_out, w_down, gate, up, after, f"ffn{l}_dgate_dup")
        d_down = matmul(act, dx_out, ta=True, a_slot="m", out_dtype=BF16, name=f"ffn{l}_dwdown")
        sent_down, token = start_grads([d_down], f"grads_start_ffn{l}_down")
        d_gate = matmul(dgate, h, ta=True, a_slot="m", after=token, out_dtype=BF16, name=f"ffn{l}_dwgate")
        d_up = matmul(dup, h, ta=True, a_slot="m", out_dtype=BF16, name=f"ffn{l}_dwup")
        sent_gate_up, token = start_grads([d_gate, d_up], f"grads_start_ffn{l}")
        dh = matmul(dgate, w_gate, a_slot="k", b_slot="k", after=token, name=f"ffn{l}_dh_gate", tn=512)
        dh = matmul(dup, w_up, a_slot="k", b_slot="k", res=dh, name=f"ffn{l}_dh_up", tn=512)
        dx, dnorm = norm_bwd(cfg, x, ffn_norm[l], dh, dx_out, f"ffn{l}_norm_bwd")
        return dx, dnorm, (sent_down, sent_gate_up)

    def start_grads(arrays, name):
        return exchange_start([(a, None) for a in arrays], False, [lax.empty(a.shape, a.dtype) for a in arrays], name)

    dx3, d_ffn_norm1, sent_ffn1 = ffn_backward(1, x3, ffn1_saved, dx4, None)

    d_ssd_out = matmul(dx3, w_od_out, tb=True, name="od_dmix")
    d_od_out = matmul(ssd_out, dx3, ta=True, out_dtype=BF16, name="od_dwout")
    dxs, dbm, dcm, dz, d_dt_raw, d_dt_bias, d_a_log, d_d_skip, d_od_gn = ssd_bwd(
        cfg, od_xc, p1, dt_raw, dt_bias, a_log, d_skip, od_gn, ssd_states, d_ssd_out, "ssd_bwd")
    dp1, conv_parts, col = dz, [], 0
    for part, dy in (("x", dxs), ("b", dbm), ("c", dcm)):
        width = dy.shape[1]
        dp1, dcw, dcb = conv_bwd(cfg, p1, cfg.DI + col, width, od_cw[:, col:col + width], dy, "od_conv_bwd_" + part,
                                 into=dp1, into_col0=cfg.DI + col)
        conv_parts.append((dcw, dcb))
        col += width
    d_od_cw = jnp.concatenate([c[0] for c in conv_parts], axis=1)
    d_od_cb = jnp.concatenate([c[1] for c in conv_parts], axis=1)
    d_dt = _ungroup_lanes(cfg, d_dt_raw).astype(BF16)
    tail = jnp.concatenate([d_dt, jnp.zeros((S, cfg.OP - cfg.ODD_IN), BF16)], axis=1)
    dp1 = write_cols(cfg, dp1, tail, cfg.DI + cfg.CD, "od_dt_cols")
    dh2 = matmul(dp1, w_od_in, tb=True, name="od_dh", tk=1536)
    d_od_in = matmul(h2, dp1, ta=True, out_dtype=BF16, name="od_dwin", tn=768)
    dx2, d_od_norm = norm_bwd(cfg, x2, od_norm, dh2, dx3, "od_norm_bwd")
    d_od_in_slots = slots_from_cols(d_od_in, functools.partial(_odd_in_unpadded, cfg), p["od_w_in"].shape[2],
                                    "od_dwin_slots")
    sent_od, token = start_grads([d_od_in_slots, d_od_out.reshape((N_DEV,) + p["od_w_out"].shape[1:])], "grads_start_od")

    dx1, d_ffn_norm0, sent_ffn0 = ffn_backward(0, x1, ffn0_saved, dx2, token)

    d_ev_out = matmul(mix, dx1, ta=True, out_dtype=BF16, name="ev_dwout")
    sent_ev_out, token = start_grads([d_ev_out.reshape((N_DEV,) + p["ev_w_out"].shape[1:])], "grads_start_ev_out")
    d_mix = matmul(dx1, w_ev_out, tb=True, after=token, name="ev_dmix")
    d_qkvg, d_glr, d_gla_wg, d_bg, d_wn = gla_bwd(cfg, p0, gla_wg, bg, wn, gla_states, d_mix[:, :cfg.DV], "gla_bwd")
    d_lru_xc, d_gate_br, d_wa, d_wi, d_ba, d_bi, d_lam = lru_bwd(
        cfg, lru_xc, p0, wa, wi, ba, bi, lam, lru_states, d_mix[:, cfg.DV:], "lru_bwd")
    d_xbr, d_lru_cw, d_lru_cb = conv_bwd(cfg, p0, lru_col, cfg.W, lru_cw, d_lru_xc, "lru_conv_bwd")
    dp0 = jnp.concatenate([d_qkvg, d_xbr, d_gate_br, d_glr, jnp.zeros((S, cfg.EP - lru_col - 2 * cfg.W - LANES), BF16)],
                          axis=1)
    d_ev_in = matmul(h0, dp0, ta=True, out_dtype=BF16, name="ev_dwin", tn=768)
    d_ev_in_slots = slots_from_cols(d_ev_in, functools.partial(_even_in_unpadded, cfg), p["ev_w_in"].shape[2],
                                    "ev_dwin_slots")
    sent_ev_in, token = start_grads([d_ev_in_slots], "grads_start_ev_in")
    dh0 = matmul(dp0, w_ev_in, tb=True, after=token, name="ev_dh", tk=1792)
    grad_x, d_ev_norm = norm_bwd(cfg, x0, ev_norm, dh0, dx1, "ev_norm_bwd")

    out = {"loss": loss, "grad_x": grad_x[None]}

    def update(names, sent, after, wait_name):
        s, r = exchange_wait(sent[0], after, wait_name + "0")
        sends, recvs = [[a] for a in s], [[a] for a in r]
        for extra in sent[1:]:
            s, r = exchange_wait(extra, after, wait_name + "1")
            for i in range(len(names)):
                sends[i].append(s[i])
                recvs[i].append(r[i])
        for i, n in enumerate(names):
            flip = n in ("ev_w_in", "od_w_in")
            shard = lambda a: jnp.swapaxes(a, 1, 2) if flip else view(n, a)
            res = adamw_sharded(recvs[i], sends[i], me, shard(p[n]), shard(p["m_" + n]), shard(p["v_" + n]),
                                "adamw_" + n, transposed=flip)
            out["grad_" + n], out["delta_" + n], out["new_m_" + n], out["new_v_" + n] = (shard(a) for a in res)
        return res[-1]

    small_full = {
        "ev_gla_w_gate": d_gla_wg[:GATE_RANK][None], "ev_lru_conv_w": d_lru_cw[None], "od_norm": d_od_norm,
        "od_conv_w": d_od_cw[None], "od_conv_b": d_od_cb, "od_gnorm": d_od_gn,
        "ev_norm": d_ev_norm, "ev_gla_b_gate": d_bg, "ev_gla_w_onorm": d_wn, "ev_lru_conv_b": d_lru_cb,
        "ev_lru_w_a": d_wa[None], "ev_lru_b_a": d_ba, "ev_lru_w_i": d_wi[None], "ev_lru_b_i": d_bi,
        "ev_lru_lam": d_lam, "od_dt_bias": _ungroup_lanes(cfg, d_dt_bias), "od_a_log": _ungroup_lanes(cfg, d_a_log),
        "od_d_skip": _ungroup_lanes(cfg, d_d_skip), "ffn_norm": jnp.concatenate([d_ffn_norm0, d_ffn_norm1], axis=0),
        "final_norm": d_final_norm.reshape(D),
    }
    small = small_sharded + replicated
    small_packed = pack([small_full[n] for n in small])
    sent_small, token = exchange_start([(small_packed, None)], True,
                                       [_landing_zone(small_packed, None, me, "gather_small_grads_own")],
                                       "gather_small_grads")

    done = update(["od_w_in", "od_w_out"], [sent_od], token, "grads_wait_od")
    done = update(["ffn_w_down"], [sent_ffn0[0], sent_ffn1[0]], done, "grads_wait_ffn_down")
    done = update(["ffn_w_gate", "ffn_w_up"], [sent_ffn0[1], sent_ffn1[1]], done, "grads_wait_ffn")
    done = update(["ev_w_out"], [sent_ev_out], done, "grads_wait_ev_out")
    done = update(["ev_w_in"], [sent_ev_in], done, "grads_wait_ev_in")

    small_all = exchange_wait(sent_small, done, "gather_small_grads_wait")[1][0]
    g_small = dict(zip(small, unpack(reduce_slots(small_all, "sum_small_grads"), [small_full[n].shape for n in small])))
    for n in small_sharded:
        width = p[n].shape[-1]
        g_small[n] = lax.dynamic_slice_in_dim(g_small[n], me * width, width, axis=g_small[n].ndim - 1)
    shapes = [p[n].shape for n in small]
    g_buf = pack([g_small[n] for n in small])
    delta, new_m, new_v = adamw(pack([p[n] for n in small]), g_buf, pack([p["m_" + n] for n in small]),
                                pack([p["v_" + n] for n in small]), "adamw_small")
    for kind, buf in (("grad_", g_buf), ("delta_", delta), ("new_m_", new_m), ("new_v_", new_v)):
        for n, a in zip(small, unpack(buf, shapes)):
            out[kind + n] = a
    return out


WEIGHTS = ['ev_norm', 'ev_w_in', 'ev_gla_w_gate', 'ev_gla_b_gate', 'ev_gla_w_onorm', 'ev_lru_conv_w', 'ev_lru_conv_b',
           'ev_lru_w_a', 'ev_lru_b_a', 'ev_lru_w_i', 'ev_lru_b_i', 'ev_lru_lam', 'ev_w_out', 'od_norm', 'od_w_in',
           'od_conv_w', 'od_conv_b', 'od_dt_bias', 'od_a_log', 'od_d_skip', 'od_gnorm', 'od_w_out', 'ffn_norm',
           'ffn_w_gate', 'ffn_w_up', 'ffn_w_down', 'final_norm']


def kernel(x, ev_norm, ev_w_in, ev_gla_w_gate, ev_gla_b_gate, ev_gla_w_onorm, ev_lru_conv_w, ev_lru_conv_b, ev_lru_w_a, ev_lru_b_a, ev_lru_w_i, ev_lru_b_i, ev_lru_lam, ev_w_out, od_norm, od_w_in, od_conv_w, od_conv_b, od_dt_bias, od_a_log, od_d_skip, od_gnorm, od_w_out, ffn_norm, ffn_w_gate, ffn_w_up, ffn_w_down, final_norm, loss_target, m_ev_norm, m_ev_w_in, m_ev_gla_w_gate, m_ev_gla_b_gate, m_ev_gla_w_onorm, m_ev_lru_conv_w, m_ev_lru_conv_b, m_ev_lru_w_a, m_ev_lru_b_a, m_ev_lru_w_i, m_ev_lru_b_i, m_ev_lru_lam, m_ev_w_out, m_od_norm, m_od_w_in, m_od_conv_w, m_od_conv_b, m_od_dt_bias, m_od_a_log, m_od_d_skip, m_od_gnorm, m_od_w_out, m_ffn_norm, m_ffn_w_gate, m_ffn_w_up, m_ffn_w_down, m_final_norm, v_ev_norm, v_ev_w_in, v_ev_gla_w_gate, v_ev_gla_b_gate, v_ev_gla_w_onorm, v_ev_lru_conv_w, v_ev_lru_conv_b, v_ev_lru_w_a, v_ev_lru_b_a, v_ev_lru_w_i, v_ev_lru_b_i, v_ev_lru_lam, v_ev_w_out, v_od_norm, v_od_w_in, v_od_conv_w, v_od_conv_b, v_od_dt_bias, v_od_a_log, v_od_d_skip, v_od_gnorm, v_od_w_out, v_ffn_norm, v_ffn_w_gate, v_ffn_w_up, v_ffn_w_down, v_final_norm):
    args = dict(locals())
    p = {n: a for n, a in args.items() if n != "loss_target"}
    cfg = Cfg(S=x.shape[1], D=x.shape[2], DFF=ffn_w_gate.shape[2] * N_DEV)
    out = train_step(cfg, p, loss_target)
    return (out["loss"], out["grad_x"], *[out["grad_" + w] for w in WEIGHTS], *[out["delta_" + w] for w in WEIGHTS],
            *[out["new_m_" + w] for w in WEIGHTS], *[out["new_v_" + w] for w in WEIGHTS])
```

```python
import functools
import math
from typing import NamedTuple

import jax
import jax.numpy as jnp
from jax import lax
from jax.experimental import pallas as pl
from jax.experimental.pallas import tpu as pltpu

F32 = jnp.float32
BF16 = jnp.bfloat16
MESH_AXES = ("x", "y", "c")
N_DEV = 8
LANES = 128
SUBLANES = 8
VMEM_LIMIT = 56 * 1024 * 1024

NORM_EPS = 1e-6
CONV_WIDTH = 4
CHUNK = 64
HEAD_K = 128
HEAD_V = 256
GATE_RANK = 16
GATE_NORM = 16.0
LRU_BLOCK = 128
LRU_C = 8.0
SSD_P = 64
SSD_N = 128
SSD_HG = 8
SSD_GW = SSD_HG * SSD_P

ADAM_LR = 0.001
ADAM_B1 = 0.9
ADAM_B2 = 0.999
ADAM_EPS = 1e-08
ADAM_WD = 0.01
ADAM_STEP = 10


class Cfg(NamedTuple):
    S: int
    D: int
    DFF: int

    @property
    def GH(self):
        return self.D // 512

    @property
    def NB(self):
        return self.D // 256

    @property
    def NG(self):
        return self.D // 256

    @property
    def DK(self):
        return HEAD_K * self.GH

    @property
    def DV(self):
        return HEAD_V * self.GH

    @property
    def W(self):
        return LRU_BLOCK * self.NB

    @property
    def DI(self):
        return SSD_GW * self.NG

    @property
    def CD(self):
        return self.DI + 2 * self.NG * SSD_N

    @property
    def NH(self):
        return SSD_HG * self.NG

    @property
    def EVEN_IN(self):
        return 2 * self.DK + 2 * self.DV + GATE_RANK + 2 * self.W

    @property
    def ODD_IN(self):
        return self.DI + self.CD + self.NH

    @property
    def EP(self):
        return _round_up(2 * self.DK + 2 * self.DV + 2 * self.W + LANES, 768)

    @property
    def OP(self):
        return _round_up(self.DI + self.CD + LANES, 768)


def _round_up(n, m):
    return (n + m - 1) // m * m


def _tile(n, pref):
    if n <= pref:
        return n
    t = pref - pref % LANES
    while n % t:
        t -= LANES
    return t


def _cparams(n_axes):
    return pltpu.CompilerParams(dimension_semantics=("arbitrary",) * n_axes, vmem_limit_bytes=VMEM_LIMIT)


def _dg(a, b, ca, cb):
    return lax.dot_general(a.astype(BF16), b.astype(BF16), (((ca,), (cb,)), ((), ())), preferred_element_type=F32)


@functools.partial(jax.custom_vjp, nondiff_argnums=(2, 3))
def bdot(a, b, ca, cb):
    return _dg(a, b, ca, cb)


def _bdot_fwd(a, b, ca, cb):
    return _dg(a, b, ca, cb), (a, b)


def _bdot_bwd(ca, cb, res, g):
    a, b = res
    da = _dg(g, b, 1, 1 - cb) if ca == 1 else _dg(b, g, 1 - cb, 1)
    db = _dg(a, g, 1 - ca, 0) if cb == 0 else _dg(g, a, 0, 1 - ca)
    return da.astype(a.dtype), db.astype(b.dtype)


bdot.defvjp(_bdot_fwd, _bdot_bwd)


def _lower_tri(n):
    r = lax.broadcasted_iota(jnp.int32, (n, n), 0)
    c = lax.broadcasted_iota(jnp.int32, (n, n), 1)
    return c <= r


def _running_sum(x, reverse):
    n = x.shape[0]
    r = lax.broadcasted_iota(jnp.int32, x.shape, 0)
    d = 1
    while d < n:
        if reverse:
            x = x + jnp.where(r < n - d, pltpu.roll(x, n - d, 0), 0.0)
        else:
            x = x + jnp.where(r >= d, pltpu.roll(x, d, 0), 0.0)
        d *= 2
    return x


@jax.custom_vjp
def cumsum_rows(x):
    return _running_sum(x, False)


cumsum_rows.defvjp(lambda x: (_running_sum(x, False), None), lambda _, g: (_running_sum(g, True),))


def _tri_dot(x, transposed):
    n = x.shape[0]
    return lax.dot_general(_lower_tri(n).astype(F32), x, (((0 if transposed else 1,), (0,)), ((), ())),
                           precision=lax.Precision.HIGHEST, preferred_element_type=F32)


@jax.custom_vjp
def cumsum_rows_mxu(x):
    return _tri_dot(x, False)


cumsum_rows_mxu.defvjp(lambda x: (_tri_dot(x, False), None), lambda _, g: (_tri_dot(g, True),))


def _row(x, i):
    r = lax.broadcasted_iota(jnp.int32, x.shape, 0)
    return jnp.sum(jnp.where(r == i, x, 0.0), axis=0, keepdims=True)


def _softplus_raw(x):
    return jnp.maximum(x, 0.0) + jnp.log(1.0 + jnp.exp(-jnp.abs(x)))


@jax.custom_vjp
def softplus(x):
    return _softplus_raw(x)


softplus.defvjp(lambda x: (_softplus_raw(x), x), lambda x, g: (g * jax.nn.sigmoid(x),))


@jax.custom_vjp
def log_sigmoid(x):
    return -_softplus_raw(-x)


log_sigmoid.defvjp(lambda x: (-_softplus_raw(-x), x), lambda x, g: (g * jax.nn.sigmoid(-x),))


def silu(x):
    return x * jax.nn.sigmoid(x)


def gelu_tanh(x):
    return 0.5 * x * (1.0 + jnp.tanh(math.sqrt(2.0 / math.pi) * (x + 0.044715 * (x * x * x))))


def _expm1(x):
    series = x * (1.0 + 0.5 * x * (1.0 + (1.0 / 3.0) * x))
    return jnp.where(jnp.abs(x) < 1e-2, series, jnp.exp(x) - 1.0)


def rms(x, w):
    return x * lax.rsqrt(jnp.mean(x * x, axis=-1, keepdims=True) + NORM_EPS) * w


def _rows_iota(shape):
    return lax.broadcasted_iota(jnp.int32, shape, 0)


def _scan_up(a, u):
    n = a.shape[0]
    r = _rows_iota(a.shape)
    d = 1
    while d < n:
        m = r >= d
        a_s = jnp.where(m, pltpu.roll(a, d, 0), 1.0)
        u_s = jnp.where(m, pltpu.roll(u, d, 0), 0.0)
        u = a * u_s + u
        a = a * a_s
        d *= 2
    return u


def _scan_down(a, u):
    n = a.shape[0]
    r = _rows_iota(a.shape)
    d = 1
    while d < n:
        m = r < n - d
        a_s = jnp.where(m, pltpu.roll(a, n - d, 0), 1.0)
        u_s = jnp.where(m, pltpu.roll(u, n - d, 0), 0.0)
        u = a * u_s + u
        a = a * a_s
        d *= 2
    return u


@jax.custom_vjp
def lin_scan(a, u):
    return _scan_up(a, u)


def _lin_scan_fwd(a, u):
    h = _scan_up(a, u)
    return h, (a, h)


def _lin_scan_bwd(res, g):
    a, h = res
    n = a.shape[0]
    r = _rows_iota(a.shape)
    a_next = jnp.where(r < n - 1, pltpu.roll(a, n - 1, 0), 0.0)
    gt = _scan_down(a_next, g)
    h_prev = jnp.where(r >= 1, pltpu.roll(h, 1, 0), 0.0)
    return gt * h_prev, gt


lin_scan.defvjp(_lin_scan_fwd, _lin_scan_bwd)


def _expand_heads(v):
    r = v.shape[0]
    return jnp.concatenate([jnp.broadcast_to(v[:, h:h + 1], (r, SSD_P)) for h in range(SSD_HG)], axis=1)


@jax.custom_vjp
def _split_heads(x):
    return tuple(x[:, h * SSD_P:(h + 1) * SSD_P] for h in range(SSD_HG))


_split_heads.defvjp(lambda x: (_split_heads(x), None), lambda _, gs: (jnp.concatenate(gs, axis=1),))


def matmul(a, b, *, ta=False, tb=False, a_slot=None, b_slot=None, res=None, after=None, out_dtype=F32,
           name, tm=1024, tn=1024, tk=2048):
    ra, ca_ = a.shape[-2:]
    rb, cb_ = b.shape[-2:]
    m_st, ka_st = (ca_, ra) if ta else (ra, ca_)
    kb_st, n_st = (cb_, rb) if tb else (rb, cb_)
    kslot = a_slot == "k"
    assert kslot == (b_slot == "k")
    m = m_st * (N_DEV if a_slot == "m" else 1)
    n = n_st * (N_DEV if b_slot == "n" else 1)
    assert ka_st == kb_st, (a.shape, b.shape, ta, tb)
    tm = m_st if a_slot == "m" else _tile(m, tm)
    tn = n_st if b_slot == "n" else _tile(n, tn)
    tk = ka_st if kslot else _tile(ka_st, tk)
    nk = ka_st // tk
    ca, cb = (0 if ta else 1), (1 if tb else 0)

    if a_slot is None:
        a_spec = pl.BlockSpec((tk, tm), lambda i, j, k: (k, i)) if ta else pl.BlockSpec((tm, tk), lambda i, j, k: (i, k))
    elif a_slot == "m":
        a_spec = (pl.BlockSpec((None, tk, tm), lambda i, j, k: (i, k, 0)) if ta
                  else pl.BlockSpec((None, tm, tk), lambda i, j, k: (i, 0, k)))
    else:
        a_spec = (pl.BlockSpec((N_DEV, tk, tm), lambda i, j, k: (0, 0, i)) if ta
                  else pl.BlockSpec((N_DEV, tm, tk), lambda i, j, k: (0, i, 0)))
    if b_slot is None:
        b_spec = pl.BlockSpec((tn, tk), lambda i, j, k: (j, k)) if tb else pl.BlockSpec((tk, tn), lambda i, j, k: (k, j))
    elif b_slot == "n":
        b_spec = (pl.BlockSpec((None, tn, tk), lambda i, j, k: (j, 0, k)) if tb
                  else pl.BlockSpec((None, tk, tn), lambda i, j, k: (j, k, 0)))
    else:
        b_spec = (pl.BlockSpec((N_DEV, tn, tk), lambda i, j, k: (0, j, 0)) if tb
                  else pl.BlockSpec((N_DEV, tk, tn), lambda i, j, k: (0, 0, j)))
    if a_slot == "m":
        o_spec, o_shape = pl.BlockSpec((None, tm, tn), lambda i, j, k: (i, 0, j)), (N_DEV, tm, n)
    elif b_slot == "n":
        o_spec, o_shape = pl.BlockSpec((None, tm, tn), lambda i, j, k: (j, i, 0)), (N_DEV, m, tn)
    else:
        o_spec, o_shape = pl.BlockSpec((tm, tn), lambda i, j, k: (i, j)), (m, n)
    assert res is None or (a_slot != "m" and b_slot != "n")

    def dot(x, y):
        return lax.dot_general(x.astype(BF16), y.astype(BF16), (((ca,), (cb,)), ((), ())), preferred_element_type=F32)

    def body(*refs):
        a_ref, b_ref = refs[:2]
        r_ref = refs[2] if res is not None else None
        o_ref = refs[2 + (res is not None) + (after is not None)]

        def finish(acc):
            if r_ref is not None:
                acc = acc + r_ref[...].astype(F32)
            o_ref[...] = acc.astype(o_ref.dtype)

        if kslot:
            acc = dot(a_ref[0], b_ref[0])
            for s in range(1, N_DEV):
                acc = acc + dot(a_ref[s], b_ref[s])
            finish(acc)
        elif nk == 1:
            finish(dot(a_ref[...], b_ref[...]))
        else:
            acc_ref = refs[-1]
            k = pl.program_id(2)

            @pl.when(k == 0)
            def _():
                acc_ref[...] = dot(a_ref[...], b_ref[...])

            @pl.when(k > 0)
            def _():
                acc_ref[...] += dot(a_ref[...], b_ref[...])

            @pl.when(k == nk - 1)
            def _():
                finish(acc_ref[...])

    in_specs = [a_spec, b_spec]
    args = [a, b]
    if res is not None:
        in_specs.append(pl.BlockSpec((tm, tn), lambda i, j, k: (i, j)))
        args.append(res)
    if after is not None:
        in_specs.append(pl.BlockSpec(memory_space=pl.ANY))
        args.append(after)
    return pl.pallas_call(
        body, name=name, grid=(m // tm, n // tn, nk), in_specs=in_specs, out_specs=o_spec,
        out_shape=jax.ShapeDtypeStruct(o_shape, out_dtype),
        scratch_shapes=[pltpu.VMEM((tm, tn), F32)] if nk > 1 else [], compiler_params=_cparams(3),
    )(*args)


def seq_call(name, fn, grid, ins, outs, accs=(), carries=()):
    n_in, n_out, n_acc = len(ins), len(outs), len(accs)

    def body(*refs):
        in_refs = refs[:n_in]
        out_refs = refs[n_in:n_in + n_out]
        acc_refs = refs[n_in + n_out:n_in + n_out + n_acc]
        c_refs = refs[n_in + n_out + n_acc:]

        if acc_refs or c_refs:
            @pl.when(pl.program_id(1) == 0)
            def _():
                for r in tuple(acc_refs) + tuple(c_refs):
                    r[...] = jnp.zeros_like(r)

        o, a, c = fn([r[...] for r in in_refs], [r[...] for r in c_refs])
        for r, v in zip(out_refs, o, strict=True):
            r[...] = v.astype(r.dtype)
        for r, v in zip(acc_refs, a, strict=True):
            r[...] += v
        for r, v in zip(c_refs, c, strict=True):
            r[...] = v

    return pl.pallas_call(
        body, name=name, grid=grid,
        in_specs=[pl.BlockSpec(blk, im) for _, blk, im in ins],
        out_specs=[pl.BlockSpec(blk, im) for _, _, blk, im in outs] + [pl.BlockSpec(blk, im) for _, blk, im in accs],
        out_shape=[jax.ShapeDtypeStruct(s, d) for s, d, _, _ in outs] + [jax.ShapeDtypeStruct(s, F32) for s, _, _ in accs],
        scratch_shapes=[pltpu.VMEM(s, F32) for s in carries], compiler_params=_cparams(2),
    )(*[a for a, _, _ in ins])


_HBM = pl.BlockSpec(memory_space=pltpu.HBM)
_SEM = pl.BlockSpec(memory_space=pltpu.SEMAPHORE)


def _mesh_pos():
    pos = [lax.axis_index(ax) for ax in MESH_AXES]
    return pos, 4 * pos[0] + 2 * pos[1] + pos[2]


ALL_PEERS = (1, 2, 3, 4, 5, 6, 7)
NEAR_PEERS = (1, 2, 4, 6)
RELAYED = (2, 4, 6)


def _peers(pos, masks=ALL_PEERS):
    out = []
    for k in masks:
        bits = ((k >> 2) & 1, (k >> 1) & 1, k & 1)
        peer = tuple(1 - p if b else p for p, b in zip(pos, bits))
        out.append((peer, 4 * peer[0] + 2 * peer[1] + peer[2]))
    return out


def _part(x_ref, lead, gather, slot):
    ref = x_ref if lead is None else x_ref.at[lead]
    return ref if gather else ref.at[slot]


OWN_BLOCK_BYTES = 2 * 1024 * 1024


def _landing_zone(a, lead, me, name):
    r, c = a.shape[-2:]
    tr = r
    while tr * _round_up(c, LANES) * a.dtype.itemsize > OWN_BLOCK_BYTES and tr % 32 == 0:
        tr //= 2

    def body(me_ref, x_ref, o_ref):
        o_ref[...] = x_ref[...]

    x_spec = (pl.BlockSpec((tr, c), lambda i, me_ref: (i, 0)) if lead is None
              else pl.BlockSpec((None, tr, c), lambda i, me_ref: (lead, i, 0)))
    grid_spec = pltpu.PrefetchScalarGridSpec(
        num_scalar_prefetch=1, grid=(r // tr,), in_specs=[x_spec],
        out_specs=pl.BlockSpec((None, tr, c), lambda i, me_ref: (me_ref[0], i, 0)))
    return pl.pallas_call(body, name=name, grid_spec=grid_spec, out_shape=jax.ShapeDtypeStruct((N_DEV, r, c), a.dtype),
                          compiler_params=_cparams(1))(jnp.reshape(me, (1,)).astype(jnp.int32), a)


def _split_copies(items, gather, masks, x_refs, land_refs, send_sems, recv_sems):
    pos, me = _mesh_pos()
    copies = []
    for i, (_, lead) in enumerate(items):
        for k, (peer, peer_id) in enumerate(_peers(pos, masks)):
            copies.append(pltpu.make_async_remote_copy(
                src_ref=_part(x_refs[i], lead, gather, peer_id), dst_ref=land_refs[i].at[me],
                send_sem=send_sems.at[i * len(masks) + k], recv_sem=recv_sems.at[i * len(masks) + k],
                device_id=peer, device_id_type=pl.DeviceIdType.MESH))
    return copies


_SPLIT_CALL = dict(compiler_params=pltpu.CompilerParams(has_side_effects=pltpu.SideEffectType.DATAFLOW_SIDE_EFFECTING))


def exchange_start(items, gather, lands, name, masks=ALL_PEERS):
    n = len(items)
    lands = list(lands)
    xs = [a for a, _ in items]

    def body(*refs):
        x_refs, land_refs = refs[:n], refs[n:2 * n]
        send_sems, recv_sems, token = refs[2 * n], refs[2 * n + 1], refs[-1]
        for cp in _split_copies(items, gather, masks, x_refs, land_refs, send_sems, recv_sems):
            cp.start()
        token[...] = jnp.zeros_like(token)

    outs = pl.pallas_call(
        body, name=name,
        out_shape=(pltpu.SemaphoreType.DMA((n * len(masks),)), pltpu.SemaphoreType.DMA((n * len(masks),)),
                   *[pltpu.HBM(v.shape, v.dtype) for v in xs + lands], jax.ShapeDtypeStruct((SUBLANES, LANES), F32)),
        in_specs=[_HBM] * (2 * n),
        out_specs=(_SEM, _SEM, *[_HBM] * (2 * n), pl.BlockSpec(memory_space=pltpu.VMEM)),
        input_output_aliases={i: 2 + i for i in range(2 * n)}, **_SPLIT_CALL,
    )(*[pltpu.with_memory_space_constraint(v, pltpu.HBM) for v in xs + lands])
    handle = (items, gather, masks, outs[0], outs[1], outs[2:2 + n], outs[2 + n:2 + 2 * n])
    return handle, outs[-1]


def _relay_copies(n, land_refs, send_sems, recv_sems):
    pos, _ = _mesh_pos()
    sibling = (pos[0], pos[1], 1 - pos[2])
    copies = []
    for i in range(n):
        for k, (_, peer_id) in enumerate(_peers(pos, RELAYED)):
            slot = land_refs[i].at[peer_id]
            copies.append(pltpu.make_async_remote_copy(
                src_ref=slot, dst_ref=slot, send_sem=send_sems.at[i * len(RELAYED) + k],
                recv_sem=recv_sems.at[i * len(RELAYED) + k], device_id=sibling, device_id_type=pl.DeviceIdType.MESH))
    return copies


def relay(lands, name, work=None):
    n = len(lands)
    sems = n * len(RELAYED)

    def start(*refs):
        for cp in _relay_copies(n, refs[:n], refs[n], refs[n + 1]):
            cp.start()
        refs[-1][...] = jnp.zeros_like(refs[-1])

    outs = pl.pallas_call(
        start, name=name + "_start",
        out_shape=(pltpu.SemaphoreType.DMA((sems,)), pltpu.SemaphoreType.DMA((sems,)),
                   *[pltpu.HBM(v.shape, v.dtype) for v in lands], jax.ShapeDtypeStruct((SUBLANES, LANES), F32)),
        in_specs=[_HBM] * n, out_specs=(_SEM, _SEM, *[_HBM] * n, pl.BlockSpec(memory_space=pltpu.VMEM)),
        input_output_aliases={i: 2 + i for i in range(n)},
        **_SPLIT_CALL)(*[pltpu.with_memory_space_constraint(v, pltpu.HBM) for v in lands])

    def wait(*refs):
        for cp in _relay_copies(n, refs[:n], refs[n], refs[n + 1]):
            cp.wait_send()
            cp.wait_recv()

    done = None if work is None else work(outs[-1])
    extra = [] if work is None else [jax.tree.leaves(done)[0]]
    filled = list(pl.pallas_call(
        wait, name=name + "_wait", out_shape=tuple(pltpu.HBM(v.shape, v.dtype) for v in lands),
        in_specs=[_HBM] * n + [_SEM, _SEM] + [pl.BlockSpec(memory_space=pl.ANY)] * len(extra),
        out_specs=tuple([_HBM] * n), input_output_aliases={i: i for i in range(n)},
        **_SPLIT_CALL)(*outs[2:2 + n], outs[0], outs[1], *extra))
    return filled if work is None else (filled, done)


def exchange_wait(handle, after, name):
    items, gather, masks, send_sems, recv_sems, x_thru, land_thru = handle
    n = len(items)

    def body(*refs):
        x_refs, land_refs = refs[:n], refs[n:2 * n]
        for cp in _split_copies(items, gather, masks, x_refs, land_refs, refs[2 * n], refs[2 * n + 1]):
            cp.wait_send()
            cp.wait_recv()

    outs = pl.pallas_call(
        body, name=name, out_shape=tuple(pltpu.HBM(v.shape, v.dtype) for v in tuple(x_thru) + tuple(land_thru)),
        in_specs=[_HBM] * (2 * n) + [_SEM, _SEM, pl.BlockSpec(memory_space=pl.ANY)], out_specs=tuple([_HBM] * (2 * n)),
        input_output_aliases={i: i for i in range(2 * n)},
        compiler_params=pltpu.CompilerParams(has_side_effects=pltpu.SideEffectType.DATAFLOW_SIDE_EFFECTING),
    )(*x_thru, *land_thru, send_sems, recv_sems, after)
    return list(outs[:n]), list(outs[n:])


def _adam_update(w, g, m, v):
    nm = ADAM_B1 * m + (1.0 - ADAM_B1) * g
    nv = ADAM_B2 * v + (1.0 - ADAM_B2) * (g * g)
    m_hat = nm / (1.0 - ADAM_B1 ** ADAM_STEP)
    v_hat = nv / (1.0 - ADAM_B2 ** ADAM_STEP)
    return -ADAM_LR * (m_hat / (jnp.sqrt(v_hat) + ADAM_EPS) + ADAM_WD * w), nm, nv


def _sum_slots(s_ref):
    acc = s_ref[0].astype(F32)
    for j in range(1, N_DEV):
        acc = acc + s_ref[j].astype(F32)
    return acc


ADAM_BLOCK_BYTES = 10 * 1024 * 1024


def adamw_sharded(recvs, sends, me, w, m, v, name, transposed=False):
    nl = w.shape[0]
    r, c = recvs[0].shape[1:]
    assert w.shape[1:] == ((c, r) if transposed else (r, c))
    assert len(recvs) == nl and len(sends) == nl
    per_row = _round_up(c, LANES) * (nl * (N_DEV + 1) * recvs[0].dtype.itemsize + 7 * 4)
    tr = r
    while tr * per_row > ADAM_BLOCK_BYTES and tr % 16 == 0:
        tr //= 2

    def body(me_ref, *refs):
        s_refs, o_refs = refs[:nl], refs[nl:2 * nl]
        w_ref, m_ref, v_ref, g_ref, d_ref, nm_ref, nv_ref = refs[2 * nl:]
        mine = me_ref[0]

        def total(l):
            acc = jnp.where(mine == 0, o_refs[l][...], s_refs[l][0]).astype(F32)
            for j in range(1, N_DEV):
                acc = acc + jnp.where(mine == j, o_refs[l][...], s_refs[l][j]).astype(F32)
            return acc

        g = total(0)
        for l in range(1, nl):
            g = jnp.where(pl.program_id(0) == l, total(l), g)
        if transposed:
            g = g.T
        g_ref[...] = g
        d_ref[...], nm_ref[...], nv_ref[...] = _adam_update(w_ref[...], g, m_ref[...], v_ref[...])

    spec = (pl.BlockSpec((None, c, tr), lambda l, i, me_ref: (l, 0, i)) if transposed
            else pl.BlockSpec((None, tr, c), lambda l, i, me_ref: (l, i, 0)))
    shp = jax.ShapeDtypeStruct(w.shape, F32)
    grid_spec = pltpu.PrefetchScalarGridSpec(
        num_scalar_prefetch=1, grid=(nl, r // tr),
        in_specs=([pl.BlockSpec((N_DEV, tr, c), lambda l, i, me_ref: (0, i, 0))] * nl
                  + [pl.BlockSpec((None, tr, c), lambda l, i, me_ref: (me_ref[0], i, 0))] * nl + [spec, spec, spec]),
        out_specs=[spec] * 4)
    return pl.pallas_call(body, name=name, grid_spec=grid_spec, out_shape=[shp] * 4, compiler_params=_cparams(2))(
        jnp.reshape(me, (1,)).astype(jnp.int32), *recvs, *sends, w, m, v)


def reduce_slots(slots, name):
    _, r, lanes = slots.shape
    tr = _tile(r, 2048)

    def body(s_ref, o_ref):
        o_ref[...] = _sum_slots(s_ref)

    return pl.pallas_call(
        body, name=name, grid=(r // tr,),
        in_specs=[pl.BlockSpec((N_DEV, tr, lanes), lambda i: (0, i, 0))],
        out_specs=pl.BlockSpec((tr, lanes), lambda i: (i, 0)),
        out_shape=jax.ShapeDtypeStruct((r, lanes), F32), compiler_params=_cparams(1),
    )(slots)


def adamw(w, g, m, v, name):
    r, lanes = w.shape
    tr = _tile(r, 2048)

    def body(w_ref, g_ref, m_ref, v_ref, d_ref, nm_ref, nv_ref):
        d_ref[...], nm_ref[...], nv_ref[...] = _adam_update(w_ref[...], g_ref[...], m_ref[...], v_ref[...])

    spec = pl.BlockSpec((tr, lanes), lambda i: (i, 0))
    shp = jax.ShapeDtypeStruct((r, lanes), F32)
    return pl.pallas_call(body, name=name, grid=(r // tr,), in_specs=[spec] * 4, out_specs=[spec] * 3,
                          out_shape=[shp] * 3, compiler_params=_cparams(1))(w, g, m, v)


def cols_from_slots(slots, place, width, name):
    _, rows, c = slots.shape
    tr = _tile(rows, 256)

    def body(s_ref, o_ref):
        o_ref[...] = place(jnp.concatenate([s_ref[j] for j in range(N_DEV)], axis=1))

    return pl.pallas_call(
        body, name=name, grid=(rows // tr,),
        in_specs=[pl.BlockSpec((N_DEV, tr, c), lambda i: (0, i, 0))],
        out_specs=pl.BlockSpec((tr, width), lambda i: (i, 0)),
        out_shape=jax.ShapeDtypeStruct((rows, width), slots.dtype), compiler_params=_cparams(1))(slots)


def slots_from_cols(full, pick, c, name):
    rows, wide = full.shape
    tr = _tile(rows, 256)

    def body(x_ref, o_ref):
        v = pick(x_ref[...])
        for j in range(N_DEV):
            o_ref[j] = v[:, j * c:(j + 1) * c]

    return pl.pallas_call(
        body, name=name, grid=(rows // tr,),
        in_specs=[pl.BlockSpec((tr, wide), lambda i: (i, 0))],
        out_specs=pl.BlockSpec((N_DEV, tr, c), lambda i: (0, i, 0)),
        out_shape=jax.ShapeDtypeStruct((N_DEV, rows, c), full.dtype), compiler_params=_cparams(1))(full)


def _token_tile(cfg):
    return min(cfg.S, 256)


def norm_fwd(cfg, x, w, name):
    ts = _token_tile(cfg)
    d = x.shape[1]

    def fn(ins, _):
        xv, wv = ins
        return [rms(xv, wv)], [], []

    return seq_call(name, fn, (1, cfg.S // ts),
                    [(x, (ts, d), lambda g, t: (t, 0)), (w, (1, d), lambda g, t: (0, 0))],
                    [((cfg.S, d), BF16, (ts, d), lambda g, t: (t, 0))])[0]


def norm_bwd(cfg, x, w, dh, dres, name):
    ts = _token_tile(cfg)
    d = x.shape[1]

    def fn(ins, _):
        xv, wv, dhv, drv = ins
        _, vjp = jax.vjp(rms, xv, wv)
        dx, dw = vjp(dhv.astype(F32))
        return [dx + drv], [dw], []

    row = lambda g, t: (t, 0)
    out = seq_call(name, fn, (1, cfg.S // ts),
                   [(x, (ts, d), row), (w, (1, d), lambda g, t: (0, 0)), (dh, (ts, d), row), (dres, (ts, d), row)],
                   [((cfg.S, d), F32, (ts, d), row)],
                   accs=[((1, d), (1, d), lambda g, t: (0, 0))])
    return out[0], out[1]


def head_fwd_bwd(cfg, x, w, target, name):
    ts = _token_tile(cfg)
    d = x.shape[1]

    def fn(ins, _):
        xv, wv, tv = ins
        y, vjp = jax.vjp(rms, xv, wv)
        err = y - tv
        loss = 0.5 * jnp.sum(err * err) / d
        dx, dw = vjp(err / d)
        return [dx], [jnp.full((SUBLANES, LANES), loss, F32), dw], []

    row = lambda g, t: (t, 0)
    fixed = lambda g, t: (0, 0)
    dx, loss, dw = seq_call(name, fn, (1, cfg.S // ts),
                            [(x, (ts, d), row), (w, (1, d), fixed), (target, (ts, d), row)],
                            [((cfg.S, d), F32, (ts, d), row)],
                            accs=[((SUBLANES, LANES), (SUBLANES, LANES), fixed), ((1, d), (1, d), fixed)])
    return loss, dx, dw


FFN_ROWS = 1024


def ffn_gate_up(h, w_gate, w_up, name):
    s, d = h.shape
    c = w_gate.shape[1]
    tm = _tile(s, FFN_ROWS)

    def body(h_ref, wg_ref, wu_ref, g_ref, u_ref, a_ref):
        hv = h_ref[...]
        g = _dg(hv, wg_ref[...], 1, 1)
        u = _dg(hv, wu_ref[...], 1, 1)
        g_ref[...] = g
        u_ref[...] = u
        a_ref[...] = (silu(g) * u).astype(a_ref.dtype)

    w_spec = pl.BlockSpec((None, c, d), lambda i, j: (j, 0, 0))
    o_spec = pl.BlockSpec((None, tm, c), lambda i, j: (j, i, 0))
    shp = (N_DEV, s, c)
    return pl.pallas_call(
        body, name=name, grid=(s // tm, N_DEV), in_specs=[pl.BlockSpec((tm, d), lambda i, j: (i, 0)), w_spec, w_spec],
        out_specs=[o_spec] * 3,
        out_shape=[jax.ShapeDtypeStruct(shp, F32), jax.ShapeDtypeStruct(shp, F32), jax.ShapeDtypeStruct(shp, BF16)],
        compiler_params=_cparams(2))(h, w_gate, w_up)


def ffn_dgate_dup(dx, w_down, gate, up, after, name):
    s, d = dx.shape
    c = w_down.shape[1]
    tm = _tile(s, FFN_ROWS)
    extra = [] if after is None else [after]

    def body(dx_ref, wd_ref, g_ref, u_ref, *rest):
        dg_ref, du_ref = rest[len(extra):]
        dact = _dg(dx_ref[...], wd_ref[...], 1, 1)
        _, vjp = jax.vjp(lambda a, b: silu(a) * b, g_ref[...], u_ref[...])
        dg, du = vjp(dact)
        dg_ref[...] = dg.astype(dg_ref.dtype)
        du_ref[...] = du.astype(du_ref.dtype)

    blk = pl.BlockSpec((None, tm, c), lambda i, j: (j, i, 0))
    shp = jax.ShapeDtypeStruct((N_DEV, s, c), BF16)
    return pl.pallas_call(
        body, name=name, grid=(s // tm, N_DEV),
        in_specs=[pl.BlockSpec((tm, d), lambda i, j: (i, 0)), pl.BlockSpec((None, c, d), lambda i, j: (j, 0, 0)), blk, blk]
        + [pl.BlockSpec(memory_space=pl.ANY)] * len(extra),
        out_specs=[blk, blk], out_shape=[shp, shp], compiler_params=_cparams(2))(dx, w_down, gate, up, *extra)


CONV_COLS = (512, 256)
HALO = SUBLANES


def _conv_cols(col0, width):
    return next(c for c in CONV_COLS if col0 % c == 0 and width % c == 0)


def _shift_down(x, halo, j):
    if j == 0:
        return x
    r8 = _rows_iota(halo.shape)
    top = jnp.where(r8 >= j, pltpu.roll(x[:HALO], j, 0), pltpu.roll(halo, j, 0))
    return jnp.concatenate([top, pltpu.roll(x, j, 0)[HALO:]], axis=0)


def _shift_up(x, halo, j):
    if j == 0:
        return x
    n = x.shape[0]
    r8 = _rows_iota(halo.shape)
    bot = jnp.where(r8 < HALO - j, pltpu.roll(x[n - HALO:], HALO - j, 0), pltpu.roll(halo, HALO - j, 0))
    return jnp.concatenate([pltpu.roll(x, n - j, 0)[:n - HALO], bot], axis=0)


def _conv_tile(cfg):
    return min(cfg.S, 1024)


def conv_fwd(cfg, src, col0, width, w, b, name):
    tt, cb = _conv_tile(cfg), _conv_cols(col0, width)
    c0, hb = col0 // cb, tt // HALO
    nt = cfg.S // tt

    def body(x_ref, h_ref, w_ref, b_ref, o_ref):
        t = pl.program_id(1)
        x = x_ref[...]
        halo = jnp.where(t > 0, h_ref[...], 0.0)
        wv = w_ref[...]
        acc = b_ref[...] + wv[CONV_WIDTH - 1:CONV_WIDTH] * x
        for j in range(1, CONV_WIDTH):
            acc = acc + wv[CONV_WIDTH - 1 - j:CONV_WIDTH - j] * _shift_down(x, halo, j)
        o_ref[...] = acc

    return pl.pallas_call(
        body, name=name, grid=(width // cb, nt),
        in_specs=[pl.BlockSpec((tt, cb), lambda c, t: (t, c0 + c)),
                  pl.BlockSpec((HALO, cb), lambda c, t: (jnp.maximum(t * hb - 1, 0), c0 + c)),
                  pl.BlockSpec((CONV_WIDTH, cb), lambda c, t: (0, c)),
                  pl.BlockSpec((1, cb), lambda c, t: (0, c))],
        out_specs=pl.BlockSpec((tt, cb), lambda c, t: (t, c)),
        out_shape=jax.ShapeDtypeStruct((cfg.S, width), F32), compiler_params=_cparams(2),
    )(src, src, w, b)


def conv_bwd(cfg, src, col0, width, w, dy, name, into=None, into_col0=0):
    tt, cb = _conv_tile(cfg), _conv_cols(col0, width)
    c0, hb = col0 // cb, tt // HALO
    nt = cfg.S // tt
    extra = [] if into is None else [into]
    assert into_col0 % cb == 0
    o0 = into_col0 // cb

    def body(x_ref, h_ref, w_ref, dy_ref, dh_ref, *rest):
        dx_ref, dw_ref, db_ref = rest[len(extra):]
        t = pl.program_id(1)

        @pl.when(t == 0)
        def _():
            dw_ref[...] = jnp.zeros_like(dw_ref)
            db_ref[...] = jnp.zeros_like(db_ref)

        x = x_ref[...]
        halo = jnp.where(t > 0, h_ref[...], 0.0)
        dy = dy_ref[...]
        dhalo = jnp.where(t < nt - 1, dh_ref[...], 0.0)
        wv = w_ref[...]
        dx = wv[CONV_WIDTH - 1:CONV_WIDTH] * dy
        rows = [jnp.sum(dy * x, axis=0, keepdims=True)]
        for j in range(1, CONV_WIDTH):
            dx = dx + wv[CONV_WIDTH - 1 - j:CONV_WIDTH - j] * _shift_up(dy, dhalo, j)
            rows.insert(0, jnp.sum(dy * _shift_down(x, halo, j), axis=0, keepdims=True))
        dx_ref[...] = dx.astype(dx_ref.dtype)
        dw_ref[...] += jnp.concatenate(rows, axis=0)
        db_ref[...] += jnp.sum(dy, axis=0, keepdims=True)

    return pl.pallas_call(
        body, name=name, grid=(width // cb, nt),
        in_specs=[pl.BlockSpec((tt, cb), lambda c, t: (t, c0 + c)),
                  pl.BlockSpec((HALO, cb), lambda c, t: (jnp.maximum(t * hb - 1, 0), c0 + c)),
                  pl.BlockSpec((CONV_WIDTH, cb), lambda c, t: (0, c)),
                  pl.BlockSpec((tt, cb), lambda c, t: (t, c)),
                  pl.BlockSpec((HALO, cb), lambda c, t: (jnp.minimum((t + 1) * hb, nt * hb - 1), c))]
        + [pl.BlockSpec(memory_space=pl.ANY)] * len(extra),
        out_specs=[pl.BlockSpec((tt, cb), lambda c, t: (t, o0 + c)),
                   pl.BlockSpec((CONV_WIDTH, cb), lambda c, t: (0, c)),
                   pl.BlockSpec((1, cb), lambda c, t: (0, c))],
        out_shape=[jax.ShapeDtypeStruct((cfg.S, width) if into is None else into.shape, BF16),
                   jax.ShapeDtypeStruct((CONV_WIDTH, width), F32), jax.ShapeDtypeStruct((1, width), F32)],
        input_output_aliases={5: 0} if extra else {}, compiler_params=_cparams(2),
    )(src, src, w, dy, dy, *extra)


def write_cols(cfg, into, piece, col0, name):
    tt, width = _conv_tile(cfg), piece.shape[1]
    assert col0 % width == 0 and into.dtype == piece.dtype

    def body(p_ref, _, o_ref):
        o_ref[...] = p_ref[...]

    return pl.pallas_call(
        body, name=name, grid=(cfg.S // tt,),
        in_specs=[pl.BlockSpec((tt, width), lambda t: (t, 0)), pl.BlockSpec(memory_space=pl.ANY)],
        out_specs=pl.BlockSpec((tt, width), lambda t: (t, col0 // width)),
        out_shape=jax.ShapeDtypeStruct(into.shape, into.dtype), input_output_aliases={1: 0},
        compiler_params=_cparams(1))(piece, into)


def _gla_core(gh, q, k, v, g, glr, wg, bg, wn, st):
    n = glr.shape[0]
    causal = _lower_tri(n)
    outs, new = [], []
    for h in range(gh):
        log_a = log_sigmoid(bdot(glr, wg[h], 1, 0) + bg[h]) * (1.0 / GATE_NORM)
        bcum = cumsum_rows(log_a)
        b_last, b_mid = _row(bcum, n - 1), _row(bcum, n // 2)
        qs = q[h] * (HEAD_K ** -0.5)
        scores = jnp.where(causal, bdot(qs * jnp.exp(bcum - b_mid), k[h] * jnp.exp(b_mid - bcum), 1, 1), 0.0)
        o = bdot(scores, v[h], 1, 0) + bdot(qs * jnp.exp(bcum), st[h], 1, 1)
        new.append(st[h] * jnp.exp(b_last) + bdot(v[h], k[h] * jnp.exp(b_last - bcum), 0, 0))
        outs.append(rms(o, wn) * silu(g[h]))
    return jnp.concatenate(outs, axis=1), new


def _gla_ins(cfg, p0, wg, bg, wn, tmap):
    gh = cfg.GH
    ins = []
    for h in range(gh):
        ins.append((p0, (CHUNK, HEAD_K), lambda g, t, h=h: (tmap(t), h)))
    for h in range(gh):
        ins.append((p0, (CHUNK, HEAD_K), lambda g, t, h=h: (tmap(t), gh + h)))
    for h in range(gh):
        ins.append((p0, (CHUNK, HEAD_V), lambda g, t, h=h: (tmap(t), gh + h)))
    for h in range(gh):
        ins.append((p0, (CHUNK, HEAD_V), lambda g, t, h=h: (tmap(t), 2 * gh + h)))
    ins.append((p0, (CHUNK, LANES), lambda g, t: (tmap(t), 10 * gh)))
    for h in range(gh):
        ins.append((wg, (LANES, HEAD_K), lambda g, t, h=h: (0, h)))
    for h in range(gh):
        ins.append((bg, (1, HEAD_K), lambda g, t, h=h: (0, h)))
    ins.append((wn, (1, HEAD_V), lambda g, t: (0, 0)))
    return ins


def _gla_unpack(gh, vals):
    q, k, v, g = (vals[i * gh:(i + 1) * gh] for i in range(4))
    glr = vals[4 * gh]
    wg = vals[4 * gh + 1:5 * gh + 1]
    bg = vals[5 * gh + 1:6 * gh + 1]
    wn = vals[6 * gh + 1]
    return q, k, v, g, glr, wg, bg, wn, vals[6 * gh + 2:]


def gla_fwd(cfg, p0, wg, bg, wn, name):
    gh, nc = cfg.GH, cfg.S // CHUNK

    def fn(ins, st):
        q, k, v, g, glr, wgv, bgv, wnv, _ = _gla_unpack(gh, ins)
        out, new = _gla_core(gh, q, k, v, g, glr, wgv, bgv, wnv, st)
        return [out, jnp.stack(st)], [], new

    return seq_call(name, fn, (1, nc), _gla_ins(cfg, p0, wg, bg, wn, lambda t: t),
                    [((cfg.S, cfg.DV), BF16, (CHUNK, cfg.DV), lambda g, t: (t, 0)),
                     ((nc, gh, HEAD_V, HEAD_K), F32, (None, gh, HEAD_V, HEAD_K), lambda g, t: (t, 0, 0, 0))],
                    carries=[(HEAD_V, HEAD_K)] * gh)


def gla_bwd(cfg, p0, wg, bg, wn, states, dout, name):
    gh, nc = cfg.GH, cfg.S // CHUNK
    rev = lambda t: nc - 1 - t

    def fn(ins, dst):
        q, k, v, g, glr, wgv, bgv, wnv, rest = _gla_unpack(gh, ins)
        st_all, do = rest
        st = [st_all[h] for h in range(gh)]
        _, vjp = jax.vjp(functools.partial(_gla_core, gh), q, k, v, g, glr, wgv, bgv, wnv, st)
        dq, dk, dv, dg, dglr, dwg, dbg, dwn, dstate = vjp((do.astype(F32), list(dst)))
        return ([jnp.concatenate(list(dq) + list(dk) + list(dv) + list(dg), axis=1), dglr],
                [jnp.concatenate(dwg, axis=1), jnp.concatenate(dbg, axis=1), dwn], dstate)

    ins = _gla_ins(cfg, p0, wg, bg, wn, rev)
    ins.append((states, (None, gh, HEAD_V, HEAD_K), lambda g, t: (rev(t), 0, 0, 0)))
    ins.append((dout, (CHUNK, cfg.DV), lambda g, t: (rev(t), 0)))
    wide = 2 * cfg.DK + 2 * cfg.DV
    fixed = lambda g, t: (0, 0)
    return seq_call(name, fn, (1, nc), ins,
                    [((cfg.S, wide), BF16, (CHUNK, wide), lambda g, t: (rev(t), 0)),
                     ((cfg.S, LANES), BF16, (CHUNK, LANES), lambda g, t: (rev(t), 0))],
                    accs=[((LANES, cfg.DK), (LANES, cfg.DK), fixed), ((1, cfg.DK), (1, cfg.DK), fixed),
                          ((1, HEAD_V), (1, HEAD_V), fixed)],
                    carries=[(HEAD_V, HEAD_K)] * gh)


def _lru_core(xc, gate, wa, wi, ba, bi, lam, h_in):
    r = jax.nn.sigmoid(bdot(xc, wa, 1, 0) + ba)
    i = jax.nn.sigmoid(bdot(xc, wi, 1, 0) + bi)
    log_a = LRU_C * r * log_sigmoid(lam)
    a = jnp.exp(log_a)
    u = jnp.sqrt(-_expm1(2.0 * log_a)) * (i * xc)
    first = _rows_iota(a.shape) == 0
    h = lin_scan(a, u + jnp.where(first, a * h_in, 0.0))
    return h * gelu_tanh(gate), _row(h, a.shape[0] - 1)


def _lru_tile(cfg):
    return min(cfg.S, 512)


def _lru_ins(cfg, xc, p0, wa, wi, ba, bi, lam, tmap):
    tt, gh = _lru_tile(cfg), cfg.GH
    vec = lambda g, t: (0, g)
    return [(xc, (tt, LRU_BLOCK), lambda g, t: (tmap(t), g)),
            (p0, (tt, LRU_BLOCK), lambda g, t: (tmap(t), 8 * gh + g)),
            (wa, (None, LRU_BLOCK, LRU_BLOCK), lambda g, t: (g, 0, 0)),
            (wi, (None, LRU_BLOCK, LRU_BLOCK), lambda g, t: (g, 0, 0)),
            (ba, (1, LRU_BLOCK), vec), (bi, (1, LRU_BLOCK), vec), (lam, (1, LRU_BLOCK), vec)]


def lru_fwd(cfg, xc, p0, wa, wi, ba, bi, lam, name):
    tt, nb = _lru_tile(cfg), cfg.NB
    nt = cfg.S // tt

    def fn(ins, c):
        out, h_last = _lru_core(*ins, c[0])
        return [out, c[0]], [], [h_last]

    return seq_call(name, fn, (nb, nt), _lru_ins(cfg, xc, p0, wa, wi, ba, bi, lam, lambda t: t),
                    [((cfg.S, cfg.W), BF16, (tt, LRU_BLOCK), lambda g, t: (t, g)),
                     ((nb, nt, 1, LRU_BLOCK), F32, (None, None, 1, LRU_BLOCK), lambda g, t: (g, t, 0, 0))],
                    carries=[(1, LRU_BLOCK)])


def lru_bwd(cfg, xc, p0, wa, wi, ba, bi, lam, states, dout, dout_col0, name):
    tt, nb = _lru_tile(cfg), cfg.NB
    nt = cfg.S // tt
    rev = lambda t: nt - 1 - t

    def fn(ins, c):
        *fwd_ins, h_in, do = ins
        _, vjp = jax.vjp(_lru_core, *fwd_ins, h_in)
        dxc, dgate, dwa, dwi, dba, dbi, dlam, dh = vjp((do.astype(F32), c[0]))
        return [dxc, dgate], [dwa, dwi, dba, dbi, dlam], [dh]

    ins = _lru_ins(cfg, xc, p0, wa, wi, ba, bi, lam, rev)
    ins.append((states, (None, None, 1, LRU_BLOCK), lambda g, t: (g, rev(t), 0, 0)))
    ins.append((dout, (tt, LRU_BLOCK), lambda g, t: (rev(t), dout_col0 // LRU_BLOCK + g)))
    mat = ((nb, LRU_BLOCK, LRU_BLOCK), (None, LRU_BLOCK, LRU_BLOCK), lambda g, t: (g, 0, 0))
    vec = ((1, cfg.W), (1, LRU_BLOCK), lambda g, t: (0, g))
    return seq_call(name, fn, (nb, nt), ins,
                    [((cfg.S, cfg.W), F32, (tt, LRU_BLOCK), lambda g, t: (rev(t), g)),
                     ((cfg.S, cfg.W), BF16, (tt, LRU_BLOCK), lambda g, t: (rev(t), g))],
                    accs=[mat, mat, vec, vec, vec], carries=[(1, LRU_BLOCK)])


def _ssd_core(xc, bc, cc, z, dt_raw, dt_bias, a_log, d_skip, gn, st):
    n = xc.shape[0]
    x, bm, cm = silu(xc), silu(bc), silu(cc)
    dt = softplus(dt_raw + dt_bias)
    acs = cumsum_rows_mxu(dt * (-jnp.exp(a_log)))
    acs_t = acs.T
    acs_e, dt_e = _expand_heads(acs), _expand_heads(dt)
    last_e = _expand_heads(_row(acs, n - 1))
    causal = _lower_tri(n)
    cb = bdot(cm, bm, 1, 1)
    xdt = x * dt_e
    y_diag = []
    for h, xh in enumerate(_split_heads(xdt)):
        seg = acs[:, h:h + 1] - acs_t[h:h + 1, :]
        decay = jnp.where(causal, jnp.exp(jnp.minimum(seg, 0.0)), 0.0)
        y_diag.append(bdot(cb * decay, xh, 1, 0))
    y = jnp.concatenate(y_diag, axis=1) + bdot(cm, st, 1, 0) * jnp.exp(acs_e)
    new = st * jnp.exp(last_e) + bdot(bm, xdt * jnp.exp(last_e - acs_e), 0, 0)
    y = (y + _expand_heads(d_skip) * x) * silu(z)
    return rms(y, gn), new


SSD_TILED = 5


SSD_FWD_CHUNKS = 8
SSD_BWD_CHUNKS = 1


def _ssd_tile(cfg, chunks):
    return min(cfg.S, chunks * CHUNK)


def _chunk_rows(v, s):
    return v[s * CHUNK:(s + 1) * CHUNK]


def _ssd_ins(cfg, tt, xc, p1, dt_raw, dt_bias, a_log, d_skip, gn, tmap):
    ng = cfg.NG
    vec = lambda g, t: (g, 0, 0)
    return [(xc, (tt, SSD_GW), lambda g, t: (tmap(t), g)),
            (xc, (tt, SSD_N), lambda g, t: (tmap(t), 4 * ng + g)),
            (xc, (tt, SSD_N), lambda g, t: (tmap(t), 5 * ng + g)),
            (p1, (tt, SSD_GW), lambda g, t: (tmap(t), g)),
            (dt_raw, (None, tt, LANES), lambda g, t: (g, tmap(t), 0)),
            (dt_bias, (None, 1, LANES), vec), (a_log, (None, 1, LANES), vec), (d_skip, (None, 1, LANES), vec),
            (gn, (1, SSD_GW), lambda g, t: (0, g))]


def ssd_fwd(cfg, xc, p1, dt_raw, dt_bias, a_log, d_skip, gn, name):
    ng, nc, tt = cfg.NG, cfg.S // CHUNK, _ssd_tile(cfg, SSD_FWD_CHUNKS)
    nsub = tt // CHUNK

    def fn(ins, c):
        tiled, params = ins[:SSD_TILED], ins[SSD_TILED:]
        st, outs, entered = c[0], [], []
        for s in range(nsub):
            entered.append(st)
            out, st = _ssd_core(*[_chunk_rows(v, s) for v in tiled], *params, st)
            outs.append(out)
        return [jnp.concatenate(outs, axis=0), jnp.stack(entered)], [], [st]

    return seq_call(name, fn, (ng, cfg.S // tt), _ssd_ins(cfg, tt, xc, p1, dt_raw, dt_bias, a_log, d_skip, gn, lambda t: t),
                    [((cfg.S, cfg.DI), BF16, (tt, SSD_GW), lambda g, t: (t, g)),
                     ((ng, nc, SSD_N, SSD_GW), F32, (None, nsub, SSD_N, SSD_GW), lambda g, t: (g, t, 0, 0))],
                    carries=[(SSD_N, SSD_GW)])


def ssd_bwd(cfg, xc, p1, dt_raw, dt_bias, a_log, d_skip, gn, states, dout, name):
    ng, tt = cfg.NG, _ssd_tile(cfg, SSD_BWD_CHUNKS)
    nsub, nt = tt // CHUNK, cfg.S // tt
    rev = lambda t: nt - 1 - t

    def fn(ins, c):
        tiled, params = ins[:SSD_TILED], ins[SSD_TILED:SSD_TILED + 4]
        st_all, do = ins[SSD_TILED + 4:]
        dst, pieces, acc = c[0], [None] * nsub, None
        for s in reversed(range(nsub)):
            _, vjp = jax.vjp(_ssd_core, *[_chunk_rows(v, s) for v in tiled], *params, st_all[s])
            grads = vjp((_chunk_rows(do, s).astype(F32), dst))
            pieces[s], dparams, dst = grads[:SSD_TILED], grads[SSD_TILED:SSD_TILED + 4], grads[SSD_TILED + 4]
            acc = dparams if acc is None else [x + y for x, y in zip(acc, dparams)]
        return [jnp.concatenate([p[i] for p in pieces], axis=0) for i in range(SSD_TILED)], list(acc), [dst]

    ins = _ssd_ins(cfg, tt, xc, p1, dt_raw, dt_bias, a_log, d_skip, gn, rev)
    ins.append((states, (None, nsub, SSD_N, SSD_GW), lambda g, t: (g, rev(t), 0, 0)))
    ins.append((dout, (tt, SSD_GW), lambda g, t: (rev(t), g)))
    col = lambda g, t: (rev(t), g)
    vec = ((ng, 1, LANES), (None, 1, LANES), lambda g, t: (g, 0, 0))
    return seq_call(name, fn, (ng, nt), ins,
                    [((cfg.S, cfg.DI), F32, (tt, SSD_GW), col),
                     ((cfg.S, ng * SSD_N), F32, (tt, SSD_N), col),
                     ((cfg.S, ng * SSD_N), F32, (tt, SSD_N), col),
                     ((cfg.S, cfg.OP), BF16, (tt, SSD_GW), col),
                     ((ng, cfg.S, LANES), F32, (None, tt, LANES), lambda g, t: (g, rev(t), 0))],
                    accs=[vec, vec, vec, ((1, cfg.DI), (1, SSD_GW), lambda g, t: (0, g))],
                    carries=[(SSD_N, SSD_GW)])


PACK_ALIGN = SUBLANES * LANES
PACK_ROWS = 256


def pack(arrays):
    pieces = []
    for a in arrays:
        flat = a.reshape(-1).astype(F32)
        pad = _round_up(flat.shape[0], PACK_ALIGN) - flat.shape[0]
        pieces.append(jnp.pad(flat, (0, pad)) if pad else flat)
    flat = jnp.concatenate(pieces)
    pad = _round_up(flat.shape[0], PACK_ROWS * LANES) - flat.shape[0]
    return jnp.pad(flat, (0, pad)).reshape(-1, LANES)


def unpack(buf, shapes):
    lead = buf.shape[:-2]
    flat = buf.reshape(lead + (-1,))
    out, off = [], 0
    for s in shapes:
        n = math.prod(s)
        out.append(flat[..., off:off + n].reshape(lead + tuple(s)))
        off += _round_up(n, PACK_ALIGN)
    return out


def _slots_to_cols(slots):
    return slots.transpose(1, 0, 2).reshape(slots.shape[1], -1)


def _even_in_padded(cfg, w):
    main = 2 * cfg.DK + 2 * cfg.DV
    return jnp.concatenate([w[:, :main], w[:, main + GATE_RANK:], w[:, main:main + GATE_RANK],
                            jnp.zeros((w.shape[0], cfg.EP - cfg.EVEN_IN), w.dtype)], axis=1)


def _even_in_unpadded(cfg, wp):
    main = 2 * cfg.DK + 2 * cfg.DV
    rest = main + 2 * cfg.W
    return jnp.concatenate([wp[:, :main], wp[:, rest:rest + GATE_RANK], wp[:, main:rest]], axis=1)


def _odd_in_padded(cfg, w):
    return jnp.concatenate([w, jnp.zeros((w.shape[0], cfg.OP - cfg.ODD_IN), w.dtype)], axis=1)


def _odd_in_unpadded(cfg, wp):
    return wp[:, :cfg.ODD_IN]


def _group_lanes(cfg, v):
    lead = v.shape[:-1]
    g = jnp.moveaxis(v.reshape(lead + (cfg.NG, SSD_HG)), -2, 0)
    return jnp.pad(g, [(0, 0)] * (g.ndim - 1) + [(0, LANES - SSD_HG)])


def _ungroup_lanes(cfg, g):
    v = jnp.moveaxis(g[..., :SSD_HG], 0, -2)
    return v.reshape(v.shape[:-2] + (cfg.NH,))


def train_step(cfg, p, loss_target):
    S, D = cfg.S, cfg.D
    me = 4 * lax.axis_index("x") + 2 * lax.axis_index("y") + lax.axis_index("c")

    big = ["ev_w_in", "ev_w_out", "od_w_in", "od_w_out", "ffn_w_gate", "ffn_w_up", "ffn_w_down"]
    small_sharded = ["ev_gla_w_gate", "ev_lru_conv_w", "od_norm", "od_conv_w", "od_conv_b", "od_gnorm"]
    replicated = ["ev_norm", "ev_gla_b_gate", "ev_gla_w_onorm", "ev_lru_conv_b", "ev_lru_w_a", "ev_lru_b_a",
                  "ev_lru_w_i", "ev_lru_b_i", "ev_lru_lam", "od_dt_bias", "od_a_log", "od_d_skip", "ffn_norm",
                  "final_norm"]

    transposed = ("ffn_w_gate", "ffn_w_up")
    view = lambda n, a: jnp.swapaxes(a, 1, 2) if n in transposed else a
    wb = {n: view(n, p[n]).astype(BF16) for n in big}
    ffn_items = lambda l: [(wb["ffn_w_gate"], l), (wb["ffn_w_up"], l), (wb["ffn_w_down"], l)]
    ss_shapes = [p[n].shape for n in small_sharded]
    groups = [[(pack([p[n] for n in small_sharded]), None), (wb["ev_w_in"], 0), (wb["ev_w_out"], 0)], ffn_items(0),
              [(wb["od_w_in"], 0), (wb["od_w_out"], 0)], ffn_items(1)]
    gathers, tokens = [], []
    for i, g in enumerate(groups):
        lands = [_landing_zone(a, lead, me, f"gather_own{i}_{j}") for j, (a, lead) in enumerate(g)]
        handle, token = exchange_start(g, True, lands, f"gather_start{i}", NEAR_PEERS)
        gathers.append(handle)
        tokens.append(token)
    all_started = tokens[0][:1, :1] + tokens[1][:1, :1] + tokens[2][:1, :1] + tokens[3][:1, :1]

    def gathered(i, after, work):
        return relay(exchange_wait(gathers[i], after, f"gather_wait{i}")[1], f"gather_relay{i}", work)

    x0 = p["x"][0]
    (ss_all, gw_ev_in, gw_ev_out), h0 = gathered(
        0, all_started, lambda token: norm_fwd(cfg, x0, p["ev_norm"] + token[:1, :1], "ev_norm"))
    gs = dict(zip(small_sharded, unpack(ss_all, ss_shapes)))
    w_ev_in = cols_from_slots(gw_ev_in, functools.partial(_even_in_padded, cfg), cfg.EP, "ev_w_in_cols")
    w_ev_out = gw_ev_out.reshape(D, D)

    gla_wg = jnp.pad(_slots_to_cols(gs["ev_gla_w_gate"][:, 0]), ((0, LANES - GATE_RANK), (0, 0)))
    lru_cw = _slots_to_cols(gs["ev_lru_conv_w"][:, 0])
    od_norm = gs["od_norm"].transpose(1, 0, 2).reshape(1, D)
    od_cw = _slots_to_cols(gs["od_conv_w"][:, 0])
    od_cb = gs["od_conv_b"].transpose(1, 0, 2).reshape(1, cfg.CD)
    od_gn = gs["od_gnorm"].transpose(1, 0, 2).reshape(1, cfg.DI)

    target = loss_target[0]
    ev_norm = p["ev_norm"]
    bg = p["ev_gla_b_gate"]
    wn = p["ev_gla_w_onorm"]
    lru_cb = p["ev_lru_conv_b"]
    wa, wi = p["ev_lru_w_a"][0], p["ev_lru_w_i"][0]
    ba, bi, lam = p["ev_lru_b_a"], p["ev_lru_b_i"], p["ev_lru_lam"]
    dt_bias, a_log, d_skip = (_group_lanes(cfg, p[n]) for n in ("od_dt_bias", "od_a_log", "od_d_skip"))
    ffn_norm = [p["ffn_norm"][l:l + 1] for l in range(2)]
    final_norm = p["final_norm"].reshape(1, D)

    def ffn_forward(l, x, weights, next_group):
        w_gate, w_up, w_down = weights
        h = norm_fwd(cfg, x, ffn_norm[l], f"ffn{l}_norm")
        gate, up, act = ffn_gate_up(h, w_gate, w_up, f"ffn{l}_gate_up")
        down = lambda token: matmul(act, w_down, a_slot="k", b_slot="k", res=x, after=token, name=f"ffn{l}_down", tn=512)
        lands, out = (None, down(None)) if next_group is None else gathered(next_group, act, down)
        return out, (h, gate, up, act, w_gate, w_up, w_down), lands

    p0 = matmul(h0, w_ev_in, name="ev_in", tn=768)
    gla_out, gla_states = gla_fwd(cfg, p0, gla_wg, bg, wn, "gla_fwd")
    lru_col = 2 * cfg.DK + 2 * cfg.DV
    lru_xc = conv_fwd(cfg, p0, lru_col, cfg.W, lru_cw, lru_cb, "lru_conv")
    lru_out, lru_states = lru_fwd(cfg, lru_xc, p0, wa, wi, ba, bi, lam, "lru_fwd")
    mix = jnp.concatenate([gla_out, lru_out], axis=1)
    ffn0_weights, x1 = gathered(1, mix, lambda token: matmul(mix, w_ev_out, res=x0, after=token, name="ev_out"))
    x2, ffn0_saved, (gw_od_in, gw_od_out) = ffn_forward(0, x1, ffn0_weights, 2)

    w_od_in = cols_from_slots(gw_od_in, functools.partial(_odd_in_padded, cfg), cfg.OP, "od_w_in_cols")
    w_od_out = gw_od_out.reshape(cfg.DI, D)
    h2 = norm_fwd(cfg, x2, od_norm, "od_norm")
    p1 = matmul(h2, w_od_in, name="od_in", tn=768)
    od_xc = conv_fwd(cfg, p1, cfg.DI, cfg.CD, od_cw, od_cb, "od_conv")
    dt_col = cfg.DI + cfg.CD
    dt_raw = _group_lanes(cfg, p1[:, dt_col:dt_col + cfg.NH])
    ssd_out, ssd_states = ssd_fwd(cfg, od_xc, p1, dt_raw, dt_bias, a_log, d_skip, od_gn, "ssd_fwd")
    ffn1_weights, x3 = gathered(3, ssd_out, lambda token: matmul(ssd_out, w_od_out, res=x2, after=token, name="od_out"))
    x4, ffn1_saved, _ = ffn_forward(1, x3, ffn1_weights, None)

    loss_part, dx4, d_final_norm = head_fwd_bwd(cfg, x4, final_norm, target, "head")
    loss = lax.psum(loss_part[0, 0], MESH_AXES)

    def ffn_backward(l, x, saved, dx_out, after):
        h, gate, up, act, w_gate, w_up, w_down = saved
        dgate, dup = ffn_dgate_dup(dx_out, w_down, gate, up, after, f"ffn{l}_dgate_dup")
        d_down = matmul(act, dx_out, ta=True, a_slot="m", out_dtype=BF16, name=f"ffn{l}_dwdown")
        sent_down, token = start_grads([d_down], f"grads_start_ffn{l}_down")
        d_gate = matmul(dgate, h, ta=True, a_slot="m", after=token, out_dtype=BF16, name=f"ffn{l}_dwgate")
        d_up = matmul(dup, h, ta=True, a_slot="m", out_dtype=BF16, name=f"ffn{l}_dwup")
        sent_gate_up, token = start_grads([d_gate, d_up], f"grads_start_ffn{l}")
        dh = matmul(dgate, w_gate, a_slot="k", b_slot="k", after=token, name=f"ffn{l}_dh_gate", tn=512)
        dh = matmul(dup, w_up, a_slot="k", b_slot="k", res=dh, name=f"ffn{l}_dh_up", tn=512)
        dx, dnorm = norm_bwd(cfg, x, ffn_norm[l], dh, dx_out, f"ffn{l}_norm_bwd")
        return dx, dnorm, (sent_down, sent_gate_up)

    def start_grads(arrays, name):
        return exchange_start([(a, None) for a in arrays], False, [lax.empty(a.shape, a.dtype) for a in arrays], name)

    dx3, d_ffn_norm1, sent_ffn1 = ffn_backward(1, x3, ffn1_saved, dx4, None)

    d_ssd_out = matmul(dx3, w_od_out, tb=True, name="od_dmix")
    d_od_out = matmul(ssd_out, dx3, ta=True, out_dtype=BF16, name="od_dwout")
    dxs, dbm, dcm, dz, d_dt_raw, d_dt_bias, d_a_log, d_d_skip, d_od_gn = ssd_bwd(
        cfg, od_xc, p1, dt_raw, dt_bias, a_log, d_skip, od_gn, ssd_states, d_ssd_out, "ssd_bwd")
    dp1, conv_parts, col = dz, [], 0
    for part, dy in (("x", dxs), ("b", dbm), ("c", dcm)):
        width = dy.shape[1]
        dp1, dcw, dcb = conv_bwd(cfg, p1, cfg.DI + col, width, od_cw[:, col:col + width], dy, "od_conv_bwd_" + part,
                                 into=dp1, into_col0=cfg.DI + col)
        conv_parts.append((dcw, dcb))
        col += width
    d_od_cw = jnp.concatenate([c[0] for c in conv_parts], axis=1)
    d_od_cb = jnp.concatenate([c[1] for c in conv_parts], axis=1)
    d_dt = _ungroup_lanes(cfg, d_dt_raw).astype(BF16)
    tail = jnp.concatenate([d_dt, jnp.zeros((S, cfg.OP - cfg.ODD_IN), BF16)], axis=1)
    dp1 = write_cols(cfg, dp1, tail, cfg.DI + cfg.CD, "od_dt_cols")
    dh2 = matmul(dp1, w_od_in, tb=True, name="od_dh", tk=1536)
    d_od_in = matmul(h2, dp1, ta=True, out_dtype=BF16, name="od_dwin", tn=768)
    dx2, d_od_norm = norm_bwd(cfg, x2, od_norm, dh2, dx3, "od_norm_bwd")
    d_od_in_slots = slots_from_cols(d_od_in, functools.partial(_odd_in_unpadded, cfg), p["od_w_in"].shape[2],
                                    "od_dwin_slots")
    sent_od, token = start_grads([d_od_in_slots, d_od_out.reshape((N_DEV,) + p["od_w_out"].shape[1:])], "grads_start_od")

    dx1, d_ffn_norm0, sent_ffn0 = ffn_backward(0, x1, ffn0_saved, dx2, token)

    d_ev_out = matmul(mix, dx1, ta=True, out_dtype=BF16, name="ev_dwout")
    sent_ev_out, token = start_grads([d_ev_out.reshape((N_DEV,) + p["ev_w_out"].shape[1:])], "grads_start_ev_out")
    d_mix = matmul(dx1, w_ev_out, tb=True, after=token, name="ev_dmix")
    d_qkvg, d_glr, d_gla_wg, d_bg, d_wn = gla_bwd(cfg, p0, gla_wg, bg, wn, gla_states, d_mix, "gla_bwd")
    d_lru_xc, d_gate_br, d_wa, d_wi, d_ba, d_bi, d_lam = lru_bwd(
        cfg, lru_xc, p0, wa, wi, ba, bi, lam, lru_states, d_mix, cfg.DV, "lru_bwd")
    d_xbr, d_lru_cw, d_lru_cb = conv_bwd(cfg, p0, lru_col, cfg.W, lru_cw, d_lru_xc, "lru_conv_bwd")
    dp0 = jnp.concatenate([d_qkvg, d_xbr, d_gate_br, d_glr, jnp.zeros((S, cfg.EP - lru_col - 2 * cfg.W - LANES), BF16)],
                          axis=1)
    d_ev_in = matmul(h0, dp0, ta=True, out_dtype=BF16, name="ev_dwin", tn=768)
    d_ev_in_slots = slots_from_cols(d_ev_in, functools.partial(_even_in_unpadded, cfg), p["ev_w_in"].shape[2],
                                    "ev_dwin_slots")
    sent_ev_in, token = start_grads([d_ev_in_slots], "grads_start_ev_in")
    dh0 = matmul(dp0, w_ev_in, tb=True, after=token, name="ev_dh", tk=1792)
    grad_x, d_ev_norm = norm_bwd(cfg, x0, ev_norm, dh0, dx1, "ev_norm_bwd")

    out = {"loss": loss, "grad_x": grad_x[None]}

    def update(names, sent, after, wait_name):
        s, r = exchange_wait(sent[0], after, wait_name + "0")
        sends, recvs = [[a] for a in s], [[a] for a in r]
        for extra in sent[1:]:
            s, r = exchange_wait(extra, after, wait_name + "1")
            for i in range(len(names)):
                sends[i].append(s[i])
                recvs[i].append(r[i])
        for i, n in enumerate(names):
            flip = n in ("ev_w_in", "od_w_in")
            shard = lambda a: jnp.swapaxes(a, 1, 2) if flip else view(n, a)
            res = adamw_sharded(recvs[i], sends[i], me, shard(p[n]), shard(p["m_" + n]), shard(p["v_" + n]),
                                "adamw_" + n, transposed=flip)
            out["grad_" + n], out["delta_" + n], out["new_m_" + n], out["new_v_" + n] = (shard(a) for a in res)
        return res[-1]

    small_full = {
        "ev_gla_w_gate": d_gla_wg[:GATE_RANK][None], "ev_lru_conv_w": d_lru_cw[None], "od_norm": d_od_norm,
        "od_conv_w": d_od_cw[None], "od_conv_b": d_od_cb, "od_gnorm": d_od_gn,
        "ev_norm": d_ev_norm, "ev_gla_b_gate": d_bg, "ev_gla_w_onorm": d_wn, "ev_lru_conv_b": d_lru_cb,
        "ev_lru_w_a": d_wa[None], "ev_lru_b_a": d_ba, "ev_lru_w_i": d_wi[None], "ev_lru_b_i": d_bi,
        "ev_lru_lam": d_lam, "od_dt_bias": _ungroup_lanes(cfg, d_dt_bias), "od_a_log": _ungroup_lanes(cfg, d_a_log),
        "od_d_skip": _ungroup_lanes(cfg, d_d_skip), "ffn_norm": jnp.concatenate([d_ffn_norm0, d_ffn_norm1], axis=0),
        "final_norm": d_final_norm.reshape(D),
    }
    small = small_sharded + replicated
    small_packed = pack([small_full[n] for n in small])
    sent_small, token = exchange_start([(small_packed, None)], True,
                                       [_landing_zone(small_packed, None, me, "gather_small_grads_own")],
                                       "gather_small_grads")

    done = update(["od_w_in", "od_w_out"], [sent_od], token, "grads_wait_od")
    done = update(["ffn_w_down"], [sent_ffn0[0], sent_ffn1[0]], done, "grads_wait_ffn_down")
    done = update(["ffn_w_gate", "ffn_w_up"], [sent_ffn0[1], sent_ffn1[1]], done, "grads_wait_ffn")
    done = update(["ev_w_out"], [sent_ev_out], done, "grads_wait_ev_out")
    done = update(["ev_w_in"], [sent_ev_in], done, "grads_wait_ev_in")

    small_all = exchange_wait(sent_small, done, "gather_small_grads_wait")[1][0]
    g_small = dict(zip(small, unpack(reduce_slots(small_all, "sum_small_grads"), [small_full[n].shape for n in small])))
    for n in small_sharded:
        width = p[n].shape[-1]
        g_small[n] = lax.dynamic_slice_in_dim(g_small[n], me * width, width, axis=g_small[n].ndim - 1)
    shapes = [p[n].shape for n in small]
    g_buf = pack([g_small[n] for n in small])
    delta, new_m, new_v = adamw(pack([p[n] for n in small]), g_buf, pack([p["m_" + n] for n in small]),
                                pack([p["v_" + n] for n in small]), "adamw_small")
    for kind, buf in (("grad_", g_buf), ("delta_", delta), ("new_m_", new_m), ("new_v_", new_v)):
        for n, a in zip(small, unpack(buf, shapes)):
            out[kind + n] = a
    return out


WEIGHTS = ['ev_norm', 'ev_w_in', 'ev_gla_w_gate', 'ev_gla_b_gate', 'ev_gla_w_onorm', 'ev_lru_conv_w', 'ev_lru_conv_b',
           'ev_lru_w_a', 'ev_lru_b_a', 'ev_lru_w_i', 'ev_lru_b_i', 'ev_lru_lam', 'ev_w_out', 'od_norm', 'od_w_in',
           'od_conv_w', 'od_conv_b', 'od_dt_bias', 'od_a_log', 'od_d_skip', 'od_gnorm', 'od_w_out', 'ffn_norm',
           'ffn_w_gate', 'ffn_w_up', 'ffn_w_down', 'final_norm']


def kernel(x, ev_norm, ev_w_in, ev_gla_w_gate, ev_gla_b_gate, ev_gla_w_onorm, ev_lru_conv_w, ev_lru_conv_b, ev_lru_w_a, ev_lru_b_a, ev_lru_w_i, ev_lru_b_i, ev_lru_lam, ev_w_out, od_norm, od_w_in, od_conv_w, od_conv_b, od_dt_bias, od_a_log, od_d_skip, od_gnorm, od_w_out, ffn_norm, ffn_w_gate, ffn_w_up, ffn_w_down, final_norm, loss_target, m_ev_norm, m_ev_w_in, m_ev_gla_w_gate, m_ev_gla_b_gate, m_ev_gla_w_onorm, m_ev_lru_conv_w, m_ev_lru_conv_b, m_ev_lru_w_a, m_ev_lru_b_a, m_ev_lru_w_i, m_ev_lru_b_i, m_ev_lru_lam, m_ev_w_out, m_od_norm, m_od_w_in, m_od_conv_w, m_od_conv_b, m_od_dt_bias, m_od_a_log, m_od_d_skip, m_od_gnorm, m_od_w_out, m_ffn_norm, m_ffn_w_gate, m_ffn_w_up, m_ffn_w_down, m_final_norm, v_ev_norm, v_ev_w_in, v_ev_gla_w_gate, v_ev_gla_b_gate, v_ev_gla_w_onorm, v_ev_lru_conv_w, v_ev_lru_conv_b, v_ev_lru_w_a, v_ev_lru_b_a, v_ev_lru_w_i, v_ev_lru_b_i, v_ev_lru_lam, v_ev_w_out, v_od_norm, v_od_w_in, v_od_conv_w, v_od_conv_b, v_od_dt_bias, v_od_a_log, v_od_d_skip, v_od_gnorm, v_od_w_out, v_ffn_norm, v_ffn_w_gate, v_ffn_w_up, v_ffn_w_down, v_final_norm):
    args = dict(locals())
    p = {n: a for n, a in args.items() if n != "loss_target"}
    cfg = Cfg(S=x.shape[1], D=x.shape[2], DFF=ffn_w_gate.shape[2] * N_DEV)
    out = train_step(cfg, p, loss_target)
    return (out["loss"], out["grad_x"], *[out["grad_" + w] for w in WEIGHTS], *[out["delta_" + w] for w in WEIGHTS],
            *[out["new_m_" + w] for w in WEIGHTS], *[out["new_v_" + w] for w in WEIGHTS])
```

```python
import functools
import math
from typing import NamedTuple

import jax
import jax.numpy as jnp
from jax import lax
from jax.experimental import pallas as pl
from jax.experimental.pallas import tpu as pltpu

F32 = jnp.float32
BF16 = jnp.bfloat16
MESH_AXES = ("x", "y", "c")
N_DEV = 8
LANES = 128
SUBLANES = 8
VMEM_LIMIT = 56 * 1024 * 1024

NORM_EPS = 1e-6
CONV_WIDTH = 4
CHUNK = 64
HEAD_K = 128
HEAD_V = 256
GATE_RANK = 16
GATE_NORM = 16.0
LRU_BLOCK = 128
LRU_C = 8.0
SSD_P = 64
SSD_N = 128
SSD_HG = 8
SSD_GW = SSD_HG * SSD_P

ADAM_LR = 0.001
ADAM_B1 = 0.9
ADAM_B2 = 0.999
ADAM_EPS = 1e-08
ADAM_WD = 0.01
ADAM_STEP = 10


class Cfg(NamedTuple):
    S: int
    D: int
    DFF: int

    @property
    def GH(self):
        return self.D // 512

    @property
    def NB(self):
        return self.D // 256

    @property
    def NG(self):
        return self.D // 256

    @property
    def DK(self):
        return HEAD_K * self.GH

    @property
    def DV(self):
        return HEAD_V * self.GH

    @property
    def W(self):
        return LRU_BLOCK * self.NB

    @property
    def DI(self):
        return SSD_GW * self.NG

    @property
    def CD(self):
        return self.DI + 2 * self.NG * SSD_N

    @property
    def NH(self):
        return SSD_HG * self.NG

    @property
    def EVEN_IN(self):
        return 2 * self.DK + 2 * self.DV + GATE_RANK + 2 * self.W

    @property
    def ODD_IN(self):
        return self.DI + self.CD + self.NH

    @property
    def EP(self):
        return _round_up(2 * self.DK + 2 * self.DV + 2 * self.W + LANES, 768)

    @property
    def OP(self):
        return _round_up(self.DI + self.CD + LANES, 768)


def _round_up(n, m):
    return (n + m - 1) // m * m


def _tile(n, pref):
    if n <= pref:
        return n
    t = pref - pref % LANES
    while n % t:
        t -= LANES
    return t


def _cparams(n_axes):
    return pltpu.CompilerParams(dimension_semantics=("arbitrary",) * n_axes, vmem_limit_bytes=VMEM_LIMIT)


def _dg(a, b, ca, cb):
    return lax.dot_general(a.astype(BF16), b.astype(BF16), (((ca,), (cb,)), ((), ())), preferred_element_type=F32)


@functools.partial(jax.custom_vjp, nondiff_argnums=(2, 3))
def bdot(a, b, ca, cb):
    return _dg(a, b, ca, cb)


def _bdot_fwd(a, b, ca, cb):
    return _dg(a, b, ca, cb), (a, b)


def _bdot_bwd(ca, cb, res, g):
    a, b = res
    da = _dg(g, b, 1, 1 - cb) if ca == 1 else _dg(b, g, 1 - cb, 1)
    db = _dg(a, g, 1 - ca, 0) if cb == 0 else _dg(g, a, 0, 1 - ca)
    return da.astype(a.dtype), db.astype(b.dtype)


bdot.defvjp(_bdot_fwd, _bdot_bwd)


def _lower_tri(n):
    r = lax.broadcasted_iota(jnp.int32, (n, n), 0)
    c = lax.broadcasted_iota(jnp.int32, (n, n), 1)
    return c <= r


def _running_sum(x, reverse):
    n = x.shape[0]
    r = lax.broadcasted_iota(jnp.int32, x.shape, 0)
    d = 1
    while d < n:
        if reverse:
            x = x + jnp.where(r < n - d, pltpu.roll(x, n - d, 0), 0.0)
        else:
            x = x + jnp.where(r >= d, pltpu.roll(x, d, 0), 0.0)
        d *= 2
    return x


@jax.custom_vjp
def cumsum_rows(x):
    return _running_sum(x, False)


cumsum_rows.defvjp(lambda x: (_running_sum(x, False), None), lambda _, g: (_running_sum(g, True),))


def _tri_dot(x, transposed):
    n = x.shape[0]
    return lax.dot_general(_lower_tri(n).astype(F32), x, (((0 if transposed else 1,), (0,)), ((), ())),
                           precision=lax.Precision.HIGHEST, preferred_element_type=F32)


@jax.custom_vjp
def cumsum_rows_mxu(x):
    return _tri_dot(x, False)


cumsum_rows_mxu.defvjp(lambda x: (_tri_dot(x, False), None), lambda _, g: (_tri_dot(g, True),))


def _row(x, i):
    r = lax.broadcasted_iota(jnp.int32, x.shape, 0)
    return jnp.sum(jnp.where(r == i, x, 0.0), axis=0, keepdims=True)


def _softplus_raw(x):
    return jnp.maximum(x, 0.0) + jnp.log(1.0 + jnp.exp(-jnp.abs(x)))


@jax.custom_vjp
def softplus(x):
    return _softplus_raw(x)


softplus.defvjp(lambda x: (_softplus_raw(x), x), lambda x, g: (g * jax.nn.sigmoid(x),))


@jax.custom_vjp
def log_sigmoid(x):
    return -_softplus_raw(-x)


log_sigmoid.defvjp(lambda x: (-_softplus_raw(-x), x), lambda x, g: (g * jax.nn.sigmoid(-x),))


def silu(x):
    return x * jax.nn.sigmoid(x)


def gelu_tanh(x):
    return 0.5 * x * (1.0 + jnp.tanh(math.sqrt(2.0 / math.pi) * (x + 0.044715 * (x * x * x))))


def _expm1(x):
    series = x * (1.0 + 0.5 * x * (1.0 + (1.0 / 3.0) * x))
    return jnp.where(jnp.abs(x) < 1e-2, series, jnp.exp(x) - 1.0)


def rms(x, w):
    return x * lax.rsqrt(jnp.mean(x * x, axis=-1, keepdims=True) + NORM_EPS) * w


def _rows_iota(shape):
    return lax.broadcasted_iota(jnp.int32, shape, 0)


def _scan_up(a, u):
    n = a.shape[0]
    r = _rows_iota(a.shape)
    d = 1
    while d < n:
        m = r >= d
        a_s = jnp.where(m, pltpu.roll(a, d, 0), 1.0)
        u_s = jnp.where(m, pltpu.roll(u, d, 0), 0.0)
        u = a * u_s + u
        a = a * a_s
        d *= 2
    return u


def _scan_down(a, u):
    n = a.shape[0]
    r = _rows_iota(a.shape)
    d = 1
    while d < n:
        m = r < n - d
        a_s = jnp.where(m, pltpu.roll(a, n - d, 0), 1.0)
        u_s = jnp.where(m, pltpu.roll(u, n - d, 0), 0.0)
        u = a * u_s + u
        a = a * a_s
        d *= 2
    return u


@jax.custom_vjp
def lin_scan(a, u):
    return _scan_up(a, u)


def _lin_scan_fwd(a, u):
    h = _scan_up(a, u)
    return h, (a, h)


def _lin_scan_bwd(res, g):
    a, h = res
    n = a.shape[0]
    r = _rows_iota(a.shape)
    a_next = jnp.where(r < n - 1, pltpu.roll(a, n - 1, 0), 0.0)
    gt = _scan_down(a_next, g)
    h_prev = jnp.where(r >= 1, pltpu.roll(h, 1, 0), 0.0)
    return gt * h_prev, gt


lin_scan.defvjp(_lin_scan_fwd, _lin_scan_bwd)


def _expand_heads(v):
    r = v.shape[0]
    return jnp.concatenate([jnp.broadcast_to(v[:, h:h + 1], (r, SSD_P)) for h in range(SSD_HG)], axis=1)


@jax.custom_vjp
def _split_heads(x):
    return tuple(x[:, h * SSD_P:(h + 1) * SSD_P] for h in range(SSD_HG))


_split_heads.defvjp(lambda x: (_split_heads(x), None), lambda _, gs: (jnp.concatenate(gs, axis=1),))


def matmul(a, b, *, ta=False, tb=False, a_slot=None, b_slot=None, res=None, after=None, out_dtype=F32,
           name, tm=1024, tn=1024, tk=2048):
    ra, ca_ = a.shape[-2:]
    rb, cb_ = b.shape[-2:]
    m_st, ka_st = (ca_, ra) if ta else (ra, ca_)
    kb_st, n_st = (cb_, rb) if tb else (rb, cb_)
    kslot = a_slot == "k"
    assert kslot == (b_slot == "k")
    m = m_st * (N_DEV if a_slot == "m" else 1)
    n = n_st * (N_DEV if b_slot == "n" else 1)
    assert ka_st == kb_st, (a.shape, b.shape, ta, tb)
    tm = m_st if a_slot == "m" else _tile(m, tm)
    tn = n_st if b_slot == "n" else _tile(n, tn)
    tk = ka_st if kslot else _tile(ka_st, tk)
    nk = ka_st // tk
    ca, cb = (0 if ta else 1), (1 if tb else 0)

    if a_slot is None:
        a_spec = pl.BlockSpec((tk, tm), lambda i, j, k: (k, i)) if ta else pl.BlockSpec((tm, tk), lambda i, j, k: (i, k))
    elif a_slot == "m":
        a_spec = (pl.BlockSpec((None, tk, tm), lambda i, j, k: (i, k, 0)) if ta
                  else pl.BlockSpec((None, tm, tk), lambda i, j, k: (i, 0, k)))
    else:
        a_spec = (pl.BlockSpec((N_DEV, tk, tm), lambda i, j, k: (0, 0, i)) if ta
                  else pl.BlockSpec((N_DEV, tm, tk), lambda i, j, k: (0, i, 0)))
    if b_slot is None:
        b_spec = pl.BlockSpec((tn, tk), lambda i, j, k: (j, k)) if tb else pl.BlockSpec((tk, tn), lambda i, j, k: (k, j))
    elif b_slot == "n":
        b_spec = (pl.BlockSpec((None, tn, tk), lambda i, j, k: (j, 0, k)) if tb
                  else pl.BlockSpec((None, tk, tn), lambda i, j, k: (j, k, 0)))
    else:
        b_spec = (pl.BlockSpec((N_DEV, tn, tk), lambda i, j, k: (0, j, 0)) if tb
                  else pl.BlockSpec((N_DEV, tk, tn), lambda i, j, k: (0, 0, j)))
    if a_slot == "m":
        o_spec, o_shape = pl.BlockSpec((None, tm, tn), lambda i, j, k: (i, 0, j)), (N_DEV, tm, n)
    elif b_slot == "n":
        o_spec, o_shape = pl.BlockSpec((None, tm, tn), lambda i, j, k: (j, i, 0)), (N_DEV, m, tn)
    else:
        o_spec, o_shape = pl.BlockSpec((tm, tn), lambda i, j, k: (i, j)), (m, n)
    assert res is None or (a_slot != "m" and b_slot != "n")

    def dot(x, y):
        return lax.dot_general(x.astype(BF16), y.astype(BF16), (((ca,), (cb,)), ((), ())), preferred_element_type=F32)

    def body(*refs):
        a_ref, b_ref = refs[:2]
        r_ref = refs[2] if res is not None else None
        o_ref = refs[2 + (res is not None) + (after is not None)]

        def finish(acc):
            if r_ref is not None:
                acc = acc + r_ref[...].astype(F32)
            o_ref[...] = acc.astype(o_ref.dtype)

        if kslot:
            acc = dot(a_ref[0], b_ref[0])
            for s in range(1, N_DEV):
                acc = acc + dot(a_ref[s], b_ref[s])
            finish(acc)
        elif nk == 1:
            finish(dot(a_ref[...], b_ref[...]))
        else:
            acc_ref = refs[-1]
            k = pl.program_id(2)

            @pl.when(k == 0)
            def _():
                acc_ref[...] = dot(a_ref[...], b_ref[...])

            @pl.when(k > 0)
            def _():
                acc_ref[...] += dot(a_ref[...], b_ref[...])

            @pl.when(k == nk - 1)
            def _():
                finish(acc_ref[...])

    in_specs = [a_spec, b_spec]
    args = [a, b]
    if res is not None:
        in_specs.append(pl.BlockSpec((tm, tn), lambda i, j, k: (i, j)))
        args.append(res)
    if after is not None:
        in_specs.append(pl.BlockSpec(memory_space=pl.ANY))
        args.append(after)
    return pl.pallas_call(
        body, name=name, grid=(m // tm, n // tn, nk), in_specs=in_specs, out_specs=o_spec,
        out_shape=jax.ShapeDtypeStruct(o_shape, out_dtype),
        scratch_shapes=[pltpu.VMEM((tm, tn), F32)] if nk > 1 else [], compiler_params=_cparams(3),
    )(*args)


def seq_call(name, fn, grid, ins, outs, accs=(), carries=()):
    n_in, n_out, n_acc = len(ins), len(outs), len(accs)

    def body(*refs):
        in_refs = refs[:n_in]
        out_refs = refs[n_in:n_in + n_out]
        acc_refs = refs[n_in + n_out:n_in + n_out + n_acc]
        c_refs = refs[n_in + n_out + n_acc:]

        if acc_refs or c_refs:
            @pl.when(pl.program_id(1) == 0)
            def _():
                for r in tuple(acc_refs) + tuple(c_refs):
                    r[...] = jnp.zeros_like(r)

        o, a, c = fn([r[...] for r in in_refs], [r[...] for r in c_refs])
        for r, v in zip(out_refs, o, strict=True):
            r[...] = v.astype(r.dtype)
        for r, v in zip(acc_refs, a, strict=True):
            r[...] += v
        for r, v in zip(c_refs, c, strict=True):
            r[...] = v

    return pl.pallas_call(
        body, name=name, grid=grid,
        in_specs=[pl.BlockSpec(blk, im) for _, blk, im in ins],
        out_specs=[pl.BlockSpec(blk, im) for _, _, blk, im in outs] + [pl.BlockSpec(blk, im) for _, blk, im in accs],
        out_shape=[jax.ShapeDtypeStruct(s, d) for s, d, _, _ in outs] + [jax.ShapeDtypeStruct(s, F32) for s, _, _ in accs],
        scratch_shapes=[pltpu.VMEM(s, F32) for s in carries], compiler_params=_cparams(2),
    )(*[a for a, _, _ in ins])


_HBM = pl.BlockSpec(memory_space=pltpu.HBM)
_SEM = pl.BlockSpec(memory_space=pltpu.SEMAPHORE)


def _mesh_pos():
    pos = [lax.axis_index(ax) for ax in MESH_AXES]
    return pos, 4 * pos[0] + 2 * pos[1] + pos[2]


ALL_PEERS = (1, 2, 3, 4, 5, 6, 7)
NEAR_PEERS = (1, 2, 4, 6)
RELAYED = (2, 4, 6)


def _peers(pos, masks=ALL_PEERS):
    out = []
    for k in masks:
        bits = ((k >> 2) & 1, (k >> 1) & 1, k & 1)
        peer = tuple(1 - p if b else p for p, b in zip(pos, bits))
        out.append((peer, 4 * peer[0] + 2 * peer[1] + peer[2]))
    return out


def _part(x_ref, lead, gather, slot):
    ref = x_ref if lead is None else x_ref.at[lead]
    return ref if gather else ref.at[slot]


OWN_BLOCK_BYTES = 2 * 1024 * 1024


def _landing_zone(a, lead, me, name):
    r, c = a.shape[-2:]
    tr = r
    while tr * _round_up(c, LANES) * a.dtype.itemsize > OWN_BLOCK_BYTES and tr % 32 == 0:
        tr //= 2

    def body(me_ref, x_ref, o_ref):
        o_ref[...] = x_ref[...]

    x_spec = (pl.BlockSpec((tr, c), lambda i, me_ref: (i, 0)) if lead is None
              else pl.BlockSpec((None, tr, c), lambda i, me_ref: (lead, i, 0)))
    grid_spec = pltpu.PrefetchScalarGridSpec(
        num_scalar_prefetch=1, grid=(r // tr,), in_specs=[x_spec],
        out_specs=pl.BlockSpec((None, tr, c), lambda i, me_ref: (me_ref[0], i, 0)))
    return pl.pallas_call(body, name=name, grid_spec=grid_spec, out_shape=jax.ShapeDtypeStruct((N_DEV, r, c), a.dtype),
                          compiler_params=_cparams(1))(jnp.reshape(me, (1,)).astype(jnp.int32), a)


def _split_copies(items, gather, masks, x_refs, land_refs, send_sems, recv_sems):
    pos, me = _mesh_pos()
    copies = []
    for i, (_, lead) in enumerate(items):
        for k, (peer, peer_id) in enumerate(_peers(pos, masks)):
            copies.append(pltpu.make_async_remote_copy(
                src_ref=_part(x_refs[i], lead, gather, peer_id), dst_ref=land_refs[i].at[me],
                send_sem=send_sems.at[i * len(masks) + k], recv_sem=recv_sems.at[i * len(masks) + k],
                device_id=peer, device_id_type=pl.DeviceIdType.MESH))
    return copies


_SPLIT_CALL = dict(compiler_params=pltpu.CompilerParams(has_side_effects=pltpu.SideEffectType.DATAFLOW_SIDE_EFFECTING))


def exchange_start(items, gather, lands, name, masks=ALL_PEERS):
    n = len(items)
    lands = list(lands)
    xs = [a for a, _ in items]

    def body(*refs):
        x_refs, land_refs = refs[:n], refs[n:2 * n]
        send_sems, recv_sems, token = refs[2 * n], refs[2 * n + 1], refs[-1]
        for cp in _split_copies(items, gather, masks, x_refs, land_refs, send_sems, recv_sems):
            cp.start()
        token[...] = jnp.zeros_like(token)

    outs = pl.pallas_call(
        body, name=name,
        out_shape=(pltpu.SemaphoreType.DMA((n * len(masks),)), pltpu.SemaphoreType.DMA((n * len(masks),)),
                   *[pltpu.HBM(v.shape, v.dtype) for v in xs + lands], jax.ShapeDtypeStruct((SUBLANES, LANES), F32)),
        in_specs=[_HBM] * (2 * n),
        out_specs=(_SEM, _SEM, *[_HBM] * (2 * n), pl.BlockSpec(memory_space=pltpu.VMEM)),
        input_output_aliases={i: 2 + i for i in range(2 * n)}, **_SPLIT_CALL,
    )(*[pltpu.with_memory_space_constraint(v, pltpu.HBM) for v in xs + lands])
    handle = (items, gather, masks, outs[0], outs[1], outs[2:2 + n], outs[2 + n:2 + 2 * n])
    return handle, outs[-1]


def _relay_copies(n, land_refs, send_sems, recv_sems):
    pos, _ = _mesh_pos()
    sibling = (pos[0], pos[1], 1 - pos[2])
    copies = []
    for i in range(n):
        for k, (_, peer_id) in enumerate(_peers(pos, RELAYED)):
            slot = land_refs[i].at[peer_id]
            copies.append(pltpu.make_async_remote_copy(
                src_ref=slot, dst_ref=slot, send_sem=send_sems.at[i * len(RELAYED) + k],
                recv_sem=recv_sems.at[i * len(RELAYED) + k], device_id=sibling, device_id_type=pl.DeviceIdType.MESH))
    return copies


def relay(lands, name, work=None):
    n = len(lands)
    sems = n * len(RELAYED)

    def start(*refs):
        for cp in _relay_copies(n, refs[:n], refs[n], refs[n + 1]):
            cp.start()
        refs[-1][...] = jnp.zeros_like(refs[-1])

    outs = pl.pallas_call(
        start, name=name + "_start",
        out_shape=(pltpu.SemaphoreType.DMA((sems,)), pltpu.SemaphoreType.DMA((sems,)),
                   *[pltpu.HBM(v.shape, v.dtype) for v in lands], jax.ShapeDtypeStruct((SUBLANES, LANES), F32)),
        in_specs=[_HBM] * n, out_specs=(_SEM, _SEM, *[_HBM] * n, pl.BlockSpec(memory_space=pltpu.VMEM)),
        input_output_aliases={i: 2 + i for i in range(n)},
        **_SPLIT_CALL)(*[pltpu.with_memory_space_constraint(v, pltpu.HBM) for v in lands])

    def wait(*refs):
        for cp in _relay_copies(n, refs[:n], refs[n], refs[n + 1]):
            cp.wait_send()
            cp.wait_recv()

    done = None if work is None else work(outs[-1])
    extra = [] if work is None else [jax.tree.leaves(done)[0]]
    filled = list(pl.pallas_call(
        wait, name=name + "_wait", out_shape=tuple(pltpu.HBM(v.shape, v.dtype) for v in lands),
        in_specs=[_HBM] * n + [_SEM, _SEM] + [pl.BlockSpec(memory_space=pl.ANY)] * len(extra),
        out_specs=tuple([_HBM] * n), input_output_aliases={i: i for i in range(n)},
        **_SPLIT_CALL)(*outs[2:2 + n], outs[0], outs[1], *extra))
    return filled if work is None else (filled, done)


def exchange_wait(handle, after, name):
    items, gather, masks, send_sems, recv_sems, x_thru, land_thru = handle
    n = len(items)

    def body(*refs):
        x_refs, land_refs = refs[:n], refs[n:2 * n]
        for cp in _split_copies(items, gather, masks, x_refs, land_refs, refs[2 * n], refs[2 * n + 1]):
            cp.wait_send()
            cp.wait_recv()

    outs = pl.pallas_call(
        body, name=name, out_shape=tuple(pltpu.HBM(v.shape, v.dtype) for v in tuple(x_thru) + tuple(land_thru)),
        in_specs=[_HBM] * (2 * n) + [_SEM, _SEM, pl.BlockSpec(memory_space=pl.ANY)], out_specs=tuple([_HBM] * (2 * n)),
        input_output_aliases={i: i for i in range(2 * n)},
        compiler_params=pltpu.CompilerParams(has_side_effects=pltpu.SideEffectType.DATAFLOW_SIDE_EFFECTING),
    )(*x_thru, *land_thru, send_sems, recv_sems, after)
    return list(outs[:n]), list(outs[n:])


def _adam_update(w, g, m, v):
    nm = ADAM_B1 * m + (1.0 - ADAM_B1) * g
    nv = ADAM_B2 * v + (1.0 - ADAM_B2) * (g * g)
    m_hat = nm / (1.0 - ADAM_B1 ** ADAM_STEP)
    v_hat = nv / (1.0 - ADAM_B2 ** ADAM_STEP)
    return -ADAM_LR * (m_hat / (jnp.sqrt(v_hat) + ADAM_EPS) + ADAM_WD * w), nm, nv


def _sum_slots(s_ref):
    acc = s_ref[0].astype(F32)
    for j in range(1, N_DEV):
        acc = acc + s_ref[j].astype(F32)
    return acc


ADAM_BLOCK_BYTES = 10 * 1024 * 1024


def adamw_sharded(recvs, sends, me, w, m, v, name, transposed=False):
    nl = w.shape[0]
    r, c = recvs[0].shape[1:]
    assert w.shape[1:] == ((c, r) if transposed else (r, c))
    assert len(recvs) == nl and len(sends) == nl
    per_row = _round_up(c, LANES) * (nl * (N_DEV + 1) * recvs[0].dtype.itemsize + 7 * 4)
    tr = r
    while tr * per_row > ADAM_BLOCK_BYTES and tr % 16 == 0:
        tr //= 2

    def body(me_ref, *refs):
        s_refs, o_refs = refs[:nl], refs[nl:2 * nl]
        w_ref, m_ref, v_ref, g_ref, d_ref, nm_ref, nv_ref = refs[2 * nl:]
        mine = me_ref[0]

        def total(l):
            acc = jnp.where(mine == 0, o_refs[l][...], s_refs[l][0]).astype(F32)
            for j in range(1, N_DEV):
                acc = acc + jnp.where(mine == j, o_refs[l][...], s_refs[l][j]).astype(F32)
            return acc

        g = total(0)
        for l in range(1, nl):
            g = jnp.where(pl.program_id(0) == l, total(l), g)
        if transposed:
            g = g.T
        g_ref[...] = g
        d_ref[...], nm_ref[...], nv_ref[...] = _adam_update(w_ref[...], g, m_ref[...], v_ref[...])

    spec = (pl.BlockSpec((None, c, tr), lambda l, i, me_ref: (l, 0, i)) if transposed
            else pl.BlockSpec((None, tr, c), lambda l, i, me_ref: (l, i, 0)))
    shp = jax.ShapeDtypeStruct(w.shape, F32)
    grid_spec = pltpu.PrefetchScalarGridSpec(
        num_scalar_prefetch=1, grid=(nl, r // tr),
        in_specs=([pl.BlockSpec((N_DEV, tr, c), lambda l, i, me_ref: (0, i, 0))] * nl
                  + [pl.BlockSpec((None, tr, c), lambda l, i, me_ref: (me_ref[0], i, 0))] * nl + [spec, spec, spec]),
        out_specs=[spec] * 4)
    return pl.pallas_call(body, name=name, grid_spec=grid_spec, out_shape=[shp] * 4, compiler_params=_cparams(2))(
        jnp.reshape(me, (1,)).astype(jnp.int32), *recvs, *sends, w, m, v)


def reduce_slots(slots, name):
    _, r, lanes = slots.shape
    tr = _tile(r, 2048)

    def body(s_ref, o_ref):
        o_ref[...] = _sum_slots(s_ref)

    return pl.pallas_call(
        body, name=name, grid=(r // tr,),
        in_specs=[pl.BlockSpec((N_DEV, tr, lanes), lambda i: (0, i, 0))],
        out_specs=pl.BlockSpec((tr, lanes), lambda i: (i, 0)),
        out_shape=jax.ShapeDtypeStruct((r, lanes), F32), compiler_params=_cparams(1),
    )(slots)


def adamw(w, g, m, v, name):
    r, lanes = w.shape
    tr = _tile(r, 2048)

    def body(w_ref, g_ref, m_ref, v_ref, d_ref, nm_ref, nv_ref):
        d_ref[...], nm_ref[...], nv_ref[...] = _adam_update(w_ref[...], g_ref[...], m_ref[...], v_ref[...])

    spec = pl.BlockSpec((tr, lanes), lambda i: (i, 0))
    shp = jax.ShapeDtypeStruct((r, lanes), F32)
    return pl.pallas_call(body, name=name, grid=(r // tr,), in_specs=[spec] * 4, out_specs=[spec] * 3,
                          out_shape=[shp] * 3, compiler_params=_cparams(1))(w, g, m, v)


def cols_from_slots(slots, place, width, name):
    _, rows, c = slots.shape
    tr = _tile(rows, 256)

    def body(s_ref, o_ref):
        o_ref[...] = place(jnp.concatenate([s_ref[j] for j in range(N_DEV)], axis=1))

    return pl.pallas_call(
        body, name=name, grid=(rows // tr,),
        in_specs=[pl.BlockSpec((N_DEV, tr, c), lambda i: (0, i, 0))],
        out_specs=pl.BlockSpec((tr, width), lambda i: (i, 0)),
        out_shape=jax.ShapeDtypeStruct((rows, width), slots.dtype), compiler_params=_cparams(1))(slots)


def slots_from_cols(full, pick, c, name):
    rows, wide = full.shape
    tr = _tile(rows, 256)

    def body(x_ref, o_ref):
        v = pick(x_ref[...])
        for j in range(N_DEV):
            o_ref[j] = v[:, j * c:(j + 1) * c]

    return pl.pallas_call(
        body, name=name, grid=(rows // tr,),
        in_specs=[pl.BlockSpec((tr, wide), lambda i: (i, 0))],
        out_specs=pl.BlockSpec((N_DEV, tr, c), lambda i: (0, i, 0)),
        out_shape=jax.ShapeDtypeStruct((N_DEV, rows, c), full.dtype), compiler_params=_cparams(1))(full)


def _token_tile(cfg):
    return min(cfg.S, 256)


def norm_fwd(cfg, x, w, name):
    ts = _token_tile(cfg)
    d = x.shape[1]

    def fn(ins, _):
        xv, wv = ins
        return [rms(xv, wv)], [], []

    return seq_call(name, fn, (1, cfg.S // ts),
                    [(x, (ts, d), lambda g, t: (t, 0)), (w, (1, d), lambda g, t: (0, 0))],
                    [((cfg.S, d), BF16, (ts, d), lambda g, t: (t, 0))])[0]


class Grad(NamedTuple):
    f32: jax.Array
    bf16: jax.Array


def norm_bwd(cfg, x, w, dh, dres, name, with_bf16=True):
    ts = _token_tile(cfg)
    d = x.shape[1]

    def fn(ins, _):
        xv, wv, dhv, drv = ins
        _, vjp = jax.vjp(rms, xv, wv)
        dx, dw = vjp(dhv.astype(F32))
        return [dx + drv] * (2 if with_bf16 else 1), [dw], []

    row = lambda g, t: (t, 0)
    out = seq_call(name, fn, (1, cfg.S // ts),
                   [(x, (ts, d), row), (w, (1, d), lambda g, t: (0, 0)), (dh, (ts, d), row), (dres, (ts, d), row)],
                   [((cfg.S, d), dt, (ts, d), row) for dt in ((F32, BF16) if with_bf16 else (F32,))],
                   accs=[((1, d), (1, d), lambda g, t: (0, 0))])
    return Grad(out[0], out[1] if with_bf16 else None), out[-1]


def head_fwd_bwd(cfg, x, w, target, name):
    ts = _token_tile(cfg)
    d = x.shape[1]

    def fn(ins, _):
        xv, wv, tv = ins
        y, vjp = jax.vjp(rms, xv, wv)
        err = y - tv
        loss = 0.5 * jnp.sum(err * err) / d
        dx, dw = vjp(err / d)
        return [dx, dx], [jnp.full((SUBLANES, LANES), loss, F32), dw], []

    row = lambda g, t: (t, 0)
    fixed = lambda g, t: (0, 0)
    dx, dx_bf16, loss, dw = seq_call(name, fn, (1, cfg.S // ts),
                                     [(x, (ts, d), row), (w, (1, d), fixed), (target, (ts, d), row)],
                                     [((cfg.S, d), F32, (ts, d), row), ((cfg.S, d), BF16, (ts, d), row)],
                                     accs=[((SUBLANES, LANES), (SUBLANES, LANES), fixed), ((1, d), (1, d), fixed)])
    return loss, Grad(dx, dx_bf16), dw


FFN_ROWS = 1024


def ffn_gate_up(h, w_gate, w_up, name):
    s, d = h.shape
    c = w_gate.shape[1]
    tm = _tile(s, FFN_ROWS)

    def body(h_ref, wg_ref, wu_ref, g_ref, u_ref, a_ref):
        hv = h_ref[...]
        g = _dg(hv, wg_ref[...], 1, 1)
        u = _dg(hv, wu_ref[...], 1, 1)
        g_ref[...] = g
        u_ref[...] = u
        a_ref[...] = (silu(g) * u).astype(a_ref.dtype)

    w_spec = pl.BlockSpec((None, c, d), lambda i, j: (j, 0, 0))
    o_spec = pl.BlockSpec((None, tm, c), lambda i, j: (j, i, 0))
    shp = (N_DEV, s, c)
    return pl.pallas_call(
        body, name=name, grid=(s // tm, N_DEV), in_specs=[pl.BlockSpec((tm, d), lambda i, j: (i, 0)), w_spec, w_spec],
        out_specs=[o_spec] * 3,
        out_shape=[jax.ShapeDtypeStruct(shp, F32), jax.ShapeDtypeStruct(shp, F32), jax.ShapeDtypeStruct(shp, BF16)],
        compiler_params=_cparams(2))(h, w_gate, w_up)


def ffn_dgate_dup(dx, w_down, gate, up, after, name):
    s, d = dx.shape
    c = w_down.shape[1]
    tm = _tile(s, FFN_ROWS)
    extra = [] if after is None else [after]

    def body(dx_ref, wd_ref, g_ref, u_ref, *rest):
        dg_ref, du_ref = rest[len(extra):]
        dact = _dg(dx_ref[...], wd_ref[...], 1, 1)
        _, vjp = jax.vjp(lambda a, b: silu(a) * b, g_ref[...], u_ref[...])
        dg, du = vjp(dact)
        dg_ref[...] = dg.astype(dg_ref.dtype)
        du_ref[...] = du.astype(du_ref.dtype)

    blk = pl.BlockSpec((None, tm, c), lambda i, j: (j, i, 0))
    shp = jax.ShapeDtypeStruct((N_DEV, s, c), BF16)
    return pl.pallas_call(
        body, name=name, grid=(s // tm, N_DEV),
        in_specs=[pl.BlockSpec((tm, d), lambda i, j: (i, 0)), pl.BlockSpec((None, c, d), lambda i, j: (j, 0, 0)), blk, blk]
        + [pl.BlockSpec(memory_space=pl.ANY)] * len(extra),
        out_specs=[blk, blk], out_shape=[shp, shp], compiler_params=_cparams(2))(dx, w_down, gate, up, *extra)


CONV_COLS = (512, 256)
HALO = SUBLANES


def _conv_cols(col0, width):
    return next(c for c in CONV_COLS if col0 % c == 0 and width % c == 0)


def _shift_down(x, halo, j):
    if j == 0:
        return x
    r8 = _rows_iota(halo.shape)
    top = jnp.where(r8 >= j, pltpu.roll(x[:HALO], j, 0), pltpu.roll(halo, j, 0))
    return jnp.concatenate([top, pltpu.roll(x, j, 0)[HALO:]], axis=0)


def _shift_up(x, halo, j):
    if j == 0:
        return x
    n = x.shape[0]
    r8 = _rows_iota(halo.shape)
    bot = jnp.where(r8 < HALO - j, pltpu.roll(x[n - HALO:], HALO - j, 0), pltpu.roll(halo, HALO - j, 0))
    return jnp.concatenate([pltpu.roll(x, n - j, 0)[:n - HALO], bot], axis=0)


def _conv_tile(cfg):
    return min(cfg.S, 1024)


def conv_fwd(cfg, src, col0, width, w, b, name):
    tt, cb = _conv_tile(cfg), _conv_cols(col0, width)
    c0, hb = col0 // cb, tt // HALO
    nt = cfg.S // tt

    def body(x_ref, h_ref, w_ref, b_ref, o_ref):
        t = pl.program_id(1)
        x = x_ref[...]
        halo = jnp.where(t > 0, h_ref[...], 0.0)
        wv = w_ref[...]
        acc = b_ref[...] + wv[CONV_WIDTH - 1:CONV_WIDTH] * x
        for j in range(1, CONV_WIDTH):
            acc = acc + wv[CONV_WIDTH - 1 - j:CONV_WIDTH - j] * _shift_down(x, halo, j)
        o_ref[...] = acc

    return pl.pallas_call(
        body, name=name, grid=(width // cb, nt),
        in_specs=[pl.BlockSpec((tt, cb), lambda c, t: (t, c0 + c)),
                  pl.BlockSpec((HALO, cb), lambda c, t: (jnp.maximum(t * hb - 1, 0), c0 + c)),
                  pl.BlockSpec((CONV_WIDTH, cb), lambda c, t: (0, c)),
                  pl.BlockSpec((1, cb), lambda c, t: (0, c))],
        out_specs=pl.BlockSpec((tt, cb), lambda c, t: (t, c)),
        out_shape=jax.ShapeDtypeStruct((cfg.S, width), F32), compiler_params=_cparams(2),
    )(src, src, w, b)


def conv_bwd(cfg, src, col0, width, w, dy, name, into=None, into_col0=0):
    tt, cb = _conv_tile(cfg), _conv_cols(col0, width)
    c0, hb = col0 // cb, tt // HALO
    nt = cfg.S // tt
    extra = [] if into is None else [into]
    assert into_col0 % cb == 0
    o0 = into_col0 // cb

    def body(x_ref, h_ref, w_ref, dy_ref, dh_ref, *rest):
        dx_ref, dw_ref, db_ref = rest[len(extra):]
        t = pl.program_id(1)

        @pl.when(t == 0)
        def _():
            dw_ref[...] = jnp.zeros_like(dw_ref)
            db_ref[...] = jnp.zeros_like(db_ref)

        x = x_ref[...]
        halo = jnp.where(t > 0, h_ref[...], 0.0)
        dy = dy_ref[...]
        dhalo = jnp.where(t < nt - 1, dh_ref[...], 0.0)
        wv = w_ref[...]
        dx = wv[CONV_WIDTH - 1:CONV_WIDTH] * dy
        rows = [jnp.sum(dy * x, axis=0, keepdims=True)]
        for j in range(1, CONV_WIDTH):
            dx = dx + wv[CONV_WIDTH - 1 - j:CONV_WIDTH - j] * _shift_up(dy, dhalo, j)
            rows.insert(0, jnp.sum(dy * _shift_down(x, halo, j), axis=0, keepdims=True))
        dx_ref[...] = dx.astype(dx_ref.dtype)
        dw_ref[...] += jnp.concatenate(rows, axis=0)
        db_ref[...] += jnp.sum(dy, axis=0, keepdims=True)

    return pl.pallas_call(
        body, name=name, grid=(width // cb, nt),
        in_specs=[pl.BlockSpec((tt, cb), lambda c, t: (t, c0 + c)),
                  pl.BlockSpec((HALO, cb), lambda c, t: (jnp.maximum(t * hb - 1, 0), c0 + c)),
                  pl.BlockSpec((CONV_WIDTH, cb), lambda c, t: (0, c)),
                  pl.BlockSpec((tt, cb), lambda c, t: (t, c)),
                  pl.BlockSpec((HALO, cb), lambda c, t: (jnp.minimum((t + 1) * hb, nt * hb - 1), c))]
        + [pl.BlockSpec(memory_space=pl.ANY)] * len(extra),
        out_specs=[pl.BlockSpec((tt, cb), lambda c, t: (t, o0 + c)),
                   pl.BlockSpec((CONV_WIDTH, cb), lambda c, t: (0, c)),
                   pl.BlockSpec((1, cb), lambda c, t: (0, c))],
        out_shape=[jax.ShapeDtypeStruct((cfg.S, width) if into is None else into.shape, BF16),
                   jax.ShapeDtypeStruct((CONV_WIDTH, width), F32), jax.ShapeDtypeStruct((1, width), F32)],
        input_output_aliases={5: 0} if extra else {}, compiler_params=_cparams(2),
    )(src, src, w, dy, dy, *extra)


def write_cols(cfg, into, piece, col0, name):
    tt, width = _conv_tile(cfg), piece.shape[1]
    assert col0 % width == 0 and into.dtype == piece.dtype

    def body(p_ref, _, o_ref):
        o_ref[...] = p_ref[...]

    return pl.pallas_call(
        body, name=name, grid=(cfg.S // tt,),
        in_specs=[pl.BlockSpec((tt, width), lambda t: (t, 0)), pl.BlockSpec(memory_space=pl.ANY)],
        out_specs=pl.BlockSpec((tt, width), lambda t: (t, col0 // width)),
        out_shape=jax.ShapeDtypeStruct(into.shape, into.dtype), input_output_aliases={1: 0},
        compiler_params=_cparams(1))(piece, into)


def _gla_core(gh, q, k, v, g, glr, wg, bg, wn, st):
    n = glr.shape[0]
    causal = _lower_tri(n)
    outs, new = [], []
    for h in range(gh):
        log_a = log_sigmoid(bdot(glr, wg[h], 1, 0) + bg[h]) * (1.0 / GATE_NORM)
        bcum = cumsum_rows(log_a)
        b_last, b_mid = _row(bcum, n - 1), _row(bcum, n // 2)
        qs = q[h] * (HEAD_K ** -0.5)
        scores = jnp.where(causal, bdot(qs * jnp.exp(bcum - b_mid), k[h] * jnp.exp(b_mid - bcum), 1, 1), 0.0)
        o = bdot(scores, v[h], 1, 0) + bdot(qs * jnp.exp(bcum), st[h], 1, 1)
        new.append(st[h] * jnp.exp(b_last) + bdot(v[h], k[h] * jnp.exp(b_last - bcum), 0, 0))
        outs.append(rms(o, wn) * silu(g[h]))
    return jnp.concatenate(outs, axis=1), new


def _gla_ins(cfg, p0, wg, bg, wn, tmap):
    gh = cfg.GH
    ins = []
    for h in range(gh):
        ins.append((p0, (CHUNK, HEAD_K), lambda g, t, h=h: (tmap(t), h)))
    for h in range(gh):
        ins.append((p0, (CHUNK, HEAD_K), lambda g, t, h=h: (tmap(t), gh + h)))
    for h in range(gh):
        ins.append((p0, (CHUNK, HEAD_V), lambda g, t, h=h: (tmap(t), gh + h)))
    for h in range(gh):
        ins.append((p0, (CHUNK, HEAD_V), lambda g, t, h=h: (tmap(t), 2 * gh + h)))
    ins.append((p0, (CHUNK, LANES), lambda g, t: (tmap(t), 10 * gh)))
    for h in range(gh):
        ins.append((wg, (LANES, HEAD_K), lambda g, t, h=h: (0, h)))
    for h in range(gh):
        ins.append((bg, (1, HEAD_K), lambda g, t, h=h: (0, h)))
    ins.append((wn, (1, HEAD_V), lambda g, t: (0, 0)))
    return ins


def _gla_unpack(gh, vals):
    q, k, v, g = (vals[i * gh:(i + 1) * gh] for i in range(4))
    glr = vals[4 * gh]
    wg = vals[4 * gh + 1:5 * gh + 1]
    bg = vals[5 * gh + 1:6 * gh + 1]
    wn = vals[6 * gh + 1]
    return q, k, v, g, glr, wg, bg, wn, vals[6 * gh + 2:]


def gla_fwd(cfg, p0, wg, bg, wn, name):
    gh, nc = cfg.GH, cfg.S // CHUNK

    def fn(ins, st):
        q, k, v, g, glr, wgv, bgv, wnv, _ = _gla_unpack(gh, ins)
        out, new = _gla_core(gh, q, k, v, g, glr, wgv, bgv, wnv, st)
        return [out, jnp.stack(st)], [], new

    return seq_call(name, fn, (1, nc), _gla_ins(cfg, p0, wg, bg, wn, lambda t: t),
                    [((cfg.S, cfg.DV), BF16, (CHUNK, cfg.DV), lambda g, t: (t, 0)),
                     ((nc, gh, HEAD_V, HEAD_K), F32, (None, gh, HEAD_V, HEAD_K), lambda g, t: (t, 0, 0, 0))],
                    carries=[(HEAD_V, HEAD_K)] * gh)


def gla_bwd(cfg, p0, wg, bg, wn, states, dout, name):
    gh, nc = cfg.GH, cfg.S // CHUNK
    rev = lambda t: nc - 1 - t

    def fn(ins, dst):
        q, k, v, g, glr, wgv, bgv, wnv, rest = _gla_unpack(gh, ins)
        st_all, do = rest
        st = [st_all[h] for h in range(gh)]
        _, vjp = jax.vjp(functools.partial(_gla_core, gh), q, k, v, g, glr, wgv, bgv, wnv, st)
        dq, dk, dv, dg, dglr, dwg, dbg, dwn, dstate = vjp((do.astype(F32), list(dst)))
        return ([jnp.concatenate(list(dq) + list(dk) + list(dv) + list(dg), axis=1), dglr],
                [jnp.concatenate(dwg, axis=1), jnp.concatenate(dbg, axis=1), dwn], dstate)

    ins = _gla_ins(cfg, p0, wg, bg, wn, rev)
    ins.append((states, (None, gh, HEAD_V, HEAD_K), lambda g, t: (rev(t), 0, 0, 0)))
    ins.append((dout, (CHUNK, cfg.DV), lambda g, t: (rev(t), 0)))
    wide = 2 * cfg.DK + 2 * cfg.DV
    fixed = lambda g, t: (0, 0)
    return seq_call(name, fn, (1, nc), ins,
                    [((cfg.S, wide), BF16, (CHUNK, wide), lambda g, t: (rev(t), 0)),
                     ((cfg.S, LANES), BF16, (CHUNK, LANES), lambda g, t: (rev(t), 0))],
                    accs=[((LANES, cfg.DK), (LANES, cfg.DK), fixed), ((1, cfg.DK), (1, cfg.DK), fixed),
                          ((1, HEAD_V), (1, HEAD_V), fixed)],
                    carries=[(HEAD_V, HEAD_K)] * gh)


def _lru_core(xc, gate, wa, wi, ba, bi, lam, h_in):
    r = jax.nn.sigmoid(bdot(xc, wa, 1, 0) + ba)
    i = jax.nn.sigmoid(bdot(xc, wi, 1, 0) + bi)
    log_a = LRU_C * r * log_sigmoid(lam)
    a = jnp.exp(log_a)
    u = jnp.sqrt(-_expm1(2.0 * log_a)) * (i * xc)
    first = _rows_iota(a.shape) == 0
    h = lin_scan(a, u + jnp.where(first, a * h_in, 0.0))
    return h * gelu_tanh(gate), _row(h, a.shape[0] - 1)


def _lru_tile(cfg):
    return min(cfg.S, 512)


def _lru_ins(cfg, xc, p0, wa, wi, ba, bi, lam, tmap):
    tt, gh = _lru_tile(cfg), cfg.GH
    vec = lambda g, t: (0, g)
    return [(xc, (tt, LRU_BLOCK), lambda g, t: (tmap(t), g)),
            (p0, (tt, LRU_BLOCK), lambda g, t: (tmap(t), 8 * gh + g)),
            (wa, (None, LRU_BLOCK, LRU_BLOCK), lambda g, t: (g, 0, 0)),
            (wi, (None, LRU_BLOCK, LRU_BLOCK), lambda g, t: (g, 0, 0)),
            (ba, (1, LRU_BLOCK), vec), (bi, (1, LRU_BLOCK), vec), (lam, (1, LRU_BLOCK), vec)]


def lru_fwd(cfg, xc, p0, wa, wi, ba, bi, lam, name):
    tt, nb = _lru_tile(cfg), cfg.NB
    nt = cfg.S // tt

    def fn(ins, c):
        out, h_last = _lru_core(*ins, c[0])
        return [out, c[0]], [], [h_last]

    return seq_call(name, fn, (nb, nt), _lru_ins(cfg, xc, p0, wa, wi, ba, bi, lam, lambda t: t),
                    [((cfg.S, cfg.W), BF16, (tt, LRU_BLOCK), lambda g, t: (t, g)),
                     ((nb, nt, 1, LRU_BLOCK), F32, (None, None, 1, LRU_BLOCK), lambda g, t: (g, t, 0, 0))],
                    carries=[(1, LRU_BLOCK)])


def lru_bwd(cfg, xc, p0, wa, wi, ba, bi, lam, states, dout, dout_col0, name):
    tt, nb = _lru_tile(cfg), cfg.NB
    nt = cfg.S // tt
    rev = lambda t: nt - 1 - t

    def fn(ins, c):
        *fwd_ins, h_in, do = ins
        _, vjp = jax.vjp(_lru_core, *fwd_ins, h_in)
        dxc, dgate, dwa, dwi, dba, dbi, dlam, dh = vjp((do.astype(F32), c[0]))
        return [dxc, dgate], [dwa, dwi, dba, dbi, dlam], [dh]

    ins = _lru_ins(cfg, xc, p0, wa, wi, ba, bi, lam, rev)
    ins.append((states, (None, None, 1, LRU_BLOCK), lambda g, t: (g, rev(t), 0, 0)))
    ins.append((dout, (tt, LRU_BLOCK), lambda g, t: (rev(t), dout_col0 // LRU_BLOCK + g)))
    mat = ((nb, LRU_BLOCK, LRU_BLOCK), (None, LRU_BLOCK, LRU_BLOCK), lambda g, t: (g, 0, 0))
    vec = ((1, cfg.W), (1, LRU_BLOCK), lambda g, t: (0, g))
    return seq_call(name, fn, (nb, nt), ins,
                    [((cfg.S, cfg.W), F32, (tt, LRU_BLOCK), lambda g, t: (rev(t), g)),
                     ((cfg.S, cfg.W), BF16, (tt, LRU_BLOCK), lambda g, t: (rev(t), g))],
                    accs=[mat, mat, vec, vec, vec], carries=[(1, LRU_BLOCK)])


def _ssd_core(xc, bc, cc, z, dt_raw, dt_bias, a_log, d_skip, gn, st):
    n = xc.shape[0]
    x, bm, cm = silu(xc), silu(bc), silu(cc)
    dt = softplus(dt_raw + dt_bias)
    acs = cumsum_rows_mxu(dt * (-jnp.exp(a_log)))
    acs_t = acs.T
    acs_e, dt_e = _expand_heads(acs), _expand_heads(dt)
    last_e = _expand_heads(_row(acs, n - 1))
    causal = _lower_tri(n)
    cb = bdot(cm, bm, 1, 1)
    xdt = x * dt_e
    y_diag = []
    for h, xh in enumerate(_split_heads(xdt)):
        seg = acs[:, h:h + 1] - acs_t[h:h + 1, :]
        decay = jnp.where(causal, jnp.exp(jnp.minimum(seg, 0.0)), 0.0)
        y_diag.append(bdot(cb * decay, xh, 1, 0))
    y = jnp.concatenate(y_diag, axis=1) + bdot(cm, st, 1, 0) * jnp.exp(acs_e)
    new = st * jnp.exp(last_e) + bdot(bm, xdt * jnp.exp(last_e - acs_e), 0, 0)
    y = (y + _expand_heads(d_skip) * x) * silu(z)
    return rms(y, gn), new


SSD_TILED = 5


SSD_FWD_CHUNKS = 8
SSD_BWD_CHUNKS = 1


def _ssd_tile(cfg, chunks):
    return min(cfg.S, chunks * CHUNK)


def _chunk_rows(v, s):
    return v[s * CHUNK:(s + 1) * CHUNK]


def _ssd_ins(cfg, tt, xc, p1, dt_raw, dt_bias, a_log, d_skip, gn, tmap):
    ng = cfg.NG
    vec = lambda g, t: (g, 0, 0)
    return [(xc, (tt, SSD_GW), lambda g, t: (tmap(t), g)),
            (xc, (tt, SSD_N), lambda g, t: (tmap(t), 4 * ng + g)),
            (xc, (tt, SSD_N), lambda g, t: (tmap(t), 5 * ng + g)),
            (p1, (tt, SSD_GW), lambda g, t: (tmap(t), g)),
            (dt_raw, (None, tt, LANES), lambda g, t: (g, tmap(t), 0)),
            (dt_bias, (None, 1, LANES), vec), (a_log, (None, 1, LANES), vec), (d_skip, (None, 1, LANES), vec),
            (gn, (1, SSD_GW), lambda g, t: (0, g))]


def ssd_fwd(cfg, xc, p1, dt_raw, dt_bias, a_log, d_skip, gn, name):
    ng, nc, tt = cfg.NG, cfg.S // CHUNK, _ssd_tile(cfg, SSD_FWD_CHUNKS)
    nsub = tt // CHUNK

    def fn(ins, c):
        tiled, params = ins[:SSD_TILED], ins[SSD_TILED:]
        st, outs, entered = c[0], [], []
        for s in range(nsub):
            entered.append(st)
            out, st = _ssd_core(*[_chunk_rows(v, s) for v in tiled], *params, st)
            outs.append(out)
        return [jnp.concatenate(outs, axis=0), jnp.stack(entered)], [], [st]

    return seq_call(name, fn, (ng, cfg.S // tt), _ssd_ins(cfg, tt, xc, p1, dt_raw, dt_bias, a_log, d_skip, gn, lambda t: t),
                    [((cfg.S, cfg.DI), BF16, (tt, SSD_GW), lambda g, t: (t, g)),
                     ((ng, nc, SSD_N, SSD_GW), F32, (None, nsub, SSD_N, SSD_GW), lambda g, t: (g, t, 0, 0))],
                    carries=[(SSD_N, SSD_GW)])


def ssd_bwd(cfg, xc, p1, dt_raw, dt_bias, a_log, d_skip, gn, states, dout, name):
    ng, tt = cfg.NG, _ssd_tile(cfg, SSD_BWD_CHUNKS)
    nsub, nt = tt // CHUNK, cfg.S // tt
    rev = lambda t: nt - 1 - t

    def fn(ins, c):
        tiled, params = ins[:SSD_TILED], ins[SSD_TILED:SSD_TILED + 4]
        st_all, do = ins[SSD_TILED + 4:]
        dst, pieces, acc = c[0], [None] * nsub, None
        for s in reversed(range(nsub)):
            _, vjp = jax.vjp(_ssd_core, *[_chunk_rows(v, s) for v in tiled], *params, st_all[s])
            grads = vjp((_chunk_rows(do, s).astype(F32), dst))
            pieces[s], dparams, dst = grads[:SSD_TILED], grads[SSD_TILED:SSD_TILED + 4], grads[SSD_TILED + 4]
            acc = dparams if acc is None else [x + y for x, y in zip(acc, dparams)]
        return [jnp.concatenate([p[i] for p in pieces], axis=0) for i in range(SSD_TILED)], list(acc), [dst]

    ins = _ssd_ins(cfg, tt, xc, p1, dt_raw, dt_bias, a_log, d_skip, gn, rev)
    ins.append((states, (None, nsub, SSD_N, SSD_GW), lambda g, t: (g, rev(t), 0, 0)))
    ins.append((dout, (tt, SSD_GW), lambda g, t: (rev(t), g)))
    col = lambda g, t: (rev(t), g)
    vec = ((ng, 1, LANES), (None, 1, LANES), lambda g, t: (g, 0, 0))
    return seq_call(name, fn, (ng, nt), ins,
                    [((cfg.S, cfg.DI), F32, (tt, SSD_GW), col),
                     ((cfg.S, ng * SSD_N), F32, (tt, SSD_N), col),
                     ((cfg.S, ng * SSD_N), F32, (tt, SSD_N), col),
                     ((cfg.S, cfg.OP), BF16, (tt, SSD_GW), col),
                     ((ng, cfg.S, LANES), F32, (None, tt, LANES), lambda g, t: (g, rev(t), 0))],
                    accs=[vec, vec, vec, ((1, cfg.DI), (1, SSD_GW), lambda g, t: (0, g))],
                    carries=[(SSD_N, SSD_GW)])


PACK_ALIGN = SUBLANES * LANES
PACK_ROWS = 256


def pack(arrays):
    pieces = []
    for a in arrays:
        flat = a.reshape(-1).astype(F32)
        pad = _round_up(flat.shape[0], PACK_ALIGN) - flat.shape[0]
        pieces.append(jnp.pad(flat, (0, pad)) if pad else flat)
    flat = jnp.concatenate(pieces)
    pad = _round_up(flat.shape[0], PACK_ROWS * LANES) - flat.shape[0]
    return jnp.pad(flat, (0, pad)).reshape(-1, LANES)


def unpack(buf, shapes):
    lead = buf.shape[:-2]
    flat = buf.reshape(lead + (-1,))
    out, off = [], 0
    for s in shapes:
        n = math.prod(s)
        out.append(flat[..., off:off + n].reshape(lead + tuple(s)))
        off += _round_up(n, PACK_ALIGN)
    return out


def _slots_to_cols(slots):
    return slots.transpose(1, 0, 2).reshape(slots.shape[1], -1)


def _even_in_padded(cfg, w):
    main = 2 * cfg.DK + 2 * cfg.DV
    return jnp.concatenate([w[:, :main], w[:, main + GATE_RANK:], w[:, main:main + GATE_RANK],
                            jnp.zeros((w.shape[0], cfg.EP - cfg.EVEN_IN), w.dtype)], axis=1)


def _even_in_unpadded(cfg, wp):
    main = 2 * cfg.DK + 2 * cfg.DV
    rest = main + 2 * cfg.W
    return jnp.concatenate([wp[:, :main], wp[:, rest:rest + GATE_RANK], wp[:, main:rest]], axis=1)


def _odd_in_padded(cfg, w):
    return jnp.concatenate([w, jnp.zeros((w.shape[0], cfg.OP - cfg.ODD_IN), w.dtype)], axis=1)


def _odd_in_unpadded(cfg, wp):
    return wp[:, :cfg.ODD_IN]


def _group_lanes(cfg, v):
    lead = v.shape[:-1]
    g = jnp.moveaxis(v.reshape(lead + (cfg.NG, SSD_HG)), -2, 0)
    return jnp.pad(g, [(0, 0)] * (g.ndim - 1) + [(0, LANES - SSD_HG)])


def _ungroup_lanes(cfg, g):
    v = jnp.moveaxis(g[..., :SSD_HG], 0, -2)
    return v.reshape(v.shape[:-2] + (cfg.NH,))


def train_step(cfg, p, loss_target):
    S, D = cfg.S, cfg.D
    me = 4 * lax.axis_index("x") + 2 * lax.axis_index("y") + lax.axis_index("c")

    big = ["ev_w_in", "ev_w_out", "od_w_in", "od_w_out", "ffn_w_gate", "ffn_w_up", "ffn_w_down"]
    small_sharded = ["ev_gla_w_gate", "ev_lru_conv_w", "od_norm", "od_conv_w", "od_conv_b", "od_gnorm"]
    replicated = ["ev_norm", "ev_gla_b_gate", "ev_gla_w_onorm", "ev_lru_conv_b", "ev_lru_w_a", "ev_lru_b_a",
                  "ev_lru_w_i", "ev_lru_b_i", "ev_lru_lam", "od_dt_bias", "od_a_log", "od_d_skip", "ffn_norm",
                  "final_norm"]

    transposed = ("ffn_w_gate", "ffn_w_up")
    view = lambda n, a: jnp.swapaxes(a, 1, 2) if n in transposed else a
    wb = {n: view(n, p[n]).astype(BF16) for n in big}
    ffn_items = lambda l: [(wb["ffn_w_gate"], l), (wb["ffn_w_up"], l), (wb["ffn_w_down"], l)]
    ss_shapes = [p[n].shape for n in small_sharded]
    groups = [[(pack([p[n] for n in small_sharded]), None), (wb["ev_w_in"], 0), (wb["ev_w_out"], 0)], ffn_items(0),
              [(wb["od_w_in"], 0), (wb["od_w_out"], 0)], ffn_items(1)]
    gathers, tokens = [], []
    for i, g in enumerate(groups):
        lands = [_landing_zone(a, lead, me, f"gather_own{i}_{j}") for j, (a, lead) in enumerate(g)]
        handle, token = exchange_start(g, True, lands, f"gather_start{i}", NEAR_PEERS)
        gathers.append(handle)
        tokens.append(token)
    all_started = tokens[0][:1, :1] + tokens[1][:1, :1] + tokens[2][:1, :1] + tokens[3][:1, :1]

    def gathered(i, after, work):
        return relay(exchange_wait(gathers[i], after, f"gather_wait{i}")[1], f"gather_relay{i}", work)

    x0 = p["x"][0]
    (ss_all, gw_ev_in, gw_ev_out), h0 = gathered(
        0, all_started, lambda token: norm_fwd(cfg, x0, p["ev_norm"] + token[:1, :1], "ev_norm"))
    gs = dict(zip(small_sharded, unpack(ss_all, ss_shapes)))
    w_ev_in = cols_from_slots(gw_ev_in, functools.partial(_even_in_padded, cfg), cfg.EP, "ev_w_in_cols")
    w_ev_out = gw_ev_out.reshape(D, D)

    gla_wg = jnp.pad(_slots_to_cols(gs["ev_gla_w_gate"][:, 0]), ((0, LANES - GATE_RANK), (0, 0)))
    lru_cw = _slots_to_cols(gs["ev_lru_conv_w"][:, 0])
    od_norm = gs["od_norm"].transpose(1, 0, 2).reshape(1, D)
    od_cw = _slots_to_cols(gs["od_conv_w"][:, 0])
    od_cb = gs["od_conv_b"].transpose(1, 0, 2).reshape(1, cfg.CD)
    od_gn = gs["od_gnorm"].transpose(1, 0, 2).reshape(1, cfg.DI)

    target = loss_target[0]
    ev_norm = p["ev_norm"]
    bg = p["ev_gla_b_gate"]
    wn = p["ev_gla_w_onorm"]
    lru_cb = p["ev_lru_conv_b"]
    wa, wi = p["ev_lru_w_a"][0], p["ev_lru_w_i"][0]
    ba, bi, lam = p["ev_lru_b_a"], p["ev_lru_b_i"], p["ev_lru_lam"]
    dt_bias, a_log, d_skip = (_group_lanes(cfg, p[n]) for n in ("od_dt_bias", "od_a_log", "od_d_skip"))
    ffn_norm = [p["ffn_norm"][l:l + 1] for l in range(2)]
    final_norm = p["final_norm"].reshape(1, D)

    def ffn_forward(l, x, weights, next_group):
        w_gate, w_up, w_down = weights
        h = norm_fwd(cfg, x, ffn_norm[l], f"ffn{l}_norm")
        gate, up, act = ffn_gate_up(h, w_gate, w_up, f"ffn{l}_gate_up")
        down = lambda token: matmul(act, w_down, a_slot="k", b_slot="k", res=x, after=token, name=f"ffn{l}_down", tn=512)
        lands, out = (None, down(None)) if next_group is None else gathered(next_group, act, down)
        return out, (h, gate, up, act, w_gate, w_up, w_down), lands

    p0 = matmul(h0, w_ev_in, name="ev_in", tn=768)
    gla_out, gla_states = gla_fwd(cfg, p0, gla_wg, bg, wn, "gla_fwd")
    lru_col = 2 * cfg.DK + 2 * cfg.DV
    lru_xc = conv_fwd(cfg, p0, lru_col, cfg.W, lru_cw, lru_cb, "lru_conv")
    lru_out, lru_states = lru_fwd(cfg, lru_xc, p0, wa, wi, ba, bi, lam, "lru_fwd")
    mix = jnp.concatenate([gla_out, lru_out], axis=1)
    ffn0_weights, x1 = gathered(1, mix, lambda token: matmul(mix, w_ev_out, res=x0, after=token, name="ev_out"))
    x2, ffn0_saved, (gw_od_in, gw_od_out) = ffn_forward(0, x1, ffn0_weights, 2)

    w_od_in = cols_from_slots(gw_od_in, functools.partial(_odd_in_padded, cfg), cfg.OP, "od_w_in_cols")
    w_od_out = gw_od_out.reshape(cfg.DI, D)
    h2 = norm_fwd(cfg, x2, od_norm, "od_norm")
    p1 = matmul(h2, w_od_in, name="od_in", tn=768)
    od_xc = conv_fwd(cfg, p1, cfg.DI, cfg.CD, od_cw, od_cb, "od_conv")
    dt_col = cfg.DI + cfg.CD
    dt_raw = _group_lanes(cfg, p1[:, dt_col:dt_col + cfg.NH])
    ssd_out, ssd_states = ssd_fwd(cfg, od_xc, p1, dt_raw, dt_bias, a_log, d_skip, od_gn, "ssd_fwd")
    ffn1_weights, x3 = gathered(3, ssd_out, lambda token: matmul(ssd_out, w_od_out, res=x2, after=token, name="od_out"))
    x4, ffn1_saved, _ = ffn_forward(1, x3, ffn1_weights, None)

    loss_part, dx4, d_final_norm = head_fwd_bwd(cfg, x4, final_norm, target, "head")
    loss = lax.psum(loss_part[0, 0], MESH_AXES)

    def ffn_backward(l, x, saved, dx_out, after):
        h, gate, up, act, w_gate, w_up, w_down = saved
        dgate, dup = ffn_dgate_dup(dx_out.bf16, w_down, gate, up, after, f"ffn{l}_dgate_dup")
        d_down = matmul(act, dx_out.bf16, ta=True, a_slot="m", out_dtype=BF16, name=f"ffn{l}_dwdown")
        sent_down, token = start_grads([d_down], f"grads_start_ffn{l}_down")
        d_gate = matmul(dgate, h, ta=True, a_slot="m", after=token, out_dtype=BF16, name=f"ffn{l}_dwgate")
        d_up = matmul(dup, h, ta=True, a_slot="m", out_dtype=BF16, name=f"ffn{l}_dwup")
        sent_gate_up, token = start_grads([d_gate, d_up], f"grads_start_ffn{l}")
        dh = matmul(dgate, w_gate, a_slot="k", b_slot="k", after=token, name=f"ffn{l}_dh_gate", tn=512)
        dh = matmul(dup, w_up, a_slot="k", b_slot="k", res=dh, name=f"ffn{l}_dh_up", tn=512)
        dx, dnorm = norm_bwd(cfg, x, ffn_norm[l], dh, dx_out.f32, f"ffn{l}_norm_bwd")
        return dx, dnorm, (sent_down, sent_gate_up)

    def start_grads(arrays, name):
        return exchange_start([(a, None) for a in arrays], False, [lax.empty(a.shape, a.dtype) for a in arrays], name)

    dx3, d_ffn_norm1, sent_ffn1 = ffn_backward(1, x3, ffn1_saved, dx4, None)

    d_ssd_out = matmul(dx3.bf16, w_od_out, tb=True, name="od_dmix")
    d_od_out = matmul(ssd_out, dx3.bf16, ta=True, out_dtype=BF16, name="od_dwout")
    dxs, dbm, dcm, dz, d_dt_raw, d_dt_bias, d_a_log, d_d_skip, d_od_gn = ssd_bwd(
        cfg, od_xc, p1, dt_raw, dt_bias, a_log, d_skip, od_gn, ssd_states, d_ssd_out, "ssd_bwd")
    dp1, conv_parts, col = dz, [], 0
    for part, dy in (("x", dxs), ("b", dbm), ("c", dcm)):
        width = dy.shape[1]
        dp1, dcw, dcb = conv_bwd(cfg, p1, cfg.DI + col, width, od_cw[:, col:col + width], dy, "od_conv_bwd_" + part,
                                 into=dp1, into_col0=cfg.DI + col)
        conv_parts.append((dcw, dcb))
        col += width
    d_od_cw = jnp.concatenate([c[0] for c in conv_parts], axis=1)
    d_od_cb = jnp.concatenate([c[1] for c in conv_parts], axis=1)
    d_dt = _ungroup_lanes(cfg, d_dt_raw).astype(BF16)
    tail = jnp.concatenate([d_dt, jnp.zeros((S, cfg.OP - cfg.ODD_IN), BF16)], axis=1)
    dp1 = write_cols(cfg, dp1, tail, cfg.DI + cfg.CD, "od_dt_cols")
    dh2 = matmul(dp1, w_od_in, tb=True, name="od_dh", tk=1536)
    d_od_in = matmul(h2, dp1, ta=True, out_dtype=BF16, name="od_dwin", tn=768)
    dx2, d_od_norm = norm_bwd(cfg, x2, od_norm, dh2, dx3.f32, "od_norm_bwd")
    d_od_in_slots = slots_from_cols(d_od_in, functools.partial(_odd_in_unpadded, cfg), p["od_w_in"].shape[2],
                                    "od_dwin_slots")
    sent_od, token = start_grads([d_od_in_slots, d_od_out.reshape((N_DEV,) + p["od_w_out"].shape[1:])], "grads_start_od")

    dx1, d_ffn_norm0, sent_ffn0 = ffn_backward(0, x1, ffn0_saved, dx2, token)

    d_ev_out = matmul(mix, dx1.bf16, ta=True, out_dtype=BF16, name="ev_dwout")
    sent_ev_out, token = start_grads([d_ev_out.reshape((N_DEV,) + p["ev_w_out"].shape[1:])], "grads_start_ev_out")
    d_mix = matmul(dx1.bf16, w_ev_out, tb=True, after=token, name="ev_dmix")
    d_qkvg, d_glr, d_gla_wg, d_bg, d_wn = gla_bwd(cfg, p0, gla_wg, bg, wn, gla_states, d_mix, "gla_bwd")
    d_lru_xc, d_gate_br, d_wa, d_wi, d_ba, d_bi, d_lam = lru_bwd(
        cfg, lru_xc, p0, wa, wi, ba, bi, lam, lru_states, d_mix, cfg.DV, "lru_bwd")
    d_xbr, d_lru_cw, d_lru_cb = conv_bwd(cfg, p0, lru_col, cfg.W, lru_cw, d_lru_xc, "lru_conv_bwd")
    dp0 = jnp.concatenate([d_qkvg, d_xbr, d_gate_br, d_glr, jnp.zeros((S, cfg.EP - lru_col - 2 * cfg.W - LANES), BF16)],
                          axis=1)
    d_ev_in = matmul(h0, dp0, ta=True, out_dtype=BF16, name="ev_dwin", tn=768)
    d_ev_in_slots = slots_from_cols(d_ev_in, functools.partial(_even_in_unpadded, cfg), p["ev_w_in"].shape[2],
                                    "ev_dwin_slots")
    sent_ev_in, token = start_grads([d_ev_in_slots], "grads_start_ev_in")
    dh0 = matmul(dp0, w_ev_in, tb=True, after=token, name="ev_dh", tk=1792)
    grad_x, d_ev_norm = norm_bwd(cfg, x0, ev_norm, dh0, dx1.f32, "ev_norm_bwd", with_bf16=False)

    out = {"loss": loss, "grad_x": grad_x.f32[None]}

    def update(names, sent, after, wait_name):
        s, r = exchange_wait(sent[0], after, wait_name + "0")
        sends, recvs = [[a] for a in s], [[a] for a in r]
        for extra in sent[1:]:
            s, r = exchange_wait(extra, after, wait_name + "1")
            for i in range(len(names)):
                sends[i].append(s[i])
                recvs[i].append(r[i])
        for i, n in enumerate(names):
            flip = n in ("ev_w_in", "od_w_in")
            shard = lambda a: jnp.swapaxes(a, 1, 2) if flip else view(n, a)
            res = adamw_sharded(recvs[i], sends[i], me, shard(p[n]), shard(p["m_" + n]), shard(p["v_" + n]),
                                "adamw_" + n, transposed=flip)
            out["grad_" + n], out["delta_" + n], out["new_m_" + n], out["new_v_" + n] = (shard(a) for a in res)
        return res[-1]

    small_full = {
        "ev_gla_w_gate": d_gla_wg[:GATE_RANK][None], "ev_lru_conv_w": d_lru_cw[None], "od_norm": d_od_norm,
        "od_conv_w": d_od_cw[None], "od_conv_b": d_od_cb, "od_gnorm": d_od_gn,
        "ev_norm": d_ev_norm, "ev_gla_b_gate": d_bg, "ev_gla_w_onorm": d_wn, "ev_lru_conv_b": d_lru_cb,
        "ev_lru_w_a": d_wa[None], "ev_lru_b_a": d_ba, "ev_lru_w_i": d_wi[None], "ev_lru_b_i": d_bi,
        "ev_lru_lam": d_lam, "od_dt_bias": _ungroup_lanes(cfg, d_dt_bias), "od_a_log": _ungroup_lanes(cfg, d_a_log),
        "od_d_skip": _ungroup_lanes(cfg, d_d_skip), "ffn_norm": jnp.concatenate([d_ffn_norm0, d_ffn_norm1], axis=0),
        "final_norm": d_final_norm.reshape(D),
    }
    small = small_sharded + replicated
    small_packed = pack([small_full[n] for n in small])
    sent_small, token = exchange_start([(small_packed, None)], True,
                                       [_landing_zone(small_packed, None, me, "gather_small_grads_own")],
                                       "gather_small_grads")

    done = update(["od_w_in", "od_w_out"], [sent_od], token, "grads_wait_od")
    done = update(["ffn_w_down"], [sent_ffn0[0], sent_ffn1[0]], done, "grads_wait_ffn_down")
    done = update(["ffn_w_gate", "ffn_w_up"], [sent_ffn0[1], sent_ffn1[1]], done, "grads_wait_ffn")
    done = update(["ev_w_out"], [sent_ev_out], done, "grads_wait_ev_out")
    done = update(["ev_w_in"], [sent_ev_in], done, "grads_wait_ev_in")

    small_all = exchange_wait(sent_small, done, "gather_small_grads_wait")[1][0]
    g_small = dict(zip(small, unpack(reduce_slots(small_all, "sum_small_grads"), [small_full[n].shape for n in small])))
    for n in small_sharded:
        width = p[n].shape[-1]
        g_small[n] = lax.dynamic_slice_in_dim(g_small[n], me * width, width, axis=g_small[n].ndim - 1)
    shapes = [p[n].shape for n in small]
    g_buf = pack([g_small[n] for n in small])
    delta, new_m, new_v = adamw(pack([p[n] for n in small]), g_buf, pack([p["m_" + n] for n in small]),
                                pack([p["v_" + n] for n in small]), "adamw_small")
    for kind, buf in (("grad_", g_buf), ("delta_", delta), ("new_m_", new_m), ("new_v_", new_v)):
        for n, a in zip(small, unpack(buf, shapes)):
            out[kind + n] = a
    return out


WEIGHTS = ['ev_norm', 'ev_w_in', 'ev_gla_w_gate', 'ev_gla_b_gate', 'ev_gla_w_onorm', 'ev_lru_conv_w', 'ev_lru_conv_b',
           'ev_lru_w_a', 'ev_lru_b_a', 'ev_lru_w_i', 'ev_lru_b_i', 'ev_lru_lam', 'ev_w_out', 'od_norm', 'od_w_in',
           'od_conv_w', 'od_conv_b', 'od_dt_bias', 'od_a_log', 'od_d_skip', 'od_gnorm', 'od_w_out', 'ffn_norm',
           'ffn_w_gate', 'ffn_w_up', 'ffn_w_down', 'final_norm']


def kernel(x, ev_norm, ev_w_in, ev_gla_w_gate, ev_gla_b_gate, ev_gla_w_onorm, ev_lru_conv_w, ev_lru_conv_b, ev_lru_w_a, ev_lru_b_a, ev_lru_w_i, ev_lru_b_i, ev_lru_lam, ev_w_out, od_norm, od_w_in, od_conv_w, od_conv_b, od_dt_bias, od_a_log, od_d_skip, od_gnorm, od_w_out, ffn_norm, ffn_w_gate, ffn_w_up, ffn_w_down, final_norm, loss_target, m_ev_norm, m_ev_w_in, m_ev_gla_w_gate, m_ev_gla_b_gate, m_ev_gla_w_onorm, m_ev_lru_conv_w, m_ev_lru_conv_b, m_ev_lru_w_a, m_ev_lru_b_a, m_ev_lru_w_i, m_ev_lru_b_i, m_ev_lru_lam, m_ev_w_out, m_od_norm, m_od_w_in, m_od_conv_w, m_od_conv_b, m_od_dt_bias, m_od_a_log, m_od_d_skip, m_od_gnorm, m_od_w_out, m_ffn_norm, m_ffn_w_gate, m_ffn_w_up, m_ffn_w_down, m_final_norm, v_ev_norm, v_ev_w_in, v_ev_gla_w_gate, v_ev_gla_b_gate, v_ev_gla_w_onorm, v_ev_lru_conv_w, v_ev_lru_conv_b, v_ev_lru_w_a, v_ev_lru_b_a, v_ev_lru_w_i, v_ev_lru_b_i, v_ev_lru_lam, v_ev_w_out, v_od_norm, v_od_w_in, v_od_conv_w, v_od_conv_b, v_od_dt_bias, v_od_a_log, v_od_d_skip, v_od_gnorm, v_od_w_out, v_ffn_norm, v_ffn_w_gate, v_ffn_w_up, v_ffn_w_down, v_final_norm):
    args = dict(locals())
    p = {n: a for n, a in args.items() if n != "loss_target"}
    cfg = Cfg(S=x.shape[1], D=x.shape[2], DFF=ffn_w_gate.shape[2] * N_DEV)
    out = train_step(cfg, p, loss_target)
    return (out["loss"], out["grad_x"], *[out["grad_" + w] for w in WEIGHTS], *[out["delta_" + w] for w in WEIGHTS],
            *[out["new_m_" + w] for w in WEIGHTS], *[out["new_v_" + w] for w in WEIGHTS])
```

```python
import functools
import math
from typing import NamedTuple

import jax
import jax.numpy as jnp
from jax import lax
from jax.experimental import pallas as pl
from jax.experimental.pallas import tpu as pltpu

F32 = jnp.float32
BF16 = jnp.bfloat16
MESH_AXES = ("x", "y", "c")
N_DEV = 8
LANES = 128
SUBLANES = 8
VMEM_LIMIT = 56 * 1024 * 1024

NORM_EPS = 1e-6
CONV_WIDTH = 4
CHUNK = 64
HEAD_K = 128
HEAD_V = 256
GATE_RANK = 16
GATE_NORM = 16.0
LRU_BLOCK = 128
LRU_C = 8.0
SSD_P = 64
SSD_N = 128
SSD_HG = 8
SSD_GW = SSD_HG * SSD_P

ADAM_LR = 0.001
ADAM_B1 = 0.9
ADAM_B2 = 0.999
ADAM_EPS = 1e-08
ADAM_WD = 0.01
ADAM_STEP = 10


class Cfg(NamedTuple):
    S: int
    D: int
    DFF: int

    @property
    def GH(self):
        return self.D // 512

    @property
    def NB(self):
        return self.D // 256

    @property
    def NG(self):
        return self.D // 256

    @property
    def DK(self):
        return HEAD_K * self.GH

    @property
    def DV(self):
        return HEAD_V * self.GH

    @property
    def W(self):
        return LRU_BLOCK * self.NB

    @property
    def DI(self):
        return SSD_GW * self.NG

    @property
    def CD(self):
        return self.DI + 2 * self.NG * SSD_N

    @property
    def NH(self):
        return SSD_HG * self.NG

    @property
    def EVEN_IN(self):
        return 2 * self.DK + 2 * self.DV + GATE_RANK + 2 * self.W

    @property
    def ODD_IN(self):
        return self.DI + self.CD + self.NH

    @property
    def EP(self):
        return _round_up(2 * self.DK + 2 * self.DV + 2 * self.W + LANES, 768)

    @property
    def OP(self):
        return _round_up(self.DI + self.CD + LANES, 768)


def _round_up(n, m):
    return (n + m - 1) // m * m


def _tile(n, pref):
    if n <= pref:
        return n
    t = pref - pref % LANES
    while n % t:
        t -= LANES
    return t


def _cparams(n_axes):
    return pltpu.CompilerParams(dimension_semantics=("arbitrary",) * n_axes, vmem_limit_bytes=VMEM_LIMIT)


def _dg(a, b, ca, cb):
    return lax.dot_general(a.astype(BF16), b.astype(BF16), (((ca,), (cb,)), ((), ())), preferred_element_type=F32)


@functools.partial(jax.custom_vjp, nondiff_argnums=(2, 3))
def bdot(a, b, ca, cb):
    return _dg(a, b, ca, cb)


def _bdot_fwd(a, b, ca, cb):
    return _dg(a, b, ca, cb), (a, b)


def _bdot_bwd(ca, cb, res, g):
    a, b = res
    da = _dg(g, b, 1, 1 - cb) if ca == 1 else _dg(b, g, 1 - cb, 1)
    db = _dg(a, g, 1 - ca, 0) if cb == 0 else _dg(g, a, 0, 1 - ca)
    return da.astype(a.dtype), db.astype(b.dtype)


bdot.defvjp(_bdot_fwd, _bdot_bwd)


def _lower_tri(n):
    r = lax.broadcasted_iota(jnp.int32, (n, n), 0)
    c = lax.broadcasted_iota(jnp.int32, (n, n), 1)
    return c <= r


def _running_sum(x, reverse):
    n = x.shape[0]
    r = lax.broadcasted_iota(jnp.int32, x.shape, 0)
    d = 1
    while d < n:
        if reverse:
            x = x + jnp.where(r < n - d, pltpu.roll(x, n - d, 0), 0.0)
        else:
            x = x + jnp.where(r >= d, pltpu.roll(x, d, 0), 0.0)
        d *= 2
    return x


@jax.custom_vjp
def cumsum_rows(x):
    return _running_sum(x, False)


cumsum_rows.defvjp(lambda x: (_running_sum(x, False), None), lambda _, g: (_running_sum(g, True),))


def _tri_dot(x, transposed):
    n = x.shape[0]
    return lax.dot_general(_lower_tri(n).astype(F32), x, (((0 if transposed else 1,), (0,)), ((), ())),
                           precision=lax.Precision.HIGHEST, preferred_element_type=F32)


@jax.custom_vjp
def cumsum_rows_mxu(x):
    return _tri_dot(x, False)


cumsum_rows_mxu.defvjp(lambda x: (_tri_dot(x, False), None), lambda _, g: (_tri_dot(g, True),))


def _row(x, i):
    r = lax.broadcasted_iota(jnp.int32, x.shape, 0)
    return jnp.sum(jnp.where(r == i, x, 0.0), axis=0, keepdims=True)


def _softplus_raw(x):
    return jnp.maximum(x, 0.0) + jnp.log(1.0 + jnp.exp(-jnp.abs(x)))


@jax.custom_vjp
def softplus(x):
    return _softplus_raw(x)


softplus.defvjp(lambda x: (_softplus_raw(x), x), lambda x, g: (g * jax.nn.sigmoid(x),))


@jax.custom_vjp
def log_sigmoid(x):
    return -_softplus_raw(-x)


log_sigmoid.defvjp(lambda x: (-_softplus_raw(-x), x), lambda x, g: (g * jax.nn.sigmoid(-x),))


def silu(x):
    return x * jax.nn.sigmoid(x)


def gelu_tanh(x):
    return 0.5 * x * (1.0 + jnp.tanh(math.sqrt(2.0 / math.pi) * (x + 0.044715 * (x * x * x))))


def _expm1(x):
    series = x * (1.0 + 0.5 * x * (1.0 + (1.0 / 3.0) * x))
    return jnp.where(jnp.abs(x) < 1e-2, series, jnp.exp(x) - 1.0)


def rms(x, w):
    return x * lax.rsqrt(jnp.mean(x * x, axis=-1, keepdims=True) + NORM_EPS) * w


def _rows_iota(shape):
    return lax.broadcasted_iota(jnp.int32, shape, 0)


def _scan_up(a, u):
    n = a.shape[0]
    r = _rows_iota(a.shape)
    d = 1
    while d < n:
        m = r >= d
        a_s = jnp.where(m, pltpu.roll(a, d, 0), 1.0)
        u_s = jnp.where(m, pltpu.roll(u, d, 0), 0.0)
        u = a * u_s + u
        a = a * a_s
        d *= 2
    return u


def _scan_down(a, u):
    n = a.shape[0]
    r = _rows_iota(a.shape)
    d = 1
    while d < n:
        m = r < n - d
        a_s = jnp.where(m, pltpu.roll(a, n - d, 0), 1.0)
        u_s = jnp.where(m, pltpu.roll(u, n - d, 0), 0.0)
        u = a * u_s + u
        a = a * a_s
        d *= 2
    return u


@jax.custom_vjp
def lin_scan(a, u):
    return _scan_up(a, u)


def _lin_scan_fwd(a, u):
    h = _scan_up(a, u)
    return h, (a, h)


def _lin_scan_bwd(res, g):
    a, h = res
    n = a.shape[0]
    r = _rows_iota(a.shape)
    a_next = jnp.where(r < n - 1, pltpu.roll(a, n - 1, 0), 0.0)
    gt = _scan_down(a_next, g)
    h_prev = jnp.where(r >= 1, pltpu.roll(h, 1, 0), 0.0)
    return gt * h_prev, gt


lin_scan.defvjp(_lin_scan_fwd, _lin_scan_bwd)


def _expand_heads(v):
    r = v.shape[0]
    return jnp.concatenate([jnp.broadcast_to(v[:, h:h + 1], (r, SSD_P)) for h in range(SSD_HG)], axis=1)


@jax.custom_vjp
def _split_heads(x):
    return tuple(x[:, h * SSD_P:(h + 1) * SSD_P] for h in range(SSD_HG))


_split_heads.defvjp(lambda x: (_split_heads(x), None), lambda _, gs: (jnp.concatenate(gs, axis=1),))


def matmul(a, b, *, ta=False, tb=False, a_slot=None, b_slot=None, res=None, after=None, out_dtype=F32,
           name, tm=1024, tn=1024, tk=2048):
    ra, ca_ = a.shape[-2:]
    rb, cb_ = b.shape[-2:]
    m_st, ka_st = (ca_, ra) if ta else (ra, ca_)
    kb_st, n_st = (cb_, rb) if tb else (rb, cb_)
    kslot = a_slot == "k"
    assert kslot == (b_slot == "k")
    m = m_st * (N_DEV if a_slot == "m" else 1)
    n = n_st * (N_DEV if b_slot == "n" else 1)
    assert ka_st == kb_st, (a.shape, b.shape, ta, tb)
    tm = m_st if a_slot == "m" else _tile(m, tm)
    tn = n_st if b_slot == "n" else _tile(n, tn)
    tk = ka_st if kslot else _tile(ka_st, tk)
    nk = ka_st // tk
    ca, cb = (0 if ta else 1), (1 if tb else 0)

    if a_slot is None:
        a_spec = pl.BlockSpec((tk, tm), lambda i, j, k: (k, i)) if ta else pl.BlockSpec((tm, tk), lambda i, j, k: (i, k))
    elif a_slot == "m":
        a_spec = (pl.BlockSpec((None, tk, tm), lambda i, j, k: (i, k, 0)) if ta
                  else pl.BlockSpec((None, tm, tk), lambda i, j, k: (i, 0, k)))
    else:
        a_spec = (pl.BlockSpec((N_DEV, tk, tm), lambda i, j, k: (0, 0, i)) if ta
                  else pl.BlockSpec((N_DEV, tm, tk), lambda i, j, k: (0, i, 0)))
    if b_slot is None:
        b_spec = pl.BlockSpec((tn, tk), lambda i, j, k: (j, k)) if tb else pl.BlockSpec((tk, tn), lambda i, j, k: (k, j))
    elif b_slot == "n":
        b_spec = (pl.BlockSpec((None, tn, tk), lambda i, j, k: (j, 0, k)) if tb
                  else pl.BlockSpec((None, tk, tn), lambda i, j, k: (j, k, 0)))
    else:
        b_spec = (pl.BlockSpec((N_DEV, tn, tk), lambda i, j, k: (0, j, 0)) if tb
                  else pl.BlockSpec((N_DEV, tk, tn), lambda i, j, k: (0, 0, j)))
    if a_slot == "m":
        o_spec, o_shape = pl.BlockSpec((None, tm, tn), lambda i, j, k: (i, 0, j)), (N_DEV, tm, n)
    elif b_slot == "n":
        o_spec, o_shape = pl.BlockSpec((None, tm, tn), lambda i, j, k: (j, i, 0)), (N_DEV, m, tn)
    else:
        o_spec, o_shape = pl.BlockSpec((tm, tn), lambda i, j, k: (i, j)), (m, n)
    assert res is None or (a_slot != "m" and b_slot != "n")

    def dot(x, y):
        return lax.dot_general(x.astype(BF16), y.astype(BF16), (((ca,), (cb,)), ((), ())), preferred_element_type=F32)

    def body(*refs):
        a_ref, b_ref = refs[:2]
        r_ref = refs[2] if res is not None else None
        o_ref = refs[2 + (res is not None) + (after is not None)]

        def finish(acc):
            if r_ref is not None:
                acc = acc + r_ref[...].astype(F32)
            o_ref[...] = acc.astype(o_ref.dtype)

        if kslot:
            acc = dot(a_ref[0], b_ref[0])
            for s in range(1, N_DEV):
                acc = acc + dot(a_ref[s], b_ref[s])
            finish(acc)
        elif nk == 1:
            finish(dot(a_ref[...], b_ref[...]))
        else:
            acc_ref = refs[-1]
            k = pl.program_id(2)

            @pl.when(k == 0)
            def _():
                acc_ref[...] = dot(a_ref[...], b_ref[...])

            @pl.when(k > 0)
            def _():
                acc_ref[...] += dot(a_ref[...], b_ref[...])

            @pl.when(k == nk - 1)
            def _():
                finish(acc_ref[...])

    in_specs = [a_spec, b_spec]
    args = [a, b]
    if res is not None:
        in_specs.append(pl.BlockSpec((tm, tn), lambda i, j, k: (i, j)))
        args.append(res)
    if after is not None:
        in_specs.append(pl.BlockSpec(memory_space=pl.ANY))
        args.append(after)
    return pl.pallas_call(
        body, name=name, grid=(m // tm, n // tn, nk), in_specs=in_specs, out_specs=o_spec,
        out_shape=jax.ShapeDtypeStruct(o_shape, out_dtype),
        scratch_shapes=[pltpu.VMEM((tm, tn), F32)] if nk > 1 else [], compiler_params=_cparams(3),
    )(*args)


def seq_call(name, fn, grid, ins, outs, accs=(), carries=()):
    n_in, n_out, n_acc = len(ins), len(outs), len(accs)

    def body(*refs):
        in_refs = refs[:n_in]
        out_refs = refs[n_in:n_in + n_out]
        acc_refs = refs[n_in + n_out:n_in + n_out + n_acc]
        c_refs = refs[n_in + n_out + n_acc:]

        if acc_refs or c_refs:
            @pl.when(pl.program_id(1) == 0)
            def _():
                for r in tuple(acc_refs) + tuple(c_refs):
                    r[...] = jnp.zeros_like(r)

        o, a, c = fn([r[...] for r in in_refs], [r[...] for r in c_refs])
        for r, v in zip(out_refs, o, strict=True):
            r[...] = v.astype(r.dtype)
        for r, v in zip(acc_refs, a, strict=True):
            r[...] += v
        for r, v in zip(c_refs, c, strict=True):
            r[...] = v

    return pl.pallas_call(
        body, name=name, grid=grid,
        in_specs=[pl.BlockSpec(blk, im) for _, blk, im in ins],
        out_specs=[pl.BlockSpec(blk, im) for _, _, blk, im in outs] + [pl.BlockSpec(blk, im) for _, blk, im in accs],
        out_shape=[jax.ShapeDtypeStruct(s, d) for s, d, _, _ in outs] + [jax.ShapeDtypeStruct(s, F32) for s, _, _ in accs],
        scratch_shapes=[pltpu.VMEM(s, F32) for s in carries], compiler_params=_cparams(2),
    )(*[a for a, _, _ in ins])


_HBM = pl.BlockSpec(memory_space=pltpu.HBM)
_SEM = pl.BlockSpec(memory_space=pltpu.SEMAPHORE)


def _mesh_pos():
    pos = [lax.axis_index(ax) for ax in MESH_AXES]
    return pos, 4 * pos[0] + 2 * pos[1] + pos[2]


ALL_PEERS = (1, 2, 3, 4, 5, 6, 7)
NEAR_PEERS = (1, 2, 4, 6)
RELAYED = (2, 4, 6)


def _peers(pos, masks=ALL_PEERS):
    out = []
    for k in masks:
        bits = ((k >> 2) & 1, (k >> 1) & 1, k & 1)
        peer = tuple(1 - p if b else p for p, b in zip(pos, bits))
        out.append((peer, 4 * peer[0] + 2 * peer[1] + peer[2]))
    return out


def _part(x_ref, lead, gather, slot):
    ref = x_ref if lead is None else x_ref.at[lead]
    return ref if gather else ref.at[slot]


OWN_BLOCK_BYTES = 2 * 1024 * 1024


def _landing_zone(a, lead, me, name):
    r, c = a.shape[-2:]
    tr = r
    while tr * _round_up(c, LANES) * a.dtype.itemsize > OWN_BLOCK_BYTES and tr % 32 == 0:
        tr //= 2

    def body(me_ref, x_ref, o_ref):
        o_ref[...] = x_ref[...]

    x_spec = (pl.BlockSpec((tr, c), lambda i, me_ref: (i, 0)) if lead is None
              else pl.BlockSpec((None, tr, c), lambda i, me_ref: (lead, i, 0)))
    grid_spec = pltpu.PrefetchScalarGridSpec(
        num_scalar_prefetch=1, grid=(r // tr,), in_specs=[x_spec],
        out_specs=pl.BlockSpec((None, tr, c), lambda i, me_ref: (me_ref[0], i, 0)))
    return pl.pallas_call(body, name=name, grid_spec=grid_spec, out_shape=jax.ShapeDtypeStruct((N_DEV, r, c), a.dtype),
                          compiler_params=_cparams(1))(jnp.reshape(me, (1,)).astype(jnp.int32), a)


def _split_copies(items, gather, masks, x_refs, land_refs, send_sems, recv_sems):
    pos, me = _mesh_pos()
    copies = []
    for i, (_, lead) in enumerate(items):
        for k, (peer, peer_id) in enumerate(_peers(pos, masks)):
            copies.append(pltpu.make_async_remote_copy(
                src_ref=_part(x_refs[i], lead, gather, peer_id), dst_ref=land_refs[i].at[me],
                send_sem=send_sems.at[i * len(masks) + k], recv_sem=recv_sems.at[i * len(masks) + k],
                device_id=peer, device_id_type=pl.DeviceIdType.MESH))
    return copies


_SPLIT_CALL = dict(compiler_params=pltpu.CompilerParams(has_side_effects=pltpu.SideEffectType.DATAFLOW_SIDE_EFFECTING))


def exchange_start(items, gather, lands, name, masks=ALL_PEERS):
    n = len(items)
    lands = list(lands)
    xs = [a for a, _ in items]

    def body(*refs):
        x_refs, land_refs = refs[:n], refs[n:2 * n]
        send_sems, recv_sems, token = refs[2 * n], refs[2 * n + 1], refs[-1]
        for cp in _split_copies(items, gather, masks, x_refs, land_refs, send_sems, recv_sems):
            cp.start()
        token[...] = jnp.zeros_like(token)

    outs = pl.pallas_call(
        body, name=name,
        out_shape=(pltpu.SemaphoreType.DMA((n * len(masks),)), pltpu.SemaphoreType.DMA((n * len(masks),)),
                   *[pltpu.HBM(v.shape, v.dtype) for v in xs + lands], jax.ShapeDtypeStruct((SUBLANES, LANES), F32)),
        in_specs=[_HBM] * (2 * n),
        out_specs=(_SEM, _SEM, *[_HBM] * (2 * n), pl.BlockSpec(memory_space=pltpu.VMEM)),
        input_output_aliases={i: 2 + i for i in range(2 * n)}, **_SPLIT_CALL,
    )(*[pltpu.with_memory_space_constraint(v, pltpu.HBM) for v in xs + lands])
    handle = (items, gather, masks, outs[0], outs[1], outs[2:2 + n], outs[2 + n:2 + 2 * n])
    return handle, outs[-1]


def _relay_copies(n, land_refs, send_sems, recv_sems):
    pos, _ = _mesh_pos()
    sibling = (pos[0], pos[1], 1 - pos[2])
    copies = []
    for i in range(n):
        for k, (_, peer_id) in enumerate(_peers(pos, RELAYED)):
            slot = land_refs[i].at[peer_id]
            copies.append(pltpu.make_async_remote_copy(
                src_ref=slot, dst_ref=slot, send_sem=send_sems.at[i * len(RELAYED) + k],
                recv_sem=recv_sems.at[i * len(RELAYED) + k], device_id=sibling, device_id_type=pl.DeviceIdType.MESH))
    return copies


def relay(lands, name, work=None):
    n = len(lands)
    sems = n * len(RELAYED)

    def start(*refs):
        for cp in _relay_copies(n, refs[:n], refs[n], refs[n + 1]):
            cp.start()
        refs[-1][...] = jnp.zeros_like(refs[-1])

    outs = pl.pallas_call(
        start, name=name + "_start",
        out_shape=(pltpu.SemaphoreType.DMA((sems,)), pltpu.SemaphoreType.DMA((sems,)),
                   *[pltpu.HBM(v.shape, v.dtype) for v in lands], jax.ShapeDtypeStruct((SUBLANES, LANES), F32)),
        in_specs=[_HBM] * n, out_specs=(_SEM, _SEM, *[_HBM] * n, pl.BlockSpec(memory_space=pltpu.VMEM)),
        input_output_aliases={i: 2 + i for i in range(n)},
        **_SPLIT_CALL)(*[pltpu.with_memory_space_constraint(v, pltpu.HBM) for v in lands])

    def wait(*refs):
        for cp in _relay_copies(n, refs[:n], refs[n], refs[n + 1]):
            cp.wait_send()
            cp.wait_recv()

    done = None if work is None else work(outs[-1])
    extra = [] if work is None else [jax.tree.leaves(done)[0]]
    filled = list(pl.pallas_call(
        wait, name=name + "_wait", out_shape=tuple(pltpu.HBM(v.shape, v.dtype) for v in lands),
        in_specs=[_HBM] * n + [_SEM, _SEM] + [pl.BlockSpec(memory_space=pl.ANY)] * len(extra),
        out_specs=tuple([_HBM] * n), input_output_aliases={i: i for i in range(n)},
        **_SPLIT_CALL)(*outs[2:2 + n], outs[0], outs[1], *extra))
    return filled if work is None else (filled, done)


def exchange_wait(handle, after, name):
    items, gather, masks, send_sems, recv_sems, x_thru, land_thru = handle
    n = len(items)

    def body(*refs):
        x_refs, land_refs = refs[:n], refs[n:2 * n]
        for cp in _split_copies(items, gather, masks, x_refs, land_refs, refs[2 * n], refs[2 * n + 1]):
            cp.wait_send()
            cp.wait_recv()

    outs = pl.pallas_call(
        body, name=name, out_shape=tuple(pltpu.HBM(v.shape, v.dtype) for v in tuple(x_thru) + tuple(land_thru)),
        in_specs=[_HBM] * (2 * n) + [_SEM, _SEM, pl.BlockSpec(memory_space=pl.ANY)], out_specs=tuple([_HBM] * (2 * n)),
        input_output_aliases={i: i for i in range(2 * n)},
        compiler_params=pltpu.CompilerParams(has_side_effects=pltpu.SideEffectType.DATAFLOW_SIDE_EFFECTING),
    )(*x_thru, *land_thru, send_sems, recv_sems, after)
    return list(outs[:n]), list(outs[n:])


def _adam_update(w, g, m, v):
    nm = ADAM_B1 * m + (1.0 - ADAM_B1) * g
    nv = ADAM_B2 * v + (1.0 - ADAM_B2) * (g * g)
    m_hat = nm / (1.0 - ADAM_B1 ** ADAM_STEP)
    v_hat = nv / (1.0 - ADAM_B2 ** ADAM_STEP)
    return -ADAM_LR * (m_hat / (jnp.sqrt(v_hat) + ADAM_EPS) + ADAM_WD * w), nm, nv


def _sum_slots(s_ref):
    acc = s_ref[0].astype(F32)
    for j in range(1, N_DEV):
        acc = acc + s_ref[j].astype(F32)
    return acc


ADAM_BLOCK_BYTES = 10 * 1024 * 1024


def adamw_sharded(recvs, sends, me, w, m, v, name, transposed=False):
    nl = w.shape[0]
    r, c = recvs[0].shape[1:]
    assert w.shape[1:] == ((c, r) if transposed else (r, c))
    assert len(recvs) == nl and len(sends) == nl
    per_row = _round_up(c, LANES) * (nl * (N_DEV + 1) * recvs[0].dtype.itemsize + 7 * 4)
    tr = r
    while tr * per_row > ADAM_BLOCK_BYTES and tr % 16 == 0:
        tr //= 2

    def body(me_ref, *refs):
        s_refs, o_refs = refs[:nl], refs[nl:2 * nl]
        w_ref, m_ref, v_ref, g_ref, d_ref, nm_ref, nv_ref = refs[2 * nl:]
        mine = me_ref[0]

        def total(l):
            acc = jnp.where(mine == 0, o_refs[l][...], s_refs[l][0]).astype(F32)
            for j in range(1, N_DEV):
                acc = acc + jnp.where(mine == j, o_refs[l][...], s_refs[l][j]).astype(F32)
            return acc

        g = total(0)
        for l in range(1, nl):
            g = jnp.where(pl.program_id(0) == l, total(l), g)
        if transposed:
            g = g.T
        g_ref[...] = g
        d_ref[...], nm_ref[...], nv_ref[...] = _adam_update(w_ref[...], g, m_ref[...], v_ref[...])

    spec = (pl.BlockSpec((None, c, tr), lambda l, i, me_ref: (l, 0, i)) if transposed
            else pl.BlockSpec((None, tr, c), lambda l, i, me_ref: (l, i, 0)))
    shp = jax.ShapeDtypeStruct(w.shape, F32)
    grid_spec = pltpu.PrefetchScalarGridSpec(
        num_scalar_prefetch=1, grid=(nl, r // tr),
        in_specs=([pl.BlockSpec((N_DEV, tr, c), lambda l, i, me_ref: (0, i, 0))] * nl
                  + [pl.BlockSpec((None, tr, c), lambda l, i, me_ref: (me_ref[0], i, 0))] * nl + [spec, spec, spec]),
        out_specs=[spec] * 4)
    return pl.pallas_call(body, name=name, grid_spec=grid_spec, out_shape=[shp] * 4, compiler_params=_cparams(2))(
        jnp.reshape(me, (1,)).astype(jnp.int32), *recvs, *sends, w, m, v)


def reduce_slots(slots, name):
    _, r, lanes = slots.shape
    tr = _tile(r, 2048)

    def body(s_ref, o_ref):
        o_ref[...] = _sum_slots(s_ref)

    return pl.pallas_call(
        body, name=name, grid=(r // tr,),
        in_specs=[pl.BlockSpec((N_DEV, tr, lanes), lambda i: (0, i, 0))],
        out_specs=pl.BlockSpec((tr, lanes), lambda i: (i, 0)),
        out_shape=jax.ShapeDtypeStruct((r, lanes), F32), compiler_params=_cparams(1),
    )(slots)


def adamw(w, g, m, v, name):
    r, lanes = w.shape
    tr = _tile(r, 2048)

    def body(w_ref, g_ref, m_ref, v_ref, d_ref, nm_ref, nv_ref):
        d_ref[...], nm_ref[...], nv_ref[...] = _adam_update(w_ref[...], g_ref[...], m_ref[...], v_ref[...])

    spec = pl.BlockSpec((tr, lanes), lambda i: (i, 0))
    shp = jax.ShapeDtypeStruct((r, lanes), F32)
    return pl.pallas_call(body, name=name, grid=(r // tr,), in_specs=[spec] * 4, out_specs=[spec] * 3,
                          out_shape=[shp] * 3, compiler_params=_cparams(1))(w, g, m, v)


def cols_from_slots(slots, place, width, name):
    _, rows, c = slots.shape
    tr = _tile(rows, 256)

    def body(s_ref, o_ref):
        o_ref[...] = place(jnp.concatenate([s_ref[j] for j in range(N_DEV)], axis=1))

    return pl.pallas_call(
        body, name=name, grid=(rows // tr,),
        in_specs=[pl.BlockSpec((N_DEV, tr, c), lambda i: (0, i, 0))],
        out_specs=pl.BlockSpec((tr, width), lambda i: (i, 0)),
        out_shape=jax.ShapeDtypeStruct((rows, width), slots.dtype), compiler_params=_cparams(1))(slots)


def slots_from_cols(full, pick, c, name):
    rows, wide = full.shape
    tr = _tile(rows, 256)

    def body(x_ref, o_ref):
        v = pick(x_ref[...])
        for j in range(N_DEV):
            o_ref[j] = v[:, j * c:(j + 1) * c]

    return pl.pallas_call(
        body, name=name, grid=(rows // tr,),
        in_specs=[pl.BlockSpec((tr, wide), lambda i: (i, 0))],
        out_specs=pl.BlockSpec((N_DEV, tr, c), lambda i: (0, i, 0)),
        out_shape=jax.ShapeDtypeStruct((N_DEV, rows, c), full.dtype), compiler_params=_cparams(1))(full)


def _token_tile(cfg):
    return min(cfg.S, 256)


def norm_fwd(cfg, x, w, name):
    ts = _token_tile(cfg)
    d = x.shape[1]

    def fn(ins, _):
        xv, wv = ins
        return [rms(xv, wv)], [], []

    return seq_call(name, fn, (1, cfg.S // ts),
                    [(x, (ts, d), lambda g, t: (t, 0)), (w, (1, d), lambda g, t: (0, 0))],
                    [((cfg.S, d), BF16, (ts, d), lambda g, t: (t, 0))])[0]


class Grad(NamedTuple):
    f32: jax.Array
    bf16: jax.Array


def norm_bwd(cfg, x, w, dh, dres, name, with_bf16=True):
    ts = _token_tile(cfg)
    d = x.shape[1]

    def fn(ins, _):
        xv, wv, dhv, drv = ins
        _, vjp = jax.vjp(rms, xv, wv)
        dx, dw = vjp(dhv.astype(F32))
        return [dx + drv] * (2 if with_bf16 else 1), [dw], []

    row = lambda g, t: (t, 0)
    out = seq_call(name, fn, (1, cfg.S // ts),
                   [(x, (ts, d), row), (w, (1, d), lambda g, t: (0, 0)), (dh, (ts, d), row), (dres, (ts, d), row)],
                   [((cfg.S, d), dt, (ts, d), row) for dt in ((F32, BF16) if with_bf16 else (F32,))],
                   accs=[((1, d), (1, d), lambda g, t: (0, 0))])
    return Grad(out[0], out[1] if with_bf16 else None), out[-1]


def head_fwd_bwd(cfg, x, w, target, name):
    ts = _token_tile(cfg)
    d = x.shape[1]

    def fn(ins, _):
        xv, wv, tv = ins
        y, vjp = jax.vjp(rms, xv, wv)
        err = y - tv
        loss = 0.5 * jnp.sum(err * err) / d
        dx, dw = vjp(err / d)
        return [dx, dx], [jnp.full((SUBLANES, LANES), loss, F32), dw], []

    row = lambda g, t: (t, 0)
    fixed = lambda g, t: (0, 0)
    dx, dx_bf16, loss, dw = seq_call(name, fn, (1, cfg.S // ts),
                                     [(x, (ts, d), row), (w, (1, d), fixed), (target, (ts, d), row)],
                                     [((cfg.S, d), F32, (ts, d), row), ((cfg.S, d), BF16, (ts, d), row)],
                                     accs=[((SUBLANES, LANES), (SUBLANES, LANES), fixed), ((1, d), (1, d), fixed)])
    return loss, Grad(dx, dx_bf16), dw


FFN_ROWS = 1024


def ffn_gate_up(h, w_gate, w_up, name):
    s, d = h.shape
    c = w_gate.shape[1]
    tm = _tile(s, FFN_ROWS)

    def body(h_ref, wg_ref, wu_ref, g_ref, u_ref, a_ref):
        hv = h_ref[...]
        g = _dg(hv, wg_ref[...], 1, 1)
        u = _dg(hv, wu_ref[...], 1, 1)
        g_ref[...] = g
        u_ref[...] = u
        a_ref[...] = (silu(g) * u).astype(a_ref.dtype)

    w_spec = pl.BlockSpec((None, c, d), lambda i, j: (j, 0, 0))
    o_spec = pl.BlockSpec((None, tm, c), lambda i, j: (j, i, 0))
    shp = (N_DEV, s, c)
    return pl.pallas_call(
        body, name=name, grid=(s // tm, N_DEV), in_specs=[pl.BlockSpec((tm, d), lambda i, j: (i, 0)), w_spec, w_spec],
        out_specs=[o_spec] * 3,
        out_shape=[jax.ShapeDtypeStruct(shp, F32), jax.ShapeDtypeStruct(shp, F32), jax.ShapeDtypeStruct(shp, BF16)],
        compiler_params=_cparams(2))(h, w_gate, w_up)


def ffn_dgate_dup(dx, w_down, gate, up, after, name):
    s, d = dx.shape
    c = w_down.shape[1]
    tm = _tile(s, FFN_ROWS)
    extra = [] if after is None else [after]

    def body(dx_ref, wd_ref, g_ref, u_ref, *rest):
        dg_ref, du_ref = rest[len(extra):]
        dact = _dg(dx_ref[...], wd_ref[...], 1, 1)
        _, vjp = jax.vjp(lambda a, b: silu(a) * b, g_ref[...], u_ref[...])
        dg, du = vjp(dact)
        dg_ref[...] = dg.astype(dg_ref.dtype)
        du_ref[...] = du.astype(du_ref.dtype)

    blk = pl.BlockSpec((None, tm, c), lambda i, j: (j, i, 0))
    shp = jax.ShapeDtypeStruct((N_DEV, s, c), BF16)
    return pl.pallas_call(
        body, name=name, grid=(s // tm, N_DEV),
        in_specs=[pl.BlockSpec((tm, d), lambda i, j: (i, 0)), pl.BlockSpec((None, c, d), lambda i, j: (j, 0, 0)), blk, blk]
        + [pl.BlockSpec(memory_space=pl.ANY)] * len(extra),
        out_specs=[blk, blk], out_shape=[shp, shp], compiler_params=_cparams(2))(dx, w_down, gate, up, *extra)


CONV_COLS = (512, 256)
HALO = SUBLANES


def _conv_cols(col0, width):
    return next(c for c in CONV_COLS if col0 % c == 0 and width % c == 0)


def _shift_down(x, halo, j):
    if j == 0:
        return x
    r8 = _rows_iota(halo.shape)
    top = jnp.where(r8 >= j, pltpu.roll(x[:HALO], j, 0), pltpu.roll(halo, j, 0))
    return jnp.concatenate([top, pltpu.roll(x, j, 0)[HALO:]], axis=0)


def _shift_up(x, halo, j):
    if j == 0:
        return x
    n = x.shape[0]
    r8 = _rows_iota(halo.shape)
    bot = jnp.where(r8 < HALO - j, pltpu.roll(x[n - HALO:], HALO - j, 0), pltpu.roll(halo, HALO - j, 0))
    return jnp.concatenate([pltpu.roll(x, n - j, 0)[:n - HALO], bot], axis=0)


def _conv_tile(cfg):
    return min(cfg.S, 1024)


def conv_fwd(cfg, src, col0, width, w, b, name):
    tt, cb = _conv_tile(cfg), _conv_cols(col0, width)
    c0, hb = col0 // cb, tt // HALO
    nt = cfg.S // tt

    def body(x_ref, h_ref, w_ref, b_ref, o_ref):
        t = pl.program_id(1)
        x = x_ref[...]
        halo = jnp.where(t > 0, h_ref[...], 0.0)
        wv = w_ref[...]
        acc = b_ref[...] + wv[CONV_WIDTH - 1:CONV_WIDTH] * x
        for j in range(1, CONV_WIDTH):
            acc = acc + wv[CONV_WIDTH - 1 - j:CONV_WIDTH - j] * _shift_down(x, halo, j)
        o_ref[...] = acc

    return pl.pallas_call(
        body, name=name, grid=(width // cb, nt),
        in_specs=[pl.BlockSpec((tt, cb), lambda c, t: (t, c0 + c)),
                  pl.BlockSpec((HALO, cb), lambda c, t: (jnp.maximum(t * hb - 1, 0), c0 + c)),
                  pl.BlockSpec((CONV_WIDTH, cb), lambda c, t: (0, c)),
                  pl.BlockSpec((1, cb), lambda c, t: (0, c))],
        out_specs=pl.BlockSpec((tt, cb), lambda c, t: (t, c)),
        out_shape=jax.ShapeDtypeStruct((cfg.S, width), F32), compiler_params=_cparams(2),
    )(src, src, w, b)


def conv_bwd(cfg, src, col0, width, w, dy, name, into=None, into_col0=0):
    tt, cb = _conv_tile(cfg), _conv_cols(col0, width)
    c0, hb = col0 // cb, tt // HALO
    nt = cfg.S // tt
    extra = [] if into is None else [into]
    assert into_col0 % cb == 0
    o0 = into_col0 // cb

    def body(x_ref, h_ref, w_ref, dy_ref, dh_ref, *rest):
        dx_ref, dw_ref, db_ref = rest[len(extra):]
        t = pl.program_id(1)

        @pl.when(t == 0)
        def _():
            dw_ref[...] = jnp.zeros_like(dw_ref)
            db_ref[...] = jnp.zeros_like(db_ref)

        x = x_ref[...]
        halo = jnp.where(t > 0, h_ref[...], 0.0)
        dy = dy_ref[...]
        dhalo = jnp.where(t < nt - 1, dh_ref[...], 0.0)
        wv = w_ref[...]
        dx = wv[CONV_WIDTH - 1:CONV_WIDTH] * dy
        rows = [jnp.sum(dy * x, axis=0, keepdims=True)]
        for j in range(1, CONV_WIDTH):
            dx = dx + wv[CONV_WIDTH - 1 - j:CONV_WIDTH - j] * _shift_up(dy, dhalo, j)
            rows.insert(0, jnp.sum(dy * _shift_down(x, halo, j), axis=0, keepdims=True))
        dx_ref[...] = dx.astype(dx_ref.dtype)
        dw_ref[...] += jnp.concatenate(rows, axis=0)
        db_ref[...] += jnp.sum(dy, axis=0, keepdims=True)

    return pl.pallas_call(
        body, name=name, grid=(width // cb, nt),
        in_specs=[pl.BlockSpec((tt, cb), lambda c, t: (t, c0 + c)),
                  pl.BlockSpec((HALO, cb), lambda c, t: (jnp.maximum(t * hb - 1, 0), c0 + c)),
                  pl.BlockSpec((CONV_WIDTH, cb), lambda c, t: (0, c)),
                  pl.BlockSpec((tt, cb), lambda c, t: (t, c)),
                  pl.BlockSpec((HALO, cb), lambda c, t: (jnp.minimum((t + 1) * hb, nt * hb - 1), c))]
        + [pl.BlockSpec(memory_space=pl.ANY)] * len(extra),
        out_specs=[pl.BlockSpec((tt, cb), lambda c, t: (t, o0 + c)),
                   pl.BlockSpec((CONV_WIDTH, cb), lambda c, t: (0, c)),
                   pl.BlockSpec((1, cb), lambda c, t: (0, c))],
        out_shape=[jax.ShapeDtypeStruct((cfg.S, width) if into is None else into.shape, BF16),
                   jax.ShapeDtypeStruct((CONV_WIDTH, width), F32), jax.ShapeDtypeStruct((1, width), F32)],
        input_output_aliases={5: 0} if extra else {}, compiler_params=_cparams(2),
    )(src, src, w, dy, dy, *extra)


def write_cols(cfg, into, piece, col0, name):
    tt, width = _conv_tile(cfg), piece.shape[1]
    assert col0 % width == 0 and into.dtype == piece.dtype

    def body(p_ref, _, o_ref):
        o_ref[...] = p_ref[...]

    return pl.pallas_call(
        body, name=name, grid=(cfg.S // tt,),
        in_specs=[pl.BlockSpec((tt, width), lambda t: (t, 0)), pl.BlockSpec(memory_space=pl.ANY)],
        out_specs=pl.BlockSpec((tt, width), lambda t: (t, col0 // width)),
        out_shape=jax.ShapeDtypeStruct(into.shape, into.dtype), input_output_aliases={1: 0},
        compiler_params=_cparams(1))(piece, into)


def _gla_core(gh, q, k, v, g, glr, wg, bg, wn, st):
    n = glr.shape[0]
    causal = _lower_tri(n)
    outs, new = [], []
    for h in range(gh):
        log_a = log_sigmoid(bdot(glr, wg[h], 1, 0) + bg[h]) * (1.0 / GATE_NORM)
        bcum = cumsum_rows(log_a)
        b_last, b_mid = _row(bcum, n - 1), _row(bcum, n // 2)
        qs = q[h] * (HEAD_K ** -0.5)
        scores = jnp.where(causal, bdot(qs * jnp.exp(bcum - b_mid), k[h] * jnp.exp(b_mid - bcum), 1, 1), 0.0)
        o = bdot(scores, v[h], 1, 0) + bdot(qs * jnp.exp(bcum), st[h], 1, 1)
        new.append(st[h] * jnp.exp(b_last) + bdot(v[h], k[h] * jnp.exp(b_last - bcum), 0, 0))
        outs.append(rms(o, wn) * silu(g[h]))
    return jnp.concatenate(outs, axis=1), new


def _gla_ins(cfg, p0, wg, bg, wn, tmap):
    gh = cfg.GH
    ins = []
    for h in range(gh):
        ins.append((p0, (CHUNK, HEAD_K), lambda g, t, h=h: (tmap(t), h)))
    for h in range(gh):
        ins.append((p0, (CHUNK, HEAD_K), lambda g, t, h=h: (tmap(t), gh + h)))
    for h in range(gh):
        ins.append((p0, (CHUNK, HEAD_V), lambda g, t, h=h: (tmap(t), gh + h)))
    for h in range(gh):
        ins.append((p0, (CHUNK, HEAD_V), lambda g, t, h=h: (tmap(t), 2 * gh + h)))
    ins.append((p0, (CHUNK, LANES), lambda g, t: (tmap(t), 10 * gh)))
    for h in range(gh):
        ins.append((wg, (LANES, HEAD_K), lambda g, t, h=h: (0, h)))
    for h in range(gh):
        ins.append((bg, (1, HEAD_K), lambda g, t, h=h: (0, h)))
    ins.append((wn, (1, HEAD_V), lambda g, t: (0, 0)))
    return ins


def _gla_unpack(gh, vals):
    q, k, v, g = (vals[i * gh:(i + 1) * gh] for i in range(4))
    glr = vals[4 * gh]
    wg = vals[4 * gh + 1:5 * gh + 1]
    bg = vals[5 * gh + 1:6 * gh + 1]
    wn = vals[6 * gh + 1]
    return q, k, v, g, glr, wg, bg, wn, vals[6 * gh + 2:]


def gla_fwd(cfg, p0, wg, bg, wn, name):
    gh, nc = cfg.GH, cfg.S // CHUNK

    def fn(ins, st):
        q, k, v, g, glr, wgv, bgv, wnv, _ = _gla_unpack(gh, ins)
        out, new = _gla_core(gh, q, k, v, g, glr, wgv, bgv, wnv, st)
        return [out, jnp.stack(st)], [], new

    return seq_call(name, fn, (1, nc), _gla_ins(cfg, p0, wg, bg, wn, lambda t: t),
                    [((cfg.S, cfg.DV), BF16, (CHUNK, cfg.DV), lambda g, t: (t, 0)),
                     ((nc, gh, HEAD_V, HEAD_K), F32, (None, gh, HEAD_V, HEAD_K), lambda g, t: (t, 0, 0, 0))],
                    carries=[(HEAD_V, HEAD_K)] * gh)


def gla_bwd(cfg, p0, wg, bg, wn, states, dout, name):
    gh, nc = cfg.GH, cfg.S // CHUNK
    rev = lambda t: nc - 1 - t

    def fn(ins, dst):
        q, k, v, g, glr, wgv, bgv, wnv, rest = _gla_unpack(gh, ins)
        st_all, do = rest
        st = [st_all[h] for h in range(gh)]
        _, vjp = jax.vjp(functools.partial(_gla_core, gh), q, k, v, g, glr, wgv, bgv, wnv, st)
        dq, dk, dv, dg, dglr, dwg, dbg, dwn, dstate = vjp((do.astype(F32), list(dst)))
        return ([jnp.concatenate(list(dq) + list(dk) + list(dv) + list(dg), axis=1), dglr],
                [jnp.concatenate(dwg, axis=1), jnp.concatenate(dbg, axis=1), dwn], dstate)

    ins = _gla_ins(cfg, p0, wg, bg, wn, rev)
    ins.append((states, (None, gh, HEAD_V, HEAD_K), lambda g, t: (rev(t), 0, 0, 0)))
    ins.append((dout, (CHUNK, cfg.DV), lambda g, t: (rev(t), 0)))
    wide = 2 * cfg.DK + 2 * cfg.DV
    fixed = lambda g, t: (0, 0)
    return seq_call(name, fn, (1, nc), ins,
                    [((cfg.S, wide), BF16, (CHUNK, wide), lambda g, t: (rev(t), 0)),
                     ((cfg.S, LANES), BF16, (CHUNK, LANES), lambda g, t: (rev(t), 0))],
                    accs=[((LANES, cfg.DK), (LANES, cfg.DK), fixed), ((1, cfg.DK), (1, cfg.DK), fixed),
                          ((1, HEAD_V), (1, HEAD_V), fixed)],
                    carries=[(HEAD_V, HEAD_K)] * gh)


def _lru_core(xc, gate, wa, wi, ba, bi, lam, h_in):
    r = jax.nn.sigmoid(bdot(xc, wa, 1, 0) + ba)
    i = jax.nn.sigmoid(bdot(xc, wi, 1, 0) + bi)
    log_a = LRU_C * r * log_sigmoid(lam)
    a = jnp.exp(log_a)
    u = jnp.sqrt(-_expm1(2.0 * log_a)) * (i * xc)
    first = _rows_iota(a.shape) == 0
    h = lin_scan(a, u + jnp.where(first, a * h_in, 0.0))
    return h * gelu_tanh(gate), _row(h, a.shape[0] - 1)


def _lru_tile(cfg):
    return min(cfg.S, 512)


def _lru_ins(cfg, xc, p0, wa, wi, ba, bi, lam, tmap):
    tt, gh = _lru_tile(cfg), cfg.GH
    vec = lambda g, t: (0, g)
    return [(xc, (tt, LRU_BLOCK), lambda g, t: (tmap(t), g)),
            (p0, (tt, LRU_BLOCK), lambda g, t: (tmap(t), 8 * gh + g)),
            (wa, (None, LRU_BLOCK, LRU_BLOCK), lambda g, t: (g, 0, 0)),
            (wi, (None, LRU_BLOCK, LRU_BLOCK), lambda g, t: (g, 0, 0)),
            (ba, (1, LRU_BLOCK), vec), (bi, (1, LRU_BLOCK), vec), (lam, (1, LRU_BLOCK), vec)]


def lru_fwd(cfg, xc, p0, wa, wi, ba, bi, lam, name):
    tt, nb = _lru_tile(cfg), cfg.NB
    nt = cfg.S // tt

    def fn(ins, c):
        out, h_last = _lru_core(*ins, c[0])
        return [out, c[0]], [], [h_last]

    return seq_call(name, fn, (nb, nt), _lru_ins(cfg, xc, p0, wa, wi, ba, bi, lam, lambda t: t),
                    [((cfg.S, cfg.W), BF16, (tt, LRU_BLOCK), lambda g, t: (t, g)),
                     ((nb, nt, 1, LRU_BLOCK), F32, (None, None, 1, LRU_BLOCK), lambda g, t: (g, t, 0, 0))],
                    carries=[(1, LRU_BLOCK)])


def lru_bwd(cfg, xc, p0, wa, wi, ba, bi, lam, states, dout, dout_col0, name):
    tt, nb = _lru_tile(cfg), cfg.NB
    nt = cfg.S // tt
    rev = lambda t: nt - 1 - t

    def fn(ins, c):
        *fwd_ins, h_in, do = ins
        _, vjp = jax.vjp(_lru_core, *fwd_ins, h_in)
        dxc, dgate, dwa, dwi, dba, dbi, dlam, dh = vjp((do.astype(F32), c[0]))
        return [dxc, dgate], [dwa, dwi, dba, dbi, dlam], [dh]

    ins = _lru_ins(cfg, xc, p0, wa, wi, ba, bi, lam, rev)
    ins.append((states, (None, None, 1, LRU_BLOCK), lambda g, t: (g, rev(t), 0, 0)))
    ins.append((dout, (tt, LRU_BLOCK), lambda g, t: (rev(t), dout_col0 // LRU_BLOCK + g)))
    mat = ((nb, LRU_BLOCK, LRU_BLOCK), (None, LRU_BLOCK, LRU_BLOCK), lambda g, t: (g, 0, 0))
    vec = ((1, cfg.W), (1, LRU_BLOCK), lambda g, t: (0, g))
    return seq_call(name, fn, (nb, nt), ins,
                    [((cfg.S, cfg.W), F32, (tt, LRU_BLOCK), lambda g, t: (rev(t), g)),
                     ((cfg.S, cfg.W), BF16, (tt, LRU_BLOCK), lambda g, t: (rev(t), g))],
                    accs=[mat, mat, vec, vec, vec], carries=[(1, LRU_BLOCK)])


def _ssd_core(xc, bc, cc, z, dt_raw, dt_bias, a_log, d_skip, gn, st):
    n = xc.shape[0]
    x, bm, cm = silu(xc), silu(bc), silu(cc)
    dt = softplus(dt_raw + dt_bias)
    acs = cumsum_rows_mxu(dt * (-jnp.exp(a_log)))
    acs_t = acs.T
    acs_e, dt_e = _expand_heads(acs), _expand_heads(dt)
    last_e = _expand_heads(_row(acs, n - 1))
    causal = _lower_tri(n)
    cb = bdot(cm, bm, 1, 1)
    xdt = x * dt_e
    y_diag = []
    for h, xh in enumerate(_split_heads(xdt)):
        seg = acs[:, h:h + 1] - acs_t[h:h + 1, :]
        decay = jnp.where(causal, jnp.exp(jnp.minimum(seg, 0.0)), 0.0)
        y_diag.append(bdot(cb * decay, xh, 1, 0))
    y = jnp.concatenate(y_diag, axis=1) + bdot(cm, st, 1, 0) * jnp.exp(acs_e)
    new = st * jnp.exp(last_e) + bdot(bm, xdt * jnp.exp(last_e - acs_e), 0, 0)
    y = (y + _expand_heads(d_skip) * x) * silu(z)
    return rms(y, gn), new


SSD_TILED = 5


SSD_FWD_CHUNKS = 8
SSD_BWD_CHUNKS = 1


def _ssd_tile(cfg, chunks):
    return min(cfg.S, chunks * CHUNK)


def _chunk_rows(v, s):
    return v[s * CHUNK:(s + 1) * CHUNK]


def _ssd_ins(cfg, tt, xc, p1, dt_raw, dt_bias, a_log, d_skip, gn, tmap):
    ng = cfg.NG
    vec = lambda g, t: (g, 0, 0)
    return [(xc, (tt, SSD_GW), lambda g, t: (tmap(t), g)),
            (xc, (tt, SSD_N), lambda g, t: (tmap(t), 4 * ng + g)),
            (xc, (tt, SSD_N), lambda g, t: (tmap(t), 5 * ng + g)),
            (p1, (tt, SSD_GW), lambda g, t: (tmap(t), g)),
            (dt_raw, (None, tt, LANES), lambda g, t: (g, tmap(t), 0)),
            (dt_bias, (None, 1, LANES), vec), (a_log, (None, 1, LANES), vec), (d_skip, (None, 1, LANES), vec),
            (gn, (1, SSD_GW), lambda g, t: (0, g))]


def ssd_fwd(cfg, xc, p1, dt_raw, dt_bias, a_log, d_skip, gn, name):
    ng, nc, tt = cfg.NG, cfg.S // CHUNK, _ssd_tile(cfg, SSD_FWD_CHUNKS)
    nsub = tt // CHUNK

    def fn(ins, c):
        tiled, params = ins[:SSD_TILED], ins[SSD_TILED:]
        st, outs, entered = c[0], [], []
        for s in range(nsub):
            entered.append(st)
            out, st = _ssd_core(*[_chunk_rows(v, s) for v in tiled], *params, st)
            outs.append(out)
        return [jnp.concatenate(outs, axis=0), jnp.stack(entered)], [], [st]

    return seq_call(name, fn, (ng, cfg.S // tt), _ssd_ins(cfg, tt, xc, p1, dt_raw, dt_bias, a_log, d_skip, gn, lambda t: t),
                    [((cfg.S, cfg.DI), BF16, (tt, SSD_GW), lambda g, t: (t, g)),
                     ((ng, nc, SSD_N, SSD_GW), F32, (None, nsub, SSD_N, SSD_GW), lambda g, t: (g, t, 0, 0))],
                    carries=[(SSD_N, SSD_GW)])


def ssd_bwd(cfg, xc, p1, dt_raw, dt_bias, a_log, d_skip, gn, states, dout, name):
    ng, tt = cfg.NG, _ssd_tile(cfg, SSD_BWD_CHUNKS)
    nsub, nt = tt // CHUNK, cfg.S // tt
    rev = lambda t: nt - 1 - t

    def fn(ins, c):
        tiled, params = ins[:SSD_TILED], ins[SSD_TILED:SSD_TILED + 4]
        st_all, do = ins[SSD_TILED + 4:]
        dst, pieces, acc = c[0], [None] * nsub, None
        for s in reversed(range(nsub)):
            _, vjp = jax.vjp(_ssd_core, *[_chunk_rows(v, s) for v in tiled], *params, st_all[s])
            grads = vjp((_chunk_rows(do, s).astype(F32), dst))
            pieces[s], dparams, dst = grads[:SSD_TILED], grads[SSD_TILED:SSD_TILED + 4], grads[SSD_TILED + 4]
            acc = dparams if acc is None else [x + y for x, y in zip(acc, dparams)]
        return [jnp.concatenate([p[i] for p in pieces], axis=0) for i in range(SSD_TILED)], list(acc), [dst]

    ins = _ssd_ins(cfg, tt, xc, p1, dt_raw, dt_bias, a_log, d_skip, gn, rev)
    ins.append((states, (None, nsub, SSD_N, SSD_GW), lambda g, t: (g, rev(t), 0, 0)))
    ins.append((dout, (tt, SSD_GW), lambda g, t: (rev(t), g)))
    col = lambda g, t: (rev(t), g)
    vec = ((ng, 1, LANES), (None, 1, LANES), lambda g, t: (g, 0, 0))
    return seq_call(name, fn, (ng, nt), ins,
                    [((cfg.S, cfg.DI), F32, (tt, SSD_GW), col),
                     ((cfg.S, ng * SSD_N), F32, (tt, SSD_N), col),
                     ((cfg.S, ng * SSD_N), F32, (tt, SSD_N), col),
                     ((cfg.S, cfg.OP), BF16, (tt, SSD_GW), col),
                     ((ng, cfg.S, LANES), F32, (None, tt, LANES), lambda g, t: (g, rev(t), 0))],
                    accs=[vec, vec, vec, ((1, cfg.DI), (1, SSD_GW), lambda g, t: (0, g))],
                    carries=[(SSD_N, SSD_GW)])


PACK_ALIGN = SUBLANES * LANES
PACK_ROWS = 256


def pack(arrays):
    pieces = []
    for a in arrays:
        flat = a.reshape(-1).astype(F32)
        pad = _round_up(flat.shape[0], PACK_ALIGN) - flat.shape[0]
        pieces.append(jnp.pad(flat, (0, pad)) if pad else flat)
    flat = jnp.concatenate(pieces)
    pad = _round_up(flat.shape[0], PACK_ROWS * LANES) - flat.shape[0]
    return jnp.pad(flat, (0, pad)).reshape(-1, LANES)


def unpack(buf, shapes):
    lead = buf.shape[:-2]
    flat = buf.reshape(lead + (-1,))
    out, off = [], 0
    for s in shapes:
        n = math.prod(s)
        out.append(flat[..., off:off + n].reshape(lead + tuple(s)))
        off += _round_up(n, PACK_ALIGN)
    return out


def _slots_to_cols(slots):
    return slots.transpose(1, 0, 2).reshape(slots.shape[1], -1)


def _even_in_padded(cfg, w):
    main = 2 * cfg.DK + 2 * cfg.DV
    return jnp.concatenate([w[:, :main], w[:, main + GATE_RANK:], w[:, main:main + GATE_RANK],
                            jnp.zeros((w.shape[0], cfg.EP - cfg.EVEN_IN), w.dtype)], axis=1)


def _even_in_unpadded(cfg, wp):
    main = 2 * cfg.DK + 2 * cfg.DV
    rest = main + 2 * cfg.W
    return jnp.concatenate([wp[:, :main], wp[:, rest:rest + GATE_RANK], wp[:, main:rest]], axis=1)


def _odd_in_padded(cfg, w):
    return jnp.concatenate([w, jnp.zeros((w.shape[0], cfg.OP - cfg.ODD_IN), w.dtype)], axis=1)


def _odd_in_unpadded(cfg, wp):
    return wp[:, :cfg.ODD_IN]


def _group_lanes(cfg, v):
    lead = v.shape[:-1]
    g = jnp.moveaxis(v.reshape(lead + (cfg.NG, SSD_HG)), -2, 0)
    return jnp.pad(g, [(0, 0)] * (g.ndim - 1) + [(0, LANES - SSD_HG)])


def _ungroup_lanes(cfg, g):
    v = jnp.moveaxis(g[..., :SSD_HG], 0, -2)
    return v.reshape(v.shape[:-2] + (cfg.NH,))


def train_step(cfg, p, loss_target):
    S, D = cfg.S, cfg.D
    me = 4 * lax.axis_index("x") + 2 * lax.axis_index("y") + lax.axis_index("c")

    big = ["ev_w_in", "ev_w_out", "od_w_in", "od_w_out", "ffn_w_gate", "ffn_w_up", "ffn_w_down"]
    small_sharded = ["ev_gla_w_gate", "ev_lru_conv_w", "od_norm", "od_conv_w", "od_conv_b", "od_gnorm"]
    replicated = ["ev_norm", "ev_gla_b_gate", "ev_gla_w_onorm", "ev_lru_conv_b", "ev_lru_w_a", "ev_lru_b_a",
                  "ev_lru_w_i", "ev_lru_b_i", "ev_lru_lam", "od_dt_bias", "od_a_log", "od_d_skip", "ffn_norm",
                  "final_norm"]

    transposed = ("ffn_w_gate", "ffn_w_up")
    view = lambda n, a: jnp.swapaxes(a, 1, 2) if n in transposed else a
    wb = {n: view(n, p[n]).astype(BF16) for n in big}
    ffn_items = lambda l: [(wb["ffn_w_gate"], l), (wb["ffn_w_up"], l), (wb["ffn_w_down"], l)]
    ss_shapes = [p[n].shape for n in small_sharded]
    groups = [[(pack([p[n] for n in small_sharded]), None), (wb["ev_w_in"], 0), (wb["ev_w_out"], 0)], ffn_items(0),
              [(wb["od_w_in"], 0), (wb["od_w_out"], 0)], ffn_items(1)]
    gathers, tokens = [], []
    for i, g in enumerate(groups):
        lands = [_landing_zone(a, lead, me, f"gather_own{i}_{j}") for j, (a, lead) in enumerate(g)]
        handle, token = exchange_start(g, True, lands, f"gather_start{i}", NEAR_PEERS)
        gathers.append(handle)
        tokens.append(token)
    all_started = tokens[0][:1, :1] + tokens[1][:1, :1] + tokens[2][:1, :1] + tokens[3][:1, :1]

    def gathered(i, after, work):
        return relay(exchange_wait(gathers[i], after, f"gather_wait{i}")[1], f"gather_relay{i}", work)

    x0 = p["x"][0]
    (ss_all, gw_ev_in, gw_ev_out), h0 = gathered(
        0, all_started, lambda token: norm_fwd(cfg, x0, p["ev_norm"] + token[:1, :1], "ev_norm"))
    gs = dict(zip(small_sharded, unpack(ss_all, ss_shapes)))
    w_ev_in = cols_from_slots(gw_ev_in, functools.partial(_even_in_padded, cfg), cfg.EP, "ev_w_in_cols")
    w_ev_out = gw_ev_out.reshape(D, D)

    gla_wg = jnp.pad(_slots_to_cols(gs["ev_gla_w_gate"][:, 0]), ((0, LANES - GATE_RANK), (0, 0)))
    lru_cw = _slots_to_cols(gs["ev_lru_conv_w"][:, 0])
    od_norm = gs["od_norm"].transpose(1, 0, 2).reshape(1, D)
    od_cw = _slots_to_cols(gs["od_conv_w"][:, 0])
    od_cb = gs["od_conv_b"].transpose(1, 0, 2).reshape(1, cfg.CD)
    od_gn = gs["od_gnorm"].transpose(1, 0, 2).reshape(1, cfg.DI)

    target = loss_target[0]
    ev_norm = p["ev_norm"]
    bg = p["ev_gla_b_gate"]
    wn = p["ev_gla_w_onorm"]
    lru_cb = p["ev_lru_conv_b"]
    wa, wi = p["ev_lru_w_a"][0], p["ev_lru_w_i"][0]
    ba, bi, lam = p["ev_lru_b_a"], p["ev_lru_b_i"], p["ev_lru_lam"]
    dt_bias, a_log, d_skip = (_group_lanes(cfg, p[n]) for n in ("od_dt_bias", "od_a_log", "od_d_skip"))
    ffn_norm = [p["ffn_norm"][l:l + 1] for l in range(2)]
    final_norm = p["final_norm"].reshape(1, D)

    def ffn_forward(l, x, weights, next_group):
        w_gate, w_up, w_down = weights
        h = norm_fwd(cfg, x, ffn_norm[l], f"ffn{l}_norm")
        gate, up, act = ffn_gate_up(h, w_gate, w_up, f"ffn{l}_gate_up")
        down = lambda token: matmul(act, w_down, a_slot="k", b_slot="k", res=x, after=token, name=f"ffn{l}_down", tn=512)
        lands, out = (None, down(None)) if next_group is None else gathered(next_group, act, down)
        return out, (h, gate, up, act, w_gate, w_up, w_down), lands

    p0 = matmul(h0, w_ev_in, name="ev_in", tn=768)
    gla_out, gla_states = gla_fwd(cfg, p0, gla_wg, bg, wn, "gla_fwd")
    lru_col = 2 * cfg.DK + 2 * cfg.DV
    lru_xc = conv_fwd(cfg, p0, lru_col, cfg.W, lru_cw, lru_cb, "lru_conv")
    lru_out, lru_states = lru_fwd(cfg, lru_xc, p0, wa, wi, ba, bi, lam, "lru_fwd")
    mix = jnp.concatenate([gla_out, lru_out], axis=1)
    ffn0_weights, x1 = gathered(1, mix, lambda token: matmul(mix, w_ev_out, res=x0, after=token, name="ev_out"))
    x2, ffn0_saved, (gw_od_in, gw_od_out) = ffn_forward(0, x1, ffn0_weights, 2)

    w_od_in = cols_from_slots(gw_od_in, functools.partial(_odd_in_padded, cfg), cfg.OP, "od_w_in_cols")
    w_od_out = gw_od_out.reshape(cfg.DI, D)
    h2 = norm_fwd(cfg, x2, od_norm, "od_norm")
    p1 = matmul(h2, w_od_in, name="od_in", tn=768)
    od_xc = conv_fwd(cfg, p1, cfg.DI, cfg.CD, od_cw, od_cb, "od_conv")
    dt_col = cfg.DI + cfg.CD
    dt_raw = _group_lanes(cfg, p1[:, dt_col:dt_col + cfg.NH])
    ssd_out, ssd_states = ssd_fwd(cfg, od_xc, p1, dt_raw, dt_bias, a_log, d_skip, od_gn, "ssd_fwd")
    ffn1_weights, x3 = gathered(3, ssd_out, lambda token: matmul(ssd_out, w_od_out, res=x2, after=token, name="od_out"))
    x4, ffn1_saved, _ = ffn_forward(1, x3, ffn1_weights, None)

    loss_part, dx4, d_final_norm = head_fwd_bwd(cfg, x4, final_norm, target, "head")
    loss = lax.psum(loss_part[0, 0], MESH_AXES)

    def ffn_backward(l, x, saved, dx_out, after):
        h, gate, up, act, w_gate, w_up, w_down = saved
        dgate, dup = ffn_dgate_dup(dx_out.bf16, w_down, gate, up, after, f"ffn{l}_dgate_dup")
        d_down = matmul(act, dx_out.bf16, ta=True, a_slot="m", out_dtype=BF16, name=f"ffn{l}_dwdown", tk=S)
        sent_down, token = start_grads([d_down], f"grads_start_ffn{l}_down")
        d_gate = matmul(dgate, h, ta=True, a_slot="m", after=token, out_dtype=BF16, name=f"ffn{l}_dwgate", tk=S)
        d_up = matmul(dup, h, ta=True, a_slot="m", out_dtype=BF16, name=f"ffn{l}_dwup", tk=S)
        sent_gate_up, token = start_grads([d_gate, d_up], f"grads_start_ffn{l}")
        dh = matmul(dgate, w_gate, a_slot="k", b_slot="k", after=token, name=f"ffn{l}_dh_gate", tn=512)
        dh = matmul(dup, w_up, a_slot="k", b_slot="k", res=dh, name=f"ffn{l}_dh_up", tn=512)
        dx, dnorm = norm_bwd(cfg, x, ffn_norm[l], dh, dx_out.f32, f"ffn{l}_norm_bwd")
        return dx, dnorm, (sent_down, sent_gate_up)

    def start_grads(arrays, name):
        return exchange_start([(a, None) for a in arrays], False, [lax.empty(a.shape, a.dtype) for a in arrays], name)

    dx3, d_ffn_norm1, sent_ffn1 = ffn_backward(1, x3, ffn1_saved, dx4, None)

    d_ssd_out = matmul(dx3.bf16, w_od_out, tb=True, name="od_dmix")
    d_od_out = matmul(ssd_out, dx3.bf16, ta=True, out_dtype=BF16, name="od_dwout", tk=S)
    dxs, dbm, dcm, dz, d_dt_raw, d_dt_bias, d_a_log, d_d_skip, d_od_gn = ssd_bwd(
        cfg, od_xc, p1, dt_raw, dt_bias, a_log, d_skip, od_gn, ssd_states, d_ssd_out, "ssd_bwd")
    dp1, conv_parts, col = dz, [], 0
    for part, dy in (("x", dxs), ("b", dbm), ("c", dcm)):
        width = dy.shape[1]
        dp1, dcw, dcb = conv_bwd(cfg, p1, cfg.DI + col, width, od_cw[:, col:col + width], dy, "od_conv_bwd_" + part,
                                 into=dp1, into_col0=cfg.DI + col)
        conv_parts.append((dcw, dcb))
        col += width
    d_od_cw = jnp.concatenate([c[0] for c in conv_parts], axis=1)
    d_od_cb = jnp.concatenate([c[1] for c in conv_parts], axis=1)
    d_dt = _ungroup_lanes(cfg, d_dt_raw).astype(BF16)
    tail = jnp.concatenate([d_dt, jnp.zeros((S, cfg.OP - cfg.ODD_IN), BF16)], axis=1)
    dp1 = write_cols(cfg, dp1, tail, cfg.DI + cfg.CD, "od_dt_cols")
    dh2 = matmul(dp1, w_od_in, tb=True, name="od_dh", tk=1536)
    d_od_in = matmul(h2, dp1, ta=True, out_dtype=BF16, name="od_dwin", tn=768, tk=S)
    dx2, d_od_norm = norm_bwd(cfg, x2, od_norm, dh2, dx3.f32, "od_norm_bwd")
    d_od_in_slots = slots_from_cols(d_od_in, functools.partial(_odd_in_unpadded, cfg), p["od_w_in"].shape[2],
                                    "od_dwin_slots")
    sent_od, token = start_grads([d_od_in_slots, d_od_out.reshape((N_DEV,) + p["od_w_out"].shape[1:])], "grads_start_od")

    dx1, d_ffn_norm0, sent_ffn0 = ffn_backward(0, x1, ffn0_saved, dx2, token)

    d_ev_out = matmul(mix, dx1.bf16, ta=True, out_dtype=BF16, name="ev_dwout", tk=S)
    sent_ev_out, token = start_grads([d_ev_out.reshape((N_DEV,) + p["ev_w_out"].shape[1:])], "grads_start_ev_out")
    d_mix = matmul(dx1.bf16, w_ev_out, tb=True, after=token, name="ev_dmix")
    d_qkvg, d_glr, d_gla_wg, d_bg, d_wn = gla_bwd(cfg, p0, gla_wg, bg, wn, gla_states, d_mix, "gla_bwd")
    d_lru_xc, d_gate_br, d_wa, d_wi, d_ba, d_bi, d_lam = lru_bwd(
        cfg, lru_xc, p0, wa, wi, ba, bi, lam, lru_states, d_mix, cfg.DV, "lru_bwd")
    d_xbr, d_lru_cw, d_lru_cb = conv_bwd(cfg, p0, lru_col, cfg.W, lru_cw, d_lru_xc, "lru_conv_bwd")
    dp0 = jnp.concatenate([d_qkvg, d_xbr, d_gate_br, d_glr, jnp.zeros((S, cfg.EP - lru_col - 2 * cfg.W - LANES), BF16)],
                          axis=1)
    d_ev_in = matmul(h0, dp0, ta=True, out_dtype=BF16, name="ev_dwin", tn=768, tk=S)
    d_ev_in_slots = slots_from_cols(d_ev_in, functools.partial(_even_in_unpadded, cfg), p["ev_w_in"].shape[2],
                                    "ev_dwin_slots")
    sent_ev_in, token = start_grads([d_ev_in_slots], "grads_start_ev_in")
    dh0 = matmul(dp0, w_ev_in, tb=True, after=token, name="ev_dh", tk=1792)
    grad_x, d_ev_norm = norm_bwd(cfg, x0, ev_norm, dh0, dx1.f32, "ev_norm_bwd", with_bf16=False)

    out = {"loss": loss, "grad_x": grad_x.f32[None]}

    def update(names, sent, after, wait_name):
        s, r = exchange_wait(sent[0], after, wait_name + "0")
        sends, recvs = [[a] for a in s], [[a] for a in r]
        for extra in sent[1:]:
            s, r = exchange_wait(extra, after, wait_name + "1")
            for i in range(len(names)):
                sends[i].append(s[i])
                recvs[i].append(r[i])
        for i, n in enumerate(names):
            flip = n in ("ev_w_in", "od_w_in")
            shard = lambda a: jnp.swapaxes(a, 1, 2) if flip else view(n, a)
            res = adamw_sharded(recvs[i], sends[i], me, shard(p[n]), shard(p["m_" + n]), shard(p["v_" + n]),
                                "adamw_" + n, transposed=flip)
            out["grad_" + n], out["delta_" + n], out["new_m_" + n], out["new_v_" + n] = (shard(a) for a in res)
        return res[-1]

    small_full = {
        "ev_gla_w_gate": d_gla_wg[:GATE_RANK][None], "ev_lru_conv_w": d_lru_cw[None], "od_norm": d_od_norm,
        "od_conv_w": d_od_cw[None], "od_conv_b": d_od_cb, "od_gnorm": d_od_gn,
        "ev_norm": d_ev_norm, "ev_gla_b_gate": d_bg, "ev_gla_w_onorm": d_wn, "ev_lru_conv_b": d_lru_cb,
        "ev_lru_w_a": d_wa[None], "ev_lru_b_a": d_ba, "ev_lru_w_i": d_wi[None], "ev_lru_b_i": d_bi,
        "ev_lru_lam": d_lam, "od_dt_bias": _ungroup_lanes(cfg, d_dt_bias), "od_a_log": _ungroup_lanes(cfg, d_a_log),
        "od_d_skip": _ungroup_lanes(cfg, d_d_skip), "ffn_norm": jnp.concatenate([d_ffn_norm0, d_ffn_norm1], axis=0),
        "final_norm": d_final_norm.reshape(D),
    }
    small = small_sharded + replicated
    small_packed = pack([small_full[n] for n in small])
    sent_small, token = exchange_start([(small_packed, None)], True,
                                       [_landing_zone(small_packed, None, me, "gather_small_grads_own")],
                                       "gather_small_grads")

    done = update(["od_w_in", "od_w_out"], [sent_od], token, "grads_wait_od")
    done = update(["ffn_w_down"], [sent_ffn0[0], sent_ffn1[0]], done, "grads_wait_ffn_down")
    done = update(["ffn_w_gate", "ffn_w_up"], [sent_ffn0[1], sent_ffn1[1]], done, "grads_wait_ffn")
    done = update(["ev_w_out"], [sent_ev_out], done, "grads_wait_ev_out")
    done = update(["ev_w_in"], [sent_ev_in], done, "grads_wait_ev_in")

    small_all = exchange_wait(sent_small, done, "gather_small_grads_wait")[1][0]
    g_small = dict(zip(small, unpack(reduce_slots(small_all, "sum_small_grads"), [small_full[n].shape for n in small])))
    for n in small_sharded:
        width = p[n].shape[-1]
        g_small[n] = lax.dynamic_slice_in_dim(g_small[n], me * width, width, axis=g_small[n].ndim - 1)
    shapes = [p[n].shape for n in small]
    g_buf = pack([g_small[n] for n in small])
    delta, new_m, new_v = adamw(pack([p[n] for n in small]), g_buf, pack([p["m_" + n] for n in small]),
                                pack([p["v_" + n] for n in small]), "adamw_small")
    for kind, buf in (("grad_", g_buf), ("delta_", delta), ("new_m_", new_m), ("new_v_", new_v)):
        for n, a in zip(small, unpack(buf, shapes)):
            out[kind + n] = a
    return out


WEIGHTS = ['ev_norm', 'ev_w_in', 'ev_gla_w_gate', 'ev_gla_b_gate', 'ev_gla_w_onorm', 'ev_lru_conv_w', 'ev_lru_conv_b',
           'ev_lru_w_a', 'ev_lru_b_a', 'ev_lru_w_i', 'ev_lru_b_i', 'ev_lru_lam', 'ev_w_out', 'od_norm', 'od_w_in',
           'od_conv_w', 'od_conv_b', 'od_dt_bias', 'od_a_log', 'od_d_skip', 'od_gnorm', 'od_w_out', 'ffn_norm',
           'ffn_w_gate', 'ffn_w_up', 'ffn_w_down', 'final_norm']


def kernel(x, ev_norm, ev_w_in, ev_gla_w_gate, ev_gla_b_gate, ev_gla_w_onorm, ev_lru_conv_w, ev_lru_conv_b, ev_lru_w_a, ev_lru_b_a, ev_lru_w_i, ev_lru_b_i, ev_lru_lam, ev_w_out, od_norm, od_w_in, od_conv_w, od_conv_b, od_dt_bias, od_a_log, od_d_skip, od_gnorm, od_w_out, ffn_norm, ffn_w_gate, ffn_w_up, ffn_w_down, final_norm, loss_target, m_ev_norm, m_ev_w_in, m_ev_gla_w_gate, m_ev_gla_b_gate, m_ev_gla_w_onorm, m_ev_lru_conv_w, m_ev_lru_conv_b, m_ev_lru_w_a, m_ev_lru_b_a, m_ev_lru_w_i, m_ev_lru_b_i, m_ev_lru_lam, m_ev_w_out, m_od_norm, m_od_w_in, m_od_conv_w, m_od_conv_b, m_od_dt_bias, m_od_a_log, m_od_d_skip, m_od_gnorm, m_od_w_out, m_ffn_norm, m_ffn_w_gate, m_ffn_w_up, m_ffn_w_down, m_final_norm, v_ev_norm, v_ev_w_in, v_ev_gla_w_gate, v_ev_gla_b_gate, v_ev_gla_w_onorm, v_ev_lru_conv_w, v_ev_lru_conv_b, v_ev_lru_w_a, v_ev_lru_b_a, v_ev_lru_w_i, v_ev_lru_b_i, v_ev_lru_lam, v_ev_w_out, v_od_norm, v_od_w_in, v_od_conv_w, v_od_conv_b, v_od_dt_bias, v_od_a_log, v_od_d_skip, v_od_gnorm, v_od_w_out, v_ffn_norm, v_ffn_w_gate, v_ffn_w_up, v_ffn_w_down, v_final_norm):
    args = dict(locals())
    p = {n: a for n, a in args.items() if n != "loss_target"}
    cfg = Cfg(S=x.shape[1], D=x.shape[2], DFF=ffn_w_gate.shape[2] * N_DEV)
    out = train_step(cfg, p, loss_target)
    return (out["loss"], out["grad_x"], *[out["grad_" + w] for w in WEIGHTS], *[out["delta_" + w] for w in WEIGHTS],
            *[out["new_m_" + w] for w in WEIGHTS], *[out["new_v_" + w] for w in WEIGHTS])
```

```python
import functools
import math
from typing import NamedTuple

import jax
import jax.numpy as jnp
from jax import lax
from jax.experimental import pallas as pl
from jax.experimental.pallas import tpu as pltpu

F32 = jnp.float32
BF16 = jnp.bfloat16
MESH_AXES = ("x", "y", "c")
N_DEV = 8
LANES = 128
SUBLANES = 8
VMEM_LIMIT = 56 * 1024 * 1024

NORM_EPS = 1e-6
CONV_WIDTH = 4
CHUNK = 64
HEAD_K = 128
HEAD_V = 256
GATE_RANK = 16
GATE_NORM = 16.0
LRU_BLOCK = 128
LRU_C = 8.0
SSD_P = 64
SSD_N = 128
SSD_HG = 8
SSD_GW = SSD_HG * SSD_P

ADAM_LR = 0.001
ADAM_B1 = 0.9
ADAM_B2 = 0.999
ADAM_EPS = 1e-08
ADAM_WD = 0.01
ADAM_STEP = 10


class Cfg(NamedTuple):
    S: int
    D: int
    DFF: int

    @property
    def GH(self):
        return self.D // 512

    @property
    def NB(self):
        return self.D // 256

    @property
    def NG(self):
        return self.D // 256

    @property
    def DK(self):
        return HEAD_K * self.GH

    @property
    def DV(self):
        return HEAD_V * self.GH

    @property
    def W(self):
        return LRU_BLOCK * self.NB

    @property
    def DI(self):
        return SSD_GW * self.NG

    @property
    def CD(self):
        return self.DI + 2 * self.NG * SSD_N

    @property
    def NH(self):
        return SSD_HG * self.NG

    @property
    def EVEN_IN(self):
        return 2 * self.DK + 2 * self.DV + GATE_RANK + 2 * self.W

    @property
    def ODD_IN(self):
        return self.DI + self.CD + self.NH

    @property
    def EP(self):
        return _round_up(2 * self.DK + 2 * self.DV + 2 * self.W + LANES, 768)

    @property
    def OP(self):
        return _round_up(self.DI + self.CD + LANES, 768)


def _round_up(n, m):
    return (n + m - 1) // m * m


def _tile(n, pref):
    if n <= pref:
        return n
    t = pref - pref % LANES
    while n % t:
        t -= LANES
    return t


def _cparams(n_axes):
    return pltpu.CompilerParams(dimension_semantics=("arbitrary",) * n_axes, vmem_limit_bytes=VMEM_LIMIT)


def _dg(a, b, ca, cb):
    return lax.dot_general(a.astype(BF16), b.astype(BF16), (((ca,), (cb,)), ((), ())), preferred_element_type=F32)


@functools.partial(jax.custom_vjp, nondiff_argnums=(2, 3))
def bdot(a, b, ca, cb):
    return _dg(a, b, ca, cb)


def _bdot_fwd(a, b, ca, cb):
    return _dg(a, b, ca, cb), (a, b)


def _bdot_bwd(ca, cb, res, g):
    a, b = res
    da = _dg(g, b, 1, 1 - cb) if ca == 1 else _dg(b, g, 1 - cb, 1)
    db = _dg(a, g, 1 - ca, 0) if cb == 0 else _dg(g, a, 0, 1 - ca)
    return da.astype(a.dtype), db.astype(b.dtype)


bdot.defvjp(_bdot_fwd, _bdot_bwd)


def _lower_tri(n):
    r = lax.broadcasted_iota(jnp.int32, (n, n), 0)
    c = lax.broadcasted_iota(jnp.int32, (n, n), 1)
    return c <= r


def _running_sum(x, reverse):
    n = x.shape[0]
    r = lax.broadcasted_iota(jnp.int32, x.shape, 0)
    d = 1
    while d < n:
        if reverse:
            x = x + jnp.where(r < n - d, pltpu.roll(x, n - d, 0), 0.0)
        else:
            x = x + jnp.where(r >= d, pltpu.roll(x, d, 0), 0.0)
        d *= 2
    return x


@jax.custom_vjp
def cumsum_rows(x):
    return _running_sum(x, False)


cumsum_rows.defvjp(lambda x: (_running_sum(x, False), None), lambda _, g: (_running_sum(g, True),))


def _tri_dot(x, transposed):
    n = x.shape[0]
    return lax.dot_general(_lower_tri(n).astype(F32), x, (((0 if transposed else 1,), (0,)), ((), ())),
                           precision=lax.Precision.HIGHEST, preferred_element_type=F32)


@jax.custom_vjp
def cumsum_rows_mxu(x):
    return _tri_dot(x, False)


cumsum_rows_mxu.defvjp(lambda x: (_tri_dot(x, False), None), lambda _, g: (_tri_dot(g, True),))


def _row(x, i):
    r = lax.broadcasted_iota(jnp.int32, x.shape, 0)
    return jnp.sum(jnp.where(r == i, x, 0.0), axis=0, keepdims=True)


def _softplus_raw(x):
    return jnp.maximum(x, 0.0) + jnp.log(1.0 + jnp.exp(-jnp.abs(x)))


@jax.custom_vjp
def softplus(x):
    return _softplus_raw(x)


softplus.defvjp(lambda x: (_softplus_raw(x), x), lambda x, g: (g * jax.nn.sigmoid(x),))


@jax.custom_vjp
def log_sigmoid(x):
    return -_softplus_raw(-x)


log_sigmoid.defvjp(lambda x: (-_softplus_raw(-x), x), lambda x, g: (g * jax.nn.sigmoid(-x),))


def silu(x):
    return x * jax.nn.sigmoid(x)


def gelu_tanh(x):
    return 0.5 * x * (1.0 + jnp.tanh(math.sqrt(2.0 / math.pi) * (x + 0.044715 * (x * x * x))))


def _expm1(x):
    series = x * (1.0 + 0.5 * x * (1.0 + (1.0 / 3.0) * x))
    return jnp.where(jnp.abs(x) < 1e-2, series, jnp.exp(x) - 1.0)


def rms(x, w):
    return x * lax.rsqrt(jnp.mean(x * x, axis=-1, keepdims=True) + NORM_EPS) * w


def _rows_iota(shape):
    return lax.broadcasted_iota(jnp.int32, shape, 0)


def _scan_up(a, u):
    n = a.shape[0]
    r = _rows_iota(a.shape)
    d = 1
    while d < n:
        m = r >= d
        a_s = jnp.where(m, pltpu.roll(a, d, 0), 1.0)
        u_s = jnp.where(m, pltpu.roll(u, d, 0), 0.0)
        u = a * u_s + u
        a = a * a_s
        d *= 2
    return u


def _scan_down(a, u):
    n = a.shape[0]
    r = _rows_iota(a.shape)
    d = 1
    while d < n:
        m = r < n - d
        a_s = jnp.where(m, pltpu.roll(a, n - d, 0), 1.0)
        u_s = jnp.where(m, pltpu.roll(u, n - d, 0), 0.0)
        u = a * u_s + u
        a = a * a_s
        d *= 2
    return u


@jax.custom_vjp
def lin_scan(a, u):
    return _scan_up(a, u)


def _lin_scan_fwd(a, u):
    h = _scan_up(a, u)
    return h, (a, h)


def _lin_scan_bwd(res, g):
    a, h = res
    n = a.shape[0]
    r = _rows_iota(a.shape)
    a_next = jnp.where(r < n - 1, pltpu.roll(a, n - 1, 0), 0.0)
    gt = _scan_down(a_next, g)
    h_prev = jnp.where(r >= 1, pltpu.roll(h, 1, 0), 0.0)
    return gt * h_prev, gt


lin_scan.defvjp(_lin_scan_fwd, _lin_scan_bwd)


def _expand_heads(v):
    r = v.shape[0]
    return jnp.concatenate([jnp.broadcast_to(v[:, h:h + 1], (r, SSD_P)) for h in range(SSD_HG)], axis=1)


@jax.custom_vjp
def _split_heads(x):
    return tuple(x[:, h * SSD_P:(h + 1) * SSD_P] for h in range(SSD_HG))


_split_heads.defvjp(lambda x: (_split_heads(x), None), lambda _, gs: (jnp.concatenate(gs, axis=1),))


def matmul(a, b, *, ta=False, tb=False, a_slot=None, b_slot=None, res=None, after=None, out_dtype=F32,
           name, tm=1024, tn=1024, tk=2048):
    ra, ca_ = a.shape[-2:]
    rb, cb_ = b.shape[-2:]
    m_st, ka_st = (ca_, ra) if ta else (ra, ca_)
    kb_st, n_st = (cb_, rb) if tb else (rb, cb_)
    kslot = a_slot == "k"
    assert kslot == (b_slot == "k")
    m = m_st * (N_DEV if a_slot == "m" else 1)
    n = n_st * (N_DEV if b_slot == "n" else 1)
    assert ka_st == kb_st, (a.shape, b.shape, ta, tb)
    tm = m_st if a_slot == "m" else _tile(m, tm)
    tn = n_st if b_slot == "n" else _tile(n, tn)
    tk = ka_st if kslot else _tile(ka_st, tk)
    nk = ka_st // tk
    ca, cb = (0 if ta else 1), (1 if tb else 0)

    if a_slot is None:
        a_spec = pl.BlockSpec((tk, tm), lambda i, j, k: (k, i)) if ta else pl.BlockSpec((tm, tk), lambda i, j, k: (i, k))
    elif a_slot == "m":
        a_spec = (pl.BlockSpec((None, tk, tm), lambda i, j, k: (i, k, 0)) if ta
                  else pl.BlockSpec((None, tm, tk), lambda i, j, k: (i, 0, k)))
    else:
        a_spec = (pl.BlockSpec((N_DEV, tk, tm), lambda i, j, k: (0, 0, i)) if ta
                  else pl.BlockSpec((N_DEV, tm, tk), lambda i, j, k: (0, i, 0)))
    if b_slot is None:
        b_spec = pl.BlockSpec((tn, tk), lambda i, j, k: (j, k)) if tb else pl.BlockSpec((tk, tn), lambda i, j, k: (k, j))
    elif b_slot == "n":
        b_spec = (pl.BlockSpec((None, tn, tk), lambda i, j, k: (j, 0, k)) if tb
                  else pl.BlockSpec((None, tk, tn), lambda i, j, k: (j, k, 0)))
    else:
        b_spec = (pl.BlockSpec((N_DEV, tn, tk), lambda i, j, k: (0, j, 0)) if tb
                  else pl.BlockSpec((N_DEV, tk, tn), lambda i, j, k: (0, 0, j)))
    if a_slot == "m":
        o_spec, o_shape = pl.BlockSpec((None, tm, tn), lambda i, j, k: (i, 0, j)), (N_DEV, tm, n)
    elif b_slot == "n":
        o_spec, o_shape = pl.BlockSpec((None, tm, tn), lambda i, j, k: (j, i, 0)), (N_DEV, m, tn)
    else:
        o_spec, o_shape = pl.BlockSpec((tm, tn), lambda i, j, k: (i, j)), (m, n)
    assert res is None or (a_slot != "m" and b_slot != "n")

    def dot(x, y):
        return lax.dot_general(x.astype(BF16), y.astype(BF16), (((ca,), (cb,)), ((), ())), preferred_element_type=F32)

    def body(*refs):
        a_ref, b_ref = refs[:2]
        r_ref = refs[2] if res is not None else None
        o_ref = refs[2 + (res is not None) + (after is not None)]

        def finish(acc):
            if r_ref is not None:
                acc = acc + r_ref[...].astype(F32)
            o_ref[...] = acc.astype(o_ref.dtype)

        if kslot:
            acc = dot(a_ref[0], b_ref[0])
            for s in range(1, N_DEV):
                acc = acc + dot(a_ref[s], b_ref[s])
            finish(acc)
        elif nk == 1:
            finish(dot(a_ref[...], b_ref[...]))
        else:
            acc_ref = refs[-1]
            k = pl.program_id(2)

            @pl.when(k == 0)
            def _():
                acc_ref[...] = dot(a_ref[...], b_ref[...])

            @pl.when(k > 0)
            def _():
                acc_ref[...] += dot(a_ref[...], b_ref[...])

            @pl.when(k == nk - 1)
            def _():
                finish(acc_ref[...])

    in_specs = [a_spec, b_spec]
    args = [a, b]
    if res is not None:
        in_specs.append(pl.BlockSpec((tm, tn), lambda i, j, k: (i, j)))
        args.append(res)
    if after is not None:
        in_specs.append(pl.BlockSpec(memory_space=pl.ANY))
        args.append(after)
    return pl.pallas_call(
        body, name=name, grid=(m // tm, n // tn, nk), in_specs=in_specs, out_specs=o_spec,
        out_shape=jax.ShapeDtypeStruct(o_shape, out_dtype),
        scratch_shapes=[pltpu.VMEM((tm, tn), F32)] if nk > 1 else [], compiler_params=_cparams(3),
    )(*args)


def seq_call(name, fn, grid, ins, outs, accs=(), carries=()):
    n_in, n_out, n_acc = len(ins), len(outs), len(accs)

    def body(*refs):
        in_refs = refs[:n_in]
        out_refs = refs[n_in:n_in + n_out]
        acc_refs = refs[n_in + n_out:n_in + n_out + n_acc]
        c_refs = refs[n_in + n_out + n_acc:]

        if acc_refs or c_refs:
            @pl.when(pl.program_id(1) == 0)
            def _():
                for r in tuple(acc_refs) + tuple(c_refs):
                    r[...] = jnp.zeros_like(r)

        o, a, c = fn([r[...] for r in in_refs], [r[...] for r in c_refs])
        for r, v in zip(out_refs, o, strict=True):
            r[...] = v.astype(r.dtype)
        for r, v in zip(acc_refs, a, strict=True):
            r[...] += v
        for r, v in zip(c_refs, c, strict=True):
            r[...] = v

    return pl.pallas_call(
        body, name=name, grid=grid,
        in_specs=[pl.BlockSpec(blk, im) for _, blk, im in ins],
        out_specs=[pl.BlockSpec(blk, im) for _, _, blk, im in outs] + [pl.BlockSpec(blk, im) for _, blk, im in accs],
        out_shape=[jax.ShapeDtypeStruct(s, d) for s, d, _, _ in outs] + [jax.ShapeDtypeStruct(s, F32) for s, _, _ in accs],
        scratch_shapes=[pltpu.VMEM(s, F32) for s in carries], compiler_params=_cparams(2),
    )(*[a for a, _, _ in ins])


_HBM = pl.BlockSpec(memory_space=pltpu.HBM)
_SEM = pl.BlockSpec(memory_space=pltpu.SEMAPHORE)


def _mesh_pos():
    pos = [lax.axis_index(ax) for ax in MESH_AXES]
    return pos, 4 * pos[0] + 2 * pos[1] + pos[2]


ALL_PEERS = (1, 2, 3, 4, 5, 6, 7)
NEAR_PEERS = (1, 2, 4, 6)
RELAYED = (2, 4, 6)


def _peers(pos, masks=ALL_PEERS):
    out = []
    for k in masks:
        bits = ((k >> 2) & 1, (k >> 1) & 1, k & 1)
        peer = tuple(1 - p if b else p for p, b in zip(pos, bits))
        out.append((peer, 4 * peer[0] + 2 * peer[1] + peer[2]))
    return out


def _part(x_ref, lead, gather, slot):
    ref = x_ref if lead is None else x_ref.at[lead]
    return ref if gather else ref.at[slot]


OWN_BLOCK_BYTES = 2 * 1024 * 1024


def _landing_zone(a, lead, me, name):
    r, c = a.shape[-2:]
    tr = r
    while tr * _round_up(c, LANES) * a.dtype.itemsize > OWN_BLOCK_BYTES and tr % 32 == 0:
        tr //= 2

    def body(me_ref, x_ref, o_ref):
        o_ref[...] = x_ref[...]

    x_spec = (pl.BlockSpec((tr, c), lambda i, me_ref: (i, 0)) if lead is None
              else pl.BlockSpec((None, tr, c), lambda i, me_ref: (lead, i, 0)))
    grid_spec = pltpu.PrefetchScalarGridSpec(
        num_scalar_prefetch=1, grid=(r // tr,), in_specs=[x_spec],
        out_specs=pl.BlockSpec((None, tr, c), lambda i, me_ref: (me_ref[0], i, 0)))
    return pl.pallas_call(body, name=name, grid_spec=grid_spec, out_shape=jax.ShapeDtypeStruct((N_DEV, r, c), a.dtype),
                          compiler_params=_cparams(1))(jnp.reshape(me, (1,)).astype(jnp.int32), a)


def _split_copies(items, gather, masks, x_refs, land_refs, send_sems, recv_sems):
    pos, me = _mesh_pos()
    copies = []
    for i, (_, lead) in enumerate(items):
        for k, (peer, peer_id) in enumerate(_peers(pos, masks)):
            copies.append(pltpu.make_async_remote_copy(
                src_ref=_part(x_refs[i], lead, gather, peer_id), dst_ref=land_refs[i].at[me],
                send_sem=send_sems.at[i * len(masks) + k], recv_sem=recv_sems.at[i * len(masks) + k],
                device_id=peer, device_id_type=pl.DeviceIdType.MESH))
    return copies


_SPLIT_CALL = dict(compiler_params=pltpu.CompilerParams(has_side_effects=pltpu.SideEffectType.DATAFLOW_SIDE_EFFECTING))


def exchange_start(items, gather, lands, name, masks=ALL_PEERS):
    n = len(items)
    lands = list(lands)
    xs = [a for a, _ in items]

    def body(*refs):
        x_refs, land_refs = refs[:n], refs[n:2 * n]
        send_sems, recv_sems, token = refs[2 * n], refs[2 * n + 1], refs[-1]
        for cp in _split_copies(items, gather, masks, x_refs, land_refs, send_sems, recv_sems):
            cp.start()
        token[...] = jnp.zeros_like(token)

    outs = pl.pallas_call(
        body, name=name,
        out_shape=(pltpu.SemaphoreType.DMA((n * len(masks),)), pltpu.SemaphoreType.DMA((n * len(masks),)),
                   *[pltpu.HBM(v.shape, v.dtype) for v in xs + lands], jax.ShapeDtypeStruct((SUBLANES, LANES), F32)),
        in_specs=[_HBM] * (2 * n),
        out_specs=(_SEM, _SEM, *[_HBM] * (2 * n), pl.BlockSpec(memory_space=pltpu.VMEM)),
        input_output_aliases={i: 2 + i for i in range(2 * n)}, **_SPLIT_CALL,
    )(*[pltpu.with_memory_space_constraint(v, pltpu.HBM) for v in xs + lands])
    handle = (items, gather, masks, outs[0], outs[1], outs[2:2 + n], outs[2 + n:2 + 2 * n])
    return handle, outs[-1]


def _relay_copies(n, land_refs, send_sems, recv_sems):
    pos, _ = _mesh_pos()
    sibling = (pos[0], pos[1], 1 - pos[2])
    copies = []
    for i in range(n):
        for k, (_, peer_id) in enumerate(_peers(pos, RELAYED)):
            slot = land_refs[i].at[peer_id]
            copies.append(pltpu.make_async_remote_copy(
                src_ref=slot, dst_ref=slot, send_sem=send_sems.at[i * len(RELAYED) + k],
                recv_sem=recv_sems.at[i * len(RELAYED) + k], device_id=sibling, device_id_type=pl.DeviceIdType.MESH))
    return copies


def relay(lands, name, work=None):
    n = len(lands)
    sems = n * len(RELAYED)

    def start(*refs):
        for cp in _relay_copies(n, refs[:n], refs[n], refs[n + 1]):
            cp.start()
        refs[-1][...] = jnp.zeros_like(refs[-1])

    outs = pl.pallas_call(
        start, name=name + "_start",
        out_shape=(pltpu.SemaphoreType.DMA((sems,)), pltpu.SemaphoreType.DMA((sems,)),
                   *[pltpu.HBM(v.shape, v.dtype) for v in lands], jax.ShapeDtypeStruct((SUBLANES, LANES), F32)),
        in_specs=[_HBM] * n, out_specs=(_SEM, _SEM, *[_HBM] * n, pl.BlockSpec(memory_space=pltpu.VMEM)),
        input_output_aliases={i: 2 + i for i in range(n)},
        **_SPLIT_CALL)(*[pltpu.with_memory_space_constraint(v, pltpu.HBM) for v in lands])

    def wait(*refs):
        for cp in _relay_copies(n, refs[:n], refs[n], refs[n + 1]):
            cp.wait_send()
            cp.wait_recv()

    done = None if work is None else work(outs[-1])
    extra = [] if work is None else [jax.tree.leaves(done)[0]]
    filled = list(pl.pallas_call(
        wait, name=name + "_wait", out_shape=tuple(pltpu.HBM(v.shape, v.dtype) for v in lands),
        in_specs=[_HBM] * n + [_SEM, _SEM] + [pl.BlockSpec(memory_space=pl.ANY)] * len(extra),
        out_specs=tuple([_HBM] * n), input_output_aliases={i: i for i in range(n)},
        **_SPLIT_CALL)(*outs[2:2 + n], outs[0], outs[1], *extra))
    return filled if work is None else (filled, done)


def exchange_wait(handle, after, name):
    items, gather, masks, send_sems, recv_sems, x_thru, land_thru = handle
    n = len(items)

    def body(*refs):
        x_refs, land_refs = refs[:n], refs[n:2 * n]
        for cp in _split_copies(items, gather, masks, x_refs, land_refs, refs[2 * n], refs[2 * n + 1]):
            cp.wait_send()
            cp.wait_recv()

    outs = pl.pallas_call(
        body, name=name, out_shape=tuple(pltpu.HBM(v.shape, v.dtype) for v in tuple(x_thru) + tuple(land_thru)),
        in_specs=[_HBM] * (2 * n) + [_SEM, _SEM, pl.BlockSpec(memory_space=pl.ANY)], out_specs=tuple([_HBM] * (2 * n)),
        input_output_aliases={i: i for i in range(2 * n)},
        compiler_params=pltpu.CompilerParams(has_side_effects=pltpu.SideEffectType.DATAFLOW_SIDE_EFFECTING),
    )(*x_thru, *land_thru, send_sems, recv_sems, after)
    return list(outs[:n]), list(outs[n:])


def _adam_update(w, g, m, v):
    nm = ADAM_B1 * m + (1.0 - ADAM_B1) * g
    nv = ADAM_B2 * v + (1.0 - ADAM_B2) * (g * g)
    m_hat = nm / (1.0 - ADAM_B1 ** ADAM_STEP)
    v_hat = nv / (1.0 - ADAM_B2 ** ADAM_STEP)
    return -ADAM_LR * (m_hat / (jnp.sqrt(v_hat) + ADAM_EPS) + ADAM_WD * w), nm, nv


def _sum_slots(s_ref):
    acc = s_ref[0].astype(F32)
    for j in range(1, N_DEV):
        acc = acc + s_ref[j].astype(F32)
    return acc


ADAM_BLOCK_BYTES = 10 * 1024 * 1024


def adamw_sharded(recvs, sends, me, w, m, v, name, transposed=False):
    nl = w.shape[0]
    r, c = recvs[0].shape[1:]
    assert w.shape[1:] == ((c, r) if transposed else (r, c))
    assert len(recvs) == nl and len(sends) == nl
    per_row = _round_up(c, LANES) * (nl * (N_DEV + 1) * recvs[0].dtype.itemsize + 7 * 4)
    tr = r
    while tr * per_row > ADAM_BLOCK_BYTES and tr % 16 == 0:
        tr //= 2

    def body(me_ref, *refs):
        s_refs, o_refs = refs[:nl], refs[nl:2 * nl]
        w_ref, m_ref, v_ref, g_ref, d_ref, nm_ref, nv_ref = refs[2 * nl:]
        mine = me_ref[0]

        def total(l):
            acc = jnp.where(mine == 0, o_refs[l][...], s_refs[l][0]).astype(F32)
            for j in range(1, N_DEV):
                acc = acc + jnp.where(mine == j, o_refs[l][...], s_refs[l][j]).astype(F32)
            return acc

        g = total(0)
        for l in range(1, nl):
            g = jnp.where(pl.program_id(0) == l, total(l), g)
        if transposed:
            g = g.T
        g_ref[...] = g
        d_ref[...], nm_ref[...], nv_ref[...] = _adam_update(w_ref[...], g, m_ref[...], v_ref[...])

    spec = (pl.BlockSpec((None, c, tr), lambda l, i, me_ref: (l, 0, i)) if transposed
            else pl.BlockSpec((None, tr, c), lambda l, i, me_ref: (l, i, 0)))
    shp = jax.ShapeDtypeStruct(w.shape, F32)
    grid_spec = pltpu.PrefetchScalarGridSpec(
        num_scalar_prefetch=1, grid=(nl, r // tr),
        in_specs=([pl.BlockSpec((N_DEV, tr, c), lambda l, i, me_ref: (0, i, 0))] * nl
                  + [pl.BlockSpec((None, tr, c), lambda l, i, me_ref: (me_ref[0], i, 0))] * nl + [spec, spec, spec]),
        out_specs=[spec] * 4)
    return pl.pallas_call(body, name=name, grid_spec=grid_spec, out_shape=[shp] * 4, compiler_params=_cparams(2))(
        jnp.reshape(me, (1,)).astype(jnp.int32), *recvs, *sends, w, m, v)


def reduce_slots(slots, name):
    _, r, lanes = slots.shape
    tr = _tile(r, 2048)

    def body(s_ref, o_ref):
        o_ref[...] = _sum_slots(s_ref)

    return pl.pallas_call(
        body, name=name, grid=(r // tr,),
        in_specs=[pl.BlockSpec((N_DEV, tr, lanes), lambda i: (0, i, 0))],
        out_specs=pl.BlockSpec((tr, lanes), lambda i: (i, 0)),
        out_shape=jax.ShapeDtypeStruct((r, lanes), F32), compiler_params=_cparams(1),
    )(slots)


def adamw(w, g, m, v, name):
    r, lanes = w.shape
    tr = _tile(r, 2048)

    def body(w_ref, g_ref, m_ref, v_ref, d_ref, nm_ref, nv_ref):
        d_ref[...], nm_ref[...], nv_ref[...] = _adam_update(w_ref[...], g_ref[...], m_ref[...], v_ref[...])

    spec = pl.BlockSpec((tr, lanes), lambda i: (i, 0))
    shp = jax.ShapeDtypeStruct((r, lanes), F32)
    return pl.pallas_call(body, name=name, grid=(r // tr,), in_specs=[spec] * 4, out_specs=[spec] * 3,
                          out_shape=[shp] * 3, compiler_params=_cparams(1))(w, g, m, v)


def cols_from_slots(slots, place, width, name):
    _, rows, c = slots.shape
    tr = _tile(rows, 256)

    def body(s_ref, o_ref):
        o_ref[...] = place(jnp.concatenate([s_ref[j] for j in range(N_DEV)], axis=1))

    return pl.pallas_call(
        body, name=name, grid=(rows // tr,),
        in_specs=[pl.BlockSpec((N_DEV, tr, c), lambda i: (0, i, 0))],
        out_specs=pl.BlockSpec((tr, width), lambda i: (i, 0)),
        out_shape=jax.ShapeDtypeStruct((rows, width), slots.dtype), compiler_params=_cparams(1))(slots)


def slots_from_cols(full, pick, c, name):
    rows, wide = full.shape
    tr = _tile(rows, 256)

    def body(x_ref, o_ref):
        v = pick(x_ref[...])
        for j in range(N_DEV):
            o_ref[j] = v[:, j * c:(j + 1) * c]

    return pl.pallas_call(
        body, name=name, grid=(rows // tr,),
        in_specs=[pl.BlockSpec((tr, wide), lambda i: (i, 0))],
        out_specs=pl.BlockSpec((N_DEV, tr, c), lambda i: (0, i, 0)),
        out_shape=jax.ShapeDtypeStruct((N_DEV, rows, c), full.dtype), compiler_params=_cparams(1))(full)


def _token_tile(cfg):
    return min(cfg.S, 256)


def norm_fwd(cfg, x, w, name):
    ts = _token_tile(cfg)
    d = x.shape[1]

    def fn(ins, _):
        xv, wv = ins
        return [rms(xv, wv)], [], []

    return seq_call(name, fn, (1, cfg.S // ts),
                    [(x, (ts, d), lambda g, t: (t, 0)), (w, (1, d), lambda g, t: (0, 0))],
                    [((cfg.S, d), BF16, (ts, d), lambda g, t: (t, 0))])[0]


class Grad(NamedTuple):
    f32: jax.Array
    bf16: jax.Array


def norm_bwd(cfg, x, w, dh, dres, name, with_bf16=True):
    ts = _token_tile(cfg)
    d = x.shape[1]

    def fn(ins, _):
        xv, wv, dhv, drv = ins
        _, vjp = jax.vjp(rms, xv, wv)
        dx, dw = vjp(dhv.astype(F32))
        return [dx + drv] * (2 if with_bf16 else 1), [dw], []

    row = lambda g, t: (t, 0)
    out = seq_call(name, fn, (1, cfg.S // ts),
                   [(x, (ts, d), row), (w, (1, d), lambda g, t: (0, 0)), (dh, (ts, d), row), (dres, (ts, d), row)],
                   [((cfg.S, d), dt, (ts, d), row) for dt in ((F32, BF16) if with_bf16 else (F32,))],
                   accs=[((1, d), (1, d), lambda g, t: (0, 0))])
    return Grad(out[0], out[1] if with_bf16 else None), out[-1]


def head_fwd_bwd(cfg, x, w, target, name):
    ts = _token_tile(cfg)
    d = x.shape[1]

    def fn(ins, _):
        xv, wv, tv = ins
        y, vjp = jax.vjp(rms, xv, wv)
        err = y - tv
        loss = 0.5 * jnp.sum(err * err) / d
        dx, dw = vjp(err / d)
        return [dx, dx], [jnp.full((SUBLANES, LANES), loss, F32), dw], []

    row = lambda g, t: (t, 0)
    fixed = lambda g, t: (0, 0)
    dx, dx_bf16, loss, dw = seq_call(name, fn, (1, cfg.S // ts),
                                     [(x, (ts, d), row), (w, (1, d), fixed), (target, (ts, d), row)],
                                     [((cfg.S, d), F32, (ts, d), row), ((cfg.S, d), BF16, (ts, d), row)],
                                     accs=[((SUBLANES, LANES), (SUBLANES, LANES), fixed), ((1, d), (1, d), fixed)])
    return loss, Grad(dx, dx_bf16), dw


FFN_ROWS = 1024


def ffn_gate_up(h, w_gate, w_up, name):
    s, d = h.shape
    c = w_gate.shape[1]
    tm = _tile(s, FFN_ROWS)

    def body(h_ref, wg_ref, wu_ref, g_ref, u_ref, a_ref):
        hv = h_ref[...]
        g = _dg(hv, wg_ref[...], 1, 1)
        u = _dg(hv, wu_ref[...], 1, 1)
        g_ref[...] = g
        u_ref[...] = u
        a_ref[...] = (silu(g) * u).astype(a_ref.dtype)

    w_spec = pl.BlockSpec((None, c, d), lambda i, j: (j, 0, 0))
    o_spec = pl.BlockSpec((None, tm, c), lambda i, j: (j, i, 0))
    shp = (N_DEV, s, c)
    return pl.pallas_call(
        body, name=name, grid=(s // tm, N_DEV), in_specs=[pl.BlockSpec((tm, d), lambda i, j: (i, 0)), w_spec, w_spec],
        out_specs=[o_spec] * 3,
        out_shape=[jax.ShapeDtypeStruct(shp, F32), jax.ShapeDtypeStruct(shp, F32), jax.ShapeDtypeStruct(shp, BF16)],
        compiler_params=_cparams(2))(h, w_gate, w_up)


def ffn_dgate_dup(dx, w_down, gate, up, after, name):
    s, d = dx.shape
    c = w_down.shape[1]
    tm = _tile(s, FFN_ROWS)
    extra = [] if after is None else [after]

    def body(dx_ref, wd_ref, g_ref, u_ref, *rest):
        dg_ref, du_ref = rest[len(extra):]
        dact = _dg(dx_ref[...], wd_ref[...], 1, 1)
        _, vjp = jax.vjp(lambda a, b: silu(a) * b, g_ref[...], u_ref[...])
        dg, du = vjp(dact)
        dg_ref[...] = dg.astype(dg_ref.dtype)
        du_ref[...] = du.astype(du_ref.dtype)

    blk = pl.BlockSpec((None, tm, c), lambda i, j: (j, i, 0))
    shp = jax.ShapeDtypeStruct((N_DEV, s, c), BF16)
    return pl.pallas_call(
        body, name=name, grid=(s // tm, N_DEV),
        in_specs=[pl.BlockSpec((tm, d), lambda i, j: (i, 0)), pl.BlockSpec((None, c, d), lambda i, j: (j, 0, 0)), blk, blk]
        + [pl.BlockSpec(memory_space=pl.ANY)] * len(extra),
        out_specs=[blk, blk], out_shape=[shp, shp], compiler_params=_cparams(2))(dx, w_down, gate, up, *extra)


CONV_COLS = (512, 256)
HALO = SUBLANES


def _conv_cols(col0, width):
    return next(c for c in CONV_COLS if col0 % c == 0 and width % c == 0)


def _shift_down(x, halo, j):
    if j == 0:
        return x
    r8 = _rows_iota(halo.shape)
    top = jnp.where(r8 >= j, pltpu.roll(x[:HALO], j, 0), pltpu.roll(halo, j, 0))
    return jnp.concatenate([top, pltpu.roll(x, j, 0)[HALO:]], axis=0)


def _shift_up(x, halo, j):
    if j == 0:
        return x
    n = x.shape[0]
    r8 = _rows_iota(halo.shape)
    bot = jnp.where(r8 < HALO - j, pltpu.roll(x[n - HALO:], HALO - j, 0), pltpu.roll(halo, HALO - j, 0))
    return jnp.concatenate([pltpu.roll(x, n - j, 0)[:n - HALO], bot], axis=0)


def _conv_tile(cfg):
    return min(cfg.S, 1024)


def conv_fwd(cfg, src, col0, width, w, b, name):
    tt, cb = _conv_tile(cfg), _conv_cols(col0, width)
    c0, hb = col0 // cb, tt // HALO
    nt = cfg.S // tt

    def body(x_ref, h_ref, w_ref, b_ref, o_ref):
        t = pl.program_id(1)
        x = x_ref[...]
        halo = jnp.where(t > 0, h_ref[...], 0.0)
        wv = w_ref[...]
        acc = b_ref[...] + wv[CONV_WIDTH - 1:CONV_WIDTH] * x
        for j in range(1, CONV_WIDTH):
            acc = acc + wv[CONV_WIDTH - 1 - j:CONV_WIDTH - j] * _shift_down(x, halo, j)
        o_ref[...] = acc

    return pl.pallas_call(
        body, name=name, grid=(width // cb, nt),
        in_specs=[pl.BlockSpec((tt, cb), lambda c, t: (t, c0 + c)),
                  pl.BlockSpec((HALO, cb), lambda c, t: (jnp.maximum(t * hb - 1, 0), c0 + c)),
                  pl.BlockSpec((CONV_WIDTH, cb), lambda c, t: (0, c)),
                  pl.BlockSpec((1, cb), lambda c, t: (0, c))],
        out_specs=pl.BlockSpec((tt, cb), lambda c, t: (t, c)),
        out_shape=jax.ShapeDtypeStruct((cfg.S, width), F32), compiler_params=_cparams(2),
    )(src, src, w, b)


def conv_bwd(cfg, src, col0, width, w, dy, name, into=None, into_col0=0):
    tt, cb = _conv_tile(cfg), _conv_cols(col0, width)
    c0, hb = col0 // cb, tt // HALO
    nt = cfg.S // tt
    extra = [] if into is None else [into]
    assert into_col0 % cb == 0
    o0 = into_col0 // cb

    def body(x_ref, h_ref, w_ref, dy_ref, dh_ref, *rest):
        dx_ref, dw_ref, db_ref = rest[len(extra):]
        t = pl.program_id(1)

        @pl.when(t == 0)
        def _():
            dw_ref[...] = jnp.zeros_like(dw_ref)
            db_ref[...] = jnp.zeros_like(db_ref)

        x = x_ref[...]
        halo = jnp.where(t > 0, h_ref[...], 0.0)
        dy = dy_ref[...]
        dhalo = jnp.where(t < nt - 1, dh_ref[...], 0.0)
        wv = w_ref[...]
        dx = wv[CONV_WIDTH - 1:CONV_WIDTH] * dy
        rows = [jnp.sum(dy * x, axis=0, keepdims=True)]
        for j in range(1, CONV_WIDTH):
            dx = dx + wv[CONV_WIDTH - 1 - j:CONV_WIDTH - j] * _shift_up(dy, dhalo, j)
            rows.insert(0, jnp.sum(dy * _shift_down(x, halo, j), axis=0, keepdims=True))
        dx_ref[...] = dx.astype(dx_ref.dtype)
        dw_ref[...] += jnp.concatenate(rows, axis=0)
        db_ref[...] += jnp.sum(dy, axis=0, keepdims=True)

    return pl.pallas_call(
        body, name=name, grid=(width // cb, nt),
        in_specs=[pl.BlockSpec((tt, cb), lambda c, t: (t, c0 + c)),
                  pl.BlockSpec((HALO, cb), lambda c, t: (jnp.maximum(t * hb - 1, 0), c0 + c)),
                  pl.BlockSpec((CONV_WIDTH, cb), lambda c, t: (0, c)),
                  pl.BlockSpec((tt, cb), lambda c, t: (t, c)),
                  pl.BlockSpec((HALO, cb), lambda c, t: (jnp.minimum((t + 1) * hb, nt * hb - 1), c))]
        + [pl.BlockSpec(memory_space=pl.ANY)] * len(extra),
        out_specs=[pl.BlockSpec((tt, cb), lambda c, t: (t, o0 + c)),
                   pl.BlockSpec((CONV_WIDTH, cb), lambda c, t: (0, c)),
                   pl.BlockSpec((1, cb), lambda c, t: (0, c))],
        out_shape=[jax.ShapeDtypeStruct((cfg.S, width) if into is None else into.shape, BF16),
                   jax.ShapeDtypeStruct((CONV_WIDTH, width), F32), jax.ShapeDtypeStruct((1, width), F32)],
        input_output_aliases={5: 0} if extra else {}, compiler_params=_cparams(2),
    )(src, src, w, dy, dy, *extra)


def write_cols(cfg, into, piece, col0, name):
    tt, width = _conv_tile(cfg), piece.shape[1]
    assert col0 % width == 0 and into.dtype == piece.dtype

    def body(p_ref, _, o_ref):
        o_ref[...] = p_ref[...]

    return pl.pallas_call(
        body, name=name, grid=(cfg.S // tt,),
        in_specs=[pl.BlockSpec((tt, width), lambda t: (t, 0)), pl.BlockSpec(memory_space=pl.ANY)],
        out_specs=pl.BlockSpec((tt, width), lambda t: (t, col0 // width)),
        out_shape=jax.ShapeDtypeStruct(into.shape, into.dtype), input_output_aliases={1: 0},
        compiler_params=_cparams(1))(piece, into)


def _gla_core(gh, q, k, v, g, glr, wg, bg, wn, st):
    n = glr.shape[0]
    causal = _lower_tri(n)
    outs, new = [], []
    for h in range(gh):
        log_a = log_sigmoid(bdot(glr, wg[h], 1, 0) + bg[h]) * (1.0 / GATE_NORM)
        bcum = cumsum_rows(log_a)
        b_last, b_mid = _row(bcum, n - 1), _row(bcum, n // 2)
        qs = q[h] * (HEAD_K ** -0.5)
        scores = jnp.where(causal, bdot(qs * jnp.exp(bcum - b_mid), k[h] * jnp.exp(b_mid - bcum), 1, 1), 0.0)
        o = bdot(scores, v[h], 1, 0) + bdot(qs * jnp.exp(bcum), st[h], 1, 1)
        new.append(st[h] * jnp.exp(b_last) + bdot(v[h], k[h] * jnp.exp(b_last - bcum), 0, 0))
        outs.append(rms(o, wn) * silu(g[h]))
    return jnp.concatenate(outs, axis=1), new


def _gla_ins(cfg, p0, wg, bg, wn, tmap):
    gh = cfg.GH
    ins = []
    for h in range(gh):
        ins.append((p0, (CHUNK, HEAD_K), lambda g, t, h=h: (tmap(t), h)))
    for h in range(gh):
        ins.append((p0, (CHUNK, HEAD_K), lambda g, t, h=h: (tmap(t), gh + h)))
    for h in range(gh):
        ins.append((p0, (CHUNK, HEAD_V), lambda g, t, h=h: (tmap(t), gh + h)))
    for h in range(gh):
        ins.append((p0, (CHUNK, HEAD_V), lambda g, t, h=h: (tmap(t), 2 * gh + h)))
    ins.append((p0, (CHUNK, LANES), lambda g, t: (tmap(t), 10 * gh)))
    for h in range(gh):
        ins.append((wg, (LANES, HEAD_K), lambda g, t, h=h: (0, h)))
    for h in range(gh):
        ins.append((bg, (1, HEAD_K), lambda g, t, h=h: (0, h)))
    ins.append((wn, (1, HEAD_V), lambda g, t: (0, 0)))
    return ins


def _gla_unpack(gh, vals):
    q, k, v, g = (vals[i * gh:(i + 1) * gh] for i in range(4))
    glr = vals[4 * gh]
    wg = vals[4 * gh + 1:5 * gh + 1]
    bg = vals[5 * gh + 1:6 * gh + 1]
    wn = vals[6 * gh + 1]
    return q, k, v, g, glr, wg, bg, wn, vals[6 * gh + 2:]


def gla_fwd(cfg, p0, wg, bg, wn, name):
    gh, nc = cfg.GH, cfg.S // CHUNK

    def fn(ins, st):
        q, k, v, g, glr, wgv, bgv, wnv, _ = _gla_unpack(gh, ins)
        out, new = _gla_core(gh, q, k, v, g, glr, wgv, bgv, wnv, st)
        return [out, jnp.stack(st)], [], new

    return seq_call(name, fn, (1, nc), _gla_ins(cfg, p0, wg, bg, wn, lambda t: t),
                    [((cfg.S, cfg.DV), BF16, (CHUNK, cfg.DV), lambda g, t: (t, 0)),
                     ((nc, gh, HEAD_V, HEAD_K), F32, (None, gh, HEAD_V, HEAD_K), lambda g, t: (t, 0, 0, 0))],
                    carries=[(HEAD_V, HEAD_K)] * gh)


def gla_bwd(cfg, p0, wg, bg, wn, states, dout, name):
    gh, nc = cfg.GH, cfg.S // CHUNK
    rev = lambda t: nc - 1 - t

    def fn(ins, dst):
        q, k, v, g, glr, wgv, bgv, wnv, rest = _gla_unpack(gh, ins)
        st_all, do = rest
        st = [st_all[h] for h in range(gh)]
        _, vjp = jax.vjp(functools.partial(_gla_core, gh), q, k, v, g, glr, wgv, bgv, wnv, st)
        dq, dk, dv, dg, dglr, dwg, dbg, dwn, dstate = vjp((do.astype(F32), list(dst)))
        return ([jnp.concatenate(list(dq) + list(dk) + list(dv) + list(dg), axis=1), dglr],
                [jnp.concatenate(dwg, axis=1), jnp.concatenate(dbg, axis=1), dwn], dstate)

    ins = _gla_ins(cfg, p0, wg, bg, wn, rev)
    ins.append((states, (None, gh, HEAD_V, HEAD_K), lambda g, t: (rev(t), 0, 0, 0)))
    ins.append((dout, (CHUNK, cfg.DV), lambda g, t: (rev(t), 0)))
    wide = 2 * cfg.DK + 2 * cfg.DV
    fixed = lambda g, t: (0, 0)
    return seq_call(name, fn, (1, nc), ins,
                    [((cfg.S, wide), BF16, (CHUNK, wide), lambda g, t: (rev(t), 0)),
                     ((cfg.S, LANES), BF16, (CHUNK, LANES), lambda g, t: (rev(t), 0))],
                    accs=[((LANES, cfg.DK), (LANES, cfg.DK), fixed), ((1, cfg.DK), (1, cfg.DK), fixed),
                          ((1, HEAD_V), (1, HEAD_V), fixed)],
                    carries=[(HEAD_V, HEAD_K)] * gh)


def _lru_core(xc, gate, wa, wi, ba, bi, lam, h_in):
    r = jax.nn.sigmoid(bdot(xc, wa, 1, 0) + ba)
    i = jax.nn.sigmoid(bdot(xc, wi, 1, 0) + bi)
    log_a = LRU_C * r * log_sigmoid(lam)
    a = jnp.exp(log_a)
    u = jnp.sqrt(-_expm1(2.0 * log_a)) * (i * xc)
    first = _rows_iota(a.shape) == 0
    h = lin_scan(a, u + jnp.where(first, a * h_in, 0.0))
    return h * gelu_tanh(gate), _row(h, a.shape[0] - 1)


def _lru_tile(cfg):
    return min(cfg.S, 512)


def _lru_ins(cfg, xc, p0, wa, wi, ba, bi, lam, tmap):
    tt, gh = _lru_tile(cfg), cfg.GH
    vec = lambda g, t: (0, g)
    return [(xc, (tt, LRU_BLOCK), lambda g, t: (tmap(t), g)),
            (p0, (tt, LRU_BLOCK), lambda g, t: (tmap(t), 8 * gh + g)),
            (wa, (None, LRU_BLOCK, LRU_BLOCK), lambda g, t: (g, 0, 0)),
            (wi, (None, LRU_BLOCK, LRU_BLOCK), lambda g, t: (g, 0, 0)),
            (ba, (1, LRU_BLOCK), vec), (bi, (1, LRU_BLOCK), vec), (lam, (1, LRU_BLOCK), vec)]


def lru_fwd(cfg, xc, p0, wa, wi, ba, bi, lam, name):
    tt, nb = _lru_tile(cfg), cfg.NB
    nt = cfg.S // tt

    def fn(ins, c):
        out, h_last = _lru_core(*ins, c[0])
        return [out, c[0]], [], [h_last]

    return seq_call(name, fn, (nb, nt), _lru_ins(cfg, xc, p0, wa, wi, ba, bi, lam, lambda t: t),
                    [((cfg.S, cfg.W), BF16, (tt, LRU_BLOCK), lambda g, t: (t, g)),
                     ((nb, nt, 1, LRU_BLOCK), F32, (None, None, 1, LRU_BLOCK), lambda g, t: (g, t, 0, 0))],
                    carries=[(1, LRU_BLOCK)])


def lru_bwd(cfg, xc, p0, wa, wi, ba, bi, lam, states, dout, dout_col0, name):
    tt, nb = _lru_tile(cfg), cfg.NB
    nt = cfg.S // tt
    rev = lambda t: nt - 1 - t

    def fn(ins, c):
        *fwd_ins, h_in, do = ins
        _, vjp = jax.vjp(_lru_core, *fwd_ins, h_in)
        dxc, dgate, dwa, dwi, dba, dbi, dlam, dh = vjp((do.astype(F32), c[0]))
        return [dxc, dgate], [dwa, dwi, dba, dbi, dlam], [dh]

    ins = _lru_ins(cfg, xc, p0, wa, wi, ba, bi, lam, rev)
    ins.append((states, (None, None, 1, LRU_BLOCK), lambda g, t: (g, rev(t), 0, 0)))
    ins.append((dout, (tt, LRU_BLOCK), lambda g, t: (rev(t), dout_col0 // LRU_BLOCK + g)))
    mat = ((nb, LRU_BLOCK, LRU_BLOCK), (None, LRU_BLOCK, LRU_BLOCK), lambda g, t: (g, 0, 0))
    vec = ((1, cfg.W), (1, LRU_BLOCK), lambda g, t: (0, g))
    return seq_call(name, fn, (nb, nt), ins,
                    [((cfg.S, cfg.W), F32, (tt, LRU_BLOCK), lambda g, t: (rev(t), g)),
                     ((cfg.S, cfg.W), BF16, (tt, LRU_BLOCK), lambda g, t: (rev(t), g))],
                    accs=[mat, mat, vec, vec, vec], carries=[(1, LRU_BLOCK)])


def _ssd_core(xc, bc, cc, z, dt_raw, dt_bias, a_log, d_skip, gn, st):
    n = xc.shape[0]
    x, bm, cm = silu(xc), silu(bc), silu(cc)
    dt = softplus(dt_raw + dt_bias)
    acs = cumsum_rows_mxu(dt * (-jnp.exp(a_log)))
    acs_t = acs.T
    acs_e, dt_e = _expand_heads(acs), _expand_heads(dt)
    last_e = _expand_heads(_row(acs, n - 1))
    causal = _lower_tri(n)
    cb = bdot(cm, bm, 1, 1)
    xdt = x * dt_e
    y_diag = []
    for h, xh in enumerate(_split_heads(xdt)):
        seg = acs[:, h:h + 1] - acs_t[h:h + 1, :]
        decay = jnp.where(causal, jnp.exp(jnp.minimum(seg, 0.0)), 0.0)
        y_diag.append(bdot(cb * decay, xh, 1, 0))
    y = jnp.concatenate(y_diag, axis=1) + bdot(cm, st, 1, 0) * jnp.exp(acs_e)
    new = st * jnp.exp(last_e) + bdot(bm, xdt * jnp.exp(last_e - acs_e), 0, 0)
    y = (y + _expand_heads(d_skip) * x) * silu(z)
    return rms(y, gn), new


SSD_TILED = 5


SSD_FWD_CHUNKS = 8
SSD_BWD_CHUNKS = 1


def _ssd_tile(cfg, chunks):
    return min(cfg.S, chunks * CHUNK)


def _chunk_rows(v, s):
    return v[s * CHUNK:(s + 1) * CHUNK]


def _ssd_ins(cfg, tt, xc, p1, dt_raw, dt_bias, a_log, d_skip, gn, tmap):
    ng = cfg.NG
    vec = lambda g, t: (g, 0, 0)
    return [(xc, (tt, SSD_GW), lambda g, t: (tmap(t), g)),
            (xc, (tt, SSD_N), lambda g, t: (tmap(t), 4 * ng + g)),
            (xc, (tt, SSD_N), lambda g, t: (tmap(t), 5 * ng + g)),
            (p1, (tt, SSD_GW), lambda g, t: (tmap(t), g)),
            (dt_raw, (None, tt, LANES), lambda g, t: (g, tmap(t), 0)),
            (dt_bias, (None, 1, LANES), vec), (a_log, (None, 1, LANES), vec), (d_skip, (None, 1, LANES), vec),
            (gn, (1, SSD_GW), lambda g, t: (0, g))]


def ssd_fwd(cfg, xc, p1, dt_raw, dt_bias, a_log, d_skip, gn, name):
    ng, nc, tt = cfg.NG, cfg.S // CHUNK, _ssd_tile(cfg, SSD_FWD_CHUNKS)
    nsub = tt // CHUNK

    def fn(ins, c):
        tiled, params = ins[:SSD_TILED], ins[SSD_TILED:]
        st, outs, entered = c[0], [], []
        for s in range(nsub):
            entered.append(st)
            out, st = _ssd_core(*[_chunk_rows(v, s) for v in tiled], *params, st)
            outs.append(out)
        return [jnp.concatenate(outs, axis=0), jnp.stack(entered)], [], [st]

    return seq_call(name, fn, (ng, cfg.S // tt), _ssd_ins(cfg, tt, xc, p1, dt_raw, dt_bias, a_log, d_skip, gn, lambda t: t),
                    [((cfg.S, cfg.DI), BF16, (tt, SSD_GW), lambda g, t: (t, g)),
                     ((ng, nc, SSD_N, SSD_GW), F32, (None, nsub, SSD_N, SSD_GW), lambda g, t: (g, t, 0, 0))],
                    carries=[(SSD_N, SSD_GW)])


def ssd_bwd(cfg, xc, p1, dt_raw, dt_bias, a_log, d_skip, gn, states, dout, name):
    ng, tt = cfg.NG, _ssd_tile(cfg, SSD_BWD_CHUNKS)
    nsub, nt = tt // CHUNK, cfg.S // tt
    rev = lambda t: nt - 1 - t

    def fn(ins, c):
        tiled, params = ins[:SSD_TILED], ins[SSD_TILED:SSD_TILED + 4]
        st_all, do = ins[SSD_TILED + 4:]
        dst, pieces, acc = c[0], [None] * nsub, None
        for s in reversed(range(nsub)):
            _, vjp = jax.vjp(_ssd_core, *[_chunk_rows(v, s) for v in tiled], *params, st_all[s])
            grads = vjp((_chunk_rows(do, s).astype(F32), dst))
            pieces[s], dparams, dst = grads[:SSD_TILED], grads[SSD_TILED:SSD_TILED + 4], grads[SSD_TILED + 4]
            acc = dparams if acc is None else [x + y for x, y in zip(acc, dparams)]
        return [jnp.concatenate([p[i] for p in pieces], axis=0) for i in range(SSD_TILED)], list(acc), [dst]

    ins = _ssd_ins(cfg, tt, xc, p1, dt_raw, dt_bias, a_log, d_skip, gn, rev)
    ins.append((states, (None, nsub, SSD_N, SSD_GW), lambda g, t: (g, rev(t), 0, 0)))
    ins.append((dout, (tt, SSD_GW), lambda g, t: (rev(t), g)))
    col = lambda g, t: (rev(t), g)
    vec = ((ng, 1, LANES), (None, 1, LANES), lambda g, t: (g, 0, 0))
    return seq_call(name, fn, (ng, nt), ins,
                    [((cfg.S, cfg.DI), F32, (tt, SSD_GW), col),
                     ((cfg.S, ng * SSD_N), F32, (tt, SSD_N), col),
                     ((cfg.S, ng * SSD_N), F32, (tt, SSD_N), col),
                     ((cfg.S, cfg.OP), BF16, (tt, SSD_GW), col),
                     ((ng, cfg.S, LANES), F32, (None, tt, LANES), lambda g, t: (g, rev(t), 0))],
                    accs=[vec, vec, vec, ((1, cfg.DI), (1, SSD_GW), lambda g, t: (0, g))],
                    carries=[(SSD_N, SSD_GW)])


PACK_ALIGN = SUBLANES * LANES
PACK_ROWS = 256


def pack(arrays):
    pieces = []
    for a in arrays:
        flat = a.reshape(-1).astype(F32)
        pad = _round_up(flat.shape[0], PACK_ALIGN) - flat.shape[0]
        pieces.append(jnp.pad(flat, (0, pad)) if pad else flat)
    flat = jnp.concatenate(pieces)
    pad = _round_up(flat.shape[0], PACK_ROWS * LANES) - flat.shape[0]
    return jnp.pad(flat, (0, pad)).reshape(-1, LANES)


def unpack(buf, shapes):
    lead = buf.shape[:-2]
    flat = buf.reshape(lead + (-1,))
    out, off = [], 0
    for s in shapes:
        n = math.prod(s)
        out.append(flat[..., off:off + n].reshape(lead + tuple(s)))
        off += _round_up(n, PACK_ALIGN)
    return out


def _slots_to_cols(slots):
    return slots.transpose(1, 0, 2).reshape(slots.shape[1], -1)


def _even_in_padded(cfg, w):
    main = 2 * cfg.DK + 2 * cfg.DV
    return jnp.concatenate([w[:, :main], w[:, main + GATE_RANK:], w[:, main:main + GATE_RANK],
                            jnp.zeros((w.shape[0], cfg.EP - cfg.EVEN_IN), w.dtype)], axis=1)


def _even_in_unpadded(cfg, wp):
    main = 2 * cfg.DK + 2 * cfg.DV
    rest = main + 2 * cfg.W
    return jnp.concatenate([wp[:, :main], wp[:, rest:rest + GATE_RANK], wp[:, main:rest]], axis=1)


def _odd_in_padded(cfg, w):
    return jnp.concatenate([w, jnp.zeros((w.shape[0], cfg.OP - cfg.ODD_IN), w.dtype)], axis=1)


def _odd_in_unpadded(cfg, wp):
    return wp[:, :cfg.ODD_IN]


def _group_lanes(cfg, v):
    lead = v.shape[:-1]
    g = jnp.moveaxis(v.reshape(lead + (cfg.NG, SSD_HG)), -2, 0)
    return jnp.pad(g, [(0, 0)] * (g.ndim - 1) + [(0, LANES - SSD_HG)])


def _ungroup_lanes(cfg, g):
    v = jnp.moveaxis(g[..., :SSD_HG], 0, -2)
    return v.reshape(v.shape[:-2] + (cfg.NH,))


def train_step(cfg, p, loss_target):
    S, D = cfg.S, cfg.D
    me = 4 * lax.axis_index("x") + 2 * lax.axis_index("y") + lax.axis_index("c")

    big = ["ev_w_in", "ev_w_out", "od_w_in", "od_w_out", "ffn_w_gate", "ffn_w_up", "ffn_w_down"]
    small_sharded = ["ev_gla_w_gate", "ev_lru_conv_w", "od_norm", "od_conv_w", "od_conv_b", "od_gnorm"]
    replicated = ["ev_norm", "ev_gla_b_gate", "ev_gla_w_onorm", "ev_lru_conv_b", "ev_lru_w_a", "ev_lru_b_a",
                  "ev_lru_w_i", "ev_lru_b_i", "ev_lru_lam", "od_dt_bias", "od_a_log", "od_d_skip", "ffn_norm",
                  "final_norm"]

    transposed = ("ffn_w_gate", "ffn_w_up")
    view = lambda n, a: jnp.swapaxes(a, 1, 2) if n in transposed else a
    wb = {n: view(n, p[n]).astype(BF16) for n in big}
    ffn_items = lambda l: [(wb["ffn_w_gate"], l), (wb["ffn_w_up"], l), (wb["ffn_w_down"], l)]
    ss_shapes = [p[n].shape for n in small_sharded]
    groups = [[(wb["ev_w_in"], 0)], [(pack([p[n] for n in small_sharded]), None), (wb["ev_w_out"], 0)], ffn_items(0),
              [(wb["od_w_in"], 0), (wb["od_w_out"], 0)], ffn_items(1)]
    gathers, tokens = [], []
    for i, g in enumerate(groups):
        lands = [_landing_zone(a, lead, me, f"gather_own{i}_{j}") for j, (a, lead) in enumerate(g)]
        handle, token = exchange_start(g, True, lands, f"gather_start{i}", NEAR_PEERS)
        gathers.append(handle)
        tokens.append(token)
    all_started = sum(t[:1, :1] for t in tokens)

    def gathered(i, after, work):
        return relay(exchange_wait(gathers[i], after, f"gather_wait{i}")[1], f"gather_relay{i}", work)

    x0 = p["x"][0]
    (gw_ev_in,), h0 = gathered(
        0, all_started, lambda token: norm_fwd(cfg, x0, p["ev_norm"] + token[:1, :1], "ev_norm"))
    w_ev_in = cols_from_slots(gw_ev_in, functools.partial(_even_in_padded, cfg), cfg.EP, "ev_w_in_cols")
    (ss_all, gw_ev_out), p0 = gathered(
        1, w_ev_in, lambda token: matmul(h0, w_ev_in, after=token, name="ev_in", tn=768))
    gs = dict(zip(small_sharded, unpack(ss_all, ss_shapes)))
    w_ev_out = gw_ev_out.reshape(D, D)

    gla_wg = jnp.pad(_slots_to_cols(gs["ev_gla_w_gate"][:, 0]), ((0, LANES - GATE_RANK), (0, 0)))
    lru_cw = _slots_to_cols(gs["ev_lru_conv_w"][:, 0])
    od_norm = gs["od_norm"].transpose(1, 0, 2).reshape(1, D)
    od_cw = _slots_to_cols(gs["od_conv_w"][:, 0])
    od_cb = gs["od_conv_b"].transpose(1, 0, 2).reshape(1, cfg.CD)
    od_gn = gs["od_gnorm"].transpose(1, 0, 2).reshape(1, cfg.DI)

    target = loss_target[0]
    ev_norm = p["ev_norm"]
    bg = p["ev_gla_b_gate"]
    wn = p["ev_gla_w_onorm"]
    lru_cb = p["ev_lru_conv_b"]
    wa, wi = p["ev_lru_w_a"][0], p["ev_lru_w_i"][0]
    ba, bi, lam = p["ev_lru_b_a"], p["ev_lru_b_i"], p["ev_lru_lam"]
    dt_bias, a_log, d_skip = (_group_lanes(cfg, p[n]) for n in ("od_dt_bias", "od_a_log", "od_d_skip"))
    ffn_norm = [p["ffn_norm"][l:l + 1] for l in range(2)]
    final_norm = p["final_norm"].reshape(1, D)

    def ffn_forward(l, x, weights, next_group):
        w_gate, w_up, w_down = weights
        h = norm_fwd(cfg, x, ffn_norm[l], f"ffn{l}_norm")
        gate, up, act = ffn_gate_up(h, w_gate, w_up, f"ffn{l}_gate_up")
        down = lambda token: matmul(act, w_down, a_slot="k", b_slot="k", res=x, after=token, name=f"ffn{l}_down", tn=512)
        lands, out = (None, down(None)) if next_group is None else gathered(next_group, act, down)
        return out, (h, gate, up, act, w_gate, w_up, w_down), lands

    gla_out, gla_states = gla_fwd(cfg, p0, gla_wg, bg, wn, "gla_fwd")
    lru_col = 2 * cfg.DK + 2 * cfg.DV
    lru_xc = conv_fwd(cfg, p0, lru_col, cfg.W, lru_cw, lru_cb, "lru_conv")
    lru_out, lru_states = lru_fwd(cfg, lru_xc, p0, wa, wi, ba, bi, lam, "lru_fwd")
    mix = jnp.concatenate([gla_out, lru_out], axis=1)
    ffn0_weights, x1 = gathered(2, mix, lambda token: matmul(mix, w_ev_out, res=x0, after=token, name="ev_out"))
    x2, ffn0_saved, (gw_od_in, gw_od_out) = ffn_forward(0, x1, ffn0_weights, 3)

    w_od_in = cols_from_slots(gw_od_in, functools.partial(_odd_in_padded, cfg), cfg.OP, "od_w_in_cols")
    w_od_out = gw_od_out.reshape(cfg.DI, D)
    h2 = norm_fwd(cfg, x2, od_norm, "od_norm")
    p1 = matmul(h2, w_od_in, name="od_in", tn=768)
    od_xc = conv_fwd(cfg, p1, cfg.DI, cfg.CD, od_cw, od_cb, "od_conv")
    dt_col = cfg.DI + cfg.CD
    dt_raw = _group_lanes(cfg, p1[:, dt_col:dt_col + cfg.NH])
    ssd_out, ssd_states = ssd_fwd(cfg, od_xc, p1, dt_raw, dt_bias, a_log, d_skip, od_gn, "ssd_fwd")
    ffn1_weights, x3 = gathered(4, ssd_out, lambda token: matmul(ssd_out, w_od_out, res=x2, after=token, name="od_out"))
    x4, ffn1_saved, _ = ffn_forward(1, x3, ffn1_weights, None)

    loss_part, dx4, d_final_norm = head_fwd_bwd(cfg, x4, final_norm, target, "head")
    loss = lax.psum(loss_part[0, 0], MESH_AXES)

    def ffn_backward(l, x, saved, dx_out, after):
        h, gate, up, act, w_gate, w_up, w_down = saved
        dgate, dup = ffn_dgate_dup(dx_out.bf16, w_down, gate, up, after, f"ffn{l}_dgate_dup")
        d_down = matmul(act, dx_out.bf16, ta=True, a_slot="m", out_dtype=BF16, name=f"ffn{l}_dwdown", tk=S)
        sent_down, token = start_grads([d_down], f"grads_start_ffn{l}_down")
        d_gate = matmul(dgate, h, ta=True, a_slot="m", after=token, out_dtype=BF16, name=f"ffn{l}_dwgate", tk=S)
        d_up = matmul(dup, h, ta=True, a_slot="m", out_dtype=BF16, name=f"ffn{l}_dwup", tk=S)
        sent_gate_up, token = start_grads([d_gate, d_up], f"grads_start_ffn{l}")
        dh = matmul(dgate, w_gate, a_slot="k", b_slot="k", after=token, name=f"ffn{l}_dh_gate", tn=512)
        dh = matmul(dup, w_up, a_slot="k", b_slot="k", res=dh, name=f"ffn{l}_dh_up", tn=512)
        dx, dnorm = norm_bwd(cfg, x, ffn_norm[l], dh, dx_out.f32, f"ffn{l}_norm_bwd")
        return dx, dnorm, (sent_down, sent_gate_up)

    def start_grads(arrays, name):
        return exchange_start([(a, None) for a in arrays], False, [lax.empty(a.shape, a.dtype) for a in arrays], name)

    dx3, d_ffn_norm1, sent_ffn1 = ffn_backward(1, x3, ffn1_saved, dx4, None)

    d_ssd_out = matmul(dx3.bf16, w_od_out, tb=True, name="od_dmix")
    d_od_out = matmul(ssd_out, dx3.bf16, ta=True, out_dtype=BF16, name="od_dwout", tk=S)
    dxs, dbm, dcm, dz, d_dt_raw, d_dt_bias, d_a_log, d_d_skip, d_od_gn = ssd_bwd(
        cfg, od_xc, p1, dt_raw, dt_bias, a_log, d_skip, od_gn, ssd_states, d_ssd_out, "ssd_bwd")
    dp1, conv_parts, col = dz, [], 0
    for part, dy in (("x", dxs), ("b", dbm), ("c", dcm)):
        width = dy.shape[1]
        dp1, dcw, dcb = conv_bwd(cfg, p1, cfg.DI + col, width, od_cw[:, col:col + width], dy, "od_conv_bwd_" + part,
                                 into=dp1, into_col0=cfg.DI + col)
        conv_parts.append((dcw, dcb))
        col += width
    d_od_cw = jnp.concatenate([c[0] for c in conv_parts], axis=1)
    d_od_cb = jnp.concatenate([c[1] for c in conv_parts], axis=1)
    d_dt = _ungroup_lanes(cfg, d_dt_raw).astype(BF16)
    tail = jnp.concatenate([d_dt, jnp.zeros((S, cfg.OP - cfg.ODD_IN), BF16)], axis=1)
    dp1 = write_cols(cfg, dp1, tail, cfg.DI + cfg.CD, "od_dt_cols")
    dh2 = matmul(dp1, w_od_in, tb=True, name="od_dh", tk=1536)
    d_od_in = matmul(h2, dp1, ta=True, out_dtype=BF16, name="od_dwin", tn=768, tk=S)
    dx2, d_od_norm = norm_bwd(cfg, x2, od_norm, dh2, dx3.f32, "od_norm_bwd")
    d_od_in_slots = slots_from_cols(d_od_in, functools.partial(_odd_in_unpadded, cfg), p["od_w_in"].shape[2],
                                    "od_dwin_slots")
    sent_od, token = start_grads([d_od_in_slots, d_od_out.reshape((N_DEV,) + p["od_w_out"].shape[1:])], "grads_start_od")

    dx1, d_ffn_norm0, sent_ffn0 = ffn_backward(0, x1, ffn0_saved, dx2, token)

    d_ev_out = matmul(mix, dx1.bf16, ta=True, out_dtype=BF16, name="ev_dwout", tk=S)
    sent_ev_out, token = start_grads([d_ev_out.reshape((N_DEV,) + p["ev_w_out"].shape[1:])], "grads_start_ev_out")
    d_mix = matmul(dx1.bf16, w_ev_out, tb=True, after=token, name="ev_dmix")
    d_qkvg, d_glr, d_gla_wg, d_bg, d_wn = gla_bwd(cfg, p0, gla_wg, bg, wn, gla_states, d_mix, "gla_bwd")
    d_lru_xc, d_gate_br, d_wa, d_wi, d_ba, d_bi, d_lam = lru_bwd(
        cfg, lru_xc, p0, wa, wi, ba, bi, lam, lru_states, d_mix, cfg.DV, "lru_bwd")
    d_xbr, d_lru_cw, d_lru_cb = conv_bwd(cfg, p0, lru_col, cfg.W, lru_cw, d_lru_xc, "lru_conv_bwd")
    dp0 = jnp.concatenate([d_qkvg, d_xbr, d_gate_br, d_glr, jnp.zeros((S, cfg.EP - lru_col - 2 * cfg.W - LANES), BF16)],
                          axis=1)
    d_ev_in = matmul(h0, dp0, ta=True, out_dtype=BF16, name="ev_dwin", tn=768, tk=S)
    d_ev_in_slots = slots_from_cols(d_ev_in, functools.partial(_even_in_unpadded, cfg), p["ev_w_in"].shape[2],
                                    "ev_dwin_slots")
    sent_ev_in, token = start_grads([d_ev_in_slots], "grads_start_ev_in")
    dh0 = matmul(dp0, w_ev_in, tb=True, after=token, name="ev_dh", tk=1792)
    grad_x, d_ev_norm = norm_bwd(cfg, x0, ev_norm, dh0, dx1.f32, "ev_norm_bwd", with_bf16=False)

    out = {"loss": loss, "grad_x": grad_x.f32[None]}

    def update(names, sent, after, wait_name):
        s, r = exchange_wait(sent[0], after, wait_name + "0")
        sends, recvs = [[a] for a in s], [[a] for a in r]
        for extra in sent[1:]:
            s, r = exchange_wait(extra, after, wait_name + "1")
            for i in range(len(names)):
                sends[i].append(s[i])
                recvs[i].append(r[i])
        for i, n in enumerate(names):
            flip = n in ("ev_w_in", "od_w_in")
            shard = lambda a: jnp.swapaxes(a, 1, 2) if flip else view(n, a)
            res = adamw_sharded(recvs[i], sends[i], me, shard(p[n]), shard(p["m_" + n]), shard(p["v_" + n]),
                                "adamw_" + n, transposed=flip)
            out["grad_" + n], out["delta_" + n], out["new_m_" + n], out["new_v_" + n] = (shard(a) for a in res)
        return res[-1]

    small_full = {
        "ev_gla_w_gate": d_gla_wg[:GATE_RANK][None], "ev_lru_conv_w": d_lru_cw[None], "od_norm": d_od_norm,
        "od_conv_w": d_od_cw[None], "od_conv_b": d_od_cb, "od_gnorm": d_od_gn,
        "ev_norm": d_ev_norm, "ev_gla_b_gate": d_bg, "ev_gla_w_onorm": d_wn, "ev_lru_conv_b": d_lru_cb,
        "ev_lru_w_a": d_wa[None], "ev_lru_b_a": d_ba, "ev_lru_w_i": d_wi[None], "ev_lru_b_i": d_bi,
        "ev_lru_lam": d_lam, "od_dt_bias": _ungroup_lanes(cfg, d_dt_bias), "od_a_log": _ungroup_lanes(cfg, d_a_log),
        "od_d_skip": _ungroup_lanes(cfg, d_d_skip), "ffn_norm": jnp.concatenate([d_ffn_norm0, d_ffn_norm1], axis=0),
        "final_norm": d_final_norm.reshape(D),
    }
    small = small_sharded + replicated
    small_packed = pack([small_full[n] for n in small])
    sent_small, token = exchange_start([(small_packed, None)], True,
                                       [_landing_zone(small_packed, None, me, "gather_small_grads_own")],
                                       "gather_small_grads")

    done = update(["od_w_in", "od_w_out"], [sent_od], token, "grads_wait_od")
    done = update(["ffn_w_down"], [sent_ffn0[0], sent_ffn1[0]], done, "grads_wait_ffn_down")
    done = update(["ffn_w_gate", "ffn_w_up"], [sent_ffn0[1], sent_ffn1[1]], done, "grads_wait_ffn")
    done = update(["ev_w_out"], [sent_ev_out], done, "grads_wait_ev_out")
    done = update(["ev_w_in"], [sent_ev_in], done, "grads_wait_ev_in")

    small_all = exchange_wait(sent_small, done, "gather_small_grads_wait")[1][0]
    g_small = dict(zip(small, unpack(reduce_slots(small_all, "sum_small_grads"), [small_full[n].shape for n in small])))
    for n in small_sharded:
        width = p[n].shape[-1]
        g_small[n] = lax.dynamic_slice_in_dim(g_small[n], me * width, width, axis=g_small[n].ndim - 1)
    shapes = [p[n].shape for n in small]
    g_buf = pack([g_small[n] for n in small])
    delta, new_m, new_v = adamw(pack([p[n] for n in small]), g_buf, pack([p["m_" + n] for n in small]),
                                pack([p["v_" + n] for n in small]), "adamw_small")
    for kind, buf in (("grad_", g_buf), ("delta_", delta), ("new_m_", new_m), ("new_v_", new_v)):
        for n, a in zip(small, unpack(buf, shapes)):
            out[kind + n] = a
    return out


WEIGHTS = ['ev_norm', 'ev_w_in', 'ev_gla_w_gate', 'ev_gla_b_gate', 'ev_gla_w_onorm', 'ev_lru_conv_w', 'ev_lru_conv_b',
           'ev_lru_w_a', 'ev_lru_b_a', 'ev_lru_w_i', 'ev_lru_b_i', 'ev_lru_lam', 'ev_w_out', 'od_norm', 'od_w_in',
           'od_conv_w', 'od_conv_b', 'od_dt_bias', 'od_a_log', 'od_d_skip', 'od_gnorm', 'od_w_out', 'ffn_norm',
           'ffn_w_gate', 'ffn_w_up', 'ffn_w_down', 'final_norm']


def kernel(x, ev_norm, ev_w_in, ev_gla_w_gate, ev_gla_b_gate, ev_gla_w_onorm, ev_lru_conv_w, ev_lru_conv_b, ev_lru_w_a, ev_lru_b_a, ev_lru_w_i, ev_lru_b_i, ev_lru_lam, ev_w_out, od_norm, od_w_in, od_conv_w, od_conv_b, od_dt_bias, od_a_log, od_d_skip, od_gnorm, od_w_out, ffn_norm, ffn_w_gate, ffn_w_up, ffn_w_down, final_norm, loss_target, m_ev_norm, m_ev_w_in, m_ev_gla_w_gate, m_ev_gla_b_gate, m_ev_gla_w_onorm, m_ev_lru_conv_w, m_ev_lru_conv_b, m_ev_lru_w_a, m_ev_lru_b_a, m_ev_lru_w_i, m_ev_lru_b_i, m_ev_lru_lam, m_ev_w_out, m_od_norm, m_od_w_in, m_od_conv_w, m_od_conv_b, m_od_dt_bias, m_od_a_log, m_od_d_skip, m_od_gnorm, m_od_w_out, m_ffn_norm, m_ffn_w_gate, m_ffn_w_up, m_ffn_w_down, m_final_norm, v_ev_norm, v_ev_w_in, v_ev_gla_w_gate, v_ev_gla_b_gate, v_ev_gla_w_onorm, v_ev_lru_conv_w, v_ev_lru_conv_b, v_ev_lru_w_a, v_ev_lru_b_a, v_ev_lru_w_i, v_ev_lru_b_i, v_ev_lru_lam, v_ev_w_out, v_od_norm, v_od_w_in, v_od_conv_w, v_od_conv_b, v_od_dt_bias, v_od_a_log, v_od_d_skip, v_od_gnorm, v_od_w_out, v_ffn_norm, v_ffn_w_gate, v_ffn_w_up, v_ffn_w_down, v_final_norm):
    args = dict(locals())
    p = {n: a for n, a in args.items() if n != "loss_target"}
    cfg = Cfg(S=x.shape[1], D=x.shape[2], DFF=ffn_w_gate.shape[2] * N_DEV)
    out = train_step(cfg, p, loss_target)
    return (out["loss"], out["grad_x"], *[out["grad_" + w] for w in WEIGHTS], *[out["delta_" + w] for w in WEIGHTS],
            *[out["new_m_" + w] for w in WEIGHTS], *[out["new_v_" + w] for w in WEIGHTS])
```
